```python
import jax
import jax.numpy as jnp
from jax import lax
import numpy as np

D_MODEL = 2048
BATCH = 4
SEQ = 4096
DEPTH = 2

GRID_W = 64
CTX_LEN = 256
EPS = 1e-6
NEG_INF = -1e30

CONV_CH = D_MODEL // 4
CONV_K = 31
NA_HEADS = 16
NA_HD = D_MODEL // 32
NA_DIM = NA_HEADS * NA_HD
NA_KH_MAX = 8
NA_KW = 16
NA_QB = 16
NA_KB = 32
HG_HEADS = 4
HG_DK = D_MODEL // 16
HG_DV = HG_DK
HG_DIM = HG_HEADS * HG_DK
HG_CHUNK = 64
MIX_DIM = CONV_CH + NA_DIM + HG_DIM
OFF_NA = 2 * CONV_CH
OFF_HG = OFF_NA + 3 * NA_DIM
IN_DIM = OFF_HG + 5 * HG_DIM
N_GROUPS = 4
EXP_PER_GROUP = 8
N_EXPERTS = N_GROUPS * EXP_PER_GROUP
TOP_K = 2
D_EXPERT = D_MODEL // 4
MOE_BLOCK = 128

kernel_name = "hybrid_conv_natten_hgrn2_hmoe_dit"


def _rmsnorm(x, g):
    xf = x.astype(jnp.float32)
    y = xf * lax.rsqrt(jnp.mean(xf * xf, axis=-1, keepdims=True) + EPS)
    return (y * g.astype(jnp.float32)).astype(x.dtype)


def _layernorm(x, g, b):
    xf = x.astype(jnp.float32)
    mu = jnp.mean(xf, axis=-1, keepdims=True)
    var = jnp.mean(jnp.square(xf - mu), axis=-1, keepdims=True)
    y = (xf - mu) * lax.rsqrt(var + EPS) * g.astype(jnp.float32) + b.astype(jnp.float32)
    return y.astype(x.dtype)


def _conv_module(u, w_dw, b_dw, ln_g, ln_b):
    a, gate = jnp.split(u, 2, axis=-1)
    h = a * jax.nn.sigmoid(gate)
    h = lax.conv_general_dilated(
        h, w_dw[:, None, :].astype(h.dtype), window_strides=(1,),
        padding=[(CONV_K // 2, CONV_K // 2)],
        dimension_numbers=("NWC", "WIO", "NWC"), feature_group_count=CONV_CH)
    h = h + b_dw.astype(h.dtype)
    return jax.nn.silu(_layernorm(h, ln_g, ln_b))


def _na_col_tables():
    nb = GRID_W // NA_QB
    qcol = np.arange(GRID_W).reshape(nb, NA_QB)
    kstart = np.clip(np.arange(nb) * NA_QB - NA_KW // 2, 0, GRID_W - NA_KB)
    kcol = kstart[:, None] + np.arange(NA_KB)[None, :]
    wstart = np.clip(qcol - NA_KW // 2, 0, GRID_W - NA_KW)
    dcol = kcol[:, None, :] - qcol[:, :, None]
    ok = (kcol[:, None, :] >= wstart[:, :, None]) & (kcol[:, None, :] < wstart[:, :, None] + NA_KW)
    dcol_idx = np.clip(dcol + NA_KW - 1, 0, 2 * NA_KW - 2)
    return kcol, dcol_idx, ok


def _na_latent(q, k, v, kc, vc, rpb):
    B, T, H, hd = q.shape
    rows = T // GRID_W
    kh = min(NA_KH_MAX, rows)
    nb = GRID_W // NA_QB
    kcol, dcol_idx, ok = _na_col_tables()
    scale = hd ** -0.5
    col_bias = jnp.where(ok, rpb.astype(jnp.float32)[:, :, dcol_idx], NEG_INF)
    qg = jnp.moveaxis(q.reshape(B, rows, nb, NA_QB, H, hd), 1, 0)
    kg = k.reshape(B, rows, GRID_W, H, hd)
    vg = v.reshape(B, rows, GRID_W, H, hd)

    def row_block(args):
        r, q_r = args
        sr = jnp.clip(r - kh // 2, 0, rows - kh)
        k_b = lax.dynamic_slice_in_dim(kg, sr, kh, axis=1)[:, :, kcol]
        v_b = lax.dynamic_slice_in_dim(vg, sr, kh, axis=1)[:, :, kcol]
        drow = sr + jnp.arange(kh) - r
        bias = jnp.take(col_bias, drow + NA_KH_MAX - 1, axis=1).transpose(0, 2, 3, 1, 4)
        s_win = jnp.einsum("bjqhd,brjkhd->bhjqrk", q_r, k_b).astype(jnp.float32) * scale + bias[None]
        s_ctx = jnp.einsum("bjqhd,bchd->bhjqc", q_r, kc).astype(jnp.float32) * scale
        s = jnp.concatenate([s_win.reshape(B, H, nb, NA_QB, kh * NA_KB), s_ctx], axis=-1)
        p = jax.nn.softmax(s, axis=-1).astype(v.dtype)
        p_win = p[..., : kh * NA_KB].reshape(B, H, nb, NA_QB, kh, NA_KB)
        p_ctx = p[..., kh * NA_KB:]
        return (jnp.einsum("bhjqrk,brjkhd->bjqhd", p_win, v_b)
                + jnp.einsum("bhjqc,bchd->bjqhd", p_ctx, vc))

    o = lax.map(row_block, (jnp.arange(rows), qg))
    return jnp.moveaxis(o, 0, 1).reshape(B, T, H * hd)


def _ctx_attn(q, k, v):
    B, L, H, hd = q.shape
    s = jnp.einsum("bqhd,bkhd->bhqk", q, k).astype(jnp.float32) * hd ** -0.5
    p = jax.nn.softmax(s, axis=-1).astype(v.dtype)
    return jnp.einsum("bhqk,bkhd->bqhd", p, v).reshape(B, L, H * hd)


def _heads(a):
    B, T, _ = a.shape
    return a.astype(jnp.float32).reshape(B, T, HG_HEADS, -1).transpose(0, 2, 1, 3)


def _hgrn_gates(z, lb):
    logf = jnp.logaddexp(jnp.log(lb), jnp.log1p(-lb) + jax.nn.log_sigmoid(z))
    return logf, -jnp.expm1(logf)


def _hgrn_scan(q, k, v, logf, s0):
    B, H, T, DK = q.shape
    DV = v.shape[-1]
    n = T // HG_CHUNK

    def chunks(a):
        return a.reshape(B, H, n, HG_CHUNK, a.shape[-1]).transpose(2, 0, 1, 3, 4)

    tri = jnp.tril(jnp.ones((HG_CHUNK, HG_CHUNK), dtype=bool))[:, :, None]

    def step(S, inp):
        qc, kc, vc, lf = inp
        b = jnp.cumsum(lf, axis=2)
        diff = b[:, :, :, None, :] - b[:, :, None, :, :]
        decay = jnp.exp(jnp.where(tri, diff, -jnp.inf))
        att = jnp.einsum("bhtk,bhsk,bhtsk->bhts", qc, kc, decay)
        o = (jnp.einsum("bhts,bhsv->bhtv", att, vc)
             + jnp.einsum("bhtk,bhkv->bhtv", qc * jnp.exp(b), S))
        b_last = b[:, :, -1, :]
        S = (jnp.exp(b_last)[..., None] * S
             + jnp.einsum("bhsk,bhsv->bhkv", kc * jnp.exp(b_last[:, :, None, :] - b), vc))
        return S, o

    S, o = lax.scan(step, s0, (chunks(q), chunks(k), chunks(v), chunks(logf)))
    return o.transpose(1, 2, 0, 3, 4).reshape(B, H, T, DV), S


def _hgrn_readout(o, g, norm_g):
    B, H, T, DV = o.shape
    o = o * lax.rsqrt(jnp.mean(o * o, axis=-1, keepdims=True) + EPS)
    o = o.transpose(0, 2, 1, 3).reshape(B, T, H * DV) * norm_g.astype(jnp.float32)
    return (o * jax.nn.silu(g.astype(jnp.float32))).astype(g.dtype)


def _hgrn_branch(ul, uc, lb, norm_g, with_ctx):
    def prep(u):
        q, i, zf, zb, g = jnp.split(u, 5, axis=-1)
        return jax.nn.silu(_heads(q)), _heads(i), _heads(zf), _heads(zb), g

    ql, il, zfl, zbl, gl = prep(ul)
    qc, ic, zfc, zbc, gc = prep(uc)
    lb = lb.astype(jnp.float32)
    lb_f = lb[0].reshape(HG_HEADS, 1, HG_DK)
    lb_b = lb[1].reshape(HG_HEADS, 1, HG_DK)
    s0 = jnp.zeros((ul.shape[0], HG_HEADS, HG_DK, HG_DV), jnp.float32)

    def flip(a):
        return a[:, :, ::-1]

    lf, kf = _hgrn_gates(zfc, lb_f)
    oc_f, sc_f = _hgrn_scan(qc, kf, ic, lf, s0)
    lf, kf = _hgrn_gates(zfl, lb_f)
    ol_f, _ = _hgrn_scan(ql, kf, il, lf, sc_f)
    lbk, kb = _hgrn_gates(flip(zbc), lb_b)
    oc_b, sc_b = _hgrn_scan(flip(qc), kb, flip(ic), lbk, s0)
    lbk, kb = _hgrn_gates(flip(zbl), lb_b)
    ol_b, _ = _hgrn_scan(flip(ql), kb, flip(il), lbk, sc_b)

    out_l = _hgrn_readout(ol_f + flip(ol_b), gl, norm_g)
    out_c = _hgrn_readout(oc_f + flip(oc_b), gc, norm_g) if with_ctx else None
    return out_l, out_c


def _hier_moe(h, w_rg, b_rg, w_re, b_re, w1, w3, w2):
    N, D = h.shape
    pg = jax.nn.softmax((h @ w_rg).astype(jnp.float32) + b_rg.astype(jnp.float32), axis=-1)
    p_grp, grp = lax.top_k(pg, 1)
    le = ((h @ w_re).astype(jnp.float32) + b_re.astype(jnp.float32)).reshape(N, N_GROUPS, EXP_PER_GROUP)
    le = jnp.take_along_axis(le, grp[:, :, None], axis=1)[:, 0]
    p_exp, idx = lax.top_k(jax.nn.softmax(le, axis=-1), TOP_K)
    p_exp = p_exp / jnp.sum(p_exp, axis=-1, keepdims=True)
    eid = (grp * EXP_PER_GROUP + idx).reshape(-1).astype(jnp.int32)
    wt = (p_grp * p_exp).reshape(-1)
    A = N * TOP_K
    tok = jnp.arange(A, dtype=jnp.int32) // TOP_K
    nblk = -(-A // MOE_BLOCK) + N_EXPERTS
    P = nblk * MOE_BLOCK
    cnt = jax.ops.segment_sum(jnp.ones((A,), jnp.int32), eid, num_segments=N_EXPERTS)
    pcnt = (cnt + MOE_BLOCK - 1) // MOE_BLOCK * MOE_BLOCK
    pend = jnp.cumsum(pcnt)
    pstart = pend - pcnt
    start = jnp.cumsum(cnt) - cnt
    order = jnp.argsort(eid)
    se = eid[order]
    dst = pstart[se] + (jnp.arange(A, dtype=jnp.int32) - start[se])
    slot_tok = jnp.full((P,), N, jnp.int32).at[dst].set(tok[order])
    slot_w = jnp.zeros((P,), jnp.float32).at[dst].set(wt[order])
    blk_e = jnp.minimum(jnp.searchsorted(pend, jnp.arange(nblk, dtype=jnp.int32) * MOE_BLOCK, side="right"),
                        N_EXPERTS - 1)
    hpad = jnp.concatenate([h, jnp.zeros((1, D), h.dtype)], axis=0)
    xs = hpad[slot_tok].reshape(nblk, MOE_BLOCK, D)

    def expert_block(args):
        e, xb = args
        return (jax.nn.silu(xb @ w1[e]) * (xb @ w3[e])) @ w2[e]

    ys = lax.map(expert_block, (blk_e, xs)).reshape(P, D)
    ys = ys * slot_w.astype(ys.dtype)[:, None]
    return jax.ops.segment_sum(ys, slot_tok, num_segments=N + 1)[:N]


def setup_inputs(seed: int = 0) -> dict:
    key = jax.random.key(seed)
    ks = jax.random.split(key, 26)
    f32 = jnp.float32
    D = D_MODEL
    inv = D ** -0.5

    def nrm(k, shape, s):
        return jax.random.normal(k, shape, f32) * s

    return {
        "x": nrm(ks[0], (BATCH, SEQ, D), 1.0),
        "c": nrm(ks[1], (BATCH, D), 1.0),
        "ctx": nrm(ks[2], (BATCH, CTX_LEN, D), 1.0),
        "c_ctx": nrm(ks[3], (D,), 1.0),
        "w_ada": nrm(ks[4], (DEPTH, D, 6 * D), 0.5 * inv),
        "b_ada": nrm(ks[5], (DEPTH, 6 * D), 0.02),
        "g_mix": 1.0 + nrm(ks[6], (DEPTH, D), 0.02),
        "g_ffn": 1.0 + nrm(ks[7], (DEPTH, D), 0.02),
        "w_in": nrm(ks[8], (DEPTH, D, IN_DIM), inv),
        "conv_w": nrm(ks[9], (DEPTH, CONV_K, CONV_CH), CONV_K ** -0.5),
        "conv_b": nrm(ks[10], (DEPTH, CONV_CH), 0.02),
        "conv_ln_g": 1.0 + nrm(ks[11], (DEPTH, CONV_CH), 0.02),
        "conv_ln_b": nrm(ks[12], (DEPTH, CONV_CH), 0.02),
        "na_rpb": nrm(ks[13], (DEPTH, NA_HEADS, 2 * NA_KH_MAX - 1, 2 * NA_KW - 1), 0.1),
        "hgrn_lb": nrm(ks[14], (DEPTH, 2, HG_DIM), 0.5),
        "hgrn_norm_g": 1.0 + nrm(ks[15], (DEPTH, HG_DIM), 0.02),
        "w_out": nrm(ks[16], (DEPTH, MIX_DIM, D), MIX_DIM ** -0.5),
        "w_router_group": nrm(ks[17], (DEPTH, D, N_GROUPS), inv),
        "b_router_group": nrm(ks[18], (DEPTH, N_GROUPS), 0.01),
        "w_router_expert": nrm(ks[19], (DEPTH, D, N_EXPERTS), inv),
        "b_router_expert": nrm(ks[20], (DEPTH, N_EXPERTS), 0.01),
        "w_exp_gate": nrm(ks[21], (DEPTH, N_EXPERTS, D, D_EXPERT), inv),
        "w_exp_up": nrm(ks[22], (DEPTH, N_EXPERTS, D, D_EXPERT), inv),
        "w_exp_down": nrm(ks[23], (DEPTH, N_EXPERTS, D_EXPERT, D), D_EXPERT ** -0.5),
        "g_final": 1.0 + nrm(ks[24], (D,), 0.02),
    }


def reference(x, c, ctx, c_ctx, w_ada, b_ada, g_mix, g_ffn, w_in, conv_w, conv_b, conv_ln_g,
              conv_ln_b, na_rpb, hgrn_lb, hgrn_norm_g, w_out, w_router_group, b_router_group,
              w_router_expert, b_router_expert, w_exp_gate, w_exp_up, w_exp_down, g_final):
    B, T, D = x.shape
    L = ctx.shape[1]
    lbs = jnp.cumsum(jax.nn.softmax(hgrn_lb.astype(jnp.float32), axis=0), axis=0)
    lbs = lbs - lbs[:1]
    sc_lat = jax.nn.silu(c)
    sc_ctx = jax.nn.silu(c_ctx)
    xl, xc = x, ctx
    for l in range(DEPTH):
        with_ctx = l < DEPTH - 1
        mod_l = (sc_lat @ w_ada[l] + b_ada[l])[:, None, :]
        mod_c = (sc_ctx @ w_ada[l] + b_ada[l])[None, None, :]
        sh1_l, s1_l, ga1_l, sh2_l, s2_l, ga2_l = jnp.split(mod_l, 6, axis=-1)
        sh1_c, s1_c, ga1_c, sh2_c, s2_c, ga2_c = jnp.split(mod_c, 6, axis=-1)

        hl = _rmsnorm(xl, g_mix[l]) * (1.0 + s1_l) + sh1_l
        hc = _rmsnorm(xc, g_mix[l]) * (1.0 + s1_c) + sh1_c
        ul = hl @ w_in[l]
        uc = hc @ w_in[l]

        conv_l = _conv_module(ul[..., :OFF_NA], conv_w[l], conv_b[l], conv_ln_g[l], conv_ln_b[l])

        qa, ka, va = [a.reshape(B, T, NA_HEADS, NA_HD) for a in jnp.split(ul[..., OFF_NA:OFF_HG], 3, axis=-1)]
        qcx, kcx, vcx = [a.reshape(B, L, NA_HEADS, NA_HD) for a in jnp.split(uc[..., OFF_NA:OFF_HG], 3, axis=-1)]
        na_l = _na_latent(qa, ka, va, kcx, vcx, na_rpb[l])

        hg_l, hg_c = _hgrn_branch(ul[..., OFF_HG:], uc[..., OFF_HG:], lbs[l], hgrn_norm_g[l], with_ctx)

        xl = xl + ga1_l * (jnp.concatenate([conv_l, na_l, hg_l], axis=-1) @ w_out[l])
        if with_ctx:
            conv_c = _conv_module(uc[..., :OFF_NA], conv_w[l], conv_b[l], conv_ln_g[l], conv_ln_b[l])
            na_c = _ctx_attn(qcx, kcx, vcx)
            xc = xc + ga1_c * (jnp.concatenate([conv_c, na_c, hg_c], axis=-1) @ w_out[l])

        hl = (_rmsnorm(xl, g_ffn[l]) * (1.0 + s2_l) + sh2_l).reshape(B * T, D)
        if with_ctx:
            hc = (_rmsnorm(xc, g_ffn[l]) * (1.0 + s2_c) + sh2_c).reshape(B * L, D)
            tokens = jnp.concatenate([hl, hc], axis=0)
        else:
            tokens = hl
        f = _hier_moe(tokens, w_router_group[l], b_router_group[l], w_router_expert[l],
                      b_router_expert[l], w_exp_gate[l], w_exp_up[l], w_exp_down[l])
        xl = xl + ga2_l * f[: B * T].reshape(B, T, D)
        if with_ctx:
            xc = xc + ga2_c * f[B * T:].reshape(B, L, D)
    return _rmsnorm(xl, g_final)
```

```python
import functools

import numpy as np
import jax
import jax.numpy as jnp
from jax import lax
from jax.experimental import pallas as pl
from jax.experimental.pallas import tpu as pltpu

F32 = jnp.float32
BF16 = jnp.bfloat16

EPS = 1e-6
NEG_INF = -1e30

GRID_W = 64
CONV_K = 31
NA_HEADS = 16
NA_HD = 64
NA_KH = 8
NA_KW = 16
HG_HEADS = 4
HG_DK = 128
N_GROUPS = 4
EXP_PER_GROUP = 8
N_EXPERTS = N_GROUPS * EXP_PER_GROUP
TOP_K = 2

LANES = 128
SUBLANES = 8
VMEM_BYTES = 64 * 1024 * 1024

ROW_TILE = 256
IN_TM = 1024
IN_TN = 512
OUT_TM = 512
NA_ROWS = 8
NA_KEY_ROWS = 16
HG_BLOCK = 256
MOE_BM = 256
ROUTER_PAD = LANES
HALO = 16


def _pick_tile(pref, *extents):
    t = pref
    while t > ROW_TILE and any(e % t for e in extents):
        t //= 2
    assert all(e % t == 0 for e in extents)
    return t


def _cparams(sem, vmem_mb):
    return pltpu.CompilerParams(dimension_semantics=sem, vmem_limit_bytes=vmem_mb * 1024 * 1024)


def _dot(a, b):
    return jnp.dot(a, b, preferred_element_type=F32)


def _dot_nt(a, b):
    return lax.dot_general(a, b, (((1,), (1,)), ((), ())), preferred_element_type=F32)


def _dot_tn(a, b):
    return lax.dot_general(a, b, (((0,), (0,)), ((), ())), preferred_element_type=F32)


def _sigmoid(x):
    return 1.0 / (1.0 + jnp.exp(-x))


def _silu(x):
    return x * _sigmoid(x)


def _ada_kernel(c_ref, w_ref, b_ref, o_ref):
    sc = _silu(c_ref[...])
    o_ref[0] = jnp.dot(sc, w_ref[0], precision=lax.Precision.HIGHEST,
                       preferred_element_type=F32) + b_ref[0]


def _ada_mod(cond, w_ada, b_ada):
    depth, d, n = w_ada.shape
    tn = 1024
    return pl.pallas_call(
        _ada_kernel,
        grid=(depth, n // tn),
        in_specs=[
            pl.BlockSpec((SUBLANES, d), lambda l, j: (0, 0)),
            pl.BlockSpec((1, d, tn), lambda l, j: (l, 0, j)),
            pl.BlockSpec((1, 1, tn), lambda l, j: (l, 0, j)),
        ],
        out_specs=pl.BlockSpec((1, SUBLANES, tn), lambda l, j: (l, 0, j)),
        out_shape=jax.ShapeDtypeStruct((depth, SUBLANES, n), F32),
        compiler_params=_cparams(("arbitrary", "arbitrary"), 40),
    )(cond, w_ada, b_ada.reshape(depth, 1, n))


def _rms_mod(x, g, scale, shift):
    y = x * lax.rsqrt(jnp.mean(x * x, axis=-1, keepdims=True) + EPS)
    return (y * g) * (1.0 + scale) + shift


def _norm_in_kernel(x_ref, g_ref, sh_ref, sc_ref, w_ref, o_ref, h_ref, *, tiles_per_batch, nbatch):
    i = pl.program_id(0)

    @pl.when(pl.program_id(1) == 0)
    def _():
        r = jnp.minimum(i // tiles_per_batch, nbatch)
        h = _rms_mod(x_ref[...], g_ref[...], sc_ref[pl.ds(r, 1), :], sh_ref[pl.ds(r, 1), :])
        h_ref[...] = h.astype(BF16)

    o_ref[...] = _dot(h_ref[...], w_ref[...])


def _norm_in(x, g, mod, w_bf16, nbatch, lat_rows):
    n, d = x.shape
    nout = w_bf16.shape[1]
    tm, tn = _pick_tile(IN_TM, lat_rows // nbatch, n), IN_TN
    kern = functools.partial(_norm_in_kernel, tiles_per_batch=lat_rows // nbatch // tm, nbatch=nbatch)
    return pl.pallas_call(
        kern,
        grid=(n // tm, nout // tn),
        in_specs=[
            pl.BlockSpec((tm, d), lambda i, j: (i, 0)),
            pl.BlockSpec((1, d), lambda i, j: (0, 0)),
            pl.BlockSpec((SUBLANES, d), lambda i, j: (0, 0)),
            pl.BlockSpec((SUBLANES, d), lambda i, j: (0, 1)),
            pl.BlockSpec((d, tn), lambda i, j: (0, j)),
        ],
        out_specs=pl.BlockSpec((tm, tn), lambda i, j: (i, j)),
        out_shape=jax.ShapeDtypeStruct((n, nout), F32),
        scratch_shapes=[pltpu.VMEM((tm, d), BF16)],
        compiler_params=_cparams(("arbitrary", "arbitrary"), 48),
    )(x, g.reshape(1, d), mod, mod, w_bf16)


def _conv_kernel(ap_ref, gp_ref, a_ref, gt_ref, an_ref, gn_ref, w_ref, b_ref, lg_ref, lb_ref,
                 o_ref, buf_ref, acc_ref, *, lat_tiles, tiles_per_seq):
    i = pl.program_id(0)
    tc, ch = a_ref.shape
    is_lat = i < lat_tiles
    pos = i % tiles_per_seq
    first = jnp.logical_or(jnp.logical_not(is_lat), pos == 0)
    last = jnp.logical_or(jnp.logical_not(is_lat), pos == tiles_per_seq - 1)

    buf_ref[0:HALO] = jnp.where(first, 0.0, ap_ref[...] * _sigmoid(gp_ref[...]))
    buf_ref[HALO:HALO + tc] = a_ref[...] * _sigmoid(gt_ref[...])
    buf_ref[HALO + tc:2 * HALO + tc] = jnp.where(last, 0.0, an_ref[...] * _sigmoid(gn_ref[...]))

    rows = 64
    base = HALO - CONV_K // 2
    for c in range(ch // LANES):
        cs = slice(c * LANES, (c + 1) * LANES)
        for r in range(tc // rows):
            acc = jnp.zeros((rows, LANES), F32)
            for k in range(CONV_K):
                acc = acc + w_ref[k:k + 1, cs] * buf_ref[base + r * rows + k:base + (r + 1) * rows + k, cs]
            acc_ref[r * rows:(r + 1) * rows, cs] = acc

    h = acc_ref[...] + b_ref[...]
    mu = jnp.mean(h, axis=-1, keepdims=True)
    var = jnp.mean(jnp.square(h - mu), axis=-1, keepdims=True)
    y = (h - mu) * lax.rsqrt(var + EPS) * lg_ref[...] + lb_ref[...]
    o_ref[...] = _silu(y).astype(o_ref.dtype)


def _conv_module(u, w_dw, b_dw, ln_g, ln_b, n_rows, lat_rows, seq):
    ch = w_dw.shape[1]
    tc = ROW_TILE
    per = tc // HALO
    nh = u.shape[0] // HALO
    kern = functools.partial(_conv_kernel, lat_tiles=lat_rows // tc, tiles_per_seq=seq // tc)
    prev_map = lambda c: (lambda i: (jnp.maximum(i * per - 1, 0), c))
    next_map = lambda c: (lambda i: (jnp.minimum((i + 1) * per, nh - 1), c))
    vec = lambda a: a.reshape(1, ch)
    return pl.pallas_call(
        kern,
        grid=(n_rows // tc,),
        in_specs=[
            pl.BlockSpec((HALO, ch), prev_map(0)),
            pl.BlockSpec((HALO, ch), prev_map(1)),
            pl.BlockSpec((tc, ch), lambda i: (i, 0)),
            pl.BlockSpec((tc, ch), lambda i: (i, 1)),
            pl.BlockSpec((HALO, ch), next_map(0)),
            pl.BlockSpec((HALO, ch), next_map(1)),
            pl.BlockSpec((CONV_K, ch), lambda i: (0, 0)),
            pl.BlockSpec((1, ch), lambda i: (0, 0)),
            pl.BlockSpec((1, ch), lambda i: (0, 0)),
            pl.BlockSpec((1, ch), lambda i: (0, 0)),
        ],
        out_specs=pl.BlockSpec((tc, ch), lambda i: (i, 0)),
        out_shape=jax.ShapeDtypeStruct((n_rows, ch), BF16),
        scratch_shapes=[pltpu.VMEM((tc + 2 * HALO, ch), F32), pltpu.VMEM((tc, ch), F32)],
        compiler_params=_cparams(("arbitrary",), 16),
    )(u, u, u, u, u, u, w_dw, vec(b_dw), vec(ln_g), vec(ln_b))


def _na_bias_tables(rows):
    groups = rows // NA_ROWS
    reps = [0, min(1, groups - 1), groups - 1]
    i = np.arange(NA_ROWS)[:, None, None, None]
    c = np.arange(GRID_W)[None, :, None, None]
    kr = np.arange(NA_KEY_ROWS)[None, None, :, None]
    kc = np.arange(GRID_W)[None, None, None, :]
    drow_l, dcol_l, ok_l = [], [], []
    for g in reps:
        r = NA_ROWS * g + i
        start = np.clip(NA_ROWS * g - NA_KH // 2, 0, rows - NA_KEY_ROWS)
        rk = start + kr
        sr = np.clip(r - NA_KH // 2, 0, rows - NA_KH)
        row_ok = (rk >= sr) & (rk < sr + NA_KH)
        ws = np.clip(c - NA_KW // 2, 0, GRID_W - NA_KW)
        col_ok = (kc >= ws) & (kc < ws + NA_KW)
        ok = np.broadcast_to(row_ok & col_ok, (NA_ROWS, GRID_W, NA_KEY_ROWS, GRID_W))
        drow = np.broadcast_to(np.clip(rk - r + NA_KH - 1, 0, 2 * NA_KH - 2), ok.shape)
        dcol = np.broadcast_to(np.clip(kc - c + NA_KW - 1, 0, 2 * NA_KW - 2), ok.shape)
        nq, nk = NA_ROWS * GRID_W, NA_KEY_ROWS * GRID_W
        drow_l.append(drow.reshape(nq, nk))
        dcol_l.append(dcol.reshape(nq, nk))
        ok_l.append(ok.reshape(nq, nk))
    return np.stack(drow_l), np.stack(dcol_l), np.stack(ok_l)


def _na_bias(rpb, rows):
    drow, dcol, ok = _na_bias_tables(rows)
    b = rpb.astype(F32)[:, drow, dcol]
    b = jnp.where(ok[None], b, NEG_INF)
    return jnp.transpose(b, (1, 0, 2, 3))


def _softmax_pv(s_parts, v_parts):
    m = functools.reduce(jnp.maximum, [jnp.max(s, axis=-1, keepdims=True) for s in s_parts])
    acc, l = None, None
    for s, v in zip(s_parts, v_parts):
        p = jnp.exp(s - m)
        ps = jnp.sum(p, axis=-1, keepdims=True)
        pv = _dot(p.astype(BF16), v)
        l = ps if l is None else l + ps
        acc = pv if acc is None else acc + pv
    return acc / l


def _na_kernel(q_ref, k0, k1, k2, k3, v0, v1, v2, v3, kc_ref, vc_ref, bias_ref, o_ref):
    q2 = q_ref[...] * (NA_HD ** -0.5)
    lane = lax.broadcasted_iota(jnp.int32, q2.shape, 1)
    ks = [k[...].astype(BF16) for k in (k0, k1, k2, k3)]
    vs = [v[...].astype(BF16) for v in (v0, v1, v2, v3)]
    kc = kc_ref[...].astype(BF16)
    vc = vc_ref[...].astype(BF16)
    kw = ks[0].shape[0]
    outs = []
    for a in range(2):
        sel = (lane < NA_HD) if a == 0 else (lane >= NA_HD)
        qa = jnp.where(sel, q2, 0.0).astype(BF16)
        s_parts = [_dot_nt(qa, k) + bias_ref[0, a, :, m * kw:(m + 1) * kw] for m, k in enumerate(ks)]
        s_parts.append(_dot_nt(qa, kc))
        outs.append(_softmax_pv(s_parts, vs + [vc]))
    o_ref[...] = jnp.where(lane < NA_HD, outs[0], outs[1]).astype(o_ref.dtype)


def _na_latent(u, bias, nbatch, seq, lat_rows, off_na):
    rows = seq // GRID_W
    groups = rows // NA_ROWS
    nq = NA_ROWS * GRID_W
    kblk = ROW_TILE
    nkb = NA_KEY_ROWS * GRID_W // kblk
    assert nkb == 4
    qcol = off_na // LANES
    heads2 = NA_HEADS * NA_HD // LANES
    kcol, vcol = qcol + heads2, qcol + 2 * heads2
    kb_per_batch = seq // kblk
    kb_per_grow = GRID_W * NA_ROWS // kblk
    lat_kb = lat_rows // kblk

    def kmap(col, m):
        def f(h, g, b):
            st = jnp.clip(g * kb_per_grow - (NA_KH // 2) * GRID_W // kblk, 0, kb_per_batch - nkb)
            return (b * kb_per_batch + st + m, col + h)
        return f

    def bmap(h, g, b):
        cls = jnp.where(g == 0, 0, jnp.where(g == groups - 1, 2, 1))
        return (cls, h, 0, 0)

    in_specs = [pl.BlockSpec((nq, LANES), lambda h, g, b: (b * groups + g, qcol + h))]
    in_specs += [pl.BlockSpec((kblk, LANES), kmap(kcol, m)) for m in range(nkb)]
    in_specs += [pl.BlockSpec((kblk, LANES), kmap(vcol, m)) for m in range(nkb)]
    in_specs += [pl.BlockSpec((ROW_TILE, LANES), lambda h, g, b: (lat_kb + b, kcol + h)),
                 pl.BlockSpec((ROW_TILE, LANES), lambda h, g, b: (lat_kb + b, vcol + h)),
                 pl.BlockSpec((1, 2, nq, NA_KEY_ROWS * GRID_W), bmap)]
    return pl.pallas_call(
        _na_kernel,
        grid=(heads2, groups, nbatch),
        in_specs=in_specs,
        out_specs=pl.BlockSpec((nq, LANES), lambda h, g, b: (b * groups + g, h)),
        out_shape=jax.ShapeDtypeStruct((lat_rows, NA_HEADS * NA_HD), BF16),
        compiler_params=_cparams(("arbitrary", "arbitrary", "arbitrary"), 40),
    )(*([u] * 11), bias)


def _ctx_attn_kernel(q_ref, k_ref, v_ref, o_ref):
    q2 = q_ref[...] * (NA_HD ** -0.5)
    lane = lax.broadcasted_iota(jnp.int32, q2.shape, 1)
    k = k_ref[...].astype(BF16)
    v = v_ref[...].astype(BF16)
    outs = []
    for a in range(2):
        sel = (lane < NA_HD) if a == 0 else (lane >= NA_HD)
        qa = jnp.where(sel, q2, 0.0).astype(BF16)
        outs.append(_softmax_pv([_dot_nt(qa, k)], [v]))
    o_ref[...] = jnp.where(lane < NA_HD, outs[0], outs[1]).astype(o_ref.dtype)


def _ctx_attn(u, nbatch, ctx_len, lat_rows, off_na):
    assert ctx_len == ROW_TILE
    qcol = off_na // LANES
    heads2 = NA_HEADS * NA_HD // LANES
    base = lat_rows // ROW_TILE
    spec = lambda col: pl.BlockSpec((ROW_TILE, LANES), lambda b, h: (base + b, col + h))
    return pl.pallas_call(
        _ctx_attn_kernel,
        grid=(nbatch, heads2),
        in_specs=[spec(qcol), spec(qcol + heads2), spec(qcol + 2 * heads2)],
        out_specs=pl.BlockSpec((ROW_TILE, LANES), lambda b, h: (b, h)),
        out_shape=jax.ShapeDtypeStruct((nbatch * ctx_len, NA_HEADS * NA_HD), BF16),
        compiler_params=_cparams(("arbitrary", "arbitrary"), 16),
    )(u, u, u)


def _hg_level_map(rev):
    t = np.arange(HG_BLOCK)[:, None]
    s = np.arange(HG_BLOCK)[None, :]
    x = t ^ s
    lvl = np.where(x > 0, np.frexp(np.maximum(x, 1))[1] - 1, -1)
    causal = (s < t) if not rev else (s > t)
    nlev = int(np.log2(HG_BLOCK))
    out = np.where(causal, lvl, -1)
    out = np.where(t == s, nlev, out)
    return out.astype(np.int32)


def _hg_tri(rev):
    t = np.arange(HG_BLOCK)[:, None]
    s = np.arange(HG_BLOCK)[None, :]
    return ((s <= t) if not rev else (s >= t)).astype(np.float32)


def _hg_anchor(b3, m, rev):
    nv = b3.shape[0]
    if m >= SUBLANES:
        w = m // SUBLANES
        b4 = b3.reshape(nv // (2 * w), 2 * w, SUBLANES, LANES)
        a = b4[:, w:w + 1, 0:1, :] if rev else b4[:, w - 1:w, SUBLANES - 1:SUBLANES, :]
        return jnp.broadcast_to(a, b4.shape).reshape(b3.shape)
    sub = lax.broadcasted_iota(jnp.int32, b3.shape, 1)
    out = None
    for g in range(SUBLANES // (2 * m)):
        idx = g * 2 * m + (m if rev else m - 1)
        a = jnp.broadcast_to(b3[:, idx:idx + 1, :], b3.shape)
        out = a if out is None else jnp.where(sub >= g * 2 * m, a, out)
    return out


def _hg_block(q, v, z, alog, clog, oml, tri, lv, st, rev):
    n = q.shape[0]
    nlev = int(np.log2(n))
    q = _silu(q)
    lsig = jnp.minimum(z, 0.0) - jnp.log1p(jnp.exp(-jnp.abs(z)))
    cc = clog + lsig
    logf = jnp.maximum(alog, cc) + jnp.log1p(jnp.exp(-jnp.abs(alog - cc)))
    kk = oml * _sigmoid(-z)

    hi = logf.astype(BF16)
    r1 = logf - hi.astype(F32)
    mid = r1.astype(BF16)
    lo = (r1 - mid.astype(F32)).astype(BF16)
    b = _dot(tri, hi) + _dot(tri, mid) + _dot(tri, lo)

    b3 = b.reshape(n // SUBLANES, SUBLANES, LANES)
    row = lax.broadcasted_iota(jnp.int32, q.shape, 0)
    att = jnp.zeros((n, n), F32)
    for lev in range(nlev):
        m = 1 << lev
        anc = _hg_anchor(b3, m, rev).reshape(q.shape)
        e = jnp.exp(-jnp.abs(b - anc))
        later = (row & m) != 0
        is_q = jnp.logical_not(later) if rev else later
        qm = jnp.where(is_q, q * e, 0.0).astype(BF16)
        km = jnp.where(is_q, 0.0, kk * e).astype(BF16)
        att = jnp.where(lv == lev, _dot_nt(qm, km), att)
    att = jnp.where(lv == nlev, _dot_nt(q.astype(BF16), kk.astype(BF16)), att)

    vb = v.astype(BF16)
    b_last = b[0:1, :] if rev else b[n - 1:n, :]
    qh = (q * jnp.exp(b)).astype(BF16)
    o = _dot(att.astype(BF16), vb) + _dot_nt(qh, st.astype(BF16))
    kh = (kk * jnp.exp(b_last - b)).astype(BF16)
    st_new = st * jnp.exp(b_last) + _dot_tn(vb, kh)
    return o, st_new


def _hg_fwd_kernel(q_ref, v_ref, z_ref, al_ref, cl_ref, om_ref, tri_ref, lv_ref, o_ref, st_ref):
    @pl.when(pl.program_id(1) == 0)
    def _():
        st_ref[...] = jnp.zeros_like(st_ref)

    tri = tri_ref[...]
    lv = lv_ref[...]
    for h in range(HG_HEADS):
        hs = slice(h * HG_DK, (h + 1) * HG_DK)
        o, st = _hg_block(q_ref[:, hs], v_ref[:, hs], z_ref[:, hs], al_ref[:, hs], cl_ref[:, hs],
                          om_ref[:, hs], tri, lv, st_ref[h], False)
        o_ref[:, hs] = o
        st_ref[h] = st


def _hg_bwd_kernel(q_ref, v_ref, z_ref, g_ref, of_ref, al_ref, cl_ref, om_ref, ng_ref, tri_ref, lv_ref,
                   o_ref, st_ref):
    @pl.when(pl.program_id(1) == 0)
    def _():
        st_ref[...] = jnp.zeros_like(st_ref)

    tri = tri_ref[...]
    lv = lv_ref[...]
    for h in range(HG_HEADS):
        hs = slice(h * HG_DK, (h + 1) * HG_DK)
        o, st = _hg_block(q_ref[:, hs], v_ref[:, hs], z_ref[:, hs], al_ref[:, hs], cl_ref[:, hs],
                          om_ref[:, hs], tri, lv, st_ref[h], True)
        st_ref[h] = st
        t = of_ref[:, hs] + o
        y = t * lax.rsqrt(jnp.mean(t * t, axis=-1, keepdims=True) + EPS)
        o_ref[:, hs] = (y * ng_ref[:, hs] * _silu(g_ref[:, hs])).astype(o_ref.dtype)


def _hgrn(u, lb, norm_g, nbatch, seq, ctx_len, lat_rows, off_hg):
    assert ctx_len == HG_BLOCK
    n = u.shape[0]
    hd = HG_HEADS * HG_DK
    col = off_hg // hd
    per = seq // HG_BLOCK
    lat_blocks = lat_rows // HG_BLOCK
    lbf = lb.astype(F32)
    alog, clog, oml = jnp.log(lbf), jnp.log1p(-lbf), 1.0 - lbf

    def fmap(c):
        return lambda b, j: (jnp.where(j == 0, lat_blocks + b, b * per + j - 1), c)

    def bmap(c):
        return lambda b, j: (jnp.where(j == 0, lat_blocks + b, b * per + per - j), c)

    const = lambda shape: pl.BlockSpec(shape, lambda b, j: (0, 0))
    grid = (nbatch, per + 1)
    vec = lambda a: a.reshape(1, hd)
    o_f = pl.pallas_call(
        _hg_fwd_kernel,
        grid=grid,
        in_specs=[pl.BlockSpec((HG_BLOCK, hd), fmap(col)), pl.BlockSpec((HG_BLOCK, hd), fmap(col + 1)),
                  pl.BlockSpec((HG_BLOCK, hd), fmap(col + 2)),
                  const((1, hd)), const((1, hd)), const((1, hd)),
                  const((HG_BLOCK, HG_BLOCK)), const((HG_BLOCK, HG_BLOCK))],
        out_specs=pl.BlockSpec((HG_BLOCK, hd), fmap(0)),
        out_shape=jax.ShapeDtypeStruct((n, hd), F32),
        scratch_shapes=[pltpu.VMEM((HG_HEADS, HG_DK, HG_DK), F32)],
        compiler_params=_cparams(("arbitrary", "arbitrary"), 32),
    )(u, u, u, vec(alog[0]), vec(clog[0]), vec(oml[0]),
      jnp.asarray(_hg_tri(False), BF16), jnp.asarray(_hg_level_map(False)))
    return pl.pallas_call(
        _hg_bwd_kernel,
        grid=grid,
        in_specs=[pl.BlockSpec((HG_BLOCK, hd), bmap(col)), pl.BlockSpec((HG_BLOCK, hd), bmap(col + 1)),
                  pl.BlockSpec((HG_BLOCK, hd), bmap(col + 3)), pl.BlockSpec((HG_BLOCK, hd), bmap(col + 4)),
                  pl.BlockSpec((HG_BLOCK, hd), bmap(0)),
                  const((1, hd)), const((1, hd)), const((1, hd)), const((1, hd)),
                  const((HG_BLOCK, HG_BLOCK)), const((HG_BLOCK, HG_BLOCK))],
        out_specs=pl.BlockSpec((HG_BLOCK, hd), bmap(0)),
        out_shape=jax.ShapeDtypeStruct((n, hd), BF16),
        scratch_shapes=[pltpu.VMEM((HG_HEADS, HG_DK, HG_DK), F32)],
        compiler_params=_cparams(("arbitrary", "arbitrary"), 32),
    )(u, u, u, u, o_f, vec(alog[1]), vec(clog[1]), vec(oml[1]), vec(norm_g.astype(F32)),
      jnp.asarray(_hg_tri(True), BF16), jnp.asarray(_hg_level_map(True)))


def _out_kernel(x_ref, cv_ref, na_ref, hg_ref, w_ref, ga_ref, g2_ref, sh2_ref, s2_ref, wr_ref, br_ref,
                xo_ref, h_ref, lg_ref, *, tiles_per_batch, nbatch):
    r = jnp.minimum(pl.program_id(0) // tiles_per_batch, nbatch)
    c0 = cv_ref.shape[1]
    c1 = c0 + na_ref.shape[1]
    mix = (_dot(cv_ref[...], w_ref[0:c0, :]) + _dot(na_ref[...], w_ref[c0:c1, :])
           + _dot(hg_ref[...], w_ref[c1:, :]))
    xn = x_ref[...] + ga_ref[pl.ds(r, 1), :] * mix
    xo_ref[...] = xn
    h = _rms_mod(xn, g2_ref[...], s2_ref[pl.ds(r, 1), :], sh2_ref[pl.ds(r, 1), :])
    h_ref[...] = h.astype(h_ref.dtype)
    lg_ref[...] = jnp.dot(h, wr_ref[...], precision=lax.Precision.HIGHEST,
                          preferred_element_type=F32) + br_ref[...]


def _out_proj(x, conv, na, hg, w_bf16, mod, g_ffn, w_router, b_router, n_rows, nbatch, lat_rows):
    d = x.shape[1]
    tm = _pick_tile(OUT_TM, lat_rows // nbatch, n_rows)
    kern = functools.partial(_out_kernel, tiles_per_batch=lat_rows // nbatch // tm, nbatch=nbatch)
    row = lambda w: pl.BlockSpec((tm, w), lambda i: (i, 0))
    const = lambda shape, c=0: pl.BlockSpec(shape, lambda i: (0, c))
    return pl.pallas_call(
        kern,
        grid=(n_rows // tm,),
        in_specs=[row(d), row(conv.shape[1]), row(na.shape[1]), row(hg.shape[1]),
                  const((d, d)),
                  const((SUBLANES, d), 2),
                  const((1, d)),
                  const((SUBLANES, d), 3),
                  const((SUBLANES, d), 4),
                  const((d, ROUTER_PAD)), const((1, ROUTER_PAD))],
        out_specs=[row(d), row(d), row(ROUTER_PAD)],
        out_shape=[jax.ShapeDtypeStruct((n_rows, d), F32), jax.ShapeDtypeStruct((n_rows, d), BF16),
                   jax.ShapeDtypeStruct((n_rows, ROUTER_PAD), F32)],
        compiler_params=_cparams(("arbitrary",), 56),
    )(x, conv, na, hg, w_bf16, mod, g_ffn.reshape(1, d), mod, mod, w_router, b_router)


def _moe_kernel(be_ref, nu_ref, xs_ref, w1_ref, w3_ref, w2_ref, sw_ref, o_ref, w1b, w3b, w2b):
    i = pl.program_id(0)
    e = be_ref[i]
    prev = be_ref[jnp.maximum(i - 1, 0)]

    @pl.when(jnp.logical_or(i == 0, e != prev))
    def _():
        w1b[...] = w1_ref[0].astype(BF16)
        w3b[...] = w3_ref[0].astype(BF16)
        w2b[...] = w2_ref[0].astype(BF16)

    @pl.when(i < nu_ref[0])
    def _():
        x = xs_ref[...]
        a = (_silu(_dot(x, w1b[...])) * _dot(x, w3b[...])).astype(BF16)
        o_ref[...] = _dot(a, w2b[...]) * sw_ref[...]

    @pl.when(i >= nu_ref[0])
    def _():
        o_ref[...] = jnp.zeros_like(o_ref)


def _moe_experts(xs, slot_w, blk_e, nused, w1, w3, w2):
    p, d = xs.shape
    de = w1.shape[2]
    bm = MOE_BM
    grid_spec = pltpu.PrefetchScalarGridSpec(
        num_scalar_prefetch=2,
        grid=(p // bm,),
        in_specs=[pl.BlockSpec((bm, d), lambda i, be, nu: (i, 0)),
                  pl.BlockSpec((1, d, de), lambda i, be, nu: (be[i], 0, 0)),
                  pl.BlockSpec((1, d, de), lambda i, be, nu: (be[i], 0, 0)),
                  pl.BlockSpec((1, de, d), lambda i, be, nu: (be[i], 0, 0)),
                  pl.BlockSpec((bm, 1), lambda i, be, nu: (i, 0))],
        out_specs=pl.BlockSpec((bm, d), lambda i, be, nu: (i, 0)),
        scratch_shapes=[pltpu.VMEM((d, de), BF16), pltpu.VMEM((d, de), BF16), pltpu.VMEM((de, d), BF16)],
    )
    return pl.pallas_call(
        _moe_kernel,
        grid_spec=grid_spec,
        out_shape=jax.ShapeDtypeStruct((p, d), F32),
        compiler_params=_cparams(("arbitrary",), 48),
    )(blk_e, nused, xs, w1, w3, w2, slot_w.reshape(p, 1))


def _route(logits, n):
    pg = jax.nn.softmax(logits[:, :N_GROUPS], axis=-1)
    p_grp, grp = lax.top_k(pg, 1)
    le = logits[:, N_GROUPS:N_GROUPS + N_EXPERTS].reshape(n, N_GROUPS, EXP_PER_GROUP)
    le = jnp.take_along_axis(le, grp[:, :, None], axis=1)[:, 0]
    p_exp, idx = lax.top_k(jax.nn.softmax(le, axis=-1), TOP_K)
    p_exp = p_exp / jnp.sum(p_exp, axis=-1, keepdims=True)
    eid = (grp * EXP_PER_GROUP + idx).reshape(-1).astype(jnp.int32)
    wt = (p_grp * p_exp).reshape(-1)
    a = n * TOP_K
    bm = MOE_BM
    nblk = -(-a // bm) + N_EXPERTS
    p = nblk * bm
    cnt = jnp.sum((eid[:, None] == jnp.arange(N_EXPERTS, dtype=jnp.int32)[None, :]).astype(jnp.int32), axis=0)
    pcnt = (cnt + bm - 1) // bm * bm
    pend = jnp.cumsum(pcnt)
    pstart = pend - pcnt
    start = jnp.cumsum(cnt) - cnt
    order = jnp.argsort(eid)
    se = eid[order]
    dst_sorted = pstart[se] + (jnp.arange(a, dtype=jnp.int32) - start[se])
    tok = jnp.arange(a, dtype=jnp.int32) // TOP_K
    slot_tok = jnp.zeros((p,), jnp.int32).at[dst_sorted].set(tok[order])
    slot_w = jnp.zeros((p,), F32).at[dst_sorted].set(wt[order])
    pos = jnp.zeros((a,), jnp.int32).at[order].set(dst_sorted).reshape(n, TOP_K)
    blk_e = jnp.minimum(jnp.searchsorted(pend, jnp.arange(nblk, dtype=jnp.int32) * bm, side="right"),
                        N_EXPERTS - 1).astype(jnp.int32)
    nused = (pend[-1:] // bm).astype(jnp.int32)
    return slot_tok, slot_w, pos, blk_e, nused


def _combine_kernel(x_ref, y0_ref, y1_ref, ga_ref, gf_ref, o_ref, *, tiles_per_batch, nbatch, final):
    r = jnp.minimum(pl.program_id(0) // tiles_per_batch, nbatch)
    xn = x_ref[...] + ga_ref[pl.ds(r, 1), :] * (y0_ref[...] + y1_ref[...])
    if final:
        xn = xn * lax.rsqrt(jnp.mean(xn * xn, axis=-1, keepdims=True) + EPS) * gf_ref[...]
    o_ref[...] = xn


def _combine(x, y0, y1, mod, g_final, n_rows, nbatch, lat_rows, final):
    d = x.shape[1]
    tm = _pick_tile(OUT_TM, lat_rows // nbatch, n_rows)
    kern = functools.partial(_combine_kernel, tiles_per_batch=lat_rows // nbatch // tm, nbatch=nbatch,
                             final=final)
    row = pl.BlockSpec((tm, d), lambda i: (i, 0))
    return pl.pallas_call(
        kern,
        grid=(n_rows // tm,),
        in_specs=[row, row, row, pl.BlockSpec((SUBLANES, d), lambda i: (0, 5)),
                  pl.BlockSpec((1, d), lambda i: (0, 0))],
        out_specs=row,
        out_shape=jax.ShapeDtypeStruct((n_rows, d), F32),
        compiler_params=_cparams(("arbitrary",), 40),
    )(x, y0, y1, mod, g_final.reshape(1, d))


def kernel(x, c, ctx, c_ctx, w_ada, b_ada, g_mix, g_ffn, w_in, conv_w, conv_b, conv_ln_g, conv_ln_b,
           na_rpb, hgrn_lb, hgrn_norm_g, w_out, w_router_group, b_router_group, w_router_expert,
           b_router_expert, w_exp_gate, w_exp_up, w_exp_down, g_final):
    nb, seq, d = x.shape
    ctx_len = ctx.shape[1]
    depth = w_ada.shape[0]
    lat_rows = nb * seq
    n_all = lat_rows + nb * ctx_len
    conv_ch = conv_w.shape[2]
    off_na = 2 * conv_ch
    off_hg = off_na + 3 * NA_HEADS * NA_HD
    rows = seq // GRID_W
    assert nb < SUBLANES and rows % NA_ROWS == 0 and rows >= NA_KEY_ROWS

    lbs = jnp.cumsum(jax.nn.softmax(hgrn_lb.astype(F32), axis=0), axis=0)
    lbs = lbs - lbs[:1]

    cond = jnp.concatenate([c, c_ctx[None, :], jnp.zeros((SUBLANES - nb - 1, d), F32)], axis=0)
    mod = _ada_mod(cond, w_ada, b_ada)

    xs = jnp.concatenate([x.reshape(lat_rows, d), ctx.reshape(nb * ctx_len, d)], axis=0)
    for l in range(depth):
        with_ctx = l < depth - 1
        n_act = n_all if with_ctx else lat_rows
        u = _norm_in(xs, g_mix[l], mod[l], w_in[l].astype(BF16), nb, lat_rows)

        conv = _conv_module(u, conv_w[l], conv_b[l], conv_ln_g[l], conv_ln_b[l], n_act, lat_rows, seq)
        na = _na_latent(u, _na_bias(na_rpb[l], rows), nb, seq, lat_rows, off_na)
        if with_ctx:
            na = jnp.concatenate([na, _ctx_attn(u, nb, ctx_len, lat_rows, off_na)], axis=0)
        hg = _hgrn(u, lbs[l], hgrn_norm_g[l], nb, seq, ctx_len, lat_rows, off_hg)

        w_router = jnp.concatenate(
            [w_router_group[l], w_router_expert[l],
             jnp.zeros((d, ROUTER_PAD - N_GROUPS - N_EXPERTS), F32)], axis=1)
        b_router = jnp.concatenate(
            [b_router_group[l], b_router_expert[l],
             jnp.zeros((ROUTER_PAD - N_GROUPS - N_EXPERTS,), F32)]).reshape(1, ROUTER_PAD)
        x_mid, h, logits = _out_proj(xs, conv, na, hg, w_out[l].astype(BF16), mod[l], g_ffn[l],
                                     w_router, b_router, n_act, nb, lat_rows)

        slot_tok, slot_w, pos, blk_e, nused = _route(logits, n_act)
        ys = _moe_experts(jnp.take(h, slot_tok, axis=0), slot_w, blk_e, nused,
                          w_exp_gate[l], w_exp_up[l], w_exp_down[l])
        y0 = jnp.take(ys, pos[:, 0], axis=0)
        y1 = jnp.take(ys, pos[:, 1], axis=0)
        xs = _combine(x_mid, y0, y1, mod[l], g_final, n_act, nb, lat_rows, final=not with_ctx)
    return xs.reshape(nb, seq, d)
```

```python
import functools

import numpy as np
import jax
import jax.numpy as jnp
from jax import lax
from jax.experimental import pallas as pl
from jax.experimental.pallas import tpu as pltpu

F32 = jnp.float32
BF16 = jnp.bfloat16

EPS = 1e-6
NEG_INF = -1e30

GRID_W = 64
CONV_K = 31
NA_HEADS = 16
NA_HD = 64
NA_KH = 8
NA_KW = 16
HG_HEADS = 4
HG_DK = 128
N_GROUPS = 4
EXP_PER_GROUP = 8
N_EXPERTS = N_GROUPS * EXP_PER_GROUP
TOP_K = 2

LANES = 128
SUBLANES = 8
VMEM_BYTES = 64 * 1024 * 1024

ROW_TILE = 256
IN_TM = 1024
IN_TN = 512
OUT_TM = 512
NA_ROWS = 8
NA_KEY_ROWS = 16
HG_BLOCK = 256
MOE_BM = 256
ROUTER_PAD = LANES
HALO = 16


def _pick_tile(pref, *extents):
    t = pref
    while t > ROW_TILE and any(e % t for e in extents):
        t //= 2
    assert all(e % t == 0 for e in extents)
    return t


def _cparams(sem, vmem_mb):
    return pltpu.CompilerParams(dimension_semantics=sem, vmem_limit_bytes=vmem_mb * 1024 * 1024)


def _dot(a, b):
    return jnp.dot(a, b, preferred_element_type=F32)


def _dot_nt(a, b):
    return lax.dot_general(a, b, (((1,), (1,)), ((), ())), preferred_element_type=F32)


def _dot_tn(a, b):
    return lax.dot_general(a, b, (((0,), (0,)), ((), ())), preferred_element_type=F32)


def _sigmoid(x):
    return 1.0 / (1.0 + jnp.exp(-x))


def _silu(x):
    return x * _sigmoid(x)


def _ada_kernel(c_ref, w_ref, b_ref, o_ref):
    sc = _silu(c_ref[...])
    o_ref[0] = jnp.dot(sc, w_ref[0], precision=lax.Precision.HIGHEST,
                       preferred_element_type=F32) + b_ref[0]


def _ada_mod(cond, w_ada, b_ada):
    depth, d, n = w_ada.shape
    tn = 1024
    return pl.pallas_call(
        _ada_kernel,
        grid=(depth, n // tn),
        in_specs=[
            pl.BlockSpec((SUBLANES, d), lambda l, j: (0, 0)),
            pl.BlockSpec((1, d, tn), lambda l, j: (l, 0, j)),
            pl.BlockSpec((1, 1, tn), lambda l, j: (l, 0, j)),
        ],
        out_specs=pl.BlockSpec((1, SUBLANES, tn), lambda l, j: (l, 0, j)),
        out_shape=jax.ShapeDtypeStruct((depth, SUBLANES, n), F32),
        compiler_params=_cparams(("arbitrary", "arbitrary"), 40),
    )(cond, w_ada, b_ada.reshape(depth, 1, n))


def _rms_mod(x, g, scale, shift):
    y = x * lax.rsqrt(jnp.mean(x * x, axis=-1, keepdims=True) + EPS)
    return (y * g) * (1.0 + scale) + shift


def _norm_in_kernel(x_ref, g_ref, sh_ref, sc_ref, w_ref, o_ref, h_ref, *, tiles_per_batch, nbatch):
    i = pl.program_id(0)

    @pl.when(pl.program_id(1) == 0)
    def _():
        r = jnp.minimum(i // tiles_per_batch, nbatch)
        h = _rms_mod(x_ref[...], g_ref[...], sc_ref[pl.ds(r, 1), :], sh_ref[pl.ds(r, 1), :])
        h_ref[...] = h.astype(BF16)

    o_ref[...] = _dot(h_ref[...], w_ref[...])


def _norm_in(x, g, mod, w_bf16, nbatch, lat_rows):
    n, d = x.shape
    nout = w_bf16.shape[1]
    tm, tn = _pick_tile(IN_TM, lat_rows // nbatch, n), IN_TN
    kern = functools.partial(_norm_in_kernel, tiles_per_batch=lat_rows // nbatch // tm, nbatch=nbatch)
    return pl.pallas_call(
        kern,
        grid=(n // tm, nout // tn),
        in_specs=[
            pl.BlockSpec((tm, d), lambda i, j: (i, 0)),
            pl.BlockSpec((1, d), lambda i, j: (0, 0)),
            pl.BlockSpec((SUBLANES, d), lambda i, j: (0, 0)),
            pl.BlockSpec((SUBLANES, d), lambda i, j: (0, 1)),
            pl.BlockSpec((d, tn), lambda i, j: (0, j)),
        ],
        out_specs=pl.BlockSpec((tm, tn), lambda i, j: (i, j)),
        out_shape=jax.ShapeDtypeStruct((n, nout), F32),
        scratch_shapes=[pltpu.VMEM((tm, d), BF16)],
        compiler_params=_cparams(("arbitrary", "arbitrary"), 48),
    )(x, g.reshape(1, d), mod, mod, w_bf16)


def _conv_kernel(ap_ref, gp_ref, a_ref, gt_ref, an_ref, gn_ref, w_ref, b_ref, lg_ref, lb_ref,
                 o_ref, buf_ref, acc_ref, *, lat_tiles, tiles_per_seq):
    i = pl.program_id(0)
    tc, ch = a_ref.shape
    is_lat = i < lat_tiles
    pos = i % tiles_per_seq
    first = jnp.logical_or(jnp.logical_not(is_lat), pos == 0)
    last = jnp.logical_or(jnp.logical_not(is_lat), pos == tiles_per_seq - 1)

    buf_ref[0:HALO] = jnp.where(first, 0.0, ap_ref[...] * _sigmoid(gp_ref[...]))
    buf_ref[HALO:HALO + tc] = a_ref[...] * _sigmoid(gt_ref[...])
    buf_ref[HALO + tc:2 * HALO + tc] = jnp.where(last, 0.0, an_ref[...] * _sigmoid(gn_ref[...]))

    rows = 64
    base = HALO - CONV_K // 2
    for c in range(ch // LANES):
        cs = slice(c * LANES, (c + 1) * LANES)
        for r in range(tc // rows):
            acc = jnp.zeros((rows, LANES), F32)
            for k in range(CONV_K):
                acc = acc + w_ref[k:k + 1, cs] * buf_ref[base + r * rows + k:base + (r + 1) * rows + k, cs]
            acc_ref[r * rows:(r + 1) * rows, cs] = acc

    h = acc_ref[...] + b_ref[...]
    mu = jnp.mean(h, axis=-1, keepdims=True)
    var = jnp.mean(jnp.square(h - mu), axis=-1, keepdims=True)
    y = (h - mu) * lax.rsqrt(var + EPS) * lg_ref[...] + lb_ref[...]
    o_ref[...] = _silu(y).astype(o_ref.dtype)


def _conv_module(u, w_dw, b_dw, ln_g, ln_b, n_rows, lat_rows, seq):
    ch = w_dw.shape[1]
    tc = ROW_TILE
    per = tc // HALO
    nh = u.shape[0] // HALO
    kern = functools.partial(_conv_kernel, lat_tiles=lat_rows // tc, tiles_per_seq=seq // tc)
    prev_map = lambda c: (lambda i: (jnp.maximum(i * per - 1, 0), c))
    next_map = lambda c: (lambda i: (jnp.minimum((i + 1) * per, nh - 1), c))
    vec = lambda a: a.reshape(1, ch)
    return pl.pallas_call(
        kern,
        grid=(n_rows // tc,),
        in_specs=[
            pl.BlockSpec((HALO, ch), prev_map(0)),
            pl.BlockSpec((HALO, ch), prev_map(1)),
            pl.BlockSpec((tc, ch), lambda i: (i, 0)),
            pl.BlockSpec((tc, ch), lambda i: (i, 1)),
            pl.BlockSpec((HALO, ch), next_map(0)),
            pl.BlockSpec((HALO, ch), next_map(1)),
            pl.BlockSpec((CONV_K, ch), lambda i: (0, 0)),
            pl.BlockSpec((1, ch), lambda i: (0, 0)),
            pl.BlockSpec((1, ch), lambda i: (0, 0)),
            pl.BlockSpec((1, ch), lambda i: (0, 0)),
        ],
        out_specs=pl.BlockSpec((tc, ch), lambda i: (i, 0)),
        out_shape=jax.ShapeDtypeStruct((n_rows, ch), BF16),
        scratch_shapes=[pltpu.VMEM((tc + 2 * HALO, ch), F32), pltpu.VMEM((tc, ch), F32)],
        compiler_params=_cparams(("arbitrary",), 16),
    )(u, u, u, u, u, u, w_dw, vec(b_dw), vec(ln_g), vec(ln_b))


def _na_bias_tables(rows):
    groups = rows // NA_ROWS
    reps = [0, min(1, groups - 1), groups - 1]
    out = []
    for g in reps:
        start = int(np.clip(NA_ROWS * g - NA_KH // 2, 0, rows - NA_KEY_ROWS))
        per_row = []
        for i in range(NA_ROWS):
            r = NA_ROWS * g + i
            sr = int(np.clip(r - NA_KH // 2, 0, rows - NA_KH))
            per_row.append((sr - start, sr - r + NA_KH - 1))
        out.append(per_row)
    return out


def _na_bias(rpb, rows):
    nh = rpb.shape[0]
    ndr, ndc = 2 * NA_KH - 1, 2 * NA_KW - 1
    period = 2 * GRID_W - 1
    pad = GRID_W - NA_KW
    vp = jnp.pad(rpb.astype(F32), ((0, 0), (0, 0), (pad, period - ndc - pad)))
    hank = jnp.tile(vp, (1, 1, GRID_W + 1))[:, :, :GRID_W * (period + 1)]
    hank = hank.reshape(nh, ndr, GRID_W, period + 1)[..., :GRID_W]
    toe = hank[:, :, ::-1, :]
    c = np.arange(GRID_W)[:, None]
    j = np.arange(GRID_W)[None, :]
    ws = np.clip(c - NA_KW // 2, 0, GRID_W - NA_KW)
    col_ok = (j >= ws) & (j < ws + NA_KW)
    toe = jnp.where(col_ok[None, None], toe, NEG_INF)
    flat = jnp.transpose(toe, (0, 2, 1, 3)).reshape(nh, GRID_W, ndr * GRID_W)
    nk = NA_KEY_ROWS * GRID_W
    classes = []
    for per_row in _na_bias_tables(rows):
        blocks = []
        for off, lo in per_row:
            win = flat[:, :, lo * GRID_W:(lo + NA_KH) * GRID_W]
            blocks.append(jnp.pad(win, ((0, 0), (0, 0), (off * GRID_W, nk - (off + NA_KH) * GRID_W)),
                                  constant_values=NEG_INF))
        classes.append(jnp.stack(blocks, axis=1).reshape(nh, NA_ROWS * GRID_W, nk))
    return jnp.stack(classes)


def _softmax_pv(s_parts, v_parts):
    m = functools.reduce(jnp.maximum, [jnp.max(s, axis=-1, keepdims=True) for s in s_parts])
    acc, l = None, None
    for s, v in zip(s_parts, v_parts):
        p = jnp.exp(s - m)
        ps = jnp.sum(p, axis=-1, keepdims=True)
        pv = _dot(p.astype(BF16), v)
        l = ps if l is None else l + ps
        acc = pv if acc is None else acc + pv
    return acc / l


def _na_kernel(q_ref, k0, k1, k2, k3, v0, v1, v2, v3, kc_ref, vc_ref, bias_ref, o_ref):
    q2 = q_ref[...] * (NA_HD ** -0.5)
    lane = lax.broadcasted_iota(jnp.int32, q2.shape, 1)
    ks = [k[...].astype(BF16) for k in (k0, k1, k2, k3)]
    vs = [v[...].astype(BF16) for v in (v0, v1, v2, v3)]
    kc = kc_ref[...].astype(BF16)
    vc = vc_ref[...].astype(BF16)
    kw = ks[0].shape[0]
    outs = []
    for a in range(2):
        sel = (lane < NA_HD) if a == 0 else (lane >= NA_HD)
        qa = jnp.where(sel, q2, 0.0).astype(BF16)
        s_parts = [_dot_nt(qa, k) + bias_ref[0, a, :, m * kw:(m + 1) * kw] for m, k in enumerate(ks)]
        s_parts.append(_dot_nt(qa, kc))
        outs.append(_softmax_pv(s_parts, vs + [vc]))
    o_ref[...] = jnp.where(lane < NA_HD, outs[0], outs[1]).astype(o_ref.dtype)


def _na_latent(u, bias, nbatch, seq, lat_rows, off_na):
    rows = seq // GRID_W
    groups = rows // NA_ROWS
    nq = NA_ROWS * GRID_W
    kblk = ROW_TILE
    nkb = NA_KEY_ROWS * GRID_W // kblk
    assert nkb == 4
    qcol = off_na // LANES
    heads2 = NA_HEADS * NA_HD // LANES
    kcol, vcol = qcol + heads2, qcol + 2 * heads2
    kb_per_batch = seq // kblk
    kb_per_grow = GRID_W * NA_ROWS // kblk
    lat_kb = lat_rows // kblk

    def kmap(col, m):
        def f(h, g, b):
            st = jnp.clip(g * kb_per_grow - (NA_KH // 2) * GRID_W // kblk, 0, kb_per_batch - nkb)
            return (b * kb_per_batch + st + m, col + h)
        return f

    def bmap(h, g, b):
        cls = jnp.where(g == 0, 0, jnp.where(g == groups - 1, 2, 1))
        return (cls, h, 0, 0)

    in_specs = [pl.BlockSpec((nq, LANES), lambda h, g, b: (b * groups + g, qcol + h))]
    in_specs += [pl.BlockSpec((kblk, LANES), kmap(kcol, m)) for m in range(nkb)]
    in_specs += [pl.BlockSpec((kblk, LANES), kmap(vcol, m)) for m in range(nkb)]
    in_specs += [pl.BlockSpec((ROW_TILE, LANES), lambda h, g, b: (lat_kb + b, kcol + h)),
                 pl.BlockSpec((ROW_TILE, LANES), lambda h, g, b: (lat_kb + b, vcol + h)),
                 pl.BlockSpec((1, 2, nq, NA_KEY_ROWS * GRID_W), bmap)]
    return pl.pallas_call(
        _na_kernel,
        grid=(heads2, groups, nbatch),
        in_specs=in_specs,
        out_specs=pl.BlockSpec((nq, LANES), lambda h, g, b: (b * groups + g, h)),
        out_shape=jax.ShapeDtypeStruct((lat_rows, NA_HEADS * NA_HD), BF16),
        compiler_params=_cparams(("arbitrary", "arbitrary", "arbitrary"), 40),
    )(*([u] * 11), bias)


def _ctx_attn_kernel(q_ref, k_ref, v_ref, o_ref):
    q2 = q_ref[...] * (NA_HD ** -0.5)
    lane = lax.broadcasted_iota(jnp.int32, q2.shape, 1)
    k = k_ref[...].astype(BF16)
    v = v_ref[...].astype(BF16)
    outs = []
    for a in range(2):
        sel = (lane < NA_HD) if a == 0 else (lane >= NA_HD)
        qa = jnp.where(sel, q2, 0.0).astype(BF16)
        outs.append(_softmax_pv([_dot_nt(qa, k)], [v]))
    o_ref[...] = jnp.where(lane < NA_HD, outs[0], outs[1]).astype(o_ref.dtype)


def _ctx_attn(u, nbatch, ctx_len, lat_rows, off_na):
    assert ctx_len == ROW_TILE
    qcol = off_na // LANES
    heads2 = NA_HEADS * NA_HD // LANES
    base = lat_rows // ROW_TILE
    spec = lambda col: pl.BlockSpec((ROW_TILE, LANES), lambda b, h: (base + b, col + h))
    return pl.pallas_call(
        _ctx_attn_kernel,
        grid=(nbatch, heads2),
        in_specs=[spec(qcol), spec(qcol + heads2), spec(qcol + 2 * heads2)],
        out_specs=pl.BlockSpec((ROW_TILE, LANES), lambda b, h: (b, h)),
        out_shape=jax.ShapeDtypeStruct((nbatch * ctx_len, NA_HEADS * NA_HD), BF16),
        compiler_params=_cparams(("arbitrary", "arbitrary"), 16),
    )(u, u, u)


def _hg_level_map(rev):
    t = np.arange(HG_BLOCK)[:, None]
    s = np.arange(HG_BLOCK)[None, :]
    x = t ^ s
    lvl = np.where(x > 0, np.frexp(np.maximum(x, 1))[1] - 1, -1)
    causal = (s < t) if not rev else (s > t)
    nlev = int(np.log2(HG_BLOCK))
    out = np.where(causal, lvl, -1)
    out = np.where(t == s, nlev, out)
    return out.astype(np.int32)


def _hg_tri(rev):
    t = np.arange(HG_BLOCK)[:, None]
    s = np.arange(HG_BLOCK)[None, :]
    return ((s <= t) if not rev else (s >= t)).astype(np.float32)


def _hg_anchor(b3, m, rev):
    nv = b3.shape[0]
    if m >= SUBLANES:
        w = m // SUBLANES
        b4 = b3.reshape(nv // (2 * w), 2 * w, SUBLANES, LANES)
        a = b4[:, w:w + 1, 0:1, :] if rev else b4[:, w - 1:w, SUBLANES - 1:SUBLANES, :]
        return jnp.broadcast_to(a, b4.shape).reshape(b3.shape)
    sub = lax.broadcasted_iota(jnp.int32, b3.shape, 1)
    out = None
    for g in range(SUBLANES // (2 * m)):
        idx = g * 2 * m + (m if rev else m - 1)
        a = jnp.broadcast_to(b3[:, idx:idx + 1, :], b3.shape)
        out = a if out is None else jnp.where(sub >= g * 2 * m, a, out)
    return out


def _hg_block(q, v, z, alog, clog, oml, tri, lv, st, rev):
    n = q.shape[0]
    nlev = int(np.log2(n))
    q = _silu(q)
    lsig = jnp.minimum(z, 0.0) - jnp.log1p(jnp.exp(-jnp.abs(z)))
    cc = clog + lsig
    logf = jnp.maximum(alog, cc) + jnp.log1p(jnp.exp(-jnp.abs(alog - cc)))
    kk = oml * _sigmoid(-z)

    hi = logf.astype(BF16)
    r1 = logf - hi.astype(F32)
    mid = r1.astype(BF16)
    lo = (r1 - mid.astype(F32)).astype(BF16)
    b = _dot(tri, hi) + _dot(tri, mid) + _dot(tri, lo)

    b3 = b.reshape(n // SUBLANES, SUBLANES, LANES)
    row = lax.broadcasted_iota(jnp.int32, q.shape, 0)
    att = jnp.zeros((n, n), F32)
    for lev in range(nlev):
        m = 1 << lev
        anc = _hg_anchor(b3, m, rev).reshape(q.shape)
        e = jnp.exp(-jnp.abs(b - anc))
        later = (row & m) != 0
        is_q = jnp.logical_not(later) if rev else later
        qm = jnp.where(is_q, q * e, 0.0).astype(BF16)
        km = jnp.where(is_q, 0.0, kk * e).astype(BF16)
        att = jnp.where(lv == lev, _dot_nt(qm, km), att)
    att = jnp.where(lv == nlev, _dot_nt(q.astype(BF16), kk.astype(BF16)), att)

    vb = v.astype(BF16)
    b_last = b[0:1, :] if rev else b[n - 1:n, :]
    qh = (q * jnp.exp(b)).astype(BF16)
    o = _dot(att.astype(BF16), vb) + _dot_nt(qh, st.astype(BF16))
    kh = (kk * jnp.exp(b_last - b)).astype(BF16)
    st_new = st * jnp.exp(b_last) + _dot_tn(vb, kh)
    return o, st_new


def _hg_fwd_kernel(q_ref, v_ref, z_ref, al_ref, cl_ref, om_ref, tri_ref, lv_ref, o_ref, st_ref):
    @pl.when(pl.program_id(1) == 0)
    def _():
        st_ref[...] = jnp.zeros_like(st_ref)

    tri = tri_ref[...]
    lv = lv_ref[...]
    for h in range(HG_HEADS):
        hs = slice(h * HG_DK, (h + 1) * HG_DK)
        o, st = _hg_block(q_ref[:, hs], v_ref[:, hs], z_ref[:, hs], al_ref[:, hs], cl_ref[:, hs],
                          om_ref[:, hs], tri, lv, st_ref[h], False)
        o_ref[:, hs] = o
        st_ref[h] = st


def _hg_bwd_kernel(q_ref, v_ref, z_ref, g_ref, of_ref, al_ref, cl_ref, om_ref, ng_ref, tri_ref, lv_ref,
                   o_ref, st_ref):
    @pl.when(pl.program_id(1) == 0)
    def _():
        st_ref[...] = jnp.zeros_like(st_ref)

    tri = tri_ref[...]
    lv = lv_ref[...]
    for h in range(HG_HEADS):
        hs = slice(h * HG_DK, (h + 1) * HG_DK)
        o, st = _hg_block(q_ref[:, hs], v_ref[:, hs], z_ref[:, hs], al_ref[:, hs], cl_ref[:, hs],
                          om_ref[:, hs], tri, lv, st_ref[h], True)
        st_ref[h] = st
        t = of_ref[:, hs] + o
        y = t * lax.rsqrt(jnp.mean(t * t, axis=-1, keepdims=True) + EPS)
        o_ref[:, hs] = (y * ng_ref[:, hs] * _silu(g_ref[:, hs])).astype(o_ref.dtype)


def _hgrn(u, lb, norm_g, nbatch, seq, ctx_len, lat_rows, off_hg):
    assert ctx_len == HG_BLOCK
    n = u.shape[0]
    hd = HG_HEADS * HG_DK
    col = off_hg // hd
    per = seq // HG_BLOCK
    lat_blocks = lat_rows // HG_BLOCK
    lbf = lb.astype(F32)
    alog, clog, oml = jnp.log(lbf), jnp.log1p(-lbf), 1.0 - lbf

    def fmap(c):
        return lambda b, j: (jnp.where(j == 0, lat_blocks + b, b * per + j - 1), c)

    def bmap(c):
        return lambda b, j: (jnp.where(j == 0, lat_blocks + b, b * per + per - j), c)

    const = lambda shape: pl.BlockSpec(shape, lambda b, j: (0, 0))
    grid = (nbatch, per + 1)
    vec = lambda a: a.reshape(1, hd)
    o_f = pl.pallas_call(
        _hg_fwd_kernel,
        grid=grid,
        in_specs=[pl.BlockSpec((HG_BLOCK, hd), fmap(col)), pl.BlockSpec((HG_BLOCK, hd), fmap(col + 1)),
                  pl.BlockSpec((HG_BLOCK, hd), fmap(col + 2)),
                  const((1, hd)), const((1, hd)), const((1, hd)),
                  const((HG_BLOCK, HG_BLOCK)), const((HG_BLOCK, HG_BLOCK))],
        out_specs=pl.BlockSpec((HG_BLOCK, hd), fmap(0)),
        out_shape=jax.ShapeDtypeStruct((n, hd), F32),
        scratch_shapes=[pltpu.VMEM((HG_HEADS, HG_DK, HG_DK), F32)],
        compiler_params=_cparams(("arbitrary", "arbitrary"), 32),
    )(u, u, u, vec(alog[0]), vec(clog[0]), vec(oml[0]),
      jnp.asarray(_hg_tri(False), BF16), jnp.asarray(_hg_level_map(False)))
    return pl.pallas_call(
        _hg_bwd_kernel,
        grid=grid,
        in_specs=[pl.BlockSpec((HG_BLOCK, hd), bmap(col)), pl.BlockSpec((HG_BLOCK, hd), bmap(col + 1)),
                  pl.BlockSpec((HG_BLOCK, hd), bmap(col + 3)), pl.BlockSpec((HG_BLOCK, hd), bmap(col + 4)),
                  pl.BlockSpec((HG_BLOCK, hd), bmap(0)),
                  const((1, hd)), const((1, hd)), const((1, hd)), const((1, hd)),
                  const((HG_BLOCK, HG_BLOCK)), const((HG_BLOCK, HG_BLOCK))],
        out_specs=pl.BlockSpec((HG_BLOCK, hd), bmap(0)),
        out_shape=jax.ShapeDtypeStruct((n, hd), BF16),
        scratch_shapes=[pltpu.VMEM((HG_HEADS, HG_DK, HG_DK), F32)],
        compiler_params=_cparams(("arbitrary", "arbitrary"), 32),
    )(u, u, u, u, o_f, vec(alog[1]), vec(clog[1]), vec(oml[1]), vec(norm_g.astype(F32)),
      jnp.asarray(_hg_tri(True), BF16), jnp.asarray(_hg_level_map(True)))


def _out_kernel(x_ref, cv_ref, na_ref, hg_ref, w_ref, ga_ref, g2_ref, sh2_ref, s2_ref, wr_ref, br_ref,
                xo_ref, h_ref, lg_ref, *, tiles_per_batch, nbatch):
    r = jnp.minimum(pl.program_id(0) // tiles_per_batch, nbatch)
    c0 = cv_ref.shape[1]
    c1 = c0 + na_ref.shape[1]
    mix = (_dot(cv_ref[...], w_ref[0:c0, :]) + _dot(na_ref[...], w_ref[c0:c1, :])
           + _dot(hg_ref[...], w_ref[c1:, :]))
    xn = x_ref[...] + ga_ref[pl.ds(r, 1), :] * mix
    xo_ref[...] = xn
    h = _rms_mod(xn, g2_ref[...], s2_ref[pl.ds(r, 1), :], sh2_ref[pl.ds(r, 1), :])
    h_ref[...] = h.astype(h_ref.dtype)
    lg_ref[...] = jnp.dot(h, wr_ref[...], precision=lax.Precision.HIGHEST,
                          preferred_element_type=F32) + br_ref[...]


def _out_proj(x, conv, na, hg, w_bf16, mod, g_ffn, w_router, b_router, n_rows, nbatch, lat_rows):
    d = x.shape[1]
    tm = _pick_tile(OUT_TM, lat_rows // nbatch, n_rows)
    kern = functools.partial(_out_kernel, tiles_per_batch=lat_rows // nbatch // tm, nbatch=nbatch)
    row = lambda w: pl.BlockSpec((tm, w), lambda i: (i, 0))
    const = lambda shape, c=0: pl.BlockSpec(shape, lambda i: (0, c))
    return pl.pallas_call(
        kern,
        grid=(n_rows // tm,),
        in_specs=[row(d), row(conv.shape[1]), row(na.shape[1]), row(hg.shape[1]),
                  const((d, d)),
                  const((SUBLANES, d), 2),
                  const((1, d)),
                  const((SUBLANES, d), 3),
                  const((SUBLANES, d), 4),
                  const((d, ROUTER_PAD)), const((1, ROUTER_PAD))],
        out_specs=[row(d), row(d), row(ROUTER_PAD)],
        out_shape=[jax.ShapeDtypeStruct((n_rows, d), F32), jax.ShapeDtypeStruct((n_rows, d), BF16),
                   jax.ShapeDtypeStruct((n_rows, ROUTER_PAD), F32)],
        compiler_params=_cparams(("arbitrary",), 56),
    )(x, conv, na, hg, w_bf16, mod, g_ffn.reshape(1, d), mod, mod, w_router, b_router)


def _moe_kernel(be_ref, nu_ref, xs_ref, w1_ref, w3_ref, w2_ref, sw_ref, o_ref, w1b, w3b, w2b):
    i = pl.program_id(0)
    e = be_ref[i]
    prev = be_ref[jnp.maximum(i - 1, 0)]

    @pl.when(jnp.logical_or(i == 0, e != prev))
    def _():
        w1b[...] = w1_ref[0].astype(BF16)
        w3b[...] = w3_ref[0].astype(BF16)
        w2b[...] = w2_ref[0].astype(BF16)

    @pl.when(i < nu_ref[0])
    def _():
        x = xs_ref[...]
        a = (_silu(_dot(x, w1b[...])) * _dot(x, w3b[...])).astype(BF16)
        o_ref[...] = _dot(a, w2b[...]) * sw_ref[...]

    @pl.when(i >= nu_ref[0])
    def _():
        o_ref[...] = jnp.zeros_like(o_ref)


def _moe_experts(xs, slot_w, blk_e, nused, w1, w3, w2):
    p, d = xs.shape
    de = w1.shape[2]
    bm = MOE_BM
    grid_spec = pltpu.PrefetchScalarGridSpec(
        num_scalar_prefetch=2,
        grid=(p // bm,),
        in_specs=[pl.BlockSpec((bm, d), lambda i, be, nu: (i, 0)),
                  pl.BlockSpec((1, d, de), lambda i, be, nu: (be[i], 0, 0)),
                  pl.BlockSpec((1, d, de), lambda i, be, nu: (be[i], 0, 0)),
                  pl.BlockSpec((1, de, d), lambda i, be, nu: (be[i], 0, 0)),
                  pl.BlockSpec((bm, 1), lambda i, be, nu: (i, 0))],
        out_specs=pl.BlockSpec((bm, d), lambda i, be, nu: (i, 0)),
        scratch_shapes=[pltpu.VMEM((d, de), BF16), pltpu.VMEM((d, de), BF16), pltpu.VMEM((de, d), BF16)],
    )
    return pl.pallas_call(
        _moe_kernel,
        grid_spec=grid_spec,
        out_shape=jax.ShapeDtypeStruct((p, d), F32),
        compiler_params=_cparams(("arbitrary",), 48),
    )(blk_e, nused, xs, w1, w3, w2, slot_w.reshape(p, 1))


def _route(logits, n):
    pg = jax.nn.softmax(logits[:, :N_GROUPS], axis=-1)
    p_grp, grp = lax.top_k(pg, 1)
    le = logits[:, N_GROUPS:N_GROUPS + N_EXPERTS].reshape(n, N_GROUPS, EXP_PER_GROUP)
    le = jnp.take_along_axis(le, grp[:, :, None], axis=1)[:, 0]
    p_exp, idx = lax.top_k(jax.nn.softmax(le, axis=-1), TOP_K)
    p_exp = p_exp / jnp.sum(p_exp, axis=-1, keepdims=True)
    eid = (grp * EXP_PER_GROUP + idx).reshape(-1).astype(jnp.int32)
    wt = (p_grp * p_exp).reshape(-1)
    a = n * TOP_K
    bm = MOE_BM
    nblk = -(-a // bm) + N_EXPERTS
    p = nblk * bm
    cnt = jnp.sum((eid[:, None] == jnp.arange(N_EXPERTS, dtype=jnp.int32)[None, :]).astype(jnp.int32), axis=0)
    pcnt = (cnt + bm - 1) // bm * bm
    pend = jnp.cumsum(pcnt)
    pstart = pend - pcnt
    start = jnp.cumsum(cnt) - cnt
    order = jnp.argsort(eid)
    se = eid[order]
    dst_sorted = pstart[se] + (jnp.arange(a, dtype=jnp.int32) - start[se])
    tok = jnp.arange(a, dtype=jnp.int32) // TOP_K
    slot_tok = jnp.zeros((p,), jnp.int32).at[dst_sorted].set(tok[order])
    slot_w = jnp.zeros((p,), F32).at[dst_sorted].set(wt[order])
    pos = jnp.zeros((a,), jnp.int32).at[order].set(dst_sorted).reshape(n, TOP_K)
    blk_e = jnp.minimum(jnp.searchsorted(pend, jnp.arange(nblk, dtype=jnp.int32) * bm, side="right"),
                        N_EXPERTS - 1).astype(jnp.int32)
    nused = (pend[-1:] // bm).astype(jnp.int32)
    return slot_tok, slot_w, pos, blk_e, nused


def _combine_kernel(x_ref, y0_ref, y1_ref, ga_ref, gf_ref, o_ref, *, tiles_per_batch, nbatch, final):
    r = jnp.minimum(pl.program_id(0) // tiles_per_batch, nbatch)
    xn = x_ref[...] + ga_ref[pl.ds(r, 1), :] * (y0_ref[...] + y1_ref[...])
    if final:
        xn = xn * lax.rsqrt(jnp.mean(xn * xn, axis=-1, keepdims=True) + EPS) * gf_ref[...]
    o_ref[...] = xn


def _combine(x, y0, y1, mod, g_final, n_rows, nbatch, lat_rows, final):
    d = x.shape[1]
    tm = _pick_tile(OUT_TM, lat_rows // nbatch, n_rows)
    kern = functools.partial(_combine_kernel, tiles_per_batch=lat_rows // nbatch // tm, nbatch=nbatch,
                             final=final)
    row = pl.BlockSpec((tm, d), lambda i: (i, 0))
    return pl.pallas_call(
        kern,
        grid=(n_rows // tm,),
        in_specs=[row, row, row, pl.BlockSpec((SUBLANES, d), lambda i: (0, 5)),
                  pl.BlockSpec((1, d), lambda i: (0, 0))],
        out_specs=row,
        out_shape=jax.ShapeDtypeStruct((n_rows, d), F32),
        compiler_params=_cparams(("arbitrary",), 40),
    )(x, y0, y1, mod, g_final.reshape(1, d))


def kernel(x, c, ctx, c_ctx, w_ada, b_ada, g_mix, g_ffn, w_in, conv_w, conv_b, conv_ln_g, conv_ln_b,
           na_rpb, hgrn_lb, hgrn_norm_g, w_out, w_router_group, b_router_group, w_router_expert,
           b_router_expert, w_exp_gate, w_exp_up, w_exp_down, g_final):
    nb, seq, d = x.shape
    ctx_len = ctx.shape[1]
    depth = w_ada.shape[0]
    lat_rows = nb * seq
    n_all = lat_rows + nb * ctx_len
    conv_ch = conv_w.shape[2]
    off_na = 2 * conv_ch
    off_hg = off_na + 3 * NA_HEADS * NA_HD
    rows = seq // GRID_W
    assert nb < SUBLANES and rows % NA_ROWS == 0 and rows >= NA_KEY_ROWS

    lbs = jnp.cumsum(jax.nn.softmax(hgrn_lb.astype(F32), axis=0), axis=0)
    lbs = lbs - lbs[:1]

    cond = jnp.concatenate([c, c_ctx[None, :], jnp.zeros((SUBLANES - nb - 1, d), F32)], axis=0)
    mod = _ada_mod(cond, w_ada, b_ada)

    xs = jnp.concatenate([x.reshape(lat_rows, d), ctx.reshape(nb * ctx_len, d)], axis=0)
    for l in range(depth):
        with_ctx = l < depth - 1
        n_act = n_all if with_ctx else lat_rows
        u = _norm_in(xs, g_mix[l], mod[l], w_in[l].astype(BF16), nb, lat_rows)

        conv = _conv_module(u, conv_w[l], conv_b[l], conv_ln_g[l], conv_ln_b[l], n_act, lat_rows, seq)
        na = _na_latent(u, _na_bias(na_rpb[l], rows), nb, seq, lat_rows, off_na)
        if with_ctx:
            na = jnp.concatenate([na, _ctx_attn(u, nb, ctx_len, lat_rows, off_na)], axis=0)
        hg = _hgrn(u, lbs[l], hgrn_norm_g[l], nb, seq, ctx_len, lat_rows, off_hg)

        w_router = jnp.concatenate(
            [w_router_group[l], w_router_expert[l],
             jnp.zeros((d, ROUTER_PAD - N_GROUPS - N_EXPERTS), F32)], axis=1)
        b_router = jnp.concatenate(
            [b_router_group[l], b_router_expert[l],
             jnp.zeros((ROUTER_PAD - N_GROUPS - N_EXPERTS,), F32)]).reshape(1, ROUTER_PAD)
        x_mid, h, logits = _out_proj(xs, conv, na, hg, w_out[l].astype(BF16), mod[l], g_ffn[l],
                                     w_router, b_router, n_act, nb, lat_rows)

        slot_tok, slot_w, pos, blk_e, nused = _route(logits, n_act)
        ys = _moe_experts(jnp.take(h, slot_tok, axis=0), slot_w, blk_e, nused,
                          w_exp_gate[l], w_exp_up[l], w_exp_down[l])
        y0 = jnp.take(ys, pos[:, 0], axis=0)
        y1 = jnp.take(ys, pos[:, 1], axis=0)
        xs = _combine(x_mid, y0, y1, mod[l], g_final, n_act, nb, lat_rows, final=not with_ctx)
    return xs.reshape(nb, seq, d)
```

```python
import functools

import numpy as np
import jax
import jax.numpy as jnp
from jax import lax
from jax.experimental import pallas as pl
from jax.experimental.pallas import tpu as pltpu

F32 = jnp.float32
BF16 = jnp.bfloat16

EPS = 1e-6
NEG_INF = -1e30

GRID_W = 64
CONV_K = 31
NA_HEADS = 16
NA_HD = 64
NA_KH = 8
NA_KW = 16
HG_HEADS = 4
HG_DK = 128
N_GROUPS = 4
EXP_PER_GROUP = 8
N_EXPERTS = N_GROUPS * EXP_PER_GROUP
TOP_K = 2

LANES = 128
SUBLANES = 8
VMEM_BYTES = 64 * 1024 * 1024

ROW_TILE = 256
IN_TM = 1024
IN_TN = 512
OUT_TM = 512
NA_ROWS = 8
NA_KEY_ROWS = 16
HG_BLOCK = 256
MOE_BM = 256
ROUTER_PAD = LANES
HALO = 16


def _pick_tile(pref, *extents):
    t = pref
    while t > ROW_TILE and any(e % t for e in extents):
        t //= 2
    assert all(e % t == 0 for e in extents)
    return t


def _cparams(sem, vmem_mb):
    return pltpu.CompilerParams(dimension_semantics=sem, vmem_limit_bytes=vmem_mb * 1024 * 1024)


def _dot(a, b):
    return jnp.dot(a, b, preferred_element_type=F32)


def _dot_nt(a, b):
    return lax.dot_general(a, b, (((1,), (1,)), ((), ())), preferred_element_type=F32)


def _dot_tn(a, b):
    return lax.dot_general(a, b, (((0,), (0,)), ((), ())), preferred_element_type=F32)


def _sigmoid(x):
    return 1.0 / (1.0 + jnp.exp(-x))


def _silu(x):
    return x * _sigmoid(x)


def _ada_kernel(c_ref, w_ref, b_ref, o_ref):
    sc = _silu(c_ref[...])
    o_ref[0] = jnp.dot(sc, w_ref[0], precision=lax.Precision.HIGHEST,
                       preferred_element_type=F32) + b_ref[0]


def _ada_mod(cond, w_ada, b_ada):
    depth, d, n = w_ada.shape
    tn = 1024
    return pl.pallas_call(
        _ada_kernel,
        grid=(depth, n // tn),
        in_specs=[
            pl.BlockSpec((SUBLANES, d), lambda l, j: (0, 0)),
            pl.BlockSpec((1, d, tn), lambda l, j: (l, 0, j)),
            pl.BlockSpec((1, 1, tn), lambda l, j: (l, 0, j)),
        ],
        out_specs=pl.BlockSpec((1, SUBLANES, tn), lambda l, j: (l, 0, j)),
        out_shape=jax.ShapeDtypeStruct((depth, SUBLANES, n), F32),
        compiler_params=_cparams(("arbitrary", "arbitrary"), 40),
    )(cond, w_ada, b_ada.reshape(depth, 1, n))


def _rms_mod(x, g, scale, shift):
    y = x * lax.rsqrt(jnp.mean(x * x, axis=-1, keepdims=True) + EPS)
    return (y * g) * (1.0 + scale) + shift


def _norm_in_kernel(x_ref, g_ref, sh_ref, sc_ref, w_ref, o_ref, h_ref, *, tiles_per_batch, nbatch):
    i = pl.program_id(0)

    @pl.when(pl.program_id(1) == 0)
    def _():
        r = jnp.minimum(i // tiles_per_batch, nbatch)
        h = _rms_mod(x_ref[...], g_ref[...], sc_ref[pl.ds(r, 1), :], sh_ref[pl.ds(r, 1), :])
        h_ref[...] = h.astype(BF16)

    o_ref[...] = _dot(h_ref[...], w_ref[...])


def _norm_in(x, g, mod, w_bf16, l, nbatch, lat_rows):
    n, d = x.shape
    nout = w_bf16.shape[2]
    tm, tn = _pick_tile(IN_TM, lat_rows // nbatch, n), IN_TN
    kern = functools.partial(_norm_in_kernel, tiles_per_batch=lat_rows // nbatch // tm, nbatch=nbatch)
    return pl.pallas_call(
        kern,
        grid=(n // tm, nout // tn),
        in_specs=[
            pl.BlockSpec((tm, d), lambda i, j: (i, 0)),
            pl.BlockSpec((1, d), lambda i, j: (0, 0)),
            pl.BlockSpec((None, SUBLANES, d), lambda i, j: (l, 0, 0)),
            pl.BlockSpec((None, SUBLANES, d), lambda i, j: (l, 0, 1)),
            pl.BlockSpec((None, d, tn), lambda i, j: (l, 0, j)),
        ],
        out_specs=pl.BlockSpec((tm, tn), lambda i, j: (i, j)),
        out_shape=jax.ShapeDtypeStruct((n, nout), F32),
        scratch_shapes=[pltpu.VMEM((tm, d), BF16)],
        compiler_params=_cparams(("arbitrary", "arbitrary"), 48),
    )(x, g.reshape(1, d), mod, mod, w_bf16)


def _conv_kernel(ap_ref, gp_ref, a_ref, gt_ref, an_ref, gn_ref, w_ref, b_ref, lg_ref, lb_ref,
                 o_ref, buf_ref, acc_ref, *, lat_tiles, tiles_per_seq):
    i = pl.program_id(0)
    tc, ch = a_ref.shape
    is_lat = i < lat_tiles
    pos = i % tiles_per_seq
    first = jnp.logical_or(jnp.logical_not(is_lat), pos == 0)
    last = jnp.logical_or(jnp.logical_not(is_lat), pos == tiles_per_seq - 1)

    buf_ref[0:HALO] = jnp.where(first, 0.0, ap_ref[...] * _sigmoid(gp_ref[...]))
    buf_ref[HALO:HALO + tc] = a_ref[...] * _sigmoid(gt_ref[...])
    buf_ref[HALO + tc:2 * HALO + tc] = jnp.where(last, 0.0, an_ref[...] * _sigmoid(gn_ref[...]))

    rows = 64
    base = HALO - CONV_K // 2
    for c in range(ch // LANES):
        cs = slice(c * LANES, (c + 1) * LANES)
        for r in range(tc // rows):
            acc = jnp.zeros((rows, LANES), F32)
            for k in range(CONV_K):
                acc = acc + w_ref[k:k + 1, cs] * buf_ref[base + r * rows + k:base + (r + 1) * rows + k, cs]
            acc_ref[r * rows:(r + 1) * rows, cs] = acc

    h = acc_ref[...] + b_ref[...]
    mu = jnp.mean(h, axis=-1, keepdims=True)
    var = jnp.mean(jnp.square(h - mu), axis=-1, keepdims=True)
    y = (h - mu) * lax.rsqrt(var + EPS) * lg_ref[...] + lb_ref[...]
    o_ref[...] = _silu(y).astype(o_ref.dtype)


def _conv_module(u, w_dw, b_dw, ln_g, ln_b, n_rows, lat_rows, seq):
    ch = w_dw.shape[1]
    tc = ROW_TILE
    per = tc // HALO
    nh = u.shape[0] // HALO
    kern = functools.partial(_conv_kernel, lat_tiles=lat_rows // tc, tiles_per_seq=seq // tc)
    prev_map = lambda c: (lambda i: (jnp.maximum(i * per - 1, 0), c))
    next_map = lambda c: (lambda i: (jnp.minimum((i + 1) * per, nh - 1), c))
    vec = lambda a: a.reshape(1, ch)
    return pl.pallas_call(
        kern,
        grid=(n_rows // tc,),
        in_specs=[
            pl.BlockSpec((HALO, ch), prev_map(0)),
            pl.BlockSpec((HALO, ch), prev_map(1)),
            pl.BlockSpec((tc, ch), lambda i: (i, 0)),
            pl.BlockSpec((tc, ch), lambda i: (i, 1)),
            pl.BlockSpec((HALO, ch), next_map(0)),
            pl.BlockSpec((HALO, ch), next_map(1)),
            pl.BlockSpec((CONV_K, ch), lambda i: (0, 0)),
            pl.BlockSpec((1, ch), lambda i: (0, 0)),
            pl.BlockSpec((1, ch), lambda i: (0, 0)),
            pl.BlockSpec((1, ch), lambda i: (0, 0)),
        ],
        out_specs=pl.BlockSpec((tc, ch), lambda i: (i, 0)),
        out_shape=jax.ShapeDtypeStruct((n_rows, ch), BF16),
        scratch_shapes=[pltpu.VMEM((tc + 2 * HALO, ch), F32), pltpu.VMEM((tc, ch), F32)],
        compiler_params=_cparams(("arbitrary",), 16),
    )(u, u, u, u, u, u, w_dw, vec(b_dw), vec(ln_g), vec(ln_b))


def _na_bias_tables(rows):
    groups = rows // NA_ROWS
    reps = [0, min(1, groups - 1), groups - 1]
    out = []
    for g in reps:
        start = int(np.clip(NA_ROWS * g - NA_KH // 2, 0, rows - NA_KEY_ROWS))
        per_row = []
        for i in range(NA_ROWS):
            r = NA_ROWS * g + i
            sr = int(np.clip(r - NA_KH // 2, 0, rows - NA_KH))
            per_row.append((sr - start, sr - r + NA_KH - 1))
        out.append(per_row)
    return out


def _na_bias(rpb, rows):
    nh = rpb.shape[0]
    ndr, ndc = 2 * NA_KH - 1, 2 * NA_KW - 1
    period = 2 * GRID_W - 1
    pad = GRID_W - NA_KW
    vp = jnp.pad(rpb.astype(F32), ((0, 0), (0, 0), (pad, period - ndc - pad)))
    hank = jnp.tile(vp, (1, 1, GRID_W + 1))[:, :, :GRID_W * (period + 1)]
    hank = hank.reshape(nh, ndr, GRID_W, period + 1)[..., :GRID_W]
    toe = hank[:, :, ::-1, :]
    c = np.arange(GRID_W)[:, None]
    j = np.arange(GRID_W)[None, :]
    ws = np.clip(c - NA_KW // 2, 0, GRID_W - NA_KW)
    col_ok = (j >= ws) & (j < ws + NA_KW)
    toe = jnp.where(col_ok[None, None], toe, NEG_INF)
    flat = jnp.transpose(toe, (0, 2, 1, 3)).reshape(nh, GRID_W, ndr * GRID_W)
    nk = NA_KEY_ROWS * GRID_W
    classes = []
    for per_row in _na_bias_tables(rows):
        blocks = []
        for off, lo in per_row:
            win = flat[:, :, lo * GRID_W:(lo + NA_KH) * GRID_W]
            blocks.append(jnp.pad(win, ((0, 0), (0, 0), (off * GRID_W, nk - (off + NA_KH) * GRID_W)),
                                  constant_values=NEG_INF))
        classes.append(jnp.stack(blocks, axis=1).reshape(nh, NA_ROWS * GRID_W, nk))
    return jnp.stack(classes)


def _softmax_pv(s_parts, v_parts):
    m = functools.reduce(jnp.maximum, [jnp.max(s, axis=-1, keepdims=True) for s in s_parts])
    acc, l = None, None
    for s, v in zip(s_parts, v_parts):
        p = jnp.exp(s - m)
        ps = jnp.sum(p, axis=-1, keepdims=True)
        pv = _dot(p.astype(BF16), v)
        l = ps if l is None else l + ps
        acc = pv if acc is None else acc + pv
    return acc / l


def _na_kernel(q_ref, k0, k1, k2, k3, v0, v1, v2, v3, kc_ref, vc_ref, bias_ref, o_ref):
    q2 = q_ref[...] * (NA_HD ** -0.5)
    lane = lax.broadcasted_iota(jnp.int32, q2.shape, 1)
    ks = [k[...].astype(BF16) for k in (k0, k1, k2, k3)]
    vs = [v[...].astype(BF16) for v in (v0, v1, v2, v3)]
    kc = kc_ref[...].astype(BF16)
    vc = vc_ref[...].astype(BF16)
    kw = ks[0].shape[0]
    outs = []
    for a in range(2):
        sel = (lane < NA_HD) if a == 0 else (lane >= NA_HD)
        qa = jnp.where(sel, q2, 0.0).astype(BF16)
        s_parts = [_dot_nt(qa, k) + bias_ref[0, a, :, m * kw:(m + 1) * kw] for m, k in enumerate(ks)]
        s_parts.append(_dot_nt(qa, kc))
        outs.append(_softmax_pv(s_parts, vs + [vc]))
    o_ref[...] = jnp.where(lane < NA_HD, outs[0], outs[1]).astype(o_ref.dtype)


def _na_latent(u, bias, nbatch, seq, lat_rows, off_na):
    rows = seq // GRID_W
    groups = rows // NA_ROWS
    nq = NA_ROWS * GRID_W
    kblk = ROW_TILE
    nkb = NA_KEY_ROWS * GRID_W // kblk
    assert nkb == 4
    qcol = off_na // LANES
    heads2 = NA_HEADS * NA_HD // LANES
    kcol, vcol = qcol + heads2, qcol + 2 * heads2
    kb_per_batch = seq // kblk
    kb_per_grow = GRID_W * NA_ROWS // kblk
    lat_kb = lat_rows // kblk

    def kmap(col, m):
        def f(h, g, b):
            st = jnp.clip(g * kb_per_grow - (NA_KH // 2) * GRID_W // kblk, 0, kb_per_batch - nkb)
            return (b * kb_per_batch + st + m, col + h)
        return f

    def bmap(h, g, b):
        cls = jnp.where(g == 0, 0, jnp.where(g == groups - 1, 2, 1))
        return (cls, h, 0, 0)

    in_specs = [pl.BlockSpec((nq, LANES), lambda h, g, b: (b * groups + g, qcol + h))]
    in_specs += [pl.BlockSpec((kblk, LANES), kmap(kcol, m)) for m in range(nkb)]
    in_specs += [pl.BlockSpec((kblk, LANES), kmap(vcol, m)) for m in range(nkb)]
    in_specs += [pl.BlockSpec((ROW_TILE, LANES), lambda h, g, b: (lat_kb + b, kcol + h)),
                 pl.BlockSpec((ROW_TILE, LANES), lambda h, g, b: (lat_kb + b, vcol + h)),
                 pl.BlockSpec((1, 2, nq, NA_KEY_ROWS * GRID_W), bmap)]
    return pl.pallas_call(
        _na_kernel,
        grid=(heads2, groups, nbatch),
        in_specs=in_specs,
        out_specs=pl.BlockSpec((nq, LANES), lambda h, g, b: (b * groups + g, h)),
        out_shape=jax.ShapeDtypeStruct((lat_rows, NA_HEADS * NA_HD), BF16),
        compiler_params=_cparams(("arbitrary", "arbitrary", "arbitrary"), 40),
    )(*([u] * 11), bias)


def _ctx_attn_kernel(q_ref, k_ref, v_ref, o_ref):
    q2 = q_ref[...] * (NA_HD ** -0.5)
    lane = lax.broadcasted_iota(jnp.int32, q2.shape, 1)
    k = k_ref[...].astype(BF16)
    v = v_ref[...].astype(BF16)
    outs = []
    for a in range(2):
        sel = (lane < NA_HD) if a == 0 else (lane >= NA_HD)
        qa = jnp.where(sel, q2, 0.0).astype(BF16)
        outs.append(_softmax_pv([_dot_nt(qa, k)], [v]))
    o_ref[...] = jnp.where(lane < NA_HD, outs[0], outs[1]).astype(o_ref.dtype)


def _ctx_attn(u, nbatch, ctx_len, lat_rows, off_na):
    assert ctx_len == ROW_TILE
    qcol = off_na // LANES
    heads2 = NA_HEADS * NA_HD // LANES
    base = lat_rows // ROW_TILE
    spec = lambda col: pl.BlockSpec((ROW_TILE, LANES), lambda b, h: (base + b, col + h))
    return pl.pallas_call(
        _ctx_attn_kernel,
        grid=(nbatch, heads2),
        in_specs=[spec(qcol), spec(qcol + heads2), spec(qcol + 2 * heads2)],
        out_specs=pl.BlockSpec((ROW_TILE, LANES), lambda b, h: (b, h)),
        out_shape=jax.ShapeDtypeStruct((nbatch * ctx_len, NA_HEADS * NA_HD), BF16),
        compiler_params=_cparams(("arbitrary", "arbitrary"), 16),
    )(u, u, u)


def _hg_level_map(rev):
    t = np.arange(HG_BLOCK)[:, None]
    s = np.arange(HG_BLOCK)[None, :]
    x = t ^ s
    lvl = np.where(x > 0, np.frexp(np.maximum(x, 1))[1] - 1, -1)
    causal = (s < t) if not rev else (s > t)
    nlev = int(np.log2(HG_BLOCK))
    out = np.where(causal, lvl, -1)
    out = np.where(t == s, nlev, out)
    return out.astype(np.int32)


def _hg_tri(rev):
    t = np.arange(HG_BLOCK)[:, None]
    s = np.arange(HG_BLOCK)[None, :]
    return ((s <= t) if not rev else (s >= t)).astype(np.float32)


def _hg_anchor(b3, m, rev):
    nv = b3.shape[0]
    if m >= SUBLANES:
        w = m // SUBLANES
        b4 = b3.reshape(nv // (2 * w), 2 * w, SUBLANES, LANES)
        a = b4[:, w:w + 1, 0:1, :] if rev else b4[:, w - 1:w, SUBLANES - 1:SUBLANES, :]
        return jnp.broadcast_to(a, b4.shape).reshape(b3.shape)
    sub = lax.broadcasted_iota(jnp.int32, b3.shape, 1)
    out = None
    for g in range(SUBLANES // (2 * m)):
        idx = g * 2 * m + (m if rev else m - 1)
        a = jnp.broadcast_to(b3[:, idx:idx + 1, :], b3.shape)
        out = a if out is None else jnp.where(sub >= g * 2 * m, a, out)
    return out


def _hg_block(q, v, z, alog, clog, oml, tri, lv, st, rev):
    n = q.shape[0]
    nlev = int(np.log2(n))
    q = _silu(q)
    lsig = jnp.minimum(z, 0.0) - jnp.log1p(jnp.exp(-jnp.abs(z)))
    cc = clog + lsig
    logf = jnp.maximum(alog, cc) + jnp.log1p(jnp.exp(-jnp.abs(alog - cc)))
    kk = oml * _sigmoid(-z)

    hi = logf.astype(BF16)
    r1 = logf - hi.astype(F32)
    mid = r1.astype(BF16)
    lo = (r1 - mid.astype(F32)).astype(BF16)
    b = _dot(tri, hi) + _dot(tri, mid) + _dot(tri, lo)

    b3 = b.reshape(n // SUBLANES, SUBLANES, LANES)
    row = lax.broadcasted_iota(jnp.int32, q.shape, 0)
    att = jnp.zeros((n, n), F32)
    for lev in range(nlev):
        m = 1 << lev
        anc = _hg_anchor(b3, m, rev).reshape(q.shape)
        e = jnp.exp(-jnp.abs(b - anc))
        later = (row & m) != 0
        is_q = jnp.logical_not(later) if rev else later
        qm = jnp.where(is_q, q * e, 0.0).astype(BF16)
        km = jnp.where(is_q, 0.0, kk * e).astype(BF16)
        att = jnp.where(lv == lev, _dot_nt(qm, km), att)
    att = jnp.where(lv == nlev, _dot_nt(q.astype(BF16), kk.astype(BF16)), att)

    vb = v.astype(BF16)
    b_last = b[0:1, :] if rev else b[n - 1:n, :]
    qh = (q * jnp.exp(b)).astype(BF16)
    o = _dot(att.astype(BF16), vb) + _dot_nt(qh, st.astype(BF16))
    kh = (kk * jnp.exp(b_last - b)).astype(BF16)
    st_new = st * jnp.exp(b_last) + _dot_tn(vb, kh)
    return o, st_new


def _hg_fwd_kernel(q_ref, v_ref, z_ref, al_ref, cl_ref, om_ref, tri_ref, lv_ref, o_ref, st_ref):
    @pl.when(pl.program_id(1) == 0)
    def _():
        st_ref[...] = jnp.zeros_like(st_ref)

    tri = tri_ref[...]
    lv = lv_ref[...]
    for h in range(HG_HEADS):
        hs = slice(h * HG_DK, (h + 1) * HG_DK)
        o, st = _hg_block(q_ref[:, hs], v_ref[:, hs], z_ref[:, hs], al_ref[:, hs], cl_ref[:, hs],
                          om_ref[:, hs], tri, lv, st_ref[h], False)
        o_ref[:, hs] = o
        st_ref[h] = st


def _hg_bwd_kernel(q_ref, v_ref, z_ref, g_ref, of_ref, al_ref, cl_ref, om_ref, ng_ref, tri_ref, lv_ref,
                   o_ref, st_ref):
    @pl.when(pl.program_id(1) == 0)
    def _():
        st_ref[...] = jnp.zeros_like(st_ref)

    tri = tri_ref[...]
    lv = lv_ref[...]
    for h in range(HG_HEADS):
        hs = slice(h * HG_DK, (h + 1) * HG_DK)
        o, st = _hg_block(q_ref[:, hs], v_ref[:, hs], z_ref[:, hs], al_ref[:, hs], cl_ref[:, hs],
                          om_ref[:, hs], tri, lv, st_ref[h], True)
        st_ref[h] = st
        t = of_ref[:, hs] + o
        y = t * lax.rsqrt(jnp.mean(t * t, axis=-1, keepdims=True) + EPS)
        o_ref[:, hs] = (y * ng_ref[:, hs] * _silu(g_ref[:, hs])).astype(o_ref.dtype)


def _hgrn(u, lb, norm_g, nbatch, seq, ctx_len, lat_rows, off_hg):
    assert ctx_len == HG_BLOCK
    n = u.shape[0]
    hd = HG_HEADS * HG_DK
    col = off_hg // hd
    per = seq // HG_BLOCK
    lat_blocks = lat_rows // HG_BLOCK
    lbf = lb.astype(F32)
    alog, clog, oml = jnp.log(lbf), jnp.log1p(-lbf), 1.0 - lbf

    def fmap(c):
        return lambda b, j: (jnp.where(j == 0, lat_blocks + b, b * per + j - 1), c)

    def bmap(c):
        return lambda b, j: (jnp.where(j == 0, lat_blocks + b, b * per + per - j), c)

    const = lambda shape: pl.BlockSpec(shape, lambda b, j: (0, 0))
    grid = (nbatch, per + 1)
    vec = lambda a: a.reshape(1, hd)
    o_f = pl.pallas_call(
        _hg_fwd_kernel,
        grid=grid,
        in_specs=[pl.BlockSpec((HG_BLOCK, hd), fmap(col)), pl.BlockSpec((HG_BLOCK, hd), fmap(col + 1)),
                  pl.BlockSpec((HG_BLOCK, hd), fmap(col + 2)),
                  const((1, hd)), const((1, hd)), const((1, hd)),
                  const((HG_BLOCK, HG_BLOCK)), const((HG_BLOCK, HG_BLOCK))],
        out_specs=pl.BlockSpec((HG_BLOCK, hd), fmap(0)),
        out_shape=jax.ShapeDtypeStruct((n, hd), F32),
        scratch_shapes=[pltpu.VMEM((HG_HEADS, HG_DK, HG_DK), F32)],
        compiler_params=_cparams(("arbitrary", "arbitrary"), 32),
    )(u, u, u, vec(alog[0]), vec(clog[0]), vec(oml[0]),
      jnp.asarray(_hg_tri(False), BF16), jnp.asarray(_hg_level_map(False)))
    return pl.pallas_call(
        _hg_bwd_kernel,
        grid=grid,
        in_specs=[pl.BlockSpec((HG_BLOCK, hd), bmap(col)), pl.BlockSpec((HG_BLOCK, hd), bmap(col + 1)),
                  pl.BlockSpec((HG_BLOCK, hd), bmap(col + 3)), pl.BlockSpec((HG_BLOCK, hd), bmap(col + 4)),
                  pl.BlockSpec((HG_BLOCK, hd), bmap(0)),
                  const((1, hd)), const((1, hd)), const((1, hd)), const((1, hd)),
                  const((HG_BLOCK, HG_BLOCK)), const((HG_BLOCK, HG_BLOCK))],
        out_specs=pl.BlockSpec((HG_BLOCK, hd), bmap(0)),
        out_shape=jax.ShapeDtypeStruct((n, hd), BF16),
        scratch_shapes=[pltpu.VMEM((HG_HEADS, HG_DK, HG_DK), F32)],
        compiler_params=_cparams(("arbitrary", "arbitrary"), 32),
    )(u, u, u, u, o_f, vec(alog[1]), vec(clog[1]), vec(oml[1]), vec(norm_g.astype(F32)),
      jnp.asarray(_hg_tri(True), BF16), jnp.asarray(_hg_level_map(True)))


def _out_kernel(x_ref, cv_ref, na_ref, hg_ref, w_ref, ga_ref, g2_ref, sh2_ref, s2_ref, wr_ref, br_ref,
                xo_ref, h_ref, rt_ref, *, tiles_per_batch, nbatch):
    r = jnp.minimum(pl.program_id(0) // tiles_per_batch, nbatch)
    c0 = cv_ref.shape[1]
    c1 = c0 + na_ref.shape[1]
    mix = (_dot(cv_ref[...], w_ref[0:c0, :]) + _dot(na_ref[...], w_ref[c0:c1, :])
           + _dot(hg_ref[...], w_ref[c1:, :]))
    xn = x_ref[...] + ga_ref[pl.ds(r, 1), :] * mix
    xo_ref[...] = xn
    h = _rms_mod(xn, g2_ref[...], s2_ref[pl.ds(r, 1), :], sh2_ref[pl.ds(r, 1), :])
    h_ref[...] = h.astype(h_ref.dtype)
    logits = jnp.dot(h, wr_ref[...], precision=lax.Precision.HIGHEST,
                     preferred_element_type=F32) + br_ref[...]
    rt_ref[...] = _route_rows(logits)


def _route_rows(lg):
    lane = lax.broadcasted_iota(jnp.int32, lg.shape, 1)
    big = jnp.int32(2 ** 30)
    low = jnp.float32(-3e38)

    def first_max(vals, mask):
        m = jnp.max(vals, axis=-1, keepdims=True)
        idx = jnp.min(jnp.where(jnp.logical_and(vals == m, mask), lane, big), axis=-1, keepdims=True)
        return m, idx

    gmask = lane < N_GROUPS
    gl = jnp.where(gmask, lg, low)
    gm, grp = first_max(gl, gmask)
    p_grp = 1.0 / jnp.sum(jnp.where(gmask, jnp.exp(gl - gm), 0.0), axis=-1, keepdims=True)
    lo = N_GROUPS + grp * EXP_PER_GROUP
    emask = jnp.logical_and(lane >= lo, lane < lo + EXP_PER_GROUP)
    el = jnp.where(emask, lg, low)
    m1, i1 = first_max(el, emask)
    emask2 = jnp.logical_and(emask, lane != i1)
    el2 = jnp.where(emask2, lg, low)
    m2, i2 = first_max(el2, emask2)
    t = jnp.exp(m2 - m1)
    w1 = p_grp / (1.0 + t)
    w2 = p_grp * t / (1.0 + t)
    e1 = (i1 - N_GROUPS).astype(F32)
    e2 = (i2 - N_GROUPS).astype(F32)
    return jnp.where(lane == 0, e1, jnp.where(lane == 1, e2, jnp.where(lane == 2, w1,
                     jnp.where(lane == 3, w2, 0.0))))


def _out_proj(x, conv, na, hg, w_bf16, l, mod, g_ffn, w_router, b_router, n_rows, nbatch, lat_rows):
    d = x.shape[1]
    tm = _pick_tile(OUT_TM, lat_rows // nbatch, n_rows)
    kern = functools.partial(_out_kernel, tiles_per_batch=lat_rows // nbatch // tm, nbatch=nbatch)
    row = lambda w: pl.BlockSpec((tm, w), lambda i: (i, 0))
    const = lambda shape: pl.BlockSpec(shape, lambda i: (0, 0))
    modc = lambda c: pl.BlockSpec((None, SUBLANES, d), lambda i: (l, 0, c))
    return pl.pallas_call(
        kern,
        grid=(n_rows // tm,),
        in_specs=[row(d), row(conv.shape[1]), row(na.shape[1]), row(hg.shape[1]),
                  pl.BlockSpec((None, d, d), lambda i: (l, 0, 0)),
                  modc(2),
                  const((1, d)),
                  modc(3),
                  modc(4),
                  const((d, ROUTER_PAD)), const((1, ROUTER_PAD))],
        out_specs=[row(d), row(d), row(ROUTER_PAD)],
        out_shape=[jax.ShapeDtypeStruct((n_rows, d), F32), jax.ShapeDtypeStruct((n_rows, d), BF16),
                   jax.ShapeDtypeStruct((n_rows, ROUTER_PAD), F32)],
        compiler_params=_cparams(("arbitrary",), 56),
    )(x, conv, na, hg, w_bf16, mod, g_ffn.reshape(1, d), mod, mod, w_router, b_router)


def _moe_kernel(be_ref, nu_ref, xs_ref, w1_ref, w3_ref, w2_ref, sw_ref, o_ref, w1b, w3b, w2b):
    i = pl.program_id(0)
    e = be_ref[i]
    prev = be_ref[jnp.maximum(i - 1, 0)]

    @pl.when(jnp.logical_or(i == 0, e != prev))
    def _():
        w1b[...] = w1_ref[...].astype(BF16)
        w3b[...] = w3_ref[...].astype(BF16)
        w2b[...] = w2_ref[...].astype(BF16)

    @pl.when(i < nu_ref[0])
    def _():
        x = xs_ref[...]
        a = (_silu(_dot(x, w1b[...])) * _dot(x, w3b[...])).astype(BF16)
        o_ref[...] = _dot(a, w2b[...]) * sw_ref[...]

    @pl.when(i >= nu_ref[0])
    def _():
        o_ref[...] = jnp.zeros_like(o_ref)


def _moe_experts(xs, slot_w, blk_e, nused, w1, w3, w2, l):
    p, d = xs.shape
    de = w1.shape[3]
    bm = MOE_BM
    grid_spec = pltpu.PrefetchScalarGridSpec(
        num_scalar_prefetch=2,
        grid=(p // bm,),
        in_specs=[pl.BlockSpec((bm, d), lambda i, be, nu: (i, 0)),
                  pl.BlockSpec((None, None, d, de), lambda i, be, nu: (l, be[i], 0, 0)),
                  pl.BlockSpec((None, None, d, de), lambda i, be, nu: (l, be[i], 0, 0)),
                  pl.BlockSpec((None, None, de, d), lambda i, be, nu: (l, be[i], 0, 0)),
                  pl.BlockSpec((bm, 1), lambda i, be, nu: (i, 0))],
        out_specs=pl.BlockSpec((bm, d), lambda i, be, nu: (i, 0)),
        scratch_shapes=[pltpu.VMEM((d, de), BF16), pltpu.VMEM((d, de), BF16), pltpu.VMEM((de, d), BF16)],
    )
    return pl.pallas_call(
        _moe_kernel,
        grid_spec=grid_spec,
        out_shape=jax.ShapeDtypeStruct((p, d), F32),
        compiler_params=_cparams(("arbitrary",), 48),
    )(blk_e, nused, xs, w1, w3, w2, slot_w.reshape(p, 1))


def _route_meta(route, n):
    i32 = jnp.int32
    eid = route[:, 0:TOP_K].astype(i32).reshape(-1)
    wt = route[:, TOP_K:2 * TOP_K].reshape(-1)
    a = n * TOP_K
    bm = MOE_BM
    nblk = -(-a // bm) + N_EXPERTS
    p = nblk * bm
    experts = jnp.arange(N_EXPERTS, dtype=i32)[None, :]
    ja = jnp.arange(a, dtype=i32)
    se, order, wsort = lax.sort((eid, ja, wt), num_keys=1, is_stable=True)
    cnt = jnp.sum((eid[:, None] == experts).astype(i32), axis=0)
    pcnt = (cnt + bm - 1) // bm * bm
    pend = jnp.cumsum(pcnt)
    pstart = pend - pcnt
    end = jnp.cumsum(cnt)
    start = end - cnt
    off = pstart - start
    d_off = off - jnp.concatenate([jnp.zeros((1,), i32), off[:-1]])
    dst_sorted = ja + jnp.sum(jnp.where(ja[:, None] >= start[None, :], d_off[None, :], 0), axis=1)
    _, pos = lax.sort((order, dst_sorted), num_keys=1)
    jp = jnp.arange(p, dtype=i32)
    in_or_after = jp[:, None] >= pstart[None, :]
    src = jp - jnp.sum(jnp.where(in_or_after, d_off[None, :], 0), axis=1)
    valid = src < jnp.sum(jnp.where(in_or_after, cnt[None, :], 0), axis=1)
    src = jnp.where(valid, src, 0)
    slot_tok = jnp.take(order, src) // TOP_K
    slot_w = jnp.where(valid, jnp.take(wsort, src), 0.0)
    jb = jnp.arange(nblk, dtype=i32) * bm
    blk_e = jnp.minimum(jnp.sum((jb[:, None] >= pend[None, :]).astype(i32), axis=1), N_EXPERTS - 1)
    nused = (pend[-1:] // bm).astype(i32)
    return slot_tok, slot_w, pos.reshape(n, TOP_K), blk_e, nused


def _combine_kernel(x_ref, y0_ref, y1_ref, ga_ref, gf_ref, o_ref, *, tiles_per_batch, nbatch, final):
    r = jnp.minimum(pl.program_id(0) // tiles_per_batch, nbatch)
    xn = x_ref[...] + ga_ref[pl.ds(r, 1), :] * (y0_ref[...] + y1_ref[...])
    if final:
        xn = xn * lax.rsqrt(jnp.mean(xn * xn, axis=-1, keepdims=True) + EPS) * gf_ref[...]
    o_ref[...] = xn


def _combine(x, y0, y1, mod, l, g_final, n_rows, nbatch, lat_rows, final):
    d = x.shape[1]
    tm = _pick_tile(OUT_TM, lat_rows // nbatch, n_rows)
    kern = functools.partial(_combine_kernel, tiles_per_batch=lat_rows // nbatch // tm, nbatch=nbatch,
                             final=final)
    row = pl.BlockSpec((tm, d), lambda i: (i, 0))
    return pl.pallas_call(
        kern,
        grid=(n_rows // tm,),
        in_specs=[row, row, row, pl.BlockSpec((None, SUBLANES, d), lambda i: (l, 0, 5)),
                  pl.BlockSpec((1, d), lambda i: (0, 0))],
        out_specs=row,
        out_shape=jax.ShapeDtypeStruct((n_rows, d), F32),
        compiler_params=_cparams(("arbitrary",), 40),
    )(x, y0, y1, mod, g_final.reshape(1, d))


def kernel(x, c, ctx, c_ctx, w_ada, b_ada, g_mix, g_ffn, w_in, conv_w, conv_b, conv_ln_g, conv_ln_b,
           na_rpb, hgrn_lb, hgrn_norm_g, w_out, w_router_group, b_router_group, w_router_expert,
           b_router_expert, w_exp_gate, w_exp_up, w_exp_down, g_final):
    nb, seq, d = x.shape
    ctx_len = ctx.shape[1]
    depth = w_ada.shape[0]
    lat_rows = nb * seq
    n_all = lat_rows + nb * ctx_len
    conv_ch = conv_w.shape[2]
    off_na = 2 * conv_ch
    off_hg = off_na + 3 * NA_HEADS * NA_HD
    rows = seq // GRID_W
    assert nb < SUBLANES and rows % NA_ROWS == 0 and rows >= NA_KEY_ROWS

    lbs = jnp.cumsum(jax.nn.softmax(hgrn_lb.astype(F32), axis=0), axis=0)
    lbs = lbs - lbs[:1]

    cond = jnp.concatenate([c, c_ctx[None, :], jnp.zeros((SUBLANES - nb - 1, d), F32)], axis=0)
    mod = _ada_mod(cond, w_ada, b_ada)

    xs = jnp.concatenate([x.reshape(lat_rows, d), ctx.reshape(nb * ctx_len, d)], axis=0)
    w_in_b = w_in.astype(BF16)
    w_out_b = w_out.astype(BF16)
    for l in range(depth):
        with_ctx = l < depth - 1
        n_act = n_all if with_ctx else lat_rows
        u = _norm_in(xs, g_mix[l], mod, w_in_b, l, nb, lat_rows)

        conv = _conv_module(u, conv_w[l], conv_b[l], conv_ln_g[l], conv_ln_b[l], n_act, lat_rows, seq)
        na = _na_latent(u, _na_bias(na_rpb[l], rows), nb, seq, lat_rows, off_na)
        if with_ctx:
            na = jnp.concatenate([na, _ctx_attn(u, nb, ctx_len, lat_rows, off_na)], axis=0)
        hg = _hgrn(u, lbs[l], hgrn_norm_g[l], nb, seq, ctx_len, lat_rows, off_hg)

        w_router = jnp.concatenate(
            [w_router_group[l], w_router_expert[l],
             jnp.zeros((d, ROUTER_PAD - N_GROUPS - N_EXPERTS), F32)], axis=1)
        b_router = jnp.concatenate(
            [b_router_group[l], b_router_expert[l],
             jnp.zeros((ROUTER_PAD - N_GROUPS - N_EXPERTS,), F32)]).reshape(1, ROUTER_PAD)
        x_mid, h, route = _out_proj(xs, conv, na, hg, w_out_b, l, mod, g_ffn[l],
                                    w_router, b_router, n_act, nb, lat_rows)

        slot_tok, slot_w, pos, blk_e, nused = _route_meta(route, n_act)
        ys = _moe_experts(jnp.take(h, slot_tok, axis=0), slot_w, blk_e, nused,
                          w_exp_gate, w_exp_up, w_exp_down, l)
        y0 = jnp.take(ys, pos[:, 0], axis=0)
        y1 = jnp.take(ys, pos[:, 1], axis=0)
        xs = _combine(x_mid, y0, y1, mod, l, g_final, n_act, nb, lat_rows, final=not with_ctx)
    return xs.reshape(nb, seq, d)
```

```python
import functools

import numpy as np
import jax
import jax.numpy as jnp
from jax import lax
from jax.experimental import pallas as pl
from jax.experimental.pallas import tpu as pltpu

F32 = jnp.float32
BF16 = jnp.bfloat16

EPS = 1e-6
NEG_INF = -1e30
LOG2_E = 1.4426950408889634

GRID_W = 64
CONV_K = 31
NA_HEADS = 16
NA_HD = 64
NA_KH = 8
NA_KW = 16
HG_HEADS = 4
HG_DK = 128
N_GROUPS = 4
EXP_PER_GROUP = 8
N_EXPERTS = N_GROUPS * EXP_PER_GROUP
TOP_K = 2

LANES = 128
SUBLANES = 8
VMEM_BYTES = 64 * 1024 * 1024

ROW_TILE = 256
IN_TM = 1024
IN_TN = 512
OUT_TM = 512
NA_ROWS = 8
NA_KEY_ROWS = 16
HG_BLOCK = 256
MOE_BM = 256
ROUTER_PAD = LANES
HALO = 16


def _pick_tile(pref, *extents):
    t = pref
    while t > ROW_TILE and any(e % t for e in extents):
        t //= 2
    assert all(e % t == 0 for e in extents)
    return t


def _cparams(sem, vmem_mb):
    return pltpu.CompilerParams(dimension_semantics=sem, vmem_limit_bytes=vmem_mb * 1024 * 1024)


def _dot(a, b):
    return jnp.dot(a, b, preferred_element_type=F32)


def _dot_nt(a, b):
    return lax.dot_general(a, b, (((1,), (1,)), ((), ())), preferred_element_type=F32)


def _dot_tn(a, b):
    return lax.dot_general(a, b, (((0,), (0,)), ((), ())), preferred_element_type=F32)


def _sigmoid(x):
    return 1.0 / (1.0 + jnp.exp(-x))


def _silu(x):
    return x * _sigmoid(x)


def _ada_kernel(c_ref, w_ref, b_ref, o_ref):
    sc = _silu(c_ref[...])
    o_ref[0] = jnp.dot(sc, w_ref[0], precision=lax.Precision.HIGHEST,
                       preferred_element_type=F32) + b_ref[0]


def _ada_mod(cond, w_ada, b_ada):
    depth, d, n = w_ada.shape
    tn = 1024
    return pl.pallas_call(
        _ada_kernel,
        grid=(depth, n // tn),
        in_specs=[
            pl.BlockSpec((SUBLANES, d), lambda l, j: (0, 0)),
            pl.BlockSpec((1, d, tn), lambda l, j: (l, 0, j)),
            pl.BlockSpec((1, 1, tn), lambda l, j: (l, 0, j)),
        ],
        out_specs=pl.BlockSpec((1, SUBLANES, tn), lambda l, j: (l, 0, j)),
        out_shape=jax.ShapeDtypeStruct((depth, SUBLANES, n), F32),
        compiler_params=_cparams(("arbitrary", "arbitrary"), 40),
    )(cond, w_ada, b_ada.reshape(depth, 1, n))


def _rms_mod(x, g, scale, shift):
    y = x * lax.rsqrt(jnp.mean(x * x, axis=-1, keepdims=True) + EPS)
    return (y * g) * (1.0 + scale) + shift


def _norm_in_kernel(x_ref, g_ref, sh_ref, sc_ref, w_ref, o_ref, h_ref, *, tiles_per_batch, nbatch):
    i = pl.program_id(0)

    @pl.when(pl.program_id(1) == 0)
    def _():
        r = jnp.minimum(i // tiles_per_batch, nbatch)
        h = _rms_mod(x_ref[...], g_ref[...], sc_ref[pl.ds(r, 1), :], sh_ref[pl.ds(r, 1), :])
        h_ref[...] = h.astype(BF16)

    o_ref[...] = _dot(h_ref[...], w_ref[...])


def _norm_in(x, g, mod, w_bf16, l, nbatch, lat_rows):
    n, d = x.shape
    nout = w_bf16.shape[2]
    tm, tn = _pick_tile(IN_TM, lat_rows // nbatch, n), IN_TN
    kern = functools.partial(_norm_in_kernel, tiles_per_batch=lat_rows // nbatch // tm, nbatch=nbatch)
    return pl.pallas_call(
        kern,
        grid=(n // tm, nout // tn),
        in_specs=[
            pl.BlockSpec((tm, d), lambda i, j: (i, 0)),
            pl.BlockSpec((1, d), lambda i, j: (0, 0)),
            pl.BlockSpec((None, SUBLANES, d), lambda i, j: (l, 0, 0)),
            pl.BlockSpec((None, SUBLANES, d), lambda i, j: (l, 0, 1)),
            pl.BlockSpec((None, d, tn), lambda i, j: (l, 0, j)),
        ],
        out_specs=pl.BlockSpec((tm, tn), lambda i, j: (i, j)),
        out_shape=jax.ShapeDtypeStruct((n, nout), F32),
        scratch_shapes=[pltpu.VMEM((tm, d), BF16)],
        compiler_params=_cparams(("arbitrary", "arbitrary"), 48),
    )(x, g.reshape(1, d), mod, mod, w_bf16)


def _conv_kernel(ap_ref, gp_ref, a_ref, gt_ref, an_ref, gn_ref, w_ref, b_ref, lg_ref, lb_ref,
                 o_ref, buf_ref, acc_ref, *, lat_tiles, tiles_per_seq):
    i = pl.program_id(0)
    tc, ch = a_ref.shape
    is_lat = i < lat_tiles
    pos = i % tiles_per_seq
    first = jnp.logical_or(jnp.logical_not(is_lat), pos == 0)
    last = jnp.logical_or(jnp.logical_not(is_lat), pos == tiles_per_seq - 1)

    buf_ref[0:HALO] = jnp.where(first, 0.0, ap_ref[...] * _sigmoid(gp_ref[...]))
    buf_ref[HALO:HALO + tc] = a_ref[...] * _sigmoid(gt_ref[...])
    buf_ref[HALO + tc:2 * HALO + tc] = jnp.where(last, 0.0, an_ref[...] * _sigmoid(gn_ref[...]))

    rows = 64
    base = HALO - CONV_K // 2
    for c in range(ch // LANES):
        cs = slice(c * LANES, (c + 1) * LANES)
        for r in range(tc // rows):
            acc = jnp.zeros((rows, LANES), F32)
            for k in range(CONV_K):
                acc = acc + w_ref[k:k + 1, cs] * buf_ref[base + r * rows + k:base + (r + 1) * rows + k, cs]
            acc_ref[r * rows:(r + 1) * rows, cs] = acc

    h = acc_ref[...] + b_ref[...]
    mu = jnp.mean(h, axis=-1, keepdims=True)
    var = jnp.mean(jnp.square(h - mu), axis=-1, keepdims=True)
    y = (h - mu) * lax.rsqrt(var + EPS) * lg_ref[...] + lb_ref[...]
    o_ref[...] = _silu(y).astype(o_ref.dtype)


def _conv_module(u, w_dw, b_dw, ln_g, ln_b, n_rows, lat_rows, seq):
    ch = w_dw.shape[1]
    tc = ROW_TILE
    per = tc // HALO
    nh = u.shape[0] // HALO
    kern = functools.partial(_conv_kernel, lat_tiles=lat_rows // tc, tiles_per_seq=seq // tc)
    prev_map = lambda c: (lambda i: (jnp.maximum(i * per - 1, 0), c))
    next_map = lambda c: (lambda i: (jnp.minimum((i + 1) * per, nh - 1), c))
    vec = lambda a: a.reshape(1, ch)
    return pl.pallas_call(
        kern,
        grid=(n_rows // tc,),
        in_specs=[
            pl.BlockSpec((HALO, ch), prev_map(0)),
            pl.BlockSpec((HALO, ch), prev_map(1)),
            pl.BlockSpec((tc, ch), lambda i: (i, 0)),
            pl.BlockSpec((tc, ch), lambda i: (i, 1)),
            pl.BlockSpec((HALO, ch), next_map(0)),
            pl.BlockSpec((HALO, ch), next_map(1)),
            pl.BlockSpec((CONV_K, ch), lambda i: (0, 0)),
            pl.BlockSpec((1, ch), lambda i: (0, 0)),
            pl.BlockSpec((1, ch), lambda i: (0, 0)),
            pl.BlockSpec((1, ch), lambda i: (0, 0)),
        ],
        out_specs=pl.BlockSpec((tc, ch), lambda i: (i, 0)),
        out_shape=jax.ShapeDtypeStruct((n_rows, ch), BF16),
        scratch_shapes=[pltpu.VMEM((tc + 2 * HALO, ch), F32), pltpu.VMEM((tc, ch), F32)],
        compiler_params=_cparams(("arbitrary",), 16),
    )(u, u, u, u, u, u, w_dw, vec(b_dw), vec(ln_g), vec(ln_b))


def _na_bias_tables(rows):
    groups = rows // NA_ROWS
    reps = [0, min(1, groups - 1), groups - 1]
    out = []
    for g in reps:
        start = int(np.clip(NA_ROWS * g - NA_KH // 2, 0, rows - NA_KEY_ROWS))
        per_row = []
        for i in range(NA_ROWS):
            r = NA_ROWS * g + i
            sr = int(np.clip(r - NA_KH // 2, 0, rows - NA_KH))
            per_row.append((sr - start, sr - r + NA_KH - 1))
        out.append(per_row)
    return out


def _na_bias(rpb, rows):
    nh = rpb.shape[0]
    ndr, ndc = 2 * NA_KH - 1, 2 * NA_KW - 1
    period = 2 * GRID_W - 1
    pad = GRID_W - NA_KW
    vp = jnp.pad(rpb.astype(F32), ((0, 0), (0, 0), (pad, period - ndc - pad)))
    hank = jnp.tile(vp, (1, 1, GRID_W + 1))[:, :, :GRID_W * (period + 1)]
    hank = hank.reshape(nh, ndr, GRID_W, period + 1)[..., :GRID_W]
    toe = hank[:, :, ::-1, :]
    c = np.arange(GRID_W)[:, None]
    j = np.arange(GRID_W)[None, :]
    ws = np.clip(c - NA_KW // 2, 0, GRID_W - NA_KW)
    col_ok = (j >= ws) & (j < ws + NA_KW)
    toe = jnp.where(col_ok[None, None], toe, NEG_INF)
    flat = jnp.transpose(toe, (0, 2, 1, 3)).reshape(nh, GRID_W, ndr * GRID_W)
    nk = NA_KEY_ROWS * GRID_W
    lpad = NA_ROWS * GRID_W
    total = 2 * nk

    def padded(shift):
        return jnp.pad(flat, ((0, 0), (0, 0), (lpad - shift, total - flat.shape[2] - lpad + shift)))

    tab = jnp.stack([padded(0), padded(GRID_W)])
    tab = tab.reshape(2, nh, GRID_W, total // LANES, LANES).transpose(0, 1, 3, 2, 4)
    tables = _na_bias_tables(rows)
    mask = np.full((len(tables), NA_ROWS, nk), NEG_INF, np.float32)
    dvals = []
    for cls, per_row in enumerate(tables):
        dvals.append(per_row[0][1] - per_row[0][0])
        for i, (off, lo) in enumerate(per_row):
            assert lo - off == dvals[-1] - i and -NA_ROWS <= lo - off < NA_ROWS
            mask[cls, i, off * GRID_W:(off + NA_KH) * GRID_W] = 0.0
    assert len({d % 2 for d in dvals}) == 1
    return tab, jnp.asarray(mask), tuple(dvals)


def _softmax_pv(s_parts, v_parts):
    m = functools.reduce(jnp.maximum, [jnp.max(s, axis=-1, keepdims=True) for s in s_parts])
    acc, l = None, None
    for s, v in zip(s_parts, v_parts):
        p = jnp.exp(s - m)
        ps = jnp.sum(p, axis=-1, keepdims=True)
        pv = _dot(p.astype(BF16), v)
        l = ps if l is None else l + ps
        acc = pv if acc is None else acc + pv
    return acc / l


def _na_kernel(q_ref, k0, k1, k2, k3, v0, v1, v2, v3, kc_ref, vc_ref, tab_ref, mask_ref, o_ref, *,
               groups, dvals):
    g = pl.program_id(1)
    dcls = jnp.where(g == 0, dvals[0], jnp.where(g == groups - 1, dvals[2], dvals[1]))
    q2 = q_ref[...] * (NA_HD ** -0.5)
    lane = lax.broadcasted_iota(jnp.int32, q2.shape, 1)
    ks = [k[...].astype(BF16) for k in (k0, k1, k2, k3)]
    vs = [v[...].astype(BF16) for v in (v0, v1, v2, v3)]
    kc = kc_ref[...].astype(BF16)
    vc = vc_ref[...].astype(BF16)
    kw = ks[0].shape[0]
    bpk = kw // LANES
    pad_blocks = NA_ROWS * GRID_W // LANES
    outs = []
    for a in range(2):
        sel = (lane < NA_HD) if a == 0 else (lane >= NA_HD)
        qa = jnp.where(sel, q2, 0.0).astype(BF16)
        s_parts = []
        for m, k in enumerate(ks):
            s = _dot_nt(qa, k)
            row_blocks = []
            for i in range(NA_ROWS):
                copy = (dvals[0] - i) % 2
                first = (dcls - i - copy + 2 * pad_blocks) // 2 + m * bpk
                bias = jnp.concatenate([tab_ref[copy, a, first + t] for t in range(bpk)], axis=1)
                bias = bias + mask_ref[i:i + 1, m * kw:(m + 1) * kw]
                row_blocks.append(s[i * GRID_W:(i + 1) * GRID_W, :] + bias)
            s_parts.append(jnp.concatenate(row_blocks, axis=0))
        s_parts.append(_dot_nt(qa, kc))
        outs.append(_softmax_pv(s_parts, vs + [vc]))
    o_ref[...] = jnp.where(lane < NA_HD, outs[0], outs[1]).astype(o_ref.dtype)


def _na_latent(u, bias, nbatch, seq, lat_rows, off_na):
    tab, mask, dvals = bias
    rows = seq // GRID_W
    groups = rows // NA_ROWS
    nq = NA_ROWS * GRID_W
    kblk = ROW_TILE
    nkb = NA_KEY_ROWS * GRID_W // kblk
    assert nkb == 4
    qcol = off_na // LANES
    heads2 = NA_HEADS * NA_HD // LANES
    kcol, vcol = qcol + heads2, qcol + 2 * heads2
    kb_per_batch = seq // kblk
    kb_per_grow = GRID_W * NA_ROWS // kblk
    lat_kb = lat_rows // kblk

    def kmap(col, m):
        def f(h, g, b):
            st = jnp.clip(g * kb_per_grow - (NA_KH // 2) * GRID_W // kblk, 0, kb_per_batch - nkb)
            return (b * kb_per_batch + st + m, col + h)
        return f

    def mask_map(h, g, b):
        return (jnp.where(g == 0, 0, jnp.where(g == groups - 1, 2, 1)), 0, 0)

    in_specs = [pl.BlockSpec((nq, LANES), lambda h, g, b: (b * groups + g, qcol + h))]
    in_specs += [pl.BlockSpec((kblk, LANES), kmap(kcol, m)) for m in range(nkb)]
    in_specs += [pl.BlockSpec((kblk, LANES), kmap(vcol, m)) for m in range(nkb)]
    in_specs += [pl.BlockSpec((ROW_TILE, LANES), lambda h, g, b: (lat_kb + b, kcol + h)),
                 pl.BlockSpec((ROW_TILE, LANES), lambda h, g, b: (lat_kb + b, vcol + h)),
                 pl.BlockSpec((2, 2) + tab.shape[2:], lambda h, g, b: (0, h, 0, 0, 0)),
                 pl.BlockSpec((None,) + mask.shape[1:], mask_map)]
    return pl.pallas_call(
        functools.partial(_na_kernel, groups=groups, dvals=dvals),
        grid=(heads2, groups, nbatch),
        in_specs=in_specs,
        out_specs=pl.BlockSpec((nq, LANES), lambda h, g, b: (b * groups + g, h)),
        out_shape=jax.ShapeDtypeStruct((lat_rows, NA_HEADS * NA_HD), BF16),
        compiler_params=_cparams(("arbitrary", "arbitrary", "arbitrary"), 40),
    )(*([u] * 11), tab, mask)


def _ctx_attn_kernel(q_ref, k_ref, v_ref, o_ref):
    q2 = q_ref[...] * (NA_HD ** -0.5)
    lane = lax.broadcasted_iota(jnp.int32, q2.shape, 1)
    k = k_ref[...].astype(BF16)
    v = v_ref[...].astype(BF16)
    outs = []
    for a in range(2):
        sel = (lane < NA_HD) if a == 0 else (lane >= NA_HD)
        qa = jnp.where(sel, q2, 0.0).astype(BF16)
        outs.append(_softmax_pv([_dot_nt(qa, k)], [v]))
    o_ref[...] = jnp.where(lane < NA_HD, outs[0], outs[1]).astype(o_ref.dtype)


def _ctx_attn(u, nbatch, ctx_len, lat_rows, off_na):
    assert ctx_len == ROW_TILE
    qcol = off_na // LANES
    heads2 = NA_HEADS * NA_HD // LANES
    base = lat_rows // ROW_TILE
    spec = lambda col: pl.BlockSpec((ROW_TILE, LANES), lambda b, h: (base + b, col + h))
    return pl.pallas_call(
        _ctx_attn_kernel,
        grid=(nbatch, heads2),
        in_specs=[spec(qcol), spec(qcol + heads2), spec(qcol + 2 * heads2)],
        out_specs=pl.BlockSpec((ROW_TILE, LANES), lambda b, h: (b, h)),
        out_shape=jax.ShapeDtypeStruct((nbatch * ctx_len, NA_HEADS * NA_HD), BF16),
        compiler_params=_cparams(("arbitrary", "arbitrary"), 16),
    )(u, u, u)


def _hg_level_map(rev):
    t = np.arange(HG_BLOCK)[:, None]
    s = np.arange(HG_BLOCK)[None, :]
    x = t ^ s
    lvl = np.where(x > 0, np.frexp(np.maximum(x, 1))[1] - 1, -1)
    causal = (s < t) if not rev else (s > t)
    nlev = int(np.log2(HG_BLOCK))
    out = np.where(causal, lvl, -1)
    out = np.where(t == s, nlev, out)
    return out.astype(np.int32)


def _hg_tri(rev):
    t = np.arange(HG_BLOCK)[:, None]
    s = np.arange(HG_BLOCK)[None, :]
    return ((s <= t) if not rev else (s >= t)).astype(np.float32)


def _hg_anchor(b3, m, rev):
    nv = b3.shape[0]
    if m >= SUBLANES:
        w = m // SUBLANES
        b4 = b3.reshape(nv // (2 * w), 2 * w, SUBLANES, LANES)
        a = b4[:, w:w + 1, 0:1, :] if rev else b4[:, w - 1:w, SUBLANES - 1:SUBLANES, :]
        return jnp.broadcast_to(a, b4.shape).reshape(b3.shape)
    sub = lax.broadcasted_iota(jnp.int32, b3.shape, 1)
    out = None
    for g in range(SUBLANES // (2 * m)):
        idx = g * 2 * m + (m if rev else m - 1)
        a = jnp.broadcast_to(b3[:, idx:idx + 1, :], b3.shape)
        out = a if out is None else jnp.where(sub >= g * 2 * m, a, out)
    return out


def _hg_block(q, v, z, alog, clog, oml, tri, lv, st, rev):
    n = q.shape[0]
    nlev = int(np.log2(n))
    q = _silu(q)
    t = jnp.exp(-jnp.abs(z))
    lsig = jnp.minimum(z, 0.0) - jnp.log(1.0 + t)
    cc = clog + lsig
    logf = jnp.maximum(alog, cc) + jnp.log(1.0 + jnp.exp(-jnp.abs(alog - cc)))
    kk = oml * jnp.where(z >= 0.0, t, 1.0) / (1.0 + t)

    hi = logf.astype(BF16)
    r1 = logf - hi.astype(F32)
    mid = r1.astype(BF16)
    lo = (r1 - mid.astype(F32)).astype(BF16)
    b = (_dot(tri, hi) + _dot(tri, mid) + _dot(tri, lo)) * LOG2_E

    b3 = b.reshape(n // SUBLANES, SUBLANES, LANES)
    att = jnp.zeros((n, n), F32)
    for lev in range(nlev):
        anc = _hg_anchor(b3, 1 << lev, rev).reshape(q.shape)
        e = jnp.exp2(-jnp.abs(b - anc))
        att = jnp.where(lv == lev, _dot_nt((q * e).astype(BF16), (kk * e).astype(BF16)), att)
    att = jnp.where(lv == nlev, _dot_nt(q.astype(BF16), kk.astype(BF16)), att)

    vb = v.astype(BF16)
    b_last = b[0:1, :] if rev else b[n - 1:n, :]
    qh = (q * jnp.exp2(b)).astype(BF16)
    o = _dot(att.astype(BF16), vb) + _dot_nt(qh, st.astype(BF16))
    kh = (kk * jnp.exp2(b_last - b)).astype(BF16)
    st_new = st * jnp.exp2(b_last) + _dot_tn(vb, kh)
    return o, st_new


def _hg_fwd_kernel(q_ref, v_ref, z_ref, al_ref, cl_ref, om_ref, tri_ref, lv_ref, o_ref, st_ref):
    @pl.when(pl.program_id(1) == 0)
    def _():
        st_ref[...] = jnp.zeros_like(st_ref)

    tri = tri_ref[...]
    lv = lv_ref[...]
    for h in range(HG_HEADS):
        hs = slice(h * HG_DK, (h + 1) * HG_DK)
        o, st = _hg_block(q_ref[:, hs], v_ref[:, hs], z_ref[:, hs], al_ref[:, hs], cl_ref[:, hs],
                          om_ref[:, hs], tri, lv, st_ref[h], False)
        o_ref[:, hs] = o
        st_ref[h] = st


def _hg_bwd_kernel(q_ref, v_ref, z_ref, g_ref, of_ref, al_ref, cl_ref, om_ref, ng_ref, tri_ref, lv_ref,
                   o_ref, st_ref):
    @pl.when(pl.program_id(1) == 0)
    def _():
        st_ref[...] = jnp.zeros_like(st_ref)

    tri = tri_ref[...]
    lv = lv_ref[...]
    for h in range(HG_HEADS):
        hs = slice(h * HG_DK, (h + 1) * HG_DK)
        o, st = _hg_block(q_ref[:, hs], v_ref[:, hs], z_ref[:, hs], al_ref[:, hs], cl_ref[:, hs],
                          om_ref[:, hs], tri, lv, st_ref[h], True)
        st_ref[h] = st
        t = of_ref[:, hs] + o
        y = t * lax.rsqrt(jnp.mean(t * t, axis=-1, keepdims=True) + EPS)
        o_ref[:, hs] = (y * ng_ref[:, hs] * _silu(g_ref[:, hs])).astype(o_ref.dtype)


def _hgrn(u, lb, norm_g, nbatch, seq, ctx_len, lat_rows, off_hg):
    assert ctx_len == HG_BLOCK
    n = u.shape[0]
    hd = HG_HEADS * HG_DK
    col = off_hg // hd
    per = seq // HG_BLOCK
    lat_blocks = lat_rows // HG_BLOCK
    lbf = lb.astype(F32)
    alog, clog, oml = jnp.log(lbf), jnp.log1p(-lbf), 1.0 - lbf

    def fmap(c):
        return lambda b, j: (jnp.where(j == 0, lat_blocks + b, b * per + j - 1), c)

    def bmap(c):
        return lambda b, j: (jnp.where(j == 0, lat_blocks + b, b * per + per - j), c)

    const = lambda shape: pl.BlockSpec(shape, lambda b, j: (0, 0))
    grid = (nbatch, per + 1)
    vec = lambda a: a.reshape(1, hd)
    o_f = pl.pallas_call(
        _hg_fwd_kernel,
        grid=grid,
        in_specs=[pl.BlockSpec((HG_BLOCK, hd), fmap(col)), pl.BlockSpec((HG_BLOCK, hd), fmap(col + 1)),
                  pl.BlockSpec((HG_BLOCK, hd), fmap(col + 2)),
                  const((1, hd)), const((1, hd)), const((1, hd)),
                  const((HG_BLOCK, HG_BLOCK)), const((HG_BLOCK, HG_BLOCK))],
        out_specs=pl.BlockSpec((HG_BLOCK, hd), fmap(0)),
        out_shape=jax.ShapeDtypeStruct((n, hd), F32),
        scratch_shapes=[pltpu.VMEM((HG_HEADS, HG_DK, HG_DK), F32)],
        compiler_params=_cparams(("arbitrary", "arbitrary"), 32),
    )(u, u, u, vec(alog[0]), vec(clog[0]), vec(oml[0]),
      jnp.asarray(_hg_tri(False), BF16), jnp.asarray(_hg_level_map(False)))
    return pl.pallas_call(
        _hg_bwd_kernel,
        grid=grid,
        in_specs=[pl.BlockSpec((HG_BLOCK, hd), bmap(col)), pl.BlockSpec((HG_BLOCK, hd), bmap(col + 1)),
                  pl.BlockSpec((HG_BLOCK, hd), bmap(col + 3)), pl.BlockSpec((HG_BLOCK, hd), bmap(col + 4)),
                  pl.BlockSpec((HG_BLOCK, hd), bmap(0)),
                  const((1, hd)), const((1, hd)), const((1, hd)), const((1, hd)),
                  const((HG_BLOCK, HG_BLOCK)), const((HG_BLOCK, HG_BLOCK))],
        out_specs=pl.BlockSpec((HG_BLOCK, hd), bmap(0)),
        out_shape=jax.ShapeDtypeStruct((n, hd), BF16),
        scratch_shapes=[pltpu.VMEM((HG_HEADS, HG_DK, HG_DK), F32)],
        compiler_params=_cparams(("arbitrary", "arbitrary"), 32),
    )(u, u, u, u, o_f, vec(alog[1]), vec(clog[1]), vec(oml[1]), vec(norm_g.astype(F32)),
      jnp.asarray(_hg_tri(True), BF16), jnp.asarray(_hg_level_map(True)))


def _out_kernel(x_ref, cv_ref, na_ref, hg_ref, w_ref, ga_ref, g2_ref, sh2_ref, s2_ref, wrh_ref, wrl_ref, br_ref,
                xo_ref, h_ref, rt_ref, *, tiles_per_batch, nbatch):
    r = jnp.minimum(pl.program_id(0) // tiles_per_batch, nbatch)
    c0 = cv_ref.shape[1]
    c1 = c0 + na_ref.shape[1]
    mix = (_dot(cv_ref[...], w_ref[0:c0, :]) + _dot(na_ref[...], w_ref[c0:c1, :])
           + _dot(hg_ref[...], w_ref[c1:, :]))
    xn = x_ref[...] + ga_ref[pl.ds(r, 1), :] * mix
    xo_ref[...] = xn
    h = _rms_mod(xn, g2_ref[...], s2_ref[pl.ds(r, 1), :], sh2_ref[pl.ds(r, 1), :])
    h_ref[...] = h.astype(h_ref.dtype)
    h_hi = h.astype(BF16)
    h_lo = (h - h_hi.astype(F32)).astype(BF16)
    logits = (_dot(h_hi, wrh_ref[...]) + (_dot(h_lo, wrh_ref[...]) + _dot(h_hi, wrl_ref[...]))
              + br_ref[...])
    rt_ref[...] = _route_rows(logits)


def _route_rows(lg):
    lane = lax.broadcasted_iota(jnp.int32, lg.shape, 1)
    big = jnp.int32(2 ** 30)
    low = jnp.float32(-3e38)

    def first_max(vals, mask):
        m = jnp.max(vals, axis=-1, keepdims=True)
        idx = jnp.min(jnp.where(jnp.logical_and(vals == m, mask), lane, big), axis=-1, keepdims=True)
        return m, idx

    gmask = lane < N_GROUPS
    gl = jnp.where(gmask, lg, low)
    gm, grp = first_max(gl, gmask)
    p_grp = 1.0 / jnp.sum(jnp.where(gmask, jnp.exp(gl - gm), 0.0), axis=-1, keepdims=True)
    lo = N_GROUPS + grp * EXP_PER_GROUP
    emask = jnp.logical_and(lane >= lo, lane < lo + EXP_PER_GROUP)
    el = jnp.where(emask, lg, low)
    m1, i1 = first_max(el, emask)
    emask2 = jnp.logical_and(emask, lane != i1)
    el2 = jnp.where(emask2, lg, low)
    m2, i2 = first_max(el2, emask2)
    t = jnp.exp(m2 - m1)
    w1 = p_grp / (1.0 + t)
    w2 = p_grp * t / (1.0 + t)
    e1 = (i1 - N_GROUPS).astype(F32)
    e2 = (i2 - N_GROUPS).astype(F32)
    return jnp.where(lane == 0, e1, jnp.where(lane == 1, e2, jnp.where(lane == 2, w1,
                     jnp.where(lane == 3, w2, 0.0))))


def _out_proj(x, conv, na, hg, w_bf16, l, mod, g_ffn, w_router, b_router, n_rows, nbatch, lat_rows):
    d = x.shape[1]
    w_router_hi = w_router.astype(BF16)
    w_router_lo = (w_router - w_router_hi.astype(F32)).astype(BF16)
    tm = _pick_tile(OUT_TM, lat_rows // nbatch, n_rows)
    kern = functools.partial(_out_kernel, tiles_per_batch=lat_rows // nbatch // tm, nbatch=nbatch)
    row = lambda w: pl.BlockSpec((tm, w), lambda i: (i, 0))
    const = lambda shape: pl.BlockSpec(shape, lambda i: (0, 0))
    modc = lambda c: pl.BlockSpec((None, SUBLANES, d), lambda i: (l, 0, c))
    return pl.pallas_call(
        kern,
        grid=(n_rows // tm,),
        in_specs=[row(d), row(conv.shape[1]), row(na.shape[1]), row(hg.shape[1]),
                  pl.BlockSpec((None, d, d), lambda i: (l, 0, 0)),
                  modc(2),
                  const((1, d)),
                  modc(3),
                  modc(4),
                  const((d, ROUTER_PAD)), const((d, ROUTER_PAD)), const((1, ROUTER_PAD))],
        out_specs=[row(d), row(d), row(ROUTER_PAD)],
        out_shape=[jax.ShapeDtypeStruct((n_rows, d), F32), jax.ShapeDtypeStruct((n_rows, d), BF16),
                   jax.ShapeDtypeStruct((n_rows, ROUTER_PAD), F32)],
        compiler_params=_cparams(("arbitrary",), 56),
    )(x, conv, na, hg, w_bf16, mod, g_ffn.reshape(1, d), mod, mod, w_router_hi, w_router_lo, b_router)


def _moe_kernel(be_ref, nu_ref, xs_ref, w1_ref, w3_ref, w2_ref, sw_ref, o_ref, w1b, w3b, w2b):
    i = pl.program_id(0)
    e = be_ref[i]
    prev = be_ref[jnp.maximum(i - 1, 0)]

    @pl.when(jnp.logical_or(i == 0, e != prev))
    def _():
        w1b[...] = w1_ref[...].astype(BF16)
        w3b[...] = w3_ref[...].astype(BF16)
        w2b[...] = w2_ref[...].astype(BF16)

    @pl.when(i < nu_ref[0])
    def _():
        x = xs_ref[...]
        a = (_silu(_dot(x, w1b[...])) * _dot(x, w3b[...])).astype(BF16)
        o_ref[...] = (_dot(a, w2b[...]) * sw_ref[...]).astype(o_ref.dtype)

    @pl.when(i >= nu_ref[0])
    def _():
        o_ref[...] = jnp.zeros_like(o_ref)


def _moe_experts(xs, slot_w, blk_e, nused, w1, w3, w2, l):
    p, d = xs.shape
    de = w1.shape[3]
    bm = MOE_BM
    grid_spec = pltpu.PrefetchScalarGridSpec(
        num_scalar_prefetch=2,
        grid=(p // bm,),
        in_specs=[pl.BlockSpec((bm, d), lambda i, be, nu: (i, 0)),
                  pl.BlockSpec((None, None, d, de), lambda i, be, nu: (l, be[i], 0, 0)),
                  pl.BlockSpec((None, None, d, de), lambda i, be, nu: (l, be[i], 0, 0)),
                  pl.BlockSpec((None, None, de, d), lambda i, be, nu: (l, be[i], 0, 0)),
                  pl.BlockSpec((bm, 1), lambda i, be, nu: (i, 0))],
        out_specs=pl.BlockSpec((bm, d), lambda i, be, nu: (i, 0)),
        scratch_shapes=[pltpu.VMEM((d, de), BF16), pltpu.VMEM((d, de), BF16), pltpu.VMEM((de, d), BF16)],
    )
    return pl.pallas_call(
        _moe_kernel,
        grid_spec=grid_spec,
        out_shape=jax.ShapeDtypeStruct((p, d), BF16),
        compiler_params=_cparams(("arbitrary",), 48),
    )(blk_e, nused, xs, w1, w3, w2, slot_w.reshape(p, 1))


def _route_meta(route, n):
    i32 = jnp.int32
    eid = route[:, 0:TOP_K].astype(i32).reshape(-1)
    wt = route[:, TOP_K:2 * TOP_K].reshape(-1)
    a = n * TOP_K
    bm = MOE_BM
    nblk = -(-a // bm) + N_EXPERTS
    p = nblk * bm
    experts = jnp.arange(N_EXPERTS, dtype=i32)[None, :]
    ja = jnp.arange(a, dtype=i32)
    se, order, wsort = lax.sort((eid, ja, wt), num_keys=1, is_stable=True)
    cnt = jnp.sum((eid[:, None] == experts).astype(i32), axis=0)
    pcnt = (cnt + bm - 1) // bm * bm
    pend = jnp.cumsum(pcnt)
    pstart = pend - pcnt
    end = jnp.cumsum(cnt)
    start = end - cnt
    off = pstart - start
    d_off = off - jnp.concatenate([jnp.zeros((1,), i32), off[:-1]])
    dst_sorted = ja + jnp.sum(jnp.where(ja[:, None] >= start[None, :], d_off[None, :], 0), axis=1)
    _, pos = lax.sort((order, dst_sorted), num_keys=1)
    jp = jnp.arange(p, dtype=i32)
    in_or_after = jp[:, None] >= pstart[None, :]
    src = jp - jnp.sum(jnp.where(in_or_after, d_off[None, :], 0), axis=1)
    valid = src < jnp.sum(jnp.where(in_or_after, cnt[None, :], 0), axis=1)
    src = jnp.where(valid, src, 0)
    slot_tok = jnp.take(order, src) // TOP_K
    slot_w = jnp.where(valid, jnp.take(wsort, src), 0.0)
    jb = jnp.arange(nblk, dtype=i32) * bm
    blk_e = jnp.minimum(jnp.sum((jb[:, None] >= pend[None, :]).astype(i32), axis=1), N_EXPERTS - 1)
    nused = (pend[-1:] // bm).astype(i32)
    return slot_tok, slot_w, pos.reshape(n, TOP_K), blk_e, nused


def _combine_kernel(x_ref, y0_ref, y1_ref, ga_ref, gf_ref, o_ref, *, tiles_per_batch, nbatch, final):
    r = jnp.minimum(pl.program_id(0) // tiles_per_batch, nbatch)
    xn = x_ref[...] + ga_ref[pl.ds(r, 1), :] * (y0_ref[...].astype(F32) + y1_ref[...].astype(F32))
    if final:
        xn = xn * lax.rsqrt(jnp.mean(xn * xn, axis=-1, keepdims=True) + EPS) * gf_ref[...]
    o_ref[...] = xn


def _combine(x, y0, y1, mod, l, g_final, n_rows, nbatch, lat_rows, final):
    d = x.shape[1]
    tm = _pick_tile(OUT_TM, lat_rows // nbatch, n_rows)
    kern = functools.partial(_combine_kernel, tiles_per_batch=lat_rows // nbatch // tm, nbatch=nbatch,
                             final=final)
    row = pl.BlockSpec((tm, d), lambda i: (i, 0))
    return pl.pallas_call(
        kern,
        grid=(n_rows // tm,),
        in_specs=[row, row, row, pl.BlockSpec((None, SUBLANES, d), lambda i: (l, 0, 5)),
                  pl.BlockSpec((1, d), lambda i: (0, 0))],
        out_specs=row,
        out_shape=jax.ShapeDtypeStruct((n_rows, d), F32),
        compiler_params=_cparams(("arbitrary",), 40),
    )(x, y0, y1, mod, g_final.reshape(1, d))


def kernel(x, c, ctx, c_ctx, w_ada, b_ada, g_mix, g_ffn, w_in, conv_w, conv_b, conv_ln_g, conv_ln_b,
           na_rpb, hgrn_lb, hgrn_norm_g, w_out, w_router_group, b_router_group, w_router_expert,
           b_router_expert, w_exp_gate, w_exp_up, w_exp_down, g_final):
    nb, seq, d = x.shape
    ctx_len = ctx.shape[1]
    depth = w_ada.shape[0]
    lat_rows = nb * seq
    n_all = lat_rows + nb * ctx_len
    conv_ch = conv_w.shape[2]
    off_na = 2 * conv_ch
    off_hg = off_na + 3 * NA_HEADS * NA_HD
    rows = seq // GRID_W
    assert nb < SUBLANES and rows % NA_ROWS == 0 and rows >= NA_KEY_ROWS

    lbs = jnp.cumsum(jax.nn.softmax(hgrn_lb.astype(F32), axis=0), axis=0)
    lbs = lbs - lbs[:1]

    cond = jnp.concatenate([c, c_ctx[None, :], jnp.zeros((SUBLANES - nb - 1, d), F32)], axis=0)
    mod = _ada_mod(cond, w_ada, b_ada)

    xs = jnp.concatenate([x.reshape(lat_rows, d), ctx.reshape(nb * ctx_len, d)], axis=0)
    w_in_b = w_in.astype(BF16)
    w_out_b = w_out.astype(BF16)
    for l in range(depth):
        with_ctx = l < depth - 1
        n_act = n_all if with_ctx else lat_rows
        u = _norm_in(xs, g_mix[l], mod, w_in_b, l, nb, lat_rows)

        conv = _conv_module(u, conv_w[l], conv_b[l], conv_ln_g[l], conv_ln_b[l], n_act, lat_rows, seq)
        na = _na_latent(u, _na_bias(na_rpb[l], rows), nb, seq, lat_rows, off_na)
        if with_ctx:
            na = jnp.concatenate([na, _ctx_attn(u, nb, ctx_len, lat_rows, off_na)], axis=0)
        hg = _hgrn(u, lbs[l], hgrn_norm_g[l], nb, seq, ctx_len, lat_rows, off_hg)

        w_router = jnp.concatenate(
            [w_router_group[l], w_router_expert[l],
             jnp.zeros((d, ROUTER_PAD - N_GROUPS - N_EXPERTS), F32)], axis=1)
        b_router = jnp.concatenate(
            [b_router_group[l], b_router_expert[l],
             jnp.zeros((ROUTER_PAD - N_GROUPS - N_EXPERTS,), F32)]).reshape(1, ROUTER_PAD)
        x_mid, h, route = _out_proj(xs, conv, na, hg, w_out_b, l, mod, g_ffn[l],
                                    w_router, b_router, n_act, nb, lat_rows)

        slot_tok, slot_w, pos, blk_e, nused = _route_meta(route, n_act)
        ys = _moe_experts(jnp.take(h, slot_tok, axis=0), slot_w, blk_e, nused,
                          w_exp_gate, w_exp_up, w_exp_down, l)
        y0 = jnp.take(ys, pos[:, 0], axis=0)
        y1 = jnp.take(ys, pos[:, 1], axis=0)
        xs = _combine(x_mid, y0, y1, mod, l, g_final, n_act, nb, lat_rows, final=not with_ctx)
    return xs.reshape(nb, seq, d)
```

```python
import functools

import numpy as np
import jax
import jax.numpy as jnp
from jax import lax
from jax.experimental import pallas as pl
from jax.experimental.pallas import tpu as pltpu

F32 = jnp.float32
BF16 = jnp.bfloat16

EPS = 1e-6
NEG_INF = -1e30
LOG2_E = 1.4426950408889634

GRID_W = 64
CONV_K = 31
NA_HEADS = 16
NA_HD = 64
NA_KH = 8
NA_KW = 16
HG_HEADS = 4
HG_DK = 128
N_GROUPS = 4
EXP_PER_GROUP = 8
N_EXPERTS = N_GROUPS * EXP_PER_GROUP
TOP_K = 2

LANES = 128
SUBLANES = 8
VMEM_BYTES = 64 * 1024 * 1024

ROW_TILE = 256
IN_TM = 1024
IN_TN = 512
OUT_TM = 512
NA_ROWS = 8
NA_KEY_ROWS = 16
HG_BLOCK = 256
MOE_BM = 256
ROUTER_PAD = LANES
HALO = 16


def _pick_tile(pref, *extents):
    t = pref
    while t > ROW_TILE and any(e % t for e in extents):
        t //= 2
    assert all(e % t == 0 for e in extents)
    return t


def _cparams(sem, vmem_mb):
    return pltpu.CompilerParams(dimension_semantics=sem, vmem_limit_bytes=vmem_mb * 1024 * 1024)


def _dot(a, b):
    return jnp.dot(a, b, preferred_element_type=F32)


def _dot_nt(a, b):
    return lax.dot_general(a, b, (((1,), (1,)), ((), ())), preferred_element_type=F32)


def _dot_tn(a, b):
    return lax.dot_general(a, b, (((0,), (0,)), ((), ())), preferred_element_type=F32)


def _sigmoid(x):
    return 1.0 / (1.0 + jnp.exp(-x))


def _silu(x):
    return x * _sigmoid(x)


def _ada_kernel(c_ref, w_ref, b_ref, o_ref):
    sc = _silu(c_ref[...])
    o_ref[0] = jnp.dot(sc, w_ref[0], precision=lax.Precision.HIGHEST,
                       preferred_element_type=F32) + b_ref[0]


def _ada_mod(cond, w_ada, b_ada):
    depth, d, n = w_ada.shape
    tn = 1024
    return pl.pallas_call(
        _ada_kernel,
        grid=(depth, n // tn),
        in_specs=[
            pl.BlockSpec((SUBLANES, d), lambda l, j: (0, 0)),
            pl.BlockSpec((1, d, tn), lambda l, j: (l, 0, j)),
            pl.BlockSpec((1, 1, tn), lambda l, j: (l, 0, j)),
        ],
        out_specs=pl.BlockSpec((1, SUBLANES, tn), lambda l, j: (l, 0, j)),
        out_shape=jax.ShapeDtypeStruct((depth, SUBLANES, n), F32),
        compiler_params=_cparams(("arbitrary", "arbitrary"), 40),
    )(cond, w_ada, b_ada.reshape(depth, 1, n))


def _rms_mod(x, g, scale, shift):
    y = x * lax.rsqrt(jnp.mean(x * x, axis=-1, keepdims=True) + EPS)
    return (y * g) * (1.0 + scale) + shift


def _norm_in_kernel(x_ref, g_ref, sh_ref, sc_ref, w_ref, o_ref, h_ref, *, tiles_per_batch, nbatch):
    i = pl.program_id(0)

    @pl.when(pl.program_id(1) == 0)
    def _():
        r = jnp.minimum(i // tiles_per_batch, nbatch)
        h = _rms_mod(x_ref[...], g_ref[...], sc_ref[pl.ds(r, 1), :], sh_ref[pl.ds(r, 1), :])
        h_ref[...] = h.astype(BF16)

    o_ref[...] = _dot(h_ref[...], w_ref[...])


def _norm_in(x, g, mod, w_bf16, l, nbatch, lat_rows):
    n, d = x.shape
    nout = w_bf16.shape[2]
    tm, tn = _pick_tile(IN_TM, lat_rows // nbatch, n), IN_TN
    kern = functools.partial(_norm_in_kernel, tiles_per_batch=lat_rows // nbatch // tm, nbatch=nbatch)
    return pl.pallas_call(
        kern,
        grid=(n // tm, nout // tn),
        in_specs=[
            pl.BlockSpec((tm, d), lambda i, j: (i, 0)),
            pl.BlockSpec((1, d), lambda i, j: (0, 0)),
            pl.BlockSpec((None, SUBLANES, d), lambda i, j: (l, 0, 0)),
            pl.BlockSpec((None, SUBLANES, d), lambda i, j: (l, 0, 1)),
            pl.BlockSpec((None, d, tn), lambda i, j: (l, 0, j)),
        ],
        out_specs=pl.BlockSpec((tm, tn), lambda i, j: (i, j)),
        out_shape=jax.ShapeDtypeStruct((n, nout), F32),
        scratch_shapes=[pltpu.VMEM((tm, d), BF16)],
        compiler_params=_cparams(("arbitrary", "arbitrary"), 48),
    )(x, g.reshape(1, d), mod, mod, w_bf16)


def _conv_kernel(ap_ref, gp_ref, a_ref, gt_ref, an_ref, gn_ref, w_ref, b_ref, lg_ref, lb_ref,
                 o_ref, buf_ref, acc_ref, *, lat_tiles, tiles_per_seq):
    i = pl.program_id(0)
    tc, ch = a_ref.shape
    is_lat = i < lat_tiles
    pos = i % tiles_per_seq
    first = jnp.logical_or(jnp.logical_not(is_lat), pos == 0)
    last = jnp.logical_or(jnp.logical_not(is_lat), pos == tiles_per_seq - 1)

    buf_ref[0:HALO] = jnp.where(first, 0.0, ap_ref[...] * _sigmoid(gp_ref[...]))
    buf_ref[HALO:HALO + tc] = a_ref[...] * _sigmoid(gt_ref[...])
    buf_ref[HALO + tc:2 * HALO + tc] = jnp.where(last, 0.0, an_ref[...] * _sigmoid(gn_ref[...]))

    rows = 64
    base = HALO - CONV_K // 2
    for c in range(ch // LANES):
        cs = slice(c * LANES, (c + 1) * LANES)
        for r in range(tc // rows):
            acc = jnp.zeros((rows, LANES), F32)
            for k in range(CONV_K):
                acc = acc + w_ref[k:k + 1, cs] * buf_ref[base + r * rows + k:base + (r + 1) * rows + k, cs]
            acc_ref[r * rows:(r + 1) * rows, cs] = acc

    h = acc_ref[...] + b_ref[...]
    mu = jnp.mean(h, axis=-1, keepdims=True)
    var = jnp.mean(jnp.square(h - mu), axis=-1, keepdims=True)
    y = (h - mu) * lax.rsqrt(var + EPS) * lg_ref[...] + lb_ref[...]
    o_ref[...] = _silu(y).astype(o_ref.dtype)


def _conv_module(u, w_dw, b_dw, ln_g, ln_b, n_rows, lat_rows, seq):
    ch = w_dw.shape[1]
    tc = ROW_TILE
    per = tc // HALO
    nh = u.shape[0] // HALO
    kern = functools.partial(_conv_kernel, lat_tiles=lat_rows // tc, tiles_per_seq=seq // tc)
    prev_map = lambda c: (lambda i: (jnp.maximum(i * per - 1, 0), c))
    next_map = lambda c: (lambda i: (jnp.minimum((i + 1) * per, nh - 1), c))
    vec = lambda a: a.reshape(1, ch)
    return pl.pallas_call(
        kern,
        grid=(n_rows // tc,),
        in_specs=[
            pl.BlockSpec((HALO, ch), prev_map(0)),
            pl.BlockSpec((HALO, ch), prev_map(1)),
            pl.BlockSpec((tc, ch), lambda i: (i, 0)),
            pl.BlockSpec((tc, ch), lambda i: (i, 1)),
            pl.BlockSpec((HALO, ch), next_map(0)),
            pl.BlockSpec((HALO, ch), next_map(1)),
            pl.BlockSpec((CONV_K, ch), lambda i: (0, 0)),
            pl.BlockSpec((1, ch), lambda i: (0, 0)),
            pl.BlockSpec((1, ch), lambda i: (0, 0)),
            pl.BlockSpec((1, ch), lambda i: (0, 0)),
        ],
        out_specs=pl.BlockSpec((tc, ch), lambda i: (i, 0)),
        out_shape=jax.ShapeDtypeStruct((n_rows, ch), BF16),
        scratch_shapes=[pltpu.VMEM((tc + 2 * HALO, ch), F32), pltpu.VMEM((tc, ch), F32)],
        compiler_params=_cparams(("arbitrary",), 16),
    )(u, u, u, u, u, u, w_dw, vec(b_dw), vec(ln_g), vec(ln_b))


def _na_bias_tables(rows):
    groups = rows // NA_ROWS
    reps = [0, min(1, groups - 1), groups - 1]
    out = []
    for g in reps:
        start = int(np.clip(NA_ROWS * g - NA_KH // 2, 0, rows - NA_KEY_ROWS))
        per_row = []
        for i in range(NA_ROWS):
            r = NA_ROWS * g + i
            sr = int(np.clip(r - NA_KH // 2, 0, rows - NA_KH))
            per_row.append((sr - start, sr - r + NA_KH - 1))
        out.append(per_row)
    return out


def _na_bias(rpb, rows):
    nh = rpb.shape[0]
    ndr, ndc = 2 * NA_KH - 1, 2 * NA_KW - 1
    period = 2 * GRID_W - 1
    pad = GRID_W - NA_KW
    vp = jnp.pad(rpb.astype(F32), ((0, 0), (0, 0), (pad, period - ndc - pad)))
    hank = jnp.tile(vp, (1, 1, GRID_W + 1))[:, :, :GRID_W * (period + 1)]
    hank = hank.reshape(nh, ndr, GRID_W, period + 1)[..., :GRID_W]
    toe = hank[:, :, ::-1, :]
    c = np.arange(GRID_W)[:, None]
    j = np.arange(GRID_W)[None, :]
    ws = np.clip(c - NA_KW // 2, 0, GRID_W - NA_KW)
    col_ok = (j >= ws) & (j < ws + NA_KW)
    toe = jnp.where(col_ok[None, None], toe, NEG_INF)
    flat = jnp.transpose(toe, (0, 2, 1, 3)).reshape(nh, GRID_W, ndr * GRID_W)
    nk = NA_KEY_ROWS * GRID_W
    lpad = NA_ROWS * GRID_W
    total = 2 * nk

    def padded(shift):
        return jnp.pad(flat, ((0, 0), (0, 0), (lpad - shift, total - flat.shape[2] - lpad + shift)))

    tab = jnp.stack([padded(0), padded(GRID_W)])
    tab = tab.reshape(2, nh, GRID_W, total // LANES, LANES).transpose(0, 1, 3, 2, 4)
    tables = _na_bias_tables(rows)
    mask = np.full((len(tables), NA_ROWS, nk), NEG_INF, np.float32)
    dvals = []
    for cls, per_row in enumerate(tables):
        dvals.append(per_row[0][1] - per_row[0][0])
        for i, (off, lo) in enumerate(per_row):
            assert lo - off == dvals[-1] - i and -NA_ROWS <= lo - off < NA_ROWS
            mask[cls, i, off * GRID_W:(off + NA_KH) * GRID_W] = 0.0
    assert len({d % 2 for d in dvals}) == 1
    return tab, jnp.asarray(mask), tuple(dvals)


def _softmax_pv(s_parts, v_parts):
    m = functools.reduce(jnp.maximum, [jnp.max(s, axis=-1, keepdims=True) for s in s_parts])
    acc, l = None, None
    for s, v in zip(s_parts, v_parts):
        p = jnp.exp(s - m)
        ps = jnp.sum(p, axis=-1, keepdims=True)
        pv = _dot(p.astype(BF16), v)
        l = ps if l is None else l + ps
        acc = pv if acc is None else acc + pv
    return acc / l


def _na_kernel(q_ref, k0, k1, k2, k3, v0, v1, v2, v3, kc_ref, vc_ref, tab_ref, mask_ref, o_ref, *,
               groups, dvals):
    g = pl.program_id(1)
    dcls = jnp.where(g == 0, dvals[0], jnp.where(g == groups - 1, dvals[2], dvals[1]))
    q2 = q_ref[...] * (NA_HD ** -0.5)
    lane = lax.broadcasted_iota(jnp.int32, q2.shape, 1)
    ks = [k[...].astype(BF16) for k in (k0, k1, k2, k3)]
    vs = [v[...].astype(BF16) for v in (v0, v1, v2, v3)]
    kc = kc_ref[...].astype(BF16)
    vc = vc_ref[...].astype(BF16)
    kw = ks[0].shape[0]
    bpk = kw // LANES
    pad_blocks = NA_ROWS * GRID_W // LANES
    outs = []
    for a in range(2):
        sel = (lane < NA_HD) if a == 0 else (lane >= NA_HD)
        qa = jnp.where(sel, q2, 0.0).astype(BF16)
        s_parts = []
        for m, k in enumerate(ks):
            s = _dot_nt(qa, k)
            row_blocks = []
            for i in range(NA_ROWS):
                copy = (dvals[0] - i) % 2
                first = (dcls - i - copy + 2 * pad_blocks) // 2 + m * bpk
                bias = jnp.concatenate([tab_ref[copy, a, first + t] for t in range(bpk)], axis=1)
                bias = bias + mask_ref[i:i + 1, m * kw:(m + 1) * kw]
                row_blocks.append(s[i * GRID_W:(i + 1) * GRID_W, :] + bias)
            s_parts.append(jnp.concatenate(row_blocks, axis=0))
        s_parts.append(_dot_nt(qa, kc))
        outs.append(_softmax_pv(s_parts, vs + [vc]))
    o_ref[...] = jnp.where(lane < NA_HD, outs[0], outs[1]).astype(o_ref.dtype)


def _na_latent(u, bias, nbatch, seq, lat_rows, off_na):
    tab, mask, dvals = bias
    rows = seq // GRID_W
    groups = rows // NA_ROWS
    nq = NA_ROWS * GRID_W
    kblk = ROW_TILE
    nkb = NA_KEY_ROWS * GRID_W // kblk
    assert nkb == 4
    qcol = off_na // LANES
    heads2 = NA_HEADS * NA_HD // LANES
    kcol, vcol = qcol + heads2, qcol + 2 * heads2
    kb_per_batch = seq // kblk
    kb_per_grow = GRID_W * NA_ROWS // kblk
    lat_kb = lat_rows // kblk

    def kmap(col, m):
        def f(h, g, b):
            st = jnp.clip(g * kb_per_grow - (NA_KH // 2) * GRID_W // kblk, 0, kb_per_batch - nkb)
            return (b * kb_per_batch + st + m, col + h)
        return f

    def mask_map(h, g, b):
        return (jnp.where(g == 0, 0, jnp.where(g == groups - 1, 2, 1)), 0, 0)

    in_specs = [pl.BlockSpec((nq, LANES), lambda h, g, b: (b * groups + g, qcol + h))]
    in_specs += [pl.BlockSpec((kblk, LANES), kmap(kcol, m)) for m in range(nkb)]
    in_specs += [pl.BlockSpec((kblk, LANES), kmap(vcol, m)) for m in range(nkb)]
    in_specs += [pl.BlockSpec((ROW_TILE, LANES), lambda h, g, b: (lat_kb + b, kcol + h)),
                 pl.BlockSpec((ROW_TILE, LANES), lambda h, g, b: (lat_kb + b, vcol + h)),
                 pl.BlockSpec((2, 2) + tab.shape[2:], lambda h, g, b: (0, h, 0, 0, 0)),
                 pl.BlockSpec((None,) + mask.shape[1:], mask_map)]
    return pl.pallas_call(
        functools.partial(_na_kernel, groups=groups, dvals=dvals),
        grid=(heads2, groups, nbatch),
        in_specs=in_specs,
        out_specs=pl.BlockSpec((nq, LANES), lambda h, g, b: (b * groups + g, h)),
        out_shape=jax.ShapeDtypeStruct((lat_rows, NA_HEADS * NA_HD), BF16),
        compiler_params=_cparams(("arbitrary", "arbitrary", "arbitrary"), 40),
    )(*([u] * 11), tab, mask)


def _ctx_attn_kernel(q_ref, k_ref, v_ref, o_ref):
    q2 = q_ref[...] * (NA_HD ** -0.5)
    lane = lax.broadcasted_iota(jnp.int32, q2.shape, 1)
    k = k_ref[...].astype(BF16)
    v = v_ref[...].astype(BF16)
    outs = []
    for a in range(2):
        sel = (lane < NA_HD) if a == 0 else (lane >= NA_HD)
        qa = jnp.where(sel, q2, 0.0).astype(BF16)
        outs.append(_softmax_pv([_dot_nt(qa, k)], [v]))
    o_ref[...] = jnp.where(lane < NA_HD, outs[0], outs[1]).astype(o_ref.dtype)


def _ctx_attn(u, nbatch, ctx_len, lat_rows, off_na):
    assert ctx_len == ROW_TILE
    qcol = off_na // LANES
    heads2 = NA_HEADS * NA_HD // LANES
    base = lat_rows // ROW_TILE
    spec = lambda col: pl.BlockSpec((ROW_TILE, LANES), lambda b, h: (base + b, col + h))
    return pl.pallas_call(
        _ctx_attn_kernel,
        grid=(nbatch, heads2),
        in_specs=[spec(qcol), spec(qcol + heads2), spec(qcol + 2 * heads2)],
        out_specs=pl.BlockSpec((ROW_TILE, LANES), lambda b, h: (b, h)),
        out_shape=jax.ShapeDtypeStruct((nbatch * ctx_len, NA_HEADS * NA_HD), BF16),
        compiler_params=_cparams(("arbitrary", "arbitrary"), 16),
    )(u, u, u)


def _hg_level_map(rev):
    t = np.arange(HG_BLOCK)[:, None]
    s = np.arange(HG_BLOCK)[None, :]
    x = t ^ s
    lvl = np.where(x > 0, np.frexp(np.maximum(x, 1))[1] - 1, -1)
    causal = (s < t) if not rev else (s > t)
    nlev = int(np.log2(HG_BLOCK))
    out = np.where(causal, lvl, -1)
    out = np.where(t == s, nlev, out)
    return out.astype(np.int32)


def _hg_tri(rev):
    t = np.arange(HG_BLOCK)[:, None]
    s = np.arange(HG_BLOCK)[None, :]
    return ((s <= t) if not rev else (s >= t)).astype(np.float32)


def _hg_anchor(b3, m, rev):
    nv = b3.shape[0]
    if m >= SUBLANES:
        w = m // SUBLANES
        b4 = b3.reshape(nv // (2 * w), 2 * w, SUBLANES, LANES)
        a = b4[:, w:w + 1, 0:1, :] if rev else b4[:, w - 1:w, SUBLANES - 1:SUBLANES, :]
        return jnp.broadcast_to(a, b4.shape).reshape(b3.shape)
    sub = lax.broadcasted_iota(jnp.int32, b3.shape, 1)
    out = None
    for g in range(SUBLANES // (2 * m)):
        idx = g * 2 * m + (m if rev else m - 1)
        a = jnp.broadcast_to(b3[:, idx:idx + 1, :], b3.shape)
        out = a if out is None else jnp.where(sub >= g * 2 * m, a, out)
    return out


def _hg_block(q, v, z, alog, clog, oml, tri, lv, st, rev):
    n = q.shape[0]
    nlev = int(np.log2(n))
    q = _silu(q)
    t = jnp.exp(-jnp.abs(z))
    lsig = jnp.minimum(z, 0.0) - jnp.log(1.0 + t)
    cc = clog + lsig
    logf = jnp.maximum(alog, cc) + jnp.log(1.0 + jnp.exp(-jnp.abs(alog - cc)))
    kk = oml * jnp.where(z >= 0.0, t, 1.0) / (1.0 + t)

    hi = logf.astype(BF16)
    r1 = logf - hi.astype(F32)
    mid = r1.astype(BF16)
    lo = (r1 - mid.astype(F32)).astype(BF16)
    b = (_dot(tri, hi) + _dot(tri, mid) + _dot(tri, lo)) * LOG2_E

    b3 = b.reshape(n // SUBLANES, SUBLANES, LANES)
    att = jnp.zeros((n, n), F32)
    for lev in range(nlev):
        anc = _hg_anchor(b3, 1 << lev, rev).reshape(q.shape)
        e = jnp.exp2(-jnp.abs(b - anc))
        att = jnp.where(lv == lev, _dot_nt((q * e).astype(BF16), (kk * e).astype(BF16)), att)
    att = jnp.where(lv == nlev, _dot_nt(q.astype(BF16), kk.astype(BF16)), att)

    vb = v.astype(BF16)
    b_last = b[0:1, :] if rev else b[n - 1:n, :]
    qh = (q * jnp.exp2(b)).astype(BF16)
    o = _dot(att.astype(BF16), vb) + _dot_nt(qh, st.astype(BF16))
    kh = (kk * jnp.exp2(b_last - b)).astype(BF16)
    st_new = st * jnp.exp2(b_last) + _dot_tn(vb, kh)
    return o, st_new


def _hg_fwd_kernel(q_ref, v_ref, z_ref, al_ref, cl_ref, om_ref, tri_ref, lv_ref, o_ref, st_ref):
    @pl.when(pl.program_id(1) == 0)
    def _():
        st_ref[...] = jnp.zeros_like(st_ref)

    tri = tri_ref[...]
    lv = lv_ref[...]
    for h in range(HG_HEADS):
        hs = slice(h * HG_DK, (h + 1) * HG_DK)
        o, st = _hg_block(q_ref[:, hs], v_ref[:, hs], z_ref[:, hs], al_ref[:, hs], cl_ref[:, hs],
                          om_ref[:, hs], tri, lv, st_ref[h], False)
        o_ref[:, hs] = o
        st_ref[h] = st


def _hg_bwd_kernel(q_ref, v_ref, z_ref, g_ref, of_ref, al_ref, cl_ref, om_ref, ng_ref, tri_ref, lv_ref,
                   o_ref, st_ref):
    @pl.when(pl.program_id(1) == 0)
    def _():
        st_ref[...] = jnp.zeros_like(st_ref)

    tri = tri_ref[...]
    lv = lv_ref[...]
    for h in range(HG_HEADS):
        hs = slice(h * HG_DK, (h + 1) * HG_DK)
        o, st = _hg_block(q_ref[:, hs], v_ref[:, hs], z_ref[:, hs], al_ref[:, hs], cl_ref[:, hs],
                          om_ref[:, hs], tri, lv, st_ref[h], True)
        st_ref[h] = st
        t = of_ref[:, hs] + o
        y = t * lax.rsqrt(jnp.mean(t * t, axis=-1, keepdims=True) + EPS)
        o_ref[:, hs] = (y * ng_ref[:, hs] * _silu(g_ref[:, hs])).astype(o_ref.dtype)


def _hgrn(u, lb, norm_g, nbatch, seq, ctx_len, lat_rows, off_hg):
    assert ctx_len == HG_BLOCK
    n = u.shape[0]
    hd = HG_HEADS * HG_DK
    col = off_hg // hd
    per = seq // HG_BLOCK
    lat_blocks = lat_rows // HG_BLOCK
    lbf = lb.astype(F32)
    alog, clog, oml = jnp.log(lbf), jnp.log1p(-lbf), 1.0 - lbf

    def fmap(c):
        return lambda b, j: (jnp.where(j == 0, lat_blocks + b, b * per + j - 1), c)

    def bmap(c):
        return lambda b, j: (jnp.where(j == 0, lat_blocks + b, b * per + per - j), c)

    const = lambda shape: pl.BlockSpec(shape, lambda b, j: (0, 0))
    grid = (nbatch, per + 1)
    vec = lambda a: a.reshape(1, hd)
    o_f = pl.pallas_call(
        _hg_fwd_kernel,
        grid=grid,
        in_specs=[pl.BlockSpec((HG_BLOCK, hd), fmap(col)), pl.BlockSpec((HG_BLOCK, hd), fmap(col + 1)),
                  pl.BlockSpec((HG_BLOCK, hd), fmap(col + 2)),
                  const((1, hd)), const((1, hd)), const((1, hd)),
                  const((HG_BLOCK, HG_BLOCK)), const((HG_BLOCK, HG_BLOCK))],
        out_specs=pl.BlockSpec((HG_BLOCK, hd), fmap(0)),
        out_shape=jax.ShapeDtypeStruct((n, hd), F32),
        scratch_shapes=[pltpu.VMEM((HG_HEADS, HG_DK, HG_DK), F32)],
        compiler_params=_cparams(("arbitrary", "arbitrary"), 32),
    )(u, u, u, vec(alog[0]), vec(clog[0]), vec(oml[0]),
      jnp.asarray(_hg_tri(False), BF16), jnp.asarray(_hg_level_map(False)))
    return pl.pallas_call(
        _hg_bwd_kernel,
        grid=grid,
        in_specs=[pl.BlockSpec((HG_BLOCK, hd), bmap(col)), pl.BlockSpec((HG_BLOCK, hd), bmap(col + 1)),
                  pl.BlockSpec((HG_BLOCK, hd), bmap(col + 3)), pl.BlockSpec((HG_BLOCK, hd), bmap(col + 4)),
                  pl.BlockSpec((HG_BLOCK, hd), bmap(0)),
                  const((1, hd)), const((1, hd)), const((1, hd)), const((1, hd)),
                  const((HG_BLOCK, HG_BLOCK)), const((HG_BLOCK, HG_BLOCK))],
        out_specs=pl.BlockSpec((HG_BLOCK, hd), bmap(0)),
        out_shape=jax.ShapeDtypeStruct((n, hd), BF16),
        scratch_shapes=[pltpu.VMEM((HG_HEADS, HG_DK, HG_DK), F32)],
        compiler_params=_cparams(("arbitrary", "arbitrary"), 32),
    )(u, u, u, u, o_f, vec(alog[1]), vec(clog[1]), vec(oml[1]), vec(norm_g.astype(F32)),
      jnp.asarray(_hg_tri(True), BF16), jnp.asarray(_hg_level_map(True)))


def _out_kernel(x_ref, cv_ref, na_ref, hg_ref, w_ref, ga_ref, g2_ref, sh2_ref, s2_ref, wrh_ref, wrl_ref, br_ref,
                xo_ref, h_ref, rt_ref, *, tiles_per_batch, nbatch):
    r = jnp.minimum(pl.program_id(0) // tiles_per_batch, nbatch)
    c0 = cv_ref.shape[1]
    c1 = c0 + na_ref.shape[1]
    mix = (_dot(cv_ref[...], w_ref[0:c0, :]) + _dot(na_ref[...], w_ref[c0:c1, :])
           + _dot(hg_ref[...], w_ref[c1:, :]))
    xn = x_ref[...] + ga_ref[pl.ds(r, 1), :] * mix
    xo_ref[...] = xn
    h = _rms_mod(xn, g2_ref[...], s2_ref[pl.ds(r, 1), :], sh2_ref[pl.ds(r, 1), :])
    h_ref[...] = h.astype(h_ref.dtype)
    h_hi = h.astype(BF16)
    h_lo = (h - h_hi.astype(F32)).astype(BF16)
    logits = (_dot(h_hi, wrh_ref[...]) + (_dot(h_lo, wrh_ref[...]) + _dot(h_hi, wrl_ref[...]))
              + br_ref[...])
    rt_ref[...] = _route_rows(logits)


def _route_rows(lg):
    lane = lax.broadcasted_iota(jnp.int32, lg.shape, 1)
    big = jnp.int32(2 ** 30)
    low = jnp.float32(-3e38)

    def first_max(vals, mask):
        m = jnp.max(vals, axis=-1, keepdims=True)
        idx = jnp.min(jnp.where(jnp.logical_and(vals == m, mask), lane, big), axis=-1, keepdims=True)
        return m, idx

    gmask = lane < N_GROUPS
    gl = jnp.where(gmask, lg, low)
    gm, grp = first_max(gl, gmask)
    p_grp = 1.0 / jnp.sum(jnp.where(gmask, jnp.exp(gl - gm), 0.0), axis=-1, keepdims=True)
    lo = N_GROUPS + grp * EXP_PER_GROUP
    emask = jnp.logical_and(lane >= lo, lane < lo + EXP_PER_GROUP)
    el = jnp.where(emask, lg, low)
    m1, i1 = first_max(el, emask)
    emask2 = jnp.logical_and(emask, lane != i1)
    el2 = jnp.where(emask2, lg, low)
    m2, i2 = first_max(el2, emask2)
    t = jnp.exp(m2 - m1)
    w1 = p_grp / (1.0 + t)
    w2 = p_grp * t / (1.0 + t)
    e1 = (i1 - N_GROUPS).astype(F32)
    e2 = (i2 - N_GROUPS).astype(F32)
    return jnp.where(lane == 0, e1, jnp.where(lane == 1, e2, jnp.where(lane == 2, w1,
                     jnp.where(lane == 3, w2, 0.0))))


def _out_proj(x, conv, na, hg, w_bf16, l, mod, g_ffn, w_router, b_router, n_rows, nbatch, lat_rows):
    d = x.shape[1]
    w_router_hi = w_router.astype(BF16)
    w_router_lo = (w_router - w_router_hi.astype(F32)).astype(BF16)
    tm = _pick_tile(OUT_TM, lat_rows // nbatch, n_rows)
    kern = functools.partial(_out_kernel, tiles_per_batch=lat_rows // nbatch // tm, nbatch=nbatch)
    row = lambda w: pl.BlockSpec((tm, w), lambda i: (i, 0))
    const = lambda shape: pl.BlockSpec(shape, lambda i: (0, 0))
    modc = lambda c: pl.BlockSpec((None, SUBLANES, d), lambda i: (l, 0, c))
    return pl.pallas_call(
        kern,
        grid=(n_rows // tm,),
        in_specs=[row(d), row(conv.shape[1]), row(na.shape[1]), row(hg.shape[1]),
                  pl.BlockSpec((None, d, d), lambda i: (l, 0, 0)),
                  modc(2),
                  const((1, d)),
                  modc(3),
                  modc(4),
                  const((d, ROUTER_PAD)), const((d, ROUTER_PAD)), const((1, ROUTER_PAD))],
        out_specs=[row(d), row(d), row(ROUTER_PAD)],
        out_shape=[jax.ShapeDtypeStruct((n_rows, d), F32), jax.ShapeDtypeStruct((n_rows, d), BF16),
                   jax.ShapeDtypeStruct((n_rows, ROUTER_PAD), F32)],
        compiler_params=_cparams(("arbitrary",), 56),
    )(x, conv, na, hg, w_bf16, mod, g_ffn.reshape(1, d), mod, mod, w_router_hi, w_router_lo, b_router)


def _moe_kernel(be_ref, nu_ref, xs_ref, w1_ref, w3_ref, w2_ref, sw_ref, o_ref, w1b, w3b, w2b):
    i = pl.program_id(0)
    e = be_ref[i]
    prev = be_ref[jnp.maximum(i - 1, 0)]

    @pl.when(jnp.logical_or(i == 0, e != prev))
    def _():
        w1b[...] = w1_ref[...].astype(BF16)
        w3b[...] = w3_ref[...].astype(BF16)
        w2b[...] = w2_ref[...].astype(BF16)

    @pl.when(i < nu_ref[0])
    def _():
        x = xs_ref[...]
        a = (_silu(_dot(x, w1b[...])) * _dot(x, w3b[...])).astype(BF16)
        o_ref[...] = (_dot(a, w2b[...]) * sw_ref[...]).astype(o_ref.dtype)

    @pl.when(i >= nu_ref[0])
    def _():
        o_ref[...] = jnp.zeros_like(o_ref)


def _moe_experts(xs, slot_w, blk_e, nused, w1, w3, w2, l):
    p, d = xs.shape
    de = w1.shape[3]
    bm = MOE_BM
    grid_spec = pltpu.PrefetchScalarGridSpec(
        num_scalar_prefetch=2,
        grid=(p // bm,),
        in_specs=[pl.BlockSpec((bm, d), lambda i, be, nu: (i, 0)),
                  pl.BlockSpec((None, None, d, de), lambda i, be, nu: (l, be[i], 0, 0)),
                  pl.BlockSpec((None, None, d, de), lambda i, be, nu: (l, be[i], 0, 0)),
                  pl.BlockSpec((None, None, de, d), lambda i, be, nu: (l, be[i], 0, 0)),
                  pl.BlockSpec((bm, 1), lambda i, be, nu: (i, 0))],
        out_specs=pl.BlockSpec((bm, d), lambda i, be, nu: (i, 0)),
        scratch_shapes=[pltpu.VMEM((d, de), BF16), pltpu.VMEM((d, de), BF16), pltpu.VMEM((de, d), BF16)],
    )
    return pl.pallas_call(
        _moe_kernel,
        grid_spec=grid_spec,
        out_shape=jax.ShapeDtypeStruct((p, d), BF16),
        compiler_params=_cparams(("arbitrary",), 48),
    )(blk_e, nused, xs, w1, w3, w2, slot_w.reshape(p, 1))


def _rows(a, idx):
    return a.at[idx].get(mode="promise_in_bounds")


def _route_meta(route, n):
    i32 = jnp.int32
    eid = route[:, 0:TOP_K].astype(i32).reshape(-1)
    wt = route[:, TOP_K:2 * TOP_K].reshape(-1)
    a = n * TOP_K
    bm = MOE_BM
    nblk = -(-a // bm) + N_EXPERTS
    p = nblk * bm
    experts = jnp.arange(N_EXPERTS, dtype=i32)[None, :]
    ja = jnp.arange(a, dtype=i32)
    se, order, wsort = lax.sort((eid, ja, wt), num_keys=1, is_stable=True)
    cnt = jnp.sum((eid[:, None] == experts).astype(i32), axis=0)
    pcnt = (cnt + bm - 1) // bm * bm
    pend = jnp.cumsum(pcnt)
    pstart = pend - pcnt
    end = jnp.cumsum(cnt)
    start = end - cnt
    off = pstart - start
    d_off = off - jnp.concatenate([jnp.zeros((1,), i32), off[:-1]])
    dst_sorted = ja + jnp.sum(jnp.where(ja[:, None] >= start[None, :], d_off[None, :], 0), axis=1)
    _, pos = lax.sort((order, dst_sorted), num_keys=1)
    jp = jnp.arange(p, dtype=i32)
    in_or_after = jp[:, None] >= pstart[None, :]
    src = jp - jnp.sum(jnp.where(in_or_after, d_off[None, :], 0), axis=1)
    valid = src < jnp.sum(jnp.where(in_or_after, cnt[None, :], 0), axis=1)
    src = jnp.where(valid, src, jp % a)
    slot_tok = _rows(order, src) // TOP_K
    slot_w = jnp.where(valid, _rows(wsort, src), 0.0)
    jb = jnp.arange(nblk, dtype=i32) * bm
    blk_e = jnp.minimum(jnp.sum((jb[:, None] >= pend[None, :]).astype(i32), axis=1), N_EXPERTS - 1)
    nused = (pend[-1:] // bm).astype(i32)
    return slot_tok, slot_w, pos.reshape(n, TOP_K), blk_e, nused


def _combine_kernel(x_ref, y0_ref, y1_ref, ga_ref, gf_ref, o_ref, *, tiles_per_batch, nbatch, final):
    r = jnp.minimum(pl.program_id(0) // tiles_per_batch, nbatch)
    xn = x_ref[...] + ga_ref[pl.ds(r, 1), :] * (y0_ref[...].astype(F32) + y1_ref[...].astype(F32))
    if final:
        xn = xn * lax.rsqrt(jnp.mean(xn * xn, axis=-1, keepdims=True) + EPS) * gf_ref[...]
    o_ref[...] = xn


def _combine(x, y0, y1, mod, l, g_final, n_rows, nbatch, lat_rows, final):
    d = x.shape[1]
    tm = _pick_tile(OUT_TM, lat_rows // nbatch, n_rows)
    kern = functools.partial(_combine_kernel, tiles_per_batch=lat_rows // nbatch // tm, nbatch=nbatch,
                             final=final)
    row = pl.BlockSpec((tm, d), lambda i: (i, 0))
    return pl.pallas_call(
        kern,
        grid=(n_rows // tm,),
        in_specs=[row, row, row, pl.BlockSpec((None, SUBLANES, d), lambda i: (l, 0, 5)),
                  pl.BlockSpec((1, d), lambda i: (0, 0))],
        out_specs=row,
        out_shape=jax.ShapeDtypeStruct((n_rows, d), F32),
        compiler_params=_cparams(("arbitrary",), 40),
    )(x, y0, y1, mod, g_final.reshape(1, d))


def kernel(x, c, ctx, c_ctx, w_ada, b_ada, g_mix, g_ffn, w_in, conv_w, conv_b, conv_ln_g, conv_ln_b,
           na_rpb, hgrn_lb, hgrn_norm_g, w_out, w_router_group, b_router_group, w_router_expert,
           b_router_expert, w_exp_gate, w_exp_up, w_exp_down, g_final):
    nb, seq, d = x.shape
    ctx_len = ctx.shape[1]
    depth = w_ada.shape[0]
    lat_rows = nb * seq
    n_all = lat_rows + nb * ctx_len
    conv_ch = conv_w.shape[2]
    off_na = 2 * conv_ch
    off_hg = off_na + 3 * NA_HEADS * NA_HD
    rows = seq // GRID_W
    assert nb < SUBLANES and rows % NA_ROWS == 0 and rows >= NA_KEY_ROWS

    lbs = jnp.cumsum(jax.nn.softmax(hgrn_lb.astype(F32), axis=0), axis=0)
    lbs = lbs - lbs[:1]

    cond = jnp.concatenate([c, c_ctx[None, :], jnp.zeros((SUBLANES - nb - 1, d), F32)], axis=0)
    mod = _ada_mod(cond, w_ada, b_ada)

    xs = jnp.concatenate([x.reshape(lat_rows, d), ctx.reshape(nb * ctx_len, d)], axis=0)
    w_in_b = w_in.astype(BF16)
    w_out_b = w_out.astype(BF16)
    for l in range(depth):
        with_ctx = l < depth - 1
        n_act = n_all if with_ctx else lat_rows
        u = _norm_in(xs, g_mix[l], mod, w_in_b, l, nb, lat_rows)

        conv = _conv_module(u, conv_w[l], conv_b[l], conv_ln_g[l], conv_ln_b[l], n_act, lat_rows, seq)
        na = _na_latent(u, _na_bias(na_rpb[l], rows), nb, seq, lat_rows, off_na)
        if with_ctx:
            na = jnp.concatenate([na, _ctx_attn(u, nb, ctx_len, lat_rows, off_na)], axis=0)
        hg = _hgrn(u, lbs[l], hgrn_norm_g[l], nb, seq, ctx_len, lat_rows, off_hg)

        w_router = jnp.concatenate(
            [w_router_group[l], w_router_expert[l],
             jnp.zeros((d, ROUTER_PAD - N_GROUPS - N_EXPERTS), F32)], axis=1)
        b_router = jnp.concatenate(
            [b_router_group[l], b_router_expert[l],
             jnp.zeros((ROUTER_PAD - N_GROUPS - N_EXPERTS,), F32)]).reshape(1, ROUTER_PAD)
        x_mid, h, route = _out_proj(xs, conv, na, hg, w_out_b, l, mod, g_ffn[l],
                                    w_router, b_router, n_act, nb, lat_rows)

        slot_tok, slot_w, pos, blk_e, nused = _route_meta(route, n_act)
        ys = _moe_experts(_rows(h, slot_tok), slot_w, blk_e, nused, w_exp_gate, w_exp_up, w_exp_down, l)
        y0 = _rows(ys, pos[:, 0])
        y1 = _rows(ys, pos[:, 1])
        xs = _combine(x_mid, y0, y1, mod, l, g_final, n_act, nb, lat_rows, final=not with_ctx)
    return xs.reshape(nb, seq, d)
```

```python
import functools

import numpy as np
import jax
import jax.numpy as jnp
from jax import lax
from jax.experimental import pallas as pl
from jax.experimental.pallas import tpu as pltpu

F32 = jnp.float32
BF16 = jnp.bfloat16

EPS = 1e-6
NEG_INF = -1e30
LOG2_E = 1.4426950408889634

GRID_W = 64
CONV_K = 31
NA_HEADS = 16
NA_HD = 64
NA_KH = 8
NA_KW = 16
HG_HEADS = 4
HG_DK = 128
N_GROUPS = 4
EXP_PER_GROUP = 8
N_EXPERTS = N_GROUPS * EXP_PER_GROUP
TOP_K = 2

LANES = 128
SUBLANES = 8
VMEM_BYTES = 64 * 1024 * 1024

ROW_TILE = 256
IN_TM = 1024
IN_TN = 512
OUT_TM = 512
NA_ROWS = 8
NA_KEY_ROWS = 16
HG_BLOCK = 256
MOE_BM = 256
ROUTER_PAD = LANES
HALO = 16


def _pick_tile(pref, *extents):
    t = pref
    while t > ROW_TILE and any(e % t for e in extents):
        t //= 2
    assert all(e % t == 0 for e in extents)
    return t


def _cparams(sem, vmem_mb):
    return pltpu.CompilerParams(dimension_semantics=sem, vmem_limit_bytes=vmem_mb * 1024 * 1024)


def _dot(a, b):
    return jnp.dot(a, b, preferred_element_type=F32)


def _dot_nt(a, b):
    return lax.dot_general(a, b, (((1,), (1,)), ((), ())), preferred_element_type=F32)


def _dot_tn(a, b):
    return lax.dot_general(a, b, (((0,), (0,)), ((), ())), preferred_element_type=F32)


def _sigmoid(x):
    return 1.0 / (1.0 + jnp.exp(-x))


def _silu(x):
    return x * _sigmoid(x)


def _ada_kernel(c_ref, w_ref, b_ref, o_ref):
    sc = _silu(c_ref[...])
    o_ref[0] = jnp.dot(sc, w_ref[0], precision=lax.Precision.HIGHEST,
                       preferred_element_type=F32) + b_ref[0]


def _ada_mod(cond, w_ada, b_ada):
    depth, d, n = w_ada.shape
    tn = 1024
    return pl.pallas_call(
        _ada_kernel,
        grid=(depth, n // tn),
        in_specs=[
            pl.BlockSpec((SUBLANES, d), lambda l, j: (0, 0)),
            pl.BlockSpec((1, d, tn), lambda l, j: (l, 0, j)),
            pl.BlockSpec((1, 1, tn), lambda l, j: (l, 0, j)),
        ],
        out_specs=pl.BlockSpec((1, SUBLANES, tn), lambda l, j: (l, 0, j)),
        out_shape=jax.ShapeDtypeStruct((depth, SUBLANES, n), F32),
        compiler_params=_cparams(("arbitrary", "arbitrary"), 40),
    )(cond, w_ada, b_ada.reshape(depth, 1, n))


def _rms_mod(x, g, scale, shift):
    y = x * lax.rsqrt(jnp.mean(x * x, axis=-1, keepdims=True) + EPS)
    return (y * g) * (1.0 + scale) + shift


def _norm_in_kernel(x_ref, g_ref, sh_ref, sc_ref, w_ref, of_ref, ob_ref, h_ref, *, tiles_per_batch, nbatch,
                    att_lo, att_hi):
    i = pl.program_id(0)
    j = pl.program_id(1)

    @pl.when(j == 0)
    def _():
        r = jnp.minimum(i // tiles_per_batch, nbatch)
        h = _rms_mod(x_ref[...], g_ref[...], sc_ref[pl.ds(r, 1), :], sh_ref[pl.ds(r, 1), :])
        h_ref[...] = h.astype(BF16)

    res = _dot(h_ref[...], w_ref[...])
    is_att = jnp.logical_and(j >= att_lo, j < att_hi)

    @pl.when(is_att)
    def _():
        ob_ref[...] = res.astype(BF16)

    @pl.when(jnp.logical_not(is_att))
    def _():
        of_ref[...] = res


def _norm_in(x, g, mod, w_bf16, l, nbatch, lat_rows, off_na, off_hg):
    n, d = x.shape
    nout = w_bf16.shape[2]
    tm, tn = _pick_tile(IN_TM, lat_rows // nbatch, n), IN_TN
    att_lo, att_hi = off_na // tn, off_hg // tn
    n_att = att_hi - att_lo
    assert off_na % tn == 0 and off_hg % tn == 0 and att_lo >= 1
    kern = functools.partial(_norm_in_kernel, tiles_per_batch=lat_rows // nbatch // tm, nbatch=nbatch,
                             att_lo=att_lo, att_hi=att_hi)
    return pl.pallas_call(
        kern,
        grid=(n // tm, nout // tn),
        in_specs=[
            pl.BlockSpec((tm, d), lambda i, j: (i, 0)),
            pl.BlockSpec((1, d), lambda i, j: (0, 0)),
            pl.BlockSpec((None, SUBLANES, d), lambda i, j: (l, 0, 0)),
            pl.BlockSpec((None, SUBLANES, d), lambda i, j: (l, 0, 1)),
            pl.BlockSpec((None, d, tn), lambda i, j: (l, 0, j)),
        ],
        out_specs=[
            pl.BlockSpec((tm, tn), lambda i, j: (i, jnp.where(j < att_lo, j, jnp.maximum(j - n_att, att_lo - 1)))),
            pl.BlockSpec((tm, tn), lambda i, j: (i, jnp.clip(j - att_lo, 0, n_att - 1))),
        ],
        out_shape=[jax.ShapeDtypeStruct((n, nout - n_att * tn), F32),
                   jax.ShapeDtypeStruct((n, n_att * tn), BF16)],
        scratch_shapes=[pltpu.VMEM((tm, d), BF16)],
        compiler_params=_cparams(("arbitrary", "arbitrary"), 48),
    )(x, g.reshape(1, d), mod, mod, w_bf16)


def _conv_kernel(ap_ref, gp_ref, a_ref, gt_ref, an_ref, gn_ref, w_ref, b_ref, lg_ref, lb_ref,
                 o_ref, buf_ref, acc_ref, *, lat_tiles, tiles_per_seq):
    i = pl.program_id(0)
    tc, ch = a_ref.shape
    is_lat = i < lat_tiles
    pos = i % tiles_per_seq
    first = jnp.logical_or(jnp.logical_not(is_lat), pos == 0)
    last = jnp.logical_or(jnp.logical_not(is_lat), pos == tiles_per_seq - 1)

    buf_ref[0:HALO] = jnp.where(first, 0.0, ap_ref[...] * _sigmoid(gp_ref[...]))
    buf_ref[HALO:HALO + tc] = a_ref[...] * _sigmoid(gt_ref[...])
    buf_ref[HALO + tc:2 * HALO + tc] = jnp.where(last, 0.0, an_ref[...] * _sigmoid(gn_ref[...]))

    rows = 64
    base = HALO - CONV_K // 2
    for c in range(ch // LANES):
        cs = slice(c * LANES, (c + 1) * LANES)
        for r in range(tc // rows):
            acc = jnp.zeros((rows, LANES), F32)
            for k in range(CONV_K):
                acc = acc + w_ref[k:k + 1, cs] * buf_ref[base + r * rows + k:base + (r + 1) * rows + k, cs]
            acc_ref[r * rows:(r + 1) * rows, cs] = acc

    h = acc_ref[...] + b_ref[...]
    mu = jnp.mean(h, axis=-1, keepdims=True)
    var = jnp.mean(jnp.square(h - mu), axis=-1, keepdims=True)
    y = (h - mu) * lax.rsqrt(var + EPS) * lg_ref[...] + lb_ref[...]
    o_ref[...] = _silu(y).astype(o_ref.dtype)


def _conv_module(u, w_dw, b_dw, ln_g, ln_b, n_rows, lat_rows, seq):
    ch = w_dw.shape[1]
    tc = ROW_TILE
    per = tc // HALO
    nh = u.shape[0] // HALO
    kern = functools.partial(_conv_kernel, lat_tiles=lat_rows // tc, tiles_per_seq=seq // tc)
    prev_map = lambda c: (lambda i: (jnp.maximum(i * per - 1, 0), c))
    next_map = lambda c: (lambda i: (jnp.minimum((i + 1) * per, nh - 1), c))
    vec = lambda a: a.reshape(1, ch)
    return pl.pallas_call(
        kern,
        grid=(n_rows // tc,),
        in_specs=[
            pl.BlockSpec((HALO, ch), prev_map(0)),
            pl.BlockSpec((HALO, ch), prev_map(1)),
            pl.BlockSpec((tc, ch), lambda i: (i, 0)),
            pl.BlockSpec((tc, ch), lambda i: (i, 1)),
            pl.BlockSpec((HALO, ch), next_map(0)),
            pl.BlockSpec((HALO, ch), next_map(1)),
            pl.BlockSpec((CONV_K, ch), lambda i: (0, 0)),
            pl.BlockSpec((1, ch), lambda i: (0, 0)),
            pl.BlockSpec((1, ch), lambda i: (0, 0)),
            pl.BlockSpec((1, ch), lambda i: (0, 0)),
        ],
        out_specs=pl.BlockSpec((tc, ch), lambda i: (i, 0)),
        out_shape=jax.ShapeDtypeStruct((n_rows, ch), BF16),
        scratch_shapes=[pltpu.VMEM((tc + 2 * HALO, ch), F32), pltpu.VMEM((tc, ch), F32)],
        compiler_params=_cparams(("arbitrary",), 16),
    )(u, u, u, u, u, u, w_dw, vec(b_dw), vec(ln_g), vec(ln_b))


def _na_bias_tables(rows):
    groups = rows // NA_ROWS
    reps = [0, min(1, groups - 1), groups - 1]
    out = []
    for g in reps:
        start = int(np.clip(NA_ROWS * g - NA_KH // 2, 0, rows - NA_KEY_ROWS))
        per_row = []
        for i in range(NA_ROWS):
            r = NA_ROWS * g + i
            sr = int(np.clip(r - NA_KH // 2, 0, rows - NA_KH))
            per_row.append((sr - start, sr - r + NA_KH - 1))
        out.append(per_row)
    return out


def _na_bias(rpb, rows):
    nh = rpb.shape[0]
    ndr, ndc = 2 * NA_KH - 1, 2 * NA_KW - 1
    period = 2 * GRID_W - 1
    pad = GRID_W - NA_KW
    vp = jnp.pad(rpb.astype(F32), ((0, 0), (0, 0), (pad, period - ndc - pad)))
    hank = jnp.tile(vp, (1, 1, GRID_W + 1))[:, :, :GRID_W * (period + 1)]
    hank = hank.reshape(nh, ndr, GRID_W, period + 1)[..., :GRID_W]
    toe = hank[:, :, ::-1, :]
    c = np.arange(GRID_W)[:, None]
    j = np.arange(GRID_W)[None, :]
    ws = np.clip(c - NA_KW // 2, 0, GRID_W - NA_KW)
    col_ok = (j >= ws) & (j < ws + NA_KW)
    toe = jnp.where(col_ok[None, None], toe, NEG_INF)
    flat = jnp.transpose(toe, (0, 2, 1, 3)).reshape(nh, GRID_W, ndr * GRID_W)
    nk = NA_KEY_ROWS * GRID_W
    lpad = NA_ROWS * GRID_W
    total = 2 * nk

    def padded(shift):
        return jnp.pad(flat, ((0, 0), (0, 0), (lpad - shift, total - flat.shape[2] - lpad + shift)))

    tab = jnp.stack([padded(0), padded(GRID_W)])
    tab = tab.reshape(2, nh, GRID_W, total // LANES, LANES).transpose(0, 1, 3, 2, 4)
    tables = _na_bias_tables(rows)
    mask = np.full((len(tables), NA_ROWS, nk), NEG_INF, np.float32)
    dvals = []
    for cls, per_row in enumerate(tables):
        dvals.append(per_row[0][1] - per_row[0][0])
        for i, (off, lo) in enumerate(per_row):
            assert lo - off == dvals[-1] - i and -NA_ROWS <= lo - off < NA_ROWS
            mask[cls, i, off * GRID_W:(off + NA_KH) * GRID_W] = 0.0
    assert len({d % 2 for d in dvals}) == 1
    return tab, jnp.asarray(mask), tuple(dvals)


def _softmax_pv(s_parts, v_parts):
    m = functools.reduce(jnp.maximum, [jnp.max(s, axis=-1, keepdims=True) for s in s_parts])
    acc, l = None, None
    for s, v in zip(s_parts, v_parts):
        p = jnp.exp(s - m)
        ps = jnp.sum(p, axis=-1, keepdims=True)
        pv = _dot(p.astype(BF16), v)
        l = ps if l is None else l + ps
        acc = pv if acc is None else acc + pv
    return acc / l


def _na_kernel(q_ref, k0, k1, k2, k3, v0, v1, v2, v3, kc_ref, vc_ref, tab_ref, mask_ref, o_ref, *,
               groups, dvals):
    g = pl.program_id(1)
    dcls = jnp.where(g == 0, dvals[0], jnp.where(g == groups - 1, dvals[2], dvals[1]))
    q2 = q_ref[...] * (NA_HD ** -0.5)
    lane = lax.broadcasted_iota(jnp.int32, q2.shape, 1)
    ks = [k[...] for k in (k0, k1, k2, k3)]
    vs = [v[...] for v in (v0, v1, v2, v3)]
    kc = kc_ref[...]
    vc = vc_ref[...]
    kw = ks[0].shape[0]
    bpk = kw // LANES
    pad_blocks = NA_ROWS * GRID_W // LANES
    outs = []
    for a in range(2):
        sel = (lane < NA_HD) if a == 0 else (lane >= NA_HD)
        qa = jnp.where(sel, q2, jnp.zeros_like(q2))
        s_parts = []
        for m, k in enumerate(ks):
            s = _dot_nt(qa, k)
            row_blocks = []
            for i in range(NA_ROWS):
                copy = (dvals[0] - i) % 2
                first = (dcls - i - copy + 2 * pad_blocks) // 2 + m * bpk
                bias = jnp.concatenate([tab_ref[copy, a, first + t] for t in range(bpk)], axis=1)
                bias = bias + mask_ref[i:i + 1, m * kw:(m + 1) * kw]
                row_blocks.append(s[i * GRID_W:(i + 1) * GRID_W, :] + bias)
            s_parts.append(jnp.concatenate(row_blocks, axis=0))
        s_parts.append(_dot_nt(qa, kc))
        outs.append(_softmax_pv(s_parts, vs + [vc]))
    o_ref[...] = jnp.where(lane < NA_HD, outs[0], outs[1]).astype(o_ref.dtype)


def _na_latent(u, bias, nbatch, seq, lat_rows, off_na):
    tab, mask, dvals = bias
    rows = seq // GRID_W
    groups = rows // NA_ROWS
    nq = NA_ROWS * GRID_W
    kblk = ROW_TILE
    nkb = NA_KEY_ROWS * GRID_W // kblk
    assert nkb == 4
    qcol = off_na // LANES
    heads2 = NA_HEADS * NA_HD // LANES
    kcol, vcol = qcol + heads2, qcol + 2 * heads2
    kb_per_batch = seq // kblk
    kb_per_grow = GRID_W * NA_ROWS // kblk
    lat_kb = lat_rows // kblk

    def kmap(col, m):
        def f(h, g, b):
            st = jnp.clip(g * kb_per_grow - (NA_KH // 2) * GRID_W // kblk, 0, kb_per_batch - nkb)
            return (b * kb_per_batch + st + m, col + h)
        return f

    def mask_map(h, g, b):
        return (jnp.where(g == 0, 0, jnp.where(g == groups - 1, 2, 1)), 0, 0)

    in_specs = [pl.BlockSpec((nq, LANES), lambda h, g, b: (b * groups + g, qcol + h))]
    in_specs += [pl.BlockSpec((kblk, LANES), kmap(kcol, m)) for m in range(nkb)]
    in_specs += [pl.BlockSpec((kblk, LANES), kmap(vcol, m)) for m in range(nkb)]
    in_specs += [pl.BlockSpec((ROW_TILE, LANES), lambda h, g, b: (lat_kb + b, kcol + h)),
                 pl.BlockSpec((ROW_TILE, LANES), lambda h, g, b: (lat_kb + b, vcol + h)),
                 pl.BlockSpec((2, 2) + tab.shape[2:], lambda h, g, b: (0, h, 0, 0, 0)),
                 pl.BlockSpec((None,) + mask.shape[1:], mask_map)]
    return pl.pallas_call(
        functools.partial(_na_kernel, groups=groups, dvals=dvals),
        grid=(heads2, groups, nbatch),
        in_specs=in_specs,
        out_specs=pl.BlockSpec((nq, LANES), lambda h, g, b: (b * groups + g, h)),
        out_shape=jax.ShapeDtypeStruct((lat_rows, NA_HEADS * NA_HD), BF16),
        compiler_params=_cparams(("arbitrary", "arbitrary", "arbitrary"), 40),
    )(*([u] * 11), tab, mask)


def _ctx_attn_kernel(q_ref, k_ref, v_ref, o_ref):
    q2 = q_ref[...] * (NA_HD ** -0.5)
    lane = lax.broadcasted_iota(jnp.int32, q2.shape, 1)
    k = k_ref[...]
    v = v_ref[...]
    outs = []
    for a in range(2):
        sel = (lane < NA_HD) if a == 0 else (lane >= NA_HD)
        qa = jnp.where(sel, q2, jnp.zeros_like(q2))
        outs.append(_softmax_pv([_dot_nt(qa, k)], [v]))
    o_ref[...] = jnp.where(lane < NA_HD, outs[0], outs[1]).astype(o_ref.dtype)


def _ctx_attn(u, nbatch, ctx_len, lat_rows, off_na):
    assert ctx_len == ROW_TILE
    qcol = off_na // LANES
    heads2 = NA_HEADS * NA_HD // LANES
    base = lat_rows // ROW_TILE
    spec = lambda col: pl.BlockSpec((ROW_TILE, LANES), lambda b, h: (base + b, col + h))
    return pl.pallas_call(
        _ctx_attn_kernel,
        grid=(nbatch, heads2),
        in_specs=[spec(qcol), spec(qcol + heads2), spec(qcol + 2 * heads2)],
        out_specs=pl.BlockSpec((ROW_TILE, LANES), lambda b, h: (b, h)),
        out_shape=jax.ShapeDtypeStruct((nbatch * ctx_len, NA_HEADS * NA_HD), BF16),
        compiler_params=_cparams(("arbitrary", "arbitrary"), 16),
    )(u, u, u)


def _hg_level_map(rev):
    size = HG_BLOCK // 2
    t = np.arange(size)[:, None]
    s = np.arange(size)[None, :]
    x = t ^ s
    lvl = np.where(x > 0, np.frexp(np.maximum(x, 1))[1] - 1, -1)
    causal = (s < t) if not rev else (s > t)
    out = np.where(causal, lvl, -1)
    out = np.where(t == s, int(np.log2(size)), out)
    return out.astype(np.int32)


def _hg_tri(rev):
    t = np.arange(HG_BLOCK)[:, None]
    s = np.arange(HG_BLOCK)[None, :]
    return ((s <= t) if not rev else (s >= t)).astype(np.float32)


def _hg_anchor(b3, m, rev):
    nv = b3.shape[0]
    if m >= SUBLANES:
        w = m // SUBLANES
        b4 = b3.reshape(nv // (2 * w), 2 * w, SUBLANES, LANES)
        a = b4[:, w:w + 1, 0:1, :] if rev else b4[:, w - 1:w, SUBLANES - 1:SUBLANES, :]
        return jnp.broadcast_to(a, b4.shape).reshape(b3.shape)
    sub = lax.broadcasted_iota(jnp.int32, b3.shape, 1)
    out = None
    for g in range(SUBLANES // (2 * m)):
        idx = g * 2 * m + (m if rev else m - 1)
        a = jnp.broadcast_to(b3[:, idx:idx + 1, :], b3.shape)
        out = a if out is None else jnp.where(sub >= g * 2 * m, a, out)
    return out


def _hg_pick(q3, k3, m, rev):
    nv = q3.shape[0]
    if m >= SUBLANES:
        w = m // SUBLANES
        shape4 = (nv // (2 * w), 2 * w, SUBLANES, LANES)
        q4, k4 = q3.reshape(shape4), k3.reshape(shape4)
        lower, upper = (q4, k4) if rev else (k4, q4)
        return jnp.concatenate([lower[:, :w], upper[:, w:]], axis=1).reshape(q3.shape)
    upper_rows = (lax.broadcasted_iota(jnp.int32, q3.shape, 1) & m) != 0
    return jnp.where(upper_rows, k3 if rev else q3, q3 if rev else k3)


def _hg_block(q, v, z, alog, clog, oml, tri, lv, st, rev):
    n = q.shape[0]
    half = n // 2
    nlev = int(np.log2(n))
    q = _silu(q)
    t = jnp.exp(-jnp.abs(z))
    lsig = jnp.minimum(z, 0.0) - jnp.log(1.0 + t)
    cc = clog + lsig
    logf = jnp.maximum(alog, cc) + jnp.log(1.0 + jnp.exp(-jnp.abs(alog - cc)))
    kk = oml * jnp.where(z >= 0.0, t, 1.0) / (1.0 + t)

    hi = logf.astype(BF16)
    r1 = logf - hi.astype(F32)
    mid = r1.astype(BF16)
    lo = (r1 - mid.astype(F32)).astype(BF16)
    b = (_dot(tri, hi) + _dot(tri, mid) + _dot(tri, lo)) * LOG2_E

    shape3 = (n // SUBLANES, SUBLANES, LANES)
    b3, q3, k3 = b.reshape(shape3), q.reshape(shape3), kk.reshape(shape3)
    halves = (slice(0, half), slice(half, n))
    qb, kb = q.astype(BF16), kk.astype(BF16)
    acc = [jnp.where(lv == nlev - 1, _dot_nt(qb[hs], kb[hs]), 0.0) for hs in halves]
    for lev in range(nlev - 1):
        m = 1 << lev
        e = jnp.exp2(-jnp.abs(b3 - _hg_anchor(b3, m, rev)))
        w = (_hg_pick(q3, k3, m, rev) * e).reshape(n, LANES).astype(BF16)
        acc = [jnp.where(lv == lev, _dot_nt(w[hs], w[hs]), a) for hs, a in zip(halves, acc)]
    first, second = (halves[1], halves[0]) if rev else halves
    e = jnp.exp2(-jnp.abs(b - (b[half:half + 1, :] if rev else b[half - 1:half, :])))
    top = _dot_nt((q[second] * e[second]).astype(BF16), (kk[first] * e[first]).astype(BF16))

    vb = v.astype(BF16)
    a0, a1 = acc[0].astype(BF16), acc[1].astype(BF16)
    tb = top.astype(BF16)
    if rev:
        o_lo = _dot(jnp.concatenate([a0, tb], axis=1), vb)
        o_hi = _dot(a1, vb[halves[1]])
    else:
        o_lo = _dot(a0, vb[halves[0]])
        o_hi = _dot(jnp.concatenate([tb, a1], axis=1), vb)
    b_last = b[0:1, :] if rev else b[n - 1:n, :]
    qh = (q * jnp.exp2(b)).astype(BF16)
    o = jnp.concatenate([o_lo, o_hi], axis=0) + _dot_nt(qh, st.astype(BF16))
    kh = (kk * jnp.exp2(b_last - b)).astype(BF16)
    st_new = st * jnp.exp2(b_last) + _dot_tn(vb, kh)
    return o, st_new


def _hg_fwd_kernel(q_ref, v_ref, z_ref, al_ref, cl_ref, om_ref, tri_ref, lv_ref, o_ref, st_ref):
    @pl.when(pl.program_id(1) == 0)
    def _():
        st_ref[...] = jnp.zeros_like(st_ref)

    tri = tri_ref[...]
    lv = lv_ref[...]
    for h in range(HG_HEADS):
        hs = slice(h * HG_DK, (h + 1) * HG_DK)
        o, st = _hg_block(q_ref[:, hs], v_ref[:, hs], z_ref[:, hs], al_ref[:, hs], cl_ref[:, hs],
                          om_ref[:, hs], tri, lv, st_ref[h], False)
        o_ref[:, hs] = o
        st_ref[h] = st


def _hg_bwd_kernel(q_ref, v_ref, z_ref, g_ref, of_ref, al_ref, cl_ref, om_ref, ng_ref, tri_ref, lv_ref,
                   o_ref, st_ref):
    @pl.when(pl.program_id(1) == 0)
    def _():
        st_ref[...] = jnp.zeros_like(st_ref)

    tri = tri_ref[...]
    lv = lv_ref[...]
    for h in range(HG_HEADS):
        hs = slice(h * HG_DK, (h + 1) * HG_DK)
        o, st = _hg_block(q_ref[:, hs], v_ref[:, hs], z_ref[:, hs], al_ref[:, hs], cl_ref[:, hs],
                          om_ref[:, hs], tri, lv, st_ref[h], True)
        st_ref[h] = st
        t = of_ref[:, hs] + o
        y = t * lax.rsqrt(jnp.mean(t * t, axis=-1, keepdims=True) + EPS)
        o_ref[:, hs] = (y * ng_ref[:, hs] * _silu(g_ref[:, hs])).astype(o_ref.dtype)


def _hgrn(u, lb, norm_g, nbatch, seq, ctx_len, lat_rows, off_hg):
    assert ctx_len == HG_BLOCK
    n = u.shape[0]
    hd = HG_HEADS * HG_DK
    col = off_hg // hd
    per = seq // HG_BLOCK
    lat_blocks = lat_rows // HG_BLOCK
    lbf = lb.astype(F32)
    alog, clog, oml = jnp.log(lbf), jnp.log1p(-lbf), 1.0 - lbf

    def fmap(c):
        return lambda b, j: (jnp.where(j == 0, lat_blocks + b, b * per + j - 1), c)

    def bmap(c):
        return lambda b, j: (jnp.where(j == 0, lat_blocks + b, b * per + per - j), c)

    const = lambda shape: pl.BlockSpec(shape, lambda b, j: (0, 0))
    grid = (nbatch, per + 1)
    vec = lambda a: a.reshape(1, hd)
    o_f = pl.pallas_call(
        _hg_fwd_kernel,
        grid=grid,
        in_specs=[pl.BlockSpec((HG_BLOCK, hd), fmap(col)), pl.BlockSpec((HG_BLOCK, hd), fmap(col + 1)),
                  pl.BlockSpec((HG_BLOCK, hd), fmap(col + 2)),
                  const((1, hd)), const((1, hd)), const((1, hd)),
                  const((HG_BLOCK, HG_BLOCK)), const((HG_BLOCK // 2, HG_BLOCK // 2))],
        out_specs=pl.BlockSpec((HG_BLOCK, hd), fmap(0)),
        out_shape=jax.ShapeDtypeStruct((n, hd), F32),
        scratch_shapes=[pltpu.VMEM((HG_HEADS, HG_DK, HG_DK), F32)],
        compiler_params=_cparams(("arbitrary", "arbitrary"), 32),
    )(u, u, u, vec(alog[0]), vec(clog[0]), vec(oml[0]),
      jnp.asarray(_hg_tri(False), BF16), jnp.asarray(_hg_level_map(False)))
    return pl.pallas_call(
        _hg_bwd_kernel,
        grid=grid,
        in_specs=[pl.BlockSpec((HG_BLOCK, hd), bmap(col)), pl.BlockSpec((HG_BLOCK, hd), bmap(col + 1)),
                  pl.BlockSpec((HG_BLOCK, hd), bmap(col + 3)), pl.BlockSpec((HG_BLOCK, hd), bmap(col + 4)),
                  pl.BlockSpec((HG_BLOCK, hd), bmap(0)),
                  const((1, hd)), const((1, hd)), const((1, hd)), const((1, hd)),
                  const((HG_BLOCK, HG_BLOCK)), const((HG_BLOCK // 2, HG_BLOCK // 2))],
        out_specs=pl.BlockSpec((HG_BLOCK, hd), bmap(0)),
        out_shape=jax.ShapeDtypeStruct((n, hd), BF16),
        scratch_shapes=[pltpu.VMEM((HG_HEADS, HG_DK, HG_DK), F32)],
        compiler_params=_cparams(("arbitrary", "arbitrary"), 32),
    )(u, u, u, u, o_f, vec(alog[1]), vec(clog[1]), vec(oml[1]), vec(norm_g.astype(F32)),
      jnp.asarray(_hg_tri(True), BF16), jnp.asarray(_hg_level_map(True)))


def _out_kernel(x_ref, cv_ref, na_ref, hg_ref, w_ref, ga_ref, g2_ref, sh2_ref, s2_ref, wrh_ref, wrl_ref, br_ref,
                xo_ref, h_ref, rt_ref, *, tiles_per_batch, nbatch):
    r = jnp.minimum(pl.program_id(0) // tiles_per_batch, nbatch)
    c0 = cv_ref.shape[1]
    c1 = c0 + na_ref.shape[1]
    mix = (_dot(cv_ref[...], w_ref[0:c0, :]) + _dot(na_ref[...], w_ref[c0:c1, :])
           + _dot(hg_ref[...], w_ref[c1:, :]))
    xn = x_ref[...] + ga_ref[pl.ds(r, 1), :] * mix
    xo_ref[...] = xn
    h = _rms_mod(xn, g2_ref[...], s2_ref[pl.ds(r, 1), :], sh2_ref[pl.ds(r, 1), :])
    h_ref[...] = h.astype(h_ref.dtype)
    h_hi = h.astype(BF16)
    h_lo = (h - h_hi.astype(F32)).astype(BF16)
    logits = (_dot(h_hi, wrh_ref[...]) + (_dot(h_lo, wrh_ref[...]) + _dot(h_hi, wrl_ref[...]))
              + br_ref[...])
    rt_ref[...] = _route_rows(logits)


def _route_rows(lg):
    lane = lax.broadcasted_iota(jnp.int32, lg.shape, 1)
    big = jnp.int32(2 ** 30)
    low = jnp.float32(-3e38)

    def first_max(vals, mask):
        m = jnp.max(vals, axis=-1, keepdims=True)
        idx = jnp.min(jnp.where(jnp.logical_and(vals == m, mask), lane, big), axis=-1, keepdims=True)
        return m, idx

    gmask = lane < N_GROUPS
    gl = jnp.where(gmask, lg, low)
    gm, grp = first_max(gl, gmask)
    p_grp = 1.0 / jnp.sum(jnp.where(gmask, jnp.exp(gl - gm), 0.0), axis=-1, keepdims=True)
    lo = N_GROUPS + grp * EXP_PER_GROUP
    emask = jnp.logical_and(lane >= lo, lane < lo + EXP_PER_GROUP)
    el = jnp.where(emask, lg, low)
    m1, i1 = first_max(el, emask)
    emask2 = jnp.logical_and(emask, lane != i1)
    el2 = jnp.where(emask2, lg, low)
    m2, i2 = first_max(el2, emask2)
    t = jnp.exp(m2 - m1)
    w1 = p_grp / (1.0 + t)
    w2 = p_grp * t / (1.0 + t)
    e1 = (i1 - N_GROUPS).astype(F32)
    e2 = (i2 - N_GROUPS).astype(F32)
    return jnp.where(lane == 0, e1, jnp.where(lane == 1, e2, jnp.where(lane == 2, w1,
                     jnp.where(lane == 3, w2, 0.0))))


def _out_proj(x, conv, na, hg, w_bf16, l, mod, g_ffn, w_router, b_router, n_rows, nbatch, lat_rows):
    d = x.shape[1]
    w_router_hi = w_router.astype(BF16)
    w_router_lo = (w_router - w_router_hi.astype(F32)).astype(BF16)
    tm = _pick_tile(OUT_TM, lat_rows // nbatch, n_rows)
    kern = functools.partial(_out_kernel, tiles_per_batch=lat_rows // nbatch // tm, nbatch=nbatch)
    row = lambda w: pl.BlockSpec((tm, w), lambda i: (i, 0))
    const = lambda shape: pl.BlockSpec(shape, lambda i: (0, 0))
    modc = lambda c: pl.BlockSpec((None, SUBLANES, d), lambda i: (l, 0, c))
    return pl.pallas_call(
        kern,
        grid=(n_rows // tm,),
        in_specs=[row(d), row(conv.shape[1]), row(na.shape[1]), row(hg.shape[1]),
                  pl.BlockSpec((None, d, d), lambda i: (l, 0, 0)),
                  modc(2),
                  const((1, d)),
                  modc(3),
                  modc(4),
                  const((d, ROUTER_PAD)), const((d, ROUTER_PAD)), const((1, ROUTER_PAD))],
        out_specs=[row(d), row(d), row(ROUTER_PAD)],
        out_shape=[jax.ShapeDtypeStruct((n_rows, d), F32), jax.ShapeDtypeStruct((n_rows, d), BF16),
                   jax.ShapeDtypeStruct((n_rows, ROUTER_PAD), F32)],
        compiler_params=_cparams(("arbitrary",), 56),
    )(x, conv, na, hg, w_bf16, mod, g_ffn.reshape(1, d), mod, mod, w_router_hi, w_router_lo, b_router)


def _moe_kernel(be_ref, nu_ref, xs_ref, w1_ref, w3_ref, w2_ref, sw_ref, o_ref, w1b, w3b, w2b):
    i = pl.program_id(0)
    e = be_ref[i]
    prev = be_ref[jnp.maximum(i - 1, 0)]

    @pl.when(jnp.logical_or(i == 0, e != prev))
    def _():
        w1b[...] = w1_ref[...].astype(BF16)
        w3b[...] = w3_ref[...].astype(BF16)
        w2b[...] = w2_ref[...].astype(BF16)

    @pl.when(i < nu_ref[0])
    def _():
        x = xs_ref[...]
        a = (_silu(_dot(x, w1b[...])) * _dot(x, w3b[...])).astype(BF16)
        o_ref[...] = (_dot(a, w2b[...]) * sw_ref[...]).astype(o_ref.dtype)

    @pl.when(i >= nu_ref[0])
    def _():
        o_ref[...] = jnp.zeros_like(o_ref)


def _moe_experts(xs, slot_w, blk_e, nused, w1, w3, w2, l):
    p, d = xs.shape
    de = w1.shape[3]
    bm = MOE_BM
    grid_spec = pltpu.PrefetchScalarGridSpec(
        num_scalar_prefetch=2,
        grid=(p // bm,),
        in_specs=[pl.BlockSpec((bm, d), lambda i, be, nu: (i, 0)),
                  pl.BlockSpec((None, None, d, de), lambda i, be, nu: (l, be[i], 0, 0)),
                  pl.BlockSpec((None, None, d, de), lambda i, be, nu: (l, be[i], 0, 0)),
                  pl.BlockSpec((None, None, de, d), lambda i, be, nu: (l, be[i], 0, 0)),
                  pl.BlockSpec((bm, 1), lambda i, be, nu: (i, 0))],
        out_specs=pl.BlockSpec((bm, d), lambda i, be, nu: (i, 0)),
        scratch_shapes=[pltpu.VMEM((d, de), BF16), pltpu.VMEM((d, de), BF16), pltpu.VMEM((de, d), BF16)],
    )
    return pl.pallas_call(
        _moe_kernel,
        grid_spec=grid_spec,
        out_shape=jax.ShapeDtypeStruct((p, d), BF16),
        compiler_params=_cparams(("arbitrary",), 48),
    )(blk_e, nused, xs, w1, w3, w2, slot_w.reshape(p, 1))


def _rows(a, idx):
    return a.at[idx].get(mode="promise_in_bounds")


def _route_meta(route, n):
    i32 = jnp.int32
    eid = route[:, 0:TOP_K].astype(i32).reshape(-1)
    wt = route[:, TOP_K:2 * TOP_K].reshape(-1)
    a = n * TOP_K
    bm = MOE_BM
    nblk = -(-a // bm) + N_EXPERTS
    p = nblk * bm
    experts = jnp.arange(N_EXPERTS, dtype=i32)[None, :]
    ja = jnp.arange(a, dtype=i32)
    se, order, wsort = lax.sort((eid, ja, wt), num_keys=1, is_stable=True)
    cnt = jnp.sum((eid[:, None] == experts).astype(i32), axis=0)
    pcnt = (cnt + bm - 1) // bm * bm
    pend = jnp.cumsum(pcnt)
    pstart = pend - pcnt
    end = jnp.cumsum(cnt)
    start = end - cnt
    off = pstart - start
    d_off = off - jnp.concatenate([jnp.zeros((1,), i32), off[:-1]])
    dst_sorted = ja + jnp.sum(jnp.where(ja[:, None] >= start[None, :], d_off[None, :], 0), axis=1)
    _, pos = lax.sort((order, dst_sorted), num_keys=1)
    jp = jnp.arange(p, dtype=i32)
    in_or_after = jp[:, None] >= pstart[None, :]
    src = jp - jnp.sum(jnp.where(in_or_after, d_off[None, :], 0), axis=1)
    valid = src < jnp.sum(jnp.where(in_or_after, cnt[None, :], 0), axis=1)
    src = jnp.where(valid, src, jp % a)
    slot_tok = _rows(order, src) // TOP_K
    slot_w = jnp.where(valid, _rows(wsort, src), 0.0)
    jb = jnp.arange(nblk, dtype=i32) * bm
    blk_e = jnp.minimum(jnp.sum((jb[:, None] >= pend[None, :]).astype(i32), axis=1), N_EXPERTS - 1)
    nused = (pend[-1:] // bm).astype(i32)
    return slot_tok, slot_w, pos.reshape(n, TOP_K), blk_e, nused


def _combine_kernel(x_ref, y0_ref, y1_ref, ga_ref, gf_ref, o_ref, *, tiles_per_batch, nbatch, final):
    r = jnp.minimum(pl.program_id(0) // tiles_per_batch, nbatch)
    xn = x_ref[...] + ga_ref[pl.ds(r, 1), :] * (y0_ref[...].astype(F32) + y1_ref[...].astype(F32))
    if final:
        xn = xn * lax.rsqrt(jnp.mean(xn * xn, axis=-1, keepdims=True) + EPS) * gf_ref[...]
    o_ref[...] = xn


def _combine(x, y0, y1, mod, l, g_final, n_rows, nbatch, lat_rows, final):
    d = x.shape[1]
    tm = _pick_tile(OUT_TM, lat_rows // nbatch, n_rows)
    kern = functools.partial(_combine_kernel, tiles_per_batch=lat_rows // nbatch // tm, nbatch=nbatch,
                             final=final)
    row = pl.BlockSpec((tm, d), lambda i: (i, 0))
    return pl.pallas_call(
        kern,
        grid=(n_rows // tm,),
        in_specs=[row, row, row, pl.BlockSpec((None, SUBLANES, d), lambda i: (l, 0, 5)),
                  pl.BlockSpec((1, d), lambda i: (0, 0))],
        out_specs=row,
        out_shape=jax.ShapeDtypeStruct((n_rows, d), F32),
        compiler_params=_cparams(("arbitrary",), 40),
    )(x, y0, y1, mod, g_final.reshape(1, d))


def kernel(x, c, ctx, c_ctx, w_ada, b_ada, g_mix, g_ffn, w_in, conv_w, conv_b, conv_ln_g, conv_ln_b,
           na_rpb, hgrn_lb, hgrn_norm_g, w_out, w_router_group, b_router_group, w_router_expert,
           b_router_expert, w_exp_gate, w_exp_up, w_exp_down, g_final):
    nb, seq, d = x.shape
    ctx_len = ctx.shape[1]
    depth = w_ada.shape[0]
    lat_rows = nb * seq
    n_all = lat_rows + nb * ctx_len
    conv_ch = conv_w.shape[2]
    off_na = 2 * conv_ch
    off_hg = off_na + 3 * NA_HEADS * NA_HD
    rows = seq // GRID_W
    assert nb < SUBLANES and rows % NA_ROWS == 0 and rows >= NA_KEY_ROWS

    lbs = jnp.cumsum(jax.nn.softmax(hgrn_lb.astype(F32), axis=0), axis=0)
    lbs = lbs - lbs[:1]

    cond = jnp.concatenate([c, c_ctx[None, :], jnp.zeros((SUBLANES - nb - 1, d), F32)], axis=0)
    mod = _ada_mod(cond, w_ada, b_ada)

    xs = jnp.concatenate([x.reshape(lat_rows, d), ctx.reshape(nb * ctx_len, d)], axis=0)
    w_in_b = w_in.astype(BF16)
    w_out_b = w_out.astype(BF16)
    for l in range(depth):
        with_ctx = l < depth - 1
        n_act = n_all if with_ctx else lat_rows
        u, u_att = _norm_in(xs, g_mix[l], mod, w_in_b, l, nb, lat_rows, off_na, off_hg)

        conv = _conv_module(u, conv_w[l], conv_b[l], conv_ln_g[l], conv_ln_b[l], n_act, lat_rows, seq)
        na = _na_latent(u_att, _na_bias(na_rpb[l], rows), nb, seq, lat_rows, 0)
        if with_ctx:
            na = jnp.concatenate([na, _ctx_attn(u_att, nb, ctx_len, lat_rows, 0)], axis=0)
        hg = _hgrn(u, lbs[l], hgrn_norm_g[l], nb, seq, ctx_len, lat_rows, off_na)

        w_router = jnp.concatenate(
            [w_router_group[l], w_router_expert[l],
             jnp.zeros((d, ROUTER_PAD - N_GROUPS - N_EXPERTS), F32)], axis=1)
        b_router = jnp.concatenate(
            [b_router_group[l], b_router_expert[l],
             jnp.zeros((ROUTER_PAD - N_GROUPS - N_EXPERTS,), F32)]).reshape(1, ROUTER_PAD)
        x_mid, h, route = _out_proj(xs, conv, na, hg, w_out_b, l, mod, g_ffn[l],
                                    w_router, b_router, n_act, nb, lat_rows)

        slot_tok, slot_w, pos, blk_e, nused = _route_meta(route, n_act)
        ys = _moe_experts(_rows(h, slot_tok), slot_w, blk_e, nused, w_exp_gate, w_exp_up, w_exp_down, l)
        y0 = _rows(ys, pos[:, 0])
        y1 = _rows(ys, pos[:, 1])
        xs = _combine(x_mid, y0, y1, mod, l, g_final, n_act, nb, lat_rows, final=not with_ctx)
    return xs.reshape(nb, seq, d)
```

```python
import functools

import numpy as np
import jax
import jax.numpy as jnp
from jax import lax
from jax.experimental import pallas as pl
from jax.experimental.pallas import tpu as pltpu

F32 = jnp.float32
BF16 = jnp.bfloat16

EPS = 1e-6
NEG_INF = -1e30
LOG2_E = 1.4426950408889634

GRID_W = 64
CONV_K = 31
NA_HEADS = 16
NA_HD = 64
NA_KH = 8
NA_KW = 16
HG_HEADS = 4
HG_DK = 128
N_GROUPS = 4
EXP_PER_GROUP = 8
N_EXPERTS = N_GROUPS * EXP_PER_GROUP
TOP_K = 2

LANES = 128
SUBLANES = 8
VMEM_BYTES = 64 * 1024 * 1024

ROW_TILE = 256
IN_TM = 1024
IN_TN = 512
OUT_TM = 512
NA_ROWS = 8
NA_KEY_ROWS = 16
HG_BLOCK = 256
MOE_BM = 512
ROUTER_PAD = LANES
HALO = 16


def _pick_tile(pref, *extents):
    t = pref
    while t > ROW_TILE and any(e % t for e in extents):
        t //= 2
    assert all(e % t == 0 for e in extents)
    return t


def _cparams(sem, vmem_mb):
    return pltpu.CompilerParams(dimension_semantics=sem, vmem_limit_bytes=vmem_mb * 1024 * 1024)


def _dot(a, b):
    return jnp.dot(a, b, preferred_element_type=F32)


def _dot_nt(a, b):
    return lax.dot_general(a, b, (((1,), (1,)), ((), ())), preferred_element_type=F32)


def _dot_tn(a, b):
    return lax.dot_general(a, b, (((0,), (0,)), ((), ())), preferred_element_type=F32)


def _sigmoid(x):
    return 1.0 / (1.0 + jnp.exp(-x))


def _silu(x):
    return x * _sigmoid(x)


def _ada_kernel(c_ref, w_ref, b_ref, o_ref):
    sc = _silu(c_ref[...])
    o_ref[0] = jnp.dot(sc, w_ref[0], precision=lax.Precision.HIGHEST,
                       preferred_element_type=F32) + b_ref[0]


def _ada_mod(cond, w_ada, b_ada):
    depth, d, n = w_ada.shape
    tn = 1024
    return pl.pallas_call(
        _ada_kernel,
        grid=(depth, n // tn),
        in_specs=[
            pl.BlockSpec((SUBLANES, d), lambda l, j: (0, 0)),
            pl.BlockSpec((1, d, tn), lambda l, j: (l, 0, j)),
            pl.BlockSpec((1, 1, tn), lambda l, j: (l, 0, j)),
        ],
        out_specs=pl.BlockSpec((1, SUBLANES, tn), lambda l, j: (l, 0, j)),
        out_shape=jax.ShapeDtypeStruct((depth, SUBLANES, n), F32),
        compiler_params=_cparams(("arbitrary", "arbitrary"), 40),
    )(cond, w_ada, b_ada.reshape(depth, 1, n))


def _rms_mod(x, g, scale, shift):
    y = x * lax.rsqrt(jnp.mean(x * x, axis=-1, keepdims=True) + EPS)
    return (y * g) * (1.0 + scale) + shift


def _norm_in_kernel(x_ref, g_ref, sh_ref, sc_ref, w_ref, o_ref, h_ref, *, tiles_per_batch, nbatch):
    i = pl.program_id(0)

    @pl.when(pl.program_id(1) == 0)
    def _():
        r = jnp.minimum(i // tiles_per_batch, nbatch)
        h = _rms_mod(x_ref[...], g_ref[...], sc_ref[pl.ds(r, 1), :], sh_ref[pl.ds(r, 1), :])
        h_ref[...] = h.astype(BF16)

    o_ref[...] = _dot(h_ref[...], w_ref[...])


def _norm_in(x, g, mod, w_bf16, l, nbatch, lat_rows):
    n, d = x.shape
    nout = w_bf16.shape[2]
    tm, tn = _pick_tile(IN_TM, lat_rows // nbatch, n), IN_TN
    kern = functools.partial(_norm_in_kernel, tiles_per_batch=lat_rows // nbatch // tm, nbatch=nbatch)
    return pl.pallas_call(
        kern,
        grid=(n // tm, nout // tn),
        in_specs=[
            pl.BlockSpec((tm, d), lambda i, j: (i, 0)),
            pl.BlockSpec((1, d), lambda i, j: (0, 0)),
            pl.BlockSpec((None, SUBLANES, d), lambda i, j: (l, 0, 0)),
            pl.BlockSpec((None, SUBLANES, d), lambda i, j: (l, 0, 1)),
            pl.BlockSpec((None, d, tn), lambda i, j: (l, 0, j)),
        ],
        out_specs=pl.BlockSpec((tm, tn), lambda i, j: (i, j)),
        out_shape=jax.ShapeDtypeStruct((n, nout), F32),
        scratch_shapes=[pltpu.VMEM((tm, d), BF16)],
        compiler_params=_cparams(("arbitrary", "arbitrary"), 48),
    )(x, g.reshape(1, d), mod, mod, w_bf16)


def _conv_kernel(ap_ref, gp_ref, a_ref, gt_ref, an_ref, gn_ref, w_ref, b_ref, lg_ref, lb_ref,
                 o_ref, buf_ref, acc_ref, *, lat_tiles, tiles_per_seq):
    i = pl.program_id(0)
    tc, ch = a_ref.shape
    is_lat = i < lat_tiles
    pos = i % tiles_per_seq
    first = jnp.logical_or(jnp.logical_not(is_lat), pos == 0)
    last = jnp.logical_or(jnp.logical_not(is_lat), pos == tiles_per_seq - 1)

    buf_ref[0:HALO] = jnp.where(first, 0.0, ap_ref[...] * _sigmoid(gp_ref[...]))
    buf_ref[HALO:HALO + tc] = a_ref[...] * _sigmoid(gt_ref[...])
    buf_ref[HALO + tc:2 * HALO + tc] = jnp.where(last, 0.0, an_ref[...] * _sigmoid(gn_ref[...]))

    rows = 64
    base = HALO - CONV_K // 2
    for c in range(ch // LANES):
        cs = slice(c * LANES, (c + 1) * LANES)
        for r in range(tc // rows):
            acc = None
            for res in range(SUBLANES):
                y = None
                for k in range(CONV_K):
                    if (base + k) % SUBLANES != res:
                        continue
                    lo = r * rows + (base + k) // SUBLANES * SUBLANES
                    term = w_ref[k:k + 1, cs] * buf_ref[lo:lo + rows + SUBLANES, cs]
                    y = term if y is None else y + term
                if y is not None:
                    y = y[res:res + rows]
                    acc = y if acc is None else acc + y
            acc_ref[r * rows:(r + 1) * rows, cs] = acc

    h = acc_ref[...] + b_ref[...]
    mu = jnp.mean(h, axis=-1, keepdims=True)
    var = jnp.mean(jnp.square(h - mu), axis=-1, keepdims=True)
    y = (h - mu) * lax.rsqrt(var + EPS) * lg_ref[...] + lb_ref[...]
    o_ref[...] = _silu(y).astype(o_ref.dtype)


def _conv_module(u, w_dw, b_dw, ln_g, ln_b, n_rows, lat_rows, seq):
    ch = w_dw.shape[1]
    tc = ROW_TILE
    per = tc // HALO
    nh = u.shape[0] // HALO
    kern = functools.partial(_conv_kernel, lat_tiles=lat_rows // tc, tiles_per_seq=seq // tc)
    prev_map = lambda c: (lambda i: (jnp.maximum(i * per - 1, 0), c))
    next_map = lambda c: (lambda i: (jnp.minimum((i + 1) * per, nh - 1), c))
    vec = lambda a: a.reshape(1, ch)
    return pl.pallas_call(
        kern,
        grid=(n_rows // tc,),
        in_specs=[
            pl.BlockSpec((HALO, ch), prev_map(0)),
            pl.BlockSpec((HALO, ch), prev_map(1)),
            pl.BlockSpec((tc, ch), lambda i: (i, 0)),
            pl.BlockSpec((tc, ch), lambda i: (i, 1)),
            pl.BlockSpec((HALO, ch), next_map(0)),
            pl.BlockSpec((HALO, ch), next_map(1)),
            pl.BlockSpec((CONV_K, ch), lambda i: (0, 0)),
            pl.BlockSpec((1, ch), lambda i: (0, 0)),
            pl.BlockSpec((1, ch), lambda i: (0, 0)),
            pl.BlockSpec((1, ch), lambda i: (0, 0)),
        ],
        out_specs=pl.BlockSpec((tc, ch), lambda i: (i, 0)),
        out_shape=jax.ShapeDtypeStruct((n_rows, ch), BF16),
        scratch_shapes=[pltpu.VMEM((tc + 2 * HALO, ch), F32), pltpu.VMEM((tc, ch), F32)],
        compiler_params=_cparams(("arbitrary",), 16),
    )(u, u, u, u, u, u, w_dw, vec(b_dw), vec(ln_g), vec(ln_b))


def _na_bias_tables(rows):
    groups = rows // NA_ROWS
    reps = [0, min(1, groups - 1), groups - 1]
    out = []
    for g in reps:
        start = int(np.clip(NA_ROWS * g - NA_KH // 2, 0, rows - NA_KEY_ROWS))
        per_row = []
        for i in range(NA_ROWS):
            r = NA_ROWS * g + i
            sr = int(np.clip(r - NA_KH // 2, 0, rows - NA_KH))
            per_row.append((sr - start, sr - r + NA_KH - 1))
        out.append(per_row)
    return out


def _na_bias(rpb, rows):
    nh = rpb.shape[0]
    ndr, ndc = 2 * NA_KH - 1, 2 * NA_KW - 1
    period = 2 * GRID_W - 1
    pad = GRID_W - NA_KW
    vp = jnp.pad(rpb.astype(F32), ((0, 0), (0, 0), (pad, period - ndc - pad)))
    hank = jnp.tile(vp, (1, 1, GRID_W + 1))[:, :, :GRID_W * (period + 1)]
    hank = hank.reshape(nh, ndr, GRID_W, period + 1)[..., :GRID_W]
    toe = hank[:, :, ::-1, :]
    c = np.arange(GRID_W)[:, None]
    j = np.arange(GRID_W)[None, :]
    ws = np.clip(c - NA_KW // 2, 0, GRID_W - NA_KW)
    col_ok = (j >= ws) & (j < ws + NA_KW)
    toe = jnp.where(col_ok[None, None], toe, NEG_INF)
    flat = jnp.transpose(toe, (0, 2, 1, 3)).reshape(nh, GRID_W, ndr * GRID_W)
    nk = NA_KEY_ROWS * GRID_W
    lpad = NA_ROWS * GRID_W
    total = 2 * nk

    def padded(shift):
        return jnp.pad(flat, ((0, 0), (0, 0), (lpad - shift, total - flat.shape[2] - lpad + shift)))

    tab = jnp.stack([padded(0), padded(GRID_W)])
    tab = tab.reshape(2, nh, GRID_W, total // LANES, LANES).transpose(0, 1, 3, 2, 4)
    tables = _na_bias_tables(rows)
    mask = np.full((len(tables), NA_ROWS, nk), NEG_INF, np.float32)
    dvals = []
    for cls, per_row in enumerate(tables):
        dvals.append(per_row[0][1] - per_row[0][0])
        for i, (off, lo) in enumerate(per_row):
            assert lo - off == dvals[-1] - i and -NA_ROWS <= lo - off < NA_ROWS
            mask[cls, i, off * GRID_W:(off + NA_KH) * GRID_W] = 0.0
    assert len({d % 2 for d in dvals}) == 1
    return tab, jnp.asarray(mask), tuple(dvals)


def _softmax_pv(s_parts, v_parts):
    m = functools.reduce(jnp.maximum, [jnp.max(s, axis=-1, keepdims=True) for s in s_parts])
    acc, l = None, None
    for s, v in zip(s_parts, v_parts):
        p = jnp.exp(s - m)
        ps = jnp.sum(p, axis=-1, keepdims=True)
        pv = _dot(p.astype(BF16), v)
        l = ps if l is None else l + ps
        acc = pv if acc is None else acc + pv
    return acc / l


def _na_kernel(q_ref, k0, k1, k2, k3, v0, v1, v2, v3, kc_ref, vc_ref, tab_ref, mask_ref, o_ref, *,
               groups, dvals):
    g = pl.program_id(1)
    dcls = jnp.where(g == 0, dvals[0], jnp.where(g == groups - 1, dvals[2], dvals[1]))
    q2 = q_ref[...] * (NA_HD ** -0.5)
    lane = lax.broadcasted_iota(jnp.int32, q2.shape, 1)
    ks = [k[...].astype(BF16) for k in (k0, k1, k2, k3)]
    vs = [v[...].astype(BF16) for v in (v0, v1, v2, v3)]
    kc = kc_ref[...].astype(BF16)
    vc = vc_ref[...].astype(BF16)
    kw = ks[0].shape[0]
    bpk = kw // LANES
    pad_blocks = NA_ROWS * GRID_W // LANES
    outs = []
    for a in range(2):
        sel = (lane < NA_HD) if a == 0 else (lane >= NA_HD)
        qa = jnp.where(sel, q2, 0.0).astype(BF16)
        s_parts = []
        for m, k in enumerate(ks):
            s = _dot_nt(qa, k)
            row_blocks = []
            for i in range(NA_ROWS):
                copy = (dvals[0] - i) % 2
                first = (dcls - i - copy + 2 * pad_blocks) // 2 + m * bpk
                bias = jnp.concatenate([tab_ref[copy, a, first + t] for t in range(bpk)], axis=1)
                bias = bias + mask_ref[i:i + 1, m * kw:(m + 1) * kw]
                row_blocks.append(s[i * GRID_W:(i + 1) * GRID_W, :] + bias)
            s_parts.append(jnp.concatenate(row_blocks, axis=0))
        s_parts.append(_dot_nt(qa, kc))
        outs.append(_softmax_pv(s_parts, vs + [vc]))
    o_ref[...] = jnp.where(lane < NA_HD, outs[0], outs[1]).astype(o_ref.dtype)


def _na_latent(u, bias, nbatch, seq, lat_rows, off_na):
    tab, mask, dvals = bias
    rows = seq // GRID_W
    groups = rows // NA_ROWS
    nq = NA_ROWS * GRID_W
    kblk = ROW_TILE
    nkb = NA_KEY_ROWS * GRID_W // kblk
    assert nkb == 4
    qcol = off_na // LANES
    heads2 = NA_HEADS * NA_HD // LANES
    kcol, vcol = qcol + heads2, qcol + 2 * heads2
    kb_per_batch = seq // kblk
    kb_per_grow = GRID_W * NA_ROWS // kblk
    lat_kb = lat_rows // kblk

    def kmap(col, m):
        def f(h, g, b):
            st = jnp.clip(g * kb_per_grow - (NA_KH // 2) * GRID_W // kblk, 0, kb_per_batch - nkb)
            return (b * kb_per_batch + st + m, col + h)
        return f

    def mask_map(h, g, b):
        return (jnp.where(g == 0, 0, jnp.where(g == groups - 1, 2, 1)), 0, 0)

    in_specs = [pl.BlockSpec((nq, LANES), lambda h, g, b: (b * groups + g, qcol + h))]
    in_specs += [pl.BlockSpec((kblk, LANES), kmap(kcol, m)) for m in range(nkb)]
    in_specs += [pl.BlockSpec((kblk, LANES), kmap(vcol, m)) for m in range(nkb)]
    in_specs += [pl.BlockSpec((ROW_TILE, LANES), lambda h, g, b: (lat_kb + b, kcol + h)),
                 pl.BlockSpec((ROW_TILE, LANES), lambda h, g, b: (lat_kb + b, vcol + h)),
                 pl.BlockSpec((2, 2) + tab.shape[2:], lambda h, g, b: (0, h, 0, 0, 0)),
                 pl.BlockSpec((None,) + mask.shape[1:], mask_map)]
    return pl.pallas_call(
        functools.partial(_na_kernel, groups=groups, dvals=dvals),
        grid=(heads2, groups, nbatch),
        in_specs=in_specs,
        out_specs=pl.BlockSpec((nq, LANES), lambda h, g, b: (b * groups + g, h)),
        out_shape=jax.ShapeDtypeStruct((lat_rows, NA_HEADS * NA_HD), BF16),
        compiler_params=_cparams(("arbitrary", "arbitrary", "arbitrary"), 40),
    )(*([u] * 11), tab, mask)


def _ctx_attn_kernel(q_ref, k_ref, v_ref, o_ref):
    q2 = q_ref[...] * (NA_HD ** -0.5)
    lane = lax.broadcasted_iota(jnp.int32, q2.shape, 1)
    k = k_ref[...].astype(BF16)
    v = v_ref[...].astype(BF16)
    outs = []
    for a in range(2):
        sel = (lane < NA_HD) if a == 0 else (lane >= NA_HD)
        qa = jnp.where(sel, q2, 0.0).astype(BF16)
        outs.append(_softmax_pv([_dot_nt(qa, k)], [v]))
    o_ref[...] = jnp.where(lane < NA_HD, outs[0], outs[1]).astype(o_ref.dtype)


def _ctx_attn(u, nbatch, ctx_len, lat_rows, off_na):
    assert ctx_len == ROW_TILE
    qcol = off_na // LANES
    heads2 = NA_HEADS * NA_HD // LANES
    base = lat_rows // ROW_TILE
    spec = lambda col: pl.BlockSpec((ROW_TILE, LANES), lambda b, h: (base + b, col + h))
    return pl.pallas_call(
        _ctx_attn_kernel,
        grid=(nbatch, heads2),
        in_specs=[spec(qcol), spec(qcol + heads2), spec(qcol + 2 * heads2)],
        out_specs=pl.BlockSpec((ROW_TILE, LANES), lambda b, h: (b, h)),
        out_shape=jax.ShapeDtypeStruct((nbatch * ctx_len, NA_HEADS * NA_HD), BF16),
        compiler_params=_cparams(("arbitrary", "arbitrary"), 16),
    )(u, u, u)


def _hg_level_map(rev):
    size = HG_BLOCK // 2
    t = np.arange(size)[:, None]
    s = np.arange(size)[None, :]
    x = t ^ s
    lvl = np.where(x > 0, np.frexp(np.maximum(x, 1))[1] - 1, -1)
    causal = (s < t) if not rev else (s > t)
    out = np.where(causal, lvl, -1)
    out = np.where(t == s, int(np.log2(size)), out)
    return out.astype(np.int32)


def _hg_tri(rev):
    t = np.arange(HG_BLOCK)[:, None]
    s = np.arange(HG_BLOCK)[None, :]
    return ((s <= t) if not rev else (s >= t)).astype(np.float32)


def _hg_anchor(b3, m, rev):
    nv = b3.shape[0]
    if m >= SUBLANES:
        w = m // SUBLANES
        b4 = b3.reshape(nv // (2 * w), 2 * w, SUBLANES, LANES)
        a = b4[:, w:w + 1, 0:1, :] if rev else b4[:, w - 1:w, SUBLANES - 1:SUBLANES, :]
        return jnp.broadcast_to(a, b4.shape).reshape(b3.shape)
    sub = lax.broadcasted_iota(jnp.int32, b3.shape, 1)
    out = None
    for g in range(SUBLANES // (2 * m)):
        idx = g * 2 * m + (m if rev else m - 1)
        a = jnp.broadcast_to(b3[:, idx:idx + 1, :], b3.shape)
        out = a if out is None else jnp.where(sub >= g * 2 * m, a, out)
    return out


def _neg_abs(x):
    bits = lax.bitcast_convert_type(x, jnp.uint32) | jnp.uint32(0x80000000)
    return lax.bitcast_convert_type(bits, F32)


def _hg_pick(q3, k3, m, rev):
    nv = q3.shape[0]
    if m >= SUBLANES:
        w = m // SUBLANES
        shape4 = (nv // (2 * w), 2 * w, SUBLANES, LANES)
        q4, k4 = q3.reshape(shape4), k3.reshape(shape4)
        lower, upper = (q4, k4) if rev else (k4, q4)
        return jnp.concatenate([lower[:, :w], upper[:, w:]], axis=1).reshape(q3.shape)
    upper_rows = (lax.broadcasted_iota(jnp.int32, q3.shape, 1) & m) != 0
    return jnp.where(upper_rows, k3 if rev else q3, q3 if rev else k3)


def _hg_block(q, v, z, alog, clog, oml, tri, lv, st, rev):
    n = q.shape[0]
    half = n // 2
    nlev = int(np.log2(n))
    q = _silu(q)
    t = jnp.exp(-jnp.abs(z))
    lsig = jnp.minimum(z, 0.0) - jnp.log(1.0 + t)
    cc = clog + lsig
    logf = jnp.maximum(alog, cc) + jnp.log(1.0 + jnp.exp(-jnp.abs(alog - cc)))
    kk = oml * jnp.where(z >= 0.0, t, 1.0) / (1.0 + t)

    hi = logf.astype(BF16)
    r1 = logf - hi.astype(F32)
    mid = r1.astype(BF16)
    lo = (r1 - mid.astype(F32)).astype(BF16)
    b = (_dot(tri, hi) + _dot(tri, mid) + _dot(tri, lo)) * LOG2_E

    shape3 = (n // SUBLANES, SUBLANES, LANES)
    b3, q3, k3 = b.reshape(shape3), q.reshape(shape3), kk.reshape(shape3)
    halves = (slice(0, half), slice(half, n))
    qb, kb = q.astype(BF16), kk.astype(BF16)
    acc = [jnp.where(lv == nlev - 1, _dot_nt(qb[hs], kb[hs]), 0.0) for hs in halves]
    for lev in range(nlev - 1):
        m = 1 << lev
        e = jnp.exp2(_neg_abs(b3 - _hg_anchor(b3, m, rev)))
        w = (_hg_pick(q3, k3, m, rev) * e).reshape(n, LANES).astype(BF16)
        acc = [jnp.where(lv == lev, _dot_nt(w[hs], w[hs]), a) for hs, a in zip(halves, acc)]
    first, second = (halves[1], halves[0]) if rev else halves
    e = jnp.exp2(_neg_abs(b - (b[half:half + 1, :] if rev else b[half - 1:half, :])))
    top = _dot_nt((q[second] * e[second]).astype(BF16), (kk[first] * e[first]).astype(BF16))

    vb = v.astype(BF16)
    a0, a1 = acc[0].astype(BF16), acc[1].astype(BF16)
    tb = top.astype(BF16)
    if rev:
        o_lo = _dot(jnp.concatenate([a0, tb], axis=1), vb)
        o_hi = _dot(a1, vb[halves[1]])
    else:
        o_lo = _dot(a0, vb[halves[0]])
        o_hi = _dot(jnp.concatenate([tb, a1], axis=1), vb)
    b_last = b[0:1, :] if rev else b[n - 1:n, :]
    qh = (q * jnp.exp2(b)).astype(BF16)
    o = jnp.concatenate([o_lo, o_hi], axis=0) + _dot_nt(qh, st.astype(BF16))
    kh = (kk * jnp.exp2(b_last - b)).astype(BF16)
    st_new = st * jnp.exp2(b_last) + _dot_tn(vb, kh)
    return o, st_new


def _hg_fwd_kernel(q_ref, v_ref, z_ref, al_ref, cl_ref, om_ref, tri_ref, lv_ref, o_ref, st_ref):
    @pl.when(pl.program_id(1) == 0)
    def _():
        st_ref[...] = jnp.zeros_like(st_ref)

    tri = tri_ref[...]
    lv = lv_ref[...]
    for h in range(HG_HEADS):
        hs = slice(h * HG_DK, (h + 1) * HG_DK)
        o, st = _hg_block(q_ref[:, hs], v_ref[:, hs], z_ref[:, hs], al_ref[:, hs], cl_ref[:, hs],
                          om_ref[:, hs], tri, lv, st_ref[h], False)
        o_ref[:, hs] = o
        st_ref[h] = st


def _hg_bwd_kernel(q_ref, v_ref, z_ref, g_ref, of_ref, al_ref, cl_ref, om_ref, ng_ref, tri_ref, lv_ref,
                   o_ref, st_ref):
    @pl.when(pl.program_id(1) == 0)
    def _():
        st_ref[...] = jnp.zeros_like(st_ref)

    tri = tri_ref[...]
    lv = lv_ref[...]
    for h in range(HG_HEADS):
        hs = slice(h * HG_DK, (h + 1) * HG_DK)
        o, st = _hg_block(q_ref[:, hs], v_ref[:, hs], z_ref[:, hs], al_ref[:, hs], cl_ref[:, hs],
                          om_ref[:, hs], tri, lv, st_ref[h], True)
        st_ref[h] = st
        t = of_ref[:, hs] + o
        y = t * lax.rsqrt(jnp.mean(t * t, axis=-1, keepdims=True) + EPS)
        o_ref[:, hs] = (y * ng_ref[:, hs] * _silu(g_ref[:, hs])).astype(o_ref.dtype)


def _hgrn(u, lb, norm_g, nbatch, seq, ctx_len, lat_rows, off_hg):
    assert ctx_len == HG_BLOCK
    n = u.shape[0]
    hd = HG_HEADS * HG_DK
    col = off_hg // hd
    per = seq // HG_BLOCK
    lat_blocks = lat_rows // HG_BLOCK
    lbf = lb.astype(F32)
    alog, clog, oml = jnp.log(lbf), jnp.log1p(-lbf), 1.0 - lbf

    def fmap(c):
        return lambda b, j: (jnp.where(j == 0, lat_blocks + b, b * per + j - 1), c)

    def bmap(c):
        return lambda b, j: (jnp.where(j == 0, lat_blocks + b, b * per + per - j), c)

    const = lambda shape: pl.BlockSpec(shape, lambda b, j: (0, 0))
    grid = (nbatch, per + 1)
    vec = lambda a: a.reshape(1, hd)
    o_f = pl.pallas_call(
        _hg_fwd_kernel,
        grid=grid,
        in_specs=[pl.BlockSpec((HG_BLOCK, hd), fmap(col)), pl.BlockSpec((HG_BLOCK, hd), fmap(col + 1)),
                  pl.BlockSpec((HG_BLOCK, hd), fmap(col + 2)),
                  const((1, hd)), const((1, hd)), const((1, hd)),
                  const((HG_BLOCK, HG_BLOCK)), const((HG_BLOCK // 2, HG_BLOCK // 2))],
        out_specs=pl.BlockSpec((HG_BLOCK, hd), fmap(0)),
        out_shape=jax.ShapeDtypeStruct((n, hd), F32),
        scratch_shapes=[pltpu.VMEM((HG_HEADS, HG_DK, HG_DK), F32)],
        compiler_params=_cparams(("arbitrary", "arbitrary"), 32),
    )(u, u, u, vec(alog[0]), vec(clog[0]), vec(oml[0]),
      jnp.asarray(_hg_tri(False), BF16), jnp.asarray(_hg_level_map(False)))
    return pl.pallas_call(
        _hg_bwd_kernel,
        grid=grid,
        in_specs=[pl.BlockSpec((HG_BLOCK, hd), bmap(col)), pl.BlockSpec((HG_BLOCK, hd), bmap(col + 1)),
                  pl.BlockSpec((HG_BLOCK, hd), bmap(col + 3)), pl.BlockSpec((HG_BLOCK, hd), bmap(col + 4)),
                  pl.BlockSpec((HG_BLOCK, hd), bmap(0)),
                  const((1, hd)), const((1, hd)), const((1, hd)), const((1, hd)),
                  const((HG_BLOCK, HG_BLOCK)), const((HG_BLOCK // 2, HG_BLOCK // 2))],
        out_specs=pl.BlockSpec((HG_BLOCK, hd), bmap(0)),
        out_shape=jax.ShapeDtypeStruct((n, hd), BF16),
        scratch_shapes=[pltpu.VMEM((HG_HEADS, HG_DK, HG_DK), F32)],
        compiler_params=_cparams(("arbitrary", "arbitrary"), 32),
    )(u, u, u, u, o_f, vec(alog[1]), vec(clog[1]), vec(oml[1]), vec(norm_g.astype(F32)),
      jnp.asarray(_hg_tri(True), BF16), jnp.asarray(_hg_level_map(True)))


def _out_kernel(x_ref, cv_ref, na_ref, hg_ref, w_ref, ga_ref, g2_ref, sh2_ref, s2_ref, wrh_ref, wrl_ref, br_ref,
                xo_ref, h_ref, rt_ref, *, tiles_per_batch, nbatch):
    r = jnp.minimum(pl.program_id(0) // tiles_per_batch, nbatch)
    c0 = cv_ref.shape[1]
    c1 = c0 + na_ref.shape[1]
    mix = (_dot(cv_ref[...], w_ref[0:c0, :]) + _dot(na_ref[...], w_ref[c0:c1, :])
           + _dot(hg_ref[...], w_ref[c1:, :]))
    xn = x_ref[...] + ga_ref[pl.ds(r, 1), :] * mix
    xo_ref[...] = xn
    h = _rms_mod(xn, g2_ref[...], s2_ref[pl.ds(r, 1), :], sh2_ref[pl.ds(r, 1), :])
    h_ref[...] = h.astype(h_ref.dtype)
    h_hi = h.astype(BF16)
    h_lo = (h - h_hi.astype(F32)).astype(BF16)
    logits = (_dot(h_hi, wrh_ref[...]) + (_dot(h_lo, wrh_ref[...]) + _dot(h_hi, wrl_ref[...]))
              + br_ref[...])
    rt_ref[...] = _route_rows(logits)


def _route_rows(lg):
    lane = lax.broadcasted_iota(jnp.int32, lg.shape, 1)
    big = jnp.int32(2 ** 30)
    low = jnp.float32(-3e38)

    def first_max(vals, mask):
        m = jnp.max(vals, axis=-1, keepdims=True)
        idx = jnp.min(jnp.where(jnp.logical_and(vals == m, mask), lane, big), axis=-1, keepdims=True)
        return m, idx

    gmask = lane < N_GROUPS
    gl = jnp.where(gmask, lg, low)
    gm, grp = first_max(gl, gmask)
    p_grp = 1.0 / jnp.sum(jnp.where(gmask, jnp.exp(gl - gm), 0.0), axis=-1, keepdims=True)
    lo = N_GROUPS + grp * EXP_PER_GROUP
    emask = jnp.logical_and(lane >= lo, lane < lo + EXP_PER_GROUP)
    el = jnp.where(emask, lg, low)
    m1, i1 = first_max(el, emask)
    emask2 = jnp.logical_and(emask, lane != i1)
    el2 = jnp.where(emask2, lg, low)
    m2, i2 = first_max(el2, emask2)
    t = jnp.exp(m2 - m1)
    w1 = p_grp / (1.0 + t)
    w2 = p_grp * t / (1.0 + t)
    e1 = (i1 - N_GROUPS).astype(F32)
    e2 = (i2 - N_GROUPS).astype(F32)
    return jnp.where(lane == 0, e1, jnp.where(lane == 1, e2, jnp.where(lane == 2, w1,
                     jnp.where(lane == 3, w2, 0.0))))


def _out_proj(x, conv, na, hg, w_bf16, l, mod, g_ffn, w_router, b_router, n_rows, nbatch, lat_rows):
    d = x.shape[1]
    w_router_hi = w_router.astype(BF16)
    w_router_lo = (w_router - w_router_hi.astype(F32)).astype(BF16)
    tm = _pick_tile(OUT_TM, lat_rows // nbatch, n_rows)
    kern = functools.partial(_out_kernel, tiles_per_batch=lat_rows // nbatch // tm, nbatch=nbatch)
    row = lambda w: pl.BlockSpec((tm, w), lambda i: (i, 0))
    const = lambda shape: pl.BlockSpec(shape, lambda i: (0, 0))
    modc = lambda c: pl.BlockSpec((None, SUBLANES, d), lambda i: (l, 0, c))
    return pl.pallas_call(
        kern,
        grid=(n_rows // tm,),
        in_specs=[row(d), row(conv.shape[1]), row(na.shape[1]), row(hg.shape[1]),
                  pl.BlockSpec((None, d, d), lambda i: (l, 0, 0)),
                  modc(2),
                  const((1, d)),
                  modc(3),
                  modc(4),
                  const((d, ROUTER_PAD)), const((d, ROUTER_PAD)), const((1, ROUTER_PAD))],
        out_specs=[row(d), row(d), row(ROUTER_PAD)],
        out_shape=[jax.ShapeDtypeStruct((n_rows, d), F32), jax.ShapeDtypeStruct((n_rows, d), BF16),
                   jax.ShapeDtypeStruct((n_rows, ROUTER_PAD), F32)],
        compiler_params=_cparams(("arbitrary",), 56),
    )(x, conv, na, hg, w_bf16, mod, g_ffn.reshape(1, d), mod, mod, w_router_hi, w_router_lo, b_router)


def _moe_kernel(be_ref, nu_ref, xs_ref, w1_ref, w3_ref, w2_ref, sw_ref, o_ref, w1b, w3b, w2b):
    i = pl.program_id(0)
    e = be_ref[i]
    prev = be_ref[jnp.maximum(i - 1, 0)]

    @pl.when(jnp.logical_or(i == 0, e != prev))
    def _():
        w1b[...] = w1_ref[...].astype(BF16)
        w3b[...] = w3_ref[...].astype(BF16)
        w2b[...] = w2_ref[...].astype(BF16)

    @pl.when(i < nu_ref[0])
    def _():
        x = xs_ref[...]
        a = (_silu(_dot(x, w1b[...])) * _dot(x, w3b[...])).astype(BF16)
        o_ref[...] = (_dot(a, w2b[...]) * sw_ref[...]).astype(o_ref.dtype)

    @pl.when(i >= nu_ref[0])
    def _():
        o_ref[...] = jnp.zeros_like(o_ref)


def _moe_experts(xs, slot_w, blk_e, nused, w1, w3, w2, l):
    p, d = xs.shape
    de = w1.shape[3]
    bm = MOE_BM
    grid_spec = pltpu.PrefetchScalarGridSpec(
        num_scalar_prefetch=2,
        grid=(p // bm,),
        in_specs=[pl.BlockSpec((bm, d), lambda i, be, nu: (i, 0)),
                  pl.BlockSpec((None, None, d, de), lambda i, be, nu: (l, be[i], 0, 0)),
                  pl.BlockSpec((None, None, d, de), lambda i, be, nu: (l, be[i], 0, 0)),
                  pl.BlockSpec((None, None, de, d), lambda i, be, nu: (l, be[i], 0, 0)),
                  pl.BlockSpec((bm, 1), lambda i, be, nu: (i, 0))],
        out_specs=pl.BlockSpec((bm, d), lambda i, be, nu: (i, 0)),
        scratch_shapes=[pltpu.VMEM((d, de), BF16), pltpu.VMEM((d, de), BF16), pltpu.VMEM((de, d), BF16)],
    )
    return pl.pallas_call(
        _moe_kernel,
        grid_spec=grid_spec,
        out_shape=jax.ShapeDtypeStruct((p, d), BF16),
        compiler_params=_cparams(("arbitrary",), 48),
    )(blk_e, nused, xs, w1, w3, w2, slot_w.reshape(p, 1))


def _rows(a, idx):
    return a.at[idx].get(mode="promise_in_bounds")


def _route_meta(route, n):
    i32 = jnp.int32
    eid = route[:, 0:TOP_K].astype(i32).reshape(-1)
    wt = route[:, TOP_K:2 * TOP_K].reshape(-1)
    a = n * TOP_K
    bm = MOE_BM
    nblk = -(-a // bm) + N_EXPERTS
    p = nblk * bm
    experts = jnp.arange(N_EXPERTS, dtype=i32)[None, :]
    ja = jnp.arange(a, dtype=i32)
    se, order, wsort = lax.sort((eid, ja, wt), num_keys=1, is_stable=True)
    cnt = jnp.sum((eid[:, None] == experts).astype(i32), axis=0)
    pcnt = (cnt + bm - 1) // bm * bm
    pend = jnp.cumsum(pcnt)
    pstart = pend - pcnt
    end = jnp.cumsum(cnt)
    start = end - cnt
    off = pstart - start
    d_off = off - jnp.concatenate([jnp.zeros((1,), i32), off[:-1]])
    dst_sorted = ja + jnp.sum(jnp.where(ja[:, None] >= start[None, :], d_off[None, :], 0), axis=1)
    _, pos = lax.sort((order, dst_sorted), num_keys=1)
    jp = jnp.arange(p, dtype=i32)
    in_or_after = jp[:, None] >= pstart[None, :]
    src = jp - jnp.sum(jnp.where(in_or_after, d_off[None, :], 0), axis=1)
    valid = src < jnp.sum(jnp.where(in_or_after, cnt[None, :], 0), axis=1)
    src = jnp.where(valid, src, jp % a)
    slot_tok = _rows(order, src) // TOP_K
    slot_w = jnp.where(valid, _rows(wsort, src), 0.0)
    jb = jnp.arange(nblk, dtype=i32) * bm
    blk_e = jnp.minimum(jnp.sum((jb[:, None] >= pend[None, :]).astype(i32), axis=1), N_EXPERTS - 1)
    nused = (pend[-1:] // bm).astype(i32)
    return slot_tok, slot_w, pos.reshape(n, TOP_K), blk_e, nused


def _combine_kernel(x_ref, y_ref, ga_ref, gf_ref, o_ref, *, tiles_per_batch, nbatch, final):
    r = jnp.minimum(pl.program_id(0) // tiles_per_batch, nbatch)
    d = x_ref.shape[1]
    y = y_ref[:, 0:d].astype(F32)
    for k in range(1, TOP_K):
        y = y + y_ref[:, k * d:(k + 1) * d].astype(F32)
    xn = x_ref[...] + ga_ref[pl.ds(r, 1), :] * y
    if final:
        xn = xn * lax.rsqrt(jnp.mean(xn * xn, axis=-1, keepdims=True) + EPS) * gf_ref[...]
    o_ref[...] = xn


def _combine(x, y, mod, l, g_final, n_rows, nbatch, lat_rows, final):
    d = x.shape[1]
    tm = _pick_tile(OUT_TM, lat_rows // nbatch, n_rows)
    kern = functools.partial(_combine_kernel, tiles_per_batch=lat_rows // nbatch // tm, nbatch=nbatch,
                             final=final)
    row = pl.BlockSpec((tm, d), lambda i: (i, 0))
    return pl.pallas_call(
        kern,
        grid=(n_rows // tm,),
        in_specs=[row, pl.BlockSpec((tm, TOP_K * d), lambda i: (i, 0)),
                  pl.BlockSpec((None, SUBLANES, d), lambda i: (l, 0, 5)),
                  pl.BlockSpec((1, d), lambda i: (0, 0))],
        out_specs=row,
        out_shape=jax.ShapeDtypeStruct((n_rows, d), F32),
        compiler_params=_cparams(("arbitrary",), 40),
    )(x, y, mod, g_final.reshape(1, d))


def kernel(x, c, ctx, c_ctx, w_ada, b_ada, g_mix, g_ffn, w_in, conv_w, conv_b, conv_ln_g, conv_ln_b,
           na_rpb, hgrn_lb, hgrn_norm_g, w_out, w_router_group, b_router_group, w_router_expert,
           b_router_expert, w_exp_gate, w_exp_up, w_exp_down, g_final):
    nb, seq, d = x.shape
    ctx_len = ctx.shape[1]
    depth = w_ada.shape[0]
    lat_rows = nb * seq
    n_all = lat_rows + nb * ctx_len
    conv_ch = conv_w.shape[2]
    off_na = 2 * conv_ch
    off_hg = off_na + 3 * NA_HEADS * NA_HD
    rows = seq // GRID_W
    assert nb < SUBLANES and rows % NA_ROWS == 0 and rows >= NA_KEY_ROWS

    lbs = jnp.cumsum(jax.nn.softmax(hgrn_lb.astype(F32), axis=0), axis=0)
    lbs = lbs - lbs[:1]

    cond = jnp.concatenate([c, c_ctx[None, :], jnp.zeros((SUBLANES - nb - 1, d), F32)], axis=0)
    mod = _ada_mod(cond, w_ada, b_ada)

    xs = jnp.concatenate([x.reshape(lat_rows, d), ctx.reshape(nb * ctx_len, d)], axis=0)
    w_in_b = w_in.astype(BF16)
    w_out_b = w_out.astype(BF16)
    for l in range(depth):
        with_ctx = l < depth - 1
        n_act = n_all if with_ctx else lat_rows
        u = _norm_in(xs, g_mix[l], mod, w_in_b, l, nb, lat_rows)

        conv = _conv_module(u, conv_w[l], conv_b[l], conv_ln_g[l], conv_ln_b[l], n_act, lat_rows, seq)
        na = _na_latent(u, _na_bias(na_rpb[l], rows), nb, seq, lat_rows, off_na)
        if with_ctx:
            na = jnp.concatenate([na, _ctx_attn(u, nb, ctx_len, lat_rows, off_na)], axis=0)
        hg = _hgrn(u, lbs[l], hgrn_norm_g[l], nb, seq, ctx_len, lat_rows, off_hg)

        w_router = jnp.concatenate(
            [w_router_group[l], w_router_expert[l],
             jnp.zeros((d, ROUTER_PAD - N_GROUPS - N_EXPERTS), F32)], axis=1)
        b_router = jnp.concatenate(
            [b_router_group[l], b_router_expert[l],
             jnp.zeros((ROUTER_PAD - N_GROUPS - N_EXPERTS,), F32)]).reshape(1, ROUTER_PAD)
        x_mid, h, route = _out_proj(xs, conv, na, hg, w_out_b, l, mod, g_ffn[l],
                                    w_router, b_router, n_act, nb, lat_rows)

        slot_tok, slot_w, pos, blk_e, nused = _route_meta(route, n_act)
        ys = _moe_experts(_rows(h, slot_tok), slot_w, blk_e, nused, w_exp_gate, w_exp_up, w_exp_down, l)
        y = _rows(ys, pos.reshape(-1)).reshape(n_act, TOP_K * d)
        xs = _combine(x_mid, y, mod, l, g_final, n_act, nb, lat_rows, final=not with_ctx)
    return xs.reshape(nb, seq, d)
```

```python
import functools

import numpy as np
import jax
import jax.numpy as jnp
from jax import lax
from jax.experimental import pallas as pl
from jax.experimental.pallas import tpu as pltpu

F32 = jnp.float32
BF16 = jnp.bfloat16

EPS = 1e-6
NEG_INF = -1e30
LOG2_E = 1.4426950408889634

GRID_W = 64
CONV_K = 31
NA_HEADS = 16
NA_HD = 64
NA_KH = 8
NA_KW = 16
HG_HEADS = 4
HG_DK = 128
N_GROUPS = 4
EXP_PER_GROUP = 8
N_EXPERTS = N_GROUPS * EXP_PER_GROUP
TOP_K = 2

LANES = 128
SUBLANES = 8
VMEM_BYTES = 64 * 1024 * 1024

ROW_TILE = 256
IN_TM = 1024
IN_TN = 512
OUT_TM = 512
NA_ROWS = 8
NA_KEY_ROWS = 16
HG_BLOCK = 256
MOE_BM = 512
ROUTER_PAD = LANES
HALO = 16


def _pick_tile(pref, *extents):
    t = pref
    while t > ROW_TILE and any(e % t for e in extents):
        t //= 2
    assert all(e % t == 0 for e in extents)
    return t


def _cparams(sem, vmem_mb):
    return pltpu.CompilerParams(dimension_semantics=sem, vmem_limit_bytes=vmem_mb * 1024 * 1024)


def _dot(a, b):
    return jnp.dot(a, b, preferred_element_type=F32)


def _dot_nt(a, b):
    return lax.dot_general(a, b, (((1,), (1,)), ((), ())), preferred_element_type=F32)


def _dot_tn(a, b):
    return lax.dot_general(a, b, (((0,), (0,)), ((), ())), preferred_element_type=F32)


def _sigmoid(x):
    return 1.0 / (1.0 + jnp.exp(-x))


def _silu(x):
    return x * _sigmoid(x)


def _ada_kernel(c_ref, w_ref, b_ref, o_ref):
    sc = _silu(c_ref[...])
    o_ref[0] = jnp.dot(sc, w_ref[0], precision=lax.Precision.HIGHEST,
                       preferred_element_type=F32) + b_ref[0]


def _ada_mod(cond, w_ada, b_ada):
    depth, d, n = w_ada.shape
    tn = 1024
    return pl.pallas_call(
        _ada_kernel,
        grid=(depth, n // tn),
        in_specs=[
            pl.BlockSpec((SUBLANES, d), lambda l, j: (0, 0)),
            pl.BlockSpec((1, d, tn), lambda l, j: (l, 0, j)),
            pl.BlockSpec((1, 1, tn), lambda l, j: (l, 0, j)),
        ],
        out_specs=pl.BlockSpec((1, SUBLANES, tn), lambda l, j: (l, 0, j)),
        out_shape=jax.ShapeDtypeStruct((depth, SUBLANES, n), F32),
        compiler_params=_cparams(("arbitrary", "arbitrary"), 40),
    )(cond, w_ada, b_ada.reshape(depth, 1, n))


def _rms_mod(x, g, scale, shift):
    y = x * lax.rsqrt(jnp.mean(x * x, axis=-1, keepdims=True) + EPS)
    return (y * g) * (1.0 + scale) + shift


def _norm_in_kernel(x_ref, g_ref, sh_ref, sc_ref, w_ref, o_ref, h_ref, *, tiles_per_batch, nbatch):
    i = pl.program_id(0)

    @pl.when(pl.program_id(1) == 0)
    def _():
        r = jnp.minimum(i // tiles_per_batch, nbatch)
        h = _rms_mod(x_ref[...], g_ref[...], sc_ref[pl.ds(r, 1), :], sh_ref[pl.ds(r, 1), :])
        h_ref[...] = h.astype(BF16)

    o_ref[...] = _dot(h_ref[...], w_ref[...])


def _norm_in(x, g, mod, w_bf16, l, nbatch, lat_rows):
    n, d = x.shape
    nout = w_bf16.shape[2]
    tm, tn = _pick_tile(IN_TM, lat_rows // nbatch, n), IN_TN
    kern = functools.partial(_norm_in_kernel, tiles_per_batch=lat_rows // nbatch // tm, nbatch=nbatch)
    return pl.pallas_call(
        kern,
        grid=(n // tm, nout // tn),
        in_specs=[
            pl.BlockSpec((tm, d), lambda i, j: (i, 0)),
            pl.BlockSpec((1, d), lambda i, j: (0, 0)),
            pl.BlockSpec((None, SUBLANES, d), lambda i, j: (l, 0, 0)),
            pl.BlockSpec((None, SUBLANES, d), lambda i, j: (l, 0, 1)),
            pl.BlockSpec((None, d, tn), lambda i, j: (l, 0, j)),
        ],
        out_specs=pl.BlockSpec((tm, tn), lambda i, j: (i, j)),
        out_shape=jax.ShapeDtypeStruct((n, nout), F32),
        scratch_shapes=[pltpu.VMEM((tm, d), BF16)],
        compiler_params=_cparams(("arbitrary", "arbitrary"), 48),
    )(x, g.reshape(1, d), mod, mod, w_bf16)


def _conv_kernel(ap_ref, gp_ref, a_ref, gt_ref, an_ref, gn_ref, w_ref, b_ref, lg_ref, lb_ref,
                 o_ref, buf_ref, acc_ref, *, lat_tiles, tiles_per_seq):
    i = pl.program_id(0)
    tc, ch = a_ref.shape
    is_lat = i < lat_tiles
    pos = i % tiles_per_seq
    first = jnp.logical_or(jnp.logical_not(is_lat), pos == 0)
    last = jnp.logical_or(jnp.logical_not(is_lat), pos == tiles_per_seq - 1)

    buf_ref[0:HALO] = jnp.where(first, 0.0, ap_ref[...] * _sigmoid(gp_ref[...]))
    buf_ref[HALO:HALO + tc] = a_ref[...] * _sigmoid(gt_ref[...])
    buf_ref[HALO + tc:2 * HALO + tc] = jnp.where(last, 0.0, an_ref[...] * _sigmoid(gn_ref[...]))

    rows = 64
    base = HALO - CONV_K // 2
    for c in range(ch // LANES):
        cs = slice(c * LANES, (c + 1) * LANES)
        for r in range(tc // rows):
            acc = None
            for res in range(SUBLANES):
                y = None
                for k in range(CONV_K):
                    if (base + k) % SUBLANES != res:
                        continue
                    lo = r * rows + (base + k) // SUBLANES * SUBLANES
                    term = w_ref[k:k + 1, cs] * buf_ref[lo:lo + rows + SUBLANES, cs]
                    y = term if y is None else y + term
                if y is not None:
                    y = y[res:res + rows]
                    acc = y if acc is None else acc + y
            acc_ref[r * rows:(r + 1) * rows, cs] = acc

    h = acc_ref[...] + b_ref[...]
    mu = jnp.mean(h, axis=-1, keepdims=True)
    var = jnp.mean(jnp.square(h - mu), axis=-1, keepdims=True)
    y = (h - mu) * lax.rsqrt(var + EPS) * lg_ref[...] + lb_ref[...]
    o_ref[...] = _silu(y).astype(o_ref.dtype)


def _conv_module(u, w_dw, b_dw, ln_g, ln_b, n_rows, lat_rows, seq):
    ch = w_dw.shape[1]
    tc = ROW_TILE
    per = tc // HALO
    nh = u.shape[0] // HALO
    kern = functools.partial(_conv_kernel, lat_tiles=lat_rows // tc, tiles_per_seq=seq // tc)
    prev_map = lambda c: (lambda i: (jnp.maximum(i * per - 1, 0), c))
    next_map = lambda c: (lambda i: (jnp.minimum((i + 1) * per, nh - 1), c))
    vec = lambda a: a.reshape(1, ch)
    return pl.pallas_call(
        kern,
        grid=(n_rows // tc,),
        in_specs=[
            pl.BlockSpec((HALO, ch), prev_map(0)),
            pl.BlockSpec((HALO, ch), prev_map(1)),
            pl.BlockSpec((tc, ch), lambda i: (i, 0)),
            pl.BlockSpec((tc, ch), lambda i: (i, 1)),
            pl.BlockSpec((HALO, ch), next_map(0)),
            pl.BlockSpec((HALO, ch), next_map(1)),
            pl.BlockSpec((CONV_K, ch), lambda i: (0, 0)),
            pl.BlockSpec((1, ch), lambda i: (0, 0)),
            pl.BlockSpec((1, ch), lambda i: (0, 0)),
            pl.BlockSpec((1, ch), lambda i: (0, 0)),
        ],
        out_specs=pl.BlockSpec((tc, ch), lambda i: (i, 0)),
        out_shape=jax.ShapeDtypeStruct((n_rows, ch), BF16),
        scratch_shapes=[pltpu.VMEM((tc + 2 * HALO, ch), F32), pltpu.VMEM((tc, ch), F32)],
        compiler_params=_cparams(("arbitrary",), 16),
    )(u, u, u, u, u, u, w_dw, vec(b_dw), vec(ln_g), vec(ln_b))


def _na_bias_tables(rows):
    groups = rows // NA_ROWS
    reps = [0, min(1, groups - 1), groups - 1]
    out = []
    for g in reps:
        start = int(np.clip(NA_ROWS * g - NA_KH // 2, 0, rows - NA_KEY_ROWS))
        per_row = []
        for i in range(NA_ROWS):
            r = NA_ROWS * g + i
            sr = int(np.clip(r - NA_KH // 2, 0, rows - NA_KH))
            per_row.append((sr - start, sr - r + NA_KH - 1))
        out.append(per_row)
    return out


def _na_bias(rpb, rows):
    nh = rpb.shape[0]
    ndr, ndc = 2 * NA_KH - 1, 2 * NA_KW - 1
    period = 2 * GRID_W - 1
    pad = GRID_W - NA_KW
    vp = jnp.pad(rpb.astype(F32), ((0, 0), (0, 0), (pad, period - ndc - pad)))
    hank = jnp.tile(vp, (1, 1, GRID_W + 1))[:, :, :GRID_W * (period + 1)]
    hank = hank.reshape(nh, ndr, GRID_W, period + 1)[..., :GRID_W]
    toe = hank[:, :, ::-1, :]
    c = np.arange(GRID_W)[:, None]
    j = np.arange(GRID_W)[None, :]
    ws = np.clip(c - NA_KW // 2, 0, GRID_W - NA_KW)
    col_ok = (j >= ws) & (j < ws + NA_KW)
    toe = jnp.where(col_ok[None, None], toe, NEG_INF)
    flat = jnp.transpose(toe, (0, 2, 1, 3)).reshape(nh, GRID_W, ndr * GRID_W)
    nk = NA_KEY_ROWS * GRID_W
    lpad = NA_ROWS * GRID_W
    total = 2 * nk

    def padded(shift):
        return jnp.pad(flat, ((0, 0), (0, 0), (lpad - shift, total - flat.shape[2] - lpad + shift)))

    tab = jnp.stack([padded(0), padded(GRID_W)])
    tab = tab.reshape(2, nh, GRID_W, total // LANES, LANES).transpose(0, 1, 3, 2, 4)
    tables = _na_bias_tables(rows)
    mask = np.full((len(tables), NA_ROWS, nk), NEG_INF, np.float32)
    dvals = []
    for cls, per_row in enumerate(tables):
        dvals.append(per_row[0][1] - per_row[0][0])
        for i, (off, lo) in enumerate(per_row):
            assert lo - off == dvals[-1] - i and -NA_ROWS <= lo - off < NA_ROWS
            mask[cls, i, off * GRID_W:(off + NA_KH) * GRID_W] = 0.0
    assert len({d % 2 for d in dvals}) == 1
    return tab, jnp.asarray(mask), tuple(dvals)


def _softmax_pv(s_parts, v_parts):
    m = jnp.max(functools.reduce(jnp.maximum, [_lane_fold(s, jnp.maximum) for s in s_parts]),
                axis=-1, keepdims=True)
    acc, l = None, None
    for s, v in zip(s_parts, v_parts):
        p = jnp.exp(s - m)
        ps = _lane_fold(p, jnp.add)
        pv = _dot(p.astype(BF16), v)
        l = ps if l is None else l + ps
        acc = pv if acc is None else acc + pv
    return acc / jnp.sum(l, axis=-1, keepdims=True)


def _lane_fold(x, op):
    return functools.reduce(op, [x[:, t * LANES:(t + 1) * LANES] for t in range(x.shape[1] // LANES)])


def _na_kernel(q_ref, k0, k1, k2, k3, v0, v1, v2, v3, kc_ref, vc_ref, tab_ref, mask_ref, o_ref, *,
               groups, dvals):
    g = pl.program_id(1)
    dcls = jnp.where(g == 0, dvals[0], jnp.where(g == groups - 1, dvals[2], dvals[1]))
    q2 = q_ref[...] * (NA_HD ** -0.5)
    lane = lax.broadcasted_iota(jnp.int32, q2.shape, 1)
    ks = [k[...].astype(BF16) for k in (k0, k1, k2, k3)]
    vs = [v[...].astype(BF16) for v in (v0, v1, v2, v3)]
    kc = kc_ref[...].astype(BF16)
    vc = vc_ref[...].astype(BF16)
    kw = ks[0].shape[0]
    bpk = kw // LANES
    pad_blocks = NA_ROWS * GRID_W // LANES
    outs = []
    for a in range(2):
        sel = (lane < NA_HD) if a == 0 else (lane >= NA_HD)
        qa = jnp.where(sel, q2, 0.0).astype(BF16)
        s_parts = []
        for m, k in enumerate(ks):
            s = _dot_nt(qa, k)
            row_blocks = []
            for i in range(NA_ROWS):
                copy = (dvals[0] - i) % 2
                first = (dcls - i - copy + 2 * pad_blocks) // 2 + m * bpk
                bias = jnp.concatenate([tab_ref[copy, a, first + t] for t in range(bpk)], axis=1)
                bias = bias + mask_ref[i:i + 1, m * kw:(m + 1) * kw]
                row_blocks.append(s[i * GRID_W:(i + 1) * GRID_W, :] + bias)
            s_parts.append(jnp.concatenate(row_blocks, axis=0))
        s_parts.append(_dot_nt(qa, kc))
        outs.append(_softmax_pv(s_parts, vs + [vc]))
    o_ref[...] = jnp.where(lane < NA_HD, outs[0], outs[1]).astype(o_ref.dtype)


def _na_latent(u, bias, nbatch, seq, lat_rows, off_na):
    tab, mask, dvals = bias
    rows = seq // GRID_W
    groups = rows // NA_ROWS
    nq = NA_ROWS * GRID_W
    kblk = ROW_TILE
    nkb = NA_KEY_ROWS * GRID_W // kblk
    assert nkb == 4
    qcol = off_na // LANES
    heads2 = NA_HEADS * NA_HD // LANES
    kcol, vcol = qcol + heads2, qcol + 2 * heads2
    kb_per_batch = seq // kblk
    kb_per_grow = GRID_W * NA_ROWS // kblk
    lat_kb = lat_rows // kblk

    def kmap(col, m):
        def f(h, g, b):
            st = jnp.clip(g * kb_per_grow - (NA_KH // 2) * GRID_W // kblk, 0, kb_per_batch - nkb)
            return (b * kb_per_batch + st + m, col + h)
        return f

    def mask_map(h, g, b):
        return (jnp.where(g == 0, 0, jnp.where(g == groups - 1, 2, 1)), 0, 0)

    in_specs = [pl.BlockSpec((nq, LANES), lambda h, g, b: (b * groups + g, qcol + h))]
    in_specs += [pl.BlockSpec((kblk, LANES), kmap(kcol, m)) for m in range(nkb)]
    in_specs += [pl.BlockSpec((kblk, LANES), kmap(vcol, m)) for m in range(nkb)]
    in_specs += [pl.BlockSpec((ROW_TILE, LANES), lambda h, g, b: (lat_kb + b, kcol + h)),
                 pl.BlockSpec((ROW_TILE, LANES), lambda h, g, b: (lat_kb + b, vcol + h)),
                 pl.BlockSpec((2, 2) + tab.shape[2:], lambda h, g, b: (0, h, 0, 0, 0)),
                 pl.BlockSpec((None,) + mask.shape[1:], mask_map)]
    return pl.pallas_call(
        functools.partial(_na_kernel, groups=groups, dvals=dvals),
        grid=(heads2, groups, nbatch),
        in_specs=in_specs,
        out_specs=pl.BlockSpec((nq, LANES), lambda h, g, b: (b * groups + g, h)),
        out_shape=jax.ShapeDtypeStruct((lat_rows, NA_HEADS * NA_HD), BF16),
        compiler_params=_cparams(("arbitrary", "arbitrary", "arbitrary"), 40),
    )(*([u] * 11), tab, mask)


def _ctx_attn_kernel(q_ref, k_ref, v_ref, o_ref):
    q2 = q_ref[...] * (NA_HD ** -0.5)
    lane = lax.broadcasted_iota(jnp.int32, q2.shape, 1)
    k = k_ref[...].astype(BF16)
    v = v_ref[...].astype(BF16)
    outs = []
    for a in range(2):
        sel = (lane < NA_HD) if a == 0 else (lane >= NA_HD)
        qa = jnp.where(sel, q2, 0.0).astype(BF16)
        outs.append(_softmax_pv([_dot_nt(qa, k)], [v]))
    o_ref[...] = jnp.where(lane < NA_HD, outs[0], outs[1]).astype(o_ref.dtype)


def _ctx_attn(u, nbatch, ctx_len, lat_rows, off_na):
    assert ctx_len == ROW_TILE
    qcol = off_na // LANES
    heads2 = NA_HEADS * NA_HD // LANES
    base = lat_rows // ROW_TILE
    spec = lambda col: pl.BlockSpec((ROW_TILE, LANES), lambda b, h: (base + b, col + h))
    return pl.pallas_call(
        _ctx_attn_kernel,
        grid=(nbatch, heads2),
        in_specs=[spec(qcol), spec(qcol + heads2), spec(qcol + 2 * heads2)],
        out_specs=pl.BlockSpec((ROW_TILE, LANES), lambda b, h: (b, h)),
        out_shape=jax.ShapeDtypeStruct((nbatch * ctx_len, NA_HEADS * NA_HD), BF16),
        compiler_params=_cparams(("arbitrary", "arbitrary"), 16),
    )(u, u, u)


def _hg_level_map(rev):
    size = HG_BLOCK // 2
    t = np.arange(size)[:, None]
    s = np.arange(size)[None, :]
    x = t ^ s
    lvl = np.where(x > 0, np.frexp(np.maximum(x, 1))[1] - 1, -1)
    causal = (s < t) if not rev else (s > t)
    out = np.where(causal, lvl, -1)
    out = np.where(t == s, int(np.log2(size)), out)
    return out.astype(np.int32)


def _hg_tri(rev):
    t = np.arange(HG_BLOCK)[:, None]
    s = np.arange(HG_BLOCK)[None, :]
    return ((s <= t) if not rev else (s >= t)).astype(np.float32)


def _hg_anchor(b3, m, rev):
    nv = b3.shape[0]
    if m >= SUBLANES:
        w = m // SUBLANES
        b4 = b3.reshape(nv // (2 * w), 2 * w, SUBLANES, LANES)
        a = b4[:, w:w + 1, 0:1, :] if rev else b4[:, w - 1:w, SUBLANES - 1:SUBLANES, :]
        return jnp.broadcast_to(a, b4.shape).reshape(b3.shape)
    sub = lax.broadcasted_iota(jnp.int32, b3.shape, 1)
    out = None
    for g in range(SUBLANES // (2 * m)):
        idx = g * 2 * m + (m if rev else m - 1)
        a = jnp.broadcast_to(b3[:, idx:idx + 1, :], b3.shape)
        out = a if out is None else jnp.where(sub >= g * 2 * m, a, out)
    return out


def _neg_abs(x):
    bits = lax.bitcast_convert_type(x, jnp.uint32) | jnp.uint32(0x80000000)
    return lax.bitcast_convert_type(bits, F32)


def _hg_pick(q3, k3, m, rev):
    nv = q3.shape[0]
    if m >= SUBLANES:
        w = m // SUBLANES
        shape4 = (nv // (2 * w), 2 * w, SUBLANES, LANES)
        q4, k4 = q3.reshape(shape4), k3.reshape(shape4)
        lower, upper = (q4, k4) if rev else (k4, q4)
        return jnp.concatenate([lower[:, :w], upper[:, w:]], axis=1).reshape(q3.shape)
    upper_rows = (lax.broadcasted_iota(jnp.int32, q3.shape, 1) & m) != 0
    return jnp.where(upper_rows, k3 if rev else q3, q3 if rev else k3)


def _hg_block(q, v, z, alog, clog, oml, tri, lv, st, rev):
    n = q.shape[0]
    half = n // 2
    nlev = int(np.log2(n))
    q = _silu(q)
    t = jnp.exp(-jnp.abs(z))
    lsig = jnp.minimum(z, 0.0) - jnp.log(1.0 + t)
    cc = clog + lsig
    logf = jnp.maximum(alog, cc) + jnp.log(1.0 + jnp.exp(-jnp.abs(alog - cc)))
    kk = oml * jnp.where(z >= 0.0, t, 1.0) / (1.0 + t)

    hi = logf.astype(BF16)
    r1 = logf - hi.astype(F32)
    mid = r1.astype(BF16)
    lo = (r1 - mid.astype(F32)).astype(BF16)
    b = (_dot(tri, hi) + _dot(tri, mid) + _dot(tri, lo)) * LOG2_E

    shape3 = (n // SUBLANES, SUBLANES, LANES)
    b3, q3, k3 = b.reshape(shape3), q.reshape(shape3), kk.reshape(shape3)
    halves = (slice(0, half), slice(half, n))
    qb, kb = q.astype(BF16), kk.astype(BF16)
    acc = [jnp.where(lv == nlev - 1, _dot_nt(qb[hs], kb[hs]), 0.0) for hs in halves]
    for lev in range(nlev - 1):
        m = 1 << lev
        e = jnp.exp2(_neg_abs(b3 - _hg_anchor(b3, m, rev)))
        w = (_hg_pick(q3, k3, m, rev) * e).reshape(n, LANES).astype(BF16)
        acc = [jnp.where(lv == lev, _dot_nt(w[hs], w[hs]), a) for hs, a in zip(halves, acc)]
    first, second = (halves[1], halves[0]) if rev else halves
    e = jnp.exp2(_neg_abs(b - (b[half:half + 1, :] if rev else b[half - 1:half, :])))
    top = _dot_nt((q[second] * e[second]).astype(BF16), (kk[first] * e[first]).astype(BF16))

    vb = v.astype(BF16)
    a0, a1 = acc[0].astype(BF16), acc[1].astype(BF16)
    tb = top.astype(BF16)
    if rev:
        o_lo = _dot(jnp.concatenate([a0, tb], axis=1), vb)
        o_hi = _dot(a1, vb[halves[1]])
    else:
        o_lo = _dot(a0, vb[halves[0]])
        o_hi = _dot(jnp.concatenate([tb, a1], axis=1), vb)
    b_last = b[0:1, :] if rev else b[n - 1:n, :]
    qh = (q * jnp.exp2(b)).astype(BF16)
    o = jnp.concatenate([o_lo, o_hi], axis=0) + _dot_nt(qh, st.astype(BF16))
    kh = (kk * jnp.exp2(b_last - b)).astype(BF16)
    st_new = st * jnp.exp2(b_last) + _dot_tn(vb, kh)
    return o, st_new


def _hg_fwd_kernel(q_ref, v_ref, z_ref, al_ref, cl_ref, om_ref, tri_ref, lv_ref, o_ref, st_ref):
    @pl.when(pl.program_id(1) == 0)
    def _():
        st_ref[...] = jnp.zeros_like(st_ref)

    tri = tri_ref[...]
    lv = lv_ref[...]
    for h in range(HG_HEADS):
        hs = slice(h * HG_DK, (h + 1) * HG_DK)
        o, st = _hg_block(q_ref[:, hs], v_ref[:, hs], z_ref[:, hs], al_ref[:, hs], cl_ref[:, hs],
                          om_ref[:, hs], tri, lv, st_ref[h], False)
        o_ref[:, hs] = o
        st_ref[h] = st


def _hg_bwd_kernel(q_ref, v_ref, z_ref, g_ref, of_ref, al_ref, cl_ref, om_ref, ng_ref, tri_ref, lv_ref,
                   o_ref, st_ref):
    @pl.when(pl.program_id(1) == 0)
    def _():
        st_ref[...] = jnp.zeros_like(st_ref)

    tri = tri_ref[...]
    lv = lv_ref[...]
    for h in range(HG_HEADS):
        hs = slice(h * HG_DK, (h + 1) * HG_DK)
        o, st = _hg_block(q_ref[:, hs], v_ref[:, hs], z_ref[:, hs], al_ref[:, hs], cl_ref[:, hs],
                          om_ref[:, hs], tri, lv, st_ref[h], True)
        st_ref[h] = st
        t = of_ref[:, hs] + o
        y = t * lax.rsqrt(jnp.mean(t * t, axis=-1, keepdims=True) + EPS)
        o_ref[:, hs] = (y * ng_ref[:, hs] * _silu(g_ref[:, hs])).astype(o_ref.dtype)


def _hgrn(u, lb, norm_g, nbatch, seq, ctx_len, lat_rows, off_hg):
    assert ctx_len == HG_BLOCK
    n = u.shape[0]
    hd = HG_HEADS * HG_DK
    col = off_hg // hd
    per = seq // HG_BLOCK
    lat_blocks = lat_rows // HG_BLOCK
    lbf = lb.astype(F32)
    alog, clog, oml = jnp.log(lbf), jnp.log1p(-lbf), 1.0 - lbf

    def fmap(c):
        return lambda b, j: (jnp.where(j == 0, lat_blocks + b, b * per + j - 1), c)

    def bmap(c):
        return lambda b, j: (jnp.where(j == 0, lat_blocks + b, b * per + per - j), c)

    const = lambda shape: pl.BlockSpec(shape, lambda b, j: (0, 0))
    grid = (nbatch, per + 1)
    vec = lambda a: a.reshape(1, hd)
    o_f = pl.pallas_call(
        _hg_fwd_kernel,
        grid=grid,
        in_specs=[pl.BlockSpec((HG_BLOCK, hd), fmap(col)), pl.BlockSpec((HG_BLOCK, hd), fmap(col + 1)),
                  pl.BlockSpec((HG_BLOCK, hd), fmap(col + 2)),
                  const((1, hd)), const((1, hd)), const((1, hd)),
                  const((HG_BLOCK, HG_BLOCK)), const((HG_BLOCK // 2, HG_BLOCK // 2))],
        out_specs=pl.BlockSpec((HG_BLOCK, hd), fmap(0)),
        out_shape=jax.ShapeDtypeStruct((n, hd), F32),
        scratch_shapes=[pltpu.VMEM((HG_HEADS, HG_DK, HG_DK), F32)],
        compiler_params=_cparams(("arbitrary", "arbitrary"), 32),
    )(u, u, u, vec(alog[0]), vec(clog[0]), vec(oml[0]),
      jnp.asarray(_hg_tri(False), BF16), jnp.asarray(_hg_level_map(False)))
    return pl.pallas_call(
        _hg_bwd_kernel,
        grid=grid,
        in_specs=[pl.BlockSpec((HG_BLOCK, hd), bmap(col)), pl.BlockSpec((HG_BLOCK, hd), bmap(col + 1)),
                  pl.BlockSpec((HG_BLOCK, hd), bmap(col + 3)), pl.BlockSpec((HG_BLOCK, hd), bmap(col + 4)),
                  pl.BlockSpec((HG_BLOCK, hd), bmap(0)),
                  const((1, hd)), const((1, hd)), const((1, hd)), const((1, hd)),
                  const((HG_BLOCK, HG_BLOCK)), const((HG_BLOCK // 2, HG_BLOCK // 2))],
        out_specs=pl.BlockSpec((HG_BLOCK, hd), bmap(0)),
        out_shape=jax.ShapeDtypeStruct((n, hd), BF16),
        scratch_shapes=[pltpu.VMEM((HG_HEADS, HG_DK, HG_DK), F32)],
        compiler_params=_cparams(("arbitrary", "arbitrary"), 32),
    )(u, u, u, u, o_f, vec(alog[1]), vec(clog[1]), vec(oml[1]), vec(norm_g.astype(F32)),
      jnp.asarray(_hg_tri(True), BF16), jnp.asarray(_hg_level_map(True)))


def _out_kernel(x_ref, cv_ref, na_ref, hg_ref, w_ref, ga_ref, g2_ref, sh2_ref, s2_ref, wrh_ref, wrl_ref, br_ref,
                xo_ref, h_ref, rt_ref, *, tiles_per_batch, nbatch):
    r = jnp.minimum(pl.program_id(0) // tiles_per_batch, nbatch)
    c0 = cv_ref.shape[1]
    c1 = c0 + na_ref.shape[1]
    mix = (_dot(cv_ref[...], w_ref[0:c0, :]) + _dot(na_ref[...], w_ref[c0:c1, :])
           + _dot(hg_ref[...], w_ref[c1:, :]))
    xn = x_ref[...] + ga_ref[pl.ds(r, 1), :] * mix
    xo_ref[...] = xn
    h = _rms_mod(xn, g2_ref[...], s2_ref[pl.ds(r, 1), :], sh2_ref[pl.ds(r, 1), :])
    h_ref[...] = h.astype(h_ref.dtype)
    h_hi = h.astype(BF16)
    h_lo = (h - h_hi.astype(F32)).astype(BF16)
    logits = (_dot(h_hi, wrh_ref[...]) + (_dot(h_lo, wrh_ref[...]) + _dot(h_hi, wrl_ref[...]))
              + br_ref[...])
    rt_ref[...] = _route_rows(logits)


def _route_rows(lg):
    lane = lax.broadcasted_iota(jnp.int32, lg.shape, 1)
    big = jnp.int32(2 ** 30)
    low = jnp.float32(-3e38)

    def first_max(vals, mask):
        m = jnp.max(vals, axis=-1, keepdims=True)
        idx = jnp.min(jnp.where(jnp.logical_and(vals == m, mask), lane, big), axis=-1, keepdims=True)
        return m, idx

    gmask = lane < N_GROUPS
    gl = jnp.where(gmask, lg, low)
    gm, grp = first_max(gl, gmask)
    p_grp = 1.0 / jnp.sum(jnp.where(gmask, jnp.exp(gl - gm), 0.0), axis=-1, keepdims=True)
    lo = N_GROUPS + grp * EXP_PER_GROUP
    emask = jnp.logical_and(lane >= lo, lane < lo + EXP_PER_GROUP)
    el = jnp.where(emask, lg, low)
    m1, i1 = first_max(el, emask)
    emask2 = jnp.logical_and(emask, lane != i1)
    el2 = jnp.where(emask2, lg, low)
    m2, i2 = first_max(el2, emask2)
    t = jnp.exp(m2 - m1)
    w1 = p_grp / (1.0 + t)
    w2 = p_grp * t / (1.0 + t)
    e1 = (i1 - N_GROUPS).astype(F32)
    e2 = (i2 - N_GROUPS).astype(F32)
    return jnp.where(lane == 0, e1, jnp.where(lane == 1, e2, jnp.where(lane == 2, w1,
                     jnp.where(lane == 3, w2, 0.0))))


def _out_proj(x, conv, na, hg, w_bf16, l, mod, g_ffn, w_router, b_router, n_rows, nbatch, lat_rows):
    d = x.shape[1]
    w_router_hi = w_router.astype(BF16)
    w_router_lo = (w_router - w_router_hi.astype(F32)).astype(BF16)
    tm = _pick_tile(OUT_TM, lat_rows // nbatch, n_rows)
    kern = functools.partial(_out_kernel, tiles_per_batch=lat_rows // nbatch // tm, nbatch=nbatch)
    row = lambda w: pl.BlockSpec((tm, w), lambda i: (i, 0))
    const = lambda shape: pl.BlockSpec(shape, lambda i: (0, 0))
    modc = lambda c: pl.BlockSpec((None, SUBLANES, d), lambda i: (l, 0, c))
    return pl.pallas_call(
        kern,
        grid=(n_rows // tm,),
        in_specs=[row(d), row(conv.shape[1]), row(na.shape[1]), row(hg.shape[1]),
                  pl.BlockSpec((None, d, d), lambda i: (l, 0, 0)),
                  modc(2),
                  const((1, d)),
                  modc(3),
                  modc(4),
                  const((d, ROUTER_PAD)), const((d, ROUTER_PAD)), const((1, ROUTER_PAD))],
        out_specs=[row(d), row(d), row(ROUTER_PAD)],
        out_shape=[jax.ShapeDtypeStruct((n_rows, d), F32), jax.ShapeDtypeStruct((n_rows, d), BF16),
                   jax.ShapeDtypeStruct((n_rows, ROUTER_PAD), F32)],
        compiler_params=_cparams(("arbitrary",), 56),
    )(x, conv, na, hg, w_bf16, mod, g_ffn.reshape(1, d), mod, mod, w_router_hi, w_router_lo, b_router)


def _moe_kernel(be_ref, nu_ref, xs_ref, w1_ref, w3_ref, w2_ref, sw_ref, o_ref, w1b, w3b, w2b):
    i = pl.program_id(0)
    e = be_ref[i]
    prev = be_ref[jnp.maximum(i - 1, 0)]

    @pl.when(jnp.logical_or(i == 0, e != prev))
    def _():
        w1b[...] = w1_ref[...].astype(BF16)
        w3b[...] = w3_ref[...].astype(BF16)
        w2b[...] = w2_ref[...].astype(BF16)

    @pl.when(i < nu_ref[0])
    def _():
        x = xs_ref[...]
        a = (_silu(_dot(x, w1b[...])) * _dot(x, w3b[...])).astype(BF16)
        o_ref[...] = (_dot(a, w2b[...]) * sw_ref[...]).astype(o_ref.dtype)

    @pl.when(i >= nu_ref[0])
    def _():
        o_ref[...] = jnp.zeros_like(o_ref)


def _moe_experts(xs, slot_w, blk_e, nused, w1, w3, w2, l):
    p, d = xs.shape
    de = w1.shape[3]
    bm = MOE_BM
    grid_spec = pltpu.PrefetchScalarGridSpec(
        num_scalar_prefetch=2,
        grid=(p // bm,),
        in_specs=[pl.BlockSpec((bm, d), lambda i, be, nu: (i, 0)),
                  pl.BlockSpec((None, None, d, de), lambda i, be, nu: (l, be[i], 0, 0)),
                  pl.BlockSpec((None, None, d, de), lambda i, be, nu: (l, be[i], 0, 0)),
                  pl.BlockSpec((None, None, de, d), lambda i, be, nu: (l, be[i], 0, 0)),
                  pl.BlockSpec((bm, 1), lambda i, be, nu: (i, 0))],
        out_specs=pl.BlockSpec((bm, d), lambda i, be, nu: (i, 0)),
        scratch_shapes=[pltpu.VMEM((d, de), BF16), pltpu.VMEM((d, de), BF16), pltpu.VMEM((de, d), BF16)],
    )
    return pl.pallas_call(
        _moe_kernel,
        grid_spec=grid_spec,
        out_shape=jax.ShapeDtypeStruct((p, d), BF16),
        compiler_params=_cparams(("arbitrary",), 48),
    )(blk_e, nused, xs, w1, w3, w2, slot_w.reshape(p, 1))


def _rows(a, idx):
    return a.at[idx].get(mode="promise_in_bounds")


def _route_meta(route, n):
    i32 = jnp.int32
    eid = route[:, 0:TOP_K].astype(i32).reshape(-1)
    wt = route[:, TOP_K:2 * TOP_K].reshape(-1)
    a = n * TOP_K
    bm = MOE_BM
    nblk = -(-a // bm) + N_EXPERTS
    p = nblk * bm
    experts = jnp.arange(N_EXPERTS, dtype=i32)[None, :]
    ja = jnp.arange(a, dtype=i32)
    se, order, wsort = lax.sort((eid, ja, wt), num_keys=1, is_stable=True)
    cnt = jnp.sum((eid[:, None] == experts).astype(i32), axis=0)
    pcnt = (cnt + bm - 1) // bm * bm
    pend = jnp.cumsum(pcnt)
    pstart = pend - pcnt
    end = jnp.cumsum(cnt)
    start = end - cnt
    off = pstart - start
    d_off = off - jnp.concatenate([jnp.zeros((1,), i32), off[:-1]])
    dst_sorted = ja + jnp.sum(jnp.where(ja[:, None] >= start[None, :], d_off[None, :], 0), axis=1)
    _, pos = lax.sort((order, dst_sorted), num_keys=1)
    jp = jnp.arange(p, dtype=i32)
    in_or_after = jp[:, None] >= pstart[None, :]
    src = jp - jnp.sum(jnp.where(in_or_after, d_off[None, :], 0), axis=1)
    valid = src < jnp.sum(jnp.where(in_or_after, cnt[None, :], 0), axis=1)
    src = jnp.where(valid, src, jp % a)
    slot_tok = _rows(order, src) // TOP_K
    slot_w = jnp.where(valid, _rows(wsort, src), 0.0)
    jb = jnp.arange(nblk, dtype=i32) * bm
    blk_e = jnp.minimum(jnp.sum((jb[:, None] >= pend[None, :]).astype(i32), axis=1), N_EXPERTS - 1)
    nused = (pend[-1:] // bm).astype(i32)
    return slot_tok, slot_w, pos.reshape(n, TOP_K), blk_e, nused


def _combine_kernel(x_ref, y0_ref, y1_ref, ga_ref, gf_ref, o_ref, *, tiles_per_batch, nbatch, final):
    r = jnp.minimum(pl.program_id(0) // tiles_per_batch, nbatch)
    xn = x_ref[...] + ga_ref[pl.ds(r, 1), :] * (y0_ref[...].astype(F32) + y1_ref[...].astype(F32))
    if final:
        xn = xn * lax.rsqrt(jnp.mean(xn * xn, axis=-1, keepdims=True) + EPS) * gf_ref[...]
    o_ref[...] = xn


def _combine(x, y0, y1, mod, l, g_final, n_rows, nbatch, lat_rows, final):
    d = x.shape[1]
    tm = _pick_tile(OUT_TM, lat_rows // nbatch, n_rows)
    kern = functools.partial(_combine_kernel, tiles_per_batch=lat_rows // nbatch // tm, nbatch=nbatch,
                             final=final)
    row = pl.BlockSpec((tm, d), lambda i: (i, 0))
    return pl.pallas_call(
        kern,
        grid=(n_rows // tm,),
        in_specs=[row, row, row, pl.BlockSpec((None, SUBLANES, d), lambda i: (l, 0, 5)),
                  pl.BlockSpec((1, d), lambda i: (0, 0))],
        out_specs=row,
        out_shape=jax.ShapeDtypeStruct((n_rows, d), F32),
        compiler_params=_cparams(("arbitrary",), 40),
    )(x, y0, y1, mod, g_final.reshape(1, d))


def kernel(x, c, ctx, c_ctx, w_ada, b_ada, g_mix, g_ffn, w_in, conv_w, conv_b, conv_ln_g, conv_ln_b,
           na_rpb, hgrn_lb, hgrn_norm_g, w_out, w_router_group, b_router_group, w_router_expert,
           b_router_expert, w_exp_gate, w_exp_up, w_exp_down, g_final):
    nb, seq, d = x.shape
    ctx_len = ctx.shape[1]
    depth = w_ada.shape[0]
    lat_rows = nb * seq
    n_all = lat_rows + nb * ctx_len
    conv_ch = conv_w.shape[2]
    off_na = 2 * conv_ch
    off_hg = off_na + 3 * NA_HEADS * NA_HD
    rows = seq // GRID_W
    assert nb < SUBLANES and rows % NA_ROWS == 0 and rows >= NA_KEY_ROWS

    lbs = jnp.cumsum(jax.nn.softmax(hgrn_lb.astype(F32), axis=0), axis=0)
    lbs = lbs - lbs[:1]

    cond = jnp.concatenate([c, c_ctx[None, :], jnp.zeros((SUBLANES - nb - 1, d), F32)], axis=0)
    mod = _ada_mod(cond, w_ada, b_ada)

    xs = jnp.concatenate([x.reshape(lat_rows, d), ctx.reshape(nb * ctx_len, d)], axis=0)
    w_in_b = w_in.astype(BF16)
    w_out_b = w_out.astype(BF16)
    for l in range(depth):
        with_ctx = l < depth - 1
        n_act = n_all if with_ctx else lat_rows
        u = _norm_in(xs, g_mix[l], mod, w_in_b, l, nb, lat_rows)

        conv = _conv_module(u, conv_w[l], conv_b[l], conv_ln_g[l], conv_ln_b[l], n_act, lat_rows, seq)
        na = _na_latent(u, _na_bias(na_rpb[l], rows), nb, seq, lat_rows, off_na)
        if with_ctx:
            na = jnp.concatenate([na, _ctx_attn(u, nb, ctx_len, lat_rows, off_na)], axis=0)
        hg = _hgrn(u, lbs[l], hgrn_norm_g[l], nb, seq, ctx_len, lat_rows, off_hg)

        w_router = jnp.concatenate(
            [w_router_group[l], w_router_expert[l],
             jnp.zeros((d, ROUTER_PAD - N_GROUPS - N_EXPERTS), F32)], axis=1)
        b_router = jnp.concatenate(
            [b_router_group[l], b_router_expert[l],
             jnp.zeros((ROUTER_PAD - N_GROUPS - N_EXPERTS,), F32)]).reshape(1, ROUTER_PAD)
        x_mid, h, route = _out_proj(xs, conv, na, hg, w_out_b, l, mod, g_ffn[l],
                                    w_router, b_router, n_act, nb, lat_rows)

        slot_tok, slot_w, pos, blk_e, nused = _route_meta(route, n_act)
        ys = _moe_experts(_rows(h, slot_tok), slot_w, blk_e, nused, w_exp_gate, w_exp_up, w_exp_down, l)
        y0 = _rows(ys, pos[:, 0])
        y1 = _rows(ys, pos[:, 1])
        xs = _combine(x_mid, y0, y1, mod, l, g_final, n_act, nb, lat_rows, final=not with_ctx)
    return xs.reshape(nb, seq, d)
```

```python
import functools

import numpy as np
import jax
import jax.numpy as jnp
from jax import lax
from jax.experimental import pallas as pl
from jax.experimental.pallas import tpu as pltpu

F32 = jnp.float32
BF16 = jnp.bfloat16

EPS = 1e-6
NEG_INF = -1e30
LOG2_E = 1.4426950408889634

GRID_W = 64
CONV_K = 31
NA_HEADS = 16
NA_HD = 64
NA_KH = 8
NA_KW = 16
HG_HEADS = 4
HG_DK = 128
N_GROUPS = 4
EXP_PER_GROUP = 8
N_EXPERTS = N_GROUPS * EXP_PER_GROUP
TOP_K = 2

LANES = 128
SUBLANES = 8
VMEM_BYTES = 64 * 1024 * 1024

ROW_TILE = 256
IN_TM = 1024
IN_TN = 512
OUT_TM = 512
NA_ROWS = 8
NA_KEY_ROWS = 16
HG_BLOCK = 256
MOE_BM = 512
ROUTER_PAD = LANES
HALO = 16


def _pick_tile(pref, *extents):
    t = pref
    while t > ROW_TILE and any(e % t for e in extents):
        t //= 2
    assert all(e % t == 0 for e in extents)
    return t


def _cparams(sem, vmem_mb):
    return pltpu.CompilerParams(dimension_semantics=sem, vmem_limit_bytes=vmem_mb * 1024 * 1024)


def _dot(a, b):
    return jnp.dot(a, b, preferred_element_type=F32)


def _dot_nt(a, b):
    return lax.dot_general(a, b, (((1,), (1,)), ((), ())), preferred_element_type=F32)


def _dot_tn(a, b):
    return lax.dot_general(a, b, (((0,), (0,)), ((), ())), preferred_element_type=F32)


def _sigmoid(x):
    return 1.0 / (1.0 + jnp.exp(-x))


def _silu(x):
    return x * _sigmoid(x)


def _ada_kernel(c_ref, w_ref, b_ref, o_ref):
    sc = _silu(c_ref[...])
    o_ref[0] = jnp.dot(sc, w_ref[0], precision=lax.Precision.HIGHEST,
                       preferred_element_type=F32) + b_ref[0]


def _ada_mod(cond, w_ada, b_ada):
    depth, d, n = w_ada.shape
    tn = 1024
    return pl.pallas_call(
        _ada_kernel,
        grid=(depth, n // tn),
        in_specs=[
            pl.BlockSpec((SUBLANES, d), lambda l, j: (0, 0)),
            pl.BlockSpec((1, d, tn), lambda l, j: (l, 0, j)),
            pl.BlockSpec((1, 1, tn), lambda l, j: (l, 0, j)),
        ],
        out_specs=pl.BlockSpec((1, SUBLANES, tn), lambda l, j: (l, 0, j)),
        out_shape=jax.ShapeDtypeStruct((depth, SUBLANES, n), F32),
        compiler_params=_cparams(("arbitrary", "arbitrary"), 40),
    )(cond, w_ada, b_ada.reshape(depth, 1, n))


def _rms_mod(x, g, scale, shift):
    y = x * lax.rsqrt(jnp.mean(x * x, axis=-1, keepdims=True) + EPS)
    return (y * g) * (1.0 + scale) + shift


def _norm_in_kernel(x_ref, g_ref, sh_ref, sc_ref, w_ref, o_ref, h_ref, *, tiles_per_batch, nbatch):
    i = pl.program_id(0)

    @pl.when(pl.program_id(1) == 0)
    def _():
        r = jnp.minimum(i // tiles_per_batch, nbatch)
        h = _rms_mod(x_ref[...], g_ref[...], sc_ref[pl.ds(r, 1), :], sh_ref[pl.ds(r, 1), :])
        h_ref[...] = h.astype(BF16)

    o_ref[...] = _dot(h_ref[...], w_ref[...])


def _norm_in(x, g, mod, w_bf16, l, nbatch, lat_rows):
    n, d = x.shape
    nout = w_bf16.shape[2]
    tm, tn = _pick_tile(IN_TM, lat_rows // nbatch, n), IN_TN
    kern = functools.partial(_norm_in_kernel, tiles_per_batch=lat_rows // nbatch // tm, nbatch=nbatch)
    return pl.pallas_call(
        kern,
        grid=(n // tm, nout // tn),
        in_specs=[
            pl.BlockSpec((tm, d), lambda i, j: (i, 0)),
            pl.BlockSpec((1, d), lambda i, j: (0, 0)),
            pl.BlockSpec((None, SUBLANES, d), lambda i, j: (l, 0, 0)),
            pl.BlockSpec((None, SUBLANES, d), lambda i, j: (l, 0, 1)),
            pl.BlockSpec((None, d, tn), lambda i, j: (l, 0, j)),
        ],
        out_specs=pl.BlockSpec((tm, tn), lambda i, j: (i, j)),
        out_shape=jax.ShapeDtypeStruct((n, nout), F32),
        scratch_shapes=[pltpu.VMEM((tm, d), BF16)],
        compiler_params=_cparams(("arbitrary", "arbitrary"), 48),
    )(x, g.reshape(1, d), mod, mod, w_bf16)


def _conv_kernel(ap_ref, gp_ref, a_ref, gt_ref, an_ref, gn_ref, w_ref, b_ref, lg_ref, lb_ref,
                 o_ref, buf_ref, acc_ref, *, lat_tiles, tiles_per_seq):
    i = pl.program_id(0)
    tc, ch = a_ref.shape
    is_lat = i < lat_tiles
    pos = i % tiles_per_seq
    first = jnp.logical_or(jnp.logical_not(is_lat), pos == 0)
    last = jnp.logical_or(jnp.logical_not(is_lat), pos == tiles_per_seq - 1)

    buf_ref[0:HALO] = jnp.where(first, 0.0, ap_ref[...] * _sigmoid(gp_ref[...]))
    buf_ref[HALO:HALO + tc] = a_ref[...] * _sigmoid(gt_ref[...])
    buf_ref[HALO + tc:2 * HALO + tc] = jnp.where(last, 0.0, an_ref[...] * _sigmoid(gn_ref[...]))

    rows = 64
    base = HALO - CONV_K // 2
    for c in range(ch // LANES):
        cs = slice(c * LANES, (c + 1) * LANES)
        for r in range(tc // rows):
            acc = None
            for res in range(SUBLANES):
                y = None
                for k in range(CONV_K):
                    if (base + k) % SUBLANES != res:
                        continue
                    lo = r * rows + (base + k) // SUBLANES * SUBLANES
                    term = w_ref[k:k + 1, cs] * buf_ref[lo:lo + rows + SUBLANES, cs]
                    y = term if y is None else y + term
                if y is not None:
                    y = y[res:res + rows]
                    acc = y if acc is None else acc + y
            acc_ref[r * rows:(r + 1) * rows, cs] = acc

    h = acc_ref[...] + b_ref[...]
    mu = jnp.mean(h, axis=-1, keepdims=True)
    var = jnp.mean(jnp.square(h - mu), axis=-1, keepdims=True)
    y = (h - mu) * lax.rsqrt(var + EPS) * lg_ref[...] + lb_ref[...]
    o_ref[...] = _silu(y).astype(o_ref.dtype)


def _conv_module(u, w_dw, b_dw, ln_g, ln_b, n_rows, lat_rows, seq):
    ch = w_dw.shape[1]
    tc = ROW_TILE
    per = tc // HALO
    nh = u.shape[0] // HALO
    kern = functools.partial(_conv_kernel, lat_tiles=lat_rows // tc, tiles_per_seq=seq // tc)
    prev_map = lambda c: (lambda i: (jnp.maximum(i * per - 1, 0), c))
    next_map = lambda c: (lambda i: (jnp.minimum((i + 1) * per, nh - 1), c))
    vec = lambda a: a.reshape(1, ch)
    return pl.pallas_call(
        kern,
        grid=(n_rows // tc,),
        in_specs=[
            pl.BlockSpec((HALO, ch), prev_map(0)),
            pl.BlockSpec((HALO, ch), prev_map(1)),
            pl.BlockSpec((tc, ch), lambda i: (i, 0)),
            pl.BlockSpec((tc, ch), lambda i: (i, 1)),
            pl.BlockSpec((HALO, ch), next_map(0)),
            pl.BlockSpec((HALO, ch), next_map(1)),
            pl.BlockSpec((CONV_K, ch), lambda i: (0, 0)),
            pl.BlockSpec((1, ch), lambda i: (0, 0)),
            pl.BlockSpec((1, ch), lambda i: (0, 0)),
            pl.BlockSpec((1, ch), lambda i: (0, 0)),
        ],
        out_specs=pl.BlockSpec((tc, ch), lambda i: (i, 0)),
        out_shape=jax.ShapeDtypeStruct((n_rows, ch), BF16),
        scratch_shapes=[pltpu.VMEM((tc + 2 * HALO, ch), F32), pltpu.VMEM((tc, ch), F32)],
        compiler_params=_cparams(("arbitrary",), 16),
    )(u, u, u, u, u, u, w_dw, vec(b_dw), vec(ln_g), vec(ln_b))


def _na_bias_tables(rows):
    groups = rows // NA_ROWS
    reps = [0, min(1, groups - 1), groups - 1]
    out = []
    for g in reps:
        start = int(np.clip(NA_ROWS * g - NA_KH // 2, 0, rows - NA_KEY_ROWS))
        per_row = []
        for i in range(NA_ROWS):
            r = NA_ROWS * g + i
            sr = int(np.clip(r - NA_KH // 2, 0, rows - NA_KH))
            per_row.append((sr - start, sr - r + NA_KH - 1))
        out.append(per_row)
    return out


def _na_bias(rpb, rows):
    nh = rpb.shape[0]
    ndr, ndc = 2 * NA_KH - 1, 2 * NA_KW - 1
    period = 2 * GRID_W - 1
    pad = GRID_W - NA_KW
    vp = jnp.pad(rpb.astype(F32), ((0, 0), (0, 0), (pad, period - ndc - pad)))
    hank = jnp.tile(vp, (1, 1, GRID_W + 1))[:, :, :GRID_W * (period + 1)]
    hank = hank.reshape(nh, ndr, GRID_W, period + 1)[..., :GRID_W]
    toe = hank[:, :, ::-1, :]
    c = np.arange(GRID_W)[:, None]
    j = np.arange(GRID_W)[None, :]
    ws = np.clip(c - NA_KW // 2, 0, GRID_W - NA_KW)
    col_ok = (j >= ws) & (j < ws + NA_KW)
    toe = jnp.where(col_ok[None, None], toe, NEG_INF)
    flat = jnp.transpose(toe, (0, 2, 1, 3)).reshape(nh, GRID_W, ndr * GRID_W)
    nk = NA_KEY_ROWS * GRID_W
    lpad = NA_ROWS * GRID_W
    total = 2 * nk

    def padded(shift):
        return jnp.pad(flat, ((0, 0), (0, 0), (lpad - shift, total - flat.shape[2] - lpad + shift)))

    tab = jnp.stack([padded(0), padded(GRID_W)])
    tab = tab.reshape(2, nh, GRID_W, total // LANES, LANES).transpose(0, 1, 3, 2, 4)
    tables = _na_bias_tables(rows)
    mask = np.full((len(tables), NA_ROWS, nk), NEG_INF, np.float32)
    dvals = []
    for cls, per_row in enumerate(tables):
        dvals.append(per_row[0][1] - per_row[0][0])
        for i, (off, lo) in enumerate(per_row):
            assert lo - off == dvals[-1] - i and -NA_ROWS <= lo - off < NA_ROWS
            mask[cls, i, off * GRID_W:(off + NA_KH) * GRID_W] = 0.0
    assert len({d % 2 for d in dvals}) == 1
    assert len(tables) < 3 or rows < 3 * NA_ROWS or all(off == i for i, (off, _) in enumerate(tables[1]))
    return tab * LOG2_E, jnp.asarray(mask * LOG2_E), tuple(dvals)


def _softmax_pv(s_parts, v_parts):
    m = functools.reduce(jnp.maximum, [jnp.max(s, axis=-1, keepdims=True) for s in s_parts])
    acc, l = None, None
    for s, v in zip(s_parts, v_parts):
        p = jnp.exp2(s - m)
        ps = jnp.sum(p, axis=-1, keepdims=True)
        pv = _dot(p.astype(BF16), v)
        l = ps if l is None else l + ps
        acc = pv if acc is None else acc + pv
    return acc / l


def _na_head_edge(s_raw, s_ctx, vs, vc, tab_ref, mask_ref, a, dcls, parity, pad_blocks):
    kw = s_raw[0].shape[1]
    bpk = kw // LANES
    s_parts = []
    for m, s in enumerate(s_raw):
        row_blocks = []
        for i in range(NA_ROWS):
            copy = (parity - i) % 2
            first = (dcls - i - copy + 2 * pad_blocks) // 2 + m * bpk
            bias = jnp.concatenate([tab_ref[copy, a, first + t] for t in range(bpk)], axis=1)
            bias = bias + mask_ref[i:i + 1, m * kw:(m + 1) * kw]
            row_blocks.append(s[i * GRID_W:(i + 1) * GRID_W, :] + bias)
        s_parts.append(jnp.concatenate(row_blocks, axis=0))
    return _softmax_pv(s_parts + [s_ctx], vs + [vc])


def _na_head_interior(s_raw, s_ctx, vs, vc, tab_ref, mask_ref, a, d0, pad_blocks):
    kw = s_raw[0].shape[1]
    bpk = kw // LANES
    nblk = len(s_raw) * bpk
    p_rows, pc_rows, l_rows = [], [], []
    for i in range(NA_ROWS):
        rs = slice(i * GRID_W, (i + 1) * GRID_W)
        lo_lane, hi_lane = i * GRID_W, (i + NA_KH) * GRID_W
        b_lo, b_hi = lo_lane // LANES, -(-hi_lane // LANES)
        copy = (d0 - i) % 2
        first = (d0 - i - copy + 2 * pad_blocks) // 2
        blocks = []
        for b in range(b_lo, b_hi):
            ls = slice((b % bpk) * LANES, (b % bpk + 1) * LANES)
            sb = s_raw[b // bpk][rs, ls] + tab_ref[copy, a, first + b]
            if b * LANES < lo_lane or (b + 1) * LANES > hi_lane:
                sb = sb + mask_ref[i:i + 1, b * LANES:(b + 1) * LANES]
            blocks.append(sb)
        sw = jnp.concatenate(blocks, axis=1)
        sc = s_ctx[rs]
        m = jnp.maximum(jnp.max(sw, axis=-1, keepdims=True), jnp.max(sc, axis=-1, keepdims=True))
        pw = jnp.exp2(sw - m)
        pc = jnp.exp2(sc - m)
        l_rows.append(jnp.sum(pw, axis=-1, keepdims=True) + jnp.sum(pc, axis=-1, keepdims=True))
        pieces = [jnp.zeros((GRID_W, b_lo * LANES), BF16), pw.astype(BF16),
                  jnp.zeros((GRID_W, (nblk - b_hi) * LANES), BF16)]
        p_rows.append(jnp.concatenate([x for x in pieces if x.shape[1]], axis=1))
        pc_rows.append(pc.astype(BF16))
    p = jnp.concatenate(p_rows, axis=0)
    acc = _dot(jnp.concatenate(pc_rows, axis=0), vc)
    for m, v in enumerate(vs):
        acc = acc + _dot(p[:, m * kw:(m + 1) * kw], v)
    return acc / jnp.concatenate(l_rows, axis=0)


def _na_kernel(q_ref, k0, k1, k2, k3, v0, v1, v2, v3, kc_ref, vc_ref, tab_ref, mask_ref, o_ref, *,
               groups, dvals):
    g = pl.program_id(1)
    is_edge = jnp.logical_or(g == 0, g == groups - 1)
    pad_blocks = NA_ROWS * GRID_W // LANES

    def step(interior):
        q2 = q_ref[...] * (NA_HD ** -0.5 * LOG2_E)
        lane = lax.broadcasted_iota(jnp.int32, q2.shape, 1)
        ks = [k[...].astype(BF16) for k in (k0, k1, k2, k3)]
        vs = [v[...].astype(BF16) for v in (v0, v1, v2, v3)]
        kc = kc_ref[...].astype(BF16)
        vc = vc_ref[...].astype(BF16)
        outs = []
        for a in range(2):
            sel = (lane < NA_HD) if a == 0 else (lane >= NA_HD)
            qa = jnp.where(sel, q2, 0.0).astype(BF16)
            s_raw = [_dot_nt(qa, k) for k in ks]
            s_ctx = _dot_nt(qa, kc)
            if interior:
                outs.append(_na_head_interior(s_raw, s_ctx, vs, vc, tab_ref, mask_ref, a, dvals[1], pad_blocks))
            else:
                dcls = jnp.where(g == 0, dvals[0], dvals[2])
                outs.append(_na_head_edge(s_raw, s_ctx, vs, vc, tab_ref, mask_ref, a, dcls, dvals[0], pad_blocks))
        o_ref[...] = jnp.where(lane < NA_HD, outs[0], outs[1]).astype(o_ref.dtype)

    @pl.when(is_edge)
    def _():
        step(False)

    @pl.when(jnp.logical_not(is_edge))
    def _():
        step(True)


def _na_latent(u, bias, nbatch, seq, lat_rows, off_na):
    tab, mask, dvals = bias
    rows = seq // GRID_W
    groups = rows // NA_ROWS
    nq = NA_ROWS * GRID_W
    kblk = ROW_TILE
    nkb = NA_KEY_ROWS * GRID_W // kblk
    assert nkb == 4
    qcol = off_na // LANES
    heads2 = NA_HEADS * NA_HD // LANES
    kcol, vcol = qcol + heads2, qcol + 2 * heads2
    kb_per_batch = seq // kblk
    kb_per_grow = GRID_W * NA_ROWS // kblk
    lat_kb = lat_rows // kblk

    def kmap(col, m):
        def f(h, g, b):
            st = jnp.clip(g * kb_per_grow - (NA_KH // 2) * GRID_W // kblk, 0, kb_per_batch - nkb)
            return (b * kb_per_batch + st + m, col + h)
        return f

    def mask_map(h, g, b):
        return (jnp.where(g == 0, 0, jnp.where(g == groups - 1, 2, 1)), 0, 0)

    in_specs = [pl.BlockSpec((nq, LANES), lambda h, g, b: (b * groups + g, qcol + h))]
    in_specs += [pl.BlockSpec((kblk, LANES), kmap(kcol, m)) for m in range(nkb)]
    in_specs += [pl.BlockSpec((kblk, LANES), kmap(vcol, m)) for m in range(nkb)]
    in_specs += [pl.BlockSpec((ROW_TILE, LANES), lambda h, g, b: (lat_kb + b, kcol + h)),
                 pl.BlockSpec((ROW_TILE, LANES), lambda h, g, b: (lat_kb + b, vcol + h)),
                 pl.BlockSpec((2, 2) + tab.shape[2:], lambda h, g, b: (0, h, 0, 0, 0)),
                 pl.BlockSpec((None,) + mask.shape[1:], mask_map)]
    return pl.pallas_call(
        functools.partial(_na_kernel, groups=groups, dvals=dvals),
        grid=(heads2, groups, nbatch),
        in_specs=in_specs,
        out_specs=pl.BlockSpec((nq, LANES), lambda h, g, b: (b * groups + g, h)),
        out_shape=jax.ShapeDtypeStruct((lat_rows, NA_HEADS * NA_HD), BF16),
        compiler_params=_cparams(("arbitrary", "arbitrary", "arbitrary"), 40),
    )(*([u] * 11), tab, mask)


def _ctx_attn_kernel(q_ref, k_ref, v_ref, o_ref):
    q2 = q_ref[...] * (NA_HD ** -0.5 * LOG2_E)
    lane = lax.broadcasted_iota(jnp.int32, q2.shape, 1)
    k = k_ref[...].astype(BF16)
    v = v_ref[...].astype(BF16)
    outs = []
    for a in range(2):
        sel = (lane < NA_HD) if a == 0 else (lane >= NA_HD)
        qa = jnp.where(sel, q2, 0.0).astype(BF16)
        outs.append(_softmax_pv([_dot_nt(qa, k)], [v]))
    o_ref[...] = jnp.where(lane < NA_HD, outs[0], outs[1]).astype(o_ref.dtype)


def _ctx_attn(u, nbatch, ctx_len, lat_rows, off_na):
    assert ctx_len == ROW_TILE
    qcol = off_na // LANES
    heads2 = NA_HEADS * NA_HD // LANES
    base = lat_rows // ROW_TILE
    spec = lambda col: pl.BlockSpec((ROW_TILE, LANES), lambda b, h: (base + b, col + h))
    return pl.pallas_call(
        _ctx_attn_kernel,
        grid=(nbatch, heads2),
        in_specs=[spec(qcol), spec(qcol + heads2), spec(qcol + 2 * heads2)],
        out_specs=pl.BlockSpec((ROW_TILE, LANES), lambda b, h: (b, h)),
        out_shape=jax.ShapeDtypeStruct((nbatch * ctx_len, NA_HEADS * NA_HD), BF16),
        compiler_params=_cparams(("arbitrary", "arbitrary"), 16),
    )(u, u, u)


def _hg_level_map(rev):
    size = HG_BLOCK // 2
    t = np.arange(size)[:, None]
    s = np.arange(size)[None, :]
    x = t ^ s
    lvl = np.where(x > 0, np.frexp(np.maximum(x, 1))[1] - 1, -1)
    causal = (s < t) if not rev else (s > t)
    out = np.where(causal, lvl, -1)
    out = np.where(t == s, int(np.log2(size)), out)
    return out.astype(np.int32)


def _hg_tri(rev):
    t = np.arange(HG_BLOCK)[:, None]
    s = np.arange(HG_BLOCK)[None, :]
    return ((s <= t) if not rev else (s >= t)).astype(np.float32)


def _hg_anchor(b3, m, rev):
    nv = b3.shape[0]
    if m >= SUBLANES:
        w = m // SUBLANES
        b4 = b3.reshape(nv // (2 * w), 2 * w, SUBLANES, LANES)
        a = b4[:, w:w + 1, 0:1, :] if rev else b4[:, w - 1:w, SUBLANES - 1:SUBLANES, :]
        return jnp.broadcast_to(a, b4.shape).reshape(b3.shape)
    sub = lax.broadcasted_iota(jnp.int32, b3.shape, 1)
    out = None
    for g in range(SUBLANES // (2 * m)):
        idx = g * 2 * m + (m if rev else m - 1)
        a = jnp.broadcast_to(b3[:, idx:idx + 1, :], b3.shape)
        out = a if out is None else jnp.where(sub >= g * 2 * m, a, out)
    return out


def _neg_abs(x):
    bits = lax.bitcast_convert_type(x, jnp.uint32) | jnp.uint32(0x80000000)
    return lax.bitcast_convert_type(bits, F32)


def _hg_pick(q3, k3, m, rev):
    nv = q3.shape[0]
    if m >= SUBLANES:
        w = m // SUBLANES
        shape4 = (nv // (2 * w), 2 * w, SUBLANES, LANES)
        q4, k4 = q3.reshape(shape4), k3.reshape(shape4)
        lower, upper = (q4, k4) if rev else (k4, q4)
        return jnp.concatenate([lower[:, :w], upper[:, w:]], axis=1).reshape(q3.shape)
    upper_rows = (lax.broadcasted_iota(jnp.int32, q3.shape, 1) & m) != 0
    return jnp.where(upper_rows, k3 if rev else q3, q3 if rev else k3)


def _hg_block(q, v, z, alog, clog, oml, tri, lv, st, rev):
    n = q.shape[0]
    half = n // 2
    nlev = int(np.log2(n))
    q = _silu(q)
    t = jnp.exp(-jnp.abs(z))
    lsig = jnp.minimum(z, 0.0) - jnp.log(1.0 + t)
    cc = clog + lsig
    logf = jnp.maximum(alog, cc) + jnp.log(1.0 + jnp.exp(-jnp.abs(alog - cc)))
    kk = oml * jnp.where(z >= 0.0, t, 1.0) / (1.0 + t)

    hi = logf.astype(BF16)
    r1 = logf - hi.astype(F32)
    mid = r1.astype(BF16)
    lo = (r1 - mid.astype(F32)).astype(BF16)
    b = (_dot(tri, hi) + _dot(tri, mid) + _dot(tri, lo)) * LOG2_E

    shape3 = (n // SUBLANES, SUBLANES, LANES)
    b3, q3, k3 = b.reshape(shape3), q.reshape(shape3), kk.reshape(shape3)
    halves = (slice(0, half), slice(half, n))
    qb, kb = q.astype(BF16), kk.astype(BF16)
    acc = [jnp.where(lv == nlev - 1, _dot_nt(qb[hs], kb[hs]), 0.0) for hs in halves]
    for lev in range(nlev - 1):
        m = 1 << lev
        e = jnp.exp2(_neg_abs(b3 - _hg_anchor(b3, m, rev)))
        w = (_hg_pick(q3, k3, m, rev) * e).reshape(n, LANES).astype(BF16)
        acc = [jnp.where(lv == lev, _dot_nt(w[hs], w[hs]), a) for hs, a in zip(halves, acc)]
    first, second = (halves[1], halves[0]) if rev else halves
    e = jnp.exp2(_neg_abs(b - (b[half:half + 1, :] if rev else b[half - 1:half, :])))
    top = _dot_nt((q[second] * e[second]).astype(BF16), (kk[first] * e[first]).astype(BF16))

    vb = v.astype(BF16)
    a0, a1 = acc[0].astype(BF16), acc[1].astype(BF16)
    tb = top.astype(BF16)
    if rev:
        o_lo = _dot(jnp.concatenate([a0, tb], axis=1), vb)
        o_hi = _dot(a1, vb[halves[1]])
    else:
        o_lo = _dot(a0, vb[halves[0]])
        o_hi = _dot(jnp.concatenate([tb, a1], axis=1), vb)
    b_last = b[0:1, :] if rev else b[n - 1:n, :]
    qh = (q * jnp.exp2(b)).astype(BF16)
    o = jnp.concatenate([o_lo, o_hi], axis=0) + _dot_nt(qh, st.astype(BF16))
    kh = (kk * jnp.exp2(b_last - b)).astype(BF16)
    st_new = st * jnp.exp2(b_last) + _dot_tn(vb, kh)
    return o, st_new


def _hg_fwd_kernel(q_ref, v_ref, z_ref, al_ref, cl_ref, om_ref, tri_ref, lv_ref, o_ref, st_ref):
    @pl.when(pl.program_id(1) == 0)
    def _():
        st_ref[...] = jnp.zeros_like(st_ref)

    tri = tri_ref[...]
    lv = lv_ref[...]
    for h in range(HG_HEADS):
        hs = slice(h * HG_DK, (h + 1) * HG_DK)
        o, st = _hg_block(q_ref[:, hs], v_ref[:, hs], z_ref[:, hs], al_ref[:, hs], cl_ref[:, hs],
                          om_ref[:, hs], tri, lv, st_ref[h], False)
        o_ref[:, hs] = o
        st_ref[h] = st


def _hg_bwd_kernel(q_ref, v_ref, z_ref, g_ref, of_ref, al_ref, cl_ref, om_ref, ng_ref, tri_ref, lv_ref,
                   o_ref, st_ref):
    @pl.when(pl.program_id(1) == 0)
    def _():
        st_ref[...] = jnp.zeros_like(st_ref)

    tri = tri_ref[...]
    lv = lv_ref[...]
    for h in range(HG_HEADS):
        hs = slice(h * HG_DK, (h + 1) * HG_DK)
        o, st = _hg_block(q_ref[:, hs], v_ref[:, hs], z_ref[:, hs], al_ref[:, hs], cl_ref[:, hs],
                          om_ref[:, hs], tri, lv, st_ref[h], True)
        st_ref[h] = st
        t = of_ref[:, hs] + o
        y = t * lax.rsqrt(jnp.mean(t * t, axis=-1, keepdims=True) + EPS)
        o_ref[:, hs] = (y * ng_ref[:, hs] * _silu(g_ref[:, hs])).astype(o_ref.dtype)


def _hgrn(u, lb, norm_g, nbatch, seq, ctx_len, lat_rows, off_hg):
    assert ctx_len == HG_BLOCK
    n = u.shape[0]
    hd = HG_HEADS * HG_DK
    col = off_hg // hd
    per = seq // HG_BLOCK
    lat_blocks = lat_rows // HG_BLOCK
    lbf = lb.astype(F32)
    alog, clog, oml = jnp.log(lbf), jnp.log1p(-lbf), 1.0 - lbf

    def fmap(c):
        return lambda b, j: (jnp.where(j == 0, lat_blocks + b, b * per + j - 1), c)

    def bmap(c):
        return lambda b, j: (jnp.where(j == 0, lat_blocks + b, b * per + per - j), c)

    const = lambda shape: pl.BlockSpec(shape, lambda b, j: (0, 0))
    grid = (nbatch, per + 1)
    vec = lambda a: a.reshape(1, hd)
    o_f = pl.pallas_call(
        _hg_fwd_kernel,
        grid=grid,
        in_specs=[pl.BlockSpec((HG_BLOCK, hd), fmap(col)), pl.BlockSpec((HG_BLOCK, hd), fmap(col + 1)),
                  pl.BlockSpec((HG_BLOCK, hd), fmap(col + 2)),
                  const((1, hd)), const((1, hd)), const((1, hd)),
                  const((HG_BLOCK, HG_BLOCK)), const((HG_BLOCK // 2, HG_BLOCK // 2))],
        out_specs=pl.BlockSpec((HG_BLOCK, hd), fmap(0)),
        out_shape=jax.ShapeDtypeStruct((n, hd), F32),
        scratch_shapes=[pltpu.VMEM((HG_HEADS, HG_DK, HG_DK), F32)],
        compiler_params=_cparams(("arbitrary", "arbitrary"), 32),
    )(u, u, u, vec(alog[0]), vec(clog[0]), vec(oml[0]),
      jnp.asarray(_hg_tri(False), BF16), jnp.asarray(_hg_level_map(False)))
    return pl.pallas_call(
        _hg_bwd_kernel,
        grid=grid,
        in_specs=[pl.BlockSpec((HG_BLOCK, hd), bmap(col)), pl.BlockSpec((HG_BLOCK, hd), bmap(col + 1)),
                  pl.BlockSpec((HG_BLOCK, hd), bmap(col + 3)), pl.BlockSpec((HG_BLOCK, hd), bmap(col + 4)),
                  pl.BlockSpec((HG_BLOCK, hd), bmap(0)),
                  const((1, hd)), const((1, hd)), const((1, hd)), const((1, hd)),
                  const((HG_BLOCK, HG_BLOCK)), const((HG_BLOCK // 2, HG_BLOCK // 2))],
        out_specs=pl.BlockSpec((HG_BLOCK, hd), bmap(0)),
        out_shape=jax.ShapeDtypeStruct((n, hd), BF16),
        scratch_shapes=[pltpu.VMEM((HG_HEADS, HG_DK, HG_DK), F32)],
        compiler_params=_cparams(("arbitrary", "arbitrary"), 32),
    )(u, u, u, u, o_f, vec(alog[1]), vec(clog[1]), vec(oml[1]), vec(norm_g.astype(F32)),
      jnp.asarray(_hg_tri(True), BF16), jnp.asarray(_hg_level_map(True)))


def _out_kernel(x_ref, cv_ref, na_ref, hg_ref, w_ref, ga_ref, g2_ref, sh2_ref, s2_ref, wrh_ref, wrl_ref, br_ref,
                xo_ref, h_ref, rt_ref, *, tiles_per_batch, nbatch):
    r = jnp.minimum(pl.program_id(0) // tiles_per_batch, nbatch)
    c0 = cv_ref.shape[1]
    c1 = c0 + na_ref.shape[1]
    mix = (_dot(cv_ref[...], w_ref[0:c0, :]) + _dot(na_ref[...], w_ref[c0:c1, :])
           + _dot(hg_ref[...], w_ref[c1:, :]))
    xn = x_ref[...] + ga_ref[pl.ds(r, 1), :] * mix
    xo_ref[...] = xn
    h = _rms_mod(xn, g2_ref[...], s2_ref[pl.ds(r, 1), :], sh2_ref[pl.ds(r, 1), :])
    h_ref[...] = h.astype(h_ref.dtype)
    h_hi = h.astype(BF16)
    h_lo = (h - h_hi.astype(F32)).astype(BF16)
    logits = (_dot(h_hi, wrh_ref[...]) + (_dot(h_lo, wrh_ref[...]) + _dot(h_hi, wrl_ref[...]))
              + br_ref[...])
    rt_ref[...] = _route_rows(logits)


def _route_rows(lg):
    lane = lax.broadcasted_iota(jnp.int32, lg.shape, 1)
    big = jnp.int32(2 ** 30)
    low = jnp.float32(-3e38)

    def first_max(vals, mask):
        m = jnp.max(vals, axis=-1, keepdims=True)
        idx = jnp.min(jnp.where(jnp.logical_and(vals == m, mask), lane, big), axis=-1, keepdims=True)
        return m, idx

    gmask = lane < N_GROUPS
    gl = jnp.where(gmask, lg, low)
    gm, grp = first_max(gl, gmask)
    p_grp = 1.0 / jnp.sum(jnp.where(gmask, jnp.exp(gl - gm), 0.0), axis=-1, keepdims=True)
    lo = N_GROUPS + grp * EXP_PER_GROUP
    emask = jnp.logical_and(lane >= lo, lane < lo + EXP_PER_GROUP)
    el = jnp.where(emask, lg, low)
    m1, i1 = first_max(el, emask)
    emask2 = jnp.logical_and(emask, lane != i1)
    el2 = jnp.where(emask2, lg, low)
    m2, i2 = first_max(el2, emask2)
    t = jnp.exp(m2 - m1)
    w1 = p_grp / (1.0 + t)
    w2 = p_grp * t / (1.0 + t)
    e1 = (i1 - N_GROUPS).astype(F32)
    e2 = (i2 - N_GROUPS).astype(F32)
    return jnp.where(lane == 0, e1, jnp.where(lane == 1, e2, jnp.where(lane == 2, w1,
                     jnp.where(lane == 3, w2, 0.0))))


def _out_proj(x, conv, na, hg, w_bf16, l, mod, g_ffn, w_router, b_router, n_rows, nbatch, lat_rows):
    d = x.shape[1]
    w_router_hi = w_router.astype(BF16)
    w_router_lo = (w_router - w_router_hi.astype(F32)).astype(BF16)
    tm = _pick_tile(OUT_TM, lat_rows // nbatch, n_rows)
    kern = functools.partial(_out_kernel, tiles_per_batch=lat_rows // nbatch // tm, nbatch=nbatch)
    row = lambda w: pl.BlockSpec((tm, w), lambda i: (i, 0))
    const = lambda shape: pl.BlockSpec(shape, lambda i: (0, 0))
    modc = lambda c: pl.BlockSpec((None, SUBLANES, d), lambda i: (l, 0, c))
    return pl.pallas_call(
        kern,
        grid=(n_rows // tm,),
        in_specs=[row(d), row(conv.shape[1]), row(na.shape[1]), row(hg.shape[1]),
                  pl.BlockSpec((None, d, d), lambda i: (l, 0, 0)),
                  modc(2),
                  const((1, d)),
                  modc(3),
                  modc(4),
                  const((d, ROUTER_PAD)), const((d, ROUTER_PAD)), const((1, ROUTER_PAD))],
        out_specs=[row(d), row(d), row(ROUTER_PAD)],
        out_shape=[jax.ShapeDtypeStruct((n_rows, d), F32), jax.ShapeDtypeStruct((n_rows, d), BF16),
                   jax.ShapeDtypeStruct((n_rows, ROUTER_PAD), F32)],
        compiler_params=_cparams(("arbitrary",), 56),
    )(x, conv, na, hg, w_bf16, mod, g_ffn.reshape(1, d), mod, mod, w_router_hi, w_router_lo, b_router)


def _moe_kernel(be_ref, nu_ref, xs_ref, w1_ref, w3_ref, w2_ref, sw_ref, o_ref, w1b, w3b, w2b):
    i = pl.program_id(0)
    e = be_ref[i]
    prev = be_ref[jnp.maximum(i - 1, 0)]

    @pl.when(jnp.logical_or(i == 0, e != prev))
    def _():
        w1b[...] = w1_ref[...].astype(BF16)
        w3b[...] = w3_ref[...].astype(BF16)
        w2b[...] = w2_ref[...].astype(BF16)

    @pl.when(i < nu_ref[0])
    def _():
        x = xs_ref[...]
        a = (_silu(_dot(x, w1b[...])) * _dot(x, w3b[...])).astype(BF16)
        o_ref[...] = (_dot(a, w2b[...]) * sw_ref[...]).astype(o_ref.dtype)

    @pl.when(i >= nu_ref[0])
    def _():
        o_ref[...] = jnp.zeros_like(o_ref)


def _moe_experts(xs, slot_w, blk_e, nused, w1, w3, w2, l):
    p, d = xs.shape
    de = w1.shape[3]
    bm = MOE_BM
    grid_spec = pltpu.PrefetchScalarGridSpec(
        num_scalar_prefetch=2,
        grid=(p // bm,),
        in_specs=[pl.BlockSpec((bm, d), lambda i, be, nu: (i, 0)),
                  pl.BlockSpec((None, None, d, de), lambda i, be, nu: (l, be[i], 0, 0)),
                  pl.BlockSpec((None, None, d, de), lambda i, be, nu: (l, be[i], 0, 0)),
                  pl.BlockSpec((None, None, de, d), lambda i, be, nu: (l, be[i], 0, 0)),
                  pl.BlockSpec((bm, 1), lambda i, be, nu: (i, 0))],
        out_specs=pl.BlockSpec((bm, d), lambda i, be, nu: (i, 0)),
        scratch_shapes=[pltpu.VMEM((d, de), BF16), pltpu.VMEM((d, de), BF16), pltpu.VMEM((de, d), BF16)],
    )
    return pl.pallas_call(
        _moe_kernel,
        grid_spec=grid_spec,
        out_shape=jax.ShapeDtypeStruct((p, d), BF16),
        compiler_params=_cparams(("arbitrary",), 48),
    )(blk_e, nused, xs, w1, w3, w2, slot_w.reshape(p, 1))


def _rows(a, idx):
    return a.at[idx].get(mode="promise_in_bounds")


def _route_meta(route, n):
    i32 = jnp.int32
    eid = route[:, 0:TOP_K].astype(i32).reshape(-1)
    wt = route[:, TOP_K:2 * TOP_K].reshape(-1)
    a = n * TOP_K
    bm = MOE_BM
    nblk = -(-a // bm) + N_EXPERTS
    p = nblk * bm
    experts = jnp.arange(N_EXPERTS, dtype=i32)[None, :]
    ja = jnp.arange(a, dtype=i32)
    se, order, wsort = lax.sort((eid, ja, wt), num_keys=1, is_stable=True)
    cnt = jnp.sum((eid[:, None] == experts).astype(i32), axis=0)
    pcnt = (cnt + bm - 1) // bm * bm
    pend = jnp.cumsum(pcnt)
    pstart = pend - pcnt
    end = jnp.cumsum(cnt)
    start = end - cnt
    off = pstart - start
    d_off = off - jnp.concatenate([jnp.zeros((1,), i32), off[:-1]])
    dst_sorted = ja + jnp.sum(jnp.where(ja[:, None] >= start[None, :], d_off[None, :], 0), axis=1)
    _, pos = lax.sort((order, dst_sorted), num_keys=1)
    jp = jnp.arange(p, dtype=i32)
    in_or_after = jp[:, None] >= pstart[None, :]
    src = jp - jnp.sum(jnp.where(in_or_after, d_off[None, :], 0), axis=1)
    valid = src < jnp.sum(jnp.where(in_or_after, cnt[None, :], 0), axis=1)
    src = jnp.where(valid, src, jp % a)
    slot_tok = _rows(order, src) // TOP_K
    slot_w = jnp.where(valid, _rows(wsort, src), 0.0)
    jb = jnp.arange(nblk, dtype=i32) * bm
    blk_e = jnp.minimum(jnp.sum((jb[:, None] >= pend[None, :]).astype(i32), axis=1), N_EXPERTS - 1)
    nused = (pend[-1:] // bm).astype(i32)
    return slot_tok, slot_w, pos.reshape(n, TOP_K), blk_e, nused


def _combine_kernel(x_ref, y0_ref, y1_ref, ga_ref, gf_ref, o_ref, *, tiles_per_batch, nbatch, final):
    r = jnp.minimum(pl.program_id(0) // tiles_per_batch, nbatch)
    xn = x_ref[...] + ga_ref[pl.ds(r, 1), :] * (y0_ref[...].astype(F32) + y1_ref[...].astype(F32))
    if final:
        xn = xn * lax.rsqrt(jnp.mean(xn * xn, axis=-1, keepdims=True) + EPS) * gf_ref[...]
    o_ref[...] = xn


def _combine(x, y0, y1, mod, l, g_final, n_rows, nbatch, lat_rows, final):
    d = x.shape[1]
    tm = _pick_tile(OUT_TM, lat_rows // nbatch, n_rows)
    kern = functools.partial(_combine_kernel, tiles_per_batch=lat_rows // nbatch // tm, nbatch=nbatch,
                             final=final)
    row = pl.BlockSpec((tm, d), lambda i: (i, 0))
    return pl.pallas_call(
        kern,
        grid=(n_rows // tm,),
        in_specs=[row, row, row, pl.BlockSpec((None, SUBLANES, d), lambda i: (l, 0, 5)),
                  pl.BlockSpec((1, d), lambda i: (0, 0))],
        out_specs=row,
        out_shape=jax.ShapeDtypeStruct((n_rows, d), F32),
        compiler_params=_cparams(("arbitrary",), 40),
    )(x, y0, y1, mod, g_final.reshape(1, d))


def kernel(x, c, ctx, c_ctx, w_ada, b_ada, g_mix, g_ffn, w_in, conv_w, conv_b, conv_ln_g, conv_ln_b,
           na_rpb, hgrn_lb, hgrn_norm_g, w_out, w_router_group, b_router_group, w_router_expert,
           b_router_expert, w_exp_gate, w_exp_up, w_exp_down, g_final):
    nb, seq, d = x.shape
    ctx_len = ctx.shape[1]
    depth = w_ada.shape[0]
    lat_rows = nb * seq
    n_all = lat_rows + nb * ctx_len
    conv_ch = conv_w.shape[2]
    off_na = 2 * conv_ch
    off_hg = off_na + 3 * NA_HEADS * NA_HD
    rows = seq // GRID_W
    assert nb < SUBLANES and rows % NA_ROWS == 0 and rows >= NA_KEY_ROWS

    lbs = jnp.cumsum(jax.nn.softmax(hgrn_lb.astype(F32), axis=0), axis=0)
    lbs = lbs - lbs[:1]

    cond = jnp.concatenate([c, c_ctx[None, :], jnp.zeros((SUBLANES - nb - 1, d), F32)], axis=0)
    mod = _ada_mod(cond, w_ada, b_ada)

    xs = jnp.concatenate([x.reshape(lat_rows, d), ctx.reshape(nb * ctx_len, d)], axis=0)
    w_in_b = w_in.astype(BF16)
    w_out_b = w_out.astype(BF16)
    for l in range(depth):
        with_ctx = l < depth - 1
        n_act = n_all if with_ctx else lat_rows
        u = _norm_in(xs, g_mix[l], mod, w_in_b, l, nb, lat_rows)

        conv = _conv_module(u, conv_w[l], conv_b[l], conv_ln_g[l], conv_ln_b[l], n_act, lat_rows, seq)
        na = _na_latent(u, _na_bias(na_rpb[l], rows), nb, seq, lat_rows, off_na)
        if with_ctx:
            na = jnp.concatenate([na, _ctx_attn(u, nb, ctx_len, lat_rows, off_na)], axis=0)
        hg = _hgrn(u, lbs[l], hgrn_norm_g[l], nb, seq, ctx_len, lat_rows, off_hg)

        w_router = jnp.concatenate(
            [w_router_group[l], w_router_expert[l],
             jnp.zeros((d, ROUTER_PAD - N_GROUPS - N_EXPERTS), F32)], axis=1)
        b_router = jnp.concatenate(
            [b_router_group[l], b_router_expert[l],
             jnp.zeros((ROUTER_PAD - N_GROUPS - N_EXPERTS,), F32)]).reshape(1, ROUTER_PAD)
        x_mid, h, route = _out_proj(xs, conv, na, hg, w_out_b, l, mod, g_ffn[l],
                                    w_router, b_router, n_act, nb, lat_rows)

        slot_tok, slot_w, pos, blk_e, nused = _route_meta(route, n_act)
        ys = _moe_experts(_rows(h, slot_tok), slot_w, blk_e, nused, w_exp_gate, w_exp_up, w_exp_down, l)
        y0 = _rows(ys, pos[:, 0])
        y1 = _rows(ys, pos[:, 1])
        xs = _combine(x_mid, y0, y1, mod, l, g_final, n_act, nb, lat_rows, final=not with_ctx)
    return xs.reshape(nb, seq, d)
```

```python
import functools

import numpy as np
import jax
import jax.numpy as jnp
from jax import lax
from jax.experimental import pallas as pl
from jax.experimental.pallas import tpu as pltpu

F32 = jnp.float32
BF16 = jnp.bfloat16

EPS = 1e-6
NEG_INF = -1e30
LOG2_E = 1.4426950408889634

GRID_W = 64
CONV_K = 31
NA_HEADS = 16
NA_HD = 64
NA_KH = 8
NA_KW = 16
HG_HEADS = 4
HG_DK = 128
N_GROUPS = 4
EXP_PER_GROUP = 8
N_EXPERTS = N_GROUPS * EXP_PER_GROUP
TOP_K = 2

LANES = 128
SUBLANES = 8
VMEM_BYTES = 64 * 1024 * 1024

ROW_TILE = 256
IN_TM = 1024
IN_TN = 512
OUT_TM = 512
NA_ROWS = 8
NA_KEY_ROWS = 16
HG_BLOCK = 256
MOE_BM = 512
ROUTER_PAD = LANES
HALO = 16


def _pick_tile(pref, *extents):
    t = pref
    while t > ROW_TILE and any(e % t for e in extents):
        t //= 2
    assert all(e % t == 0 for e in extents)
    return t


def _cparams(sem, vmem_mb):
    return pltpu.CompilerParams(dimension_semantics=sem, vmem_limit_bytes=vmem_mb * 1024 * 1024)


def _dot(a, b):
    return jnp.dot(a, b, preferred_element_type=F32)


def _dot_nt(a, b):
    return lax.dot_general(a, b, (((1,), (1,)), ((), ())), preferred_element_type=F32)


def _dot_tn(a, b):
    return lax.dot_general(a, b, (((0,), (0,)), ((), ())), preferred_element_type=F32)


def _sigmoid(x):
    return 1.0 / (1.0 + jnp.exp(-x))


def _silu(x):
    return x * _sigmoid(x)


def _split_bf16(x):
    hi = x.astype(BF16)
    return hi, (x - hi.astype(F32)).astype(BF16)


def _ada_kernel(c_ref, w_ref, b_ref, o_ref):
    s_hi, s_lo = _split_bf16(_silu(c_ref[...]))
    w_hi, w_lo = _split_bf16(w_ref[0])
    o_ref[0] = _dot(s_hi, w_hi) + (_dot(s_lo, w_hi) + _dot(s_hi, w_lo)) + b_ref[0]


def _ada_mod(cond, w_ada, b_ada):
    depth, d, n = w_ada.shape
    tn = 1024
    return pl.pallas_call(
        _ada_kernel,
        grid=(depth, n // tn),
        in_specs=[
            pl.BlockSpec((SUBLANES, d), lambda l, j: (0, 0)),
            pl.BlockSpec((1, d, tn), lambda l, j: (l, 0, j)),
            pl.BlockSpec((1, 1, tn), lambda l, j: (l, 0, j)),
        ],
        out_specs=pl.BlockSpec((1, SUBLANES, tn), lambda l, j: (l, 0, j)),
        out_shape=jax.ShapeDtypeStruct((depth, SUBLANES, n), F32),
        compiler_params=_cparams(("arbitrary", "arbitrary"), 40),
    )(cond, w_ada, b_ada.reshape(depth, 1, n))


def _rms_mod(x, g, scale, shift):
    y = x * lax.rsqrt(jnp.mean(x * x, axis=-1, keepdims=True) + EPS)
    return (y * g) * (1.0 + scale) + shift


def _norm_in_kernel(x_ref, g_ref, sh_ref, sc_ref, w_ref, o_ref, h_ref, *, tiles_per_batch, nbatch):
    i = pl.program_id(0)

    @pl.when(pl.program_id(1) == 0)
    def _():
        r = jnp.minimum(i // tiles_per_batch, nbatch)
        h = _rms_mod(x_ref[...], g_ref[...], sc_ref[pl.ds(r, 1), :], sh_ref[pl.ds(r, 1), :])
        h_ref[...] = h.astype(BF16)

    o_ref[...] = _dot(h_ref[...], w_ref[...])


def _norm_in(x, g, mod, w_bf16, l, nbatch, lat_rows):
    n, d = x.shape
    nout = w_bf16.shape[2]
    tm, tn = _pick_tile(IN_TM, lat_rows // nbatch, n), IN_TN
    kern = functools.partial(_norm_in_kernel, tiles_per_batch=lat_rows // nbatch // tm, nbatch=nbatch)
    return pl.pallas_call(
        kern,
        grid=(n // tm, nout // tn),
        in_specs=[
            pl.BlockSpec((tm, d), lambda i, j: (i, 0)),
            pl.BlockSpec((1, d), lambda i, j: (0, 0)),
            pl.BlockSpec((None, SUBLANES, d), lambda i, j: (l, 0, 0)),
            pl.BlockSpec((None, SUBLANES, d), lambda i, j: (l, 0, 1)),
            pl.BlockSpec((None, d, tn), lambda i, j: (l, 0, j)),
        ],
        out_specs=pl.BlockSpec((tm, tn), lambda i, j: (i, j)),
        out_shape=jax.ShapeDtypeStruct((n, nout), F32),
        scratch_shapes=[pltpu.VMEM((tm, d), BF16)],
        compiler_params=_cparams(("arbitrary", "arbitrary"), 48),
    )(x, g.reshape(1, d), mod, mod, w_bf16)


def _conv_kernel(ap_ref, gp_ref, a_ref, gt_ref, an_ref, gn_ref, w_ref, b_ref, lg_ref, lb_ref,
                 o_ref, buf_ref, acc_ref, *, lat_tiles, tiles_per_seq):
    i = pl.program_id(0)
    tc, ch = a_ref.shape
    is_lat = i < lat_tiles
    pos = i % tiles_per_seq
    first = jnp.logical_or(jnp.logical_not(is_lat), pos == 0)
    last = jnp.logical_or(jnp.logical_not(is_lat), pos == tiles_per_seq - 1)

    buf_ref[0:HALO] = jnp.where(first, 0.0, ap_ref[...] * _sigmoid(gp_ref[...]))
    buf_ref[HALO:HALO + tc] = a_ref[...] * _sigmoid(gt_ref[...])
    buf_ref[HALO + tc:2 * HALO + tc] = jnp.where(last, 0.0, an_ref[...] * _sigmoid(gn_ref[...]))

    rows = 64
    base = HALO - CONV_K // 2
    for c in range(ch // LANES):
        cs = slice(c * LANES, (c + 1) * LANES)
        for r in range(tc // rows):
            acc = None
            for res in range(SUBLANES):
                y = None
                for k in range(CONV_K):
                    if (base + k) % SUBLANES != res:
                        continue
                    lo = r * rows + (base + k) // SUBLANES * SUBLANES
                    term = w_ref[k:k + 1, cs] * buf_ref[lo:lo + rows + SUBLANES, cs]
                    y = term if y is None else y + term
                if y is not None:
                    y = y[res:res + rows]
                    acc = y if acc is None else acc + y
            acc_ref[r * rows:(r + 1) * rows, cs] = acc

    h = acc_ref[...] + b_ref[...]
    mu = jnp.mean(h, axis=-1, keepdims=True)
    var = jnp.mean(jnp.square(h - mu), axis=-1, keepdims=True)
    y = (h - mu) * lax.rsqrt(var + EPS) * lg_ref[...] + lb_ref[...]
    o_ref[...] = _silu(y).astype(o_ref.dtype)


def _conv_module(u, w_dw, b_dw, ln_g, ln_b, n_rows, lat_rows, seq):
    ch = w_dw.shape[1]
    tc = ROW_TILE
    per = tc // HALO
    nh = u.shape[0] // HALO
    kern = functools.partial(_conv_kernel, lat_tiles=lat_rows // tc, tiles_per_seq=seq // tc)
    prev_map = lambda c: (lambda i: (jnp.maximum(i * per - 1, 0), c))
    next_map = lambda c: (lambda i: (jnp.minimum((i + 1) * per, nh - 1), c))
    vec = lambda a: a.reshape(1, ch)
    return pl.pallas_call(
        kern,
        grid=(n_rows // tc,),
        in_specs=[
            pl.BlockSpec((HALO, ch), prev_map(0)),
            pl.BlockSpec((HALO, ch), prev_map(1)),
            pl.BlockSpec((tc, ch), lambda i: (i, 0)),
            pl.BlockSpec((tc, ch), lambda i: (i, 1)),
            pl.BlockSpec((HALO, ch), next_map(0)),
            pl.BlockSpec((HALO, ch), next_map(1)),
            pl.BlockSpec((CONV_K, ch), lambda i: (0, 0)),
            pl.BlockSpec((1, ch), lambda i: (0, 0)),
            pl.BlockSpec((1, ch), lambda i: (0, 0)),
            pl.BlockSpec((1, ch), lambda i: (0, 0)),
        ],
        out_specs=pl.BlockSpec((tc, ch), lambda i: (i, 0)),
        out_shape=jax.ShapeDtypeStruct((n_rows, ch), BF16),
        scratch_shapes=[pltpu.VMEM((tc + 2 * HALO, ch), F32), pltpu.VMEM((tc, ch), F32)],
        compiler_params=_cparams(("arbitrary",), 16),
    )(u, u, u, u, u, u, w_dw, vec(b_dw), vec(ln_g), vec(ln_b))


def _na_bias_tables(rows):
    groups = rows // NA_ROWS
    reps = [0, min(1, groups - 1), groups - 1]
    out = []
    for g in reps:
        start = int(np.clip(NA_ROWS * g - NA_KH // 2, 0, rows - NA_KEY_ROWS))
        per_row = []
        for i in range(NA_ROWS):
            r = NA_ROWS * g + i
            sr = int(np.clip(r - NA_KH // 2, 0, rows - NA_KH))
            per_row.append((sr - start, sr - r + NA_KH - 1))
        out.append(per_row)
    return out


def _na_bias(rpb, rows):
    nh = rpb.shape[0]
    ndr, ndc = 2 * NA_KH - 1, 2 * NA_KW - 1
    period = 2 * GRID_W - 1
    pad = GRID_W - NA_KW
    vp = jnp.pad(rpb.astype(F32), ((0, 0), (0, 0), (pad, period - ndc - pad)))
    hank = jnp.tile(vp, (1, 1, GRID_W + 1))[:, :, :GRID_W * (period + 1)]
    hank = hank.reshape(nh, ndr, GRID_W, period + 1)[..., :GRID_W]
    toe = hank[:, :, ::-1, :]
    c = np.arange(GRID_W)[:, None]
    j = np.arange(GRID_W)[None, :]
    ws = np.clip(c - NA_KW // 2, 0, GRID_W - NA_KW)
    col_ok = (j >= ws) & (j < ws + NA_KW)
    toe = jnp.where(col_ok[None, None], toe, NEG_INF)
    flat = jnp.transpose(toe, (0, 2, 1, 3)).reshape(nh, GRID_W, ndr * GRID_W)
    nk = NA_KEY_ROWS * GRID_W
    lpad = NA_ROWS * GRID_W
    total = 2 * nk

    def padded(shift):
        return jnp.pad(flat, ((0, 0), (0, 0), (lpad - shift, total - flat.shape[2] - lpad + shift)))

    tab = jnp.stack([padded(0), padded(GRID_W)])
    tab = tab.reshape(2, nh, GRID_W, total // LANES, LANES).transpose(0, 1, 3, 2, 4)
    tables = _na_bias_tables(rows)
    mask = np.full((len(tables), NA_ROWS, nk), NEG_INF, np.float32)
    dvals = []
    for cls, per_row in enumerate(tables):
        dvals.append(per_row[0][1] - per_row[0][0])
        for i, (off, lo) in enumerate(per_row):
            assert lo - off == dvals[-1] - i and -NA_ROWS <= lo - off < NA_ROWS
            mask[cls, i, off * GRID_W:(off + NA_KH) * GRID_W] = 0.0
    assert len({d % 2 for d in dvals}) == 1
    assert len(tables) < 3 or rows < 3 * NA_ROWS or all(off == i for i, (off, _) in enumerate(tables[1]))
    return tab * LOG2_E, jnp.asarray(mask * LOG2_E), tuple(dvals)


def _softmax_pv(s_parts, v_parts):
    m = functools.reduce(jnp.maximum, [jnp.max(s, axis=-1, keepdims=True) for s in s_parts])
    acc, l = None, None
    for s, v in zip(s_parts, v_parts):
        p = jnp.exp2(s - m)
        ps = jnp.sum(p, axis=-1, keepdims=True)
        pv = _dot(p.astype(BF16), v)
        l = ps if l is None else l + ps
        acc = pv if acc is None else acc + pv
    return acc / l


def _na_head_edge(s_raw, s_ctx, vs, vc, tab_ref, mask_ref, a, dcls, parity, pad_blocks):
    kw = s_raw[0].shape[1]
    bpk = kw // LANES
    s_parts = []
    for m, s in enumerate(s_raw):
        row_blocks = []
        for i in range(NA_ROWS):
            copy = (parity - i) % 2
            first = (dcls - i - copy + 2 * pad_blocks) // 2 + m * bpk
            bias = jnp.concatenate([tab_ref[copy, a, first + t] for t in range(bpk)], axis=1)
            bias = bias + mask_ref[i:i + 1, m * kw:(m + 1) * kw]
            row_blocks.append(s[i * GRID_W:(i + 1) * GRID_W, :] + bias)
        s_parts.append(jnp.concatenate(row_blocks, axis=0))
    return _softmax_pv(s_parts + [s_ctx], vs + [vc])


def _na_head_interior(s_raw, s_ctx, vs, vc, tab_ref, mask_ref, a, d0, pad_blocks):
    kw = s_raw[0].shape[1]
    bpk = kw // LANES
    nblk = len(s_raw) * bpk
    p_rows, pc_rows, l_rows = [], [], []
    for i in range(NA_ROWS):
        rs = slice(i * GRID_W, (i + 1) * GRID_W)
        lo_lane, hi_lane = i * GRID_W, (i + NA_KH) * GRID_W
        b_lo, b_hi = lo_lane // LANES, -(-hi_lane // LANES)
        copy = (d0 - i) % 2
        first = (d0 - i - copy + 2 * pad_blocks) // 2
        blocks = []
        for b in range(b_lo, b_hi):
            ls = slice((b % bpk) * LANES, (b % bpk + 1) * LANES)
            sb = s_raw[b // bpk][rs, ls] + tab_ref[copy, a, first + b]
            if b * LANES < lo_lane or (b + 1) * LANES > hi_lane:
                sb = sb + mask_ref[i:i + 1, b * LANES:(b + 1) * LANES]
            blocks.append(sb)
        sw = jnp.concatenate(blocks, axis=1)
        sc = s_ctx[rs]
        m = jnp.maximum(jnp.max(sw, axis=-1, keepdims=True), jnp.max(sc, axis=-1, keepdims=True))
        pw = jnp.exp2(sw - m)
        pc = jnp.exp2(sc - m)
        l_rows.append(jnp.sum(pw, axis=-1, keepdims=True) + jnp.sum(pc, axis=-1, keepdims=True))
        pieces = [jnp.zeros((GRID_W, b_lo * LANES), BF16), pw.astype(BF16),
                  jnp.zeros((GRID_W, (nblk - b_hi) * LANES), BF16)]
        p_rows.append(jnp.concatenate([x for x in pieces if x.shape[1]], axis=1))
        pc_rows.append(pc.astype(BF16))
    p = jnp.concatenate(p_rows, axis=0)
    acc = _dot(jnp.concatenate(pc_rows, axis=0), vc)
    for m, v in enumerate(vs):
        acc = acc + _dot(p[:, m * kw:(m + 1) * kw], v)
    return acc / jnp.concatenate(l_rows, axis=0)


def _na_kernel(q_ref, k0, k1, k2, k3, v0, v1, v2, v3, kc_ref, vc_ref, tab_ref, mask_ref, o_ref, *,
               groups, dvals):
    g = pl.program_id(1)
    is_edge = jnp.logical_or(g == 0, g == groups - 1)
    pad_blocks = NA_ROWS * GRID_W // LANES

    def step(interior):
        q2 = q_ref[...] * (NA_HD ** -0.5 * LOG2_E)
        lane = lax.broadcasted_iota(jnp.int32, q2.shape, 1)
        ks = [k[...].astype(BF16) for k in (k0, k1, k2, k3)]
        vs = [v[...].astype(BF16) for v in (v0, v1, v2, v3)]
        kc = kc_ref[...].astype(BF16)
        vc = vc_ref[...].astype(BF16)
        scores = []
        for a in range(2):
            sel = (lane < NA_HD) if a == 0 else (lane >= NA_HD)
            qa = jnp.where(sel, q2, 0.0).astype(BF16)
            scores.append(([_dot_nt(qa, k) for k in ks], _dot_nt(qa, kc)))
        outs = []
        for a, (s_raw, s_ctx) in enumerate(scores):
            if interior:
                outs.append(_na_head_interior(s_raw, s_ctx, vs, vc, tab_ref, mask_ref, a, dvals[1], pad_blocks))
            else:
                dcls = jnp.where(g == 0, dvals[0], dvals[2])
                outs.append(_na_head_edge(s_raw, s_ctx, vs, vc, tab_ref, mask_ref, a, dcls, dvals[0], pad_blocks))
        o_ref[...] = jnp.where(lane < NA_HD, outs[0], outs[1]).astype(o_ref.dtype)

    @pl.when(is_edge)
    def _():
        step(False)

    @pl.when(jnp.logical_not(is_edge))
    def _():
        step(True)


def _na_latent(u, bias, head_base, nbatch, seq, lat_rows, off_na):
    tab, mask, dvals = bias
    rows = seq // GRID_W
    groups = rows // NA_ROWS
    nq = NA_ROWS * GRID_W
    kblk = ROW_TILE
    nkb = NA_KEY_ROWS * GRID_W // kblk
    assert nkb == 4
    qcol = off_na // LANES
    heads2 = NA_HEADS * NA_HD // LANES
    kcol, vcol = qcol + heads2, qcol + 2 * heads2
    kb_per_batch = seq // kblk
    kb_per_grow = GRID_W * NA_ROWS // kblk
    lat_kb = lat_rows // kblk

    def kmap(col, m):
        def f(h, g, b):
            st = jnp.clip(g * kb_per_grow - (NA_KH // 2) * GRID_W // kblk, 0, kb_per_batch - nkb)
            return (b * kb_per_batch + st + m, col + h)
        return f

    def mask_map(h, g, b):
        return (jnp.where(g == 0, 0, jnp.where(g == groups - 1, 2, 1)), 0, 0)

    in_specs = [pl.BlockSpec((nq, LANES), lambda h, g, b: (b * groups + g, qcol + h))]
    in_specs += [pl.BlockSpec((kblk, LANES), kmap(kcol, m)) for m in range(nkb)]
    in_specs += [pl.BlockSpec((kblk, LANES), kmap(vcol, m)) for m in range(nkb)]
    in_specs += [pl.BlockSpec((ROW_TILE, LANES), lambda h, g, b: (lat_kb + b, kcol + h)),
                 pl.BlockSpec((ROW_TILE, LANES), lambda h, g, b: (lat_kb + b, vcol + h)),
                 pl.BlockSpec((2, 2) + tab.shape[2:], lambda h, g, b: (0, head_base // 2 + h, 0, 0, 0)),
                 pl.BlockSpec((None,) + mask.shape[1:], mask_map)]
    return pl.pallas_call(
        functools.partial(_na_kernel, groups=groups, dvals=dvals),
        grid=(heads2, groups, nbatch),
        in_specs=in_specs,
        out_specs=pl.BlockSpec((nq, LANES), lambda h, g, b: (b * groups + g, h)),
        out_shape=jax.ShapeDtypeStruct((lat_rows, NA_HEADS * NA_HD), BF16),
        compiler_params=_cparams(("arbitrary", "arbitrary", "arbitrary"), 40),
    )(*([u] * 11), tab, mask)


def _ctx_attn_kernel(q_ref, k_ref, v_ref, o_ref):
    q2 = q_ref[...] * (NA_HD ** -0.5 * LOG2_E)
    lane = lax.broadcasted_iota(jnp.int32, q2.shape, 1)
    k = k_ref[...].astype(BF16)
    v = v_ref[...].astype(BF16)
    outs = []
    for a in range(2):
        sel = (lane < NA_HD) if a == 0 else (lane >= NA_HD)
        qa = jnp.where(sel, q2, 0.0).astype(BF16)
        outs.append(_softmax_pv([_dot_nt(qa, k)], [v]))
    o_ref[...] = jnp.where(lane < NA_HD, outs[0], outs[1]).astype(o_ref.dtype)


def _ctx_attn(u, nbatch, ctx_len, lat_rows, off_na):
    assert ctx_len == ROW_TILE
    qcol = off_na // LANES
    heads2 = NA_HEADS * NA_HD // LANES
    base = lat_rows // ROW_TILE
    spec = lambda col: pl.BlockSpec((ROW_TILE, LANES), lambda b, h: (base + b, col + h))
    return pl.pallas_call(
        _ctx_attn_kernel,
        grid=(nbatch, heads2),
        in_specs=[spec(qcol), spec(qcol + heads2), spec(qcol + 2 * heads2)],
        out_specs=pl.BlockSpec((ROW_TILE, LANES), lambda b, h: (b, h)),
        out_shape=jax.ShapeDtypeStruct((nbatch * ctx_len, NA_HEADS * NA_HD), BF16),
        compiler_params=_cparams(("arbitrary", "arbitrary"), 16),
    )(u, u, u)


def _hg_level_map(rev):
    size = HG_BLOCK // 2
    t = np.arange(size)[:, None]
    s = np.arange(size)[None, :]
    x = t ^ s
    lvl = np.where(x > 0, np.frexp(np.maximum(x, 1))[1] - 1, -1)
    causal = (s < t) if not rev else (s > t)
    out = np.where(causal, lvl, -1)
    out = np.where(t == s, int(np.log2(size)), out)
    return out.astype(np.int32)


def _hg_tri(rev):
    t = np.arange(HG_BLOCK)[:, None]
    s = np.arange(HG_BLOCK)[None, :]
    return ((s <= t) if not rev else (s >= t)).astype(np.float32)


def _hg_anchor(b3, m, rev):
    nv = b3.shape[0]
    if m >= SUBLANES:
        w = m // SUBLANES
        b4 = b3.reshape(nv // (2 * w), 2 * w, SUBLANES, LANES)
        a = b4[:, w:w + 1, 0:1, :] if rev else b4[:, w - 1:w, SUBLANES - 1:SUBLANES, :]
        return jnp.broadcast_to(a, b4.shape).reshape(b3.shape)
    sub = lax.broadcasted_iota(jnp.int32, b3.shape, 1)
    out = None
    for g in range(SUBLANES // (2 * m)):
        idx = g * 2 * m + (m if rev else m - 1)
        a = jnp.broadcast_to(b3[:, idx:idx + 1, :], b3.shape)
        out = a if out is None else jnp.where(sub >= g * 2 * m, a, out)
    return out


def _neg_abs(x):
    bits = lax.bitcast_convert_type(x, jnp.uint32) | jnp.uint32(0x80000000)
    return lax.bitcast_convert_type(bits, F32)


def _hg_pick(q3, k3, m, rev):
    nv = q3.shape[0]
    if m >= SUBLANES:
        w = m // SUBLANES
        shape4 = (nv // (2 * w), 2 * w, SUBLANES, LANES)
        q4, k4 = q3.reshape(shape4), k3.reshape(shape4)
        lower, upper = (q4, k4) if rev else (k4, q4)
        return jnp.concatenate([lower[:, :w], upper[:, w:]], axis=1).reshape(q3.shape)
    upper_rows = (lax.broadcasted_iota(jnp.int32, q3.shape, 1) & m) != 0
    return jnp.where(upper_rows, k3 if rev else q3, q3 if rev else k3)


def _hg_block(q, v, z, alog, clog, oml, tri, lv, st, rev):
    n = q.shape[0]
    half = n // 2
    nlev = int(np.log2(n))
    q = _silu(q)
    t = jnp.exp(-jnp.abs(z))
    lsig = jnp.minimum(z, 0.0) - jnp.log(1.0 + t)
    cc = clog + lsig
    logf = jnp.maximum(alog, cc) + jnp.log(1.0 + jnp.exp(-jnp.abs(alog - cc)))
    kk = oml * jnp.where(z >= 0.0, t, 1.0) / (1.0 + t)

    hi = logf.astype(BF16)
    r1 = logf - hi.astype(F32)
    mid = r1.astype(BF16)
    lo = (r1 - mid.astype(F32)).astype(BF16)
    b = (_dot(tri, hi) + _dot(tri, mid) + _dot(tri, lo)) * LOG2_E

    shape3 = (n // SUBLANES, SUBLANES, LANES)
    b3, q3, k3 = b.reshape(shape3), q.reshape(shape3), kk.reshape(shape3)
    halves = (slice(0, half), slice(half, n))
    qb, kb = q.astype(BF16), kk.astype(BF16)
    acc = [jnp.where(lv == nlev - 1, _dot_nt(qb[hs], kb[hs]), 0.0) for hs in halves]
    for lev in range(nlev - 1):
        m = 1 << lev
        e = jnp.exp2(_neg_abs(b3 - _hg_anchor(b3, m, rev)))
        w = (_hg_pick(q3, k3, m, rev) * e).reshape(n, LANES).astype(BF16)
        acc = [jnp.where(lv == lev, _dot_nt(w[hs], w[hs]), a) for hs, a in zip(halves, acc)]
    first, second = (halves[1], halves[0]) if rev else halves
    e = jnp.exp2(_neg_abs(b - (b[half:half + 1, :] if rev else b[half - 1:half, :])))
    top = _dot_nt((q[second] * e[second]).astype(BF16), (kk[first] * e[first]).astype(BF16))

    vb = v.astype(BF16)
    a0, a1 = acc[0].astype(BF16), acc[1].astype(BF16)
    tb = top.astype(BF16)
    if rev:
        o_lo = _dot(jnp.concatenate([a0, tb], axis=1), vb)
        o_hi = _dot(a1, vb[halves[1]])
    else:
        o_lo = _dot(a0, vb[halves[0]])
        o_hi = _dot(jnp.concatenate([tb, a1], axis=1), vb)
    b_last = b[0:1, :] if rev else b[n - 1:n, :]
    qh = (q * jnp.exp2(b)).astype(BF16)
    o = jnp.concatenate([o_lo, o_hi], axis=0) + _dot_nt(qh, st.astype(BF16))
    kh = (kk * jnp.exp2(b_last - b)).astype(BF16)
    st_new = st * jnp.exp2(b_last) + _dot_tn(vb, kh)
    return o, st_new


def _hg_fwd_kernel(q_ref, v_ref, z_ref, al_ref, cl_ref, om_ref, tri_ref, lv_ref, o_ref, st_ref):
    @pl.when(pl.program_id(1) == 0)
    def _():
        st_ref[...] = jnp.zeros_like(st_ref)

    tri = tri_ref[...]
    lv = lv_ref[...]
    for h in range(HG_HEADS):
        hs = slice(h * HG_DK, (h + 1) * HG_DK)
        o, st = _hg_block(q_ref[:, hs], v_ref[:, hs], z_ref[:, hs], al_ref[:, hs], cl_ref[:, hs],
                          om_ref[:, hs], tri, lv, st_ref[h], False)
        o_ref[:, hs] = o
        st_ref[h] = st


def _hg_bwd_kernel(q_ref, v_ref, z_ref, g_ref, of_ref, al_ref, cl_ref, om_ref, ng_ref, tri_ref, lv_ref,
                   o_ref, st_ref):
    @pl.when(pl.program_id(1) == 0)
    def _():
        st_ref[...] = jnp.zeros_like(st_ref)

    tri = tri_ref[...]
    lv = lv_ref[...]
    for h in range(HG_HEADS):
        hs = slice(h * HG_DK, (h + 1) * HG_DK)
        o, st = _hg_block(q_ref[:, hs], v_ref[:, hs], z_ref[:, hs], al_ref[:, hs], cl_ref[:, hs],
                          om_ref[:, hs], tri, lv, st_ref[h], True)
        st_ref[h] = st
        t = of_ref[:, hs] + o
        y = t * lax.rsqrt(jnp.mean(t * t, axis=-1, keepdims=True) + EPS)
        o_ref[:, hs] = (y * ng_ref[:, hs] * _silu(g_ref[:, hs])).astype(o_ref.dtype)


def _hgrn(u, lb, norm_g, nbatch, seq, ctx_len, lat_rows, off_hg):
    assert ctx_len == HG_BLOCK
    n = u.shape[0]
    hd = HG_HEADS * HG_DK
    col = off_hg // hd
    per = seq // HG_BLOCK
    lat_blocks = lat_rows // HG_BLOCK
    lbf = lb.astype(F32)
    alog, clog, oml = jnp.log(lbf), jnp.log1p(-lbf), 1.0 - lbf

    def fmap(c):
        return lambda b, j: (jnp.where(j == 0, lat_blocks + b, b * per + j - 1), c)

    def bmap(c):
        return lambda b, j: (jnp.where(j == 0, lat_blocks + b, b * per + per - j), c)

    const = lambda shape: pl.BlockSpec(shape, lambda b, j: (0, 0))
    grid = (nbatch, per + 1)
    vec = lambda a: a.reshape(1, hd)
    o_f = pl.pallas_call(
        _hg_fwd_kernel,
        grid=grid,
        in_specs=[pl.BlockSpec((HG_BLOCK, hd), fmap(col)), pl.BlockSpec((HG_BLOCK, hd), fmap(col + 1)),
                  pl.BlockSpec((HG_BLOCK, hd), fmap(col + 2)),
                  const((1, hd)), const((1, hd)), const((1, hd)),
                  const((HG_BLOCK, HG_BLOCK)), const((HG_BLOCK // 2, HG_BLOCK // 2))],
        out_specs=pl.BlockSpec((HG_BLOCK, hd), fmap(0)),
        out_shape=jax.ShapeDtypeStruct((n, hd), F32),
        scratch_shapes=[pltpu.VMEM((HG_HEADS, HG_DK, HG_DK), F32)],
        compiler_params=_cparams(("arbitrary", "arbitrary"), 32),
    )(u, u, u, vec(alog[0]), vec(clog[0]), vec(oml[0]),
      jnp.asarray(_hg_tri(False), BF16), jnp.asarray(_hg_level_map(False)))
    return pl.pallas_call(
        _hg_bwd_kernel,
        grid=grid,
        in_specs=[pl.BlockSpec((HG_BLOCK, hd), bmap(col)), pl.BlockSpec((HG_BLOCK, hd), bmap(col + 1)),
                  pl.BlockSpec((HG_BLOCK, hd), bmap(col + 3)), pl.BlockSpec((HG_BLOCK, hd), bmap(col + 4)),
                  pl.BlockSpec((HG_BLOCK, hd), bmap(0)),
                  const((1, hd)), const((1, hd)), const((1, hd)), const((1, hd)),
                  const((HG_BLOCK, HG_BLOCK)), const((HG_BLOCK // 2, HG_BLOCK // 2))],
        out_specs=pl.BlockSpec((HG_BLOCK, hd), bmap(0)),
        out_shape=jax.ShapeDtypeStruct((n, hd), BF16),
        scratch_shapes=[pltpu.VMEM((HG_HEADS, HG_DK, HG_DK), F32)],
        compiler_params=_cparams(("arbitrary", "arbitrary"), 32),
    )(u, u, u, u, o_f, vec(alog[1]), vec(clog[1]), vec(oml[1]), vec(norm_g.astype(F32)),
      jnp.asarray(_hg_tri(True), BF16), jnp.asarray(_hg_level_map(True)))


def _out_kernel(x_ref, cv_ref, na_ref, hg_ref, w_ref, ga_ref, g2_ref, sh2_ref, s2_ref, wrh_ref, wrl_ref, br_ref,
                xo_ref, h_ref, rt_ref, *, tiles_per_batch, nbatch):
    r = jnp.minimum(pl.program_id(0) // tiles_per_batch, nbatch)
    c0 = cv_ref.shape[1]
    c1 = c0 + na_ref.shape[1]
    mix = (_dot(cv_ref[...], w_ref[0:c0, :]) + _dot(na_ref[...], w_ref[c0:c1, :])
           + _dot(hg_ref[...], w_ref[c1:, :]))
    xn = x_ref[...] + ga_ref[pl.ds(r, 1), :] * mix
    xo_ref[...] = xn
    h = _rms_mod(xn, g2_ref[...], s2_ref[pl.ds(r, 1), :], sh2_ref[pl.ds(r, 1), :])
    h_ref[...] = h.astype(h_ref.dtype)
    h_hi = h.astype(BF16)
    h_lo = (h - h_hi.astype(F32)).astype(BF16)
    logits = (_dot(h_hi, wrh_ref[...]) + (_dot(h_lo, wrh_ref[...]) + _dot(h_hi, wrl_ref[...]))
              + br_ref[...])
    rt_ref[...] = _route_rows(logits)


def _route_rows(lg):
    lane = lax.broadcasted_iota(jnp.int32, lg.shape, 1)
    big = jnp.int32(2 ** 30)
    low = jnp.float32(-3e38)

    def first_max(vals, mask):
        m = jnp.max(vals, axis=-1, keepdims=True)
        idx = jnp.min(jnp.where(jnp.logical_and(vals == m, mask), lane, big), axis=-1, keepdims=True)
        return m, idx

    gmask = lane < N_GROUPS
    gl = jnp.where(gmask, lg, low)
    gm, grp = first_max(gl, gmask)
    p_grp = 1.0 / jnp.sum(jnp.where(gmask, jnp.exp(gl - gm), 0.0), axis=-1, keepdims=True)
    lo = N_GROUPS + grp * EXP_PER_GROUP
    emask = jnp.logical_and(lane >= lo, lane < lo + EXP_PER_GROUP)
    el = jnp.where(emask, lg, low)
    m1, i1 = first_max(el, emask)
    emask2 = jnp.logical_and(emask, lane != i1)
    el2 = jnp.where(emask2, lg, low)
    m2, i2 = first_max(el2, emask2)
    t = jnp.exp(m2 - m1)
    w1 = p_grp / (1.0 + t)
    w2 = p_grp * t / (1.0 + t)
    e1 = (i1 - N_GROUPS).astype(F32)
    e2 = (i2 - N_GROUPS).astype(F32)
    return jnp.where(lane == 0, e1, jnp.where(lane == 1, e2, jnp.where(lane == 2, w1,
                     jnp.where(lane == 3, w2, 0.0))))


def _out_proj(x, conv, na, hg, w_bf16, l, mod, g_ffn, w_router, b_router, n_rows, nbatch, lat_rows):
    d = x.shape[1]
    w_router_hi = w_router.astype(BF16)
    w_router_lo = (w_router - w_router_hi.astype(F32)).astype(BF16)
    tm = _pick_tile(OUT_TM, lat_rows // nbatch, n_rows)
    kern = functools.partial(_out_kernel, tiles_per_batch=lat_rows // nbatch // tm, nbatch=nbatch)
    row = lambda w: pl.BlockSpec((tm, w), lambda i: (i, 0))
    const = lambda shape: pl.BlockSpec(shape, lambda i: (0, 0))
    modc = lambda c: pl.BlockSpec((None, SUBLANES, d), lambda i: (l, 0, c))
    return pl.pallas_call(
        kern,
        grid=(n_rows // tm,),
        in_specs=[row(d), row(conv.shape[1]), row(na.shape[1]), row(hg.shape[1]),
                  pl.BlockSpec((None, d, d), lambda i: (l, 0, 0)),
                  modc(2),
                  const((1, d)),
                  modc(3),
                  modc(4),
                  const((d, ROUTER_PAD)), const((d, ROUTER_PAD)), const((1, ROUTER_PAD))],
        out_specs=[row(d), row(d), row(ROUTER_PAD)],
        out_shape=[jax.ShapeDtypeStruct((n_rows, d), F32), jax.ShapeDtypeStruct((n_rows, d), BF16),
                   jax.ShapeDtypeStruct((n_rows, ROUTER_PAD), F32)],
        compiler_params=_cparams(("arbitrary",), 56),
    )(x, conv, na, hg, w_bf16, mod, g_ffn.reshape(1, d), mod, mod, w_router_hi, w_router_lo, b_router)


def _moe_kernel(be_ref, nu_ref, xs_ref, w1_ref, w3_ref, w2_ref, sw_ref, o_ref, w1b, w3b, w2b):
    i = pl.program_id(0)
    e = be_ref[i]
    prev = be_ref[jnp.maximum(i - 1, 0)]

    @pl.when(jnp.logical_or(i == 0, e != prev))
    def _():
        w1b[...] = w1_ref[...].astype(BF16)
        w3b[...] = w3_ref[...].astype(BF16)
        w2b[...] = w2_ref[...].astype(BF16)

    @pl.when(i < nu_ref[0])
    def _():
        x = xs_ref[...]
        a = (_silu(_dot(x, w1b[...])) * _dot(x, w3b[...])).astype(BF16)
        o_ref[...] = (_dot(a, w2b[...]) * sw_ref[...]).astype(o_ref.dtype)

    @pl.when(i >= nu_ref[0])
    def _():
        o_ref[...] = jnp.zeros_like(o_ref)


def _moe_experts(xs, slot_w, blk_e, nused, w1, w3, w2, l):
    p, d = xs.shape
    de = w1.shape[3]
    bm = MOE_BM
    grid_spec = pltpu.PrefetchScalarGridSpec(
        num_scalar_prefetch=2,
        grid=(p // bm,),
        in_specs=[pl.BlockSpec((bm, d), lambda i, be, nu: (i, 0)),
                  pl.BlockSpec((None, None, d, de), lambda i, be, nu: (l, be[i], 0, 0)),
                  pl.BlockSpec((None, None, d, de), lambda i, be, nu: (l, be[i], 0, 0)),
                  pl.BlockSpec((None, None, de, d), lambda i, be, nu: (l, be[i], 0, 0)),
                  pl.BlockSpec((bm, 1), lambda i, be, nu: (i, 0))],
        out_specs=pl.BlockSpec((bm, d), lambda i, be, nu: (i, 0)),
        scratch_shapes=[pltpu.VMEM((d, de), BF16), pltpu.VMEM((d, de), BF16), pltpu.VMEM((de, d), BF16)],
    )
    return pl.pallas_call(
        _moe_kernel,
        grid_spec=grid_spec,
        out_shape=jax.ShapeDtypeStruct((p, d), BF16),
        compiler_params=_cparams(("arbitrary",), 48),
    )(blk_e, nused, xs, w1, w3, w2, slot_w.reshape(p, 1))


def _rows(a, idx):
    return a.at[idx].get(mode="promise_in_bounds")


def _route_meta(route, n):
    i32 = jnp.int32
    eid = route[:, 0:TOP_K].astype(i32).reshape(-1)
    wt = route[:, TOP_K:2 * TOP_K].reshape(-1)
    a = n * TOP_K
    bm = MOE_BM
    nblk = -(-a // bm) + N_EXPERTS
    p = nblk * bm
    experts = jnp.arange(N_EXPERTS, dtype=i32)[None, :]
    ja = jnp.arange(a, dtype=i32)
    se, order, wsort = lax.sort((eid, ja, wt), num_keys=1, is_stable=True)
    cnt = jnp.sum((eid[:, None] == experts).astype(i32), axis=0)
    pcnt = (cnt + bm - 1) // bm * bm
    pend = jnp.cumsum(pcnt)
    pstart = pend - pcnt
    end = jnp.cumsum(cnt)
    start = end - cnt
    off = pstart - start
    d_off = off - jnp.concatenate([jnp.zeros((1,), i32), off[:-1]])
    dst_sorted = ja + jnp.sum(jnp.where(ja[:, None] >= start[None, :], d_off[None, :], 0), axis=1)
    _, pos = lax.sort((order, dst_sorted), num_keys=1)
    jp = jnp.arange(p, dtype=i32)
    in_or_after = jp[:, None] >= pstart[None, :]
    src = jp - jnp.sum(jnp.where(in_or_after, d_off[None, :], 0), axis=1)
    valid = src < jnp.sum(jnp.where(in_or_after, cnt[None, :], 0), axis=1)
    src = jnp.where(valid, src, jp % a)
    slot_tok = _rows(order, src) // TOP_K
    slot_w = jnp.where(valid, _rows(wsort, src), 0.0)
    jb = jnp.arange(nblk, dtype=i32) * bm
    blk_e = jnp.minimum(jnp.sum((jb[:, None] >= pend[None, :]).astype(i32), axis=1), N_EXPERTS - 1)
    nused = (pend[-1:] // bm).astype(i32)
    return slot_tok, slot_w, pos.reshape(n, TOP_K), blk_e, nused


def _combine_kernel(x_ref, y0_ref, y1_ref, ga_ref, gf_ref, o_ref, *, tiles_per_batch, nbatch, final):
    r = jnp.minimum(pl.program_id(0) // tiles_per_batch, nbatch)
    xn = x_ref[...] + ga_ref[pl.ds(r, 1), :] * (y0_ref[...].astype(F32) + y1_ref[...].astype(F32))
    if final:
        xn = xn * lax.rsqrt(jnp.mean(xn * xn, axis=-1, keepdims=True) + EPS) * gf_ref[...]
    o_ref[...] = xn


def _combine(x, y0, y1, mod, l, g_final, n_rows, nbatch, lat_rows, final):
    d = x.shape[1]
    tm = _pick_tile(OUT_TM, lat_rows // nbatch, n_rows)
    kern = functools.partial(_combine_kernel, tiles_per_batch=lat_rows // nbatch // tm, nbatch=nbatch,
                             final=final)
    row = pl.BlockSpec((tm, d), lambda i: (i, 0))
    return pl.pallas_call(
        kern,
        grid=(n_rows // tm,),
        in_specs=[row, row, row, pl.BlockSpec((None, SUBLANES, d), lambda i: (l, 0, 5)),
                  pl.BlockSpec((1, d), lambda i: (0, 0))],
        out_specs=row,
        out_shape=jax.ShapeDtypeStruct((n_rows, d), F32),
        compiler_params=_cparams(("arbitrary",), 40),
    )(x, y0, y1, mod, g_final.reshape(1, d))


def kernel(x, c, ctx, c_ctx, w_ada, b_ada, g_mix, g_ffn, w_in, conv_w, conv_b, conv_ln_g, conv_ln_b,
           na_rpb, hgrn_lb, hgrn_norm_g, w_out, w_router_group, b_router_group, w_router_expert,
           b_router_expert, w_exp_gate, w_exp_up, w_exp_down, g_final):
    nb, seq, d = x.shape
    ctx_len = ctx.shape[1]
    depth = w_ada.shape[0]
    lat_rows = nb * seq
    n_all = lat_rows + nb * ctx_len
    conv_ch = conv_w.shape[2]
    off_na = 2 * conv_ch
    off_hg = off_na + 3 * NA_HEADS * NA_HD
    rows = seq // GRID_W
    assert nb < SUBLANES and rows % NA_ROWS == 0 and rows >= NA_KEY_ROWS

    lbs = jnp.cumsum(jax.nn.softmax(hgrn_lb.astype(F32), axis=0), axis=0)
    lbs = lbs - lbs[:1]

    cond = jnp.concatenate([c, c_ctx[None, :], jnp.zeros((SUBLANES - nb - 1, d), F32)], axis=0)
    mod = _ada_mod(cond, w_ada, b_ada)

    xs = jnp.concatenate([x.reshape(lat_rows, d), ctx.reshape(nb * ctx_len, d)], axis=0)
    w_in_b = w_in.astype(BF16)
    w_out_b = w_out.astype(BF16)
    na_bias = _na_bias(na_rpb.reshape((depth * NA_HEADS,) + na_rpb.shape[2:]), rows)
    for l in range(depth):
        with_ctx = l < depth - 1
        n_act = n_all if with_ctx else lat_rows
        u = _norm_in(xs, g_mix[l], mod, w_in_b, l, nb, lat_rows)

        conv = _conv_module(u, conv_w[l], conv_b[l], conv_ln_g[l], conv_ln_b[l], n_act, lat_rows, seq)
        na = _na_latent(u, na_bias, l * NA_HEADS, nb, seq, lat_rows, off_na)
        if with_ctx:
            na = jnp.concatenate([na, _ctx_attn(u, nb, ctx_len, lat_rows, off_na)], axis=0)
        hg = _hgrn(u, lbs[l], hgrn_norm_g[l], nb, seq, ctx_len, lat_rows, off_hg)

        w_router = jnp.concatenate(
            [w_router_group[l], w_router_expert[l],
             jnp.zeros((d, ROUTER_PAD - N_GROUPS - N_EXPERTS), F32)], axis=1)
        b_router = jnp.concatenate(
            [b_router_group[l], b_router_expert[l],
             jnp.zeros((ROUTER_PAD - N_GROUPS - N_EXPERTS,), F32)]).reshape(1, ROUTER_PAD)
        x_mid, h, route = _out_proj(xs, conv, na, hg, w_out_b, l, mod, g_ffn[l],
                                    w_router, b_router, n_act, nb, lat_rows)

        slot_tok, slot_w, pos, blk_e, nused = _route_meta(route, n_act)
        ys = _moe_experts(_rows(h, slot_tok), slot_w, blk_e, nused, w_exp_gate, w_exp_up, w_exp_down, l)
        y0 = _rows(ys, pos[:, 0])
        y1 = _rows(ys, pos[:, 1])
        xs = _combine(x_mid, y0, y1, mod, l, g_final, n_act, nb, lat_rows, final=not with_ctx)
    return xs.reshape(nb, seq, d)
```

```python
import functools

import numpy as np
import jax
import jax.numpy as jnp
from jax import lax
from jax.experimental import pallas as pl
from jax.experimental.pallas import tpu as pltpu

F32 = jnp.float32
BF16 = jnp.bfloat16

EPS = 1e-6
NEG_INF = -1e30
LOG2_E = 1.4426950408889634

GRID_W = 64
CONV_K = 31
NA_HEADS = 16
NA_HD = 64
NA_KH = 8
NA_KW = 16
HG_HEADS = 4
HG_DK = 128
N_GROUPS = 4
EXP_PER_GROUP = 8
N_EXPERTS = N_GROUPS * EXP_PER_GROUP
TOP_K = 2

LANES = 128
SUBLANES = 8
VMEM_BYTES = 64 * 1024 * 1024

ROW_TILE = 256
IN_TM = 1024
IN_TN = 512
OUT_TM = 512
NA_STEP_HEADS = 4
NA_ROWS = 8
NA_KEY_ROWS = 16
HG_BLOCK = 256
MOE_BM = 512
ROUTER_PAD = LANES
HALO = 16


def _pick_tile(pref, *extents):
    t = pref
    while t > ROW_TILE and any(e % t for e in extents):
        t //= 2
    assert all(e % t == 0 for e in extents)
    return t


def _cparams(sem, vmem_mb):
    return pltpu.CompilerParams(dimension_semantics=sem, vmem_limit_bytes=vmem_mb * 1024 * 1024)


def _dot(a, b):
    return jnp.dot(a, b, preferred_element_type=F32)


def _dot_nt(a, b):
    return lax.dot_general(a, b, (((1,), (1,)), ((), ())), preferred_element_type=F32)


def _dot_tn(a, b):
    return lax.dot_general(a, b, (((0,), (0,)), ((), ())), preferred_element_type=F32)


def _sigmoid(x):
    return 1.0 / (1.0 + jnp.exp(-x))


def _silu(x):
    return x * _sigmoid(x)


def _split_bf16(x):
    hi = x.astype(BF16)
    return hi, (x - hi.astype(F32)).astype(BF16)


def _ada_kernel(c_ref, w_ref, b_ref, o_ref):
    s_hi, s_lo = _split_bf16(_silu(c_ref[...]))
    w_hi, w_lo = _split_bf16(w_ref[0])
    o_ref[0] = _dot(s_hi, w_hi) + (_dot(s_lo, w_hi) + _dot(s_hi, w_lo)) + b_ref[0]


def _ada_mod(cond, w_ada, b_ada):
    depth, d, n = w_ada.shape
    tn = 1024
    return pl.pallas_call(
        _ada_kernel,
        grid=(depth, n // tn),
        in_specs=[
            pl.BlockSpec((SUBLANES, d), lambda l, j: (0, 0)),
            pl.BlockSpec((1, d, tn), lambda l, j: (l, 0, j)),
            pl.BlockSpec((1, 1, tn), lambda l, j: (l, 0, j)),
        ],
        out_specs=pl.BlockSpec((1, SUBLANES, tn), lambda l, j: (l, 0, j)),
        out_shape=jax.ShapeDtypeStruct((depth, SUBLANES, n), F32),
        compiler_params=_cparams(("arbitrary", "arbitrary"), 40),
    )(cond, w_ada, b_ada.reshape(depth, 1, n))


def _rms_mod(x, g, scale, shift):
    y = x * lax.rsqrt(jnp.mean(x * x, axis=-1, keepdims=True) + EPS)
    return (y * g) * (1.0 + scale) + shift


def _norm_in_kernel(x_ref, g_ref, sh_ref, sc_ref, w_ref, o_ref, h_ref, *, tiles_per_batch, nbatch):
    i = pl.program_id(0)

    @pl.when(pl.program_id(1) == 0)
    def _():
        r = jnp.minimum(i // tiles_per_batch, nbatch)
        h = _rms_mod(x_ref[...], g_ref[...], sc_ref[pl.ds(r, 1), :], sh_ref[pl.ds(r, 1), :])
        h_ref[...] = h.astype(BF16)

    o_ref[...] = _dot(h_ref[...], w_ref[...])


def _norm_in(x, g, mod, w_bf16, l, nbatch, lat_rows):
    n, d = x.shape
    nout = w_bf16.shape[2]
    tm, tn = _pick_tile(IN_TM, lat_rows // nbatch, n), IN_TN
    kern = functools.partial(_norm_in_kernel, tiles_per_batch=lat_rows // nbatch // tm, nbatch=nbatch)
    return pl.pallas_call(
        kern,
        grid=(n // tm, nout // tn),
        in_specs=[
            pl.BlockSpec((tm, d), lambda i, j: (i, 0)),
            pl.BlockSpec((1, d), lambda i, j: (0, 0)),
            pl.BlockSpec((None, SUBLANES, d), lambda i, j: (l, 0, 0)),
            pl.BlockSpec((None, SUBLANES, d), lambda i, j: (l, 0, 1)),
            pl.BlockSpec((None, d, tn), lambda i, j: (l, 0, j)),
        ],
        out_specs=pl.BlockSpec((tm, tn), lambda i, j: (i, j)),
        out_shape=jax.ShapeDtypeStruct((n, nout), F32),
        scratch_shapes=[pltpu.VMEM((tm, d), BF16)],
        compiler_params=_cparams(("arbitrary", "arbitrary"), 48),
    )(x, g.reshape(1, d), mod, mod, w_bf16)


def _conv_kernel(ap_ref, gp_ref, a_ref, gt_ref, an_ref, gn_ref, w_ref, b_ref, lg_ref, lb_ref,
                 o_ref, buf_ref, acc_ref, *, lat_tiles, tiles_per_seq):
    i = pl.program_id(0)
    tc, ch = a_ref.shape
    is_lat = i < lat_tiles
    pos = i % tiles_per_seq
    first = jnp.logical_or(jnp.logical_not(is_lat), pos == 0)
    last = jnp.logical_or(jnp.logical_not(is_lat), pos == tiles_per_seq - 1)

    buf_ref[0:HALO] = jnp.where(first, 0.0, ap_ref[...] * _sigmoid(gp_ref[...]))
    buf_ref[HALO:HALO + tc] = a_ref[...] * _sigmoid(gt_ref[...])
    buf_ref[HALO + tc:2 * HALO + tc] = jnp.where(last, 0.0, an_ref[...] * _sigmoid(gn_ref[...]))

    rows = 64
    base = HALO - CONV_K // 2
    for c in range(ch // LANES):
        cs = slice(c * LANES, (c + 1) * LANES)
        for r in range(tc // rows):
            acc = None
            for res in range(SUBLANES):
                y = None
                for k in range(CONV_K):
                    if (base + k) % SUBLANES != res:
                        continue
                    lo = r * rows + (base + k) // SUBLANES * SUBLANES
                    term = w_ref[k:k + 1, cs] * buf_ref[lo:lo + rows + SUBLANES, cs]
                    y = term if y is None else y + term
                if y is not None:
                    y = y[res:res + rows]
                    acc = y if acc is None else acc + y
            acc_ref[r * rows:(r + 1) * rows, cs] = acc

    h = acc_ref[...] + b_ref[...]
    mu = jnp.mean(h, axis=-1, keepdims=True)
    var = jnp.mean(jnp.square(h - mu), axis=-1, keepdims=True)
    y = (h - mu) * lax.rsqrt(var + EPS) * lg_ref[...] + lb_ref[...]
    o_ref[...] = _silu(y).astype(o_ref.dtype)


def _conv_module(u, w_dw, b_dw, ln_g, ln_b, n_rows, lat_rows, seq):
    ch = w_dw.shape[1]
    tc = ROW_TILE
    per = tc // HALO
    nh = u.shape[0] // HALO
    kern = functools.partial(_conv_kernel, lat_tiles=lat_rows // tc, tiles_per_seq=seq // tc)
    prev_map = lambda c: (lambda i: (jnp.maximum(i * per - 1, 0), c))
    next_map = lambda c: (lambda i: (jnp.minimum((i + 1) * per, nh - 1), c))
    vec = lambda a: a.reshape(1, ch)
    return pl.pallas_call(
        kern,
        grid=(n_rows // tc,),
        in_specs=[
            pl.BlockSpec((HALO, ch), prev_map(0)),
            pl.BlockSpec((HALO, ch), prev_map(1)),
            pl.BlockSpec((tc, ch), lambda i: (i, 0)),
            pl.BlockSpec((tc, ch), lambda i: (i, 1)),
            pl.BlockSpec((HALO, ch), next_map(0)),
            pl.BlockSpec((HALO, ch), next_map(1)),
            pl.BlockSpec((CONV_K, ch), lambda i: (0, 0)),
            pl.BlockSpec((1, ch), lambda i: (0, 0)),
            pl.BlockSpec((1, ch), lambda i: (0, 0)),
            pl.BlockSpec((1, ch), lambda i: (0, 0)),
        ],
        out_specs=pl.BlockSpec((tc, ch), lambda i: (i, 0)),
        out_shape=jax.ShapeDtypeStruct((n_rows, ch), BF16),
        scratch_shapes=[pltpu.VMEM((tc + 2 * HALO, ch), F32), pltpu.VMEM((tc, ch), F32)],
        compiler_params=_cparams(("arbitrary",), 16),
    )(u, u, u, u, u, u, w_dw, vec(b_dw), vec(ln_g), vec(ln_b))


def _na_bias_tables(rows):
    groups = rows // NA_ROWS
    reps = [0, min(1, groups - 1), groups - 1]
    out = []
    for g in reps:
        start = int(np.clip(NA_ROWS * g - NA_KH // 2, 0, rows - NA_KEY_ROWS))
        per_row = []
        for i in range(NA_ROWS):
            r = NA_ROWS * g + i
            sr = int(np.clip(r - NA_KH // 2, 0, rows - NA_KH))
            per_row.append((sr - start, sr - r + NA_KH - 1))
        out.append(per_row)
    return out


def _na_bias(rpb, rows):
    nh = rpb.shape[0]
    ndr, ndc = 2 * NA_KH - 1, 2 * NA_KW - 1
    period = 2 * GRID_W - 1
    pad = GRID_W - NA_KW
    vp = jnp.pad(rpb.astype(F32), ((0, 0), (0, 0), (pad, period - ndc - pad)))
    hank = jnp.tile(vp, (1, 1, GRID_W + 1))[:, :, :GRID_W * (period + 1)]
    hank = hank.reshape(nh, ndr, GRID_W, period + 1)[..., :GRID_W]
    toe = hank[:, :, ::-1, :]
    c = np.arange(GRID_W)[:, None]
    j = np.arange(GRID_W)[None, :]
    ws = np.clip(c - NA_KW // 2, 0, GRID_W - NA_KW)
    col_ok = (j >= ws) & (j < ws + NA_KW)
    toe = jnp.where(col_ok[None, None], toe, NEG_INF)
    flat = jnp.transpose(toe, (0, 2, 1, 3)).reshape(nh, GRID_W, ndr * GRID_W)
    nk = NA_KEY_ROWS * GRID_W
    lpad = NA_ROWS * GRID_W
    total = 2 * nk

    def padded(shift):
        return jnp.pad(flat, ((0, 0), (0, 0), (lpad - shift, total - flat.shape[2] - lpad + shift)))

    tab = jnp.stack([padded(0), padded(GRID_W)])
    tab = tab.reshape(2, nh, GRID_W, total // LANES, LANES).transpose(0, 1, 3, 2, 4)
    tables = _na_bias_tables(rows)
    mask = np.full((len(tables), NA_ROWS, nk), NEG_INF, np.float32)
    dvals = []
    for cls, per_row in enumerate(tables):
        dvals.append(per_row[0][1] - per_row[0][0])
        for i, (off, lo) in enumerate(per_row):
            assert lo - off == dvals[-1] - i and -NA_ROWS <= lo - off < NA_ROWS
            mask[cls, i, off * GRID_W:(off + NA_KH) * GRID_W] = 0.0
    assert len({d % 2 for d in dvals}) == 1
    assert len(tables) < 3 or rows < 3 * NA_ROWS or all(off == i for i, (off, _) in enumerate(tables[1]))
    return tab * LOG2_E, jnp.asarray(mask * LOG2_E), tuple(dvals)


def _softmax_pv(s_parts, v_parts):
    m = functools.reduce(jnp.maximum, [jnp.max(s, axis=-1, keepdims=True) for s in s_parts])
    acc, l = None, None
    for s, v in zip(s_parts, v_parts):
        p = jnp.exp2(s - m)
        ps = jnp.sum(p, axis=-1, keepdims=True)
        pv = _dot(p.astype(BF16), v)
        l = ps if l is None else l + ps
        acc = pv if acc is None else acc + pv
    return acc / l


def _na_head_edge(s_raw, s_ctx, vs, vc, tab_ref, mask_ref, a, dcls, parity, pad_blocks):
    kw = s_raw[0].shape[1]
    bpk = kw // LANES
    s_parts = []
    for m, s in enumerate(s_raw):
        row_blocks = []
        for i in range(NA_ROWS):
            copy = (parity - i) % 2
            first = (dcls - i - copy + 2 * pad_blocks) // 2 + m * bpk
            bias = jnp.concatenate([tab_ref[copy, a, first + t] for t in range(bpk)], axis=1)
            bias = bias + mask_ref[i:i + 1, m * kw:(m + 1) * kw]
            row_blocks.append(s[i * GRID_W:(i + 1) * GRID_W, :] + bias)
        s_parts.append(jnp.concatenate(row_blocks, axis=0))
    return _softmax_pv(s_parts + [s_ctx], vs + [vc])


def _na_head_interior(s_raw, s_ctx, vs, vc, tab_ref, mask_ref, a, d0, pad_blocks):
    kw = s_raw[0].shape[1]
    bpk = kw // LANES
    nblk = len(s_raw) * bpk
    p_rows, pc_rows, l_rows = [], [], []
    for i in range(NA_ROWS):
        rs = slice(i * GRID_W, (i + 1) * GRID_W)
        lo_lane, hi_lane = i * GRID_W, (i + NA_KH) * GRID_W
        b_lo, b_hi = lo_lane // LANES, -(-hi_lane // LANES)
        copy = (d0 - i) % 2
        first = (d0 - i - copy + 2 * pad_blocks) // 2
        blocks = []
        for b in range(b_lo, b_hi):
            ls = slice((b % bpk) * LANES, (b % bpk + 1) * LANES)
            sb = s_raw[b // bpk][rs, ls] + tab_ref[copy, a, first + b]
            if b * LANES < lo_lane or (b + 1) * LANES > hi_lane:
                sb = sb + mask_ref[i:i + 1, b * LANES:(b + 1) * LANES]
            blocks.append(sb)
        sw = jnp.concatenate(blocks, axis=1)
        sc = s_ctx[rs]
        m = jnp.maximum(jnp.max(sw, axis=-1, keepdims=True), jnp.max(sc, axis=-1, keepdims=True))
        pw = jnp.exp2(sw - m)
        pc = jnp.exp2(sc - m)
        l_rows.append(jnp.sum(pw, axis=-1, keepdims=True) + jnp.sum(pc, axis=-1, keepdims=True))
        pieces = [jnp.zeros((GRID_W, b_lo * LANES), BF16), pw.astype(BF16),
                  jnp.zeros((GRID_W, (nblk - b_hi) * LANES), BF16)]
        p_rows.append(jnp.concatenate([x for x in pieces if x.shape[1]], axis=1))
        pc_rows.append(pc.astype(BF16))
    p = jnp.concatenate(p_rows, axis=0)
    acc = _dot(jnp.concatenate(pc_rows, axis=0), vc)
    for m, v in enumerate(vs):
        acc = acc + _dot(p[:, m * kw:(m + 1) * kw], v)
    return acc / jnp.concatenate(l_rows, axis=0)


def _na_kernel(q_ref, k0, k1, k2, k3, v0, v1, v2, v3, kc_ref, vc_ref, tab_ref, mask_ref, o_ref, *,
               groups, dvals):
    g = pl.program_id(1)
    is_edge = jnp.logical_or(g == 0, g == groups - 1)
    pad_blocks = NA_ROWS * GRID_W // LANES

    def step(interior):
        q2 = q_ref[...] * (NA_HD ** -0.5 * LOG2_E)
        lane = lax.broadcasted_iota(jnp.int32, q2.shape, 1)
        ks = [k[...].astype(BF16) for k in (k0, k1, k2, k3)]
        vs = [v[...].astype(BF16) for v in (v0, v1, v2, v3)]
        kc = kc_ref[...].astype(BF16)
        vc = vc_ref[...].astype(BF16)
        nheads = q2.shape[1] // NA_HD

        def scores(a):
            sel = jnp.logical_and(lane >= a * NA_HD, lane < (a + 1) * NA_HD)
            qa = jnp.where(sel, q2, 0.0).astype(BF16)
            return [_dot_nt(qa, k) for k in ks], _dot_nt(qa, kc)

        ahead = 2
        pending = [scores(a) for a in range(min(ahead, nheads))]
        out = None
        for a in range(nheads):
            if a + ahead < nheads:
                pending.append(scores(a + ahead))
            s_raw, s_ctx = pending[a]
            if interior:
                o = _na_head_interior(s_raw, s_ctx, vs, vc, tab_ref, mask_ref, a, dvals[1], pad_blocks)
            else:
                dcls = jnp.where(g == 0, dvals[0], dvals[2])
                o = _na_head_edge(s_raw, s_ctx, vs, vc, tab_ref, mask_ref, a, dcls, dvals[0], pad_blocks)
            out = o if out is None else jnp.where(lane >= a * NA_HD, o, out)
        o_ref[...] = out.astype(o_ref.dtype)

    @pl.when(is_edge)
    def _():
        step(False)

    @pl.when(jnp.logical_not(is_edge))
    def _():
        step(True)


def _na_latent(u, bias, head_base, nbatch, seq, lat_rows, off_na):
    tab, mask, dvals = bias
    rows = seq // GRID_W
    groups = rows // NA_ROWS
    nq = NA_ROWS * GRID_W
    kblk = ROW_TILE
    nkb = NA_KEY_ROWS * GRID_W // kblk
    assert nkb == 4
    hp = NA_STEP_HEADS
    wid = hp * NA_HD
    qcol = off_na // wid
    hsteps = NA_HEADS // hp
    kcol, vcol = qcol + hsteps, qcol + 2 * hsteps
    assert off_na % wid == 0 and NA_HEADS % hp == 0 and head_base % hp == 0
    kb_per_batch = seq // kblk
    kb_per_grow = GRID_W * NA_ROWS // kblk
    lat_kb = lat_rows // kblk

    def kmap(col, m):
        def f(h, g, b):
            st = jnp.clip(g * kb_per_grow - (NA_KH // 2) * GRID_W // kblk, 0, kb_per_batch - nkb)
            return (b * kb_per_batch + st + m, col + h)
        return f

    def mask_map(h, g, b):
        return (jnp.where(g == 0, 0, jnp.where(g == groups - 1, 2, 1)), 0, 0)

    in_specs = [pl.BlockSpec((nq, wid), lambda h, g, b: (b * groups + g, qcol + h))]
    in_specs += [pl.BlockSpec((kblk, wid), kmap(kcol, m)) for m in range(nkb)]
    in_specs += [pl.BlockSpec((kblk, wid), kmap(vcol, m)) for m in range(nkb)]
    in_specs += [pl.BlockSpec((ROW_TILE, wid), lambda h, g, b: (lat_kb + b, kcol + h)),
                 pl.BlockSpec((ROW_TILE, wid), lambda h, g, b: (lat_kb + b, vcol + h)),
                 pl.BlockSpec((2, hp) + tab.shape[2:], lambda h, g, b: (0, head_base // hp + h, 0, 0, 0)),
                 pl.BlockSpec((None,) + mask.shape[1:], mask_map)]
    return pl.pallas_call(
        functools.partial(_na_kernel, groups=groups, dvals=dvals),
        grid=(hsteps, groups, nbatch),
        in_specs=in_specs,
        out_specs=pl.BlockSpec((nq, wid), lambda h, g, b: (b * groups + g, h)),
        out_shape=jax.ShapeDtypeStruct((lat_rows, NA_HEADS * NA_HD), BF16),
        compiler_params=_cparams(("arbitrary", "arbitrary", "arbitrary"), 48),
    )(*([u] * 11), tab, mask)


def _ctx_attn_kernel(q_ref, k_ref, v_ref, o_ref):
    q2 = q_ref[...] * (NA_HD ** -0.5 * LOG2_E)
    lane = lax.broadcasted_iota(jnp.int32, q2.shape, 1)
    k = k_ref[...].astype(BF16)
    v = v_ref[...].astype(BF16)
    outs = []
    for a in range(2):
        sel = (lane < NA_HD) if a == 0 else (lane >= NA_HD)
        qa = jnp.where(sel, q2, 0.0).astype(BF16)
        outs.append(_softmax_pv([_dot_nt(qa, k)], [v]))
    o_ref[...] = jnp.where(lane < NA_HD, outs[0], outs[1]).astype(o_ref.dtype)


def _ctx_attn(u, nbatch, ctx_len, lat_rows, off_na):
    assert ctx_len == ROW_TILE
    qcol = off_na // LANES
    heads2 = NA_HEADS * NA_HD // LANES
    base = lat_rows // ROW_TILE
    spec = lambda col: pl.BlockSpec((ROW_TILE, LANES), lambda b, h: (base + b, col + h))
    return pl.pallas_call(
        _ctx_attn_kernel,
        grid=(nbatch, heads2),
        in_specs=[spec(qcol), spec(qcol + heads2), spec(qcol + 2 * heads2)],
        out_specs=pl.BlockSpec((ROW_TILE, LANES), lambda b, h: (b, h)),
        out_shape=jax.ShapeDtypeStruct((nbatch * ctx_len, NA_HEADS * NA_HD), BF16),
        compiler_params=_cparams(("arbitrary", "arbitrary"), 16),
    )(u, u, u)


def _hg_level_map(rev):
    size = HG_BLOCK // 2
    t = np.arange(size)[:, None]
    s = np.arange(size)[None, :]
    x = t ^ s
    lvl = np.where(x > 0, np.frexp(np.maximum(x, 1))[1] - 1, -1)
    causal = (s < t) if not rev else (s > t)
    out = np.where(causal, lvl, -1)
    out = np.where(t == s, int(np.log2(size)), out)
    return out.astype(np.int32)


def _hg_tri(rev):
    t = np.arange(HG_BLOCK)[:, None]
    s = np.arange(HG_BLOCK)[None, :]
    return ((s <= t) if not rev else (s >= t)).astype(np.float32)


def _hg_anchor(b3, m, rev):
    nv = b3.shape[0]
    if m >= SUBLANES:
        w = m // SUBLANES
        b4 = b3.reshape(nv // (2 * w), 2 * w, SUBLANES, LANES)
        a = b4[:, w:w + 1, 0:1, :] if rev else b4[:, w - 1:w, SUBLANES - 1:SUBLANES, :]
        return jnp.broadcast_to(a, b4.shape).reshape(b3.shape)
    sub = lax.broadcasted_iota(jnp.int32, b3.shape, 1)
    out = None
    for g in range(SUBLANES // (2 * m)):
        idx = g * 2 * m + (m if rev else m - 1)
        a = jnp.broadcast_to(b3[:, idx:idx + 1, :], b3.shape)
        out = a if out is None else jnp.where(sub >= g * 2 * m, a, out)
    return out


def _neg_abs(x):
    bits = lax.bitcast_convert_type(x, jnp.uint32) | jnp.uint32(0x80000000)
    return lax.bitcast_convert_type(bits, F32)


def _hg_pick(q3, k3, m, rev):
    nv = q3.shape[0]
    if m >= SUBLANES:
        w = m // SUBLANES
        shape4 = (nv // (2 * w), 2 * w, SUBLANES, LANES)
        q4, k4 = q3.reshape(shape4), k3.reshape(shape4)
        lower, upper = (q4, k4) if rev else (k4, q4)
        return jnp.concatenate([lower[:, :w], upper[:, w:]], axis=1).reshape(q3.shape)
    upper_rows = (lax.broadcasted_iota(jnp.int32, q3.shape, 1) & m) != 0
    return jnp.where(upper_rows, k3 if rev else q3, q3 if rev else k3)


def _hg_gates(q, z, alog, clog, oml, tri):
    q = _silu(q)
    t = jnp.exp(-jnp.abs(z))
    lsig = jnp.minimum(z, 0.0) - jnp.log(1.0 + t)
    cc = clog + lsig
    logf = jnp.maximum(alog, cc) + jnp.log(1.0 + jnp.exp(-jnp.abs(alog - cc)))
    kk = oml * jnp.where(z >= 0.0, t, 1.0) / (1.0 + t)

    hi = logf.astype(BF16)
    r1 = logf - hi.astype(F32)
    mid = r1.astype(BF16)
    lo = (r1 - mid.astype(F32)).astype(BF16)
    b = (_dot(tri, hi) + _dot(tri, mid) + _dot(tri, lo)) * LOG2_E
    return q, kk, b


def _hg_mix(q, kk, b, v, lv, st, rev):
    n = q.shape[0]
    half = n // 2
    nlev = int(np.log2(n))
    shape3 = (n // SUBLANES, SUBLANES, LANES)
    b3, q3, k3 = b.reshape(shape3), q.reshape(shape3), kk.reshape(shape3)
    halves = (slice(0, half), slice(half, n))
    qb, kb = q.astype(BF16), kk.astype(BF16)
    acc = [jnp.where(lv == nlev - 1, _dot_nt(qb[hs], kb[hs]), 0.0) for hs in halves]
    for lev in range(nlev - 1):
        m = 1 << lev
        e = jnp.exp2(_neg_abs(b3 - _hg_anchor(b3, m, rev)))
        w = (_hg_pick(q3, k3, m, rev) * e).reshape(n, LANES).astype(BF16)
        acc = [jnp.where(lv == lev, _dot_nt(w[hs], w[hs]), a) for hs, a in zip(halves, acc)]
    first, second = (halves[1], halves[0]) if rev else halves
    e = jnp.exp2(_neg_abs(b - (b[half:half + 1, :] if rev else b[half - 1:half, :])))
    top = _dot_nt((q[second] * e[second]).astype(BF16), (kk[first] * e[first]).astype(BF16))

    vb = v.astype(BF16)
    a0, a1 = acc[0].astype(BF16), acc[1].astype(BF16)
    tb = top.astype(BF16)
    if rev:
        o_lo = _dot(jnp.concatenate([a0, tb], axis=1), vb)
        o_hi = _dot(a1, vb[halves[1]])
    else:
        o_lo = _dot(a0, vb[halves[0]])
        o_hi = _dot(jnp.concatenate([tb, a1], axis=1), vb)
    b_last = b[0:1, :] if rev else b[n - 1:n, :]
    qh = (q * jnp.exp2(b)).astype(BF16)
    o = jnp.concatenate([o_lo, o_hi], axis=0) + _dot_nt(qh, st.astype(BF16))
    kh = (kk * jnp.exp2(b_last - b)).astype(BF16)
    st_new = st * jnp.exp2(b_last) + _dot_tn(vb, kh)
    return o, st_new


def _hg_gates_ahead(q_ref, z_ref, al_ref, cl_ref, om_ref, tri):
    def gates(h):
        hs = slice(h * HG_DK, (h + 1) * HG_DK)
        return _hg_gates(q_ref[:, hs], z_ref[:, hs], al_ref[:, hs], cl_ref[:, hs], om_ref[:, hs], tri)

    nxt = gates(0)
    for h in range(HG_HEADS):
        cur = nxt
        if h + 1 < HG_HEADS:
            nxt = gates(h + 1)
        yield cur


def _hg_fwd_kernel(q_ref, v_ref, z_ref, al_ref, cl_ref, om_ref, tri_ref, lv_ref, o_ref, st_ref):
    @pl.when(pl.program_id(1) == 0)
    def _():
        st_ref[...] = jnp.zeros_like(st_ref)

    lv = lv_ref[...]
    gates = _hg_gates_ahead(q_ref, z_ref, al_ref, cl_ref, om_ref, tri_ref[...])
    for h in range(HG_HEADS):
        hs = slice(h * HG_DK, (h + 1) * HG_DK)
        o, st = _hg_mix(*next(gates), v_ref[:, hs], lv, st_ref[h], False)
        o_ref[:, hs] = o
        st_ref[h] = st


def _hg_bwd_kernel(q_ref, v_ref, z_ref, g_ref, of_ref, al_ref, cl_ref, om_ref, ng_ref, tri_ref, lv_ref,
                   o_ref, st_ref):
    @pl.when(pl.program_id(1) == 0)
    def _():
        st_ref[...] = jnp.zeros_like(st_ref)

    lv = lv_ref[...]
    gates = _hg_gates_ahead(q_ref, z_ref, al_ref, cl_ref, om_ref, tri_ref[...])
    for h in range(HG_HEADS):
        hs = slice(h * HG_DK, (h + 1) * HG_DK)
        o, st = _hg_mix(*next(gates), v_ref[:, hs], lv, st_ref[h], True)
        st_ref[h] = st
        t = of_ref[:, hs] + o
        y = t * lax.rsqrt(jnp.mean(t * t, axis=-1, keepdims=True) + EPS)
        o_ref[:, hs] = (y * ng_ref[:, hs] * _silu(g_ref[:, hs])).astype(o_ref.dtype)


def _hgrn(u, lb, norm_g, nbatch, seq, ctx_len, lat_rows, off_hg):
    assert ctx_len == HG_BLOCK
    n = u.shape[0]
    hd = HG_HEADS * HG_DK
    col = off_hg // hd
    per = seq // HG_BLOCK
    lat_blocks = lat_rows // HG_BLOCK
    lbf = lb.astype(F32)
    alog, clog, oml = jnp.log(lbf), jnp.log1p(-lbf), 1.0 - lbf

    def fmap(c):
        return lambda b, j: (jnp.where(j == 0, lat_blocks + b, b * per + j - 1), c)

    def bmap(c):
        return lambda b, j: (jnp.where(j == 0, lat_blocks + b, b * per + per - j), c)

    const = lambda shape: pl.BlockSpec(shape, lambda b, j: (0, 0))
    grid = (nbatch, per + 1)
    vec = lambda a: a.reshape(1, hd)
    o_f = pl.pallas_call(
        _hg_fwd_kernel,
        grid=grid,
        in_specs=[pl.BlockSpec((HG_BLOCK, hd), fmap(col)), pl.BlockSpec((HG_BLOCK, hd), fmap(col + 1)),
                  pl.BlockSpec((HG_BLOCK, hd), fmap(col + 2)),
                  const((1, hd)), const((1, hd)), const((1, hd)),
                  const((HG_BLOCK, HG_BLOCK)), const((HG_BLOCK // 2, HG_BLOCK // 2))],
        out_specs=pl.BlockSpec((HG_BLOCK, hd), fmap(0)),
        out_shape=jax.ShapeDtypeStruct((n, hd), F32),
        scratch_shapes=[pltpu.VMEM((HG_HEADS, HG_DK, HG_DK), F32)],
        compiler_params=_cparams(("arbitrary", "arbitrary"), 32),
    )(u, u, u, vec(alog[0]), vec(clog[0]), vec(oml[0]),
      jnp.asarray(_hg_tri(False), BF16), jnp.asarray(_hg_level_map(False)))
    return pl.pallas_call(
        _hg_bwd_kernel,
        grid=grid,
        in_specs=[pl.BlockSpec((HG_BLOCK, hd), bmap(col)), pl.BlockSpec((HG_BLOCK, hd), bmap(col + 1)),
                  pl.BlockSpec((HG_BLOCK, hd), bmap(col + 3)), pl.BlockSpec((HG_BLOCK, hd), bmap(col + 4)),
                  pl.BlockSpec((HG_BLOCK, hd), bmap(0)),
                  const((1, hd)), const((1, hd)), const((1, hd)), const((1, hd)),
                  const((HG_BLOCK, HG_BLOCK)), const((HG_BLOCK // 2, HG_BLOCK // 2))],
        out_specs=pl.BlockSpec((HG_BLOCK, hd), bmap(0)),
        out_shape=jax.ShapeDtypeStruct((n, hd), BF16),
        scratch_shapes=[pltpu.VMEM((HG_HEADS, HG_DK, HG_DK), F32)],
        compiler_params=_cparams(("arbitrary", "arbitrary"), 32),
    )(u, u, u, u, o_f, vec(alog[1]), vec(clog[1]), vec(oml[1]), vec(norm_g.astype(F32)),
      jnp.asarray(_hg_tri(True), BF16), jnp.asarray(_hg_level_map(True)))


def _out_kernel(x_ref, cv_ref, na_ref, hg_ref, w_ref, ga_ref, g2_ref, sh2_ref, s2_ref, wrh_ref, wrl_ref, br_ref,
                xo_ref, h_ref, rt_ref, *, tiles_per_batch, nbatch):
    r = jnp.minimum(pl.program_id(0) // tiles_per_batch, nbatch)
    c0 = cv_ref.shape[1]
    c1 = c0 + na_ref.shape[1]
    mix = (_dot(cv_ref[...], w_ref[0:c0, :]) + _dot(na_ref[...], w_ref[c0:c1, :])
           + _dot(hg_ref[...], w_ref[c1:, :]))
    xn = x_ref[...] + ga_ref[pl.ds(r, 1), :] * mix
    xo_ref[...] = xn
    h = _rms_mod(xn, g2_ref[...], s2_ref[pl.ds(r, 1), :], sh2_ref[pl.ds(r, 1), :])
    h_ref[...] = h.astype(h_ref.dtype)
    h_hi = h.astype(BF16)
    h_lo = (h - h_hi.astype(F32)).astype(BF16)
    logits = (_dot(h_hi, wrh_ref[...]) + (_dot(h_lo, wrh_ref[...]) + _dot(h_hi, wrl_ref[...]))
              + br_ref[...])
    rt_ref[...] = _route_rows(logits)


def _route_rows(lg):
    lane = lax.broadcasted_iota(jnp.int32, lg.shape, 1)
    big = jnp.int32(2 ** 30)
    low = jnp.float32(-3e38)

    def first_max(vals, mask):
        m = jnp.max(vals, axis=-1, keepdims=True)
        idx = jnp.min(jnp.where(jnp.logical_and(vals == m, mask), lane, big), axis=-1, keepdims=True)
        return m, idx

    gmask = lane < N_GROUPS
    gl = jnp.where(gmask, lg, low)
    gm, grp = first_max(gl, gmask)
    p_grp = 1.0 / jnp.sum(jnp.where(gmask, jnp.exp(gl - gm), 0.0), axis=-1, keepdims=True)
    lo = N_GROUPS + grp * EXP_PER_GROUP
    emask = jnp.logical_and(lane >= lo, lane < lo + EXP_PER_GROUP)
    el = jnp.where(emask, lg, low)
    m1, i1 = first_max(el, emask)
    emask2 = jnp.logical_and(emask, lane != i1)
    el2 = jnp.where(emask2, lg, low)
    m2, i2 = first_max(el2, emask2)
    t = jnp.exp(m2 - m1)
    w1 = p_grp / (1.0 + t)
    w2 = p_grp * t / (1.0 + t)
    e1 = (i1 - N_GROUPS).astype(F32)
    e2 = (i2 - N_GROUPS).astype(F32)
    return jnp.where(lane == 0, e1, jnp.where(lane == 1, e2, jnp.where(lane == 2, w1,
                     jnp.where(lane == 3, w2, 0.0))))


def _out_proj(x, conv, na, hg, w_bf16, l, mod, g_ffn, w_router, b_router, n_rows, nbatch, lat_rows):
    d = x.shape[1]
    w_router_hi = w_router.astype(BF16)
    w_router_lo = (w_router - w_router_hi.astype(F32)).astype(BF16)
    tm = _pick_tile(OUT_TM, lat_rows // nbatch, n_rows)
    kern = functools.partial(_out_kernel, tiles_per_batch=lat_rows // nbatch // tm, nbatch=nbatch)
    row = lambda w: pl.BlockSpec((tm, w), lambda i: (i, 0))
    const = lambda shape: pl.BlockSpec(shape, lambda i: (0, 0))
    modc = lambda c: pl.BlockSpec((None, SUBLANES, d), lambda i: (l, 0, c))
    return pl.pallas_call(
        kern,
        grid=(n_rows // tm,),
        in_specs=[row(d), row(conv.shape[1]), row(na.shape[1]), row(hg.shape[1]),
                  pl.BlockSpec((None, d, d), lambda i: (l, 0, 0)),
                  modc(2),
                  const((1, d)),
                  modc(3),
                  modc(4),
                  const((d, ROUTER_PAD)), const((d, ROUTER_PAD)), const((1, ROUTER_PAD))],
        out_specs=[row(d), row(d), row(ROUTER_PAD)],
        out_shape=[jax.ShapeDtypeStruct((n_rows, d), F32), jax.ShapeDtypeStruct((n_rows, d), BF16),
                   jax.ShapeDtypeStruct((n_rows, ROUTER_PAD), F32)],
        compiler_params=_cparams(("arbitrary",), 56),
    )(x, conv, na, hg, w_bf16, mod, g_ffn.reshape(1, d), mod, mod, w_router_hi, w_router_lo, b_router)


def _moe_kernel(be_ref, nu_ref, xs_ref, w1_ref, w3_ref, w2_ref, sw_ref, o_ref, w1b, w3b, w2b):
    i = pl.program_id(0)
    e = be_ref[i]
    prev = be_ref[jnp.maximum(i - 1, 0)]

    @pl.when(jnp.logical_or(i == 0, e != prev))
    def _():
        w1b[...] = w1_ref[...].astype(BF16)
        w3b[...] = w3_ref[...].astype(BF16)
        w2b[...] = w2_ref[...].astype(BF16)

    @pl.when(i < nu_ref[0])
    def _():
        x = xs_ref[...]
        a = (_silu(_dot(x, w1b[...])) * _dot(x, w3b[...])).astype(BF16)
        o_ref[...] = (_dot(a, w2b[...]) * sw_ref[...]).astype(o_ref.dtype)

    @pl.when(i >= nu_ref[0])
    def _():
        o_ref[...] = jnp.zeros_like(o_ref)


def _moe_experts(xs, slot_w, blk_e, nused, w1, w3, w2, l):
    p, d = xs.shape
    de = w1.shape[3]
    bm = MOE_BM
    grid_spec = pltpu.PrefetchScalarGridSpec(
        num_scalar_prefetch=2,
        grid=(p // bm,),
        in_specs=[pl.BlockSpec((bm, d), lambda i, be, nu: (i, 0)),
                  pl.BlockSpec((None, None, d, de), lambda i, be, nu: (l, be[i], 0, 0)),
                  pl.BlockSpec((None, None, d, de), lambda i, be, nu: (l, be[i], 0, 0)),
                  pl.BlockSpec((None, None, de, d), lambda i, be, nu: (l, be[i], 0, 0)),
                  pl.BlockSpec((bm, 1), lambda i, be, nu: (i, 0))],
        out_specs=pl.BlockSpec((bm, d), lambda i, be, nu: (i, 0)),
        scratch_shapes=[pltpu.VMEM((d, de), BF16), pltpu.VMEM((d, de), BF16), pltpu.VMEM((de, d), BF16)],
    )
    return pl.pallas_call(
        _moe_kernel,
        grid_spec=grid_spec,
        out_shape=jax.ShapeDtypeStruct((p, d), BF16),
        compiler_params=_cparams(("arbitrary",), 48),
    )(blk_e, nused, xs, w1, w3, w2, slot_w.reshape(p, 1))


def _rows(a, idx):
    return a.at[idx].get(mode="promise_in_bounds")


def _route_meta(route, n):
    i32 = jnp.int32
    eid = route[:, 0:TOP_K].astype(i32).reshape(-1)
    wt = route[:, TOP_K:2 * TOP_K].reshape(-1)
    a = n * TOP_K
    bm = MOE_BM
    nblk = -(-a // bm) + N_EXPERTS
    p = nblk * bm
    experts = jnp.arange(N_EXPERTS, dtype=i32)[None, :]
    ja = jnp.arange(a, dtype=i32)
    se, order, wsort = lax.sort((eid, ja, wt), num_keys=1, is_stable=True)
    cnt = jnp.sum((eid[:, None] == experts).astype(i32), axis=0)
    pcnt = (cnt + bm - 1) // bm * bm
    pend = jnp.cumsum(pcnt)
    pstart = pend - pcnt
    end = jnp.cumsum(cnt)
    start = end - cnt
    off = pstart - start
    d_off = off - jnp.concatenate([jnp.zeros((1,), i32), off[:-1]])
    dst_sorted = ja + jnp.sum(jnp.where(ja[:, None] >= start[None, :], d_off[None, :], 0), axis=1)
    _, pos = lax.sort((order, dst_sorted), num_keys=1)
    jp = jnp.arange(p, dtype=i32)
    in_or_after = jp[:, None] >= pstart[None, :]
    src = jp - jnp.sum(jnp.where(in_or_after, d_off[None, :], 0), axis=1)
    valid = src < jnp.sum(jnp.where(in_or_after, cnt[None, :], 0), axis=1)
    src = jnp.where(valid, src, jp % a)
    slot_tok = _rows(order, src) // TOP_K
    slot_w = jnp.where(valid, _rows(wsort, src), 0.0)
    jb = jnp.arange(nblk, dtype=i32) * bm
    blk_e = jnp.minimum(jnp.sum((jb[:, None] >= pend[None, :]).astype(i32), axis=1), N_EXPERTS - 1)
    nused = (pend[-1:] // bm).astype(i32)
    return slot_tok, slot_w, pos.reshape(n, TOP_K), blk_e, nused


def _combine_kernel(x_ref, y0_ref, y1_ref, ga_ref, gf_ref, o_ref, *, tiles_per_batch, nbatch, final):
    r = jnp.minimum(pl.program_id(0) // tiles_per_batch, nbatch)
    xn = x_ref[...] + ga_ref[pl.ds(r, 1), :] * (y0_ref[...].astype(F32) + y1_ref[...].astype(F32))
    if final:
        xn = xn * lax.rsqrt(jnp.mean(xn * xn, axis=-1, keepdims=True) + EPS) * gf_ref[...]
    o_ref[...] = xn


def _combine(x, y0, y1, mod, l, g_final, n_rows, nbatch, lat_rows, final):
    d = x.shape[1]
    tm = _pick_tile(OUT_TM, lat_rows // nbatch, n_rows)
    kern = functools.partial(_combine_kernel, tiles_per_batch=lat_rows // nbatch // tm, nbatch=nbatch,
                             final=final)
    row = pl.BlockSpec((tm, d), lambda i: (i, 0))
    return pl.pallas_call(
        kern,
        grid=(n_rows // tm,),
        in_specs=[row, row, row, pl.BlockSpec((None, SUBLANES, d), lambda i: (l, 0, 5)),
                  pl.BlockSpec((1, d), lambda i: (0, 0))],
        out_specs=row,
        out_shape=jax.ShapeDtypeStruct((n_rows, d), F32),
        compiler_params=_cparams(("arbitrary",), 40),
    )(x, y0, y1, mod, g_final.reshape(1, d))


def kernel(x, c, ctx, c_ctx, w_ada, b_ada, g_mix, g_ffn, w_in, conv_w, conv_b, conv_ln_g, conv_ln_b,
           na_rpb, hgrn_lb, hgrn_norm_g, w_out, w_router_group, b_router_group, w_router_expert,
           b_router_expert, w_exp_gate, w_exp_up, w_exp_down, g_final):
    nb, seq, d = x.shape
    ctx_len = ctx.shape[1]
    depth = w_ada.shape[0]
    lat_rows = nb * seq
    n_all = lat_rows + nb * ctx_len
    conv_ch = conv_w.shape[2]
    off_na = 2 * conv_ch
    off_hg = off_na + 3 * NA_HEADS * NA_HD
    rows = seq // GRID_W
    assert nb < SUBLANES and rows % NA_ROWS == 0 and rows >= NA_KEY_ROWS

    lbs = jnp.cumsum(jax.nn.softmax(hgrn_lb.astype(F32), axis=0), axis=0)
    lbs = lbs - lbs[:1]

    cond = jnp.concatenate([c, c_ctx[None, :], jnp.zeros((SUBLANES - nb - 1, d), F32)], axis=0)
    mod = _ada_mod(cond, w_ada, b_ada)

    xs = jnp.concatenate([x.reshape(lat_rows, d), ctx.reshape(nb * ctx_len, d)], axis=0)
    w_in_b = w_in.astype(BF16)
    w_out_b = w_out.astype(BF16)
    na_bias = _na_bias(na_rpb.reshape((depth * NA_HEADS,) + na_rpb.shape[2:]), rows)
    for l in range(depth):
        with_ctx = l < depth - 1
        n_act = n_all if with_ctx else lat_rows
        u = _norm_in(xs, g_mix[l], mod, w_in_b, l, nb, lat_rows)

        conv = _conv_module(u, conv_w[l], conv_b[l], conv_ln_g[l], conv_ln_b[l], n_act, lat_rows, seq)
        na = _na_latent(u, na_bias, l * NA_HEADS, nb, seq, lat_rows, off_na)
        if with_ctx:
            na = jnp.concatenate([na, _ctx_attn(u, nb, ctx_len, lat_rows, off_na)], axis=0)
        hg = _hgrn(u, lbs[l], hgrn_norm_g[l], nb, seq, ctx_len, lat_rows, off_hg)

        w_router = jnp.concatenate(
            [w_router_group[l], w_router_expert[l],
             jnp.zeros((d, ROUTER_PAD - N_GROUPS - N_EXPERTS), F32)], axis=1)
        b_router = jnp.concatenate(
            [b_router_group[l], b_router_expert[l],
             jnp.zeros((ROUTER_PAD - N_GROUPS - N_EXPERTS,), F32)]).reshape(1, ROUTER_PAD)
        x_mid, h, route = _out_proj(xs, conv, na, hg, w_out_b, l, mod, g_ffn[l],
                                    w_router, b_router, n_act, nb, lat_rows)

        slot_tok, slot_w, pos, blk_e, nused = _route_meta(route, n_act)
        ys = _moe_experts(_rows(h, slot_tok), slot_w, blk_e, nused, w_exp_gate, w_exp_up, w_exp_down, l)
        y0 = _rows(ys, pos[:, 0])
        y1 = _rows(ys, pos[:, 1])
        xs = _combine(x_mid, y0, y1, mod, l, g_final, n_act, nb, lat_rows, final=not with_ctx)
    return xs.reshape(nb, seq, d)
```

```python
import functools

import numpy as np
import jax
import jax.numpy as jnp
from jax import lax
from jax.experimental import pallas as pl
from jax.experimental.pallas import tpu as pltpu

F32 = jnp.float32
BF16 = jnp.bfloat16

EPS = 1e-6
NEG_INF = -1e30
LOG2_E = 1.4426950408889634

GRID_W = 64
CONV_K = 31
NA_HEADS = 16
NA_HD = 64
NA_KH = 8
NA_KW = 16
HG_HEADS = 4
HG_DK = 128
N_GROUPS = 4
EXP_PER_GROUP = 8
N_EXPERTS = N_GROUPS * EXP_PER_GROUP
TOP_K = 2

LANES = 128
SUBLANES = 8
VMEM_BYTES = 64 * 1024 * 1024

ROW_TILE = 256
IN_TM = 1024
IN_TN = 512
OUT_TM = 512
NA_STEP_HEADS = 4
NA_ROWS = 8
NA_KEY_ROWS = 16
HG_BLOCK = 256
MOE_BM = 512
ROUTER_PAD = LANES
HALO = 16


def _pick_tile(pref, *extents):
    t = pref
    while t > ROW_TILE and any(e % t for e in extents):
        t //= 2
    assert all(e % t == 0 for e in extents)
    return t


def _cparams(sem, vmem_mb):
    return pltpu.CompilerParams(dimension_semantics=sem, vmem_limit_bytes=vmem_mb * 1024 * 1024)


def _dot(a, b):
    return jnp.dot(a, b, preferred_element_type=F32)


def _dot_nt(a, b):
    return lax.dot_general(a, b, (((1,), (1,)), ((), ())), preferred_element_type=F32)


def _dot_tn(a, b):
    return lax.dot_general(a, b, (((0,), (0,)), ((), ())), preferred_element_type=F32)


def _sigmoid(x):
    return 1.0 / (1.0 + jnp.exp(-x))


def _silu(x):
    return x * _sigmoid(x)


def _split_bf16(x):
    hi = x.astype(BF16)
    return hi, (x - hi.astype(F32)).astype(BF16)


def _ada_kernel(c_ref, w_ref, b_ref, o_ref):
    s_hi, s_lo = _split_bf16(_silu(c_ref[...]))
    w_hi, w_lo = _split_bf16(w_ref[0])
    o_ref[0] = _dot(s_hi, w_hi) + (_dot(s_lo, w_hi) + _dot(s_hi, w_lo)) + b_ref[0]


def _ada_mod(cond, w_ada, b_ada):
    depth, d, n = w_ada.shape
    tn = 1024
    return pl.pallas_call(
        _ada_kernel,
        grid=(depth, n // tn),
        in_specs=[
            pl.BlockSpec((SUBLANES, d), lambda l, j: (0, 0)),
            pl.BlockSpec((1, d, tn), lambda l, j: (l, 0, j)),
            pl.BlockSpec((1, 1, tn), lambda l, j: (l, 0, j)),
        ],
        out_specs=pl.BlockSpec((1, SUBLANES, tn), lambda l, j: (l, 0, j)),
        out_shape=jax.ShapeDtypeStruct((depth, SUBLANES, n), F32),
        compiler_params=_cparams(("arbitrary", "arbitrary"), 40),
    )(cond, w_ada, b_ada.reshape(depth, 1, n))


def _rms_mod(x, g, scale, shift):
    y = x * lax.rsqrt(jnp.mean(x * x, axis=-1, keepdims=True) + EPS)
    return (y * g) * (1.0 + scale) + shift


def _norm_in_kernel(x_ref, g_ref, sh_ref, sc_ref, w_ref, o_ref, h_ref, *, tiles_per_batch, nbatch):
    i = pl.program_id(0)

    @pl.when(pl.program_id(1) == 0)
    def _():
        r = jnp.minimum(i // tiles_per_batch, nbatch)
        h = _rms_mod(x_ref[...], g_ref[...], sc_ref[pl.ds(r, 1), :], sh_ref[pl.ds(r, 1), :])
        h_ref[...] = h.astype(BF16)

    o_ref[...] = _dot(h_ref[...], w_ref[...])


def _norm_in(x, g, mod, w_bf16, l, nbatch, lat_rows):
    n, d = x.shape
    nout = w_bf16.shape[2]
    tm, tn = _pick_tile(IN_TM, lat_rows // nbatch, n), IN_TN
    kern = functools.partial(_norm_in_kernel, tiles_per_batch=lat_rows // nbatch // tm, nbatch=nbatch)
    return pl.pallas_call(
        kern,
        grid=(n // tm, nout // tn),
        in_specs=[
            pl.BlockSpec((tm, d), lambda i, j: (i, 0)),
            pl.BlockSpec((1, d), lambda i, j: (0, 0)),
            pl.BlockSpec((None, SUBLANES, d), lambda i, j: (l, 0, 0)),
            pl.BlockSpec((None, SUBLANES, d), lambda i, j: (l, 0, 1)),
            pl.BlockSpec((None, d, tn), lambda i, j: (l, 0, j)),
        ],
        out_specs=pl.BlockSpec((tm, tn), lambda i, j: (i, j)),
        out_shape=jax.ShapeDtypeStruct((n, nout), F32),
        scratch_shapes=[pltpu.VMEM((tm, d), BF16)],
        compiler_params=_cparams(("arbitrary", "arbitrary"), 48),
    )(x, g.reshape(1, d), mod, mod, w_bf16)


def _conv_kernel(ap_ref, gp_ref, a_ref, gt_ref, an_ref, gn_ref, w_ref, b_ref, lg_ref, lb_ref,
                 o_ref, buf_ref, acc_ref, *, lat_tiles, tiles_per_seq):
    i = pl.program_id(0)
    tc, ch = a_ref.shape
    is_lat = i < lat_tiles
    pos = i % tiles_per_seq
    first = jnp.logical_or(jnp.logical_not(is_lat), pos == 0)
    last = jnp.logical_or(jnp.logical_not(is_lat), pos == tiles_per_seq - 1)

    buf_ref[0:HALO] = jnp.where(first, 0.0, ap_ref[...] * _sigmoid(gp_ref[...]))
    buf_ref[HALO:HALO + tc] = a_ref[...] * _sigmoid(gt_ref[...])
    buf_ref[HALO + tc:2 * HALO + tc] = jnp.where(last, 0.0, an_ref[...] * _sigmoid(gn_ref[...]))

    rows = 64
    base = HALO - CONV_K // 2
    for c in range(ch // LANES):
        cs = slice(c * LANES, (c + 1) * LANES)
        for r in range(tc // rows):
            acc = None
            for res in range(SUBLANES):
                y = None
                for k in range(CONV_K):
                    if (base + k) % SUBLANES != res:
                        continue
                    lo = r * rows + (base + k) // SUBLANES * SUBLANES
                    term = w_ref[k:k + 1, cs] * buf_ref[lo:lo + rows + SUBLANES, cs]
                    y = term if y is None else y + term
                if y is not None:
                    y = y[res:res + rows]
                    acc = y if acc is None else acc + y
            acc_ref[r * rows:(r + 1) * rows, cs] = acc

    h = acc_ref[...] + b_ref[...]
    mu = jnp.mean(h, axis=-1, keepdims=True)
    var = jnp.mean(jnp.square(h - mu), axis=-1, keepdims=True)
    y = (h - mu) * lax.rsqrt(var + EPS) * lg_ref[...] + lb_ref[...]
    o_ref[...] = _silu(y).astype(o_ref.dtype)


def _conv_module(u, w_dw, b_dw, ln_g, ln_b, n_rows, lat_rows, seq):
    ch = w_dw.shape[1]
    tc = ROW_TILE
    per = tc // HALO
    nh = u.shape[0] // HALO
    kern = functools.partial(_conv_kernel, lat_tiles=lat_rows // tc, tiles_per_seq=seq // tc)
    prev_map = lambda c: (lambda i: (jnp.maximum(i * per - 1, 0), c))
    next_map = lambda c: (lambda i: (jnp.minimum((i + 1) * per, nh - 1), c))
    vec = lambda a: a.reshape(1, ch)
    return pl.pallas_call(
        kern,
        grid=(n_rows // tc,),
        in_specs=[
            pl.BlockSpec((HALO, ch), prev_map(0)),
            pl.BlockSpec((HALO, ch), prev_map(1)),
            pl.BlockSpec((tc, ch), lambda i: (i, 0)),
            pl.BlockSpec((tc, ch), lambda i: (i, 1)),
            pl.BlockSpec((HALO, ch), next_map(0)),
            pl.BlockSpec((HALO, ch), next_map(1)),
            pl.BlockSpec((CONV_K, ch), lambda i: (0, 0)),
            pl.BlockSpec((1, ch), lambda i: (0, 0)),
            pl.BlockSpec((1, ch), lambda i: (0, 0)),
            pl.BlockSpec((1, ch), lambda i: (0, 0)),
        ],
        out_specs=pl.BlockSpec((tc, ch), lambda i: (i, 0)),
        out_shape=jax.ShapeDtypeStruct((n_rows, ch), BF16),
        scratch_shapes=[pltpu.VMEM((tc + 2 * HALO, ch), F32), pltpu.VMEM((tc, ch), F32)],
        compiler_params=_cparams(("arbitrary",), 16),
    )(u, u, u, u, u, u, w_dw, vec(b_dw), vec(ln_g), vec(ln_b))


def _na_bias_tables(rows):
    groups = rows // NA_ROWS
    reps = [0, min(1, groups - 1), groups - 1]
    out = []
    for g in reps:
        start = int(np.clip(NA_ROWS * g - NA_KH // 2, 0, rows - NA_KEY_ROWS))
        per_row = []
        for i in range(NA_ROWS):
            r = NA_ROWS * g + i
            sr = int(np.clip(r - NA_KH // 2, 0, rows - NA_KH))
            per_row.append((sr - start, sr - r + NA_KH - 1))
        out.append(per_row)
    return out


def _na_bias(rpb, rows):
    nh = rpb.shape[0]
    ndr, ndc = 2 * NA_KH - 1, 2 * NA_KW - 1
    period = 2 * GRID_W - 1
    pad = GRID_W - NA_KW
    vp = jnp.pad(rpb.astype(F32), ((0, 0), (0, 0), (pad, period - ndc - pad)))
    hank = jnp.tile(vp, (1, 1, GRID_W + 1))[:, :, :GRID_W * (period + 1)]
    hank = hank.reshape(nh, ndr, GRID_W, period + 1)[..., :GRID_W]
    toe = hank[:, :, ::-1, :]
    c = np.arange(GRID_W)[:, None]
    j = np.arange(GRID_W)[None, :]
    ws = np.clip(c - NA_KW // 2, 0, GRID_W - NA_KW)
    col_ok = (j >= ws) & (j < ws + NA_KW)
    toe = jnp.where(col_ok[None, None], toe, NEG_INF)
    flat = jnp.transpose(toe, (0, 2, 1, 3)).reshape(nh, GRID_W, ndr * GRID_W)
    nk = NA_KEY_ROWS * GRID_W
    lpad = NA_ROWS * GRID_W
    total = 2 * nk

    def padded(shift):
        return jnp.pad(flat, ((0, 0), (0, 0), (lpad - shift, total - flat.shape[2] - lpad + shift)))

    tab = jnp.stack([padded(0), padded(GRID_W)])
    tab = tab.reshape(2, nh, GRID_W, total // LANES, LANES).transpose(0, 1, 3, 2, 4)
    tables = _na_bias_tables(rows)
    mask = np.full((len(tables), NA_ROWS, nk), NEG_INF, np.float32)
    dvals = []
    for cls, per_row in enumerate(tables):
        dvals.append(per_row[0][1] - per_row[0][0])
        for i, (off, lo) in enumerate(per_row):
            assert lo - off == dvals[-1] - i and -NA_ROWS <= lo - off < NA_ROWS
            mask[cls, i, off * GRID_W:(off + NA_KH) * GRID_W] = 0.0
    assert len({d % 2 for d in dvals}) == 1
    assert len(tables) < 3 or rows < 3 * NA_ROWS or all(off == i for i, (off, _) in enumerate(tables[1]))
    return tab * LOG2_E, jnp.asarray(mask * LOG2_E), tuple(dvals)


def _softmax_pv(s_parts, v_parts):
    m = functools.reduce(jnp.maximum, [jnp.max(s, axis=-1, keepdims=True) for s in s_parts])
    acc, l = None, None
    for s, v in zip(s_parts, v_parts):
        p = jnp.exp2(s - m)
        ps = jnp.sum(p, axis=-1, keepdims=True)
        pv = _dot(p.astype(BF16), v)
        l = ps if l is None else l + ps
        acc = pv if acc is None else acc + pv
    return acc / l


def _na_head_edge(s_raw, s_ctx, vs, vc, tab_ref, mask_ref, a, dcls, parity, pad_blocks):
    kw = s_raw[0].shape[1]
    bpk = kw // LANES
    s_parts = []
    for m, s in enumerate(s_raw):
        row_blocks = []
        for i in range(NA_ROWS):
            copy = (parity - i) % 2
            first = (dcls - i - copy + 2 * pad_blocks) // 2 + m * bpk
            bias = jnp.concatenate([tab_ref[copy, a, first + t] for t in range(bpk)], axis=1)
            bias = bias + mask_ref[i:i + 1, m * kw:(m + 1) * kw]
            row_blocks.append(s[i * GRID_W:(i + 1) * GRID_W, :] + bias)
        s_parts.append(jnp.concatenate(row_blocks, axis=0))
    return _softmax_pv(s_parts + [s_ctx], vs + [vc])


def _na_head_interior(s_raw, s_ctx, vs, vc, tab_ref, mask_ref, a, d0, pad_blocks):
    kw = s_raw[0].shape[1]
    bpk = kw // LANES
    nblk = len(s_raw) * bpk
    p_rows, pc_rows, l_rows = [], [], []
    for i in range(NA_ROWS):
        rs = slice(i * GRID_W, (i + 1) * GRID_W)
        lo_lane, hi_lane = i * GRID_W, (i + NA_KH) * GRID_W
        b_lo, b_hi = lo_lane // LANES, -(-hi_lane // LANES)
        copy = (d0 - i) % 2
        first = (d0 - i - copy + 2 * pad_blocks) // 2
        blocks = []
        for b in range(b_lo, b_hi):
            ls = slice((b % bpk) * LANES, (b % bpk + 1) * LANES)
            sb = s_raw[b // bpk][rs, ls] + tab_ref[copy, a, first + b]
            if b * LANES < lo_lane or (b + 1) * LANES > hi_lane:
                sb = sb + mask_ref[i:i + 1, b * LANES:(b + 1) * LANES]
            blocks.append(sb)
        sw = jnp.concatenate(blocks, axis=1)
        sc = s_ctx[rs]
        m = jnp.maximum(jnp.max(sw, axis=-1, keepdims=True), jnp.max(sc, axis=-1, keepdims=True))
        pw = jnp.exp2(sw - m)
        pc = jnp.exp2(sc - m)
        l_rows.append(jnp.sum(pw, axis=-1, keepdims=True) + jnp.sum(pc, axis=-1, keepdims=True))
        pieces = [jnp.zeros((GRID_W, b_lo * LANES), BF16), pw.astype(BF16),
                  jnp.zeros((GRID_W, (nblk - b_hi) * LANES), BF16)]
        p_rows.append(jnp.concatenate([x for x in pieces if x.shape[1]], axis=1))
        pc_rows.append(pc.astype(BF16))
    p = jnp.concatenate(p_rows, axis=0)
    acc = _dot(jnp.concatenate(pc_rows, axis=0), vc)
    for m, v in enumerate(vs):
        acc = acc + _dot(p[:, m * kw:(m + 1) * kw], v)
    return acc / jnp.concatenate(l_rows, axis=0)


def _na_kernel(q_ref, k0, k1, k2, k3, v0, v1, v2, v3, kc_ref, vc_ref, tab_ref, mask_ref, o_ref, *,
               groups, dvals):
    g = pl.program_id(1)
    is_edge = jnp.logical_or(g == 0, g == groups - 1)
    pad_blocks = NA_ROWS * GRID_W // LANES

    def step(interior):
        q2 = q_ref[...] * (NA_HD ** -0.5 * LOG2_E)
        lane = lax.broadcasted_iota(jnp.int32, q2.shape, 1)
        ks = [k[...].astype(BF16) for k in (k0, k1, k2, k3)]
        vs = [v[...].astype(BF16) for v in (v0, v1, v2, v3)]
        kc = kc_ref[...].astype(BF16)
        vc = vc_ref[...].astype(BF16)
        nheads = q2.shape[1] // NA_HD

        def scores(a):
            sel = jnp.logical_and(lane >= a * NA_HD, lane < (a + 1) * NA_HD)
            qa = jnp.where(sel, q2, 0.0).astype(BF16)
            return [_dot_nt(qa, k) for k in ks], _dot_nt(qa, kc)

        ahead = 2
        pending = [scores(a) for a in range(min(ahead, nheads))]
        out = None
        for a in range(nheads):
            if a + ahead < nheads:
                pending.append(scores(a + ahead))
            s_raw, s_ctx = pending[a]
            if interior:
                o = _na_head_interior(s_raw, s_ctx, vs, vc, tab_ref, mask_ref, a, dvals[1], pad_blocks)
            else:
                dcls = jnp.where(g == 0, dvals[0], dvals[2])
                o = _na_head_edge(s_raw, s_ctx, vs, vc, tab_ref, mask_ref, a, dcls, dvals[0], pad_blocks)
            out = o if out is None else jnp.where(lane >= a * NA_HD, o, out)
        o_ref[...] = out.astype(o_ref.dtype)

    @pl.when(is_edge)
    def _():
        step(False)

    @pl.when(jnp.logical_not(is_edge))
    def _():
        step(True)


def _na_latent(u, bias, head_base, nbatch, seq, lat_rows, off_na):
    tab, mask, dvals = bias
    rows = seq // GRID_W
    groups = rows // NA_ROWS
    nq = NA_ROWS * GRID_W
    kblk = ROW_TILE
    nkb = NA_KEY_ROWS * GRID_W // kblk
    assert nkb == 4
    hp = NA_STEP_HEADS
    wid = hp * NA_HD
    qcol = off_na // wid
    hsteps = NA_HEADS // hp
    kcol, vcol = qcol + hsteps, qcol + 2 * hsteps
    assert off_na % wid == 0 and NA_HEADS % hp == 0 and head_base % hp == 0
    kb_per_batch = seq // kblk
    kb_per_grow = GRID_W * NA_ROWS // kblk
    lat_kb = lat_rows // kblk

    def kmap(col, m):
        def f(h, g, b):
            st = jnp.clip(g * kb_per_grow - (NA_KH // 2) * GRID_W // kblk, 0, kb_per_batch - nkb)
            return (b * kb_per_batch + st + m, col + h)
        return f

    def mask_map(h, g, b):
        return (jnp.where(g == 0, 0, jnp.where(g == groups - 1, 2, 1)), 0, 0)

    in_specs = [pl.BlockSpec((nq, wid), lambda h, g, b: (b * groups + g, qcol + h))]
    in_specs += [pl.BlockSpec((kblk, wid), kmap(kcol, m)) for m in range(nkb)]
    in_specs += [pl.BlockSpec((kblk, wid), kmap(vcol, m)) for m in range(nkb)]
    in_specs += [pl.BlockSpec((ROW_TILE, wid), lambda h, g, b: (lat_kb + b, kcol + h)),
                 pl.BlockSpec((ROW_TILE, wid), lambda h, g, b: (lat_kb + b, vcol + h)),
                 pl.BlockSpec((2, hp) + tab.shape[2:], lambda h, g, b: (0, head_base // hp + h, 0, 0, 0)),
                 pl.BlockSpec((None,) + mask.shape[1:], mask_map)]
    return pl.pallas_call(
        functools.partial(_na_kernel, groups=groups, dvals=dvals),
        grid=(hsteps, groups, nbatch),
        in_specs=in_specs,
        out_specs=pl.BlockSpec((nq, wid), lambda h, g, b: (b * groups + g, h)),
        out_shape=jax.ShapeDtypeStruct((lat_rows, NA_HEADS * NA_HD), BF16),
        compiler_params=_cparams(("arbitrary", "arbitrary", "arbitrary"), 48),
    )(*([u] * 11), tab, mask)


def _ctx_attn_kernel(q_ref, k_ref, v_ref, o_ref):
    q2 = q_ref[...] * (NA_HD ** -0.5 * LOG2_E)
    lane = lax.broadcasted_iota(jnp.int32, q2.shape, 1)
    k = k_ref[...].astype(BF16)
    v = v_ref[...].astype(BF16)
    outs = []
    for a in range(2):
        sel = (lane < NA_HD) if a == 0 else (lane >= NA_HD)
        qa = jnp.where(sel, q2, 0.0).astype(BF16)
        outs.append(_softmax_pv([_dot_nt(qa, k)], [v]))
    o_ref[...] = jnp.where(lane < NA_HD, outs[0], outs[1]).astype(o_ref.dtype)


def _ctx_attn(u, nbatch, ctx_len, lat_rows, off_na):
    assert ctx_len == ROW_TILE
    qcol = off_na // LANES
    heads2 = NA_HEADS * NA_HD // LANES
    base = lat_rows // ROW_TILE
    spec = lambda col: pl.BlockSpec((ROW_TILE, LANES), lambda b, h: (base + b, col + h))
    return pl.pallas_call(
        _ctx_attn_kernel,
        grid=(nbatch, heads2),
        in_specs=[spec(qcol), spec(qcol + heads2), spec(qcol + 2 * heads2)],
        out_specs=pl.BlockSpec((ROW_TILE, LANES), lambda b, h: (b, h)),
        out_shape=jax.ShapeDtypeStruct((nbatch * ctx_len, NA_HEADS * NA_HD), BF16),
        compiler_params=_cparams(("arbitrary", "arbitrary"), 16),
    )(u, u, u)


def _hg_level_map(rev):
    size = HG_BLOCK // 2
    t = np.arange(size)[:, None]
    s = np.arange(size)[None, :]
    x = t ^ s
    lvl = np.where(x > 0, np.frexp(np.maximum(x, 1))[1] - 1, -1)
    causal = (s < t) if not rev else (s > t)
    out = np.where(causal, lvl, -1)
    out = np.where(t == s, int(np.log2(size)), out)
    return out.astype(np.int32)


def _hg_tri(rev):
    t = np.arange(HG_BLOCK)[:, None]
    s = np.arange(HG_BLOCK)[None, :]
    return ((s <= t) if not rev else (s >= t)).astype(np.float32)


def _hg_anchor(b3, m, rev):
    nv = b3.shape[0]
    if m >= SUBLANES:
        w = m // SUBLANES
        b4 = b3.reshape(nv // (2 * w), 2 * w, SUBLANES, LANES)
        a = b4[:, w:w + 1, 0:1, :] if rev else b4[:, w - 1:w, SUBLANES - 1:SUBLANES, :]
        return jnp.broadcast_to(a, b4.shape).reshape(b3.shape)
    sub = lax.broadcasted_iota(jnp.int32, b3.shape, 1)
    out = None
    for g in range(SUBLANES // (2 * m)):
        idx = g * 2 * m + (m if rev else m - 1)
        a = jnp.broadcast_to(b3[:, idx:idx + 1, :], b3.shape)
        out = a if out is None else jnp.where(sub >= g * 2 * m, a, out)
    return out


def _neg_abs(x):
    bits = lax.bitcast_convert_type(x, jnp.uint32) | jnp.uint32(0x80000000)
    return lax.bitcast_convert_type(bits, F32)


def _hg_pick(q3, k3, m, rev):
    nv = q3.shape[0]
    if m >= SUBLANES:
        w = m // SUBLANES
        shape4 = (nv // (2 * w), 2 * w, SUBLANES, LANES)
        q4, k4 = q3.reshape(shape4), k3.reshape(shape4)
        lower, upper = (q4, k4) if rev else (k4, q4)
        return jnp.concatenate([lower[:, :w], upper[:, w:]], axis=1).reshape(q3.shape)
    upper_rows = (lax.broadcasted_iota(jnp.int32, q3.shape, 1) & m) != 0
    return jnp.where(upper_rows, k3 if rev else q3, q3 if rev else k3)


def _hg_gates(q, z, alog, clog, oml, tri):
    q = _silu(q)
    t = jnp.exp(-jnp.abs(z))
    lsig = jnp.minimum(z, 0.0) - jnp.log(1.0 + t)
    cc = clog + lsig
    logf = jnp.maximum(alog, cc) + jnp.log(1.0 + jnp.exp(-jnp.abs(alog - cc)))
    kk = oml * jnp.where(z >= 0.0, t, 1.0) / (1.0 + t)

    hi = logf.astype(BF16)
    r1 = logf - hi.astype(F32)
    mid = r1.astype(BF16)
    lo = (r1 - mid.astype(F32)).astype(BF16)
    b = (_dot(tri, hi) + _dot(tri, mid) + _dot(tri, lo)) * LOG2_E
    return q, kk, b


def _hg_mix(q, kk, b, v, lv, st, rev):
    n = q.shape[0]
    half = n // 2
    nlev = int(np.log2(n))
    shape3 = (n // SUBLANES, SUBLANES, LANES)
    b3, q3, k3 = b.reshape(shape3), q.reshape(shape3), kk.reshape(shape3)
    halves = (slice(0, half), slice(half, n))
    qb, kb = q.astype(BF16), kk.astype(BF16)
    acc = [jnp.where(lv == nlev - 1, _dot_nt(qb[hs], kb[hs]), 0.0) for hs in halves]
    for lev in range(nlev - 1):
        m = 1 << lev
        e = jnp.exp2(_neg_abs(b3 - _hg_anchor(b3, m, rev)))
        w = (_hg_pick(q3, k3, m, rev) * e).reshape(n, LANES).astype(BF16)
        acc = [jnp.where(lv == lev, _dot_nt(w[hs], w[hs]), a) for hs, a in zip(halves, acc)]
    first, second = (halves[1], halves[0]) if rev else halves
    e = jnp.exp2(_neg_abs(b - (b[half:half + 1, :] if rev else b[half - 1:half, :])))
    top = _dot_nt((q[second] * e[second]).astype(BF16), (kk[first] * e[first]).astype(BF16))

    vb = v.astype(BF16)
    a0, a1 = acc[0].astype(BF16), acc[1].astype(BF16)
    tb = top.astype(BF16)
    if rev:
        o_lo = _dot(jnp.concatenate([a0, tb], axis=1), vb)
        o_hi = _dot(a1, vb[halves[1]])
    else:
        o_lo = _dot(a0, vb[halves[0]])
        o_hi = _dot(jnp.concatenate([tb, a1], axis=1), vb)
    b_last = b[0:1, :] if rev else b[n - 1:n, :]
    qh = (q * jnp.exp2(b)).astype(BF16)
    o = jnp.concatenate([o_lo, o_hi], axis=0) + _dot_nt(qh, st.astype(BF16))
    kh = (kk * jnp.exp2(b_last - b)).astype(BF16)
    st_new = st * jnp.exp2(b_last) + _dot_tn(vb, kh)
    return o, st_new


def _hg_gates_ahead(q_ref, z_ref, al_ref, cl_ref, om_ref, tri):
    def gates(h):
        hs = slice(h * HG_DK, (h + 1) * HG_DK)
        return _hg_gates(q_ref[:, hs], z_ref[:, hs], al_ref[:, hs], cl_ref[:, hs], om_ref[:, hs], tri)

    nxt = gates(0)
    for h in range(HG_HEADS):
        cur = nxt
        if h + 1 < HG_HEADS:
            nxt = gates(h + 1)
        yield cur


def _hg_fwd_kernel(q_ref, v_ref, z_ref, al_ref, cl_ref, om_ref, tri_ref, lv_ref, o_ref, st_ref):
    @pl.when(pl.program_id(1) == 0)
    def _():
        st_ref[...] = jnp.zeros_like(st_ref)

    lv = lv_ref[...]
    gates = _hg_gates_ahead(q_ref, z_ref, al_ref, cl_ref, om_ref, tri_ref[...])
    for h in range(HG_HEADS):
        hs = slice(h * HG_DK, (h + 1) * HG_DK)
        o, st = _hg_mix(*next(gates), v_ref[:, hs], lv, st_ref[h], False)
        o_ref[:, hs] = o
        st_ref[h] = st


def _hg_bwd_kernel(q_ref, v_ref, z_ref, g_ref, of_ref, al_ref, cl_ref, om_ref, ng_ref, tri_ref, lv_ref,
                   o_ref, st_ref):
    @pl.when(pl.program_id(1) == 0)
    def _():
        st_ref[...] = jnp.zeros_like(st_ref)

    lv = lv_ref[...]
    gates = _hg_gates_ahead(q_ref, z_ref, al_ref, cl_ref, om_ref, tri_ref[...])
    for h in range(HG_HEADS):
        hs = slice(h * HG_DK, (h + 1) * HG_DK)
        o, st = _hg_mix(*next(gates), v_ref[:, hs], lv, st_ref[h], True)
        st_ref[h] = st
        t = of_ref[:, hs] + o
        y = t * lax.rsqrt(jnp.mean(t * t, axis=-1, keepdims=True) + EPS)
        o_ref[:, hs] = (y * ng_ref[:, hs] * _silu(g_ref[:, hs])).astype(o_ref.dtype)


def _hgrn(u, lb, norm_g, nbatch, seq, ctx_len, lat_rows, off_hg):
    assert ctx_len == HG_BLOCK
    n = u.shape[0]
    hd = HG_HEADS * HG_DK
    col = off_hg // hd
    per = seq // HG_BLOCK
    lat_blocks = lat_rows // HG_BLOCK
    lbf = lb.astype(F32)
    alog, clog, oml = jnp.log(lbf), jnp.log1p(-lbf), 1.0 - lbf

    def fmap(c):
        return lambda b, j: (jnp.where(j == 0, lat_blocks + b, b * per + j - 1), c)

    def bmap(c):
        return lambda b, j: (jnp.where(j == 0, lat_blocks + b, b * per + per - j), c)

    const = lambda shape: pl.BlockSpec(shape, lambda b, j: (0, 0))
    grid = (nbatch, per + 1)
    vec = lambda a: a.reshape(1, hd)
    o_f = pl.pallas_call(
        _hg_fwd_kernel,
        grid=grid,
        in_specs=[pl.BlockSpec((HG_BLOCK, hd), fmap(col)), pl.BlockSpec((HG_BLOCK, hd), fmap(col + 1)),
                  pl.BlockSpec((HG_BLOCK, hd), fmap(col + 2)),
                  const((1, hd)), const((1, hd)), const((1, hd)),
                  const((HG_BLOCK, HG_BLOCK)), const((HG_BLOCK // 2, HG_BLOCK // 2))],
        out_specs=pl.BlockSpec((HG_BLOCK, hd), fmap(0)),
        out_shape=jax.ShapeDtypeStruct((n, hd), F32),
        scratch_shapes=[pltpu.VMEM((HG_HEADS, HG_DK, HG_DK), F32)],
        compiler_params=_cparams(("arbitrary", "arbitrary"), 32),
    )(u, u, u, vec(alog[0]), vec(clog[0]), vec(oml[0]),
      jnp.asarray(_hg_tri(False), BF16), jnp.asarray(_hg_level_map(False)))
    return pl.pallas_call(
        _hg_bwd_kernel,
        grid=grid,
        in_specs=[pl.BlockSpec((HG_BLOCK, hd), bmap(col)), pl.BlockSpec((HG_BLOCK, hd), bmap(col + 1)),
                  pl.BlockSpec((HG_BLOCK, hd), bmap(col + 3)), pl.BlockSpec((HG_BLOCK, hd), bmap(col + 4)),
                  pl.BlockSpec((HG_BLOCK, hd), bmap(0)),
                  const((1, hd)), const((1, hd)), const((1, hd)), const((1, hd)),
                  const((HG_BLOCK, HG_BLOCK)), const((HG_BLOCK // 2, HG_BLOCK // 2))],
        out_specs=pl.BlockSpec((HG_BLOCK, hd), bmap(0)),
        out_shape=jax.ShapeDtypeStruct((n, hd), BF16),
        scratch_shapes=[pltpu.VMEM((HG_HEADS, HG_DK, HG_DK), F32)],
        compiler_params=_cparams(("arbitrary", "arbitrary"), 32),
    )(u, u, u, u, o_f, vec(alog[1]), vec(clog[1]), vec(oml[1]), vec(norm_g.astype(F32)),
      jnp.asarray(_hg_tri(True), BF16), jnp.asarray(_hg_level_map(True)))


def _out_kernel(x_ref, cv_ref, na_ref, hg_ref, w_ref, ga_ref, g2_ref, sh2_ref, s2_ref, wrh_ref, wrl_ref, br_ref,
                xo_ref, h_ref, rt_ref, xn_ref, *, tiles_per_batch, nbatch, ntiles):
    i = pl.program_id(0)

    @pl.when(i == 0)
    def _():
        xn_ref[...] = jnp.zeros_like(xn_ref)

    rp = jnp.minimum(jnp.maximum(i - 1, 0) // tiles_per_batch, nbatch)
    h = _rms_mod(xn_ref[...], g2_ref[...], s2_ref[pl.ds(rp, 1), :], sh2_ref[pl.ds(rp, 1), :])

    r = jnp.minimum(jnp.minimum(i, ntiles - 1) // tiles_per_batch, nbatch)
    c0 = cv_ref.shape[1]
    c1 = c0 + na_ref.shape[1]
    mix = (_dot(cv_ref[...], w_ref[0:c0, :]) + _dot(na_ref[...], w_ref[c0:c1, :])
           + _dot(hg_ref[...], w_ref[c1:, :]))
    xn = x_ref[...] + ga_ref[pl.ds(r, 1), :] * mix
    xo_ref[...] = xn
    xn_ref[...] = xn

    h_ref[...] = h.astype(h_ref.dtype)
    h_hi = h.astype(BF16)
    h_lo = (h - h_hi.astype(F32)).astype(BF16)
    logits = (_dot(h_hi, wrh_ref[...]) + (_dot(h_lo, wrh_ref[...]) + _dot(h_hi, wrl_ref[...]))
              + br_ref[...])
    rt_ref[...] = _route_rows(logits)


def _route_rows(lg):
    lane = lax.broadcasted_iota(jnp.int32, lg.shape, 1)
    big = jnp.int32(2 ** 30)
    low = jnp.float32(-3e38)

    def first_max(vals, mask):
        m = jnp.max(vals, axis=-1, keepdims=True)
        idx = jnp.min(jnp.where(jnp.logical_and(vals == m, mask), lane, big), axis=-1, keepdims=True)
        return m, idx

    gmask = lane < N_GROUPS
    gl = jnp.where(gmask, lg, low)
    gm, grp = first_max(gl, gmask)
    p_grp = 1.0 / jnp.sum(jnp.where(gmask, jnp.exp(gl - gm), 0.0), axis=-1, keepdims=True)
    lo = N_GROUPS + grp * EXP_PER_GROUP
    emask = jnp.logical_and(lane >= lo, lane < lo + EXP_PER_GROUP)
    el = jnp.where(emask, lg, low)
    m1, i1 = first_max(el, emask)
    emask2 = jnp.logical_and(emask, lane != i1)
    el2 = jnp.where(emask2, lg, low)
    m2, i2 = first_max(el2, emask2)
    t = jnp.exp(m2 - m1)
    w1 = p_grp / (1.0 + t)
    w2 = p_grp * t / (1.0 + t)
    e1 = (i1 - N_GROUPS).astype(F32)
    e2 = (i2 - N_GROUPS).astype(F32)
    return jnp.where(lane == 0, e1, jnp.where(lane == 1, e2, jnp.where(lane == 2, w1,
                     jnp.where(lane == 3, w2, 0.0))))


def _out_proj(x, conv, na, hg, w_bf16, l, mod, g_ffn, w_router, b_router, n_rows, nbatch, lat_rows):
    d = x.shape[1]
    w_router_hi = w_router.astype(BF16)
    w_router_lo = (w_router - w_router_hi.astype(F32)).astype(BF16)
    tm = _pick_tile(OUT_TM, lat_rows // nbatch, n_rows)
    ntiles = n_rows // tm
    kern = functools.partial(_out_kernel, tiles_per_batch=lat_rows // nbatch // tm, nbatch=nbatch,
                             ntiles=ntiles)
    row = lambda w: pl.BlockSpec((tm, w), lambda i: (jnp.minimum(i, ntiles - 1), 0))
    lag = lambda w: pl.BlockSpec((tm, w), lambda i: (jnp.maximum(i - 1, 0), 0))
    const = lambda shape: pl.BlockSpec(shape, lambda i: (0, 0))
    modc = lambda c: pl.BlockSpec((None, SUBLANES, d), lambda i: (l, 0, c))
    return pl.pallas_call(
        kern,
        grid=(ntiles + 1,),
        in_specs=[row(d), row(conv.shape[1]), row(na.shape[1]), row(hg.shape[1]),
                  pl.BlockSpec((None, d, d), lambda i: (l, 0, 0)),
                  modc(2),
                  const((1, d)),
                  modc(3),
                  modc(4),
                  const((d, ROUTER_PAD)), const((d, ROUTER_PAD)), const((1, ROUTER_PAD))],
        out_specs=[row(d), lag(d), lag(ROUTER_PAD)],
        out_shape=[jax.ShapeDtypeStruct((n_rows, d), F32), jax.ShapeDtypeStruct((n_rows, d), BF16),
                   jax.ShapeDtypeStruct((n_rows, ROUTER_PAD), F32)],
        scratch_shapes=[pltpu.VMEM((tm, d), F32)],
        compiler_params=_cparams(("arbitrary",), 56),
    )(x, conv, na, hg, w_bf16, mod, g_ffn.reshape(1, d), mod, mod, w_router_hi, w_router_lo, b_router)


def _moe_kernel(be_ref, nu_ref, xs_ref, w1_ref, w3_ref, w2_ref, sw_ref, o_ref, w1b, w3b, w2b):
    i = pl.program_id(0)
    e = be_ref[i]
    prev = be_ref[jnp.maximum(i - 1, 0)]

    @pl.when(jnp.logical_or(i == 0, e != prev))
    def _():
        w1b[...] = w1_ref[...].astype(BF16)
        w3b[...] = w3_ref[...].astype(BF16)
        w2b[...] = w2_ref[...].astype(BF16)

    @pl.when(i < nu_ref[0])
    def _():
        x = xs_ref[...]
        a = (_silu(_dot(x, w1b[...])) * _dot(x, w3b[...])).astype(BF16)
        o_ref[...] = (_dot(a, w2b[...]) * sw_ref[...]).astype(o_ref.dtype)

    @pl.when(i >= nu_ref[0])
    def _():
        o_ref[...] = jnp.zeros_like(o_ref)


def _moe_experts(xs, slot_w, blk_e, nused, w1, w3, w2, l):
    p, d = xs.shape
    de = w1.shape[3]
    bm = MOE_BM
    grid_spec = pltpu.PrefetchScalarGridSpec(
        num_scalar_prefetch=2,
        grid=(p // bm,),
        in_specs=[pl.BlockSpec((bm, d), lambda i, be, nu: (i, 0)),
                  pl.BlockSpec((None, None, d, de), lambda i, be, nu: (l, be[i], 0, 0)),
                  pl.BlockSpec((None, None, d, de), lambda i, be, nu: (l, be[i], 0, 0)),
                  pl.BlockSpec((None, None, de, d), lambda i, be, nu: (l, be[i], 0, 0)),
                  pl.BlockSpec((bm, 1), lambda i, be, nu: (i, 0))],
        out_specs=pl.BlockSpec((bm, d), lambda i, be, nu: (i, 0)),
        scratch_shapes=[pltpu.VMEM((d, de), BF16), pltpu.VMEM((d, de), BF16), pltpu.VMEM((de, d), BF16)],
    )
    return pl.pallas_call(
        _moe_kernel,
        grid_spec=grid_spec,
        out_shape=jax.ShapeDtypeStruct((p, d), BF16),
        compiler_params=_cparams(("arbitrary",), 48),
    )(blk_e, nused, xs, w1, w3, w2, slot_w.reshape(p, 1))


def _rows(a, idx):
    return a.at[idx].get(mode="promise_in_bounds")


def _route_meta(route, n):
    i32 = jnp.int32
    eid = route[:, 0:TOP_K].astype(i32).reshape(-1)
    wt = route[:, TOP_K:2 * TOP_K].reshape(-1)
    a = n * TOP_K
    bm = MOE_BM
    nblk = -(-a // bm) + N_EXPERTS
    p = nblk * bm
    experts = jnp.arange(N_EXPERTS, dtype=i32)[None, :]
    ja = jnp.arange(a, dtype=i32)
    se, order, wsort = lax.sort((eid, ja, wt), num_keys=1, is_stable=True)
    cnt = jnp.sum((eid[:, None] == experts).astype(i32), axis=0)
    pcnt = (cnt + bm - 1) // bm * bm
    pend = jnp.cumsum(pcnt)
    pstart = pend - pcnt
    end = jnp.cumsum(cnt)
    start = end - cnt
    off = pstart - start
    d_off = off - jnp.concatenate([jnp.zeros((1,), i32), off[:-1]])
    dst_sorted = ja + jnp.sum(jnp.where(ja[:, None] >= start[None, :], d_off[None, :], 0), axis=1)
    _, pos = lax.sort((order, dst_sorted), num_keys=1)
    jp = jnp.arange(p, dtype=i32)
    in_or_after = jp[:, None] >= pstart[None, :]
    src = jp - jnp.sum(jnp.where(in_or_after, d_off[None, :], 0), axis=1)
    valid = src < jnp.sum(jnp.where(in_or_after, cnt[None, :], 0), axis=1)
    src = jnp.where(valid, src, jp % a)
    slot_tok = _rows(order, src) // TOP_K
    slot_w = jnp.where(valid, _rows(wsort, src), 0.0)
    jb = jnp.arange(nblk, dtype=i32) * bm
    blk_e = jnp.minimum(jnp.sum((jb[:, None] >= pend[None, :]).astype(i32), axis=1), N_EXPERTS - 1)
    nused = (pend[-1:] // bm).astype(i32)
    return slot_tok, slot_w, pos.reshape(n, TOP_K), blk_e, nused


def _combine_kernel(x_ref, y0_ref, y1_ref, ga_ref, gf_ref, o_ref, *, tiles_per_batch, nbatch, final):
    r = jnp.minimum(pl.program_id(0) // tiles_per_batch, nbatch)
    xn = x_ref[...] + ga_ref[pl.ds(r, 1), :] * (y0_ref[...].astype(F32) + y1_ref[...].astype(F32))
    if final:
        xn = xn * lax.rsqrt(jnp.mean(xn * xn, axis=-1, keepdims=True) + EPS) * gf_ref[...]
    o_ref[...] = xn


def _combine(x, y0, y1, mod, l, g_final, n_rows, nbatch, lat_rows, final):
    d = x.shape[1]
    tm = _pick_tile(OUT_TM, lat_rows // nbatch, n_rows)
    kern = functools.partial(_combine_kernel, tiles_per_batch=lat_rows // nbatch // tm, nbatch=nbatch,
                             final=final)
    row = pl.BlockSpec((tm, d), lambda i: (i, 0))
    return pl.pallas_call(
        kern,
        grid=(n_rows // tm,),
        in_specs=[row, row, row, pl.BlockSpec((None, SUBLANES, d), lambda i: (l, 0, 5)),
                  pl.BlockSpec((1, d), lambda i: (0, 0))],
        out_specs=row,
        out_shape=jax.ShapeDtypeStruct((n_rows, d), F32),
        compiler_params=_cparams(("arbitrary",), 40),
    )(x, y0, y1, mod, g_final.reshape(1, d))


def kernel(x, c, ctx, c_ctx, w_ada, b_ada, g_mix, g_ffn, w_in, conv_w, conv_b, conv_ln_g, conv_ln_b,
           na_rpb, hgrn_lb, hgrn_norm_g, w_out, w_router_group, b_router_group, w_router_expert,
           b_router_expert, w_exp_gate, w_exp_up, w_exp_down, g_final):
    nb, seq, d = x.shape
    ctx_len = ctx.shape[1]
    depth = w_ada.shape[0]
    lat_rows = nb * seq
    n_all = lat_rows + nb * ctx_len
    conv_ch = conv_w.shape[2]
    off_na = 2 * conv_ch
    off_hg = off_na + 3 * NA_HEADS * NA_HD
    rows = seq // GRID_W
    assert nb < SUBLANES and rows % NA_ROWS == 0 and rows >= NA_KEY_ROWS

    lbs = jnp.cumsum(jax.nn.softmax(hgrn_lb.astype(F32), axis=0), axis=0)
    lbs = lbs - lbs[:1]

    cond = jnp.concatenate([c, c_ctx[None, :], jnp.zeros((SUBLANES - nb - 1, d), F32)], axis=0)
    mod = _ada_mod(cond, w_ada, b_ada)

    xs = jnp.concatenate([x.reshape(lat_rows, d), ctx.reshape(nb * ctx_len, d)], axis=0)
    w_in_b = w_in.astype(BF16)
    w_out_b = w_out.astype(BF16)
    na_bias = _na_bias(na_rpb.reshape((depth * NA_HEADS,) + na_rpb.shape[2:]), rows)
    for l in range(depth):
        with_ctx = l < depth - 1
        n_act = n_all if with_ctx else lat_rows
        u = _norm_in(xs, g_mix[l], mod, w_in_b, l, nb, lat_rows)

        conv = _conv_module(u, conv_w[l], conv_b[l], conv_ln_g[l], conv_ln_b[l], n_act, lat_rows, seq)
        na = _na_latent(u, na_bias, l * NA_HEADS, nb, seq, lat_rows, off_na)
        if with_ctx:
            na = jnp.concatenate([na, _ctx_attn(u, nb, ctx_len, lat_rows, off_na)], axis=0)
        hg = _hgrn(u, lbs[l], hgrn_norm_g[l], nb, seq, ctx_len, lat_rows, off_hg)

        w_router = jnp.concatenate(
            [w_router_group[l], w_router_expert[l],
             jnp.zeros((d, ROUTER_PAD - N_GROUPS - N_EXPERTS), F32)], axis=1)
        b_router = jnp.concatenate(
            [b_router_group[l], b_router_expert[l],
             jnp.zeros((ROUTER_PAD - N_GROUPS - N_EXPERTS,), F32)]).reshape(1, ROUTER_PAD)
        x_mid, h, route = _out_proj(xs, conv, na, hg, w_out_b, l, mod, g_ffn[l],
                                    w_router, b_router, n_act, nb, lat_rows)

        slot_tok, slot_w, pos, blk_e, nused = _route_meta(route, n_act)
        ys = _moe_experts(_rows(h, slot_tok), slot_w, blk_e, nused, w_exp_gate, w_exp_up, w_exp_down, l)
        y0 = _rows(ys, pos[:, 0])
        y1 = _rows(ys, pos[:, 1])
        xs = _combine(x_mid, y0, y1, mod, l, g_final, n_act, nb, lat_rows, final=not with_ctx)
    return xs.reshape(nb, seq, d)
```

```python
import functools

import numpy as np
import jax
import jax.numpy as jnp
from jax import lax
from jax.experimental import pallas as pl
from jax.experimental.pallas import tpu as pltpu

F32 = jnp.float32
BF16 = jnp.bfloat16

EPS = 1e-6
NEG_INF = -1e30
LOG2_E = 1.4426950408889634

GRID_W = 64
CONV_K = 31
NA_HEADS = 16
NA_HD = 64
NA_KH = 8
NA_KW = 16
HG_HEADS = 4
HG_DK = 128
N_GROUPS = 4
EXP_PER_GROUP = 8
N_EXPERTS = N_GROUPS * EXP_PER_GROUP
TOP_K = 2

LANES = 128
SUBLANES = 8
VMEM_BYTES = 64 * 1024 * 1024

ROW_TILE = 256
IN_TM = 1024
IN_TN = 512
OUT_TM = 512
NA_STEP_HEADS = 4
NA_ROWS = 8
NA_KEY_ROWS = 16
HG_BLOCK = 256
MOE_BM = 512
ROUTER_PAD = LANES
HALO = 16


def _pick_tile(pref, *extents):
    t = pref
    while t > ROW_TILE and any(e % t for e in extents):
        t //= 2
    assert all(e % t == 0 for e in extents)
    return t


def _cparams(sem, vmem_mb):
    return pltpu.CompilerParams(dimension_semantics=sem, vmem_limit_bytes=vmem_mb * 1024 * 1024)


def _dot(a, b):
    return jnp.dot(a, b, preferred_element_type=F32)


def _dot_nt(a, b):
    return lax.dot_general(a, b, (((1,), (1,)), ((), ())), preferred_element_type=F32)


def _dot_tn(a, b):
    return lax.dot_general(a, b, (((0,), (0,)), ((), ())), preferred_element_type=F32)


def _sigmoid(x):
    return 1.0 / (1.0 + jnp.exp(-x))


def _silu(x):
    return x * _sigmoid(x)


def _split_bf16(x):
    hi = x.astype(BF16)
    return hi, (x - hi.astype(F32)).astype(BF16)


def _ada_kernel(c_ref, w_ref, b_ref, o_ref):
    s_hi, s_lo = _split_bf16(_silu(c_ref[...]))
    w_hi, w_lo = _split_bf16(w_ref[0])
    o_ref[0] = _dot(s_hi, w_hi) + (_dot(s_lo, w_hi) + _dot(s_hi, w_lo)) + b_ref[0]


def _ada_mod(cond, w_ada, b_ada):
    depth, d, n = w_ada.shape
    tn = 1024
    return pl.pallas_call(
        _ada_kernel,
        grid=(depth, n // tn),
        in_specs=[
            pl.BlockSpec((SUBLANES, d), lambda l, j: (0, 0)),
            pl.BlockSpec((1, d, tn), lambda l, j: (l, 0, j)),
            pl.BlockSpec((1, 1, tn), lambda l, j: (l, 0, j)),
        ],
        out_specs=pl.BlockSpec((1, SUBLANES, tn), lambda l, j: (l, 0, j)),
        out_shape=jax.ShapeDtypeStruct((depth, SUBLANES, n), F32),
        compiler_params=_cparams(("arbitrary", "arbitrary"), 40),
    )(cond, w_ada, b_ada.reshape(depth, 1, n))


def _rms_mod(x, g, scale, shift):
    y = x * lax.rsqrt(jnp.mean(x * x, axis=-1, keepdims=True) + EPS)
    return (y * g) * (1.0 + scale) + shift


def _norm_in_kernel(x_ref, g_ref, sh_ref, sc_ref, w_ref, o_ref, h_ref, *, tiles_per_batch, nbatch):
    i = pl.program_id(0)

    @pl.when(pl.program_id(1) == 0)
    def _():
        r = jnp.minimum(i // tiles_per_batch, nbatch)
        h = _rms_mod(x_ref[...], g_ref[...], sc_ref[pl.ds(r, 1), :], sh_ref[pl.ds(r, 1), :])
        h_ref[...] = h.astype(BF16)

    o_ref[...] = _dot(h_ref[...], w_ref[...].astype(BF16))


def _norm_in(x, g, mod, w_bf16, l, nbatch, lat_rows):
    n, d = x.shape
    nout = w_bf16.shape[2]
    tm, tn = _pick_tile(IN_TM, lat_rows // nbatch, n), IN_TN
    kern = functools.partial(_norm_in_kernel, tiles_per_batch=lat_rows // nbatch // tm, nbatch=nbatch)
    return pl.pallas_call(
        kern,
        grid=(n // tm, nout // tn),
        in_specs=[
            pl.BlockSpec((tm, d), lambda i, j: (i, 0)),
            pl.BlockSpec((1, d), lambda i, j: (0, 0)),
            pl.BlockSpec((None, SUBLANES, d), lambda i, j: (l, 0, 0)),
            pl.BlockSpec((None, SUBLANES, d), lambda i, j: (l, 0, 1)),
            pl.BlockSpec((None, d, tn), lambda i, j: (l, 0, j)),
        ],
        out_specs=pl.BlockSpec((tm, tn), lambda i, j: (i, j)),
        out_shape=jax.ShapeDtypeStruct((n, nout), F32),
        scratch_shapes=[pltpu.VMEM((tm, d), BF16)],
        compiler_params=_cparams(("arbitrary", "arbitrary"), 48),
    )(x, g.reshape(1, d), mod, mod, w_bf16)


def _conv_kernel(ap_ref, gp_ref, a_ref, gt_ref, an_ref, gn_ref, w_ref, b_ref, lg_ref, lb_ref,
                 o_ref, buf_ref, acc_ref, *, lat_tiles, tiles_per_seq):
    i = pl.program_id(0)
    tc, ch = a_ref.shape
    is_lat = i < lat_tiles
    pos = i % tiles_per_seq
    first = jnp.logical_or(jnp.logical_not(is_lat), pos == 0)
    last = jnp.logical_or(jnp.logical_not(is_lat), pos == tiles_per_seq - 1)

    buf_ref[0:HALO] = jnp.where(first, 0.0, ap_ref[...] * _sigmoid(gp_ref[...]))
    buf_ref[HALO:HALO + tc] = a_ref[...] * _sigmoid(gt_ref[...])
    buf_ref[HALO + tc:2 * HALO + tc] = jnp.where(last, 0.0, an_ref[...] * _sigmoid(gn_ref[...]))

    rows = 64
    base = HALO - CONV_K // 2
    for c in range(ch // LANES):
        cs = slice(c * LANES, (c + 1) * LANES)
        for r in range(tc // rows):
            acc = None
            for res in range(SUBLANES):
                y = None
                for k in range(CONV_K):
                    if (base + k) % SUBLANES != res:
                        continue
                    lo = r * rows + (base + k) // SUBLANES * SUBLANES
                    term = w_ref[k:k + 1, cs] * buf_ref[lo:lo + rows + SUBLANES, cs]
                    y = term if y is None else y + term
                if y is not None:
                    y = y[res:res + rows]
                    acc = y if acc is None else acc + y
            acc_ref[r * rows:(r + 1) * rows, cs] = acc

    h = acc_ref[...] + b_ref[...]
    mu = jnp.mean(h, axis=-1, keepdims=True)
    var = jnp.mean(jnp.square(h - mu), axis=-1, keepdims=True)
    y = (h - mu) * lax.rsqrt(var + EPS) * lg_ref[...] + lb_ref[...]
    o_ref[...] = _silu(y).astype(o_ref.dtype)


def _conv_module(u, w_dw, b_dw, ln_g, ln_b, n_rows, lat_rows, seq):
    ch = w_dw.shape[1]
    tc = ROW_TILE
    per = tc // HALO
    nh = u.shape[0] // HALO
    kern = functools.partial(_conv_kernel, lat_tiles=lat_rows // tc, tiles_per_seq=seq // tc)
    prev_map = lambda c: (lambda i: (jnp.maximum(i * per - 1, 0), c))
    next_map = lambda c: (lambda i: (jnp.minimum((i + 1) * per, nh - 1), c))
    vec = lambda a: a.reshape(1, ch)
    return pl.pallas_call(
        kern,
        grid=(n_rows // tc,),
        in_specs=[
            pl.BlockSpec((HALO, ch), prev_map(0)),
            pl.BlockSpec((HALO, ch), prev_map(1)),
            pl.BlockSpec((tc, ch), lambda i: (i, 0)),
            pl.BlockSpec((tc, ch), lambda i: (i, 1)),
            pl.BlockSpec((HALO, ch), next_map(0)),
            pl.BlockSpec((HALO, ch), next_map(1)),
            pl.BlockSpec((CONV_K, ch), lambda i: (0, 0)),
            pl.BlockSpec((1, ch), lambda i: (0, 0)),
            pl.BlockSpec((1, ch), lambda i: (0, 0)),
            pl.BlockSpec((1, ch), lambda i: (0, 0)),
        ],
        out_specs=pl.BlockSpec((tc, ch), lambda i: (i, 0)),
        out_shape=jax.ShapeDtypeStruct((n_rows, ch), BF16),
        scratch_shapes=[pltpu.VMEM((tc + 2 * HALO, ch), F32), pltpu.VMEM((tc, ch), F32)],
        compiler_params=_cparams(("arbitrary",), 16),
    )(u, u, u, u, u, u, w_dw, vec(b_dw), vec(ln_g), vec(ln_b))


def _na_bias_tables(rows):
    groups = rows // NA_ROWS
    reps = [0, min(1, groups - 1), groups - 1]
    out = []
    for g in reps:
        start = int(np.clip(NA_ROWS * g - NA_KH // 2, 0, rows - NA_KEY_ROWS))
        per_row = []
        for i in range(NA_ROWS):
            r = NA_ROWS * g + i
            sr = int(np.clip(r - NA_KH // 2, 0, rows - NA_KH))
            per_row.append((sr - start, sr - r + NA_KH - 1))
        out.append(per_row)
    return out


def _na_bias(rpb, rows):
    nh = rpb.shape[0]
    ndr, ndc = 2 * NA_KH - 1, 2 * NA_KW - 1
    period = 2 * GRID_W - 1
    pad = GRID_W - NA_KW
    vp = jnp.pad(rpb.astype(F32), ((0, 0), (0, 0), (pad, period - ndc - pad)))
    hank = jnp.tile(vp, (1, 1, GRID_W + 1))[:, :, :GRID_W * (period + 1)]
    hank = hank.reshape(nh, ndr, GRID_W, period + 1)[..., :GRID_W]
    toe = hank[:, :, ::-1, :]
    c = np.arange(GRID_W)[:, None]
    j = np.arange(GRID_W)[None, :]
    ws = np.clip(c - NA_KW // 2, 0, GRID_W - NA_KW)
    col_ok = (j >= ws) & (j < ws + NA_KW)
    toe = jnp.where(col_ok[None, None], toe, NEG_INF)
    flat = jnp.transpose(toe, (0, 2, 1, 3)).reshape(nh, GRID_W, ndr * GRID_W)
    nk = NA_KEY_ROWS * GRID_W
    lpad = NA_ROWS * GRID_W
    total = 2 * nk

    def padded(shift):
        return jnp.pad(flat, ((0, 0), (0, 0), (lpad - shift, total - flat.shape[2] - lpad + shift)))

    tab = jnp.stack([padded(0), padded(GRID_W)])
    tab = tab.reshape(2, nh, GRID_W, total // LANES, LANES).transpose(0, 1, 3, 2, 4)
    tables = _na_bias_tables(rows)
    mask = np.full((len(tables), NA_ROWS, nk), NEG_INF, np.float32)
    dvals = []
    for cls, per_row in enumerate(tables):
        dvals.append(per_row[0][1] - per_row[0][0])
        for i, (off, lo) in enumerate(per_row):
            assert lo - off == dvals[-1] - i and -NA_ROWS <= lo - off < NA_ROWS
            mask[cls, i, off * GRID_W:(off + NA_KH) * GRID_W] = 0.0
    assert len({d % 2 for d in dvals}) == 1
    assert len(tables) < 3 or rows < 3 * NA_ROWS or all(off == i for i, (off, _) in enumerate(tables[1]))
    return tab * LOG2_E, jnp.asarray(mask * LOG2_E), tuple(dvals)


def _softmax_pv(s_parts, v_parts):
    m = functools.reduce(jnp.maximum, [jnp.max(s, axis=-1, keepdims=True) for s in s_parts])
    acc, l = None, None
    for s, v in zip(s_parts, v_parts):
        p = jnp.exp2(s - m)
        ps = jnp.sum(p, axis=-1, keepdims=True)
        pv = _dot(p.astype(BF16), v)
        l = ps if l is None else l + ps
        acc = pv if acc is None else acc + pv
    return acc / l


def _na_head_edge(s_raw, s_ctx, vs, vc, tab_ref, mask_ref, a, dcls, parity, pad_blocks):
    kw = s_raw[0].shape[1]
    bpk = kw // LANES
    s_parts = []
    for m, s in enumerate(s_raw):
        row_blocks = []
        for i in range(NA_ROWS):
            copy = (parity - i) % 2
            first = (dcls - i - copy + 2 * pad_blocks) // 2 + m * bpk
            bias = jnp.concatenate([tab_ref[copy, a, first + t] for t in range(bpk)], axis=1)
            bias = bias + mask_ref[i:i + 1, m * kw:(m + 1) * kw]
            row_blocks.append(s[i * GRID_W:(i + 1) * GRID_W, :] + bias)
        s_parts.append(jnp.concatenate(row_blocks, axis=0))
    return _softmax_pv(s_parts + [s_ctx], vs + [vc])


def _na_head_interior(s_raw, s_ctx, vs, vc, tab_ref, mask_ref, a, d0, pad_blocks):
    kw = s_raw[0].shape[1]
    bpk = kw // LANES
    nblk = len(s_raw) * bpk
    p_rows, pc_rows, l_rows = [], [], []
    for i in range(NA_ROWS):
        rs = slice(i * GRID_W, (i + 1) * GRID_W)
        lo_lane, hi_lane = i * GRID_W, (i + NA_KH) * GRID_W
        b_lo, b_hi = lo_lane // LANES, -(-hi_lane // LANES)
        copy = (d0 - i) % 2
        first = (d0 - i - copy + 2 * pad_blocks) // 2
        blocks = []
        for b in range(b_lo, b_hi):
            ls = slice((b % bpk) * LANES, (b % bpk + 1) * LANES)
            sb = s_raw[b // bpk][rs, ls] + tab_ref[copy, a, first + b]
            if b * LANES < lo_lane or (b + 1) * LANES > hi_lane:
                sb = sb + mask_ref[i:i + 1, b * LANES:(b + 1) * LANES]
            blocks.append(sb)
        sw = jnp.concatenate(blocks, axis=1)
        sc = s_ctx[rs]
        m = jnp.maximum(jnp.max(sw, axis=-1, keepdims=True), jnp.max(sc, axis=-1, keepdims=True))
        pw = jnp.exp2(sw - m)
        pc = jnp.exp2(sc - m)
        l_rows.append(jnp.sum(pw, axis=-1, keepdims=True) + jnp.sum(pc, axis=-1, keepdims=True))
        pieces = [jnp.zeros((GRID_W, b_lo * LANES), BF16), pw.astype(BF16),
                  jnp.zeros((GRID_W, (nblk - b_hi) * LANES), BF16)]
        p_rows.append(jnp.concatenate([x for x in pieces if x.shape[1]], axis=1))
        pc_rows.append(pc.astype(BF16))
    p = jnp.concatenate(p_rows, axis=0)
    acc = _dot(jnp.concatenate(pc_rows, axis=0), vc)
    for m, v in enumerate(vs):
        acc = acc + _dot(p[:, m * kw:(m + 1) * kw], v)
    return acc / jnp.concatenate(l_rows, axis=0)


def _na_kernel(q_ref, k0, k1, k2, k3, v0, v1, v2, v3, kc_ref, vc_ref, tab_ref, mask_ref, o_ref, *,
               groups, dvals):
    g = pl.program_id(1)
    is_edge = jnp.logical_or(g == 0, g == groups - 1)
    pad_blocks = NA_ROWS * GRID_W // LANES

    def step(interior):
        q2 = q_ref[...] * (NA_HD ** -0.5 * LOG2_E)
        lane = lax.broadcasted_iota(jnp.int32, q2.shape, 1)
        ks = [k[...].astype(BF16) for k in (k0, k1, k2, k3)]
        vs = [v[...].astype(BF16) for v in (v0, v1, v2, v3)]
        kc = kc_ref[...].astype(BF16)
        vc = vc_ref[...].astype(BF16)
        nheads = q2.shape[1] // NA_HD

        def scores(a):
            sel = jnp.logical_and(lane >= a * NA_HD, lane < (a + 1) * NA_HD)
            qa = jnp.where(sel, q2, 0.0).astype(BF16)
            return [_dot_nt(qa, k) for k in ks], _dot_nt(qa, kc)

        ahead = 2
        pending = [scores(a) for a in range(min(ahead, nheads))]
        out = None
        for a in range(nheads):
            if a + ahead < nheads:
                pending.append(scores(a + ahead))
            s_raw, s_ctx = pending[a]
            if interior:
                o = _na_head_interior(s_raw, s_ctx, vs, vc, tab_ref, mask_ref, a, dvals[1], pad_blocks)
            else:
                dcls = jnp.where(g == 0, dvals[0], dvals[2])
                o = _na_head_edge(s_raw, s_ctx, vs, vc, tab_ref, mask_ref, a, dcls, dvals[0], pad_blocks)
            out = o if out is None else jnp.where(lane >= a * NA_HD, o, out)
        o_ref[...] = out.astype(o_ref.dtype)

    @pl.when(is_edge)
    def _():
        step(False)

    @pl.when(jnp.logical_not(is_edge))
    def _():
        step(True)


def _na_latent(u, bias, head_base, nbatch, seq, lat_rows, off_na):
    tab, mask, dvals = bias
    rows = seq // GRID_W
    groups = rows // NA_ROWS
    nq = NA_ROWS * GRID_W
    kblk = ROW_TILE
    nkb = NA_KEY_ROWS * GRID_W // kblk
    assert nkb == 4
    hp = NA_STEP_HEADS
    wid = hp * NA_HD
    qcol = off_na // wid
    hsteps = NA_HEADS // hp
    kcol, vcol = qcol + hsteps, qcol + 2 * hsteps
    assert off_na % wid == 0 and NA_HEADS % hp == 0 and head_base % hp == 0
    kb_per_batch = seq // kblk
    kb_per_grow = GRID_W * NA_ROWS // kblk
    lat_kb = lat_rows // kblk

    def kmap(col, m):
        def f(h, g, b):
            st = jnp.clip(g * kb_per_grow - (NA_KH // 2) * GRID_W // kblk, 0, kb_per_batch - nkb)
            return (b * kb_per_batch + st + m, col + h)
        return f

    def mask_map(h, g, b):
        return (jnp.where(g == 0, 0, jnp.where(g == groups - 1, 2, 1)), 0, 0)

    in_specs = [pl.BlockSpec((nq, wid), lambda h, g, b: (b * groups + g, qcol + h))]
    in_specs += [pl.BlockSpec((kblk, wid), kmap(kcol, m)) for m in range(nkb)]
    in_specs += [pl.BlockSpec((kblk, wid), kmap(vcol, m)) for m in range(nkb)]
    in_specs += [pl.BlockSpec((ROW_TILE, wid), lambda h, g, b: (lat_kb + b, kcol + h)),
                 pl.BlockSpec((ROW_TILE, wid), lambda h, g, b: (lat_kb + b, vcol + h)),
                 pl.BlockSpec((2, hp) + tab.shape[2:], lambda h, g, b: (0, head_base // hp + h, 0, 0, 0)),
                 pl.BlockSpec((None,) + mask.shape[1:], mask_map)]
    return pl.pallas_call(
        functools.partial(_na_kernel, groups=groups, dvals=dvals),
        grid=(hsteps, groups, nbatch),
        in_specs=in_specs,
        out_specs=pl.BlockSpec((nq, wid), lambda h, g, b: (b * groups + g, h)),
        out_shape=jax.ShapeDtypeStruct((lat_rows, NA_HEADS * NA_HD), BF16),
        compiler_params=_cparams(("arbitrary", "arbitrary", "arbitrary"), 48),
    )(*([u] * 11), tab, mask)


def _ctx_attn_kernel(q_ref, k_ref, v_ref, o_ref):
    q2 = q_ref[...] * (NA_HD ** -0.5 * LOG2_E)
    lane = lax.broadcasted_iota(jnp.int32, q2.shape, 1)
    k = k_ref[...].astype(BF16)
    v = v_ref[...].astype(BF16)
    outs = []
    for a in range(2):
        sel = (lane < NA_HD) if a == 0 else (lane >= NA_HD)
        qa = jnp.where(sel, q2, 0.0).astype(BF16)
        outs.append(_softmax_pv([_dot_nt(qa, k)], [v]))
    o_ref[...] = jnp.where(lane < NA_HD, outs[0], outs[1]).astype(o_ref.dtype)


def _ctx_attn(u, nbatch, ctx_len, lat_rows, off_na):
    assert ctx_len == ROW_TILE
    qcol = off_na // LANES
    heads2 = NA_HEADS * NA_HD // LANES
    base = lat_rows // ROW_TILE
    spec = lambda col: pl.BlockSpec((ROW_TILE, LANES), lambda b, h: (base + b, col + h))
    return pl.pallas_call(
        _ctx_attn_kernel,
        grid=(nbatch, heads2),
        in_specs=[spec(qcol), spec(qcol + heads2), spec(qcol + 2 * heads2)],
        out_specs=pl.BlockSpec((ROW_TILE, LANES), lambda b, h: (b, h)),
        out_shape=jax.ShapeDtypeStruct((nbatch * ctx_len, NA_HEADS * NA_HD), BF16),
        compiler_params=_cparams(("arbitrary", "arbitrary"), 16),
    )(u, u, u)


def _hg_level_map(rev):
    size = HG_BLOCK // 2
    t = np.arange(size)[:, None]
    s = np.arange(size)[None, :]
    x = t ^ s
    lvl = np.where(x > 0, np.frexp(np.maximum(x, 1))[1] - 1, -1)
    causal = (s < t) if not rev else (s > t)
    out = np.where(causal, lvl, -1)
    out = np.where(t == s, int(np.log2(size)), out)
    return out.astype(np.int32)


def _hg_tri(rev):
    t = np.arange(HG_BLOCK)[:, None]
    s = np.arange(HG_BLOCK)[None, :]
    return ((s <= t) if not rev else (s >= t)).astype(np.float32)


def _hg_anchor(b3, m, rev):
    nv = b3.shape[0]
    if m >= SUBLANES:
        w = m // SUBLANES
        b4 = b3.reshape(nv // (2 * w), 2 * w, SUBLANES, LANES)
        a = b4[:, w:w + 1, 0:1, :] if rev else b4[:, w - 1:w, SUBLANES - 1:SUBLANES, :]
        return jnp.broadcast_to(a, b4.shape).reshape(b3.shape)
    sub = lax.broadcasted_iota(jnp.int32, b3.shape, 1)
    out = None
    for g in range(SUBLANES // (2 * m)):
        idx = g * 2 * m + (m if rev else m - 1)
        a = jnp.broadcast_to(b3[:, idx:idx + 1, :], b3.shape)
        out = a if out is None else jnp.where(sub >= g * 2 * m, a, out)
    return out


def _neg_abs(x):
    bits = lax.bitcast_convert_type(x, jnp.uint32) | jnp.uint32(0x80000000)
    return lax.bitcast_convert_type(bits, F32)


def _hg_pick(q3, k3, m, rev):
    nv = q3.shape[0]
    if m >= SUBLANES:
        w = m // SUBLANES
        shape4 = (nv // (2 * w), 2 * w, SUBLANES, LANES)
        q4, k4 = q3.reshape(shape4), k3.reshape(shape4)
        lower, upper = (q4, k4) if rev else (k4, q4)
        return jnp.concatenate([lower[:, :w], upper[:, w:]], axis=1).reshape(q3.shape)
    upper_rows = (lax.broadcasted_iota(jnp.int32, q3.shape, 1) & m) != 0
    return jnp.where(upper_rows, k3 if rev else q3, q3 if rev else k3)


def _hg_gates(q, z, alog, clog, oml, tri):
    q = _silu(q)
    t = jnp.exp(-jnp.abs(z))
    lsig = jnp.minimum(z, 0.0) - jnp.log(1.0 + t)
    cc = clog + lsig
    logf = jnp.maximum(alog, cc) + jnp.log(1.0 + jnp.exp(-jnp.abs(alog - cc)))
    kk = oml * jnp.where(z >= 0.0, t, 1.0) / (1.0 + t)

    hi = logf.astype(BF16)
    r1 = logf - hi.astype(F32)
    mid = r1.astype(BF16)
    lo = (r1 - mid.astype(F32)).astype(BF16)
    b = (_dot(tri, hi) + _dot(tri, mid) + _dot(tri, lo)) * LOG2_E
    return q, kk, b


def _hg_mix(q, kk, b, v, lv, st, rev):
    n = q.shape[0]
    half = n // 2
    nlev = int(np.log2(n))
    shape3 = (n // SUBLANES, SUBLANES, LANES)
    b3, q3, k3 = b.reshape(shape3), q.reshape(shape3), kk.reshape(shape3)
    halves = (slice(0, half), slice(half, n))
    qb, kb = q.astype(BF16), kk.astype(BF16)
    acc = [jnp.where(lv == nlev - 1, _dot_nt(qb[hs], kb[hs]), 0.0) for hs in halves]
    for lev in range(nlev - 1):
        m = 1 << lev
        e = jnp.exp2(_neg_abs(b3 - _hg_anchor(b3, m, rev)))
        w = (_hg_pick(q3, k3, m, rev) * e).reshape(n, LANES).astype(BF16)
        acc = [jnp.where(lv == lev, _dot_nt(w[hs], w[hs]), a) for hs, a in zip(halves, acc)]
    first, second = (halves[1], halves[0]) if rev else halves
    e = jnp.exp2(_neg_abs(b - (b[half:half + 1, :] if rev else b[half - 1:half, :])))
    top = _dot_nt((q[second] * e[second]).astype(BF16), (kk[first] * e[first]).astype(BF16))

    vb = v.astype(BF16)
    a0, a1 = acc[0].astype(BF16), acc[1].astype(BF16)
    tb = top.astype(BF16)
    if rev:
        o_lo = _dot(jnp.concatenate([a0, tb], axis=1), vb)
        o_hi = _dot(a1, vb[halves[1]])
    else:
        o_lo = _dot(a0, vb[halves[0]])
        o_hi = _dot(jnp.concatenate([tb, a1], axis=1), vb)
    b_last = b[0:1, :] if rev else b[n - 1:n, :]
    qh = (q * jnp.exp2(b)).astype(BF16)
    o = jnp.concatenate([o_lo, o_hi], axis=0) + _dot_nt(qh, st.astype(BF16))
    kh = (kk * jnp.exp2(b_last - b)).astype(BF16)
    st_new = st * jnp.exp2(b_last) + _dot_tn(vb, kh)
    return o, st_new


def _hg_gates_ahead(q_ref, z_ref, al_ref, cl_ref, om_ref, tri):
    def gates(h):
        hs = slice(h * HG_DK, (h + 1) * HG_DK)
        return _hg_gates(q_ref[:, hs], z_ref[:, hs], al_ref[:, hs], cl_ref[:, hs], om_ref[:, hs], tri)

    nxt = gates(0)
    for h in range(HG_HEADS):
        cur = nxt
        if h + 1 < HG_HEADS:
            nxt = gates(h + 1)
        yield cur


def _hg_fwd_kernel(q_ref, v_ref, z_ref, al_ref, cl_ref, om_ref, tri_ref, lv_ref, o_ref, st_ref):
    @pl.when(pl.program_id(1) == 0)
    def _():
        st_ref[...] = jnp.zeros_like(st_ref)

    lv = lv_ref[...]
    gates = _hg_gates_ahead(q_ref, z_ref, al_ref, cl_ref, om_ref, tri_ref[...])
    for h in range(HG_HEADS):
        hs = slice(h * HG_DK, (h + 1) * HG_DK)
        o, st = _hg_mix(*next(gates), v_ref[:, hs], lv, st_ref[h], False)
        o_ref[:, hs] = o
        st_ref[h] = st


def _hg_bwd_kernel(q_ref, v_ref, z_ref, g_ref, of_ref, al_ref, cl_ref, om_ref, ng_ref, tri_ref, lv_ref,
                   o_ref, st_ref):
    @pl.when(pl.program_id(1) == 0)
    def _():
        st_ref[...] = jnp.zeros_like(st_ref)

    lv = lv_ref[...]
    gates = _hg_gates_ahead(q_ref, z_ref, al_ref, cl_ref, om_ref, tri_ref[...])
    for h in range(HG_HEADS):
        hs = slice(h * HG_DK, (h + 1) * HG_DK)
        o, st = _hg_mix(*next(gates), v_ref[:, hs], lv, st_ref[h], True)
        st_ref[h] = st
        t = of_ref[:, hs] + o
        y = t * lax.rsqrt(jnp.mean(t * t, axis=-1, keepdims=True) + EPS)
        o_ref[:, hs] = (y * ng_ref[:, hs] * _silu(g_ref[:, hs])).astype(o_ref.dtype)


def _hgrn(u, lb, norm_g, nbatch, seq, ctx_len, lat_rows, off_hg):
    assert ctx_len == HG_BLOCK
    n = u.shape[0]
    hd = HG_HEADS * HG_DK
    col = off_hg // hd
    per = seq // HG_BLOCK
    lat_blocks = lat_rows // HG_BLOCK
    lbf = lb.astype(F32)
    alog, clog, oml = jnp.log(lbf), jnp.log1p(-lbf), 1.0 - lbf

    def fmap(c):
        return lambda b, j: (jnp.where(j == 0, lat_blocks + b, b * per + j - 1), c)

    def bmap(c):
        return lambda b, j: (jnp.where(j == 0, lat_blocks + b, b * per + per - j), c)

    const = lambda shape: pl.BlockSpec(shape, lambda b, j: (0, 0))
    grid = (nbatch, per + 1)
    vec = lambda a: a.reshape(1, hd)
    o_f = pl.pallas_call(
        _hg_fwd_kernel,
        grid=grid,
        in_specs=[pl.BlockSpec((HG_BLOCK, hd), fmap(col)), pl.BlockSpec((HG_BLOCK, hd), fmap(col + 1)),
                  pl.BlockSpec((HG_BLOCK, hd), fmap(col + 2)),
                  const((1, hd)), const((1, hd)), const((1, hd)),
                  const((HG_BLOCK, HG_BLOCK)), const((HG_BLOCK // 2, HG_BLOCK // 2))],
        out_specs=pl.BlockSpec((HG_BLOCK, hd), fmap(0)),
        out_shape=jax.ShapeDtypeStruct((n, hd), F32),
        scratch_shapes=[pltpu.VMEM((HG_HEADS, HG_DK, HG_DK), F32)],
        compiler_params=_cparams(("arbitrary", "arbitrary"), 32),
    )(u, u, u, vec(alog[0]), vec(clog[0]), vec(oml[0]),
      jnp.asarray(_hg_tri(False), BF16), jnp.asarray(_hg_level_map(False)))
    return pl.pallas_call(
        _hg_bwd_kernel,
        grid=grid,
        in_specs=[pl.BlockSpec((HG_BLOCK, hd), bmap(col)), pl.BlockSpec((HG_BLOCK, hd), bmap(col + 1)),
                  pl.BlockSpec((HG_BLOCK, hd), bmap(col + 3)), pl.BlockSpec((HG_BLOCK, hd), bmap(col + 4)),
                  pl.BlockSpec((HG_BLOCK, hd), bmap(0)),
                  const((1, hd)), const((1, hd)), const((1, hd)), const((1, hd)),
                  const((HG_BLOCK, HG_BLOCK)), const((HG_BLOCK // 2, HG_BLOCK // 2))],
        out_specs=pl.BlockSpec((HG_BLOCK, hd), bmap(0)),
        out_shape=jax.ShapeDtypeStruct((n, hd), BF16),
        scratch_shapes=[pltpu.VMEM((HG_HEADS, HG_DK, HG_DK), F32)],
        compiler_params=_cparams(("arbitrary", "arbitrary"), 32),
    )(u, u, u, u, o_f, vec(alog[1]), vec(clog[1]), vec(oml[1]), vec(norm_g.astype(F32)),
      jnp.asarray(_hg_tri(True), BF16), jnp.asarray(_hg_level_map(True)))


def _out_kernel(x_ref, cv_ref, na_ref, hg_ref, w_ref, ga_ref, g2_ref, sh2_ref, s2_ref, wrh_ref, wrl_ref, br_ref,
                xo_ref, h_ref, rt_ref, xn_ref, *, tiles_per_batch, nbatch, ntiles):
    i = pl.program_id(0)

    @pl.when(i == 0)
    def _():
        xn_ref[...] = jnp.zeros_like(xn_ref)

    c0 = cv_ref.shape[1]
    c1 = c0 + na_ref.shape[1]
    mix = (_dot(cv_ref[...], w_ref[0:c0, :]) + _dot(na_ref[...], w_ref[c0:c1, :])
           + _dot(hg_ref[...], w_ref[c1:, :]))

    rp = jnp.minimum(jnp.maximum(i - 1, 0) // tiles_per_batch, nbatch)
    h = _rms_mod(xn_ref[...], g2_ref[...], s2_ref[pl.ds(rp, 1), :], sh2_ref[pl.ds(rp, 1), :])
    h_ref[...] = h.astype(h_ref.dtype)
    h_hi, h_lo = _split_bf16(h)
    logits = (_dot(h_hi, wrh_ref[...]) + (_dot(h_lo, wrh_ref[...]) + _dot(h_hi, wrl_ref[...]))
              + br_ref[...])
    rt_ref[...] = _route_rows(logits)

    r = jnp.minimum(jnp.minimum(i, ntiles - 1) // tiles_per_batch, nbatch)
    xn = x_ref[...] + ga_ref[pl.ds(r, 1), :] * mix
    xo_ref[...] = xn
    xn_ref[...] = xn


def _route_rows(lg):
    lane = lax.broadcasted_iota(jnp.int32, lg.shape, 1)
    big = jnp.int32(2 ** 30)
    low = jnp.float32(-3e38)

    def first_max(vals, mask):
        m = jnp.max(vals, axis=-1, keepdims=True)
        idx = jnp.min(jnp.where(jnp.logical_and(vals == m, mask), lane, big), axis=-1, keepdims=True)
        return m, idx

    gmask = lane < N_GROUPS
    gl = jnp.where(gmask, lg, low)
    gm, grp = first_max(gl, gmask)
    p_grp = 1.0 / jnp.sum(jnp.where(gmask, jnp.exp(gl - gm), 0.0), axis=-1, keepdims=True)
    lo = N_GROUPS + grp * EXP_PER_GROUP
    emask = jnp.logical_and(lane >= lo, lane < lo + EXP_PER_GROUP)
    el = jnp.where(emask, lg, low)
    m1, i1 = first_max(el, emask)
    emask2 = jnp.logical_and(emask, lane != i1)
    el2 = jnp.where(emask2, lg, low)
    m2, i2 = first_max(el2, emask2)
    t = jnp.exp(m2 - m1)
    w1 = p_grp / (1.0 + t)
    w2 = p_grp * t / (1.0 + t)
    e1 = (i1 - N_GROUPS).astype(F32)
    e2 = (i2 - N_GROUPS).astype(F32)
    return jnp.where(lane == 0, e1, jnp.where(lane == 1, e2, jnp.where(lane == 2, w1,
                     jnp.where(lane == 3, w2, 0.0))))


def _out_proj(x, conv, na, hg, w_bf16, l, mod, g_ffn, w_router, b_router, n_rows, nbatch, lat_rows):
    d = x.shape[1]
    w_router_hi = w_router.astype(BF16)
    w_router_lo = (w_router - w_router_hi.astype(F32)).astype(BF16)
    tm = _pick_tile(OUT_TM, lat_rows // nbatch, n_rows)
    ntiles = n_rows // tm
    kern = functools.partial(_out_kernel, tiles_per_batch=lat_rows // nbatch // tm, nbatch=nbatch,
                             ntiles=ntiles)
    row = lambda w: pl.BlockSpec((tm, w), lambda i: (jnp.minimum(i, ntiles - 1), 0))
    lag = lambda w: pl.BlockSpec((tm, w), lambda i: (jnp.maximum(i - 1, 0), 0))
    const = lambda shape: pl.BlockSpec(shape, lambda i: (0, 0))
    modc = lambda c: pl.BlockSpec((None, SUBLANES, d), lambda i: (l, 0, c))
    return pl.pallas_call(
        kern,
        grid=(ntiles + 1,),
        in_specs=[row(d), row(conv.shape[1]), row(na.shape[1]), row(hg.shape[1]),
                  pl.BlockSpec((None, d, d), lambda i: (l, 0, 0)),
                  modc(2),
                  const((1, d)),
                  modc(3),
                  modc(4),
                  const((d, ROUTER_PAD)), const((d, ROUTER_PAD)), const((1, ROUTER_PAD))],
        out_specs=[row(d), lag(d), lag(ROUTER_PAD)],
        out_shape=[jax.ShapeDtypeStruct((n_rows, d), F32), jax.ShapeDtypeStruct((n_rows, d), BF16),
                   jax.ShapeDtypeStruct((n_rows, ROUTER_PAD), F32)],
        scratch_shapes=[pltpu.VMEM((tm, d), F32)],
        compiler_params=_cparams(("arbitrary",), 56),
    )(x, conv, na, hg, w_bf16, mod, g_ffn.reshape(1, d), mod, mod, w_router_hi, w_router_lo, b_router)


def _moe_kernel(be_ref, nu_ref, xs_ref, w1_ref, w3_ref, w2_ref, sw_ref, o_ref, w1b, w3b, w2b):
    i = pl.program_id(0)
    e = be_ref[i]
    prev = be_ref[jnp.maximum(i - 1, 0)]

    @pl.when(jnp.logical_or(i == 0, e != prev))
    def _():
        w1b[...] = w1_ref[...].astype(BF16)
        w3b[...] = w3_ref[...].astype(BF16)
        w2b[...] = w2_ref[...].astype(BF16)

    @pl.when(i < nu_ref[0])
    def _():
        x = xs_ref[...]
        a = (_silu(_dot(x, w1b[...])) * _dot(x, w3b[...])).astype(BF16)
        o_ref[...] = (_dot(a, w2b[...]) * sw_ref[...]).astype(o_ref.dtype)

    @pl.when(i >= nu_ref[0])
    def _():
        o_ref[...] = jnp.zeros_like(o_ref)


def _moe_experts(xs, slot_w, blk_e, nused, w1, w3, w2, l):
    p, d = xs.shape
    de = w1.shape[3]
    bm = MOE_BM
    grid_spec = pltpu.PrefetchScalarGridSpec(
        num_scalar_prefetch=2,
        grid=(p // bm,),
        in_specs=[pl.BlockSpec((bm, d), lambda i, be, nu: (i, 0)),
                  pl.BlockSpec((None, None, d, de), lambda i, be, nu: (l, be[i], 0, 0)),
                  pl.BlockSpec((None, None, d, de), lambda i, be, nu: (l, be[i], 0, 0)),
                  pl.BlockSpec((None, None, de, d), lambda i, be, nu: (l, be[i], 0, 0)),
                  pl.BlockSpec((bm, 1), lambda i, be, nu: (i, 0))],
        out_specs=pl.BlockSpec((bm, d), lambda i, be, nu: (i, 0)),
        scratch_shapes=[pltpu.VMEM((d, de), BF16), pltpu.VMEM((d, de), BF16), pltpu.VMEM((de, d), BF16)],
    )
    return pl.pallas_call(
        _moe_kernel,
        grid_spec=grid_spec,
        out_shape=jax.ShapeDtypeStruct((p, d), BF16),
        compiler_params=_cparams(("arbitrary",), 48),
    )(blk_e, nused, xs, w1, w3, w2, slot_w.reshape(p, 1))


def _rows(a, idx):
    return a.at[idx].get(mode="promise_in_bounds")


def _route_meta(route, n):
    i32 = jnp.int32
    eid = route[:, 0:TOP_K].astype(i32).reshape(-1)
    wt = route[:, TOP_K:2 * TOP_K].reshape(-1)
    a = n * TOP_K
    bm = MOE_BM
    nblk = -(-a // bm) + N_EXPERTS
    p = nblk * bm
    experts = jnp.arange(N_EXPERTS, dtype=i32)[None, :]
    ja = jnp.arange(a, dtype=i32)
    se, order, wsort = lax.sort((eid, ja, wt), num_keys=1, is_stable=True)
    cnt = jnp.sum((eid[:, None] == experts).astype(i32), axis=0)
    pcnt = (cnt + bm - 1) // bm * bm
    pend = jnp.cumsum(pcnt)
    pstart = pend - pcnt
    end = jnp.cumsum(cnt)
    start = end - cnt
    off = pstart - start
    d_off = off - jnp.concatenate([jnp.zeros((1,), i32), off[:-1]])
    dst_sorted = ja + jnp.sum(jnp.where(ja[:, None] >= start[None, :], d_off[None, :], 0), axis=1)
    _, pos = lax.sort((order, dst_sorted), num_keys=1)
    jp = jnp.arange(p, dtype=i32)
    in_or_after = jp[:, None] >= pstart[None, :]
    src = jp - jnp.sum(jnp.where(in_or_after, d_off[None, :], 0), axis=1)
    valid = src < jnp.sum(jnp.where(in_or_after, cnt[None, :], 0), axis=1)
    src = jnp.where(valid, src, jp % a)
    slot_tok = _rows(order, src) // TOP_K
    slot_w = jnp.where(valid, _rows(wsort, src), 0.0)
    jb = jnp.arange(nblk, dtype=i32) * bm
    blk_e = jnp.minimum(jnp.sum((jb[:, None] >= pend[None, :]).astype(i32), axis=1), N_EXPERTS - 1)
    nused = (pend[-1:] // bm).astype(i32)
    return slot_tok, slot_w, pos.reshape(n, TOP_K), blk_e, nused


def _combine_kernel(x_ref, y0_ref, y1_ref, ga_ref, gf_ref, o_ref, *, tiles_per_batch, nbatch, final):
    r = jnp.minimum(pl.program_id(0) // tiles_per_batch, nbatch)
    xn = x_ref[...] + ga_ref[pl.ds(r, 1), :] * (y0_ref[...].astype(F32) + y1_ref[...].astype(F32))
    if final:
        xn = xn * lax.rsqrt(jnp.mean(xn * xn, axis=-1, keepdims=True) + EPS) * gf_ref[...]
    o_ref[...] = xn


def _combine(x, y0, y1, mod, l, g_final, n_rows, nbatch, lat_rows, final):
    d = x.shape[1]
    tm = _pick_tile(OUT_TM, lat_rows // nbatch, n_rows)
    kern = functools.partial(_combine_kernel, tiles_per_batch=lat_rows // nbatch // tm, nbatch=nbatch,
                             final=final)
    row = pl.BlockSpec((tm, d), lambda i: (i, 0))
    return pl.pallas_call(
        kern,
        grid=(n_rows // tm,),
        in_specs=[row, row, row, pl.BlockSpec((None, SUBLANES, d), lambda i: (l, 0, 5)),
                  pl.BlockSpec((1, d), lambda i: (0, 0))],
        out_specs=row,
        out_shape=jax.ShapeDtypeStruct((n_rows, d), F32),
        compiler_params=_cparams(("arbitrary",), 40),
    )(x, y0, y1, mod, g_final.reshape(1, d))


def kernel(x, c, ctx, c_ctx, w_ada, b_ada, g_mix, g_ffn, w_in, conv_w, conv_b, conv_ln_g, conv_ln_b,
           na_rpb, hgrn_lb, hgrn_norm_g, w_out, w_router_group, b_router_group, w_router_expert,
           b_router_expert, w_exp_gate, w_exp_up, w_exp_down, g_final):
    nb, seq, d = x.shape
    ctx_len = ctx.shape[1]
    depth = w_ada.shape[0]
    lat_rows = nb * seq
    n_all = lat_rows + nb * ctx_len
    conv_ch = conv_w.shape[2]
    off_na = 2 * conv_ch
    off_hg = off_na + 3 * NA_HEADS * NA_HD
    rows = seq // GRID_W
    assert nb < SUBLANES and rows % NA_ROWS == 0 and rows >= NA_KEY_ROWS

    lbs = jnp.cumsum(jax.nn.softmax(hgrn_lb.astype(F32), axis=0), axis=0)
    lbs = lbs - lbs[:1]

    cond = jnp.concatenate([c, c_ctx[None, :], jnp.zeros((SUBLANES - nb - 1, d), F32)], axis=0)
    mod = _ada_mod(cond, w_ada, b_ada)

    xs = jnp.concatenate([x.reshape(lat_rows, d), ctx.reshape(nb * ctx_len, d)], axis=0)
    w_out_b = w_out.astype(BF16)
    na_bias = _na_bias(na_rpb.reshape((depth * NA_HEADS,) + na_rpb.shape[2:]), rows)
    for l in range(depth):
        with_ctx = l < depth - 1
        n_act = n_all if with_ctx else lat_rows
        u = _norm_in(xs, g_mix[l], mod, w_in, l, nb, lat_rows)

        conv = _conv_module(u, conv_w[l], conv_b[l], conv_ln_g[l], conv_ln_b[l], n_act, lat_rows, seq)
        na = _na_latent(u, na_bias, l * NA_HEADS, nb, seq, lat_rows, off_na)
        if with_ctx:
            na = jnp.concatenate([na, _ctx_attn(u, nb, ctx_len, lat_rows, off_na)], axis=0)
        hg = _hgrn(u, lbs[l], hgrn_norm_g[l], nb, seq, ctx_len, lat_rows, off_hg)

        w_router = jnp.concatenate(
            [w_router_group[l], w_router_expert[l],
             jnp.zeros((d, ROUTER_PAD - N_GROUPS - N_EXPERTS), F32)], axis=1)
        b_router = jnp.concatenate(
            [b_router_group[l], b_router_expert[l],
             jnp.zeros((ROUTER_PAD - N_GROUPS - N_EXPERTS,), F32)]).reshape(1, ROUTER_PAD)
        x_mid, h, route = _out_proj(xs, conv, na, hg, w_out_b, l, mod, g_ffn[l],
                                    w_router, b_router, n_act, nb, lat_rows)

        slot_tok, slot_w, pos, blk_e, nused = _route_meta(route, n_act)
        ys = _moe_experts(_rows(h, slot_tok), slot_w, blk_e, nused, w_exp_gate, w_exp_up, w_exp_down, l)
        y0 = _rows(ys, pos[:, 0])
        y1 = _rows(ys, pos[:, 1])
        xs = _combine(x_mid, y0, y1, mod, l, g_final, n_act, nb, lat_rows, final=not with_ctx)
    return xs.reshape(nb, seq, d)
```

```python
import functools

import numpy as np
import jax
import jax.numpy as jnp
from jax import lax
from jax.experimental import pallas as pl
from jax.experimental.pallas import tpu as pltpu

F32 = jnp.float32
BF16 = jnp.bfloat16

EPS = 1e-6
NEG_INF = -1e30
LOG2_E = 1.4426950408889634

GRID_W = 64
CONV_K = 31
NA_HEADS = 16
NA_HD = 64
NA_KH = 8
NA_KW = 16
HG_HEADS = 4
HG_DK = 128
N_GROUPS = 4
EXP_PER_GROUP = 8
N_EXPERTS = N_GROUPS * EXP_PER_GROUP
TOP_K = 2

LANES = 128
SUBLANES = 8

ROW_TILE = 256
ADA_TN = 1024
IN_TM = 1024
IN_TN = 512
OUT_TM = 512
CONV_ROWS = 64
NA_STEP_HEADS = 4
NA_AHEAD = 2
NA_ROWS = 8
NA_KEY_ROWS = 16
HG_BLOCK = 256
HG_AHEAD = 2
MOE_BM = 512
ROUTER_PAD = LANES
HALO = 16


def _pick_tile(pref, *extents):
    t = pref
    while t > ROW_TILE and any(e % t for e in extents):
        t //= 2
    assert all(e % t == 0 for e in extents)
    return t


def _cparams(sem, vmem_mb):
    return pltpu.CompilerParams(dimension_semantics=sem, vmem_limit_bytes=vmem_mb * 1024 * 1024)


def _dot(a, b):
    return jnp.dot(a, b, preferred_element_type=F32)


def _dot_nt(a, b):
    return lax.dot_general(a, b, (((1,), (1,)), ((), ())), preferred_element_type=F32)


def _dot_tn(a, b):
    return lax.dot_general(a, b, (((0,), (0,)), ((), ())), preferred_element_type=F32)


def _sigmoid(x):
    return 1.0 / (1.0 + jnp.exp(-x))


def _silu(x):
    return x * _sigmoid(x)


def _split_bf16(x):
    hi = x.astype(BF16)
    return hi, (x - hi.astype(F32)).astype(BF16)


def _ada_kernel(c_ref, w_ref, b_ref, o_ref):
    s_hi, s_lo = _split_bf16(_silu(c_ref[...]))
    w_hi, w_lo = _split_bf16(w_ref[0])
    o_ref[0] = _dot(s_hi, w_hi) + (_dot(s_lo, w_hi) + _dot(s_hi, w_lo)) + b_ref[0]


def _ada_mod(cond, w_ada, b_ada):
    depth, d, n = w_ada.shape
    tn = ADA_TN
    return pl.pallas_call(
        _ada_kernel,
        grid=(depth, n // tn),
        in_specs=[
            pl.BlockSpec((SUBLANES, d), lambda l, j: (0, 0)),
            pl.BlockSpec((1, d, tn), lambda l, j: (l, 0, j)),
            pl.BlockSpec((1, 1, tn), lambda l, j: (l, 0, j)),
        ],
        out_specs=pl.BlockSpec((1, SUBLANES, tn), lambda l, j: (l, 0, j)),
        out_shape=jax.ShapeDtypeStruct((depth, SUBLANES, n), F32),
        compiler_params=_cparams(("arbitrary", "arbitrary"), 40),
    )(cond, w_ada, b_ada.reshape(depth, 1, n))


def _rms_mod(x, g, scale, shift):
    y = x * lax.rsqrt(jnp.mean(x * x, axis=-1, keepdims=True) + EPS)
    return (y * g) * (1.0 + scale) + shift


def _norm_in_kernel(x_ref, g_ref, sh_ref, sc_ref, w_ref, o_ref, h_ref, *, tiles_per_batch, nbatch):
    i = pl.program_id(0)

    @pl.when(pl.program_id(1) == 0)
    def _():
        r = jnp.minimum(i // tiles_per_batch, nbatch)
        h = _rms_mod(x_ref[...], g_ref[...], sc_ref[pl.ds(r, 1), :], sh_ref[pl.ds(r, 1), :])
        h_ref[...] = h.astype(BF16)

    o_ref[...] = _dot(h_ref[...], w_ref[...])


def _norm_in(x, g, mod, w_bf16, l, nbatch, lat_rows):
    n, d = x.shape
    nout = w_bf16.shape[2]
    tm, tn = _pick_tile(IN_TM, lat_rows // nbatch, n), IN_TN
    kern = functools.partial(_norm_in_kernel, tiles_per_batch=lat_rows // nbatch // tm, nbatch=nbatch)
    return pl.pallas_call(
        kern,
        grid=(n // tm, nout // tn),
        in_specs=[
            pl.BlockSpec((tm, d), lambda i, j: (i, 0)),
            pl.BlockSpec((1, d), lambda i, j: (0, 0)),
            pl.BlockSpec((None, SUBLANES, d), lambda i, j: (l, 0, 0)),
            pl.BlockSpec((None, SUBLANES, d), lambda i, j: (l, 0, 1)),
            pl.BlockSpec((None, d, tn), lambda i, j: (l, 0, j)),
        ],
        out_specs=pl.BlockSpec((tm, tn), lambda i, j: (i, j)),
        out_shape=jax.ShapeDtypeStruct((n, nout), F32),
        scratch_shapes=[pltpu.VMEM((tm, d), BF16)],
        compiler_params=_cparams(("arbitrary", "arbitrary"), 48),
    )(x, g.reshape(1, d), mod, mod, w_bf16)


def _conv_kernel(ap_ref, gp_ref, a_ref, gt_ref, an_ref, gn_ref, w_ref, b_ref, lg_ref, lb_ref,
                 o_ref, buf_ref, acc_ref, *, lat_tiles, tiles_per_seq):
    i = pl.program_id(0)
    tc, ch = a_ref.shape
    is_lat = i < lat_tiles
    pos = i % tiles_per_seq
    first = jnp.logical_or(jnp.logical_not(is_lat), pos == 0)
    last = jnp.logical_or(jnp.logical_not(is_lat), pos == tiles_per_seq - 1)

    buf_ref[0:HALO] = jnp.where(first, 0.0, ap_ref[...] * _sigmoid(gp_ref[...]))
    buf_ref[HALO:HALO + tc] = a_ref[...] * _sigmoid(gt_ref[...])
    buf_ref[HALO + tc:2 * HALO + tc] = jnp.where(last, 0.0, an_ref[...] * _sigmoid(gn_ref[...]))

    rows = CONV_ROWS
    base = HALO - CONV_K // 2
    for c in range(ch // LANES):
        cs = slice(c * LANES, (c + 1) * LANES)
        for r in range(tc // rows):
            acc = None
            for res in range(SUBLANES):
                y = None
                for k in range(CONV_K):
                    if (base + k) % SUBLANES != res:
                        continue
                    lo = r * rows + (base + k) // SUBLANES * SUBLANES
                    term = w_ref[k:k + 1, cs] * buf_ref[lo:lo + rows + SUBLANES, cs]
                    y = term if y is None else y + term
                if y is not None:
                    y = y[res:res + rows]
                    acc = y if acc is None else acc + y
            acc_ref[r * rows:(r + 1) * rows, cs] = acc

    h = acc_ref[...] + b_ref[...]
    mu = jnp.mean(h, axis=-1, keepdims=True)
    var = jnp.mean(jnp.square(h - mu), axis=-1, keepdims=True)
    y = (h - mu) * lax.rsqrt(var + EPS) * lg_ref[...] + lb_ref[...]
    o_ref[...] = _silu(y).astype(o_ref.dtype)


def _conv_module(u, w_dw, b_dw, ln_g, ln_b, n_rows, lat_rows, seq):
    ch = w_dw.shape[1]
    tc = ROW_TILE
    per = tc // HALO
    nh = u.shape[0] // HALO
    kern = functools.partial(_conv_kernel, lat_tiles=lat_rows // tc, tiles_per_seq=seq // tc)
    prev_map = lambda c: (lambda i: (jnp.maximum(i * per - 1, 0), c))
    next_map = lambda c: (lambda i: (jnp.minimum((i + 1) * per, nh - 1), c))
    vec = lambda a: a.reshape(1, ch)
    return pl.pallas_call(
        kern,
        grid=(n_rows // tc,),
        in_specs=[
            pl.BlockSpec((HALO, ch), prev_map(0)),
            pl.BlockSpec((HALO, ch), prev_map(1)),
            pl.BlockSpec((tc, ch), lambda i: (i, 0)),
            pl.BlockSpec((tc, ch), lambda i: (i, 1)),
            pl.BlockSpec((HALO, ch), next_map(0)),
            pl.BlockSpec((HALO, ch), next_map(1)),
            pl.BlockSpec((CONV_K, ch), lambda i: (0, 0)),
            pl.BlockSpec((1, ch), lambda i: (0, 0)),
            pl.BlockSpec((1, ch), lambda i: (0, 0)),
            pl.BlockSpec((1, ch), lambda i: (0, 0)),
        ],
        out_specs=pl.BlockSpec((tc, ch), lambda i: (i, 0)),
        out_shape=jax.ShapeDtypeStruct((n_rows, ch), BF16),
        scratch_shapes=[pltpu.VMEM((tc + 2 * HALO, ch), F32), pltpu.VMEM((tc, ch), F32)],
        compiler_params=_cparams(("arbitrary",), 16),
    )(u, u, u, u, u, u, w_dw, vec(b_dw), vec(ln_g), vec(ln_b))


def _na_bias_tables(rows):
    groups = rows // NA_ROWS
    reps = [0, min(1, groups - 1), groups - 1]
    out = []
    for g in reps:
        start = int(np.clip(NA_ROWS * g - NA_KH // 2, 0, rows - NA_KEY_ROWS))
        per_row = []
        for i in range(NA_ROWS):
            r = NA_ROWS * g + i
            sr = int(np.clip(r - NA_KH // 2, 0, rows - NA_KH))
            per_row.append((sr - start, sr - r + NA_KH - 1))
        out.append(per_row)
    return out


def _na_bias(rpb, rows):
    nh = rpb.shape[0]
    ndr, ndc = 2 * NA_KH - 1, 2 * NA_KW - 1
    period = 2 * GRID_W - 1
    pad = GRID_W - NA_KW
    vp = jnp.pad(rpb.astype(F32), ((0, 0), (0, 0), (pad, period - ndc - pad)))
    hank = jnp.tile(vp, (1, 1, GRID_W + 1))[:, :, :GRID_W * (period + 1)]
    hank = hank.reshape(nh, ndr, GRID_W, period + 1)[..., :GRID_W]
    toe = hank[:, :, ::-1, :]
    c = np.arange(GRID_W)[:, None]
    j = np.arange(GRID_W)[None, :]
    ws = np.clip(c - NA_KW // 2, 0, GRID_W - NA_KW)
    col_ok = (j >= ws) & (j < ws + NA_KW)
    toe = jnp.where(col_ok[None, None], toe, NEG_INF)
    flat = jnp.transpose(toe, (0, 2, 1, 3)).reshape(nh, GRID_W, ndr * GRID_W)
    nk = NA_KEY_ROWS * GRID_W
    lpad = NA_ROWS * GRID_W
    total = 2 * nk

    def padded(shift):
        return jnp.pad(flat, ((0, 0), (0, 0), (lpad - shift, total - flat.shape[2] - lpad + shift)))

    tab = jnp.stack([padded(0), padded(GRID_W)])
    tab = tab.reshape(2, nh, GRID_W, total // LANES, LANES).transpose(0, 1, 3, 2, 4)
    tables = _na_bias_tables(rows)
    mask = np.full((len(tables), NA_ROWS, nk), NEG_INF, np.float32)
    dvals = []
    for cls, per_row in enumerate(tables):
        dvals.append(per_row[0][1] - per_row[0][0])
        for i, (off, lo) in enumerate(per_row):
            assert lo - off == dvals[-1] - i and -NA_ROWS <= lo - off < NA_ROWS
            mask[cls, i, off * GRID_W:(off + NA_KH) * GRID_W] = 0.0
    assert len({d % 2 for d in dvals}) == 1
    assert len(tables) < 3 or rows < 3 * NA_ROWS or all(off == i for i, (off, _) in enumerate(tables[1]))
    return tab * LOG2_E, jnp.asarray(mask * LOG2_E), tuple(dvals)


def _softmax_pv(s_parts, v_parts):
    m = functools.reduce(jnp.maximum, [jnp.max(s, axis=-1, keepdims=True) for s in s_parts])
    acc, l = None, None
    for s, v in zip(s_parts, v_parts):
        p = jnp.exp2(s - m)
        ps = jnp.sum(p, axis=-1, keepdims=True)
        pv = _dot(p.astype(BF16), v)
        l = ps if l is None else l + ps
        acc = pv if acc is None else acc + pv
    return acc / l


def _na_head_edge(s_raw, s_ctx, vs, vc, tab_ref, mask_ref, a, dcls, parity, pad_blocks):
    kw = s_raw[0].shape[1]
    bpk = kw // LANES
    s_parts = []
    for m, s in enumerate(s_raw):
        row_blocks = []
        for i in range(NA_ROWS):
            copy = (parity - i) % 2
            first = (dcls - i - copy + 2 * pad_blocks) // 2 + m * bpk
            bias = jnp.concatenate([tab_ref[copy, a, first + t] for t in range(bpk)], axis=1)
            bias = bias + mask_ref[i:i + 1, m * kw:(m + 1) * kw]
            row_blocks.append(s[i * GRID_W:(i + 1) * GRID_W, :] + bias)
        s_parts.append(jnp.concatenate(row_blocks, axis=0))
    return _softmax_pv(s_parts + [s_ctx], vs + [vc])


def _na_head_interior(s_raw, s_ctx, vs, vc, tab_ref, mask_ref, a, d0, pad_blocks):
    kw = s_raw[0].shape[1]
    bpk = kw // LANES
    nblk = len(s_raw) * bpk
    p_rows, pc_rows, l_rows = [], [], []
    for i in range(NA_ROWS):
        rs = slice(i * GRID_W, (i + 1) * GRID_W)
        lo_lane, hi_lane = i * GRID_W, (i + NA_KH) * GRID_W
        b_lo, b_hi = lo_lane // LANES, -(-hi_lane // LANES)
        copy = (d0 - i) % 2
        first = (d0 - i - copy + 2 * pad_blocks) // 2
        blocks = []
        for b in range(b_lo, b_hi):
            ls = slice((b % bpk) * LANES, (b % bpk + 1) * LANES)
            sb = s_raw[b // bpk][rs, ls] + tab_ref[copy, a, first + b]
            if b * LANES < lo_lane or (b + 1) * LANES > hi_lane:
                sb = sb + mask_ref[i:i + 1, b * LANES:(b + 1) * LANES]
            blocks.append(sb)
        sw = jnp.concatenate(blocks, axis=1)
        sc = s_ctx[rs]
        m = jnp.maximum(jnp.max(sw, axis=-1, keepdims=True), jnp.max(sc, axis=-1, keepdims=True))
        pw = jnp.exp2(sw - m)
        pc = jnp.exp2(sc - m)
        l_rows.append(jnp.sum(pw, axis=-1, keepdims=True) + jnp.sum(pc, axis=-1, keepdims=True))
        pieces = [jnp.zeros((GRID_W, b_lo * LANES), BF16), pw.astype(BF16),
                  jnp.zeros((GRID_W, (nblk - b_hi) * LANES), BF16)]
        p_rows.append(jnp.concatenate([x for x in pieces if x.shape[1]], axis=1))
        pc_rows.append(pc.astype(BF16))
    p = jnp.concatenate(p_rows, axis=0)
    acc = _dot(jnp.concatenate(pc_rows, axis=0), vc)
    for m, v in enumerate(vs):
        acc = acc + _dot(p[:, m * kw:(m + 1) * kw], v)
    return acc / jnp.concatenate(l_rows, axis=0)


def _na_kernel(q_ref, k0, k1, k2, k3, v0, v1, v2, v3, kc_ref, vc_ref, tab_ref, mask_ref, o_ref, *,
               groups, dvals):
    g = pl.program_id(1)
    is_edge = jnp.logical_or(g == 0, g == groups - 1)
    pad_blocks = NA_ROWS * GRID_W // LANES

    def step(interior):
        q2 = q_ref[...] * (NA_HD ** -0.5 * LOG2_E)
        lane = lax.broadcasted_iota(jnp.int32, q2.shape, 1)
        ks = [k[...].astype(BF16) for k in (k0, k1, k2, k3)]
        vs = [v[...].astype(BF16) for v in (v0, v1, v2, v3)]
        kc = kc_ref[...].astype(BF16)
        vc = vc_ref[...].astype(BF16)
        nheads = q2.shape[1] // NA_HD

        def scores(a):
            sel = jnp.logical_and(lane >= a * NA_HD, lane < (a + 1) * NA_HD)
            qa = jnp.where(sel, q2, 0.0).astype(BF16)
            return [_dot_nt(qa, k) for k in ks], _dot_nt(qa, kc)

        pending = [scores(a) for a in range(min(NA_AHEAD, nheads))]
        out = None
        for a in range(nheads):
            if a + NA_AHEAD < nheads:
                pending.append(scores(a + NA_AHEAD))
            s_raw, s_ctx = pending[a]
            if interior:
                o = _na_head_interior(s_raw, s_ctx, vs, vc, tab_ref, mask_ref, a, dvals[1], pad_blocks)
            else:
                dcls = jnp.where(g == 0, dvals[0], dvals[2])
                o = _na_head_edge(s_raw, s_ctx, vs, vc, tab_ref, mask_ref, a, dcls, dvals[0], pad_blocks)
            out = o if out is None else jnp.where(lane >= a * NA_HD, o, out)
        o_ref[...] = out.astype(o_ref.dtype)

    @pl.when(is_edge)
    def _():
        step(False)

    @pl.when(jnp.logical_not(is_edge))
    def _():
        step(True)


def _na_latent(u, bias, head_base, nbatch, seq, lat_rows, off_na):
    tab, mask, dvals = bias
    rows = seq // GRID_W
    groups = rows // NA_ROWS
    nq = NA_ROWS * GRID_W
    kblk = ROW_TILE
    nkb = NA_KEY_ROWS * GRID_W // kblk
    assert nkb == 4
    hp = NA_STEP_HEADS
    wid = hp * NA_HD
    qcol = off_na // wid
    hsteps = NA_HEADS // hp
    kcol, vcol = qcol + hsteps, qcol + 2 * hsteps
    assert off_na % wid == 0 and NA_HEADS % hp == 0 and head_base % hp == 0
    kb_per_batch = seq // kblk
    kb_per_grow = GRID_W * NA_ROWS // kblk
    lat_kb = lat_rows // kblk

    def kmap(col, m):
        def f(h, g, b):
            st = jnp.clip(g * kb_per_grow - (NA_KH // 2) * GRID_W // kblk, 0, kb_per_batch - nkb)
            return (b * kb_per_batch + st + m, col + h)
        return f

    def mask_map(h, g, b):
        return (jnp.where(g == 0, 0, jnp.where(g == groups - 1, 2, 1)), 0, 0)

    in_specs = [pl.BlockSpec((nq, wid), lambda h, g, b: (b * groups + g, qcol + h))]
    in_specs += [pl.BlockSpec((kblk, wid), kmap(kcol, m)) for m in range(nkb)]
    in_specs += [pl.BlockSpec((kblk, wid), kmap(vcol, m)) for m in range(nkb)]
    in_specs += [pl.BlockSpec((ROW_TILE, wid), lambda h, g, b: (lat_kb + b, kcol + h)),
                 pl.BlockSpec((ROW_TILE, wid), lambda h, g, b: (lat_kb + b, vcol + h)),
                 pl.BlockSpec((2, hp) + tab.shape[2:], lambda h, g, b: (0, head_base // hp + h, 0, 0, 0)),
                 pl.BlockSpec((None,) + mask.shape[1:], mask_map)]
    return pl.pallas_call(
        functools.partial(_na_kernel, groups=groups, dvals=dvals),
        grid=(hsteps, groups, nbatch),
        in_specs=in_specs,
        out_specs=pl.BlockSpec((nq, wid), lambda h, g, b: (b * groups + g, h)),
        out_shape=jax.ShapeDtypeStruct((lat_rows, NA_HEADS * NA_HD), BF16),
        compiler_params=_cparams(("arbitrary", "arbitrary", "arbitrary"), 48),
    )(*([u] * 11), tab, mask)


def _ctx_attn_kernel(q_ref, k_ref, v_ref, o_ref):
    q2 = q_ref[...] * (NA_HD ** -0.5 * LOG2_E)
    lane = lax.broadcasted_iota(jnp.int32, q2.shape, 1)
    k = k_ref[...].astype(BF16)
    v = v_ref[...].astype(BF16)
    outs = []
    for a in range(2):
        sel = (lane < NA_HD) if a == 0 else (lane >= NA_HD)
        qa = jnp.where(sel, q2, 0.0).astype(BF16)
        outs.append(_softmax_pv([_dot_nt(qa, k)], [v]))
    o_ref[...] = jnp.where(lane < NA_HD, outs[0], outs[1]).astype(o_ref.dtype)


def _ctx_attn(u, nbatch, ctx_len, lat_rows, off_na):
    assert ctx_len == ROW_TILE
    qcol = off_na // LANES
    heads2 = NA_HEADS * NA_HD // LANES
    base = lat_rows // ROW_TILE
    spec = lambda col: pl.BlockSpec((ROW_TILE, LANES), lambda b, h: (base + b, col + h))
    return pl.pallas_call(
        _ctx_attn_kernel,
        grid=(nbatch, heads2),
        in_specs=[spec(qcol), spec(qcol + heads2), spec(qcol + 2 * heads2)],
        out_specs=pl.BlockSpec((ROW_TILE, LANES), lambda b, h: (b, h)),
        out_shape=jax.ShapeDtypeStruct((nbatch * ctx_len, NA_HEADS * NA_HD), BF16),
        compiler_params=_cparams(("arbitrary", "arbitrary"), 16),
    )(u, u, u)


def _hg_level_map(rev):
    size = HG_BLOCK // 2
    t = np.arange(size)[:, None]
    s = np.arange(size)[None, :]
    x = t ^ s
    lvl = np.where(x > 0, np.frexp(np.maximum(x, 1))[1] - 1, -1)
    causal = (s < t) if not rev else (s > t)
    out = np.where(causal, lvl, -1)
    out = np.where(t == s, int(np.log2(size)), out)
    return out.astype(np.int32)


def _hg_tri(rev):
    t = np.arange(HG_BLOCK)[:, None]
    s = np.arange(HG_BLOCK)[None, :]
    return ((s <= t) if not rev else (s >= t)).astype(np.float32)


def _hg_anchor(b3, m, rev):
    nv = b3.shape[0]
    if m >= SUBLANES:
        w = m // SUBLANES
        b4 = b3.reshape(nv // (2 * w), 2 * w, SUBLANES, LANES)
        a = b4[:, w:w + 1, 0:1, :] if rev else b4[:, w - 1:w, SUBLANES - 1:SUBLANES, :]
        return jnp.broadcast_to(a, b4.shape).reshape(b3.shape)
    sub = lax.broadcasted_iota(jnp.int32, b3.shape, 1)
    out = None
    for g in range(SUBLANES // (2 * m)):
        idx = g * 2 * m + (m if rev else m - 1)
        a = jnp.broadcast_to(b3[:, idx:idx + 1, :], b3.shape)
        out = a if out is None else jnp.where(sub >= g * 2 * m, a, out)
    return out


def _neg_abs(x):
    bits = lax.bitcast_convert_type(x, jnp.uint32) | jnp.uint32(0x80000000)
    return lax.bitcast_convert_type(bits, F32)


def _hg_pick(q3, k3, m, rev):
    nv = q3.shape[0]
    if m >= SUBLANES:
        w = m // SUBLANES
        shape4 = (nv // (2 * w), 2 * w, SUBLANES, LANES)
        q4, k4 = q3.reshape(shape4), k3.reshape(shape4)
        lower, upper = (q4, k4) if rev else (k4, q4)
        return jnp.concatenate([lower[:, :w], upper[:, w:]], axis=1).reshape(q3.shape)
    upper_rows = (lax.broadcasted_iota(jnp.int32, q3.shape, 1) & m) != 0
    return jnp.where(upper_rows, k3 if rev else q3, q3 if rev else k3)


def _hg_gates(q, z, alog, clog, oml, tri):
    q = _silu(q)
    t = jnp.exp(-jnp.abs(z))
    lsig = jnp.minimum(z, 0.0) - jnp.log(1.0 + t)
    cc = clog + lsig
    logf = jnp.maximum(alog, cc) + jnp.log(1.0 + jnp.exp(-jnp.abs(alog - cc)))
    kk = oml * jnp.where(z >= 0.0, t, 1.0) / (1.0 + t)

    hi = logf.astype(BF16)
    r1 = logf - hi.astype(F32)
    mid = r1.astype(BF16)
    lo = (r1 - mid.astype(F32)).astype(BF16)
    b = (_dot(tri, hi) + _dot(tri, mid) + _dot(tri, lo)) * LOG2_E
    return q, kk, b


def _hg_mix(q, kk, b, v, lv, st, rev):
    n = q.shape[0]
    half = n // 2
    nlev = int(np.log2(n))
    shape3 = (n // SUBLANES, SUBLANES, LANES)
    b3, q3, k3 = b.reshape(shape3), q.reshape(shape3), kk.reshape(shape3)
    halves = (slice(0, half), slice(half, n))
    qb, kb = q.astype(BF16), kk.astype(BF16)
    acc = [jnp.where(lv == nlev - 1, _dot_nt(qb[hs], kb[hs]), 0.0) for hs in halves]
    for lev in range(nlev - 1):
        m = 1 << lev
        e = jnp.exp2(_neg_abs(b3 - _hg_anchor(b3, m, rev)))
        w = (_hg_pick(q3, k3, m, rev) * e).reshape(n, LANES).astype(BF16)
        acc = [jnp.where(lv == lev, _dot_nt(w[hs], w[hs]), a) for hs, a in zip(halves, acc)]
    first, second = (halves[1], halves[0]) if rev else halves
    e = jnp.exp2(_neg_abs(b - (b[half:half + 1, :] if rev else b[half - 1:half, :])))
    top = _dot_nt((q[second] * e[second]).astype(BF16), (kk[first] * e[first]).astype(BF16))

    vb = v.astype(BF16)
    a0, a1 = acc[0].astype(BF16), acc[1].astype(BF16)
    tb = top.astype(BF16)
    if rev:
        o_lo = _dot(jnp.concatenate([a0, tb], axis=1), vb)
        o_hi = _dot(a1, vb[halves[1]])
    else:
        o_lo = _dot(a0, vb[halves[0]])
        o_hi = _dot(jnp.concatenate([tb, a1], axis=1), vb)
    b_last = b[0:1, :] if rev else b[n - 1:n, :]
    qh = (q * jnp.exp2(b)).astype(BF16)
    o = jnp.concatenate([o_lo, o_hi], axis=0) + _dot_nt(qh, st.astype(BF16))
    kh = (kk * jnp.exp2(b_last - b)).astype(BF16)
    st_new = st * jnp.exp2(b_last) + _dot_tn(vb, kh)
    return o, st_new


def _hg_gates_ahead(q_ref, z_ref, al_ref, cl_ref, om_ref, tri):
    def gates(h):
        hs = slice(h * HG_DK, (h + 1) * HG_DK)
        return _hg_gates(q_ref[:, hs], z_ref[:, hs], al_ref[:, hs], cl_ref[:, hs], om_ref[:, hs], tri)

    pending = [gates(h) for h in range(min(HG_AHEAD, HG_HEADS))]
    for h in range(HG_HEADS):
        if h + HG_AHEAD < HG_HEADS:
            pending.append(gates(h + HG_AHEAD))
        yield pending[h]


def _hg_fwd_kernel(q_ref, v_ref, z_ref, al_ref, cl_ref, om_ref, tri_ref, lv_ref, o_ref, st_ref):
    @pl.when(pl.program_id(1) == 0)
    def _():
        st_ref[...] = jnp.zeros_like(st_ref)

    lv = lv_ref[...]
    gates = _hg_gates_ahead(q_ref, z_ref, al_ref, cl_ref, om_ref, tri_ref[...])
    for h in range(HG_HEADS):
        hs = slice(h * HG_DK, (h + 1) * HG_DK)
        o, st = _hg_mix(*next(gates), v_ref[:, hs], lv, st_ref[h], False)
        o_ref[:, hs] = o
        st_ref[h] = st


def _hg_bwd_kernel(q_ref, v_ref, z_ref, g_ref, of_ref, al_ref, cl_ref, om_ref, ng_ref, tri_ref, lv_ref,
                   o_ref, st_ref):
    @pl.when(pl.program_id(1) == 0)
    def _():
        st_ref[...] = jnp.zeros_like(st_ref)

    lv = lv_ref[...]
    gates = _hg_gates_ahead(q_ref, z_ref, al_ref, cl_ref, om_ref, tri_ref[...])
    for h in range(HG_HEADS):
        hs = slice(h * HG_DK, (h + 1) * HG_DK)
        o, st = _hg_mix(*next(gates), v_ref[:, hs], lv, st_ref[h], True)
        st_ref[h] = st
        t = of_ref[:, hs] + o
        y = t * lax.rsqrt(jnp.mean(t * t, axis=-1, keepdims=True) + EPS)
        o_ref[:, hs] = (y * ng_ref[:, hs] * _silu(g_ref[:, hs])).astype(o_ref.dtype)


def _hgrn(u, lb, norm_g, nbatch, seq, ctx_len, lat_rows, off_hg):
    assert ctx_len == HG_BLOCK
    n = u.shape[0]
    hd = HG_HEADS * HG_DK
    col = off_hg // hd
    per = seq // HG_BLOCK
    lat_blocks = lat_rows // HG_BLOCK
    lbf = lb.astype(F32)
    alog, clog, oml = jnp.log(lbf), jnp.log1p(-lbf), 1.0 - lbf

    def fmap(c):
        return lambda b, j: (jnp.where(j == 0, lat_blocks + b, b * per + j - 1), c)

    def bmap(c):
        return lambda b, j: (jnp.where(j == 0, lat_blocks + b, b * per + per - j), c)

    const = lambda shape: pl.BlockSpec(shape, lambda b, j: (0, 0))
    grid = (nbatch, per + 1)
    vec = lambda a: a.reshape(1, hd)
    o_f = pl.pallas_call(
        _hg_fwd_kernel,
        grid=grid,
        in_specs=[pl.BlockSpec((HG_BLOCK, hd), fmap(col)), pl.BlockSpec((HG_BLOCK, hd), fmap(col + 1)),
                  pl.BlockSpec((HG_BLOCK, hd), fmap(col + 2)),
                  const((1, hd)), const((1, hd)), const((1, hd)),
                  const((HG_BLOCK, HG_BLOCK)), const((HG_BLOCK // 2, HG_BLOCK // 2))],
        out_specs=pl.BlockSpec((HG_BLOCK, hd), fmap(0)),
        out_shape=jax.ShapeDtypeStruct((n, hd), F32),
        scratch_shapes=[pltpu.VMEM((HG_HEADS, HG_DK, HG_DK), F32)],
        compiler_params=_cparams(("arbitrary", "arbitrary"), 32),
    )(u, u, u, vec(alog[0]), vec(clog[0]), vec(oml[0]),
      jnp.asarray(_hg_tri(False), BF16), jnp.asarray(_hg_level_map(False)))
    return pl.pallas_call(
        _hg_bwd_kernel,
        grid=grid,
        in_specs=[pl.BlockSpec((HG_BLOCK, hd), bmap(col)), pl.BlockSpec((HG_BLOCK, hd), bmap(col + 1)),
                  pl.BlockSpec((HG_BLOCK, hd), bmap(col + 3)), pl.BlockSpec((HG_BLOCK, hd), bmap(col + 4)),
                  pl.BlockSpec((HG_BLOCK, hd), bmap(0)),
                  const((1, hd)), const((1, hd)), const((1, hd)), const((1, hd)),
                  const((HG_BLOCK, HG_BLOCK)), const((HG_BLOCK // 2, HG_BLOCK // 2))],
        out_specs=pl.BlockSpec((HG_BLOCK, hd), bmap(0)),
        out_shape=jax.ShapeDtypeStruct((n, hd), BF16),
        scratch_shapes=[pltpu.VMEM((HG_HEADS, HG_DK, HG_DK), F32)],
        compiler_params=_cparams(("arbitrary", "arbitrary"), 32),
    )(u, u, u, u, o_f, vec(alog[1]), vec(clog[1]), vec(oml[1]), vec(norm_g.astype(F32)),
      jnp.asarray(_hg_tri(True), BF16), jnp.asarray(_hg_level_map(True)))


def _out_kernel(x_ref, cv_ref, na_ref, hg_ref, w_ref, ga_ref, g2_ref, sh2_ref, s2_ref, wrh_ref, wrl_ref, br_ref,
                xo_ref, h_ref, rt_ref, *, tiles_per_batch, nbatch):
    r = jnp.minimum(pl.program_id(0) // tiles_per_batch, nbatch)
    c0 = cv_ref.shape[1]
    c1 = c0 + na_ref.shape[1]
    mix = (_dot(cv_ref[...], w_ref[0:c0, :]) + _dot(na_ref[...], w_ref[c0:c1, :])
           + _dot(hg_ref[...], w_ref[c1:, :]))
    xn = x_ref[...] + ga_ref[pl.ds(r, 1), :] * mix
    xo_ref[...] = xn
    h = _rms_mod(xn, g2_ref[...], s2_ref[pl.ds(r, 1), :], sh2_ref[pl.ds(r, 1), :])
    h_ref[...] = h.astype(h_ref.dtype)
    h_hi, h_lo = _split_bf16(h)
    logits = (_dot(h_hi, wrh_ref[...]) + (_dot(h_lo, wrh_ref[...]) + _dot(h_hi, wrl_ref[...]))
              + br_ref[...])
    rt_ref[...] = _route_rows(logits)


def _route_rows(lg):
    lane = lax.broadcasted_iota(jnp.int32, lg.shape, 1)
    big = jnp.int32(2 ** 30)
    low = jnp.float32(-3e38)

    def first_max(vals, mask):
        m = jnp.max(vals, axis=-1, keepdims=True)
        idx = jnp.min(jnp.where(jnp.logical_and(vals == m, mask), lane, big), axis=-1, keepdims=True)
        return m, idx

    gmask = lane < N_GROUPS
    gl = jnp.where(gmask, lg, low)
    gm, grp = first_max(gl, gmask)
    p_grp = 1.0 / jnp.sum(jnp.where(gmask, jnp.exp(gl - gm), 0.0), axis=-1, keepdims=True)
    lo = N_GROUPS + grp * EXP_PER_GROUP
    emask = jnp.logical_and(lane >= lo, lane < lo + EXP_PER_GROUP)
    el = jnp.where(emask, lg, low)
    m1, i1 = first_max(el, emask)
    emask2 = jnp.logical_and(emask, lane != i1)
    el2 = jnp.where(emask2, lg, low)
    m2, i2 = first_max(el2, emask2)
    t = jnp.exp(m2 - m1)
    w1 = p_grp / (1.0 + t)
    w2 = p_grp * t / (1.0 + t)
    e1 = (i1 - N_GROUPS).astype(F32)
    e2 = (i2 - N_GROUPS).astype(F32)
    return jnp.where(lane == 0, e1, jnp.where(lane == 1, e2, jnp.where(lane == 2, w1,
                     jnp.where(lane == 3, w2, 0.0))))


def _out_proj(x, conv, na, hg, w_bf16, l, mod, g_ffn, w_router, b_router, n_rows, nbatch, lat_rows):
    d = x.shape[1]
    w_router_hi, w_router_lo = _split_bf16(w_router)
    tm = _pick_tile(OUT_TM, lat_rows // nbatch, n_rows)
    kern = functools.partial(_out_kernel, tiles_per_batch=lat_rows // nbatch // tm, nbatch=nbatch)
    row = lambda w: pl.BlockSpec((tm, w), lambda i: (i, 0))
    const = lambda shape: pl.BlockSpec(shape, lambda i: (0, 0))
    modc = lambda c: pl.BlockSpec((None, SUBLANES, d), lambda i: (l, 0, c))
    return pl.pallas_call(
        kern,
        grid=(n_rows // tm,),
        in_specs=[row(d), row(conv.shape[1]), row(na.shape[1]), row(hg.shape[1]),
                  pl.BlockSpec((None, d, d), lambda i: (l, 0, 0)),
                  modc(2),
                  const((1, d)),
                  modc(3),
                  modc(4),
                  const((d, ROUTER_PAD)), const((d, ROUTER_PAD)), const((1, ROUTER_PAD))],
        out_specs=[row(d), row(d), row(ROUTER_PAD)],
        out_shape=[jax.ShapeDtypeStruct((n_rows, d), F32), jax.ShapeDtypeStruct((n_rows, d), BF16),
                   jax.ShapeDtypeStruct((n_rows, ROUTER_PAD), F32)],
        compiler_params=_cparams(("arbitrary",), 56),
    )(x, conv, na, hg, w_bf16, mod, g_ffn.reshape(1, d), mod, mod, w_router_hi, w_router_lo, b_router)


def _moe_kernel(be_ref, nu_ref, xs_ref, w1_ref, w3_ref, w2_ref, sw_ref, o_ref, w1b, w3b, w2b):
    i = pl.program_id(0)
    e = be_ref[i]
    prev = be_ref[jnp.maximum(i - 1, 0)]

    @pl.when(jnp.logical_or(i == 0, e != prev))
    def _():
        w1b[...] = w1_ref[...].astype(BF16)
        w3b[...] = w3_ref[...].astype(BF16)
        w2b[...] = w2_ref[...].astype(BF16)

    @pl.when(i < nu_ref[0])
    def _():
        x = xs_ref[...]
        a = (_silu(_dot(x, w1b[...])) * _dot(x, w3b[...])).astype(BF16)
        o_ref[...] = (_dot(a, w2b[...]) * sw_ref[...]).astype(o_ref.dtype)

    @pl.when(i >= nu_ref[0])
    def _():
        o_ref[...] = jnp.zeros_like(o_ref)


def _moe_experts(xs, slot_w, blk_e, nused, w1, w3, w2, l):
    p, d = xs.shape
    de = w1.shape[3]
    bm = MOE_BM
    grid_spec = pltpu.PrefetchScalarGridSpec(
        num_scalar_prefetch=2,
        grid=(p // bm,),
        in_specs=[pl.BlockSpec((bm, d), lambda i, be, nu: (i, 0)),
                  pl.BlockSpec((None, None, d, de), lambda i, be, nu: (l, be[i], 0, 0)),
                  pl.BlockSpec((None, None, d, de), lambda i, be, nu: (l, be[i], 0, 0)),
                  pl.BlockSpec((None, None, de, d), lambda i, be, nu: (l, be[i], 0, 0)),
                  pl.BlockSpec((bm, 1), lambda i, be, nu: (i, 0))],
        out_specs=pl.BlockSpec((bm, d), lambda i, be, nu: (i, 0)),
        scratch_shapes=[pltpu.VMEM((d, de), BF16), pltpu.VMEM((d, de), BF16), pltpu.VMEM((de, d), BF16)],
    )
    return pl.pallas_call(
        _moe_kernel,
        grid_spec=grid_spec,
        out_shape=jax.ShapeDtypeStruct((p, d), BF16),
        compiler_params=_cparams(("arbitrary",), 48),
    )(blk_e, nused, xs, w1, w3, w2, slot_w.reshape(p, 1))


def _rows(a, idx):
    return a.at[idx].get(mode="promise_in_bounds")


def _route_meta(route, n):
    i32 = jnp.int32
    eid = route[:, 0:TOP_K].astype(i32).reshape(-1)
    wt = route[:, TOP_K:2 * TOP_K].reshape(-1)
    a = n * TOP_K
    bm = MOE_BM
    nblk = -(-a // bm) + N_EXPERTS
    p = nblk * bm
    experts = jnp.arange(N_EXPERTS, dtype=i32)[None, :]
    ja = jnp.arange(a, dtype=i32)
    se, order, wsort = lax.sort((eid, ja, wt), num_keys=1, is_stable=True)
    cnt = jnp.sum((eid[:, None] == experts).astype(i32), axis=0)
    pcnt = (cnt + bm - 1) // bm * bm
    pend = jnp.cumsum(pcnt)
    pstart = pend - pcnt
    end = jnp.cumsum(cnt)
    start = end - cnt
    off = pstart - start
    d_off = off - jnp.concatenate([jnp.zeros((1,), i32), off[:-1]])
    dst_sorted = ja + jnp.sum(jnp.where(ja[:, None] >= start[None, :], d_off[None, :], 0), axis=1)
    _, pos = lax.sort((order, dst_sorted), num_keys=1)
    jp = jnp.arange(p, dtype=i32)
    in_or_after = jp[:, None] >= pstart[None, :]
    src = jp - jnp.sum(jnp.where(in_or_after, d_off[None, :], 0), axis=1)
    valid = src < jnp.sum(jnp.where(in_or_after, cnt[None, :], 0), axis=1)
    src = jnp.where(valid, src, jp % a)
    slot_tok = _rows(order, src) // TOP_K
    slot_w = jnp.where(valid, _rows(wsort, src), 0.0)
    jb = jnp.arange(nblk, dtype=i32) * bm
    blk_e = jnp.minimum(jnp.sum((jb[:, None] >= pend[None, :]).astype(i32), axis=1), N_EXPERTS - 1)
    nused = (pend[-1:] // bm).astype(i32)
    return slot_tok, slot_w, pos.reshape(n, TOP_K), blk_e, nused


def _combine_kernel(x_ref, y0_ref, y1_ref, ga_ref, gf_ref, o_ref, *, tiles_per_batch, nbatch, final):
    r = jnp.minimum(pl.program_id(0) // tiles_per_batch, nbatch)
    xn = x_ref[...] + ga_ref[pl.ds(r, 1), :] * (y0_ref[...].astype(F32) + y1_ref[...].astype(F32))
    if final:
        xn = xn * lax.rsqrt(jnp.mean(xn * xn, axis=-1, keepdims=True) + EPS) * gf_ref[...]
    o_ref[...] = xn


def _combine(x, y0, y1, mod, l, g_final, n_rows, nbatch, lat_rows, final):
    d = x.shape[1]
    tm = _pick_tile(OUT_TM, lat_rows // nbatch, n_rows)
    kern = functools.partial(_combine_kernel, tiles_per_batch=lat_rows // nbatch // tm, nbatch=nbatch,
                             final=final)
    row = pl.BlockSpec((tm, d), lambda i: (i, 0))
    return pl.pallas_call(
        kern,
        grid=(n_rows // tm,),
        in_specs=[row, row, row, pl.BlockSpec((None, SUBLANES, d), lambda i: (l, 0, 5)),
                  pl.BlockSpec((1, d), lambda i: (0, 0))],
        out_specs=row,
        out_shape=jax.ShapeDtypeStruct((n_rows, d), F32),
        compiler_params=_cparams(("arbitrary",), 40),
    )(x, y0, y1, mod, g_final.reshape(1, d))


def kernel(x, c, ctx, c_ctx, w_ada, b_ada, g_mix, g_ffn, w_in, conv_w, conv_b, conv_ln_g, conv_ln_b,
           na_rpb, hgrn_lb, hgrn_norm_g, w_out, w_router_group, b_router_group, w_router_expert,
           b_router_expert, w_exp_gate, w_exp_up, w_exp_down, g_final):
    nb, seq, d = x.shape
    ctx_len = ctx.shape[1]
    depth = w_ada.shape[0]
    lat_rows = nb * seq
    n_all = lat_rows + nb * ctx_len
    conv_ch = conv_w.shape[2]
    off_na = 2 * conv_ch
    off_hg = off_na + 3 * NA_HEADS * NA_HD
    rows = seq // GRID_W
    assert nb < SUBLANES and rows % NA_ROWS == 0 and rows >= NA_KEY_ROWS

    lbs = jnp.cumsum(jax.nn.softmax(hgrn_lb.astype(F32), axis=0), axis=0)
    lbs = lbs - lbs[:1]

    cond = jnp.concatenate([c, c_ctx[None, :], jnp.zeros((SUBLANES - nb - 1, d), F32)], axis=0)
    mod = _ada_mod(cond, w_ada, b_ada)

    xs = jnp.concatenate([x.reshape(lat_rows, d), ctx.reshape(nb * ctx_len, d)], axis=0)
    w_in_b = w_in.astype(BF16)
    w_out_b = w_out.astype(BF16)
    na_bias = _na_bias(na_rpb.reshape((depth * NA_HEADS,) + na_rpb.shape[2:]), rows)
    for l in range(depth):
        with_ctx = l < depth - 1
        n_act = n_all if with_ctx else lat_rows
        u = _norm_in(xs, g_mix[l], mod, w_in_b, l, nb, lat_rows)

        conv = _conv_module(u, conv_w[l], conv_b[l], conv_ln_g[l], conv_ln_b[l], n_act, lat_rows, seq)
        na = _na_latent(u, na_bias, l * NA_HEADS, nb, seq, lat_rows, off_na)
        if with_ctx:
            na = jnp.concatenate([na, _ctx_attn(u, nb, ctx_len, lat_rows, off_na)], axis=0)
        hg = _hgrn(u, lbs[l], hgrn_norm_g[l], nb, seq, ctx_len, lat_rows, off_hg)

        w_router = jnp.concatenate(
            [w_router_group[l], w_router_expert[l],
             jnp.zeros((d, ROUTER_PAD - N_GROUPS - N_EXPERTS), F32)], axis=1)
        b_router = jnp.concatenate(
            [b_router_group[l], b_router_expert[l],
             jnp.zeros((ROUTER_PAD - N_GROUPS - N_EXPERTS,), F32)]).reshape(1, ROUTER_PAD)
        x_mid, h, route = _out_proj(xs, conv, na, hg, w_out_b, l, mod, g_ffn[l],
                                    w_router, b_router, n_act, nb, lat_rows)

        slot_tok, slot_w, pos, blk_e, nused = _route_meta(route, n_act)
        ys = _moe_experts(_rows(h, slot_tok), slot_w, blk_e, nused, w_exp_gate, w_exp_up, w_exp_down, l)
        y0 = _rows(ys, pos[:, 0])
        y1 = _rows(ys, pos[:, 1])
        xs = _combine(x_mid, y0, y1, mod, l, g_final, n_act, nb, lat_rows, final=not with_ctx)
    return xs.reshape(nb, seq, d)
```

```python
import functools

import numpy as np
import jax
import jax.numpy as jnp
from jax import lax
from jax.experimental import pallas as pl
from jax.experimental.pallas import tpu as pltpu

F32 = jnp.float32
BF16 = jnp.bfloat16

EPS = 1e-6
NEG_INF = -1e30
LOG2_E = 1.4426950408889634

GRID_W = 64
CONV_K = 31
NA_HEADS = 16
NA_HD = 64
NA_KH = 8
NA_KW = 16
HG_HEADS = 4
HG_DK = 128
N_GROUPS = 4
EXP_PER_GROUP = 8
N_EXPERTS = N_GROUPS * EXP_PER_GROUP
TOP_K = 2

LANES = 128
SUBLANES = 8

ROW_TILE = 256
ADA_TN = 1024
IN_TM = 1024
IN_TN = 512
OUT_TM = 512
CONV_ROWS = 64
NA_STEP_HEADS = 4
NA_AHEAD = 3
NA_ROWS = 8
NA_KEY_ROWS = 16
HG_BLOCK = 256
HG_AHEAD = 3
MOE_BM = 512
ROUTER_PAD = LANES
HALO = 16


def _pick_tile(pref, *extents):
    t = pref
    while t > ROW_TILE and any(e % t for e in extents):
        t //= 2
    assert all(e % t == 0 for e in extents)
    return t


def _cparams(sem, vmem_mb):
    return pltpu.CompilerParams(dimension_semantics=sem, vmem_limit_bytes=vmem_mb * 1024 * 1024)


def _dot(a, b):
    return jnp.dot(a, b, preferred_element_type=F32)


def _dot_nt(a, b):
    return lax.dot_general(a, b, (((1,), (1,)), ((), ())), preferred_element_type=F32)


def _dot_tn(a, b):
    return lax.dot_general(a, b, (((0,), (0,)), ((), ())), preferred_element_type=F32)


def _sigmoid(x):
    return 1.0 / (1.0 + jnp.exp(-x))


def _silu(x):
    return x * _sigmoid(x)


def _split_bf16(x):
    hi = x.astype(BF16)
    return hi, (x - hi.astype(F32)).astype(BF16)


def _ada_kernel(c_ref, w_ref, b_ref, o_ref):
    s_hi, s_lo = _split_bf16(_silu(c_ref[...]))
    w_hi, w_lo = _split_bf16(w_ref[0])
    o_ref[0] = _dot(s_hi, w_hi) + (_dot(s_lo, w_hi) + _dot(s_hi, w_lo)) + b_ref[0]


def _ada_mod(cond, w_ada, b_ada):
    depth, d, n = w_ada.shape
    tn = ADA_TN
    return pl.pallas_call(
        _ada_kernel,
        grid=(depth, n // tn),
        in_specs=[
            pl.BlockSpec((SUBLANES, d), lambda l, j: (0, 0)),
            pl.BlockSpec((1, d, tn), lambda l, j: (l, 0, j)),
            pl.BlockSpec((1, 1, tn), lambda l, j: (l, 0, j)),
        ],
        out_specs=pl.BlockSpec((1, SUBLANES, tn), lambda l, j: (l, 0, j)),
        out_shape=jax.ShapeDtypeStruct((depth, SUBLANES, n), F32),
        compiler_params=_cparams(("arbitrary", "arbitrary"), 40),
    )(cond, w_ada, b_ada.reshape(depth, 1, n))


def _rms_mod(x, g, scale, shift):
    y = x * lax.rsqrt(jnp.mean(x * x, axis=-1, keepdims=True) + EPS)
    return (y * g) * (1.0 + scale) + shift


def _norm_in_kernel(x_ref, g_ref, sh_ref, sc_ref, w_ref, o_ref, h_ref, *, tiles_per_batch, nbatch):
    i = pl.program_id(0)

    @pl.when(pl.program_id(1) == 0)
    def _():
        r = jnp.minimum(i // tiles_per_batch, nbatch)
        h = _rms_mod(x_ref[...], g_ref[...], sc_ref[pl.ds(r, 1), :], sh_ref[pl.ds(r, 1), :])
        h_ref[...] = h.astype(BF16)

    o_ref[...] = _dot(h_ref[...], w_ref[...])


def _norm_in(x, g, mod, w_bf16, l, nbatch, lat_rows):
    n, d = x.shape
    nout = w_bf16.shape[2]
    tm, tn = _pick_tile(IN_TM, lat_rows // nbatch, n), IN_TN
    kern = functools.partial(_norm_in_kernel, tiles_per_batch=lat_rows // nbatch // tm, nbatch=nbatch)
    return pl.pallas_call(
        kern,
        grid=(n // tm, nout // tn),
        in_specs=[
            pl.BlockSpec((tm, d), lambda i, j: (i, 0)),
            pl.BlockSpec((1, d), lambda i, j: (0, 0)),
            pl.BlockSpec((None, SUBLANES, d), lambda i, j: (l, 0, 0)),
            pl.BlockSpec((None, SUBLANES, d), lambda i, j: (l, 0, 1)),
            pl.BlockSpec((None, d, tn), lambda i, j: (l, 0, j)),
        ],
        out_specs=pl.BlockSpec((tm, tn), lambda i, j: (i, j)),
        out_shape=jax.ShapeDtypeStruct((n, nout), F32),
        scratch_shapes=[pltpu.VMEM((tm, d), BF16)],
        compiler_params=_cparams(("arbitrary", "arbitrary"), 48),
    )(x, g.reshape(1, d), mod, mod, w_bf16)


def _conv_kernel(ap_ref, gp_ref, a_ref, gt_ref, an_ref, gn_ref, w_ref, b_ref, lg_ref, lb_ref,
                 o_ref, buf_ref, acc_ref, *, lat_tiles, tiles_per_seq):
    i = pl.program_id(0)
    tc, ch = a_ref.shape
    is_lat = i < lat_tiles
    pos = i % tiles_per_seq
    first = jnp.logical_or(jnp.logical_not(is_lat), pos == 0)
    last = jnp.logical_or(jnp.logical_not(is_lat), pos == tiles_per_seq - 1)

    buf_ref[0:HALO] = jnp.where(first, 0.0, ap_ref[...] * _sigmoid(gp_ref[...]))
    buf_ref[HALO:HALO + tc] = a_ref[...] * _sigmoid(gt_ref[...])
    buf_ref[HALO + tc:2 * HALO + tc] = jnp.where(last, 0.0, an_ref[...] * _sigmoid(gn_ref[...]))

    rows = CONV_ROWS
    base = HALO - CONV_K // 2
    for c in range(ch // LANES):
        cs = slice(c * LANES, (c + 1) * LANES)
        for r in range(tc // rows):
            acc = None
            for res in range(SUBLANES):
                y = None
                for k in range(CONV_K):
                    if (base + k) % SUBLANES != res:
                        continue
                    lo = r * rows + (base + k) // SUBLANES * SUBLANES
                    term = w_ref[k:k + 1, cs] * buf_ref[lo:lo + rows + SUBLANES, cs]
                    y = term if y is None else y + term
                if y is not None:
                    y = y[res:res + rows]
                    acc = y if acc is None else acc + y
            acc_ref[r * rows:(r + 1) * rows, cs] = acc

    h = acc_ref[...] + b_ref[...]
    mu = jnp.mean(h, axis=-1, keepdims=True)
    var = jnp.mean(jnp.square(h - mu), axis=-1, keepdims=True)
    y = (h - mu) * lax.rsqrt(var + EPS) * lg_ref[...] + lb_ref[...]
    o_ref[...] = _silu(y).astype(o_ref.dtype)


def _conv_module(u, w_dw, b_dw, ln_g, ln_b, n_rows, lat_rows, seq):
    ch = w_dw.shape[1]
    tc = ROW_TILE
    per = tc // HALO
    nh = u.shape[0] // HALO
    kern = functools.partial(_conv_kernel, lat_tiles=lat_rows // tc, tiles_per_seq=seq // tc)
    prev_map = lambda c: (lambda i: (jnp.maximum(i * per - 1, 0), c))
    next_map = lambda c: (lambda i: (jnp.minimum((i + 1) * per, nh - 1), c))
    vec = lambda a: a.reshape(1, ch)
    return pl.pallas_call(
        kern,
        grid=(n_rows // tc,),
        in_specs=[
            pl.BlockSpec((HALO, ch), prev_map(0)),
            pl.BlockSpec((HALO, ch), prev_map(1)),
            pl.BlockSpec((tc, ch), lambda i: (i, 0)),
            pl.BlockSpec((tc, ch), lambda i: (i, 1)),
            pl.BlockSpec((HALO, ch), next_map(0)),
            pl.BlockSpec((HALO, ch), next_map(1)),
            pl.BlockSpec((CONV_K, ch), lambda i: (0, 0)),
            pl.BlockSpec((1, ch), lambda i: (0, 0)),
            pl.BlockSpec((1, ch), lambda i: (0, 0)),
            pl.BlockSpec((1, ch), lambda i: (0, 0)),
        ],
        out_specs=pl.BlockSpec((tc, ch), lambda i: (i, 0)),
        out_shape=jax.ShapeDtypeStruct((n_rows, ch), BF16),
        scratch_shapes=[pltpu.VMEM((tc + 2 * HALO, ch), F32), pltpu.VMEM((tc, ch), F32)],
        compiler_params=_cparams(("arbitrary",), 16),
    )(u, u, u, u, u, u, w_dw, vec(b_dw), vec(ln_g), vec(ln_b))


def _na_bias_tables(rows):
    groups = rows // NA_ROWS
    reps = [0, min(1, groups - 1), groups - 1]
    out = []
    for g in reps:
        start = int(np.clip(NA_ROWS * g - NA_KH // 2, 0, rows - NA_KEY_ROWS))
        per_row = []
        for i in range(NA_ROWS):
            r = NA_ROWS * g + i
            sr = int(np.clip(r - NA_KH // 2, 0, rows - NA_KH))
            per_row.append((sr - start, sr - r + NA_KH - 1))
        out.append(per_row)
    return out


def _na_bias(rpb, rows):
    nh = rpb.shape[0]
    ndr, ndc = 2 * NA_KH - 1, 2 * NA_KW - 1
    period = 2 * GRID_W - 1
    pad = GRID_W - NA_KW
    vp = jnp.pad(rpb.astype(F32), ((0, 0), (0, 0), (pad, period - ndc - pad)))
    hank = jnp.tile(vp, (1, 1, GRID_W + 1))[:, :, :GRID_W * (period + 1)]
    hank = hank.reshape(nh, ndr, GRID_W, period + 1)[..., :GRID_W]
    toe = hank[:, :, ::-1, :]
    c = np.arange(GRID_W)[:, None]
    j = np.arange(GRID_W)[None, :]
    ws = np.clip(c - NA_KW // 2, 0, GRID_W - NA_KW)
    col_ok = (j >= ws) & (j < ws + NA_KW)
    toe = jnp.where(col_ok[None, None], toe, NEG_INF)
    flat = jnp.transpose(toe, (0, 2, 1, 3)).reshape(nh, GRID_W, ndr * GRID_W)
    nk = NA_KEY_ROWS * GRID_W
    lpad = NA_ROWS * GRID_W
    total = 2 * nk

    def padded(shift):
        return jnp.pad(flat, ((0, 0), (0, 0), (lpad - shift, total - flat.shape[2] - lpad + shift)))

    tab = jnp.stack([padded(0), padded(GRID_W)])
    tab = tab.reshape(2, nh, GRID_W, total // LANES, LANES).transpose(0, 1, 3, 2, 4)
    tables = _na_bias_tables(rows)
    mask = np.full((len(tables), NA_ROWS, nk), NEG_INF, np.float32)
    dvals = []
    for cls, per_row in enumerate(tables):
        dvals.append(per_row[0][1] - per_row[0][0])
        for i, (off, lo) in enumerate(per_row):
            assert lo - off == dvals[-1] - i and -NA_ROWS <= lo - off < NA_ROWS
            mask[cls, i, off * GRID_W:(off + NA_KH) * GRID_W] = 0.0
    assert len({d % 2 for d in dvals}) == 1
    assert len(tables) < 3 or rows < 3 * NA_ROWS or all(off == i for i, (off, _) in enumerate(tables[1]))
    return tab * LOG2_E, jnp.asarray(mask * LOG2_E), tuple(dvals)


def _softmax_pv(s_parts, v_parts):
    m = functools.reduce(jnp.maximum, [jnp.max(s, axis=-1, keepdims=True) for s in s_parts])
    acc, l = None, None
    for s, v in zip(s_parts, v_parts):
        p = jnp.exp2(s - m)
        ps = jnp.sum(p, axis=-1, keepdims=True)
        pv = _dot(p.astype(BF16), v)
        l = ps if l is None else l + ps
        acc = pv if acc is None else acc + pv
    return acc / l


def _na_head_edge(s_raw, s_ctx, vs, vc, tab_ref, mask_ref, a, dcls, parity, pad_blocks):
    kw = s_raw[0].shape[1]
    bpk = kw // LANES
    s_parts = []
    for m, s in enumerate(s_raw):
        row_blocks = []
        for i in range(NA_ROWS):
            copy = (parity - i) % 2
            first = (dcls - i - copy + 2 * pad_blocks) // 2 + m * bpk
            bias = jnp.concatenate([tab_ref[copy, a, first + t] for t in range(bpk)], axis=1)
            bias = bias + mask_ref[i:i + 1, m * kw:(m + 1) * kw]
            row_blocks.append(s[i * GRID_W:(i + 1) * GRID_W, :] + bias)
        s_parts.append(jnp.concatenate(row_blocks, axis=0))
    return _softmax_pv(s_parts + [s_ctx], vs + [vc])


def _na_head_interior(s_raw, s_ctx, vs, vc, tab_ref, mask_ref, a, d0, pad_blocks):
    kw = s_raw[0].shape[1]
    bpk = kw // LANES
    nblk = len(s_raw) * bpk
    p_rows, pc_rows, l_rows = [], [], []
    for i in range(NA_ROWS):
        rs = slice(i * GRID_W, (i + 1) * GRID_W)
        lo_lane, hi_lane = i * GRID_W, (i + NA_KH) * GRID_W
        b_lo, b_hi = lo_lane // LANES, -(-hi_lane // LANES)
        copy = (d0 - i) % 2
        first = (d0 - i - copy + 2 * pad_blocks) // 2
        blocks = []
        for b in range(b_lo, b_hi):
            ls = slice((b % bpk) * LANES, (b % bpk + 1) * LANES)
            sb = s_raw[b // bpk][rs, ls] + tab_ref[copy, a, first + b]
            if b * LANES < lo_lane or (b + 1) * LANES > hi_lane:
                sb = sb + mask_ref[i:i + 1, b * LANES:(b + 1) * LANES]
            blocks.append(sb)
        sw = jnp.concatenate(blocks, axis=1)
        sc = s_ctx[rs]
        m = jnp.maximum(jnp.max(sw, axis=-1, keepdims=True), jnp.max(sc, axis=-1, keepdims=True))
        pw = jnp.exp2(sw - m)
        pc = jnp.exp2(sc - m)
        l_rows.append(jnp.sum(pw, axis=-1, keepdims=True) + jnp.sum(pc, axis=-1, keepdims=True))
        pieces = [jnp.zeros((GRID_W, b_lo * LANES), BF16), pw.astype(BF16),
                  jnp.zeros((GRID_W, (nblk - b_hi) * LANES), BF16)]
        p_rows.append(jnp.concatenate([x for x in pieces if x.shape[1]], axis=1))
        pc_rows.append(pc.astype(BF16))
    p = jnp.concatenate(p_rows, axis=0)
    acc = _dot(jnp.concatenate(pc_rows, axis=0), vc)
    for m, v in enumerate(vs):
        acc = acc + _dot(p[:, m * kw:(m + 1) * kw], v)
    return acc / jnp.concatenate(l_rows, axis=0)


def _na_kernel(q_ref, k0, k1, k2, k3, v0, v1, v2, v3, kc_ref, vc_ref, tab_ref, mask_ref, o_ref, *,
               groups, dvals):
    g = pl.program_id(1)
    is_edge = jnp.logical_or(g == 0, g == groups - 1)
    pad_blocks = NA_ROWS * GRID_W // LANES

    def step(interior):
        q2 = q_ref[...] * (NA_HD ** -0.5 * LOG2_E)
        lane = lax.broadcasted_iota(jnp.int32, q2.shape, 1)
        ks = [k[...].astype(BF16) for k in (k0, k1, k2, k3)]
        vs = [v[...].astype(BF16) for v in (v0, v1, v2, v3)]
        kc = kc_ref[...].astype(BF16)
        vc = vc_ref[...].astype(BF16)
        nheads = q2.shape[1] // NA_HD

        def scores(a):
            sel = jnp.logical_and(lane >= a * NA_HD, lane < (a + 1) * NA_HD)
            qa = jnp.where(sel, q2, 0.0).astype(BF16)
            return [_dot_nt(qa, k) for k in ks], _dot_nt(qa, kc)

        pending = [scores(a) for a in range(min(NA_AHEAD, nheads))]
        out = None
        for a in range(nheads):
            if a + NA_AHEAD < nheads:
                pending.append(scores(a + NA_AHEAD))
            s_raw, s_ctx = pending[a]
            if interior:
                o = _na_head_interior(s_raw, s_ctx, vs, vc, tab_ref, mask_ref, a, dvals[1], pad_blocks)
            else:
                dcls = jnp.where(g == 0, dvals[0], dvals[2])
                o = _na_head_edge(s_raw, s_ctx, vs, vc, tab_ref, mask_ref, a, dcls, dvals[0], pad_blocks)
            out = o if out is None else jnp.where(lane >= a * NA_HD, o, out)
        o_ref[...] = out.astype(o_ref.dtype)

    @pl.when(is_edge)
    def _():
        step(False)

    @pl.when(jnp.logical_not(is_edge))
    def _():
        step(True)


def _na_latent(u, bias, head_base, nbatch, seq, lat_rows, off_na):
    tab, mask, dvals = bias
    rows = seq // GRID_W
    groups = rows // NA_ROWS
    nq = NA_ROWS * GRID_W
    kblk = ROW_TILE
    nkb = NA_KEY_ROWS * GRID_W // kblk
    assert nkb == 4
    hp = NA_STEP_HEADS
    wid = hp * NA_HD
    qcol = off_na // wid
    hsteps = NA_HEADS // hp
    kcol, vcol = qcol + hsteps, qcol + 2 * hsteps
    assert off_na % wid == 0 and NA_HEADS % hp == 0 and head_base % hp == 0
    kb_per_batch = seq // kblk
    kb_per_grow = GRID_W * NA_ROWS // kblk
    lat_kb = lat_rows // kblk

    def kmap(col, m):
        def f(h, g, b):
            st = jnp.clip(g * kb_per_grow - (NA_KH // 2) * GRID_W // kblk, 0, kb_per_batch - nkb)
            return (b * kb_per_batch + st + m, col + h)
        return f

    def mask_map(h, g, b):
        return (jnp.where(g == 0, 0, jnp.where(g == groups - 1, 2, 1)), 0, 0)

    in_specs = [pl.BlockSpec((nq, wid), lambda h, g, b: (b * groups + g, qcol + h))]
    in_specs += [pl.BlockSpec((kblk, wid), kmap(kcol, m)) for m in range(nkb)]
    in_specs += [pl.BlockSpec((kblk, wid), kmap(vcol, m)) for m in range(nkb)]
    in_specs += [pl.BlockSpec((ROW_TILE, wid), lambda h, g, b: (lat_kb + b, kcol + h)),
                 pl.BlockSpec((ROW_TILE, wid), lambda h, g, b: (lat_kb + b, vcol + h)),
                 pl.BlockSpec((2, hp) + tab.shape[2:], lambda h, g, b: (0, head_base // hp + h, 0, 0, 0)),
                 pl.BlockSpec((None,) + mask.shape[1:], mask_map)]
    return pl.pallas_call(
        functools.partial(_na_kernel, groups=groups, dvals=dvals),
        grid=(hsteps, groups, nbatch),
        in_specs=in_specs,
        out_specs=pl.BlockSpec((nq, wid), lambda h, g, b: (b * groups + g, h)),
        out_shape=jax.ShapeDtypeStruct((lat_rows, NA_HEADS * NA_HD), BF16),
        compiler_params=_cparams(("arbitrary", "arbitrary", "arbitrary"), 48),
    )(*([u] * 11), tab, mask)


def _ctx_attn_kernel(q_ref, k_ref, v_ref, o_ref):
    q2 = q_ref[...] * (NA_HD ** -0.5 * LOG2_E)
    lane = lax.broadcasted_iota(jnp.int32, q2.shape, 1)
    k = k_ref[...].astype(BF16)
    v = v_ref[...].astype(BF16)
    outs = []
    for a in range(2):
        sel = (lane < NA_HD) if a == 0 else (lane >= NA_HD)
        qa = jnp.where(sel, q2, 0.0).astype(BF16)
        outs.append(_softmax_pv([_dot_nt(qa, k)], [v]))
    o_ref[...] = jnp.where(lane < NA_HD, outs[0], outs[1]).astype(o_ref.dtype)


def _ctx_attn(u, nbatch, ctx_len, lat_rows, off_na):
    assert ctx_len == ROW_TILE
    qcol = off_na // LANES
    heads2 = NA_HEADS * NA_HD // LANES
    base = lat_rows // ROW_TILE
    spec = lambda col: pl.BlockSpec((ROW_TILE, LANES), lambda b, h: (base + b, col + h))
    return pl.pallas_call(
        _ctx_attn_kernel,
        grid=(nbatch, heads2),
        in_specs=[spec(qcol), spec(qcol + heads2), spec(qcol + 2 * heads2)],
        out_specs=pl.BlockSpec((ROW_TILE, LANES), lambda b, h: (b, h)),
        out_shape=jax.ShapeDtypeStruct((nbatch * ctx_len, NA_HEADS * NA_HD), BF16),
        compiler_params=_cparams(("arbitrary", "arbitrary"), 16),
    )(u, u, u)


def _hg_level_map(rev):
    size = HG_BLOCK // 2
    t = np.arange(size)[:, None]
    s = np.arange(size)[None, :]
    x = t ^ s
    lvl = np.where(x > 0, np.frexp(np.maximum(x, 1))[1] - 1, -1)
    causal = (s < t) if not rev else (s > t)
    out = np.where(causal, lvl, -1)
    out = np.where(t == s, int(np.log2(size)), out)
    return out.astype(np.int32)


def _hg_tri(rev):
    t = np.arange(HG_BLOCK)[:, None]
    s = np.arange(HG_BLOCK)[None, :]
    return ((s <= t) if not rev else (s >= t)).astype(np.float32)


def _hg_anchor(b3, m, rev):
    nv = b3.shape[0]
    if m >= SUBLANES:
        w = m // SUBLANES
        b4 = b3.reshape(nv // (2 * w), 2 * w, SUBLANES, LANES)
        a = b4[:, w:w + 1, 0:1, :] if rev else b4[:, w - 1:w, SUBLANES - 1:SUBLANES, :]
        return jnp.broadcast_to(a, b4.shape).reshape(b3.shape)
    sub = lax.broadcasted_iota(jnp.int32, b3.shape, 1)
    out = None
    for g in range(SUBLANES // (2 * m)):
        idx = g * 2 * m + (m if rev else m - 1)
        a = jnp.broadcast_to(b3[:, idx:idx + 1, :], b3.shape)
        out = a if out is None else jnp.where(sub >= g * 2 * m, a, out)
    return out


def _neg_abs(x):
    bits = lax.bitcast_convert_type(x, jnp.uint32) | jnp.uint32(0x80000000)
    return lax.bitcast_convert_type(bits, F32)


def _hg_pick(q3, k3, m, rev):
    nv = q3.shape[0]
    if m >= SUBLANES:
        w = m // SUBLANES
        shape4 = (nv // (2 * w), 2 * w, SUBLANES, LANES)
        q4, k4 = q3.reshape(shape4), k3.reshape(shape4)
        lower, upper = (q4, k4) if rev else (k4, q4)
        return jnp.concatenate([lower[:, :w], upper[:, w:]], axis=1).reshape(q3.shape)
    upper_rows = (lax.broadcasted_iota(jnp.int32, q3.shape, 1) & m) != 0
    return jnp.where(upper_rows, k3 if rev else q3, q3 if rev else k3)


def _hg_gates(q, z, alog, clog, oml, tri):
    q = _silu(q)
    t = jnp.exp(-jnp.abs(z))
    lsig = jnp.minimum(z, 0.0) - jnp.log(1.0 + t)
    cc = clog + lsig
    logf = jnp.maximum(alog, cc) + jnp.log(1.0 + jnp.exp(-jnp.abs(alog - cc)))
    kk = oml * jnp.where(z >= 0.0, t, 1.0) / (1.0 + t)

    hi = logf.astype(BF16)
    r1 = logf - hi.astype(F32)
    mid = r1.astype(BF16)
    lo = (r1 - mid.astype(F32)).astype(BF16)
    b = (_dot(tri, hi) + _dot(tri, mid) + _dot(tri, lo)) * LOG2_E
    return q, kk, b


def _hg_mix(q, kk, b, v, lv, st, rev):
    n = q.shape[0]
    half = n // 2
    nlev = int(np.log2(n))
    shape3 = (n // SUBLANES, SUBLANES, LANES)
    b3, q3, k3 = b.reshape(shape3), q.reshape(shape3), kk.reshape(shape3)
    halves = (slice(0, half), slice(half, n))
    qb, kb = q.astype(BF16), kk.astype(BF16)
    acc = [jnp.where(lv == nlev - 1, _dot_nt(qb[hs], kb[hs]), 0.0) for hs in halves]
    for lev in range(nlev - 1):
        m = 1 << lev
        e = jnp.exp2(_neg_abs(b3 - _hg_anchor(b3, m, rev)))
        w = (_hg_pick(q3, k3, m, rev) * e).reshape(n, LANES).astype(BF16)
        acc = [jnp.where(lv == lev, _dot_nt(w[hs], w[hs]), a) for hs, a in zip(halves, acc)]
    first, second = (halves[1], halves[0]) if rev else halves
    e = jnp.exp2(_neg_abs(b - (b[half:half + 1, :] if rev else b[half - 1:half, :])))
    top = _dot_nt((q[second] * e[second]).astype(BF16), (kk[first] * e[first]).astype(BF16))

    vb = v.astype(BF16)
    a0, a1 = acc[0].astype(BF16), acc[1].astype(BF16)
    tb = top.astype(BF16)
    if rev:
        o_lo = _dot(jnp.concatenate([a0, tb], axis=1), vb)
        o_hi = _dot(a1, vb[halves[1]])
    else:
        o_lo = _dot(a0, vb[halves[0]])
        o_hi = _dot(jnp.concatenate([tb, a1], axis=1), vb)
    b_last = b[0:1, :] if rev else b[n - 1:n, :]
    qh = (q * jnp.exp2(b)).astype(BF16)
    o = jnp.concatenate([o_lo, o_hi], axis=0) + _dot_nt(qh, st.astype(BF16))
    kh = (kk * jnp.exp2(b_last - b)).astype(BF16)
    st_new = st * jnp.exp2(b_last) + _dot_tn(vb, kh)
    return o, st_new


def _hg_gates_ahead(q_ref, z_ref, al_ref, cl_ref, om_ref, tri):
    def gates(h):
        hs = slice(h * HG_DK, (h + 1) * HG_DK)
        return _hg_gates(q_ref[:, hs], z_ref[:, hs], al_ref[:, hs], cl_ref[:, hs], om_ref[:, hs], tri)

    pending = [gates(h) for h in range(min(HG_AHEAD, HG_HEADS))]
    for h in range(HG_HEADS):
        if h + HG_AHEAD < HG_HEADS:
            pending.append(gates(h + HG_AHEAD))
        yield pending[h]


def _hg_fwd_kernel(q_ref, v_ref, z_ref, al_ref, cl_ref, om_ref, tri_ref, lv_ref, o_ref, st_ref):
    @pl.when(pl.program_id(1) == 0)
    def _():
        st_ref[...] = jnp.zeros_like(st_ref)

    lv = lv_ref[...]
    gates = _hg_gates_ahead(q_ref, z_ref, al_ref, cl_ref, om_ref, tri_ref[...])
    for h in range(HG_HEADS):
        hs = slice(h * HG_DK, (h + 1) * HG_DK)
        o, st = _hg_mix(*next(gates), v_ref[:, hs], lv, st_ref[h], False)
        o_ref[:, hs] = o
        st_ref[h] = st


def _hg_bwd_kernel(q_ref, v_ref, z_ref, g_ref, of_ref, al_ref, cl_ref, om_ref, ng_ref, tri_ref, lv_ref,
                   o_ref, st_ref):
    @pl.when(pl.program_id(1) == 0)
    def _():
        st_ref[...] = jnp.zeros_like(st_ref)

    lv = lv_ref[...]
    gates = _hg_gates_ahead(q_ref, z_ref, al_ref, cl_ref, om_ref, tri_ref[...])
    for h in range(HG_HEADS):
        hs = slice(h * HG_DK, (h + 1) * HG_DK)
        o, st = _hg_mix(*next(gates), v_ref[:, hs], lv, st_ref[h], True)
        st_ref[h] = st
        t = of_ref[:, hs] + o
        y = t * lax.rsqrt(jnp.mean(t * t, axis=-1, keepdims=True) + EPS)
        o_ref[:, hs] = (y * ng_ref[:, hs] * _silu(g_ref[:, hs])).astype(o_ref.dtype)


def _hgrn(u, lb, norm_g, nbatch, seq, ctx_len, lat_rows, off_hg):
    assert ctx_len == HG_BLOCK
    n = u.shape[0]
    hd = HG_HEADS * HG_DK
    col = off_hg // hd
    per = seq // HG_BLOCK
    lat_blocks = lat_rows // HG_BLOCK
    lbf = lb.astype(F32)
    alog, clog, oml = jnp.log(lbf), jnp.log1p(-lbf), 1.0 - lbf

    def fmap(c):
        return lambda b, j: (jnp.where(j == 0, lat_blocks + b, b * per + j - 1), c)

    def bmap(c):
        return lambda b, j: (jnp.where(j == 0, lat_blocks + b, b * per + per - j), c)

    const = lambda shape: pl.BlockSpec(shape, lambda b, j: (0, 0))
    grid = (nbatch, per + 1)
    vec = lambda a: a.reshape(1, hd)
    o_f = pl.pallas_call(
        _hg_fwd_kernel,
        grid=grid,
        in_specs=[pl.BlockSpec((HG_BLOCK, hd), fmap(col)), pl.BlockSpec((HG_BLOCK, hd), fmap(col + 1)),
                  pl.BlockSpec((HG_BLOCK, hd), fmap(col + 2)),
                  const((1, hd)), const((1, hd)), const((1, hd)),
                  const((HG_BLOCK, HG_BLOCK)), const((HG_BLOCK // 2, HG_BLOCK // 2))],
        out_specs=pl.BlockSpec((HG_BLOCK, hd), fmap(0)),
        out_shape=jax.ShapeDtypeStruct((n, hd), F32),
        scratch_shapes=[pltpu.VMEM((HG_HEADS, HG_DK, HG_DK), F32)],
        compiler_params=_cparams(("arbitrary", "arbitrary"), 32),
    )(u, u, u, vec(alog[0]), vec(clog[0]), vec(oml[0]),
      jnp.asarray(_hg_tri(False), BF16), jnp.asarray(_hg_level_map(False)))
    return pl.pallas_call(
        _hg_bwd_kernel,
        grid=grid,
        in_specs=[pl.BlockSpec((HG_BLOCK, hd), bmap(col)), pl.BlockSpec((HG_BLOCK, hd), bmap(col + 1)),
                  pl.BlockSpec((HG_BLOCK, hd), bmap(col + 3)), pl.BlockSpec((HG_BLOCK, hd), bmap(col + 4)),
                  pl.BlockSpec((HG_BLOCK, hd), bmap(0)),
                  const((1, hd)), const((1, hd)), const((1, hd)), const((1, hd)),
                  const((HG_BLOCK, HG_BLOCK)), const((HG_BLOCK // 2, HG_BLOCK // 2))],
        out_specs=pl.BlockSpec((HG_BLOCK, hd), bmap(0)),
        out_shape=jax.ShapeDtypeStruct((n, hd), BF16),
        scratch_shapes=[pltpu.VMEM((HG_HEADS, HG_DK, HG_DK), F32)],
        compiler_params=_cparams(("arbitrary", "arbitrary"), 32),
    )(u, u, u, u, o_f, vec(alog[1]), vec(clog[1]), vec(oml[1]), vec(norm_g.astype(F32)),
      jnp.asarray(_hg_tri(True), BF16), jnp.asarray(_hg_level_map(True)))


def _out_kernel(x_ref, cv_ref, na_ref, hg_ref, w_ref, ga_ref, g2_ref, sh2_ref, s2_ref, wrh_ref, wrl_ref, br_ref,
                xo_ref, h_ref, rt_ref, *, tiles_per_batch, nbatch):
    r = jnp.minimum(pl.program_id(0) // tiles_per_batch, nbatch)
    c0 = cv_ref.shape[1]
    c1 = c0 + na_ref.shape[1]
    mix = (_dot(cv_ref[...], w_ref[0:c0, :]) + _dot(na_ref[...], w_ref[c0:c1, :])
           + _dot(hg_ref[...], w_ref[c1:, :]))
    xn = x_ref[...] + ga_ref[pl.ds(r, 1), :] * mix
    xo_ref[...] = xn
    h = _rms_mod(xn, g2_ref[...], s2_ref[pl.ds(r, 1), :], sh2_ref[pl.ds(r, 1), :])
    h_ref[...] = h.astype(h_ref.dtype)
    h_hi, h_lo = _split_bf16(h)
    logits = (_dot(h_hi, wrh_ref[...]) + (_dot(h_lo, wrh_ref[...]) + _dot(h_hi, wrl_ref[...]))
              + br_ref[...])
    rt_ref[...] = _route_rows(logits)


def _route_rows(lg):
    lane = lax.broadcasted_iota(jnp.int32, lg.shape, 1)
    big = jnp.int32(2 ** 30)
    low = jnp.float32(-3e38)

    def first_max(vals, mask):
        m = jnp.max(vals, axis=-1, keepdims=True)
        idx = jnp.min(jnp.where(jnp.logical_and(vals == m, mask), lane, big), axis=-1, keepdims=True)
        return m, idx

    gmask = lane < N_GROUPS
    gl = jnp.where(gmask, lg, low)
    gm, grp = first_max(gl, gmask)
    p_grp = 1.0 / jnp.sum(jnp.where(gmask, jnp.exp(gl - gm), 0.0), axis=-1, keepdims=True)
    lo = N_GROUPS + grp * EXP_PER_GROUP
    emask = jnp.logical_and(lane >= lo, lane < lo + EXP_PER_GROUP)
    el = jnp.where(emask, lg, low)
    m1, i1 = first_max(el, emask)
    emask2 = jnp.logical_and(emask, lane != i1)
    el2 = jnp.where(emask2, lg, low)
    m2, i2 = first_max(el2, emask2)
    t = jnp.exp(m2 - m1)
    w1 = p_grp / (1.0 + t)
    w2 = p_grp * t / (1.0 + t)
    e1 = (i1 - N_GROUPS).astype(F32)
    e2 = (i2 - N_GROUPS).astype(F32)
    return jnp.where(lane == 0, e1, jnp.where(lane == 1, e2, jnp.where(lane == 2, w1,
                     jnp.where(lane == 3, w2, 0.0))))


def _out_proj(x, conv, na, hg, w_bf16, l, mod, g_ffn, w_router, b_router, n_rows, nbatch, lat_rows):
    d = x.shape[1]
    w_router_hi, w_router_lo = _split_bf16(w_router)
    tm = _pick_tile(OUT_TM, lat_rows // nbatch, n_rows)
    kern = functools.partial(_out_kernel, tiles_per_batch=lat_rows // nbatch // tm, nbatch=nbatch)
    row = lambda w: pl.BlockSpec((tm, w), lambda i: (i, 0))
    const = lambda shape: pl.BlockSpec(shape, lambda i: (0, 0))
    modc = lambda c: pl.BlockSpec((None, SUBLANES, d), lambda i: (l, 0, c))
    return pl.pallas_call(
        kern,
        grid=(n_rows // tm,),
        in_specs=[row(d), row(conv.shape[1]), row(na.shape[1]), row(hg.shape[1]),
                  pl.BlockSpec((None, d, d), lambda i: (l, 0, 0)),
                  modc(2),
                  const((1, d)),
                  modc(3),
                  modc(4),
                  const((d, ROUTER_PAD)), const((d, ROUTER_PAD)), const((1, ROUTER_PAD))],
        out_specs=[row(d), row(d), row(ROUTER_PAD)],
        out_shape=[jax.ShapeDtypeStruct((n_rows, d), F32), jax.ShapeDtypeStruct((n_rows, d), BF16),
                   jax.ShapeDtypeStruct((n_rows, ROUTER_PAD), F32)],
        compiler_params=_cparams(("arbitrary",), 56),
    )(x, conv, na, hg, w_bf16, mod, g_ffn.reshape(1, d), mod, mod, w_router_hi, w_router_lo, b_router)


def _moe_kernel(be_ref, nu_ref, xs_ref, w1_ref, w3_ref, w2_ref, sw_ref, o_ref, w1b, w3b, w2b):
    i = pl.program_id(0)
    e = be_ref[i]
    prev = be_ref[jnp.maximum(i - 1, 0)]

    @pl.when(jnp.logical_or(i == 0, e != prev))
    def _():
        w1b[...] = w1_ref[...].astype(BF16)
        w3b[...] = w3_ref[...].astype(BF16)
        w2b[...] = w2_ref[...].astype(BF16)

    @pl.when(i < nu_ref[0])
    def _():
        x = xs_ref[...]
        a = (_silu(_dot(x, w1b[...])) * _dot(x, w3b[...])).astype(BF16)
        o_ref[...] = (_dot(a, w2b[...]) * sw_ref[...]).astype(o_ref.dtype)

    @pl.when(i >= nu_ref[0])
    def _():
        o_ref[...] = jnp.zeros_like(o_ref)


def _moe_experts(xs, slot_w, blk_e, nused, w1, w3, w2, l):
    p, d = xs.shape
    de = w1.shape[3]
    bm = MOE_BM
    grid_spec = pltpu.PrefetchScalarGridSpec(
        num_scalar_prefetch=2,
        grid=(p // bm,),
        in_specs=[pl.BlockSpec((bm, d), lambda i, be, nu: (i, 0)),
                  pl.BlockSpec((None, None, d, de), lambda i, be, nu: (l, be[i], 0, 0)),
                  pl.BlockSpec((None, None, d, de), lambda i, be, nu: (l, be[i], 0, 0)),
                  pl.BlockSpec((None, None, de, d), lambda i, be, nu: (l, be[i], 0, 0)),
                  pl.BlockSpec((bm, 1), lambda i, be, nu: (i, 0))],
        out_specs=pl.BlockSpec((bm, d), lambda i, be, nu: (i, 0)),
        scratch_shapes=[pltpu.VMEM((d, de), BF16), pltpu.VMEM((d, de), BF16), pltpu.VMEM((de, d), BF16)],
    )
    return pl.pallas_call(
        _moe_kernel,
        grid_spec=grid_spec,
        out_shape=jax.ShapeDtypeStruct((p, d), BF16),
        compiler_params=_cparams(("arbitrary",), 48),
    )(blk_e, nused, xs, w1, w3, w2, slot_w.reshape(p, 1))


def _rows(a, idx):
    return a.at[idx].get(mode="promise_in_bounds")


def _route_meta(route, n):
    i32 = jnp.int32
    eid = route[:, 0:TOP_K].astype(i32).reshape(-1)
    wt = route[:, TOP_K:2 * TOP_K].reshape(-1)
    a = n * TOP_K
    bm = MOE_BM
    nblk = -(-a // bm) + N_EXPERTS
    p = nblk * bm
    experts = jnp.arange(N_EXPERTS, dtype=i32)[None, :]
    ja = jnp.arange(a, dtype=i32)
    se, order, wsort = lax.sort((eid, ja, wt), num_keys=1, is_stable=True)
    cnt = jnp.sum((eid[:, None] == experts).astype(i32), axis=0)
    pcnt = (cnt + bm - 1) // bm * bm
    pend = jnp.cumsum(pcnt)
    pstart = pend - pcnt
    end = jnp.cumsum(cnt)
    start = end - cnt
    off = pstart - start
    d_off = off - jnp.concatenate([jnp.zeros((1,), i32), off[:-1]])
    dst_sorted = ja + jnp.sum(jnp.where(ja[:, None] >= start[None, :], d_off[None, :], 0), axis=1)
    _, pos = lax.sort((order, dst_sorted), num_keys=1)
    jp = jnp.arange(p, dtype=i32)
    in_or_after = jp[:, None] >= pstart[None, :]
    src = jp - jnp.sum(jnp.where(in_or_after, d_off[None, :], 0), axis=1)
    valid = src < jnp.sum(jnp.where(in_or_after, cnt[None, :], 0), axis=1)
    src = jnp.where(valid, src, jp % a)
    slot_tok = _rows(order, src) // TOP_K
    slot_w = jnp.where(valid, _rows(wsort, src), 0.0)
    jb = jnp.arange(nblk, dtype=i32) * bm
    blk_e = jnp.minimum(jnp.sum((jb[:, None] >= pend[None, :]).astype(i32), axis=1), N_EXPERTS - 1)
    nused = (pend[-1:] // bm).astype(i32)
    return slot_tok, slot_w, pos.reshape(n, TOP_K), blk_e, nused


def _combine_kernel(x_ref, y0_ref, y1_ref, ga_ref, gf_ref, o_ref, *, tiles_per_batch, nbatch, final):
    r = jnp.minimum(pl.program_id(0) // tiles_per_batch, nbatch)
    xn = x_ref[...] + ga_ref[pl.ds(r, 1), :] * (y0_ref[...].astype(F32) + y1_ref[...].astype(F32))
    if final:
        xn = xn * lax.rsqrt(jnp.mean(xn * xn, axis=-1, keepdims=True) + EPS) * gf_ref[...]
    o_ref[...] = xn


def _combine(x, y0, y1, mod, l, g_final, n_rows, nbatch, lat_rows, final):
    d = x.shape[1]
    tm = _pick_tile(OUT_TM, lat_rows // nbatch, n_rows)
    kern = functools.partial(_combine_kernel, tiles_per_batch=lat_rows // nbatch // tm, nbatch=nbatch,
                             final=final)
    row = pl.BlockSpec((tm, d), lambda i: (i, 0))
    return pl.pallas_call(
        kern,
        grid=(n_rows // tm,),
        in_specs=[row, row, row, pl.BlockSpec((None, SUBLANES, d), lambda i: (l, 0, 5)),
                  pl.BlockSpec((1, d), lambda i: (0, 0))],
        out_specs=row,
        out_shape=jax.ShapeDtypeStruct((n_rows, d), F32),
        compiler_params=_cparams(("arbitrary",), 40),
    )(x, y0, y1, mod, g_final.reshape(1, d))


def kernel(x, c, ctx, c_ctx, w_ada, b_ada, g_mix, g_ffn, w_in, conv_w, conv_b, conv_ln_g, conv_ln_b,
           na_rpb, hgrn_lb, hgrn_norm_g, w_out, w_router_group, b_router_group, w_router_expert,
           b_router_expert, w_exp_gate, w_exp_up, w_exp_down, g_final):
    nb, seq, d = x.shape
    ctx_len = ctx.shape[1]
    depth = w_ada.shape[0]
    lat_rows = nb * seq
    n_all = lat_rows + nb * ctx_len
    conv_ch = conv_w.shape[2]
    off_na = 2 * conv_ch
    off_hg = off_na + 3 * NA_HEADS * NA_HD
    rows = seq // GRID_W
    assert nb < SUBLANES and rows % NA_ROWS == 0 and rows >= NA_KEY_ROWS

    lbs = jnp.cumsum(jax.nn.softmax(hgrn_lb.astype(F32), axis=0), axis=0)
    lbs = lbs - lbs[:1]

    cond = jnp.concatenate([c, c_ctx[None, :], jnp.zeros((SUBLANES - nb - 1, d), F32)], axis=0)
    mod = _ada_mod(cond, w_ada, b_ada)

    xs = jnp.concatenate([x.reshape(lat_rows, d), ctx.reshape(nb * ctx_len, d)], axis=0)
    w_in_b = w_in.astype(BF16)
    w_out_b = w_out.astype(BF16)
    na_bias = _na_bias(na_rpb.reshape((depth * NA_HEADS,) + na_rpb.shape[2:]), rows)
    for l in range(depth):
        with_ctx = l < depth - 1
        n_act = n_all if with_ctx else lat_rows
        u = _norm_in(xs, g_mix[l], mod, w_in_b, l, nb, lat_rows)

        conv = _conv_module(u, conv_w[l], conv_b[l], conv_ln_g[l], conv_ln_b[l], n_act, lat_rows, seq)
        na = _na_latent(u, na_bias, l * NA_HEADS, nb, seq, lat_rows, off_na)
        if with_ctx:
            na = jnp.concatenate([na, _ctx_attn(u, nb, ctx_len, lat_rows, off_na)], axis=0)
        hg = _hgrn(u, lbs[l], hgrn_norm_g[l], nb, seq, ctx_len, lat_rows, off_hg)

        w_router = jnp.concatenate(
            [w_router_group[l], w_router_expert[l],
             jnp.zeros((d, ROUTER_PAD - N_GROUPS - N_EXPERTS), F32)], axis=1)
        b_router = jnp.concatenate(
            [b_router_group[l], b_router_expert[l],
             jnp.zeros((ROUTER_PAD - N_GROUPS - N_EXPERTS,), F32)]).reshape(1, ROUTER_PAD)
        x_mid, h, route = _out_proj(xs, conv, na, hg, w_out_b, l, mod, g_ffn[l],
                                    w_router, b_router, n_act, nb, lat_rows)

        slot_tok, slot_w, pos, blk_e, nused = _route_meta(route, n_act)
        ys = _moe_experts(_rows(h, slot_tok), slot_w, blk_e, nused, w_exp_gate, w_exp_up, w_exp_down, l)
        y0 = _rows(ys, pos[:, 0])
        y1 = _rows(ys, pos[:, 1])
        xs = _combine(x_mid, y0, y1, mod, l, g_final, n_act, nb, lat_rows, final=not with_ctx)
    return xs.reshape(nb, seq, d)
```

```python
import functools

import numpy as np
import jax
import jax.numpy as jnp
from jax import lax
from jax.experimental import pallas as pl
from jax.experimental.pallas import tpu as pltpu

F32 = jnp.float32
BF16 = jnp.bfloat16

EPS = 1e-6
NEG_INF = -1e30
LOG2_E = 1.4426950408889634

GRID_W = 64
CONV_K = 31
NA_HEADS = 16
NA_HD = 64
NA_KH = 8
NA_KW = 16
HG_HEADS = 4
HG_DK = 128
N_GROUPS = 4
EXP_PER_GROUP = 8
N_EXPERTS = N_GROUPS * EXP_PER_GROUP
TOP_K = 2

LANES = 128
SUBLANES = 8

ROW_TILE = 256
ADA_TN = 1024
IN_TM = 1024
IN_TN = 512
OUT_TM = 512
CONV_ROWS = 64
NA_STEP_HEADS = 4
NA_AHEAD = 4
NA_ROWS = 8
NA_KEY_ROWS = 16
HG_BLOCK = 256
HG_AHEAD = 2
MOE_BM = 512
ROUTER_PAD = LANES
HALO = 16


def _pick_tile(pref, *extents):
    t = pref
    while t > ROW_TILE and any(e % t for e in extents):
        t //= 2
    assert all(e % t == 0 for e in extents)
    return t


def _cparams(sem, vmem_mb):
    return pltpu.CompilerParams(dimension_semantics=sem, vmem_limit_bytes=vmem_mb * 1024 * 1024)


def _dot(a, b):
    return jnp.dot(a, b, preferred_element_type=F32)


def _dot_nt(a, b):
    return lax.dot_general(a, b, (((1,), (1,)), ((), ())), preferred_element_type=F32)


def _dot_tn(a, b):
    return lax.dot_general(a, b, (((0,), (0,)), ((), ())), preferred_element_type=F32)


def _sigmoid(x):
    return 1.0 / (1.0 + jnp.exp(-x))


def _silu(x):
    return x * _sigmoid(x)


def _split_bf16(x):
    hi = x.astype(BF16)
    return hi, (x - hi.astype(F32)).astype(BF16)


def _ada_kernel(c_ref, w_ref, b_ref, o_ref):
    s_hi, s_lo = _split_bf16(_silu(c_ref[...]))
    w_hi, w_lo = _split_bf16(w_ref[0])
    o_ref[0] = _dot(s_hi, w_hi) + (_dot(s_lo, w_hi) + _dot(s_hi, w_lo)) + b_ref[0]


def _ada_mod(cond, w_ada, b_ada):
    depth, d, n = w_ada.shape
    tn = ADA_TN
    return pl.pallas_call(
        _ada_kernel,
        grid=(depth, n // tn),
        in_specs=[
            pl.BlockSpec((SUBLANES, d), lambda l, j: (0, 0)),
            pl.BlockSpec((1, d, tn), lambda l, j: (l, 0, j)),
            pl.BlockSpec((1, 1, tn), lambda l, j: (l, 0, j)),
        ],
        out_specs=pl.BlockSpec((1, SUBLANES, tn), lambda l, j: (l, 0, j)),
        out_shape=jax.ShapeDtypeStruct((depth, SUBLANES, n), F32),
        compiler_params=_cparams(("arbitrary", "arbitrary"), 40),
    )(cond, w_ada, b_ada.reshape(depth, 1, n))


def _rms_mod(x, g, scale, shift):
    y = x * lax.rsqrt(jnp.mean(x * x, axis=-1, keepdims=True) + EPS)
    return (y * g) * (1.0 + scale) + shift


def _norm_in_kernel(x_ref, g_ref, sh_ref, sc_ref, w_ref, o_ref, h_ref, *, tiles_per_batch, nbatch):
    i = pl.program_id(0)

    @pl.when(pl.program_id(1) == 0)
    def _():
        r = jnp.minimum(i // tiles_per_batch, nbatch)
        h = _rms_mod(x_ref[...], g_ref[...], sc_ref[pl.ds(r, 1), :], sh_ref[pl.ds(r, 1), :])
        h_ref[...] = h.astype(BF16)

    o_ref[...] = _dot(h_ref[...], w_ref[...])


def _norm_in(x, g, mod, w_bf16, l, nbatch, lat_rows):
    n, d = x.shape
    nout = w_bf16.shape[2]
    tm, tn = _pick_tile(IN_TM, lat_rows // nbatch, n), IN_TN
    kern = functools.partial(_norm_in_kernel, tiles_per_batch=lat_rows // nbatch // tm, nbatch=nbatch)
    return pl.pallas_call(
        kern,
        grid=(n // tm, nout // tn),
        in_specs=[
            pl.BlockSpec((tm, d), lambda i, j: (i, 0)),
            pl.BlockSpec((1, d), lambda i, j: (0, 0)),
            pl.BlockSpec((None, SUBLANES, d), lambda i, j: (l, 0, 0)),
            pl.BlockSpec((None, SUBLANES, d), lambda i, j: (l, 0, 1)),
            pl.BlockSpec((None, d, tn), lambda i, j: (l, 0, j)),
        ],
        out_specs=pl.BlockSpec((tm, tn), lambda i, j: (i, j)),
        out_shape=jax.ShapeDtypeStruct((n, nout), F32),
        scratch_shapes=[pltpu.VMEM((tm, d), BF16)],
        compiler_params=_cparams(("arbitrary", "arbitrary"), 48),
    )(x, g.reshape(1, d), mod, mod, w_bf16)


def _conv_kernel(ap_ref, gp_ref, a_ref, gt_ref, an_ref, gn_ref, w_ref, b_ref, lg_ref, lb_ref,
                 o_ref, buf_ref, acc_ref, *, lat_tiles, tiles_per_seq):
    i = pl.program_id(0)
    tc, ch = a_ref.shape
    is_lat = i < lat_tiles
    pos = i % tiles_per_seq
    first = jnp.logical_or(jnp.logical_not(is_lat), pos == 0)
    last = jnp.logical_or(jnp.logical_not(is_lat), pos == tiles_per_seq - 1)

    buf_ref[0:HALO] = jnp.where(first, 0.0, ap_ref[...] * _sigmoid(gp_ref[...]))
    buf_ref[HALO:HALO + tc] = a_ref[...] * _sigmoid(gt_ref[...])
    buf_ref[HALO + tc:2 * HALO + tc] = jnp.where(last, 0.0, an_ref[...] * _sigmoid(gn_ref[...]))

    rows = CONV_ROWS
    base = HALO - CONV_K // 2
    for c in range(ch // LANES):
        cs = slice(c * LANES, (c + 1) * LANES)
        for r in range(tc // rows):
            acc = None
            for res in range(SUBLANES):
                y = None
                for k in range(CONV_K):
                    if (base + k) % SUBLANES != res:
                        continue
                    lo = r * rows + (base + k) // SUBLANES * SUBLANES
                    term = w_ref[k:k + 1, cs] * buf_ref[lo:lo + rows + SUBLANES, cs]
                    y = term if y is None else y + term
                if y is not None:
                    y = y[res:res + rows]
                    acc = y if acc is None else acc + y
            acc_ref[r * rows:(r + 1) * rows, cs] = acc

    h = acc_ref[...] + b_ref[...]
    mu = jnp.mean(h, axis=-1, keepdims=True)
    var = jnp.mean(jnp.square(h - mu), axis=-1, keepdims=True)
    y = (h - mu) * lax.rsqrt(var + EPS) * lg_ref[...] + lb_ref[...]
    o_ref[...] = _silu(y).astype(o_ref.dtype)


def _conv_module(u, w_dw, b_dw, ln_g, ln_b, n_rows, lat_rows, seq):
    ch = w_dw.shape[1]
    tc = ROW_TILE
    per = tc // HALO
    nh = u.shape[0] // HALO
    kern = functools.partial(_conv_kernel, lat_tiles=lat_rows // tc, tiles_per_seq=seq // tc)
    prev_map = lambda c: (lambda i: (jnp.maximum(i * per - 1, 0), c))
    next_map = lambda c: (lambda i: (jnp.minimum((i + 1) * per, nh - 1), c))
    vec = lambda a: a.reshape(1, ch)
    return pl.pallas_call(
        kern,
        grid=(n_rows // tc,),
        in_specs=[
            pl.BlockSpec((HALO, ch), prev_map(0)),
            pl.BlockSpec((HALO, ch), prev_map(1)),
            pl.BlockSpec((tc, ch), lambda i: (i, 0)),
            pl.BlockSpec((tc, ch), lambda i: (i, 1)),
            pl.BlockSpec((HALO, ch), next_map(0)),
            pl.BlockSpec((HALO, ch), next_map(1)),
            pl.BlockSpec((CONV_K, ch), lambda i: (0, 0)),
            pl.BlockSpec((1, ch), lambda i: (0, 0)),
            pl.BlockSpec((1, ch), lambda i: (0, 0)),
            pl.BlockSpec((1, ch), lambda i: (0, 0)),
        ],
        out_specs=pl.BlockSpec((tc, ch), lambda i: (i, 0)),
        out_shape=jax.ShapeDtypeStruct((n_rows, ch), BF16),
        scratch_shapes=[pltpu.VMEM((tc + 2 * HALO, ch), F32), pltpu.VMEM((tc, ch), F32)],
        compiler_params=_cparams(("arbitrary",), 16),
    )(u, u, u, u, u, u, w_dw, vec(b_dw), vec(ln_g), vec(ln_b))


def _na_bias_tables(rows):
    groups = rows // NA_ROWS
    reps = [0, min(1, groups - 1), groups - 1]
    out = []
    for g in reps:
        start = int(np.clip(NA_ROWS * g - NA_KH // 2, 0, rows - NA_KEY_ROWS))
        per_row = []
        for i in range(NA_ROWS):
            r = NA_ROWS * g + i
            sr = int(np.clip(r - NA_KH // 2, 0, rows - NA_KH))
            per_row.append((sr - start, sr - r + NA_KH - 1))
        out.append(per_row)
    return out


def _na_bias(rpb, rows):
    nh = rpb.shape[0]
    ndr, ndc = 2 * NA_KH - 1, 2 * NA_KW - 1
    period = 2 * GRID_W - 1
    pad = GRID_W - NA_KW
    vp = jnp.pad(rpb.astype(F32), ((0, 0), (0, 0), (pad, period - ndc - pad)))
    hank = jnp.tile(vp, (1, 1, GRID_W + 1))[:, :, :GRID_W * (period + 1)]
    hank = hank.reshape(nh, ndr, GRID_W, period + 1)[..., :GRID_W]
    toe = hank[:, :, ::-1, :]
    c = np.arange(GRID_W)[:, None]
    j = np.arange(GRID_W)[None, :]
    ws = np.clip(c - NA_KW // 2, 0, GRID_W - NA_KW)
    col_ok = (j >= ws) & (j < ws + NA_KW)
    toe = jnp.where(col_ok[None, None], toe, NEG_INF)
    flat = jnp.transpose(toe, (0, 2, 1, 3)).reshape(nh, GRID_W, ndr * GRID_W)
    nk = NA_KEY_ROWS * GRID_W
    lpad = NA_ROWS * GRID_W
    total = 2 * nk

    def padded(shift):
        return jnp.pad(flat, ((0, 0), (0, 0), (lpad - shift, total - flat.shape[2] - lpad + shift)))

    tab = jnp.stack([padded(0), padded(GRID_W)])
    tab = tab.reshape(2, nh, GRID_W, total // LANES, LANES).transpose(0, 1, 3, 2, 4)
    tables = _na_bias_tables(rows)
    mask = np.full((len(tables), NA_ROWS, nk), NEG_INF, np.float32)
    dvals = []
    for cls, per_row in enumerate(tables):
        dvals.append(per_row[0][1] - per_row[0][0])
        for i, (off, lo) in enumerate(per_row):
            assert lo - off == dvals[-1] - i and -NA_ROWS <= lo - off < NA_ROWS
            mask[cls, i, off * GRID_W:(off + NA_KH) * GRID_W] = 0.0
    assert len({d % 2 for d in dvals}) == 1
    assert len(tables) < 3 or rows < 3 * NA_ROWS or all(off == i for i, (off, _) in enumerate(tables[1]))
    return tab * LOG2_E, jnp.asarray(mask * LOG2_E), tuple(dvals)


def _softmax_pv(s_parts, v_parts):
    m = functools.reduce(jnp.maximum, [jnp.max(s, axis=-1, keepdims=True) for s in s_parts])
    acc, l = None, None
    for s, v in zip(s_parts, v_parts):
        p = jnp.exp2(s - m)
        ps = jnp.sum(p, axis=-1, keepdims=True)
        pv = _dot(p.astype(BF16), v)
        l = ps if l is None else l + ps
        acc = pv if acc is None else acc + pv
    return acc / l


def _na_head_edge(s_raw, s_ctx, vs, vc, tab_ref, mask_ref, a, dcls, parity, pad_blocks):
    kw = s_raw[0].shape[1]
    bpk = kw // LANES
    s_parts = []
    for m, s in enumerate(s_raw):
        row_blocks = []
        for i in range(NA_ROWS):
            copy = (parity - i) % 2
            first = (dcls - i - copy + 2 * pad_blocks) // 2 + m * bpk
            bias = jnp.concatenate([tab_ref[copy, a, first + t] for t in range(bpk)], axis=1)
            bias = bias + mask_ref[i:i + 1, m * kw:(m + 1) * kw]
            row_blocks.append(s[i * GRID_W:(i + 1) * GRID_W, :] + bias)
        s_parts.append(jnp.concatenate(row_blocks, axis=0))
    return _softmax_pv(s_parts + [s_ctx], vs + [vc])


def _na_head_interior(s_raw, s_ctx, vs, vc, tab_ref, mask_ref, a, d0, pad_blocks):
    kw = s_raw[0].shape[1]
    bpk = kw // LANES
    nblk = len(s_raw) * bpk
    p_rows, pc_rows, l_rows = [], [], []
    for i in range(NA_ROWS):
        rs = slice(i * GRID_W, (i + 1) * GRID_W)
        lo_lane, hi_lane = i * GRID_W, (i + NA_KH) * GRID_W
        b_lo, b_hi = lo_lane // LANES, -(-hi_lane // LANES)
        copy = (d0 - i) % 2
        first = (d0 - i - copy + 2 * pad_blocks) // 2
        blocks = []
        for b in range(b_lo, b_hi):
            ls = slice((b % bpk) * LANES, (b % bpk + 1) * LANES)
            sb = s_raw[b // bpk][rs, ls] + tab_ref[copy, a, first + b]
            if b * LANES < lo_lane or (b + 1) * LANES > hi_lane:
                sb = sb + mask_ref[i:i + 1, b * LANES:(b + 1) * LANES]
            blocks.append(sb)
        sw = jnp.concatenate(blocks, axis=1)
        sc = s_ctx[rs]
        m = jnp.maximum(jnp.max(sw, axis=-1, keepdims=True), jnp.max(sc, axis=-1, keepdims=True))
        pw = jnp.exp2(sw - m)
        pc = jnp.exp2(sc - m)
        l_rows.append(jnp.sum(pw, axis=-1, keepdims=True) + jnp.sum(pc, axis=-1, keepdims=True))
        pieces = [jnp.zeros((GRID_W, b_lo * LANES), BF16), pw.astype(BF16),
                  jnp.zeros((GRID_W, (nblk - b_hi) * LANES), BF16)]
        p_rows.append(jnp.concatenate([x for x in pieces if x.shape[1]], axis=1))
        pc_rows.append(pc.astype(BF16))
    p = jnp.concatenate(p_rows, axis=0)
    acc = _dot(jnp.concatenate(pc_rows, axis=0), vc)
    for m, v in enumerate(vs):
        acc = acc + _dot(p[:, m * kw:(m + 1) * kw], v)
    return acc / jnp.concatenate(l_rows, axis=0)


def _na_kernel(q_ref, k0, k1, k2, k3, v0, v1, v2, v3, kc_ref, vc_ref, tab_ref, mask_ref, o_ref, *,
               groups, dvals):
    g = pl.program_id(1)
    is_edge = jnp.logical_or(g == 0, g == groups - 1)
    pad_blocks = NA_ROWS * GRID_W // LANES

    def step(interior):
        q2 = q_ref[...] * (NA_HD ** -0.5 * LOG2_E)
        lane = lax.broadcasted_iota(jnp.int32, q2.shape, 1)
        ks = [k[...].astype(BF16) for k in (k0, k1, k2, k3)]
        vs = [v[...].astype(BF16) for v in (v0, v1, v2, v3)]
        kc = kc_ref[...].astype(BF16)
        vc = vc_ref[...].astype(BF16)
        nheads = q2.shape[1] // NA_HD

        def scores(a):
            sel = jnp.logical_and(lane >= a * NA_HD, lane < (a + 1) * NA_HD)
            qa = jnp.where(sel, q2, 0.0).astype(BF16)
            return [_dot_nt(qa, k) for k in ks], _dot_nt(qa, kc)

        pending = [scores(a) for a in range(min(NA_AHEAD, nheads))]
        out = None
        for a in range(nheads):
            if a + NA_AHEAD < nheads:
                pending.append(scores(a + NA_AHEAD))
            s_raw, s_ctx = pending[a]
            if interior:
                o = _na_head_interior(s_raw, s_ctx, vs, vc, tab_ref, mask_ref, a, dvals[1], pad_blocks)
            else:
                dcls = jnp.where(g == 0, dvals[0], dvals[2])
                o = _na_head_edge(s_raw, s_ctx, vs, vc, tab_ref, mask_ref, a, dcls, dvals[0], pad_blocks)
            out = o if out is None else jnp.where(lane >= a * NA_HD, o, out)
        o_ref[...] = out.astype(o_ref.dtype)

    @pl.when(is_edge)
    def _():
        step(False)

    @pl.when(jnp.logical_not(is_edge))
    def _():
        step(True)


def _na_latent(u, bias, head_base, nbatch, seq, lat_rows, off_na):
    tab, mask, dvals = bias
    rows = seq // GRID_W
    groups = rows // NA_ROWS
    nq = NA_ROWS * GRID_W
    kblk = ROW_TILE
    nkb = NA_KEY_ROWS * GRID_W // kblk
    assert nkb == 4
    hp = NA_STEP_HEADS
    wid = hp * NA_HD
    qcol = off_na // wid
    hsteps = NA_HEADS // hp
    kcol, vcol = qcol + hsteps, qcol + 2 * hsteps
    assert off_na % wid == 0 and NA_HEADS % hp == 0 and head_base % hp == 0
    kb_per_batch = seq // kblk
    kb_per_grow = GRID_W * NA_ROWS // kblk
    lat_kb = lat_rows // kblk

    def kmap(col, m):
        def f(h, g, b):
            st = jnp.clip(g * kb_per_grow - (NA_KH // 2) * GRID_W // kblk, 0, kb_per_batch - nkb)
            return (b * kb_per_batch + st + m, col + h)
        return f

    def mask_map(h, g, b):
        return (jnp.where(g == 0, 0, jnp.where(g == groups - 1, 2, 1)), 0, 0)

    in_specs = [pl.BlockSpec((nq, wid), lambda h, g, b: (b * groups + g, qcol + h))]
    in_specs += [pl.BlockSpec((kblk, wid), kmap(kcol, m)) for m in range(nkb)]
    in_specs += [pl.BlockSpec((kblk, wid), kmap(vcol, m)) for m in range(nkb)]
    in_specs += [pl.BlockSpec((ROW_TILE, wid), lambda h, g, b: (lat_kb + b, kcol + h)),
                 pl.BlockSpec((ROW_TILE, wid), lambda h, g, b: (lat_kb + b, vcol + h)),
                 pl.BlockSpec((2, hp) + tab.shape[2:], lambda h, g, b: (0, head_base // hp + h, 0, 0, 0)),
                 pl.BlockSpec((None,) + mask.shape[1:], mask_map)]
    return pl.pallas_call(
        functools.partial(_na_kernel, groups=groups, dvals=dvals),
        grid=(hsteps, groups, nbatch),
        in_specs=in_specs,
        out_specs=pl.BlockSpec((nq, wid), lambda h, g, b: (b * groups + g, h)),
        out_shape=jax.ShapeDtypeStruct((lat_rows, NA_HEADS * NA_HD), BF16),
        compiler_params=_cparams(("arbitrary", "arbitrary", "arbitrary"), 48),
    )(*([u] * 11), tab, mask)


def _ctx_attn_kernel(q_ref, k_ref, v_ref, o_ref):
    q2 = q_ref[...] * (NA_HD ** -0.5 * LOG2_E)
    lane = lax.broadcasted_iota(jnp.int32, q2.shape, 1)
    k = k_ref[...].astype(BF16)
    v = v_ref[...].astype(BF16)
    outs = []
    for a in range(2):
        sel = (lane < NA_HD) if a == 0 else (lane >= NA_HD)
        qa = jnp.where(sel, q2, 0.0).astype(BF16)
        outs.append(_softmax_pv([_dot_nt(qa, k)], [v]))
    o_ref[...] = jnp.where(lane < NA_HD, outs[0], outs[1]).astype(o_ref.dtype)


def _ctx_attn(u, nbatch, ctx_len, lat_rows, off_na):
    assert ctx_len == ROW_TILE
    qcol = off_na // LANES
    heads2 = NA_HEADS * NA_HD // LANES
    base = lat_rows // ROW_TILE
    spec = lambda col: pl.BlockSpec((ROW_TILE, LANES), lambda b, h: (base + b, col + h))
    return pl.pallas_call(
        _ctx_attn_kernel,
        grid=(nbatch, heads2),
        in_specs=[spec(qcol), spec(qcol + heads2), spec(qcol + 2 * heads2)],
        out_specs=pl.BlockSpec((ROW_TILE, LANES), lambda b, h: (b, h)),
        out_shape=jax.ShapeDtypeStruct((nbatch * ctx_len, NA_HEADS * NA_HD), BF16),
        compiler_params=_cparams(("arbitrary", "arbitrary"), 16),
    )(u, u, u)


def _hg_level_map(rev):
    size = HG_BLOCK // 2
    t = np.arange(size)[:, None]
    s = np.arange(size)[None, :]
    x = t ^ s
    lvl = np.where(x > 0, np.frexp(np.maximum(x, 1))[1] - 1, -1)
    causal = (s < t) if not rev else (s > t)
    out = np.where(causal, lvl, -1)
    out = np.where(t == s, int(np.log2(size)), out)
    return out.astype(np.int32)


def _hg_tri(rev):
    t = np.arange(HG_BLOCK)[:, None]
    s = np.arange(HG_BLOCK)[None, :]
    return ((s <= t) if not rev else (s >= t)).astype(np.float32)


def _hg_anchor(b3, m, rev):
    nv = b3.shape[0]
    if m >= SUBLANES:
        w = m // SUBLANES
        b4 = b3.reshape(nv // (2 * w), 2 * w, SUBLANES, LANES)
        a = b4[:, w:w + 1, 0:1, :] if rev else b4[:, w - 1:w, SUBLANES - 1:SUBLANES, :]
        return jnp.broadcast_to(a, b4.shape).reshape(b3.shape)
    sub = lax.broadcasted_iota(jnp.int32, b3.shape, 1)
    out = None
    for g in range(SUBLANES // (2 * m)):
        idx = g * 2 * m + (m if rev else m - 1)
        a = jnp.broadcast_to(b3[:, idx:idx + 1, :], b3.shape)
        out = a if out is None else jnp.where(sub >= g * 2 * m, a, out)
    return out


def _neg_abs(x):
    bits = lax.bitcast_convert_type(x, jnp.uint32) | jnp.uint32(0x80000000)
    return lax.bitcast_convert_type(bits, F32)


def _hg_pick(q3, k3, m, rev):
    nv = q3.shape[0]
    if m >= SUBLANES:
        w = m // SUBLANES
        shape4 = (nv // (2 * w), 2 * w, SUBLANES, LANES)
        q4, k4 = q3.reshape(shape4), k3.reshape(shape4)
        lower, upper = (q4, k4) if rev else (k4, q4)
        return jnp.concatenate([lower[:, :w], upper[:, w:]], axis=1).reshape(q3.shape)
    upper_rows = (lax.broadcasted_iota(jnp.int32, q3.shape, 1) & m) != 0
    return jnp.where(upper_rows, k3 if rev else q3, q3 if rev else k3)


def _hg_gates(q, z, alog, clog, oml, tri):
    q = _silu(q)
    t = jnp.exp(-jnp.abs(z))
    lsig = jnp.minimum(z, 0.0) - jnp.log(1.0 + t)
    cc = clog + lsig
    logf = jnp.maximum(alog, cc) + jnp.log(1.0 + jnp.exp(-jnp.abs(alog - cc)))
    kk = oml * jnp.where(z >= 0.0, t, 1.0) / (1.0 + t)

    hi = logf.astype(BF16)
    r1 = logf - hi.astype(F32)
    mid = r1.astype(BF16)
    lo = (r1 - mid.astype(F32)).astype(BF16)
    b = (_dot(tri, hi) + _dot(tri, mid) + _dot(tri, lo)) * LOG2_E
    return q, kk, b


def _hg_mix(q, kk, b, v, lv, st, rev):
    n = q.shape[0]
    half = n // 2
    nlev = int(np.log2(n))
    shape3 = (n // SUBLANES, SUBLANES, LANES)
    b3, q3, k3 = b.reshape(shape3), q.reshape(shape3), kk.reshape(shape3)
    halves = (slice(0, half), slice(half, n))
    qb, kb = q.astype(BF16), kk.astype(BF16)
    acc = [jnp.where(lv == nlev - 1, _dot_nt(qb[hs], kb[hs]), 0.0) for hs in halves]
    for lev in range(nlev - 1):
        m = 1 << lev
        e = jnp.exp2(_neg_abs(b3 - _hg_anchor(b3, m, rev)))
        w = (_hg_pick(q3, k3, m, rev) * e).reshape(n, LANES).astype(BF16)
        acc = [jnp.where(lv == lev, _dot_nt(w[hs], w[hs]), a) for hs, a in zip(halves, acc)]
    first, second = (halves[1], halves[0]) if rev else halves
    e = jnp.exp2(_neg_abs(b - (b[half:half + 1, :] if rev else b[half - 1:half, :])))
    top = _dot_nt((q[second] * e[second]).astype(BF16), (kk[first] * e[first]).astype(BF16))

    vb = v.astype(BF16)
    a0, a1 = acc[0].astype(BF16), acc[1].astype(BF16)
    tb = top.astype(BF16)
    if rev:
        o_lo = _dot(jnp.concatenate([a0, tb], axis=1), vb)
        o_hi = _dot(a1, vb[halves[1]])
    else:
        o_lo = _dot(a0, vb[halves[0]])
        o_hi = _dot(jnp.concatenate([tb, a1], axis=1), vb)
    b_last = b[0:1, :] if rev else b[n - 1:n, :]
    qh = (q * jnp.exp2(b)).astype(BF16)
    o = jnp.concatenate([o_lo, o_hi], axis=0) + _dot_nt(qh, st.astype(BF16))
    kh = (kk * jnp.exp2(b_last - b)).astype(BF16)
    st_new = st * jnp.exp2(b_last) + _dot_tn(vb, kh)
    return o, st_new


def _hg_gates_ahead(q_ref, z_ref, al_ref, cl_ref, om_ref, tri):
    def gates(h):
        hs = slice(h * HG_DK, (h + 1) * HG_DK)
        return _hg_gates(q_ref[:, hs], z_ref[:, hs], al_ref[:, hs], cl_ref[:, hs], om_ref[:, hs], tri)

    pending = [gates(h) for h in range(min(HG_AHEAD, HG_HEADS))]
    for h in range(HG_HEADS):
        if h + HG_AHEAD < HG_HEADS:
            pending.append(gates(h + HG_AHEAD))
        yield pending[h]


def _hg_fwd_kernel(q_ref, v_ref, z_ref, al_ref, cl_ref, om_ref, tri_ref, lv_ref, o_ref, st_ref):
    @pl.when(pl.program_id(1) == 0)
    def _():
        st_ref[...] = jnp.zeros_like(st_ref)

    lv = lv_ref[...]
    gates = _hg_gates_ahead(q_ref, z_ref, al_ref, cl_ref, om_ref, tri_ref[...])
    for h in range(HG_HEADS):
        hs = slice(h * HG_DK, (h + 1) * HG_DK)
        o, st = _hg_mix(*next(gates), v_ref[:, hs], lv, st_ref[h], False)
        o_ref[:, hs] = o
        st_ref[h] = st


def _hg_bwd_kernel(q_ref, v_ref, z_ref, g_ref, of_ref, al_ref, cl_ref, om_ref, ng_ref, tri_ref, lv_ref,
                   o_ref, st_ref):
    @pl.when(pl.program_id(1) == 0)
    def _():
        st_ref[...] = jnp.zeros_like(st_ref)

    lv = lv_ref[...]
    gates = _hg_gates_ahead(q_ref, z_ref, al_ref, cl_ref, om_ref, tri_ref[...])
    for h in range(HG_HEADS):
        hs = slice(h * HG_DK, (h + 1) * HG_DK)
        o, st = _hg_mix(*next(gates), v_ref[:, hs], lv, st_ref[h], True)
        st_ref[h] = st
        t = of_ref[:, hs] + o
        y = t * lax.rsqrt(jnp.mean(t * t, axis=-1, keepdims=True) + EPS)
        o_ref[:, hs] = (y * ng_ref[:, hs] * _silu(g_ref[:, hs])).astype(o_ref.dtype)


def _hgrn(u, lb, norm_g, nbatch, seq, ctx_len, lat_rows, off_hg):
    assert ctx_len == HG_BLOCK
    n = u.shape[0]
    hd = HG_HEADS * HG_DK
    col = off_hg // hd
    per = seq // HG_BLOCK
    lat_blocks = lat_rows // HG_BLOCK
    lbf = lb.astype(F32)
    alog, clog, oml = jnp.log(lbf), jnp.log1p(-lbf), 1.0 - lbf

    def fmap(c):
        return lambda b, j: (jnp.where(j == 0, lat_blocks + b, b * per + j - 1), c)

    def bmap(c):
        return lambda b, j: (jnp.where(j == 0, lat_blocks + b, b * per + per - j), c)

    const = lambda shape: pl.BlockSpec(shape, lambda b, j: (0, 0))
    grid = (nbatch, per + 1)
    vec = lambda a: a.reshape(1, hd)
    o_f = pl.pallas_call(
        _hg_fwd_kernel,
        grid=grid,
        in_specs=[pl.BlockSpec((HG_BLOCK, hd), fmap(col)), pl.BlockSpec((HG_BLOCK, hd), fmap(col + 1)),
                  pl.BlockSpec((HG_BLOCK, hd), fmap(col + 2)),
                  const((1, hd)), const((1, hd)), const((1, hd)),
                  const((HG_BLOCK, HG_BLOCK)), const((HG_BLOCK // 2, HG_BLOCK // 2))],
        out_specs=pl.BlockSpec((HG_BLOCK, hd), fmap(0)),
        out_shape=jax.ShapeDtypeStruct((n, hd), F32),
        scratch_shapes=[pltpu.VMEM((HG_HEADS, HG_DK, HG_DK), F32)],
        compiler_params=_cparams(("arbitrary", "arbitrary"), 32),
    )(u, u, u, vec(alog[0]), vec(clog[0]), vec(oml[0]),
      jnp.asarray(_hg_tri(False), BF16), jnp.asarray(_hg_level_map(False)))
    return pl.pallas_call(
        _hg_bwd_kernel,
        grid=grid,
        in_specs=[pl.BlockSpec((HG_BLOCK, hd), bmap(col)), pl.BlockSpec((HG_BLOCK, hd), bmap(col + 1)),
                  pl.BlockSpec((HG_BLOCK, hd), bmap(col + 3)), pl.BlockSpec((HG_BLOCK, hd), bmap(col + 4)),
                  pl.BlockSpec((HG_BLOCK, hd), bmap(0)),
                  const((1, hd)), const((1, hd)), const((1, hd)), const((1, hd)),
                  const((HG_BLOCK, HG_BLOCK)), const((HG_BLOCK // 2, HG_BLOCK // 2))],
        out_specs=pl.BlockSpec((HG_BLOCK, hd), bmap(0)),
        out_shape=jax.ShapeDtypeStruct((n, hd), BF16),
        scratch_shapes=[pltpu.VMEM((HG_HEADS, HG_DK, HG_DK), F32)],
        compiler_params=_cparams(("arbitrary", "arbitrary"), 32),
    )(u, u, u, u, o_f, vec(alog[1]), vec(clog[1]), vec(oml[1]), vec(norm_g.astype(F32)),
      jnp.asarray(_hg_tri(True), BF16), jnp.asarray(_hg_level_map(True)))


def _out_kernel(x_ref, cv_ref, na_ref, hg_ref, w_ref, ga_ref, g2_ref, sh2_ref, s2_ref, wrh_ref, wrl_ref, br_ref,
                xo_ref, h_ref, rt_ref, *, tiles_per_batch, nbatch):
    r = jnp.minimum(pl.program_id(0) // tiles_per_batch, nbatch)
    c0 = cv_ref.shape[1]
    c1 = c0 + na_ref.shape[1]
    mix = (_dot(cv_ref[...], w_ref[0:c0, :]) + _dot(na_ref[...], w_ref[c0:c1, :])
           + _dot(hg_ref[...], w_ref[c1:, :]))
    xn = x_ref[...] + ga_ref[pl.ds(r, 1), :] * mix
    xo_ref[...] = xn
    h = _rms_mod(xn, g2_ref[...], s2_ref[pl.ds(r, 1), :], sh2_ref[pl.ds(r, 1), :])
    h_ref[...] = h.astype(h_ref.dtype)
    h_hi, h_lo = _split_bf16(h)
    logits = (_dot(h_hi, wrh_ref[...]) + (_dot(h_lo, wrh_ref[...]) + _dot(h_hi, wrl_ref[...]))
              + br_ref[...])
    rt_ref[...] = _route_rows(logits)


def _route_rows(lg):
    lane = lax.broadcasted_iota(jnp.int32, lg.shape, 1)
    big = jnp.int32(2 ** 30)
    low = jnp.float32(-3e38)

    def first_max(vals, mask):
        m = jnp.max(vals, axis=-1, keepdims=True)
        idx = jnp.min(jnp.where(jnp.logical_and(vals == m, mask), lane, big), axis=-1, keepdims=True)
        return m, idx

    gmask = lane < N_GROUPS
    gl = jnp.where(gmask, lg, low)
    gm, grp = first_max(gl, gmask)
    p_grp = 1.0 / jnp.sum(jnp.where(gmask, jnp.exp(gl - gm), 0.0), axis=-1, keepdims=True)
    lo = N_GROUPS + grp * EXP_PER_GROUP
    emask = jnp.logical_and(lane >= lo, lane < lo + EXP_PER_GROUP)
    el = jnp.where(emask, lg, low)
    m1, i1 = first_max(el, emask)
    emask2 = jnp.logical_and(emask, lane != i1)
    el2 = jnp.where(emask2, lg, low)
    m2, i2 = first_max(el2, emask2)
    t = jnp.exp(m2 - m1)
    w1 = p_grp / (1.0 + t)
    w2 = p_grp * t / (1.0 + t)
    e1 = (i1 - N_GROUPS).astype(F32)
    e2 = (i2 - N_GROUPS).astype(F32)
    return jnp.where(lane == 0, e1, jnp.where(lane == 1, e2, jnp.where(lane == 2, w1,
                     jnp.where(lane == 3, w2, 0.0))))


def _out_proj(x, conv, na, hg, w_bf16, l, mod, g_ffn, w_router, b_router, n_rows, nbatch, lat_rows):
    d = x.shape[1]
    w_router_hi, w_router_lo = _split_bf16(w_router)
    tm = _pick_tile(OUT_TM, lat_rows // nbatch, n_rows)
    kern = functools.partial(_out_kernel, tiles_per_batch=lat_rows // nbatch // tm, nbatch=nbatch)
    row = lambda w: pl.BlockSpec((tm, w), lambda i: (i, 0))
    const = lambda shape: pl.BlockSpec(shape, lambda i: (0, 0))
    modc = lambda c: pl.BlockSpec((None, SUBLANES, d), lambda i: (l, 0, c))
    return pl.pallas_call(
        kern,
        grid=(n_rows // tm,),
        in_specs=[row(d), row(conv.shape[1]), row(na.shape[1]), row(hg.shape[1]),
                  pl.BlockSpec((None, d, d), lambda i: (l, 0, 0)),
                  modc(2),
                  const((1, d)),
                  modc(3),
                  modc(4),
                  const((d, ROUTER_PAD)), const((d, ROUTER_PAD)), const((1, ROUTER_PAD))],
        out_specs=[row(d), row(d), row(ROUTER_PAD)],
        out_shape=[jax.ShapeDtypeStruct((n_rows, d), F32), jax.ShapeDtypeStruct((n_rows, d), BF16),
                   jax.ShapeDtypeStruct((n_rows, ROUTER_PAD), F32)],
        compiler_params=_cparams(("arbitrary",), 56),
    )(x, conv, na, hg, w_bf16, mod, g_ffn.reshape(1, d), mod, mod, w_router_hi, w_router_lo, b_router)


def _moe_kernel(be_ref, nu_ref, xs_ref, w1_ref, w3_ref, w2_ref, sw_ref, o_ref, w1b, w3b, w2b):
    i = pl.program_id(0)
    e = be_ref[i]
    prev = be_ref[jnp.maximum(i - 1, 0)]

    @pl.when(jnp.logical_or(i == 0, e != prev))
    def _():
        w1b[...] = w1_ref[...].astype(BF16)
        w3b[...] = w3_ref[...].astype(BF16)
        w2b[...] = w2_ref[...].astype(BF16)

    @pl.when(i < nu_ref[0])
    def _():
        x = xs_ref[...]
        a = (_silu(_dot(x, w1b[...])) * _dot(x, w3b[...])).astype(BF16)
        o_ref[...] = (_dot(a, w2b[...]) * sw_ref[...]).astype(o_ref.dtype)

    @pl.when(i >= nu_ref[0])
    def _():
        o_ref[...] = jnp.zeros_like(o_ref)


def _moe_experts(xs, slot_w, blk_e, nused, w1, w3, w2, l):
    p, d = xs.shape
    de = w1.shape[3]
    bm = MOE_BM
    grid_spec = pltpu.PrefetchScalarGridSpec(
        num_scalar_prefetch=2,
        grid=(p // bm,),
        in_specs=[pl.BlockSpec((bm, d), lambda i, be, nu: (i, 0)),
                  pl.BlockSpec((None, None, d, de), lambda i, be, nu: (l, be[i], 0, 0)),
                  pl.BlockSpec((None, None, d, de), lambda i, be, nu: (l, be[i], 0, 0)),
                  pl.BlockSpec((None, None, de, d), lambda i, be, nu: (l, be[i], 0, 0)),
                  pl.BlockSpec((bm, 1), lambda i, be, nu: (i, 0))],
        out_specs=pl.BlockSpec((bm, d), lambda i, be, nu: (i, 0)),
        scratch_shapes=[pltpu.VMEM((d, de), BF16), pltpu.VMEM((d, de), BF16), pltpu.VMEM((de, d), BF16)],
    )
    return pl.pallas_call(
        _moe_kernel,
        grid_spec=grid_spec,
        out_shape=jax.ShapeDtypeStruct((p, d), BF16),
        compiler_params=_cparams(("arbitrary",), 48),
    )(blk_e, nused, xs, w1, w3, w2, slot_w.reshape(p, 1))


def _rows(a, idx):
    return a.at[idx].get(mode="promise_in_bounds")


def _route_meta(route, n):
    i32 = jnp.int32
    eid = route[:, 0:TOP_K].astype(i32).reshape(-1)
    wt = route[:, TOP_K:2 * TOP_K].reshape(-1)
    a = n * TOP_K
    bm = MOE_BM
    nblk = -(-a // bm) + N_EXPERTS
    p = nblk * bm
    experts = jnp.arange(N_EXPERTS, dtype=i32)[None, :]
    ja = jnp.arange(a, dtype=i32)
    se, order, wsort = lax.sort((eid, ja, wt), num_keys=1, is_stable=True)
    cnt = jnp.sum((eid[:, None] == experts).astype(i32), axis=0)
    pcnt = (cnt + bm - 1) // bm * bm
    pend = jnp.cumsum(pcnt)
    pstart = pend - pcnt
    end = jnp.cumsum(cnt)
    start = end - cnt
    off = pstart - start
    d_off = off - jnp.concatenate([jnp.zeros((1,), i32), off[:-1]])
    dst_sorted = ja + jnp.sum(jnp.where(ja[:, None] >= start[None, :], d_off[None, :], 0), axis=1)
    _, pos = lax.sort((order, dst_sorted), num_keys=1)
    jp = jnp.arange(p, dtype=i32)
    in_or_after = jp[:, None] >= pstart[None, :]
    src = jp - jnp.sum(jnp.where(in_or_after, d_off[None, :], 0), axis=1)
    valid = src < jnp.sum(jnp.where(in_or_after, cnt[None, :], 0), axis=1)
    src = jnp.where(valid, src, jp % a)
    slot_tok = _rows(order, src) // TOP_K
    slot_w = jnp.where(valid, _rows(wsort, src), 0.0)
    jb = jnp.arange(nblk, dtype=i32) * bm
    blk_e = jnp.minimum(jnp.sum((jb[:, None] >= pend[None, :]).astype(i32), axis=1), N_EXPERTS - 1)
    nused = (pend[-1:] // bm).astype(i32)
    return slot_tok, slot_w, pos.reshape(n, TOP_K), blk_e, nused


def _combine_kernel(x_ref, y0_ref, y1_ref, ga_ref, gf_ref, o_ref, *, tiles_per_batch, nbatch, final):
    r = jnp.minimum(pl.program_id(0) // tiles_per_batch, nbatch)
    xn = x_ref[...] + ga_ref[pl.ds(r, 1), :] * (y0_ref[...].astype(F32) + y1_ref[...].astype(F32))
    if final:
        xn = xn * lax.rsqrt(jnp.mean(xn * xn, axis=-1, keepdims=True) + EPS) * gf_ref[...]
    o_ref[...] = xn


def _combine(x, y0, y1, mod, l, g_final, n_rows, nbatch, lat_rows, final):
    d = x.shape[1]
    tm = _pick_tile(OUT_TM, lat_rows // nbatch, n_rows)
    kern = functools.partial(_combine_kernel, tiles_per_batch=lat_rows // nbatch // tm, nbatch=nbatch,
                             final=final)
    row = pl.BlockSpec((tm, d), lambda i: (i, 0))
    return pl.pallas_call(
        kern,
        grid=(n_rows // tm,),
        in_specs=[row, row, row, pl.BlockSpec((None, SUBLANES, d), lambda i: (l, 0, 5)),
                  pl.BlockSpec((1, d), lambda i: (0, 0))],
        out_specs=row,
        out_shape=jax.ShapeDtypeStruct((n_rows, d), F32),
        compiler_params=_cparams(("arbitrary",), 40),
    )(x, y0, y1, mod, g_final.reshape(1, d))


def kernel(x, c, ctx, c_ctx, w_ada, b_ada, g_mix, g_ffn, w_in, conv_w, conv_b, conv_ln_g, conv_ln_b,
           na_rpb, hgrn_lb, hgrn_norm_g, w_out, w_router_group, b_router_group, w_router_expert,
           b_router_expert, w_exp_gate, w_exp_up, w_exp_down, g_final):
    nb, seq, d = x.shape
    ctx_len = ctx.shape[1]
    depth = w_ada.shape[0]
    lat_rows = nb * seq
    n_all = lat_rows + nb * ctx_len
    conv_ch = conv_w.shape[2]
    off_na = 2 * conv_ch
    off_hg = off_na + 3 * NA_HEADS * NA_HD
    rows = seq // GRID_W
    assert nb < SUBLANES and rows % NA_ROWS == 0 and rows >= NA_KEY_ROWS

    lbs = jnp.cumsum(jax.nn.softmax(hgrn_lb.astype(F32), axis=0), axis=0)
    lbs = lbs - lbs[:1]

    cond = jnp.concatenate([c, c_ctx[None, :], jnp.zeros((SUBLANES - nb - 1, d), F32)], axis=0)
    mod = _ada_mod(cond, w_ada, b_ada)

    xs = jnp.concatenate([x.reshape(lat_rows, d), ctx.reshape(nb * ctx_len, d)], axis=0)
    w_in_b = w_in.astype(BF16)
    w_out_b = w_out.astype(BF16)
    na_bias = _na_bias(na_rpb.reshape((depth * NA_HEADS,) + na_rpb.shape[2:]), rows)
    for l in range(depth):
        with_ctx = l < depth - 1
        n_act = n_all if with_ctx else lat_rows
        u = _norm_in(xs, g_mix[l], mod, w_in_b, l, nb, lat_rows)

        conv = _conv_module(u, conv_w[l], conv_b[l], conv_ln_g[l], conv_ln_b[l], n_act, lat_rows, seq)
        na = _na_latent(u, na_bias, l * NA_HEADS, nb, seq, lat_rows, off_na)
        if with_ctx:
            na = jnp.concatenate([na, _ctx_attn(u, nb, ctx_len, lat_rows, off_na)], axis=0)
        hg = _hgrn(u, lbs[l], hgrn_norm_g[l], nb, seq, ctx_len, lat_rows, off_hg)

        w_router = jnp.concatenate(
            [w_router_group[l], w_router_expert[l],
             jnp.zeros((d, ROUTER_PAD - N_GROUPS - N_EXPERTS), F32)], axis=1)
        b_router = jnp.concatenate(
            [b_router_group[l], b_router_expert[l],
             jnp.zeros((ROUTER_PAD - N_GROUPS - N_EXPERTS,), F32)]).reshape(1, ROUTER_PAD)
        x_mid, h, route = _out_proj(xs, conv, na, hg, w_out_b, l, mod, g_ffn[l],
                                    w_router, b_router, n_act, nb, lat_rows)

        slot_tok, slot_w, pos, blk_e, nused = _route_meta(route, n_act)
        ys = _moe_experts(_rows(h, slot_tok), slot_w, blk_e, nused, w_exp_gate, w_exp_up, w_exp_down, l)
        y0 = _rows(ys, pos[:, 0])
        y1 = _rows(ys, pos[:, 1])
        xs = _combine(x_mid, y0, y1, mod, l, g_final, n_act, nb, lat_rows, final=not with_ctx)
    return xs.reshape(nb, seq, d)
```

```python
import functools

import numpy as np
import jax
import jax.numpy as jnp
from jax import lax
from jax.experimental import pallas as pl
from jax.experimental.pallas import tpu as pltpu

F32 = jnp.float32
BF16 = jnp.bfloat16

EPS = 1e-6
NEG_INF = -1e30
LOG2_E = 1.4426950408889634

GRID_W = 64
CONV_K = 31
NA_HEADS = 16
NA_HD = 64
NA_KH = 8
NA_KW = 16
HG_HEADS = 4
HG_DK = 128
N_GROUPS = 4
EXP_PER_GROUP = 8
N_EXPERTS = N_GROUPS * EXP_PER_GROUP
TOP_K = 2

LANES = 128
SUBLANES = 8

ROW_TILE = 256
ADA_TN = 1024
IN_TM = 1024
IN_TN = 512
OUT_TM = 512
CONV_ROWS = 64
NA_STEP_HEADS = 4
NA_AHEAD = 4
NA_ROWS = 8
NA_KEY_ROWS = 16
HG_BLOCK = 256
HG_AHEAD = 2
MOE_BM = 512
ROUTER_PAD = LANES
HALO = 16


def _pick_tile(pref, *extents):
    t = pref
    while t > ROW_TILE and any(e % t for e in extents):
        t //= 2
    assert all(e % t == 0 for e in extents)
    return t


def _cparams(sem, vmem_mb):
    return pltpu.CompilerParams(dimension_semantics=sem, vmem_limit_bytes=vmem_mb * 1024 * 1024)


def _dot(a, b):
    return jnp.dot(a, b, preferred_element_type=F32)


def _dot_nt(a, b):
    return lax.dot_general(a, b, (((1,), (1,)), ((), ())), preferred_element_type=F32)


def _dot_tn(a, b):
    return lax.dot_general(a, b, (((0,), (0,)), ((), ())), preferred_element_type=F32)


def _sigmoid(x):
    return 1.0 / (1.0 + jnp.exp(-x))


def _silu(x):
    return x * _sigmoid(x)


def _split_bf16(x):
    hi = x.astype(BF16)
    return hi, (x - hi.astype(F32)).astype(BF16)


def _ada_kernel(c_ref, w_ref, b_ref, o_ref):
    s_hi, s_lo = _split_bf16(_silu(c_ref[...]))
    w_hi, w_lo = _split_bf16(w_ref[0])
    o_ref[0] = _dot(s_hi, w_hi) + (_dot(s_lo, w_hi) + _dot(s_hi, w_lo)) + b_ref[0]


def _ada_mod(cond, w_ada, b_ada):
    depth, d, n = w_ada.shape
    tn = ADA_TN
    return pl.pallas_call(
        _ada_kernel,
        grid=(depth, n // tn),
        in_specs=[
            pl.BlockSpec((SUBLANES, d), lambda l, j: (0, 0)),
            pl.BlockSpec((1, d, tn), lambda l, j: (l, 0, j)),
            pl.BlockSpec((1, 1, tn), lambda l, j: (l, 0, j)),
        ],
        out_specs=pl.BlockSpec((1, SUBLANES, tn), lambda l, j: (l, 0, j)),
        out_shape=jax.ShapeDtypeStruct((depth, SUBLANES, n), F32),
        compiler_params=_cparams(("arbitrary", "arbitrary"), 40),
    )(cond, w_ada, b_ada.reshape(depth, 1, n))


def _rms_mod(x, g, scale, shift):
    y = x * lax.rsqrt(jnp.mean(x * x, axis=-1, keepdims=True) + EPS)
    return (y * g) * (1.0 + scale) + shift


def _norm_in_kernel(x_ref, g_ref, sh_ref, sc_ref, w_ref, o_ref, h_ref, *, tiles_per_batch, nbatch):
    i = pl.program_id(0)

    @pl.when(pl.program_id(1) == 0)
    def _():
        r = jnp.minimum(i // tiles_per_batch, nbatch)
        h = _rms_mod(x_ref[...], g_ref[...], sc_ref[pl.ds(r, 1), :], sh_ref[pl.ds(r, 1), :])
        h_ref[...] = h.astype(BF16)

    o_ref[...] = _dot(h_ref[...], w_ref[...])


def _norm_in(x, g, mod, w_bf16, l, nbatch, lat_rows):
    n, d = x.shape
    nout = w_bf16.shape[2]
    tm, tn = _pick_tile(IN_TM, lat_rows // nbatch, n), IN_TN
    kern = functools.partial(_norm_in_kernel, tiles_per_batch=lat_rows // nbatch // tm, nbatch=nbatch)
    return pl.pallas_call(
        kern,
        grid=(n // tm, nout // tn),
        in_specs=[
            pl.BlockSpec((tm, d), lambda i, j: (i, 0)),
            pl.BlockSpec((1, d), lambda i, j: (0, 0)),
            pl.BlockSpec((None, SUBLANES, d), lambda i, j: (l, 0, 0)),
            pl.BlockSpec((None, SUBLANES, d), lambda i, j: (l, 0, 1)),
            pl.BlockSpec((None, d, tn), lambda i, j: (l, 0, j)),
        ],
        out_specs=pl.BlockSpec((tm, tn), lambda i, j: (i, j)),
        out_shape=jax.ShapeDtypeStruct((n, nout), F32),
        scratch_shapes=[pltpu.VMEM((tm, d), BF16)],
        compiler_params=_cparams(("arbitrary", "arbitrary"), 48),
    )(x, g.reshape(1, d), mod, mod, w_bf16)


def _conv_kernel(ap_ref, gp_ref, a_ref, gt_ref, an_ref, gn_ref, w_ref, b_ref, lg_ref, lb_ref,
                 o_ref, buf_ref, acc_ref, *, lat_tiles, tiles_per_seq):
    i = pl.program_id(0)
    tc, ch = a_ref.shape
    is_lat = i < lat_tiles
    pos = i % tiles_per_seq
    first = jnp.logical_or(jnp.logical_not(is_lat), pos == 0)
    last = jnp.logical_or(jnp.logical_not(is_lat), pos == tiles_per_seq - 1)

    buf_ref[0:HALO] = jnp.where(first, 0.0, ap_ref[...] * _sigmoid(gp_ref[...]))
    buf_ref[HALO:HALO + tc] = a_ref[...] * _sigmoid(gt_ref[...])
    buf_ref[HALO + tc:2 * HALO + tc] = jnp.where(last, 0.0, an_ref[...] * _sigmoid(gn_ref[...]))

    rows = CONV_ROWS
    base = HALO - CONV_K // 2
    for c in range(ch // LANES):
        cs = slice(c * LANES, (c + 1) * LANES)
        for r in range(tc // rows):
            acc = None
            for res in range(SUBLANES):
                y = None
                for k in range(CONV_K):
                    if (base + k) % SUBLANES != res:
                        continue
                    lo = r * rows + (base + k) // SUBLANES * SUBLANES
                    term = w_ref[k:k + 1, cs] * buf_ref[lo:lo + rows + SUBLANES, cs]
                    y = term if y is None else y + term
                if y is not None:
                    y = y[res:res + rows]
                    acc = y if acc is None else acc + y
            acc_ref[r * rows:(r + 1) * rows, cs] = acc

    h = acc_ref[...] + b_ref[...]
    mu = jnp.mean(h, axis=-1, keepdims=True)
    var = jnp.mean(jnp.square(h - mu), axis=-1, keepdims=True)
    y = (h - mu) * lax.rsqrt(var + EPS) * lg_ref[...] + lb_ref[...]
    o_ref[...] = _silu(y).astype(o_ref.dtype)


def _conv_module(u, w_dw, b_dw, ln_g, ln_b, n_rows, lat_rows, seq):
    ch = w_dw.shape[1]
    tc = ROW_TILE
    per = tc // HALO
    nh = u.shape[0] // HALO
    kern = functools.partial(_conv_kernel, lat_tiles=lat_rows // tc, tiles_per_seq=seq // tc)
    prev_map = lambda c: (lambda i: (jnp.maximum(i * per - 1, 0), c))
    next_map = lambda c: (lambda i: (jnp.minimum((i + 1) * per, nh - 1), c))
    vec = lambda a: a.reshape(1, ch)
    return pl.pallas_call(
        kern,
        grid=(n_rows // tc,),
        in_specs=[
            pl.BlockSpec((HALO, ch), prev_map(0)),
            pl.BlockSpec((HALO, ch), prev_map(1)),
            pl.BlockSpec((tc, ch), lambda i: (i, 0)),
            pl.BlockSpec((tc, ch), lambda i: (i, 1)),
            pl.BlockSpec((HALO, ch), next_map(0)),
            pl.BlockSpec((HALO, ch), next_map(1)),
            pl.BlockSpec((CONV_K, ch), lambda i: (0, 0)),
            pl.BlockSpec((1, ch), lambda i: (0, 0)),
            pl.BlockSpec((1, ch), lambda i: (0, 0)),
            pl.BlockSpec((1, ch), lambda i: (0, 0)),
        ],
        out_specs=pl.BlockSpec((tc, ch), lambda i: (i, 0)),
        out_shape=jax.ShapeDtypeStruct((n_rows, ch), BF16),
        scratch_shapes=[pltpu.VMEM((tc + 2 * HALO, ch), F32), pltpu.VMEM((tc, ch), F32)],
        compiler_params=_cparams(("arbitrary",), 16),
    )(u, u, u, u, u, u, w_dw, vec(b_dw), vec(ln_g), vec(ln_b))


def _na_bias_tables(rows):
    groups = rows // NA_ROWS
    reps = [0, min(1, groups - 1), groups - 1]
    out = []
    for g in reps:
        start = int(np.clip(NA_ROWS * g - NA_KH // 2, 0, rows - NA_KEY_ROWS))
        per_row = []
        for i in range(NA_ROWS):
            r = NA_ROWS * g + i
            sr = int(np.clip(r - NA_KH // 2, 0, rows - NA_KH))
            per_row.append((sr - start, sr - r + NA_KH - 1))
        out.append(per_row)
    return out


def _na_bias(rpb, rows):
    nh = rpb.shape[0]
    ndr, ndc = 2 * NA_KH - 1, 2 * NA_KW - 1
    period = 2 * GRID_W - 1
    pad = GRID_W - NA_KW
    vp = jnp.pad(rpb.astype(F32), ((0, 0), (0, 0), (pad, period - ndc - pad)))
    hank = jnp.tile(vp, (1, 1, GRID_W + 1))[:, :, :GRID_W * (period + 1)]
    hank = hank.reshape(nh, ndr, GRID_W, period + 1)[..., :GRID_W]
    toe = hank[:, :, ::-1, :]
    c = np.arange(GRID_W)[:, None]
    j = np.arange(GRID_W)[None, :]
    ws = np.clip(c - NA_KW // 2, 0, GRID_W - NA_KW)
    col_ok = (j >= ws) & (j < ws + NA_KW)
    toe = jnp.where(col_ok[None, None], toe, NEG_INF)
    flat = jnp.transpose(toe, (0, 2, 1, 3)).reshape(nh, GRID_W, ndr * GRID_W)
    nk = NA_KEY_ROWS * GRID_W
    lpad = NA_ROWS * GRID_W
    total = 2 * nk

    def padded(shift):
        return jnp.pad(flat, ((0, 0), (0, 0), (lpad - shift, total - flat.shape[2] - lpad + shift)))

    tab = jnp.stack([padded(0), padded(GRID_W)])
    tab = tab.reshape(2, nh, GRID_W, total // LANES, LANES).transpose(0, 1, 3, 2, 4)
    tables = _na_bias_tables(rows)
    mask = np.full((len(tables), NA_ROWS, nk), NEG_INF, np.float32)
    dvals = []
    for cls, per_row in enumerate(tables):
        dvals.append(per_row[0][1] - per_row[0][0])
        for i, (off, lo) in enumerate(per_row):
            assert lo - off == dvals[-1] - i and -NA_ROWS <= lo - off < NA_ROWS
            mask[cls, i, off * GRID_W:(off + NA_KH) * GRID_W] = 0.0
    assert len({d % 2 for d in dvals}) == 1
    assert len(tables) < 3 or rows < 3 * NA_ROWS or all(off == i for i, (off, _) in enumerate(tables[1]))
    return tab * LOG2_E, jnp.asarray(mask * LOG2_E), tuple(dvals)


def _softmax_pv(s_parts, v_parts):
    m = functools.reduce(jnp.maximum, [jnp.max(s, axis=-1, keepdims=True) for s in s_parts])
    acc, l = None, None
    for s, v in zip(s_parts, v_parts):
        p = jnp.exp2(s - m)
        ps = jnp.sum(p, axis=-1, keepdims=True)
        pv = _dot(p.astype(BF16), v)
        l = ps if l is None else l + ps
        acc = pv if acc is None else acc + pv
    return acc / l


def _na_head_edge(s_raw, s_ctx, vs, vc, tab_ref, mask_ref, a, dcls, parity, pad_blocks):
    kw = s_raw[0].shape[1]
    bpk = kw // LANES
    s_parts = []
    for m, s in enumerate(s_raw):
        row_blocks = []
        for i in range(NA_ROWS):
            copy = (parity - i) % 2
            first = (dcls - i - copy + 2 * pad_blocks) // 2 + m * bpk
            bias = jnp.concatenate([tab_ref[copy, a, first + t] for t in range(bpk)], axis=1)
            bias = bias + mask_ref[i:i + 1, m * kw:(m + 1) * kw]
            row_blocks.append(s[i * GRID_W:(i + 1) * GRID_W, :] + bias)
        s_parts.append(jnp.concatenate(row_blocks, axis=0))
    return _softmax_pv(s_parts + [s_ctx], vs + [vc])


def _na_head_interior(s_raw, s_ctx, vs, vc, tab_ref, mask_ref, a, d0, pad_blocks):
    kw = s_raw[0].shape[1]
    bpk = kw // LANES
    nblk = len(s_raw) * bpk
    p_rows, pc_rows, l_rows = [], [], []
    for i in range(NA_ROWS):
        rs = slice(i * GRID_W, (i + 1) * GRID_W)
        lo_lane, hi_lane = i * GRID_W, (i + NA_KH) * GRID_W
        b_lo, b_hi = lo_lane // LANES, -(-hi_lane // LANES)
        copy = (d0 - i) % 2
        first = (d0 - i - copy + 2 * pad_blocks) // 2
        blocks = []
        for b in range(b_lo, b_hi):
            ls = slice((b % bpk) * LANES, (b % bpk + 1) * LANES)
            sb = s_raw[b // bpk][rs, ls] + tab_ref[copy, a, first + b]
            if b * LANES < lo_lane or (b + 1) * LANES > hi_lane:
                sb = sb + mask_ref[i:i + 1, b * LANES:(b + 1) * LANES]
            blocks.append(sb)
        sw = jnp.concatenate(blocks, axis=1)
        sc = s_ctx[rs]
        m = jnp.maximum(jnp.max(sw, axis=-1, keepdims=True), jnp.max(sc, axis=-1, keepdims=True))
        pw = jnp.exp2(sw - m)
        pc = jnp.exp2(sc - m)
        l_rows.append(jnp.sum(pw, axis=-1, keepdims=True) + jnp.sum(pc, axis=-1, keepdims=True))
        pieces = [jnp.zeros((GRID_W, b_lo * LANES), BF16), pw.astype(BF16),
                  jnp.zeros((GRID_W, (nblk - b_hi) * LANES), BF16)]
        p_rows.append(jnp.concatenate([x for x in pieces if x.shape[1]], axis=1))
        pc_rows.append(pc.astype(BF16))
    p = jnp.concatenate(p_rows, axis=0)
    acc = _dot(jnp.concatenate(pc_rows, axis=0), vc)
    for m, v in enumerate(vs):
        acc = acc + _dot(p[:, m * kw:(m + 1) * kw], v)
    return acc / jnp.concatenate(l_rows, axis=0)


def _na_kernel(q_ref, k0, k1, k2, k3, v0, v1, v2, v3, kc_ref, vc_ref, tab_ref, mask_ref, o_ref, *,
               groups, dvals):
    g = pl.program_id(1)
    is_edge = jnp.logical_or(g == 0, g == groups - 1)
    pad_blocks = NA_ROWS * GRID_W // LANES

    def step(interior):
        q2 = q_ref[...] * (NA_HD ** -0.5 * LOG2_E)
        lane = lax.broadcasted_iota(jnp.int32, q2.shape, 1)
        ks = [k[...].astype(BF16) for k in (k0, k1, k2, k3)]
        vs = [v[...].astype(BF16) for v in (v0, v1, v2, v3)]
        kc = kc_ref[...].astype(BF16)
        vc = vc_ref[...].astype(BF16)
        nheads = q2.shape[1] // NA_HD

        def scores(a):
            sel = jnp.logical_and(lane >= a * NA_HD, lane < (a + 1) * NA_HD)
            qa = jnp.where(sel, q2, 0.0).astype(BF16)
            return [_dot_nt(qa, k) for k in ks], _dot_nt(qa, kc)

        pending = [scores(a) for a in range(min(NA_AHEAD, nheads))]
        out = None
        for a in range(nheads):
            if a + NA_AHEAD < nheads:
                pending.append(scores(a + NA_AHEAD))
            s_raw, s_ctx = pending[a]
            if interior:
                o = _na_head_interior(s_raw, s_ctx, vs, vc, tab_ref, mask_ref, a, dvals[1], pad_blocks)
            else:
                dcls = jnp.where(g == 0, dvals[0], dvals[2])
                o = _na_head_edge(s_raw, s_ctx, vs, vc, tab_ref, mask_ref, a, dcls, dvals[0], pad_blocks)
            out = o if out is None else jnp.where(lane >= a * NA_HD, o, out)
        o_ref[...] = out.astype(o_ref.dtype)

    @pl.when(is_edge)
    def _():
        step(False)

    @pl.when(jnp.logical_not(is_edge))
    def _():
        step(True)


def _na_latent(u, bias, head_base, nbatch, seq, lat_rows, off_na):
    tab, mask, dvals = bias
    rows = seq // GRID_W
    groups = rows // NA_ROWS
    nq = NA_ROWS * GRID_W
    kblk = ROW_TILE
    nkb = NA_KEY_ROWS * GRID_W // kblk
    assert nkb == 4
    hp = NA_STEP_HEADS
    wid = hp * NA_HD
    qcol = off_na // wid
    hsteps = NA_HEADS // hp
    kcol, vcol = qcol + hsteps, qcol + 2 * hsteps
    assert off_na % wid == 0 and NA_HEADS % hp == 0 and head_base % hp == 0
    kb_per_batch = seq // kblk
    kb_per_grow = GRID_W * NA_ROWS // kblk
    lat_kb = lat_rows // kblk

    def kmap(col, m):
        def f(h, g, b):
            st = jnp.clip(g * kb_per_grow - (NA_KH // 2) * GRID_W // kblk, 0, kb_per_batch - nkb)
            return (b * kb_per_batch + st + m, col + h)
        return f

    def mask_map(h, g, b):
        return (jnp.where(g == 0, 0, jnp.where(g == groups - 1, 2, 1)), 0, 0)

    in_specs = [pl.BlockSpec((nq, wid), lambda h, g, b: (b * groups + g, qcol + h))]
    in_specs += [pl.BlockSpec((kblk, wid), kmap(kcol, m)) for m in range(nkb)]
    in_specs += [pl.BlockSpec((kblk, wid), kmap(vcol, m)) for m in range(nkb)]
    in_specs += [pl.BlockSpec((ROW_TILE, wid), lambda h, g, b: (lat_kb + b, kcol + h)),
                 pl.BlockSpec((ROW_TILE, wid), lambda h, g, b: (lat_kb + b, vcol + h)),
                 pl.BlockSpec((2, hp) + tab.shape[2:], lambda h, g, b: (0, head_base // hp + h, 0, 0, 0)),
                 pl.BlockSpec((None,) + mask.shape[1:], mask_map)]
    return pl.pallas_call(
        functools.partial(_na_kernel, groups=groups, dvals=dvals),
        grid=(hsteps, groups, nbatch),
        in_specs=in_specs,
        out_specs=pl.BlockSpec((nq, wid), lambda h, g, b: (b * groups + g, h)),
        out_shape=jax.ShapeDtypeStruct((lat_rows, NA_HEADS * NA_HD), BF16),
        compiler_params=_cparams(("arbitrary", "arbitrary", "arbitrary"), 48),
    )(*([u] * 11), tab, mask)


def _ctx_attn_kernel(q_ref, k_ref, v_ref, o_ref):
    q2 = q_ref[...] * (NA_HD ** -0.5 * LOG2_E)
    lane = lax.broadcasted_iota(jnp.int32, q2.shape, 1)
    k = k_ref[...].astype(BF16)
    v = v_ref[...].astype(BF16)
    outs = []
    for a in range(2):
        sel = (lane < NA_HD) if a == 0 else (lane >= NA_HD)
        qa = jnp.where(sel, q2, 0.0).astype(BF16)
        outs.append(_softmax_pv([_dot_nt(qa, k)], [v]))
    o_ref[...] = jnp.where(lane < NA_HD, outs[0], outs[1]).astype(o_ref.dtype)


def _ctx_attn(u, nbatch, ctx_len, lat_rows, off_na):
    assert ctx_len == ROW_TILE
    qcol = off_na // LANES
    heads2 = NA_HEADS * NA_HD // LANES
    base = lat_rows // ROW_TILE
    spec = lambda col: pl.BlockSpec((ROW_TILE, LANES), lambda b, h: (base + b, col + h))
    return pl.pallas_call(
        _ctx_attn_kernel,
        grid=(nbatch, heads2),
        in_specs=[spec(qcol), spec(qcol + heads2), spec(qcol + 2 * heads2)],
        out_specs=pl.BlockSpec((ROW_TILE, LANES), lambda b, h: (b, h)),
        out_shape=jax.ShapeDtypeStruct((nbatch * ctx_len, NA_HEADS * NA_HD), BF16),
        compiler_params=_cparams(("arbitrary", "arbitrary"), 16),
    )(u, u, u)


def _hg_level_map(rev):
    size = HG_BLOCK // 2
    t = np.arange(size)[:, None]
    s = np.arange(size)[None, :]
    x = t ^ s
    lvl = np.where(x > 0, np.frexp(np.maximum(x, 1))[1] - 1, -1)
    causal = (s < t) if not rev else (s > t)
    out = np.where(causal, lvl, -1)
    out = np.where(t == s, int(np.log2(size)), out)
    return out.astype(np.int32)


def _hg_tri(rev):
    t = np.arange(HG_BLOCK)[:, None]
    s = np.arange(HG_BLOCK)[None, :]
    return ((s <= t) if not rev else (s >= t)).astype(np.float32)


def _hg_anchor(b3, m, rev):
    nv = b3.shape[0]
    if m >= SUBLANES:
        w = m // SUBLANES
        b4 = b3.reshape(nv // (2 * w), 2 * w, SUBLANES, LANES)
        a = b4[:, w:w + 1, 0:1, :] if rev else b4[:, w - 1:w, SUBLANES - 1:SUBLANES, :]
        return jnp.broadcast_to(a, b4.shape).reshape(b3.shape)
    sub = lax.broadcasted_iota(jnp.int32, b3.shape, 1)
    out = None
    for g in range(SUBLANES // (2 * m)):
        idx = g * 2 * m + (m if rev else m - 1)
        a = jnp.broadcast_to(b3[:, idx:idx + 1, :], b3.shape)
        out = a if out is None else jnp.where(sub >= g * 2 * m, a, out)
    return out


def _neg_abs(x):
    bits = lax.bitcast_convert_type(x, jnp.uint32) | jnp.uint32(0x80000000)
    return lax.bitcast_convert_type(bits, F32)


def _hg_pick(q3, k3, m, rev):
    nv = q3.shape[0]
    if m >= SUBLANES:
        w = m // SUBLANES
        shape4 = (nv // (2 * w), 2 * w, SUBLANES, LANES)
        q4, k4 = q3.reshape(shape4), k3.reshape(shape4)
        lower, upper = (q4, k4) if rev else (k4, q4)
        return jnp.concatenate([lower[:, :w], upper[:, w:]], axis=1).reshape(q3.shape)
    upper_rows = (lax.broadcasted_iota(jnp.int32, q3.shape, 1) & m) != 0
    return jnp.where(upper_rows, k3 if rev else q3, q3 if rev else k3)


def _hg_gates(q, z, alog, clog, oml, tri):
    q = _silu(q)
    t = jnp.exp(-jnp.abs(z))
    lsig = jnp.minimum(z, 0.0) - jnp.log(1.0 + t)
    cc = clog + lsig
    logf = jnp.maximum(alog, cc) + jnp.log(1.0 + jnp.exp(-jnp.abs(alog - cc)))
    kk = oml * jnp.where(z >= 0.0, t, 1.0) / (1.0 + t)

    hi = logf.astype(BF16)
    r1 = logf - hi.astype(F32)
    mid = r1.astype(BF16)
    lo = (r1 - mid.astype(F32)).astype(BF16)
    b = (_dot(tri, hi) + _dot(tri, mid) + _dot(tri, lo)) * LOG2_E
    return q, kk, b


def _hg_mix(q, kk, b, v, lv, st, rev):
    n = q.shape[0]
    half = n // 2
    nlev = int(np.log2(n))
    shape3 = (n // SUBLANES, SUBLANES, LANES)
    b3, q3, k3 = b.reshape(shape3), q.reshape(shape3), kk.reshape(shape3)
    halves = (slice(0, half), slice(half, n))
    qb, kb = q.astype(BF16), kk.astype(BF16)
    acc = [jnp.where(lv == nlev - 1, _dot_nt(qb[hs], kb[hs]), 0.0) for hs in halves]
    first, second = (halves[1], halves[0]) if rev else halves
    e = jnp.exp2(_neg_abs(b - (b[half:half + 1, :] if rev else b[half - 1:half, :])))
    top = _dot_nt((q[second] * e[second]).astype(BF16), (kk[first] * e[first]).astype(BF16))
    for lev in range(nlev - 1):
        m = 1 << lev
        e = jnp.exp2(_neg_abs(b3 - _hg_anchor(b3, m, rev)))
        w = (_hg_pick(q3, k3, m, rev) * e).reshape(n, LANES).astype(BF16)
        acc = [jnp.where(lv == lev, _dot_nt(w[hs], w[hs]), a) for hs, a in zip(halves, acc)]

    vb = v.astype(BF16)
    a0, a1 = acc[0].astype(BF16), acc[1].astype(BF16)
    tb = top.astype(BF16)
    if rev:
        o_lo = _dot(jnp.concatenate([a0, tb], axis=1), vb)
        o_hi = _dot(a1, vb[halves[1]])
    else:
        o_lo = _dot(a0, vb[halves[0]])
        o_hi = _dot(jnp.concatenate([tb, a1], axis=1), vb)
    b_last = b[0:1, :] if rev else b[n - 1:n, :]
    qh = (q * jnp.exp2(b)).astype(BF16)
    o = jnp.concatenate([o_lo, o_hi], axis=0) + _dot_nt(qh, st.astype(BF16))
    kh = (kk * jnp.exp2(b_last - b)).astype(BF16)
    st_new = st * jnp.exp2(b_last) + _dot_tn(vb, kh)
    return o, st_new


def _hg_gates_ahead(q_ref, z_ref, al_ref, cl_ref, om_ref, tri):
    def gates(h):
        hs = slice(h * HG_DK, (h + 1) * HG_DK)
        return _hg_gates(q_ref[:, hs], z_ref[:, hs], al_ref[:, hs], cl_ref[:, hs], om_ref[:, hs], tri)

    pending = [gates(h) for h in range(min(HG_AHEAD, HG_HEADS))]
    for h in range(HG_HEADS):
        if h + HG_AHEAD < HG_HEADS:
            pending.append(gates(h + HG_AHEAD))
        yield pending[h]


def _hg_fwd_kernel(q_ref, v_ref, z_ref, al_ref, cl_ref, om_ref, tri_ref, lv_ref, o_ref, st_ref):
    @pl.when(pl.program_id(1) == 0)
    def _():
        st_ref[...] = jnp.zeros_like(st_ref)

    lv = lv_ref[...]
    gates = _hg_gates_ahead(q_ref, z_ref, al_ref, cl_ref, om_ref, tri_ref[...])
    for h in range(HG_HEADS):
        hs = slice(h * HG_DK, (h + 1) * HG_DK)
        o, st = _hg_mix(*next(gates), v_ref[:, hs], lv, st_ref[h], False)
        o_ref[:, hs] = o
        st_ref[h] = st


def _hg_bwd_kernel(q_ref, v_ref, z_ref, g_ref, of_ref, al_ref, cl_ref, om_ref, ng_ref, tri_ref, lv_ref,
                   o_ref, st_ref):
    @pl.when(pl.program_id(1) == 0)
    def _():
        st_ref[...] = jnp.zeros_like(st_ref)

    lv = lv_ref[...]
    gates = _hg_gates_ahead(q_ref, z_ref, al_ref, cl_ref, om_ref, tri_ref[...])
    for h in range(HG_HEADS):
        hs = slice(h * HG_DK, (h + 1) * HG_DK)
        o, st = _hg_mix(*next(gates), v_ref[:, hs], lv, st_ref[h], True)
        st_ref[h] = st
        t = of_ref[:, hs] + o
        y = t * lax.rsqrt(jnp.mean(t * t, axis=-1, keepdims=True) + EPS)
        o_ref[:, hs] = (y * ng_ref[:, hs] * _silu(g_ref[:, hs])).astype(o_ref.dtype)


def _hgrn(u, lb, norm_g, nbatch, seq, ctx_len, lat_rows, off_hg):
    assert ctx_len == HG_BLOCK
    n = u.shape[0]
    hd = HG_HEADS * HG_DK
    col = off_hg // hd
    per = seq // HG_BLOCK
    lat_blocks = lat_rows // HG_BLOCK
    lbf = lb.astype(F32)
    alog, clog, oml = jnp.log(lbf), jnp.log1p(-lbf), 1.0 - lbf

    def fmap(c):
        return lambda b, j: (jnp.where(j == 0, lat_blocks + b, b * per + j - 1), c)

    def bmap(c):
        return lambda b, j: (jnp.where(j == 0, lat_blocks + b, b * per + per - j), c)

    const = lambda shape: pl.BlockSpec(shape, lambda b, j: (0, 0))
    grid = (nbatch, per + 1)
    vec = lambda a: a.reshape(1, hd)
    o_f = pl.pallas_call(
        _hg_fwd_kernel,
        grid=grid,
        in_specs=[pl.BlockSpec((HG_BLOCK, hd), fmap(col)), pl.BlockSpec((HG_BLOCK, hd), fmap(col + 1)),
                  pl.BlockSpec((HG_BLOCK, hd), fmap(col + 2)),
                  const((1, hd)), const((1, hd)), const((1, hd)),
                  const((HG_BLOCK, HG_BLOCK)), const((HG_BLOCK // 2, HG_BLOCK // 2))],
        out_specs=pl.BlockSpec((HG_BLOCK, hd), fmap(0)),
        out_shape=jax.ShapeDtypeStruct((n, hd), F32),
        scratch_shapes=[pltpu.VMEM((HG_HEADS, HG_DK, HG_DK), F32)],
        compiler_params=_cparams(("arbitrary", "arbitrary"), 32),
    )(u, u, u, vec(alog[0]), vec(clog[0]), vec(oml[0]),
      jnp.asarray(_hg_tri(False), BF16), jnp.asarray(_hg_level_map(False)))
    return pl.pallas_call(
        _hg_bwd_kernel,
        grid=grid,
        in_specs=[pl.BlockSpec((HG_BLOCK, hd), bmap(col)), pl.BlockSpec((HG_BLOCK, hd), bmap(col + 1)),
                  pl.BlockSpec((HG_BLOCK, hd), bmap(col + 3)), pl.BlockSpec((HG_BLOCK, hd), bmap(col + 4)),
                  pl.BlockSpec((HG_BLOCK, hd), bmap(0)),
                  const((1, hd)), const((1, hd)), const((1, hd)), const((1, hd)),
                  const((HG_BLOCK, HG_BLOCK)), const((HG_BLOCK // 2, HG_BLOCK // 2))],
        out_specs=pl.BlockSpec((HG_BLOCK, hd), bmap(0)),
        out_shape=jax.ShapeDtypeStruct((n, hd), BF16),
        scratch_shapes=[pltpu.VMEM((HG_HEADS, HG_DK, HG_DK), F32)],
        compiler_params=_cparams(("arbitrary", "arbitrary"), 32),
    )(u, u, u, u, o_f, vec(alog[1]), vec(clog[1]), vec(oml[1]), vec(norm_g.astype(F32)),
      jnp.asarray(_hg_tri(True), BF16), jnp.asarray(_hg_level_map(True)))


def _out_kernel(x_ref, cv_ref, na_ref, hg_ref, w_ref, ga_ref, g2_ref, sh2_ref, s2_ref, wrh_ref, wrl_ref, br_ref,
                xo_ref, h_ref, rt_ref, *, tiles_per_batch, nbatch):
    r = jnp.minimum(pl.program_id(0) // tiles_per_batch, nbatch)
    c0 = cv_ref.shape[1]
    c1 = c0 + na_ref.shape[1]
    mix = (_dot(cv_ref[...], w_ref[0:c0, :]) + _dot(na_ref[...], w_ref[c0:c1, :])
           + _dot(hg_ref[...], w_ref[c1:, :]))
    xn = x_ref[...] + ga_ref[pl.ds(r, 1), :] * mix
    xo_ref[...] = xn
    h = _rms_mod(xn, g2_ref[...], s2_ref[pl.ds(r, 1), :], sh2_ref[pl.ds(r, 1), :])
    h_ref[...] = h.astype(h_ref.dtype)
    h_hi, h_lo = _split_bf16(h)
    logits = (_dot(h_hi, wrh_ref[...]) + (_dot(h_lo, wrh_ref[...]) + _dot(h_hi, wrl_ref[...]))
              + br_ref[...])
    rt_ref[...] = _route_rows(logits)


def _route_rows(lg):
    lane = lax.broadcasted_iota(jnp.int32, lg.shape, 1)
    big = jnp.int32(2 ** 30)
    low = jnp.float32(-3e38)

    def first_max(vals, mask):
        m = jnp.max(vals, axis=-1, keepdims=True)
        idx = jnp.min(jnp.where(jnp.logical_and(vals == m, mask), lane, big), axis=-1, keepdims=True)
        return m, idx

    gmask = lane < N_GROUPS
    gl = jnp.where(gmask, lg, low)
    gm, grp = first_max(gl, gmask)
    p_grp = 1.0 / jnp.sum(jnp.where(gmask, jnp.exp(gl - gm), 0.0), axis=-1, keepdims=True)
    lo = N_GROUPS + grp * EXP_PER_GROUP
    emask = jnp.logical_and(lane >= lo, lane < lo + EXP_PER_GROUP)
    el = jnp.where(emask, lg, low)
    m1, i1 = first_max(el, emask)
    emask2 = jnp.logical_and(emask, lane != i1)
    el2 = jnp.where(emask2, lg, low)
    m2, i2 = first_max(el2, emask2)
    t = jnp.exp(m2 - m1)
    w1 = p_grp / (1.0 + t)
    w2 = p_grp * t / (1.0 + t)
    e1 = (i1 - N_GROUPS).astype(F32)
    e2 = (i2 - N_GROUPS).astype(F32)
    return jnp.where(lane == 0, e1, jnp.where(lane == 1, e2, jnp.where(lane == 2, w1,
                     jnp.where(lane == 3, w2, 0.0))))


def _out_proj(x, conv, na, hg, w_bf16, l, mod, g_ffn, w_router, b_router, n_rows, nbatch, lat_rows):
    d = x.shape[1]
    w_router_hi, w_router_lo = _split_bf16(w_router)
    tm = _pick_tile(OUT_TM, lat_rows // nbatch, n_rows)
    kern = functools.partial(_out_kernel, tiles_per_batch=lat_rows // nbatch // tm, nbatch=nbatch)
    row = lambda w: pl.BlockSpec((tm, w), lambda i: (i, 0))
    const = lambda shape: pl.BlockSpec(shape, lambda i: (0, 0))
    modc = lambda c: pl.BlockSpec((None, SUBLANES, d), lambda i: (l, 0, c))
    return pl.pallas_call(
        kern,
        grid=(n_rows // tm,),
        in_specs=[row(d), row(conv.shape[1]), row(na.shape[1]), row(hg.shape[1]),
                  pl.BlockSpec((None, d, d), lambda i: (l, 0, 0)),
                  modc(2),
                  const((1, d)),
                  modc(3),
                  modc(4),
                  const((d, ROUTER_PAD)), const((d, ROUTER_PAD)), const((1, ROUTER_PAD))],
        out_specs=[row(d), row(d), row(ROUTER_PAD)],
        out_shape=[jax.ShapeDtypeStruct((n_rows, d), F32), jax.ShapeDtypeStruct((n_rows, d), BF16),
                   jax.ShapeDtypeStruct((n_rows, ROUTER_PAD), F32)],
        compiler_params=_cparams(("arbitrary",), 56),
    )(x, conv, na, hg, w_bf16, mod, g_ffn.reshape(1, d), mod, mod, w_router_hi, w_router_lo, b_router)


def _moe_kernel(be_ref, nu_ref, xs_ref, w1_ref, w3_ref, w2_ref, sw_ref, o_ref, w1b, w3b, w2b):
    i = pl.program_id(0)
    e = be_ref[i]
    prev = be_ref[jnp.maximum(i - 1, 0)]

    @pl.when(jnp.logical_or(i == 0, e != prev))
    def _():
        w1b[...] = w1_ref[...].astype(BF16)
        w3b[...] = w3_ref[...].astype(BF16)
        w2b[...] = w2_ref[...].astype(BF16)

    @pl.when(i < nu_ref[0])
    def _():
        x = xs_ref[...]
        a = (_silu(_dot(x, w1b[...])) * _dot(x, w3b[...])).astype(BF16)
        o_ref[...] = (_dot(a, w2b[...]) * sw_ref[...]).astype(o_ref.dtype)

    @pl.when(i >= nu_ref[0])
    def _():
        o_ref[...] = jnp.zeros_like(o_ref)


def _moe_experts(xs, slot_w, blk_e, nused, w1, w3, w2, l):
    p, d = xs.shape
    de = w1.shape[3]
    bm = MOE_BM
    grid_spec = pltpu.PrefetchScalarGridSpec(
        num_scalar_prefetch=2,
        grid=(p // bm,),
        in_specs=[pl.BlockSpec((bm, d), lambda i, be, nu: (i, 0)),
                  pl.BlockSpec((None, None, d, de), lambda i, be, nu: (l, be[i], 0, 0)),
                  pl.BlockSpec((None, None, d, de), lambda i, be, nu: (l, be[i], 0, 0)),
                  pl.BlockSpec((None, None, de, d), lambda i, be, nu: (l, be[i], 0, 0)),
                  pl.BlockSpec((bm, 1), lambda i, be, nu: (i, 0))],
        out_specs=pl.BlockSpec((bm, d), lambda i, be, nu: (i, 0)),
        scratch_shapes=[pltpu.VMEM((d, de), BF16), pltpu.VMEM((d, de), BF16), pltpu.VMEM((de, d), BF16)],
    )
    return pl.pallas_call(
        _moe_kernel,
        grid_spec=grid_spec,
        out_shape=jax.ShapeDtypeStruct((p, d), BF16),
        compiler_params=_cparams(("arbitrary",), 48),
    )(blk_e, nused, xs, w1, w3, w2, slot_w.reshape(p, 1))


def _rows(a, idx):
    return a.at[idx].get(mode="promise_in_bounds")


def _route_meta(route, n):
    i32 = jnp.int32
    eid = route[:, 0:TOP_K].astype(i32).reshape(-1)
    wt = route[:, TOP_K:2 * TOP_K].reshape(-1)
    a = n * TOP_K
    bm = MOE_BM
    nblk = -(-a // bm) + N_EXPERTS
    p = nblk * bm
    experts = jnp.arange(N_EXPERTS, dtype=i32)[None, :]
    ja = jnp.arange(a, dtype=i32)
    se, order, wsort = lax.sort((eid, ja, wt), num_keys=1, is_stable=True)
    cnt = jnp.sum((eid[:, None] == experts).astype(i32), axis=0)
    pcnt = (cnt + bm - 1) // bm * bm
    pend = jnp.cumsum(pcnt)
    pstart = pend - pcnt
    end = jnp.cumsum(cnt)
    start = end - cnt
    off = pstart - start
    d_off = off - jnp.concatenate([jnp.zeros((1,), i32), off[:-1]])
    dst_sorted = ja + jnp.sum(jnp.where(ja[:, None] >= start[None, :], d_off[None, :], 0), axis=1)
    _, pos = lax.sort((order, dst_sorted), num_keys=1)
    jp = jnp.arange(p, dtype=i32)
    in_or_after = jp[:, None] >= pstart[None, :]
    src = jp - jnp.sum(jnp.where(in_or_after, d_off[None, :], 0), axis=1)
    valid = src < jnp.sum(jnp.where(in_or_after, cnt[None, :], 0), axis=1)
    src = jnp.where(valid, src, jp % a)
    slot_tok = _rows(order, src) // TOP_K
    slot_w = jnp.where(valid, _rows(wsort, src), 0.0)
    jb = jnp.arange(nblk, dtype=i32) * bm
    blk_e = jnp.minimum(jnp.sum((jb[:, None] >= pend[None, :]).astype(i32), axis=1), N_EXPERTS - 1)
    nused = (pend[-1:] // bm).astype(i32)
    return slot_tok, slot_w, pos.reshape(n, TOP_K), blk_e, nused


def _combine_kernel(x_ref, y0_ref, y1_ref, ga_ref, gf_ref, o_ref, *, tiles_per_batch, nbatch, final):
    r = jnp.minimum(pl.program_id(0) // tiles_per_batch, nbatch)
    xn = x_ref[...] + ga_ref[pl.ds(r, 1), :] * (y0_ref[...].astype(F32) + y1_ref[...].astype(F32))
    if final:
        xn = xn * lax.rsqrt(jnp.mean(xn * xn, axis=-1, keepdims=True) + EPS) * gf_ref[...]
    o_ref[...] = xn


def _combine(x, y0, y1, mod, l, g_final, n_rows, nbatch, lat_rows, final):
    d = x.shape[1]
    tm = _pick_tile(OUT_TM, lat_rows // nbatch, n_rows)
    kern = functools.partial(_combine_kernel, tiles_per_batch=lat_rows // nbatch // tm, nbatch=nbatch,
                             final=final)
    row = pl.BlockSpec((tm, d), lambda i: (i, 0))
    return pl.pallas_call(
        kern,
        grid=(n_rows // tm,),
        in_specs=[row, row, row, pl.BlockSpec((None, SUBLANES, d), lambda i: (l, 0, 5)),
                  pl.BlockSpec((1, d), lambda i: (0, 0))],
        out_specs=row,
        out_shape=jax.ShapeDtypeStruct((n_rows, d), F32),
        compiler_params=_cparams(("arbitrary",), 40),
    )(x, y0, y1, mod, g_final.reshape(1, d))


def kernel(x, c, ctx, c_ctx, w_ada, b_ada, g_mix, g_ffn, w_in, conv_w, conv_b, conv_ln_g, conv_ln_b,
           na_rpb, hgrn_lb, hgrn_norm_g, w_out, w_router_group, b_router_group, w_router_expert,
           b_router_expert, w_exp_gate, w_exp_up, w_exp_down, g_final):
    nb, seq, d = x.shape
    ctx_len = ctx.shape[1]
    depth = w_ada.shape[0]
    lat_rows = nb * seq
    n_all = lat_rows + nb * ctx_len
    conv_ch = conv_w.shape[2]
    off_na = 2 * conv_ch
    off_hg = off_na + 3 * NA_HEADS * NA_HD
    rows = seq // GRID_W
    assert nb < SUBLANES and rows % NA_ROWS == 0 and rows >= NA_KEY_ROWS

    lbs = jnp.cumsum(jax.nn.softmax(hgrn_lb.astype(F32), axis=0), axis=0)
    lbs = lbs - lbs[:1]

    cond = jnp.concatenate([c, c_ctx[None, :], jnp.zeros((SUBLANES - nb - 1, d), F32)], axis=0)
    mod = _ada_mod(cond, w_ada, b_ada)

    xs = jnp.concatenate([x.reshape(lat_rows, d), ctx.reshape(nb * ctx_len, d)], axis=0)
    w_in_b = w_in.astype(BF16)
    w_out_b = w_out.astype(BF16)
    na_bias = _na_bias(na_rpb.reshape((depth * NA_HEADS,) + na_rpb.shape[2:]), rows)
    for l in range(depth):
        with_ctx = l < depth - 1
        n_act = n_all if with_ctx else lat_rows
        u = _norm_in(xs, g_mix[l], mod, w_in_b, l, nb, lat_rows)

        conv = _conv_module(u, conv_w[l], conv_b[l], conv_ln_g[l], conv_ln_b[l], n_act, lat_rows, seq)
        na = _na_latent(u, na_bias, l * NA_HEADS, nb, seq, lat_rows, off_na)
        if with_ctx:
            na = jnp.concatenate([na, _ctx_attn(u, nb, ctx_len, lat_rows, off_na)], axis=0)
        hg = _hgrn(u, lbs[l], hgrn_norm_g[l], nb, seq, ctx_len, lat_rows, off_hg)

        w_router = jnp.concatenate(
            [w_router_group[l], w_router_expert[l],
             jnp.zeros((d, ROUTER_PAD - N_GROUPS - N_EXPERTS), F32)], axis=1)
        b_router = jnp.concatenate(
            [b_router_group[l], b_router_expert[l],
             jnp.zeros((ROUTER_PAD - N_GROUPS - N_EXPERTS,), F32)]).reshape(1, ROUTER_PAD)
        x_mid, h, route = _out_proj(xs, conv, na, hg, w_out_b, l, mod, g_ffn[l],
                                    w_router, b_router, n_act, nb, lat_rows)

        slot_tok, slot_w, pos, blk_e, nused = _route_meta(route, n_act)
        ys = _moe_experts(_rows(h, slot_tok), slot_w, blk_e, nused, w_exp_gate, w_exp_up, w_exp_down, l)
        y0 = _rows(ys, pos[:, 0])
        y1 = _rows(ys, pos[:, 1])
        xs = _combine(x_mid, y0, y1, mod, l, g_final, n_act, nb, lat_rows, final=not with_ctx)
    return xs.reshape(nb, seq, d)
```

```python
import functools

import numpy as np
import jax
import jax.numpy as jnp
from jax import lax
from jax.experimental import pallas as pl
from jax.experimental.pallas import tpu as pltpu

F32 = jnp.float32
BF16 = jnp.bfloat16

EPS = 1e-6
NEG_INF = -1e30
LOG2_E = 1.4426950408889634

GRID_W = 64
CONV_K = 31
NA_HEADS = 16
NA_HD = 64
NA_KH = 8
NA_KW = 16
HG_HEADS = 4
HG_DK = 128
N_GROUPS = 4
EXP_PER_GROUP = 8
N_EXPERTS = N_GROUPS * EXP_PER_GROUP
TOP_K = 2

LANES = 128
SUBLANES = 8

ROW_TILE = 256
ADA_TN = 1024
IN_TM = 1024
IN_TN = 512
OUT_TM = 512
CONV_ROWS = 64
NA_STEP_HEADS = 4
NA_AHEAD = 4
NA_ROWS = 8
NA_KEY_ROWS = 16
HG_BLOCK = 256
HG_AHEAD = 2
MOE_BM = 512
ROUTER_PAD = LANES
HALO = 16


def _pick_tile(pref, *extents):
    t = pref
    while t > ROW_TILE and any(e % t for e in extents):
        t //= 2
    assert all(e % t == 0 for e in extents)
    return t


def _cparams(sem, vmem_mb):
    return pltpu.CompilerParams(dimension_semantics=sem, vmem_limit_bytes=vmem_mb * 1024 * 1024)


def _dot(a, b):
    return jnp.dot(a, b, preferred_element_type=F32)


def _dot_nt(a, b):
    return lax.dot_general(a, b, (((1,), (1,)), ((), ())), preferred_element_type=F32)


def _dot_tn(a, b):
    return lax.dot_general(a, b, (((0,), (0,)), ((), ())), preferred_element_type=F32)


def _sigmoid(x):
    return 1.0 / (1.0 + jnp.exp(-x))


def _silu(x):
    return x * _sigmoid(x)


def _split_bf16(x):
    hi = x.astype(BF16)
    return hi, (x - hi.astype(F32)).astype(BF16)


def _ada_kernel(c_ref, w_ref, b_ref, o_ref):
    s_hi, s_lo = _split_bf16(_silu(c_ref[...]))
    w_hi, w_lo = _split_bf16(w_ref[0])
    o_ref[0] = _dot(s_hi, w_hi) + (_dot(s_lo, w_hi) + _dot(s_hi, w_lo)) + b_ref[0]


def _ada_mod(cond, w_ada, b_ada):
    depth, d, n = w_ada.shape
    tn = ADA_TN
    return pl.pallas_call(
        _ada_kernel,
        grid=(depth, n // tn),
        in_specs=[
            pl.BlockSpec((SUBLANES, d), lambda l, j: (0, 0)),
            pl.BlockSpec((1, d, tn), lambda l, j: (l, 0, j)),
            pl.BlockSpec((1, 1, tn), lambda l, j: (l, 0, j)),
        ],
        out_specs=pl.BlockSpec((1, SUBLANES, tn), lambda l, j: (l, 0, j)),
        out_shape=jax.ShapeDtypeStruct((depth, SUBLANES, n), F32),
        compiler_params=_cparams(("arbitrary", "arbitrary"), 40),
    )(cond, w_ada, b_ada.reshape(depth, 1, n))


def _rms_mod(x, g, scale, shift):
    y = x * lax.rsqrt(jnp.mean(x * x, axis=-1, keepdims=True) + EPS)
    return (y * g) * (1.0 + scale) + shift


def _norm_in_kernel(x_ref, g_ref, sh_ref, sc_ref, w_ref, o_ref, h_ref, *, tiles_per_batch, nbatch):
    i = pl.program_id(0)

    @pl.when(pl.program_id(1) == 0)
    def _():
        r = jnp.minimum(i // tiles_per_batch, nbatch)
        h = _rms_mod(x_ref[...], g_ref[...], sc_ref[pl.ds(r, 1), :], sh_ref[pl.ds(r, 1), :])
        h_ref[...] = h.astype(BF16)

    o_ref[...] = _dot(h_ref[...], w_ref[...])


def _norm_in(x, g, mod, w_bf16, l, nbatch, lat_rows):
    n, d = x.shape
    nout = w_bf16.shape[2]
    tm, tn = _pick_tile(IN_TM, lat_rows // nbatch, n), IN_TN
    kern = functools.partial(_norm_in_kernel, tiles_per_batch=lat_rows // nbatch // tm, nbatch=nbatch)
    return pl.pallas_call(
        kern,
        grid=(n // tm, nout // tn),
        in_specs=[
            pl.BlockSpec((tm, d), lambda i, j: (i, 0)),
            pl.BlockSpec((1, d), lambda i, j: (0, 0)),
            pl.BlockSpec((None, SUBLANES, d), lambda i, j: (l, 0, 0)),
            pl.BlockSpec((None, SUBLANES, d), lambda i, j: (l, 0, 1)),
            pl.BlockSpec((None, d, tn), lambda i, j: (l, 0, j)),
        ],
        out_specs=pl.BlockSpec((tm, tn), lambda i, j: (i, j)),
        out_shape=jax.ShapeDtypeStruct((n, nout), F32),
        scratch_shapes=[pltpu.VMEM((tm, d), BF16)],
        compiler_params=_cparams(("arbitrary", "arbitrary"), 48),
    )(x, g.reshape(1, d), mod, mod, w_bf16)


def _conv_kernel(ap_ref, gp_ref, a_ref, gt_ref, an_ref, gn_ref, w_ref, b_ref, lg_ref, lb_ref,
                 o_ref, buf_ref, acc_ref, *, lat_tiles, tiles_per_seq):
    i = pl.program_id(0)
    tc, ch = a_ref.shape
    is_lat = i < lat_tiles
    pos = i % tiles_per_seq
    first = jnp.logical_or(jnp.logical_not(is_lat), pos == 0)
    last = jnp.logical_or(jnp.logical_not(is_lat), pos == tiles_per_seq - 1)

    buf_ref[0:HALO] = jnp.where(first, 0.0, ap_ref[...] * _sigmoid(gp_ref[...]))
    buf_ref[HALO:HALO + tc] = a_ref[...] * _sigmoid(gt_ref[...])
    buf_ref[HALO + tc:2 * HALO + tc] = jnp.where(last, 0.0, an_ref[...] * _sigmoid(gn_ref[...]))

    rows = CONV_ROWS
    base = HALO - CONV_K // 2
    for c in range(ch // LANES):
        cs = slice(c * LANES, (c + 1) * LANES)
        for r in range(tc // rows):
            acc = None
            for res in range(SUBLANES):
                y = None
                for k in range(CONV_K):
                    if (base + k) % SUBLANES != res:
                        continue
                    lo = r * rows + (base + k) // SUBLANES * SUBLANES
                    term = w_ref[k:k + 1, cs] * buf_ref[lo:lo + rows + SUBLANES, cs]
                    y = term if y is None else y + term
                if y is not None:
                    y = y[res:res + rows]
                    acc = y if acc is None else acc + y
            acc_ref[r * rows:(r + 1) * rows, cs] = acc

    h = acc_ref[...] + b_ref[...]
    mu = jnp.mean(h, axis=-1, keepdims=True)
    var = jnp.mean(jnp.square(h - mu), axis=-1, keepdims=True)
    y = (h - mu) * lax.rsqrt(var + EPS) * lg_ref[...] + lb_ref[...]
    o_ref[...] = _silu(y).astype(o_ref.dtype)


def _conv_module(u, w_dw, b_dw, ln_g, ln_b, n_rows, lat_rows, seq):
    ch = w_dw.shape[1]
    tc = ROW_TILE
    per = tc // HALO
    nh = u.shape[0] // HALO
    kern = functools.partial(_conv_kernel, lat_tiles=lat_rows // tc, tiles_per_seq=seq // tc)
    prev_map = lambda c: (lambda i: (jnp.maximum(i * per - 1, 0), c))
    next_map = lambda c: (lambda i: (jnp.minimum((i + 1) * per, nh - 1), c))
    vec = lambda a: a.reshape(1, ch)
    return pl.pallas_call(
        kern,
        grid=(n_rows // tc,),
        in_specs=[
            pl.BlockSpec((HALO, ch), prev_map(0)),
            pl.BlockSpec((HALO, ch), prev_map(1)),
            pl.BlockSpec((tc, ch), lambda i: (i, 0)),
            pl.BlockSpec((tc, ch), lambda i: (i, 1)),
            pl.BlockSpec((HALO, ch), next_map(0)),
            pl.BlockSpec((HALO, ch), next_map(1)),
            pl.BlockSpec((CONV_K, ch), lambda i: (0, 0)),
            pl.BlockSpec((1, ch), lambda i: (0, 0)),
            pl.BlockSpec((1, ch), lambda i: (0, 0)),
            pl.BlockSpec((1, ch), lambda i: (0, 0)),
        ],
        out_specs=pl.BlockSpec((tc, ch), lambda i: (i, 0)),
        out_shape=jax.ShapeDtypeStruct((n_rows, ch), BF16),
        scratch_shapes=[pltpu.VMEM((tc + 2 * HALO, ch), F32), pltpu.VMEM((tc, ch), F32)],
        compiler_params=_cparams(("arbitrary",), 16),
    )(u, u, u, u, u, u, w_dw, vec(b_dw), vec(ln_g), vec(ln_b))


def _na_bias_tables(rows):
    groups = rows // NA_ROWS
    reps = [0, min(1, groups - 1), groups - 1]
    out = []
    for g in reps:
        start = int(np.clip(NA_ROWS * g - NA_KH // 2, 0, rows - NA_KEY_ROWS))
        per_row = []
        for i in range(NA_ROWS):
            r = NA_ROWS * g + i
            sr = int(np.clip(r - NA_KH // 2, 0, rows - NA_KH))
            per_row.append((sr - start, sr - r + NA_KH - 1))
        out.append(per_row)
    return out


def _na_bias(rpb, rows):
    nh = rpb.shape[0]
    ndr, ndc = 2 * NA_KH - 1, 2 * NA_KW - 1
    period = 2 * GRID_W - 1
    pad = GRID_W - NA_KW
    vp = jnp.pad(rpb.astype(F32), ((0, 0), (0, 0), (pad, period - ndc - pad)))
    hank = jnp.tile(vp, (1, 1, GRID_W + 1))[:, :, :GRID_W * (period + 1)]
    hank = hank.reshape(nh, ndr, GRID_W, period + 1)[..., :GRID_W]
    toe = hank[:, :, ::-1, :]
    c = np.arange(GRID_W)[:, None]
    j = np.arange(GRID_W)[None, :]
    ws = np.clip(c - NA_KW // 2, 0, GRID_W - NA_KW)
    col_ok = (j >= ws) & (j < ws + NA_KW)
    toe = jnp.where(col_ok[None, None], toe, NEG_INF)
    flat = jnp.transpose(toe, (0, 2, 1, 3)).reshape(nh, GRID_W, ndr * GRID_W)
    nk = NA_KEY_ROWS * GRID_W
    lpad = NA_ROWS * GRID_W
    total = 2 * nk

    def padded(shift):
        return jnp.pad(flat, ((0, 0), (0, 0), (lpad - shift, total - flat.shape[2] - lpad + shift)))

    tab = jnp.stack([padded(0), padded(GRID_W)])
    tab = tab.reshape(2, nh, GRID_W, total // LANES, LANES).transpose(0, 1, 3, 2, 4)
    tables = _na_bias_tables(rows)
    mask = np.full((len(tables), NA_ROWS, nk), NEG_INF, np.float32)
    dvals = []
    for cls, per_row in enumerate(tables):
        dvals.append(per_row[0][1] - per_row[0][0])
        for i, (off, lo) in enumerate(per_row):
            assert lo - off == dvals[-1] - i and -NA_ROWS <= lo - off < NA_ROWS
            mask[cls, i, off * GRID_W:(off + NA_KH) * GRID_W] = 0.0
    assert len({d % 2 for d in dvals}) == 1
    assert len(tables) < 3 or rows < 3 * NA_ROWS or all(off == i for i, (off, _) in enumerate(tables[1]))
    return tab * LOG2_E, jnp.asarray(mask * LOG2_E), tuple(dvals)


def _softmax_pv(s_parts, v_parts):
    m = functools.reduce(jnp.maximum, [jnp.max(s, axis=-1, keepdims=True) for s in s_parts])
    acc, l = None, None
    for s, v in zip(s_parts, v_parts):
        p = jnp.exp2(s - m)
        ps = jnp.sum(p, axis=-1, keepdims=True)
        pv = _dot(p.astype(BF16), v)
        l = ps if l is None else l + ps
        acc = pv if acc is None else acc + pv
    return acc / l


def _na_head_edge(s_raw, s_ctx, vs, vc, tab_ref, mask_ref, a, dcls, parity, pad_blocks):
    kw = s_raw[0].shape[1]
    bpk = kw // LANES
    s_parts = []
    for m, s in enumerate(s_raw):
        row_blocks = []
        for i in range(NA_ROWS):
            copy = (parity - i) % 2
            first = (dcls - i - copy + 2 * pad_blocks) // 2 + m * bpk
            bias = jnp.concatenate([tab_ref[copy, a, first + t] for t in range(bpk)], axis=1)
            bias = bias + mask_ref[i:i + 1, m * kw:(m + 1) * kw]
            row_blocks.append(s[i * GRID_W:(i + 1) * GRID_W, :] + bias)
        s_parts.append(jnp.concatenate(row_blocks, axis=0))
    return _softmax_pv(s_parts + [s_ctx], vs + [vc])


def _na_interior_rows(nchunks, kw):
    out = []
    for m in range(nchunks):
        k_lo, k_hi = m * kw // GRID_W, (m + 1) * kw // GRID_W
        i_lo, i_hi = max(0, k_lo - NA_KH + 1), min(NA_ROWS - 1, k_hi - 1)
        out.append((i_lo * GRID_W, (i_hi + 1) * GRID_W))
    return out


def _na_head_interior(s_raw, s_ctx, vs, vc, tab_ref, mask_ref, a, d0, pad_blocks):
    kw = s_raw[0].shape[1]
    bpk = kw // LANES
    nblk = len(s_raw) * bpk
    row_ranges = _na_interior_rows(len(s_raw), kw)
    p_rows, pc_rows, l_rows = [], [], []
    for i in range(NA_ROWS):
        rs = slice(i * GRID_W, (i + 1) * GRID_W)
        lo_lane, hi_lane = i * GRID_W, (i + NA_KH) * GRID_W
        b_lo, b_hi = lo_lane // LANES, -(-hi_lane // LANES)
        copy = (d0 - i) % 2
        first = (d0 - i - copy + 2 * pad_blocks) // 2
        blocks = []
        for b in range(b_lo, b_hi):
            ls = slice((b % bpk) * LANES, (b % bpk + 1) * LANES)
            r0 = row_ranges[b // bpk][0]
            sb = s_raw[b // bpk][rs.start - r0:rs.stop - r0, ls] + tab_ref[copy, a, first + b]
            if b * LANES < lo_lane or (b + 1) * LANES > hi_lane:
                sb = sb + mask_ref[i:i + 1, b * LANES:(b + 1) * LANES]
            blocks.append(sb)
        sw = jnp.concatenate(blocks, axis=1)
        sc = s_ctx[rs]
        m = jnp.maximum(jnp.max(sw, axis=-1, keepdims=True), jnp.max(sc, axis=-1, keepdims=True))
        pw = jnp.exp2(sw - m)
        pc = jnp.exp2(sc - m)
        l_rows.append(jnp.sum(pw, axis=-1, keepdims=True) + jnp.sum(pc, axis=-1, keepdims=True))
        pieces = [jnp.zeros((GRID_W, b_lo * LANES), BF16), pw.astype(BF16),
                  jnp.zeros((GRID_W, (nblk - b_hi) * LANES), BF16)]
        p_rows.append(jnp.concatenate([x for x in pieces if x.shape[1]], axis=1))
        pc_rows.append(pc.astype(BF16))
    p = jnp.concatenate(p_rows, axis=0)
    acc = _dot(jnp.concatenate(pc_rows, axis=0), vc)
    nq = p.shape[0]
    for m, v in enumerate(vs):
        r0, r1 = row_ranges[m]
        part = _dot(p[r0:r1, m * kw:(m + 1) * kw], v)
        pieces = [jnp.zeros((r0, part.shape[1]), F32), part, jnp.zeros((nq - r1, part.shape[1]), F32)]
        acc = acc + jnp.concatenate([x for x in pieces if x.shape[0]], axis=0)
    return acc / jnp.concatenate(l_rows, axis=0)


def _na_kernel(q_ref, k0, k1, k2, k3, v0, v1, v2, v3, kc_ref, vc_ref, tab_ref, mask_ref, o_ref, *,
               groups, dvals):
    g = pl.program_id(1)
    is_edge = jnp.logical_or(g == 0, g == groups - 1)
    pad_blocks = NA_ROWS * GRID_W // LANES

    def step(interior):
        q2 = q_ref[...] * (NA_HD ** -0.5 * LOG2_E)
        lane = lax.broadcasted_iota(jnp.int32, q2.shape, 1)
        ks = [k[...].astype(BF16) for k in (k0, k1, k2, k3)]
        vs = [v[...].astype(BF16) for v in (v0, v1, v2, v3)]
        kc = kc_ref[...].astype(BF16)
        vc = vc_ref[...].astype(BF16)
        nheads = q2.shape[1] // NA_HD

        def scores(a):
            sel = jnp.logical_and(lane >= a * NA_HD, lane < (a + 1) * NA_HD)
            qa = jnp.where(sel, q2, 0.0).astype(BF16)
            if interior:
                ranges = _na_interior_rows(len(ks), ks[0].shape[0])
                return [_dot_nt(qa[r0:r1], k) for (r0, r1), k in zip(ranges, ks)], _dot_nt(qa, kc)
            return [_dot_nt(qa, k) for k in ks], _dot_nt(qa, kc)

        pending = [scores(a) for a in range(min(NA_AHEAD, nheads))]
        out = None
        for a in range(nheads):
            if a + NA_AHEAD < nheads:
                pending.append(scores(a + NA_AHEAD))
            s_raw, s_ctx = pending[a]
            if interior:
                o = _na_head_interior(s_raw, s_ctx, vs, vc, tab_ref, mask_ref, a, dvals[1], pad_blocks)
            else:
                dcls = jnp.where(g == 0, dvals[0], dvals[2])
                o = _na_head_edge(s_raw, s_ctx, vs, vc, tab_ref, mask_ref, a, dcls, dvals[0], pad_blocks)
            out = o if out is None else jnp.where(lane >= a * NA_HD, o, out)
        o_ref[...] = out.astype(o_ref.dtype)

    @pl.when(is_edge)
    def _():
        step(False)

    @pl.when(jnp.logical_not(is_edge))
    def _():
        step(True)


def _na_latent(u, bias, head_base, nbatch, seq, lat_rows, off_na):
    tab, mask, dvals = bias
    rows = seq // GRID_W
    groups = rows // NA_ROWS
    nq = NA_ROWS * GRID_W
    kblk = ROW_TILE
    nkb = NA_KEY_ROWS * GRID_W // kblk
    assert nkb == 4
    hp = NA_STEP_HEADS
    wid = hp * NA_HD
    qcol = off_na // wid
    hsteps = NA_HEADS // hp
    kcol, vcol = qcol + hsteps, qcol + 2 * hsteps
    assert off_na % wid == 0 and NA_HEADS % hp == 0 and head_base % hp == 0
    kb_per_batch = seq // kblk
    kb_per_grow = GRID_W * NA_ROWS // kblk
    lat_kb = lat_rows // kblk

    def kmap(col, m):
        def f(h, g, b):
            st = jnp.clip(g * kb_per_grow - (NA_KH // 2) * GRID_W // kblk, 0, kb_per_batch - nkb)
            return (b * kb_per_batch + st + m, col + h)
        return f

    def mask_map(h, g, b):
        return (jnp.where(g == 0, 0, jnp.where(g == groups - 1, 2, 1)), 0, 0)

    in_specs = [pl.BlockSpec((nq, wid), lambda h, g, b: (b * groups + g, qcol + h))]
    in_specs += [pl.BlockSpec((kblk, wid), kmap(kcol, m)) for m in range(nkb)]
    in_specs += [pl.BlockSpec((kblk, wid), kmap(vcol, m)) for m in range(nkb)]
    in_specs += [pl.BlockSpec((ROW_TILE, wid), lambda h, g, b: (lat_kb + b, kcol + h)),
                 pl.BlockSpec((ROW_TILE, wid), lambda h, g, b: (lat_kb + b, vcol + h)),
                 pl.BlockSpec((2, hp) + tab.shape[2:], lambda h, g, b: (0, head_base // hp + h, 0, 0, 0)),
                 pl.BlockSpec((None,) + mask.shape[1:], mask_map)]
    return pl.pallas_call(
        functools.partial(_na_kernel, groups=groups, dvals=dvals),
        grid=(hsteps, groups, nbatch),
        in_specs=in_specs,
        out_specs=pl.BlockSpec((nq, wid), lambda h, g, b: (b * groups + g, h)),
        out_shape=jax.ShapeDtypeStruct((lat_rows, NA_HEADS * NA_HD), BF16),
        compiler_params=_cparams(("arbitrary", "arbitrary", "arbitrary"), 48),
    )(*([u] * 11), tab, mask)


def _ctx_attn_kernel(q_ref, k_ref, v_ref, o_ref):
    q2 = q_ref[...] * (NA_HD ** -0.5 * LOG2_E)
    lane = lax.broadcasted_iota(jnp.int32, q2.shape, 1)
    k = k_ref[...].astype(BF16)
    v = v_ref[...].astype(BF16)
    outs = []
    for a in range(2):
        sel = (lane < NA_HD) if a == 0 else (lane >= NA_HD)
        qa = jnp.where(sel, q2, 0.0).astype(BF16)
        outs.append(_softmax_pv([_dot_nt(qa, k)], [v]))
    o_ref[...] = jnp.where(lane < NA_HD, outs[0], outs[1]).astype(o_ref.dtype)


def _ctx_attn(u, nbatch, ctx_len, lat_rows, off_na):
    assert ctx_len == ROW_TILE
    qcol = off_na // LANES
    heads2 = NA_HEADS * NA_HD // LANES
    base = lat_rows // ROW_TILE
    spec = lambda col: pl.BlockSpec((ROW_TILE, LANES), lambda b, h: (base + b, col + h))
    return pl.pallas_call(
        _ctx_attn_kernel,
        grid=(nbatch, heads2),
        in_specs=[spec(qcol), spec(qcol + heads2), spec(qcol + 2 * heads2)],
        out_specs=pl.BlockSpec((ROW_TILE, LANES), lambda b, h: (b, h)),
        out_shape=jax.ShapeDtypeStruct((nbatch * ctx_len, NA_HEADS * NA_HD), BF16),
        compiler_params=_cparams(("arbitrary", "arbitrary"), 16),
    )(u, u, u)


def _hg_level_map(rev):
    size = HG_BLOCK // 2
    t = np.arange(size)[:, None]
    s = np.arange(size)[None, :]
    x = t ^ s
    lvl = np.where(x > 0, np.frexp(np.maximum(x, 1))[1] - 1, -1)
    causal = (s < t) if not rev else (s > t)
    out = np.where(causal, lvl, -1)
    out = np.where(t == s, int(np.log2(size)), out)
    return out.astype(np.int32)


def _hg_tri(rev):
    t = np.arange(HG_BLOCK)[:, None]
    s = np.arange(HG_BLOCK)[None, :]
    return ((s <= t) if not rev else (s >= t)).astype(np.float32)


def _hg_anchor(b3, m, rev):
    nv = b3.shape[0]
    if m >= SUBLANES:
        w = m // SUBLANES
        b4 = b3.reshape(nv // (2 * w), 2 * w, SUBLANES, LANES)
        a = b4[:, w:w + 1, 0:1, :] if rev else b4[:, w - 1:w, SUBLANES - 1:SUBLANES, :]
        return jnp.broadcast_to(a, b4.shape).reshape(b3.shape)
    sub = lax.broadcasted_iota(jnp.int32, b3.shape, 1)
    out = None
    for g in range(SUBLANES // (2 * m)):
        idx = g * 2 * m + (m if rev else m - 1)
        a = jnp.broadcast_to(b3[:, idx:idx + 1, :], b3.shape)
        out = a if out is None else jnp.where(sub >= g * 2 * m, a, out)
    return out


def _neg_abs(x):
    bits = lax.bitcast_convert_type(x, jnp.uint32) | jnp.uint32(0x80000000)
    return lax.bitcast_convert_type(bits, F32)


def _hg_pick(q3, k3, m, rev):
    nv = q3.shape[0]
    if m >= SUBLANES:
        w = m // SUBLANES
        shape4 = (nv // (2 * w), 2 * w, SUBLANES, LANES)
        q4, k4 = q3.reshape(shape4), k3.reshape(shape4)
        lower, upper = (q4, k4) if rev else (k4, q4)
        return jnp.concatenate([lower[:, :w], upper[:, w:]], axis=1).reshape(q3.shape)
    upper_rows = (lax.broadcasted_iota(jnp.int32, q3.shape, 1) & m) != 0
    return jnp.where(upper_rows, k3 if rev else q3, q3 if rev else k3)


def _hg_gates(q, z, alog, clog, oml, tri):
    q = _silu(q)
    t = jnp.exp(-jnp.abs(z))
    lsig = jnp.minimum(z, 0.0) - jnp.log(1.0 + t)
    cc = clog + lsig
    logf = jnp.maximum(alog, cc) + jnp.log(1.0 + jnp.exp(-jnp.abs(alog - cc)))
    kk = oml * jnp.where(z >= 0.0, t, 1.0) / (1.0 + t)

    hi = logf.astype(BF16)
    r1 = logf - hi.astype(F32)
    mid = r1.astype(BF16)
    lo = (r1 - mid.astype(F32)).astype(BF16)
    b = (_dot(tri, hi) + _dot(tri, mid) + _dot(tri, lo)) * LOG2_E
    return q, kk, b


def _hg_mix(q, kk, b, v, lv, st, rev):
    n = q.shape[0]
    half = n // 2
    nlev = int(np.log2(n))
    shape3 = (n // SUBLANES, SUBLANES, LANES)
    b3, q3, k3 = b.reshape(shape3), q.reshape(shape3), kk.reshape(shape3)
    halves = (slice(0, half), slice(half, n))
    qb, kb = q.astype(BF16), kk.astype(BF16)
    acc = [jnp.where(lv == nlev - 1, _dot_nt(qb[hs], kb[hs]), 0.0) for hs in halves]
    for lev in range(nlev - 1):
        m = 1 << lev
        e = jnp.exp2(_neg_abs(b3 - _hg_anchor(b3, m, rev)))
        w = (_hg_pick(q3, k3, m, rev) * e).reshape(n, LANES).astype(BF16)
        acc = [jnp.where(lv == lev, _dot_nt(w[hs], w[hs]), a) for hs, a in zip(halves, acc)]
    first, second = (halves[1], halves[0]) if rev else halves
    e = jnp.exp2(_neg_abs(b - (b[half:half + 1, :] if rev else b[half - 1:half, :])))
    top = _dot_nt((q[second] * e[second]).astype(BF16), (kk[first] * e[first]).astype(BF16))

    vb = v.astype(BF16)
    a0, a1 = acc[0].astype(BF16), acc[1].astype(BF16)
    tb = top.astype(BF16)
    if rev:
        o_lo = _dot(jnp.concatenate([a0, tb], axis=1), vb)
        o_hi = _dot(a1, vb[halves[1]])
    else:
        o_lo = _dot(a0, vb[halves[0]])
        o_hi = _dot(jnp.concatenate([tb, a1], axis=1), vb)
    b_last = b[0:1, :] if rev else b[n - 1:n, :]
    qh = (q * jnp.exp2(b)).astype(BF16)
    o = jnp.concatenate([o_lo, o_hi], axis=0) + _dot_nt(qh, st.astype(BF16))
    kh = (kk * jnp.exp2(b_last - b)).astype(BF16)
    st_new = st * jnp.exp2(b_last) + _dot_tn(vb, kh)
    return o, st_new


def _hg_gates_ahead(q_ref, z_ref, al_ref, cl_ref, om_ref, tri):
    def gates(h):
        hs = slice(h * HG_DK, (h + 1) * HG_DK)
        return _hg_gates(q_ref[:, hs], z_ref[:, hs], al_ref[:, hs], cl_ref[:, hs], om_ref[:, hs], tri)

    pending = [gates(h) for h in range(min(HG_AHEAD, HG_HEADS))]
    for h in range(HG_HEADS):
        if h + HG_AHEAD < HG_HEADS:
            pending.append(gates(h + HG_AHEAD))
        yield pending[h]


def _hg_fwd_kernel(q_ref, v_ref, z_ref, al_ref, cl_ref, om_ref, tri_ref, lv_ref, o_ref, st_ref):
    @pl.when(pl.program_id(1) == 0)
    def _():
        st_ref[...] = jnp.zeros_like(st_ref)

    lv = lv_ref[...]
    gates = _hg_gates_ahead(q_ref, z_ref, al_ref, cl_ref, om_ref, tri_ref[...])
    for h in range(HG_HEADS):
        hs = slice(h * HG_DK, (h + 1) * HG_DK)
        o, st = _hg_mix(*next(gates), v_ref[:, hs], lv, st_ref[h], False)
        o_ref[:, hs] = o
        st_ref[h] = st


def _hg_bwd_kernel(q_ref, v_ref, z_ref, g_ref, of_ref, al_ref, cl_ref, om_ref, ng_ref, tri_ref, lv_ref,
                   o_ref, st_ref):
    @pl.when(pl.program_id(1) == 0)
    def _():
        st_ref[...] = jnp.zeros_like(st_ref)

    lv = lv_ref[...]
    gates = _hg_gates_ahead(q_ref, z_ref, al_ref, cl_ref, om_ref, tri_ref[...])
    for h in range(HG_HEADS):
        hs = slice(h * HG_DK, (h + 1) * HG_DK)
        o, st = _hg_mix(*next(gates), v_ref[:, hs], lv, st_ref[h], True)
        st_ref[h] = st
        t = of_ref[:, hs] + o
        y = t * lax.rsqrt(jnp.mean(t * t, axis=-1, keepdims=True) + EPS)
        o_ref[:, hs] = (y * ng_ref[:, hs] * _silu(g_ref[:, hs])).astype(o_ref.dtype)


def _hgrn(u, lb, norm_g, nbatch, seq, ctx_len, lat_rows, off_hg):
    assert ctx_len == HG_BLOCK
    n = u.shape[0]
    hd = HG_HEADS * HG_DK
    col = off_hg // hd
    per = seq // HG_BLOCK
    lat_blocks = lat_rows // HG_BLOCK
    lbf = lb.astype(F32)
    alog, clog, oml = jnp.log(lbf), jnp.log1p(-lbf), 1.0 - lbf

    def fmap(c):
        return lambda b, j: (jnp.where(j == 0, lat_blocks + b, b * per + j - 1), c)

    def bmap(c):
        return lambda b, j: (jnp.where(j == 0, lat_blocks + b, b * per + per - j), c)

    const = lambda shape: pl.BlockSpec(shape, lambda b, j: (0, 0))
    grid = (nbatch, per + 1)
    vec = lambda a: a.reshape(1, hd)
    o_f = pl.pallas_call(
        _hg_fwd_kernel,
        grid=grid,
        in_specs=[pl.BlockSpec((HG_BLOCK, hd), fmap(col)), pl.BlockSpec((HG_BLOCK, hd), fmap(col + 1)),
                  pl.BlockSpec((HG_BLOCK, hd), fmap(col + 2)),
                  const((1, hd)), const((1, hd)), const((1, hd)),
                  const((HG_BLOCK, HG_BLOCK)), const((HG_BLOCK // 2, HG_BLOCK // 2))],
        out_specs=pl.BlockSpec((HG_BLOCK, hd), fmap(0)),
        out_shape=jax.ShapeDtypeStruct((n, hd), F32),
        scratch_shapes=[pltpu.VMEM((HG_HEADS, HG_DK, HG_DK), F32)],
        compiler_params=_cparams(("arbitrary", "arbitrary"), 32),
    )(u, u, u, vec(alog[0]), vec(clog[0]), vec(oml[0]),
      jnp.asarray(_hg_tri(False), BF16), jnp.asarray(_hg_level_map(False)))
    return pl.pallas_call(
        _hg_bwd_kernel,
        grid=grid,
        in_specs=[pl.BlockSpec((HG_BLOCK, hd), bmap(col)), pl.BlockSpec((HG_BLOCK, hd), bmap(col + 1)),
                  pl.BlockSpec((HG_BLOCK, hd), bmap(col + 3)), pl.BlockSpec((HG_BLOCK, hd), bmap(col + 4)),
                  pl.BlockSpec((HG_BLOCK, hd), bmap(0)),
                  const((1, hd)), const((1, hd)), const((1, hd)), const((1, hd)),
                  const((HG_BLOCK, HG_BLOCK)), const((HG_BLOCK // 2, HG_BLOCK // 2))],
        out_specs=pl.BlockSpec((HG_BLOCK, hd), bmap(0)),
        out_shape=jax.ShapeDtypeStruct((n, hd), BF16),
        scratch_shapes=[pltpu.VMEM((HG_HEADS, HG_DK, HG_DK), F32)],
        compiler_params=_cparams(("arbitrary", "arbitrary"), 32),
    )(u, u, u, u, o_f, vec(alog[1]), vec(clog[1]), vec(oml[1]), vec(norm_g.astype(F32)),
      jnp.asarray(_hg_tri(True), BF16), jnp.asarray(_hg_level_map(True)))


def _out_kernel(x_ref, cv_ref, na_ref, hg_ref, w_ref, ga_ref, g2_ref, sh2_ref, s2_ref, wrh_ref, wrl_ref, br_ref,
                xo_ref, h_ref, rt_ref, *, tiles_per_batch, nbatch):
    r = jnp.minimum(pl.program_id(0) // tiles_per_batch, nbatch)
    c0 = cv_ref.shape[1]
    c1 = c0 + na_ref.shape[1]
    mix = (_dot(cv_ref[...], w_ref[0:c0, :]) + _dot(na_ref[...], w_ref[c0:c1, :])
           + _dot(hg_ref[...], w_ref[c1:, :]))
    xn = x_ref[...] + ga_ref[pl.ds(r, 1), :] * mix
    xo_ref[...] = xn
    h = _rms_mod(xn, g2_ref[...], s2_ref[pl.ds(r, 1), :], sh2_ref[pl.ds(r, 1), :])
    h_ref[...] = h.astype(h_ref.dtype)
    h_hi, h_lo = _split_bf16(h)
    logits = (_dot(h_hi, wrh_ref[...]) + (_dot(h_lo, wrh_ref[...]) + _dot(h_hi, wrl_ref[...]))
              + br_ref[...])
    rt_ref[...] = _route_rows(logits)


def _route_rows(lg):
    lane = lax.broadcasted_iota(jnp.int32, lg.shape, 1)
    big = jnp.int32(2 ** 30)
    low = jnp.float32(-3e38)

    def first_max(vals, mask):
        m = jnp.max(vals, axis=-1, keepdims=True)
        idx = jnp.min(jnp.where(jnp.logical_and(vals == m, mask), lane, big), axis=-1, keepdims=True)
        return m, idx

    gmask = lane < N_GROUPS
    gl = jnp.where(gmask, lg, low)
    gm, grp = first_max(gl, gmask)
    p_grp = 1.0 / jnp.sum(jnp.where(gmask, jnp.exp(gl - gm), 0.0), axis=-1, keepdims=True)
    lo = N_GROUPS + grp * EXP_PER_GROUP
    emask = jnp.logical_and(lane >= lo, lane < lo + EXP_PER_GROUP)
    el = jnp.where(emask, lg, low)
    m1, i1 = first_max(el, emask)
    emask2 = jnp.logical_and(emask, lane != i1)
    el2 = jnp.where(emask2, lg, low)
    m2, i2 = first_max(el2, emask2)
    t = jnp.exp(m2 - m1)
    w1 = p_grp / (1.0 + t)
    w2 = p_grp * t / (1.0 + t)
    e1 = (i1 - N_GROUPS).astype(F32)
    e2 = (i2 - N_GROUPS).astype(F32)
    return jnp.where(lane == 0, e1, jnp.where(lane == 1, e2, jnp.where(lane == 2, w1,
                     jnp.where(lane == 3, w2, 0.0))))


def _out_proj(x, conv, na, hg, w_bf16, l, mod, g_ffn, w_router, b_router, n_rows, nbatch, lat_rows):
    d = x.shape[1]
    w_router_hi, w_router_lo = _split_bf16(w_router)
    tm = _pick_tile(OUT_TM, lat_rows // nbatch, n_rows)
    kern = functools.partial(_out_kernel, tiles_per_batch=lat_rows // nbatch // tm, nbatch=nbatch)
    row = lambda w: pl.BlockSpec((tm, w), lambda i: (i, 0))
    const = lambda shape: pl.BlockSpec(shape, lambda i: (0, 0))
    modc = lambda c: pl.BlockSpec((None, SUBLANES, d), lambda i: (l, 0, c))
    return pl.pallas_call(
        kern,
        grid=(n_rows // tm,),
        in_specs=[row(d), row(conv.shape[1]), row(na.shape[1]), row(hg.shape[1]),
                  pl.BlockSpec((None, d, d), lambda i: (l, 0, 0)),
                  modc(2),
                  const((1, d)),
                  modc(3),
                  modc(4),
                  const((d, ROUTER_PAD)), const((d, ROUTER_PAD)), const((1, ROUTER_PAD))],
        out_specs=[row(d), row(d), row(ROUTER_PAD)],
        out_shape=[jax.ShapeDtypeStruct((n_rows, d), F32), jax.ShapeDtypeStruct((n_rows, d), BF16),
                   jax.ShapeDtypeStruct((n_rows, ROUTER_PAD), F32)],
        compiler_params=_cparams(("arbitrary",), 56),
    )(x, conv, na, hg, w_bf16, mod, g_ffn.reshape(1, d), mod, mod, w_router_hi, w_router_lo, b_router)


def _moe_kernel(be_ref, nu_ref, xs_ref, w1_ref, w3_ref, w2_ref, sw_ref, o_ref, w1b, w3b, w2b):
    i = pl.program_id(0)
    e = be_ref[i]
    prev = be_ref[jnp.maximum(i - 1, 0)]

    @pl.when(jnp.logical_or(i == 0, e != prev))
    def _():
        w1b[...] = w1_ref[...].astype(BF16)
        w3b[...] = w3_ref[...].astype(BF16)
        w2b[...] = w2_ref[...].astype(BF16)

    @pl.when(i < nu_ref[0])
    def _():
        x = xs_ref[...]
        a = (_silu(_dot(x, w1b[...])) * _dot(x, w3b[...])).astype(BF16)
        o_ref[...] = (_dot(a, w2b[...]) * sw_ref[...]).astype(o_ref.dtype)

    @pl.when(i >= nu_ref[0])
    def _():
        o_ref[...] = jnp.zeros_like(o_ref)


def _moe_experts(xs, slot_w, blk_e, nused, w1, w3, w2, l):
    p, d = xs.shape
    de = w1.shape[3]
    bm = MOE_BM
    grid_spec = pltpu.PrefetchScalarGridSpec(
        num_scalar_prefetch=2,
        grid=(p // bm,),
        in_specs=[pl.BlockSpec((bm, d), lambda i, be, nu: (i, 0)),
                  pl.BlockSpec((None, None, d, de), lambda i, be, nu: (l, be[i], 0, 0)),
                  pl.BlockSpec((None, None, d, de), lambda i, be, nu: (l, be[i], 0, 0)),
                  pl.BlockSpec((None, None, de, d), lambda i, be, nu: (l, be[i], 0, 0)),
                  pl.BlockSpec((bm, 1), lambda i, be, nu: (i, 0))],
        out_specs=pl.BlockSpec((bm, d), lambda i, be, nu: (i, 0)),
        scratch_shapes=[pltpu.VMEM((d, de), BF16), pltpu.VMEM((d, de), BF16), pltpu.VMEM((de, d), BF16)],
    )
    return pl.pallas_call(
        _moe_kernel,
        grid_spec=grid_spec,
        out_shape=jax.ShapeDtypeStruct((p, d), BF16),
        compiler_params=_cparams(("arbitrary",), 48),
    )(blk_e, nused, xs, w1, w3, w2, slot_w.reshape(p, 1))


def _rows(a, idx):
    return a.at[idx].get(mode="promise_in_bounds")


def _route_meta(route, n):
    i32 = jnp.int32
    eid = route[:, 0:TOP_K].astype(i32).reshape(-1)
    wt = route[:, TOP_K:2 * TOP_K].reshape(-1)
    a = n * TOP_K
    bm = MOE_BM
    nblk = -(-a // bm) + N_EXPERTS
    p = nblk * bm
    experts = jnp.arange(N_EXPERTS, dtype=i32)[None, :]
    ja = jnp.arange(a, dtype=i32)
    se, order, wsort = lax.sort((eid, ja, wt), num_keys=1, is_stable=True)
    cnt = jnp.sum((eid[:, None] == experts).astype(i32), axis=0)
    pcnt = (cnt + bm - 1) // bm * bm
    pend = jnp.cumsum(pcnt)
    pstart = pend - pcnt
    end = jnp.cumsum(cnt)
    start = end - cnt
    off = pstart - start
    d_off = off - jnp.concatenate([jnp.zeros((1,), i32), off[:-1]])
    dst_sorted = ja + jnp.sum(jnp.where(ja[:, None] >= start[None, :], d_off[None, :], 0), axis=1)
    _, pos = lax.sort((order, dst_sorted), num_keys=1)
    jp = jnp.arange(p, dtype=i32)
    in_or_after = jp[:, None] >= pstart[None, :]
    src = jp - jnp.sum(jnp.where(in_or_after, d_off[None, :], 0), axis=1)
    valid = src < jnp.sum(jnp.where(in_or_after, cnt[None, :], 0), axis=1)
    src = jnp.where(valid, src, jp % a)
    slot_tok = _rows(order, src) // TOP_K
    slot_w = jnp.where(valid, _rows(wsort, src), 0.0)
    jb = jnp.arange(nblk, dtype=i32) * bm
    blk_e = jnp.minimum(jnp.sum((jb[:, None] >= pend[None, :]).astype(i32), axis=1), N_EXPERTS - 1)
    nused = (pend[-1:] // bm).astype(i32)
    return slot_tok, slot_w, pos.reshape(n, TOP_K), blk_e, nused


def _combine_kernel(x_ref, y0_ref, y1_ref, ga_ref, gf_ref, o_ref, *, tiles_per_batch, nbatch, final):
    r = jnp.minimum(pl.program_id(0) // tiles_per_batch, nbatch)
    xn = x_ref[...] + ga_ref[pl.ds(r, 1), :] * (y0_ref[...].astype(F32) + y1_ref[...].astype(F32))
    if final:
        xn = xn * lax.rsqrt(jnp.mean(xn * xn, axis=-1, keepdims=True) + EPS) * gf_ref[...]
    o_ref[...] = xn


def _combine(x, y0, y1, mod, l, g_final, n_rows, nbatch, lat_rows, final):
    d = x.shape[1]
    tm = _pick_tile(OUT_TM, lat_rows // nbatch, n_rows)
    kern = functools.partial(_combine_kernel, tiles_per_batch=lat_rows // nbatch // tm, nbatch=nbatch,
                             final=final)
    row = pl.BlockSpec((tm, d), lambda i: (i, 0))
    return pl.pallas_call(
        kern,
        grid=(n_rows // tm,),
        in_specs=[row, row, row, pl.BlockSpec((None, SUBLANES, d), lambda i: (l, 0, 5)),
                  pl.BlockSpec((1, d), lambda i: (0, 0))],
        out_specs=row,
        out_shape=jax.ShapeDtypeStruct((n_rows, d), F32),
        compiler_params=_cparams(("arbitrary",), 40),
    )(x, y0, y1, mod, g_final.reshape(1, d))


def kernel(x, c, ctx, c_ctx, w_ada, b_ada, g_mix, g_ffn, w_in, conv_w, conv_b, conv_ln_g, conv_ln_b,
           na_rpb, hgrn_lb, hgrn_norm_g, w_out, w_router_group, b_router_group, w_router_expert,
           b_router_expert, w_exp_gate, w_exp_up, w_exp_down, g_final):
    nb, seq, d = x.shape
    ctx_len = ctx.shape[1]
    depth = w_ada.shape[0]
    lat_rows = nb * seq
    n_all = lat_rows + nb * ctx_len
    conv_ch = conv_w.shape[2]
    off_na = 2 * conv_ch
    off_hg = off_na + 3 * NA_HEADS * NA_HD
    rows = seq // GRID_W
    assert nb < SUBLANES and rows % NA_ROWS == 0 and rows >= NA_KEY_ROWS

    lbs = jnp.cumsum(jax.nn.softmax(hgrn_lb.astype(F32), axis=0), axis=0)
    lbs = lbs - lbs[:1]

    cond = jnp.concatenate([c, c_ctx[None, :], jnp.zeros((SUBLANES - nb - 1, d), F32)], axis=0)
    mod = _ada_mod(cond, w_ada, b_ada)

    xs = jnp.concatenate([x.reshape(lat_rows, d), ctx.reshape(nb * ctx_len, d)], axis=0)
    w_in_b = w_in.astype(BF16)
    w_out_b = w_out.astype(BF16)
    na_bias = _na_bias(na_rpb.reshape((depth * NA_HEADS,) + na_rpb.shape[2:]), rows)
    for l in range(depth):
        with_ctx = l < depth - 1
        n_act = n_all if with_ctx else lat_rows
        u = _norm_in(xs, g_mix[l], mod, w_in_b, l, nb, lat_rows)

        conv = _conv_module(u, conv_w[l], conv_b[l], conv_ln_g[l], conv_ln_b[l], n_act, lat_rows, seq)
        na = _na_latent(u, na_bias, l * NA_HEADS, nb, seq, lat_rows, off_na)
        if with_ctx:
            na = jnp.concatenate([na, _ctx_attn(u, nb, ctx_len, lat_rows, off_na)], axis=0)
        hg = _hgrn(u, lbs[l], hgrn_norm_g[l], nb, seq, ctx_len, lat_rows, off_hg)

        w_router = jnp.concatenate(
            [w_router_group[l], w_router_expert[l],
             jnp.zeros((d, ROUTER_PAD - N_GROUPS - N_EXPERTS), F32)], axis=1)
        b_router = jnp.concatenate(
            [b_router_group[l], b_router_expert[l],
             jnp.zeros((ROUTER_PAD - N_GROUPS - N_EXPERTS,), F32)]).reshape(1, ROUTER_PAD)
        x_mid, h, route = _out_proj(xs, conv, na, hg, w_out_b, l, mod, g_ffn[l],
                                    w_router, b_router, n_act, nb, lat_rows)

        slot_tok, slot_w, pos, blk_e, nused = _route_meta(route, n_act)
        ys = _moe_experts(_rows(h, slot_tok), slot_w, blk_e, nused, w_exp_gate, w_exp_up, w_exp_down, l)
        y0 = _rows(ys, pos[:, 0])
        y1 = _rows(ys, pos[:, 1])
        xs = _combine(x_mid, y0, y1, mod, l, g_final, n_act, nb, lat_rows, final=not with_ctx)
    return xs.reshape(nb, seq, d)
```

```python
import functools

import numpy as np
import jax
import jax.numpy as jnp
from jax import lax
from jax.experimental import pallas as pl
from jax.experimental.pallas import tpu as pltpu

F32 = jnp.float32
BF16 = jnp.bfloat16

EPS = 1e-6
NEG_INF = -1e30
LOG2_E = 1.4426950408889634

GRID_W = 64
CONV_K = 31
NA_HEADS = 16
NA_HD = 64
NA_KH = 8
NA_KW = 16
HG_HEADS = 4
HG_DK = 128
N_GROUPS = 4
EXP_PER_GROUP = 8
N_EXPERTS = N_GROUPS * EXP_PER_GROUP
TOP_K = 2

LANES = 128
SUBLANES = 8

ROW_TILE = 256
ADA_TN = 1024
IN_TM = 1024
IN_TN = 512
OUT_TM = 512
CONV_ROWS = 64
NA_STEP_HEADS = 8
NA_AHEAD = 4
NA_ROWS = 8
NA_KEY_ROWS = 16
HG_BLOCK = 256
HG_AHEAD = 2
MOE_BM = 512
ROUTER_PAD = LANES
HALO = 16


def _pick_tile(pref, *extents):
    t = pref
    while t > ROW_TILE and any(e % t for e in extents):
        t //= 2
    assert all(e % t == 0 for e in extents)
    return t


def _cparams(sem, vmem_mb):
    return pltpu.CompilerParams(dimension_semantics=sem, vmem_limit_bytes=vmem_mb * 1024 * 1024)


def _dot(a, b):
    return jnp.dot(a, b, preferred_element_type=F32)


def _dot_nt(a, b):
    return lax.dot_general(a, b, (((1,), (1,)), ((), ())), preferred_element_type=F32)


def _dot_tn(a, b):
    return lax.dot_general(a, b, (((0,), (0,)), ((), ())), preferred_element_type=F32)


def _sigmoid(x):
    return 1.0 / (1.0 + jnp.exp(-x))


def _silu(x):
    return x * _sigmoid(x)


def _split_bf16(x):
    hi = x.astype(BF16)
    return hi, (x - hi.astype(F32)).astype(BF16)


def _ada_kernel(c_ref, w_ref, b_ref, o_ref):
    s_hi, s_lo = _split_bf16(_silu(c_ref[...]))
    w_hi, w_lo = _split_bf16(w_ref[0])
    o_ref[0] = _dot(s_hi, w_hi) + (_dot(s_lo, w_hi) + _dot(s_hi, w_lo)) + b_ref[0]


def _ada_mod(cond, w_ada, b_ada):
    depth, d, n = w_ada.shape
    tn = ADA_TN
    return pl.pallas_call(
        _ada_kernel,
        grid=(depth, n // tn),
        in_specs=[
            pl.BlockSpec((SUBLANES, d), lambda l, j: (0, 0)),
            pl.BlockSpec((1, d, tn), lambda l, j: (l, 0, j)),
            pl.BlockSpec((1, 1, tn), lambda l, j: (l, 0, j)),
        ],
        out_specs=pl.BlockSpec((1, SUBLANES, tn), lambda l, j: (l, 0, j)),
        out_shape=jax.ShapeDtypeStruct((depth, SUBLANES, n), F32),
        compiler_params=_cparams(("arbitrary", "arbitrary"), 40),
    )(cond, w_ada, b_ada.reshape(depth, 1, n))


def _rms_mod(x, g, scale, shift):
    y = x * lax.rsqrt(jnp.mean(x * x, axis=-1, keepdims=True) + EPS)
    return (y * g) * (1.0 + scale) + shift


def _norm_in_kernel(x_ref, g_ref, sh_ref, sc_ref, w_ref, o_ref, h_ref, *, tiles_per_batch, nbatch):
    i = pl.program_id(0)

    @pl.when(pl.program_id(1) == 0)
    def _():
        r = jnp.minimum(i // tiles_per_batch, nbatch)
        h = _rms_mod(x_ref[...], g_ref[...], sc_ref[pl.ds(r, 1), :], sh_ref[pl.ds(r, 1), :])
        h_ref[...] = h.astype(BF16)

    o_ref[...] = _dot(h_ref[...], w_ref[...])


def _norm_in(x, g, mod, w_bf16, l, nbatch, lat_rows):
    n, d = x.shape
    nout = w_bf16.shape[2]
    tm, tn = _pick_tile(IN_TM, lat_rows // nbatch, n), IN_TN
    kern = functools.partial(_norm_in_kernel, tiles_per_batch=lat_rows // nbatch // tm, nbatch=nbatch)
    return pl.pallas_call(
        kern,
        grid=(n // tm, nout // tn),
        in_specs=[
            pl.BlockSpec((tm, d), lambda i, j: (i, 0)),
            pl.BlockSpec((1, d), lambda i, j: (0, 0)),
            pl.BlockSpec((None, SUBLANES, d), lambda i, j: (l, 0, 0)),
            pl.BlockSpec((None, SUBLANES, d), lambda i, j: (l, 0, 1)),
            pl.BlockSpec((None, d, tn), lambda i, j: (l, 0, j)),
        ],
        out_specs=pl.BlockSpec((tm, tn), lambda i, j: (i, j)),
        out_shape=jax.ShapeDtypeStruct((n, nout), F32),
        scratch_shapes=[pltpu.VMEM((tm, d), BF16)],
        compiler_params=_cparams(("arbitrary", "arbitrary"), 48),
    )(x, g.reshape(1, d), mod, mod, w_bf16)


def _conv_kernel(ap_ref, gp_ref, a_ref, gt_ref, an_ref, gn_ref, w_ref, b_ref, lg_ref, lb_ref,
                 o_ref, buf_ref, acc_ref, *, lat_tiles, tiles_per_seq):
    i = pl.program_id(0)
    tc, ch = a_ref.shape
    is_lat = i < lat_tiles
    pos = i % tiles_per_seq
    first = jnp.logical_or(jnp.logical_not(is_lat), pos == 0)
    last = jnp.logical_or(jnp.logical_not(is_lat), pos == tiles_per_seq - 1)

    buf_ref[0:HALO] = jnp.where(first, 0.0, ap_ref[...] * _sigmoid(gp_ref[...]))
    buf_ref[HALO:HALO + tc] = a_ref[...] * _sigmoid(gt_ref[...])
    buf_ref[HALO + tc:2 * HALO + tc] = jnp.where(last, 0.0, an_ref[...] * _sigmoid(gn_ref[...]))

    rows = CONV_ROWS
    base = HALO - CONV_K // 2
    for c in range(ch // LANES):
        cs = slice(c * LANES, (c + 1) * LANES)
        for r in range(tc // rows):
            acc = None
            for res in range(SUBLANES):
                y = None
                for k in range(CONV_K):
                    if (base + k) % SUBLANES != res:
                        continue
                    lo = r * rows + (base + k) // SUBLANES * SUBLANES
                    term = w_ref[k:k + 1, cs] * buf_ref[lo:lo + rows + SUBLANES, cs]
                    y = term if y is None else y + term
                if y is not None:
                    y = y[res:res + rows]
                    acc = y if acc is None else acc + y
            acc_ref[r * rows:(r + 1) * rows, cs] = acc

    h = acc_ref[...] + b_ref[...]
    mu = jnp.mean(h, axis=-1, keepdims=True)
    var = jnp.mean(jnp.square(h - mu), axis=-1, keepdims=True)
    y = (h - mu) * lax.rsqrt(var + EPS) * lg_ref[...] + lb_ref[...]
    o_ref[...] = _silu(y).astype(o_ref.dtype)


def _conv_module(u, w_dw, b_dw, ln_g, ln_b, n_rows, lat_rows, seq):
    ch = w_dw.shape[1]
    tc = ROW_TILE
    per = tc // HALO
    nh = u.shape[0] // HALO
    kern = functools.partial(_conv_kernel, lat_tiles=lat_rows // tc, tiles_per_seq=seq // tc)
    prev_map = lambda c: (lambda i: (jnp.maximum(i * per - 1, 0), c))
    next_map = lambda c: (lambda i: (jnp.minimum((i + 1) * per, nh - 1), c))
    vec = lambda a: a.reshape(1, ch)
    return pl.pallas_call(
        kern,
        grid=(n_rows // tc,),
        in_specs=[
            pl.BlockSpec((HALO, ch), prev_map(0)),
            pl.BlockSpec((HALO, ch), prev_map(1)),
            pl.BlockSpec((tc, ch), lambda i: (i, 0)),
            pl.BlockSpec((tc, ch), lambda i: (i, 1)),
            pl.BlockSpec((HALO, ch), next_map(0)),
            pl.BlockSpec((HALO, ch), next_map(1)),
            pl.BlockSpec((CONV_K, ch), lambda i: (0, 0)),
            pl.BlockSpec((1, ch), lambda i: (0, 0)),
            pl.BlockSpec((1, ch), lambda i: (0, 0)),
            pl.BlockSpec((1, ch), lambda i: (0, 0)),
        ],
        out_specs=pl.BlockSpec((tc, ch), lambda i: (i, 0)),
        out_shape=jax.ShapeDtypeStruct((n_rows, ch), BF16),
        scratch_shapes=[pltpu.VMEM((tc + 2 * HALO, ch), F32), pltpu.VMEM((tc, ch), F32)],
        compiler_params=_cparams(("arbitrary",), 16),
    )(u, u, u, u, u, u, w_dw, vec(b_dw), vec(ln_g), vec(ln_b))


def _na_bias_tables(rows):
    groups = rows // NA_ROWS
    reps = [0, min(1, groups - 1), groups - 1]
    out = []
    for g in reps:
        start = int(np.clip(NA_ROWS * g - NA_KH // 2, 0, rows - NA_KEY_ROWS))
        per_row = []
        for i in range(NA_ROWS):
            r = NA_ROWS * g + i
            sr = int(np.clip(r - NA_KH // 2, 0, rows - NA_KH))
            per_row.append((sr - start, sr - r + NA_KH - 1))
        out.append(per_row)
    return out


def _na_bias(rpb, rows):
    nh = rpb.shape[0]
    ndr, ndc = 2 * NA_KH - 1, 2 * NA_KW - 1
    period = 2 * GRID_W - 1
    pad = GRID_W - NA_KW
    vp = jnp.pad(rpb.astype(F32), ((0, 0), (0, 0), (pad, period - ndc - pad)))
    hank = jnp.tile(vp, (1, 1, GRID_W + 1))[:, :, :GRID_W * (period + 1)]
    hank = hank.reshape(nh, ndr, GRID_W, period + 1)[..., :GRID_W]
    toe = hank[:, :, ::-1, :]
    c = np.arange(GRID_W)[:, None]
    j = np.arange(GRID_W)[None, :]
    ws = np.clip(c - NA_KW // 2, 0, GRID_W - NA_KW)
    col_ok = (j >= ws) & (j < ws + NA_KW)
    toe = jnp.where(col_ok[None, None], toe, NEG_INF)
    flat = jnp.transpose(toe, (0, 2, 1, 3)).reshape(nh, GRID_W, ndr * GRID_W)
    nk = NA_KEY_ROWS * GRID_W
    lpad = NA_ROWS * GRID_W
    total = 2 * nk

    def padded(shift):
        return jnp.pad(flat, ((0, 0), (0, 0), (lpad - shift, total - flat.shape[2] - lpad + shift)))

    tab = jnp.stack([padded(0), padded(GRID_W)])
    tab = tab.reshape(2, nh, GRID_W, total // LANES, LANES).transpose(0, 1, 3, 2, 4)
    tables = _na_bias_tables(rows)
    mask = np.full((len(tables), NA_ROWS, nk), NEG_INF, np.float32)
    dvals = []
    for cls, per_row in enumerate(tables):
        dvals.append(per_row[0][1] - per_row[0][0])
        for i, (off, lo) in enumerate(per_row):
            assert lo - off == dvals[-1] - i and -NA_ROWS <= lo - off < NA_ROWS
            mask[cls, i, off * GRID_W:(off + NA_KH) * GRID_W] = 0.0
    assert len({d % 2 for d in dvals}) == 1
    assert len(tables) < 3 or rows < 3 * NA_ROWS or all(off == i for i, (off, _) in enumerate(tables[1]))
    return tab * LOG2_E, jnp.asarray(mask * LOG2_E), tuple(dvals)


def _softmax_pv(s_parts, v_parts):
    m = functools.reduce(jnp.maximum, [jnp.max(s, axis=-1, keepdims=True) for s in s_parts])
    acc, l = None, None
    for s, v in zip(s_parts, v_parts):
        p = jnp.exp2(s - m)
        ps = jnp.sum(p, axis=-1, keepdims=True)
        pv = _dot(p.astype(BF16), v)
        l = ps if l is None else l + ps
        acc = pv if acc is None else acc + pv
    return acc / l


def _na_head_edge(s_raw, s_ctx, vs, vc, tab_ref, mask_ref, a, dcls, parity, pad_blocks):
    kw = s_raw[0].shape[1]
    bpk = kw // LANES
    s_parts = []
    for m, s in enumerate(s_raw):
        row_blocks = []
        for i in range(NA_ROWS):
            copy = (parity - i) % 2
            first = (dcls - i - copy + 2 * pad_blocks) // 2 + m * bpk
            bias = jnp.concatenate([tab_ref[copy, a, first + t] for t in range(bpk)], axis=1)
            bias = bias + mask_ref[i:i + 1, m * kw:(m + 1) * kw]
            row_blocks.append(s[i * GRID_W:(i + 1) * GRID_W, :] + bias)
        s_parts.append(jnp.concatenate(row_blocks, axis=0))
    return _softmax_pv(s_parts + [s_ctx], vs + [vc])


def _na_interior_rows(nchunks, kw):
    out = []
    for m in range(nchunks):
        k_lo, k_hi = m * kw // GRID_W, (m + 1) * kw // GRID_W
        i_lo, i_hi = max(0, k_lo - NA_KH + 1), min(NA_ROWS - 1, k_hi - 1)
        out.append((i_lo * GRID_W, (i_hi + 1) * GRID_W))
    return out


def _na_head_interior(s_raw, s_ctx, vs, vc, tab_ref, mask_ref, a, d0, pad_blocks):
    kw = s_raw[0].shape[1]
    bpk = kw // LANES
    nblk = len(s_raw) * bpk
    row_ranges = _na_interior_rows(len(s_raw), kw)
    p_rows, pc_rows, l_rows = [], [], []
    for i in range(NA_ROWS):
        rs = slice(i * GRID_W, (i + 1) * GRID_W)
        lo_lane, hi_lane = i * GRID_W, (i + NA_KH) * GRID_W
        b_lo, b_hi = lo_lane // LANES, -(-hi_lane // LANES)
        copy = (d0 - i) % 2
        first = (d0 - i - copy + 2 * pad_blocks) // 2
        blocks = []
        for b in range(b_lo, b_hi):
            ls = slice((b % bpk) * LANES, (b % bpk + 1) * LANES)
            r0 = row_ranges[b // bpk][0]
            sb = s_raw[b // bpk][rs.start - r0:rs.stop - r0, ls] + tab_ref[copy, a, first + b]
            if b * LANES < lo_lane or (b + 1) * LANES > hi_lane:
                sb = sb + mask_ref[i:i + 1, b * LANES:(b + 1) * LANES]
            blocks.append(sb)
        sw = jnp.concatenate(blocks, axis=1)
        sc = s_ctx[rs]
        m = jnp.maximum(jnp.max(sw, axis=-1, keepdims=True), jnp.max(sc, axis=-1, keepdims=True))
        pw = jnp.exp2(sw - m)
        pc = jnp.exp2(sc - m)
        l_rows.append(jnp.sum(pw, axis=-1, keepdims=True) + jnp.sum(pc, axis=-1, keepdims=True))
        pieces = [jnp.zeros((GRID_W, b_lo * LANES), BF16), pw.astype(BF16),
                  jnp.zeros((GRID_W, (nblk - b_hi) * LANES), BF16)]
        p_rows.append(jnp.concatenate([x for x in pieces if x.shape[1]], axis=1))
        pc_rows.append(pc.astype(BF16))
    p = jnp.concatenate(p_rows, axis=0)
    acc = _dot(jnp.concatenate(pc_rows, axis=0), vc)
    nq = p.shape[0]
    for m, v in enumerate(vs):
        r0, r1 = row_ranges[m]
        part = _dot(p[r0:r1, m * kw:(m + 1) * kw], v)
        pieces = [jnp.zeros((r0, part.shape[1]), F32), part, jnp.zeros((nq - r1, part.shape[1]), F32)]
        acc = acc + jnp.concatenate([x for x in pieces if x.shape[0]], axis=0)
    return acc / jnp.concatenate(l_rows, axis=0)


def _na_kernel(q_ref, k0, k1, k2, k3, v0, v1, v2, v3, kc_ref, vc_ref, tab_ref, mask_ref, o_ref, *,
               groups, dvals):
    g = pl.program_id(1)
    is_edge = jnp.logical_or(g == 0, g == groups - 1)
    pad_blocks = NA_ROWS * GRID_W // LANES

    def step(interior):
        q2 = q_ref[...] * (NA_HD ** -0.5 * LOG2_E)
        lane = lax.broadcasted_iota(jnp.int32, q2.shape, 1)
        ks = [k[...].astype(BF16) for k in (k0, k1, k2, k3)]
        vs = [v[...].astype(BF16) for v in (v0, v1, v2, v3)]
        kc = kc_ref[...].astype(BF16)
        vc = vc_ref[...].astype(BF16)
        nheads = q2.shape[1] // NA_HD

        def scores(a):
            sel = jnp.logical_and(lane >= a * NA_HD, lane < (a + 1) * NA_HD)
            qa = jnp.where(sel, q2, 0.0).astype(BF16)
            if interior:
                ranges = _na_interior_rows(len(ks), ks[0].shape[0])
                return [_dot_nt(qa[r0:r1], k) for (r0, r1), k in zip(ranges, ks)], _dot_nt(qa, kc)
            return [_dot_nt(qa, k) for k in ks], _dot_nt(qa, kc)

        pending = [scores(a) for a in range(min(NA_AHEAD, nheads))]
        out = None
        for a in range(nheads):
            if a + NA_AHEAD < nheads:
                pending.append(scores(a + NA_AHEAD))
            s_raw, s_ctx = pending[a]
            if interior:
                o = _na_head_interior(s_raw, s_ctx, vs, vc, tab_ref, mask_ref, a, dvals[1], pad_blocks)
            else:
                dcls = jnp.where(g == 0, dvals[0], dvals[2])
                o = _na_head_edge(s_raw, s_ctx, vs, vc, tab_ref, mask_ref, a, dcls, dvals[0], pad_blocks)
            out = o if out is None else jnp.where(lane >= a * NA_HD, o, out)
        o_ref[...] = out.astype(o_ref.dtype)

    @pl.when(is_edge)
    def _():
        step(False)

    @pl.when(jnp.logical_not(is_edge))
    def _():
        step(True)


def _na_latent(u, bias, head_base, nbatch, seq, lat_rows, off_na):
    tab, mask, dvals = bias
    rows = seq // GRID_W
    groups = rows // NA_ROWS
    nq = NA_ROWS * GRID_W
    kblk = ROW_TILE
    nkb = NA_KEY_ROWS * GRID_W // kblk
    assert nkb == 4
    hp = NA_STEP_HEADS
    wid = hp * NA_HD
    qcol = off_na // wid
    hsteps = NA_HEADS // hp
    kcol, vcol = qcol + hsteps, qcol + 2 * hsteps
    assert off_na % wid == 0 and NA_HEADS % hp == 0 and head_base % hp == 0
    kb_per_batch = seq // kblk
    kb_per_grow = GRID_W * NA_ROWS // kblk
    lat_kb = lat_rows // kblk

    def kmap(col, m):
        def f(h, g, b):
            st = jnp.clip(g * kb_per_grow - (NA_KH // 2) * GRID_W // kblk, 0, kb_per_batch - nkb)
            return (b * kb_per_batch + st + m, col + h)
        return f

    def mask_map(h, g, b):
        return (jnp.where(g == 0, 0, jnp.where(g == groups - 1, 2, 1)), 0, 0)

    in_specs = [pl.BlockSpec((nq, wid), lambda h, g, b: (b * groups + g, qcol + h))]
    in_specs += [pl.BlockSpec((kblk, wid), kmap(kcol, m)) for m in range(nkb)]
    in_specs += [pl.BlockSpec((kblk, wid), kmap(vcol, m)) for m in range(nkb)]
    in_specs += [pl.BlockSpec((ROW_TILE, wid), lambda h, g, b: (lat_kb + b, kcol + h)),
                 pl.BlockSpec((ROW_TILE, wid), lambda h, g, b: (lat_kb + b, vcol + h)),
                 pl.BlockSpec((2, hp) + tab.shape[2:], lambda h, g, b: (0, head_base // hp + h, 0, 0, 0)),
                 pl.BlockSpec((None,) + mask.shape[1:], mask_map)]
    return pl.pallas_call(
        functools.partial(_na_kernel, groups=groups, dvals=dvals),
        grid=(hsteps, groups, nbatch),
        in_specs=in_specs,
        out_specs=pl.BlockSpec((nq, wid), lambda h, g, b: (b * groups + g, h)),
        out_shape=jax.ShapeDtypeStruct((lat_rows, NA_HEADS * NA_HD), BF16),
        compiler_params=_cparams(("arbitrary", "arbitrary", "arbitrary"), 56),
    )(*([u] * 11), tab, mask)


def _ctx_attn_kernel(q_ref, k_ref, v_ref, o_ref):
    q2 = q_ref[...] * (NA_HD ** -0.5 * LOG2_E)
    lane = lax.broadcasted_iota(jnp.int32, q2.shape, 1)
    k = k_ref[...].astype(BF16)
    v = v_ref[...].astype(BF16)
    outs = []
    for a in range(2):
        sel = (lane < NA_HD) if a == 0 else (lane >= NA_HD)
        qa = jnp.where(sel, q2, 0.0).astype(BF16)
        outs.append(_softmax_pv([_dot_nt(qa, k)], [v]))
    o_ref[...] = jnp.where(lane < NA_HD, outs[0], outs[1]).astype(o_ref.dtype)


def _ctx_attn(u, nbatch, ctx_len, lat_rows, off_na):
    assert ctx_len == ROW_TILE
    qcol = off_na // LANES
    heads2 = NA_HEADS * NA_HD // LANES
    base = lat_rows // ROW_TILE
    spec = lambda col: pl.BlockSpec((ROW_TILE, LANES), lambda b, h: (base + b, col + h))
    return pl.pallas_call(
        _ctx_attn_kernel,
        grid=(nbatch, heads2),
        in_specs=[spec(qcol), spec(qcol + heads2), spec(qcol + 2 * heads2)],
        out_specs=pl.BlockSpec((ROW_TILE, LANES), lambda b, h: (b, h)),
        out_shape=jax.ShapeDtypeStruct((nbatch * ctx_len, NA_HEADS * NA_HD), BF16),
        compiler_params=_cparams(("arbitrary", "arbitrary"), 16),
    )(u, u, u)


def _hg_level_map(rev):
    size = HG_BLOCK // 2
    t = np.arange(size)[:, None]
    s = np.arange(size)[None, :]
    x = t ^ s
    lvl = np.where(x > 0, np.frexp(np.maximum(x, 1))[1] - 1, -1)
    causal = (s < t) if not rev else (s > t)
    out = np.where(causal, lvl, -1)
    out = np.where(t == s, int(np.log2(size)), out)
    return out.astype(np.int32)


def _hg_tri(rev):
    t = np.arange(HG_BLOCK)[:, None]
    s = np.arange(HG_BLOCK)[None, :]
    return ((s <= t) if not rev else (s >= t)).astype(np.float32)


def _hg_anchor(b3, m, rev):
    nv = b3.shape[0]
    if m >= SUBLANES:
        w = m // SUBLANES
        b4 = b3.reshape(nv // (2 * w), 2 * w, SUBLANES, LANES)
        a = b4[:, w:w + 1, 0:1, :] if rev else b4[:, w - 1:w, SUBLANES - 1:SUBLANES, :]
        return jnp.broadcast_to(a, b4.shape).reshape(b3.shape)
    sub = lax.broadcasted_iota(jnp.int32, b3.shape, 1)
    out = None
    for g in range(SUBLANES // (2 * m)):
        idx = g * 2 * m + (m if rev else m - 1)
        a = jnp.broadcast_to(b3[:, idx:idx + 1, :], b3.shape)
        out = a if out is None else jnp.where(sub >= g * 2 * m, a, out)
    return out


def _neg_abs(x):
    bits = lax.bitcast_convert_type(x, jnp.uint32) | jnp.uint32(0x80000000)
    return lax.bitcast_convert_type(bits, F32)


def _hg_pick(q3, k3, m, rev):
    nv = q3.shape[0]
    if m >= SUBLANES:
        w = m // SUBLANES
        shape4 = (nv // (2 * w), 2 * w, SUBLANES, LANES)
        q4, k4 = q3.reshape(shape4), k3.reshape(shape4)
        lower, upper = (q4, k4) if rev else (k4, q4)
        return jnp.concatenate([lower[:, :w], upper[:, w:]], axis=1).reshape(q3.shape)
    upper_rows = (lax.broadcasted_iota(jnp.int32, q3.shape, 1) & m) != 0
    return jnp.where(upper_rows, k3 if rev else q3, q3 if rev else k3)


def _hg_gates(q, z, alog, clog, oml, tri):
    q = _silu(q)
    t = jnp.exp(-jnp.abs(z))
    lsig = jnp.minimum(z, 0.0) - jnp.log(1.0 + t)
    cc = clog + lsig
    logf = jnp.maximum(alog, cc) + jnp.log(1.0 + jnp.exp(-jnp.abs(alog - cc)))
    kk = oml * jnp.where(z >= 0.0, t, 1.0) / (1.0 + t)

    hi = logf.astype(BF16)
    r1 = logf - hi.astype(F32)
    mid = r1.astype(BF16)
    lo = (r1 - mid.astype(F32)).astype(BF16)
    b = (_dot(tri, hi) + _dot(tri, mid) + _dot(tri, lo)) * LOG2_E
    return q, kk, b


def _hg_mix(q, kk, b, v, lv, st, rev):
    n = q.shape[0]
    half = n // 2
    nlev = int(np.log2(n))
    shape3 = (n // SUBLANES, SUBLANES, LANES)
    b3, q3, k3 = b.reshape(shape3), q.reshape(shape3), kk.reshape(shape3)
    halves = (slice(0, half), slice(half, n))
    qb, kb = q.astype(BF16), kk.astype(BF16)
    acc = [jnp.where(lv == nlev - 1, _dot_nt(qb[hs], kb[hs]), 0.0) for hs in halves]
    for lev in range(nlev - 1):
        m = 1 << lev
        e = jnp.exp2(_neg_abs(b3 - _hg_anchor(b3, m, rev)))
        w = (_hg_pick(q3, k3, m, rev) * e).reshape(n, LANES).astype(BF16)
        acc = [jnp.where(lv == lev, _dot_nt(w[hs], w[hs]), a) for hs, a in zip(halves, acc)]
    first, second = (halves[1], halves[0]) if rev else halves
    e = jnp.exp2(_neg_abs(b - (b[half:half + 1, :] if rev else b[half - 1:half, :])))
    top = _dot_nt((q[second] * e[second]).astype(BF16), (kk[first] * e[first]).astype(BF16))

    vb = v.astype(BF16)
    a0, a1 = acc[0].astype(BF16), acc[1].astype(BF16)
    tb = top.astype(BF16)
    if rev:
        o_lo = _dot(jnp.concatenate([a0, tb], axis=1), vb)
        o_hi = _dot(a1, vb[halves[1]])
    else:
        o_lo = _dot(a0, vb[halves[0]])
        o_hi = _dot(jnp.concatenate([tb, a1], axis=1), vb)
    b_last = b[0:1, :] if rev else b[n - 1:n, :]
    qh = (q * jnp.exp2(b)).astype(BF16)
    o = jnp.concatenate([o_lo, o_hi], axis=0) + _dot_nt(qh, st.astype(BF16))
    kh = (kk * jnp.exp2(b_last - b)).astype(BF16)
    st_new = st * jnp.exp2(b_last) + _dot_tn(vb, kh)
    return o, st_new


def _hg_gates_ahead(q_ref, z_ref, al_ref, cl_ref, om_ref, tri):
    def gates(h):
        hs = slice(h * HG_DK, (h + 1) * HG_DK)
        return _hg_gates(q_ref[:, hs], z_ref[:, hs], al_ref[:, hs], cl_ref[:, hs], om_ref[:, hs], tri)

    pending = [gates(h) for h in range(min(HG_AHEAD, HG_HEADS))]
    for h in range(HG_HEADS):
        if h + HG_AHEAD < HG_HEADS:
            pending.append(gates(h + HG_AHEAD))
        yield pending[h]


def _hg_fwd_kernel(q_ref, v_ref, z_ref, al_ref, cl_ref, om_ref, tri_ref, lv_ref, o_ref, st_ref):
    @pl.when(pl.program_id(1) == 0)
    def _():
        st_ref[...] = jnp.zeros_like(st_ref)

    lv = lv_ref[...]
    gates = _hg_gates_ahead(q_ref, z_ref, al_ref, cl_ref, om_ref, tri_ref[...])
    for h in range(HG_HEADS):
        hs = slice(h * HG_DK, (h + 1) * HG_DK)
        o, st = _hg_mix(*next(gates), v_ref[:, hs], lv, st_ref[h], False)
        o_ref[:, hs] = o
        st_ref[h] = st


def _hg_bwd_kernel(q_ref, v_ref, z_ref, g_ref, of_ref, al_ref, cl_ref, om_ref, ng_ref, tri_ref, lv_ref,
                   o_ref, st_ref):
    @pl.when(pl.program_id(1) == 0)
    def _():
        st_ref[...] = jnp.zeros_like(st_ref)

    lv = lv_ref[...]
    gates = _hg_gates_ahead(q_ref, z_ref, al_ref, cl_ref, om_ref, tri_ref[...])
    for h in range(HG_HEADS):
        hs = slice(h * HG_DK, (h + 1) * HG_DK)
        o, st = _hg_mix(*next(gates), v_ref[:, hs], lv, st_ref[h], True)
        st_ref[h] = st
        t = of_ref[:, hs] + o
        y = t * lax.rsqrt(jnp.mean(t * t, axis=-1, keepdims=True) + EPS)
        o_ref[:, hs] = (y * ng_ref[:, hs] * _silu(g_ref[:, hs])).astype(o_ref.dtype)


def _hgrn(u, lb, norm_g, nbatch, seq, ctx_len, lat_rows, off_hg):
    assert ctx_len == HG_BLOCK
    n = u.shape[0]
    hd = HG_HEADS * HG_DK
    col = off_hg // hd
    per = seq // HG_BLOCK
    lat_blocks = lat_rows // HG_BLOCK
    lbf = lb.astype(F32)
    alog, clog, oml = jnp.log(lbf), jnp.log1p(-lbf), 1.0 - lbf

    def fmap(c):
        return lambda b, j: (jnp.where(j == 0, lat_blocks + b, b * per + j - 1), c)

    def bmap(c):
        return lambda b, j: (jnp.where(j == 0, lat_blocks + b, b * per + per - j), c)

    const = lambda shape: pl.BlockSpec(shape, lambda b, j: (0, 0))
    grid = (nbatch, per + 1)
    vec = lambda a: a.reshape(1, hd)
    o_f = pl.pallas_call(
        _hg_fwd_kernel,
        grid=grid,
        in_specs=[pl.BlockSpec((HG_BLOCK, hd), fmap(col)), pl.BlockSpec((HG_BLOCK, hd), fmap(col + 1)),
                  pl.BlockSpec((HG_BLOCK, hd), fmap(col + 2)),
                  const((1, hd)), const((1, hd)), const((1, hd)),
                  const((HG_BLOCK, HG_BLOCK)), const((HG_BLOCK // 2, HG_BLOCK // 2))],
        out_specs=pl.BlockSpec((HG_BLOCK, hd), fmap(0)),
        out_shape=jax.ShapeDtypeStruct((n, hd), F32),
        scratch_shapes=[pltpu.VMEM((HG_HEADS, HG_DK, HG_DK), F32)],
        compiler_params=_cparams(("arbitrary", "arbitrary"), 32),
    )(u, u, u, vec(alog[0]), vec(clog[0]), vec(oml[0]),
      jnp.asarray(_hg_tri(False), BF16), jnp.asarray(_hg_level_map(False)))
    return pl.pallas_call(
        _hg_bwd_kernel,
        grid=grid,
        in_specs=[pl.BlockSpec((HG_BLOCK, hd), bmap(col)), pl.BlockSpec((HG_BLOCK, hd), bmap(col + 1)),
                  pl.BlockSpec((HG_BLOCK, hd), bmap(col + 3)), pl.BlockSpec((HG_BLOCK, hd), bmap(col + 4)),
                  pl.BlockSpec((HG_BLOCK, hd), bmap(0)),
                  const((1, hd)), const((1, hd)), const((1, hd)), const((1, hd)),
                  const((HG_BLOCK, HG_BLOCK)), const((HG_BLOCK // 2, HG_BLOCK // 2))],
        out_specs=pl.BlockSpec((HG_BLOCK, hd), bmap(0)),
        out_shape=jax.ShapeDtypeStruct((n, hd), BF16),
        scratch_shapes=[pltpu.VMEM((HG_HEADS, HG_DK, HG_DK), F32)],
        compiler_params=_cparams(("arbitrary", "arbitrary"), 32),
    )(u, u, u, u, o_f, vec(alog[1]), vec(clog[1]), vec(oml[1]), vec(norm_g.astype(F32)),
      jnp.asarray(_hg_tri(True), BF16), jnp.asarray(_hg_level_map(True)))


def _out_kernel(x_ref, cv_ref, na_ref, hg_ref, w_ref, ga_ref, g2_ref, sh2_ref, s2_ref, wrh_ref, wrl_ref, br_ref,
                xo_ref, h_ref, rt_ref, *, tiles_per_batch, nbatch):
    r = jnp.minimum(pl.program_id(0) // tiles_per_batch, nbatch)
    c0 = cv_ref.shape[1]
    c1 = c0 + na_ref.shape[1]
    mix = (_dot(cv_ref[...], w_ref[0:c0, :]) + _dot(na_ref[...], w_ref[c0:c1, :])
           + _dot(hg_ref[...], w_ref[c1:, :]))
    xn = x_ref[...] + ga_ref[pl.ds(r, 1), :] * mix
    xo_ref[...] = xn
    h = _rms_mod(xn, g2_ref[...], s2_ref[pl.ds(r, 1), :], sh2_ref[pl.ds(r, 1), :])
    h_ref[...] = h.astype(h_ref.dtype)
    h_hi, h_lo = _split_bf16(h)
    logits = (_dot(h_hi, wrh_ref[...]) + (_dot(h_lo, wrh_ref[...]) + _dot(h_hi, wrl_ref[...]))
              + br_ref[...])
    rt_ref[...] = _route_rows(logits)


def _route_rows(lg):
    lane = lax.broadcasted_iota(jnp.int32, lg.shape, 1)
    big = jnp.int32(2 ** 30)
    low = jnp.float32(-3e38)

    def first_max(vals, mask):
        m = jnp.max(vals, axis=-1, keepdims=True)
        idx = jnp.min(jnp.where(jnp.logical_and(vals == m, mask), lane, big), axis=-1, keepdims=True)
        return m, idx

    gmask = lane < N_GROUPS
    gl = jnp.where(gmask, lg, low)
    gm, grp = first_max(gl, gmask)
    p_grp = 1.0 / jnp.sum(jnp.where(gmask, jnp.exp(gl - gm), 0.0), axis=-1, keepdims=True)
    lo = N_GROUPS + grp * EXP_PER_GROUP
    emask = jnp.logical_and(lane >= lo, lane < lo + EXP_PER_GROUP)
    el = jnp.where(emask, lg, low)
    m1, i1 = first_max(el, emask)
    emask2 = jnp.logical_and(emask, lane != i1)
    el2 = jnp.where(emask2, lg, low)
    m2, i2 = first_max(el2, emask2)
    t = jnp.exp(m2 - m1)
    w1 = p_grp / (1.0 + t)
    w2 = p_grp * t / (1.0 + t)
    e1 = (i1 - N_GROUPS).astype(F32)
    e2 = (i2 - N_GROUPS).astype(F32)
    return jnp.where(lane == 0, e1, jnp.where(lane == 1, e2, jnp.where(lane == 2, w1,
                     jnp.where(lane == 3, w2, 0.0))))


def _out_proj(x, conv, na, hg, w_bf16, l, mod, g_ffn, w_router, b_router, n_rows, nbatch, lat_rows):
    d = x.shape[1]
    w_router_hi, w_router_lo = _split_bf16(w_router)
    tm = _pick_tile(OUT_TM, lat_rows // nbatch, n_rows)
    kern = functools.partial(_out_kernel, tiles_per_batch=lat_rows // nbatch // tm, nbatch=nbatch)
    row = lambda w: pl.BlockSpec((tm, w), lambda i: (i, 0))
    const = lambda shape: pl.BlockSpec(shape, lambda i: (0, 0))
    modc = lambda c: pl.BlockSpec((None, SUBLANES, d), lambda i: (l, 0, c))
    return pl.pallas_call(
        kern,
        grid=(n_rows // tm,),
        in_specs=[row(d), row(conv.shape[1]), row(na.shape[1]), row(hg.shape[1]),
                  pl.BlockSpec((None, d, d), lambda i: (l, 0, 0)),
                  modc(2),
                  const((1, d)),
                  modc(3),
                  modc(4),
                  const((d, ROUTER_PAD)), const((d, ROUTER_PAD)), const((1, ROUTER_PAD))],
        out_specs=[row(d), row(d), row(ROUTER_PAD)],
        out_shape=[jax.ShapeDtypeStruct((n_rows, d), F32), jax.ShapeDtypeStruct((n_rows, d), BF16),
                   jax.ShapeDtypeStruct((n_rows, ROUTER_PAD), F32)],
        compiler_params=_cparams(("arbitrary",), 56),
    )(x, conv, na, hg, w_bf16, mod, g_ffn.reshape(1, d), mod, mod, w_router_hi, w_router_lo, b_router)


def _moe_kernel(be_ref, nu_ref, xs_ref, w1_ref, w3_ref, w2_ref, sw_ref, o_ref, w1b, w3b, w2b):
    i = pl.program_id(0)
    e = be_ref[i]
    prev = be_ref[jnp.maximum(i - 1, 0)]

    @pl.when(jnp.logical_or(i == 0, e != prev))
    def _():
        w1b[...] = w1_ref[...].astype(BF16)
        w3b[...] = w3_ref[...].astype(BF16)
        w2b[...] = w2_ref[...].astype(BF16)

    @pl.when(i < nu_ref[0])
    def _():
        x = xs_ref[...]
        a = (_silu(_dot(x, w1b[...])) * _dot(x, w3b[...])).astype(BF16)
        o_ref[...] = (_dot(a, w2b[...]) * sw_ref[...]).astype(o_ref.dtype)

    @pl.when(i >= nu_ref[0])
    def _():
        o_ref[...] = jnp.zeros_like(o_ref)


def _moe_experts(xs, slot_w, blk_e, nused, w1, w3, w2, l):
    p, d = xs.shape
    de = w1.shape[3]
    bm = MOE_BM
    grid_spec = pltpu.PrefetchScalarGridSpec(
        num_scalar_prefetch=2,
        grid=(p // bm,),
        in_specs=[pl.BlockSpec((bm, d), lambda i, be, nu: (i, 0)),
                  pl.BlockSpec((None, None, d, de), lambda i, be, nu: (l, be[i], 0, 0)),
                  pl.BlockSpec((None, None, d, de), lambda i, be, nu: (l, be[i], 0, 0)),
                  pl.BlockSpec((None, None, de, d), lambda i, be, nu: (l, be[i], 0, 0)),
                  pl.BlockSpec((bm, 1), lambda i, be, nu: (i, 0))],
        out_specs=pl.BlockSpec((bm, d), lambda i, be, nu: (i, 0)),
        scratch_shapes=[pltpu.VMEM((d, de), BF16), pltpu.VMEM((d, de), BF16), pltpu.VMEM((de, d), BF16)],
    )
    return pl.pallas_call(
        _moe_kernel,
        grid_spec=grid_spec,
        out_shape=jax.ShapeDtypeStruct((p, d), BF16),
        compiler_params=_cparams(("arbitrary",), 48),
    )(blk_e, nused, xs, w1, w3, w2, slot_w.reshape(p, 1))


def _rows(a, idx):
    return a.at[idx].get(mode="promise_in_bounds")


def _route_meta(route, n):
    i32 = jnp.int32
    eid = route[:, 0:TOP_K].astype(i32).reshape(-1)
    wt = route[:, TOP_K:2 * TOP_K].reshape(-1)
    a = n * TOP_K
    bm = MOE_BM
    nblk = -(-a // bm) + N_EXPERTS
    p = nblk * bm
    experts = jnp.arange(N_EXPERTS, dtype=i32)[None, :]
    ja = jnp.arange(a, dtype=i32)
    se, order, wsort = lax.sort((eid, ja, wt), num_keys=1, is_stable=True)
    cnt = jnp.sum((eid[:, None] == experts).astype(i32), axis=0)
    pcnt = (cnt + bm - 1) // bm * bm
    pend = jnp.cumsum(pcnt)
    pstart = pend - pcnt
    end = jnp.cumsum(cnt)
    start = end - cnt
    off = pstart - start
    d_off = off - jnp.concatenate([jnp.zeros((1,), i32), off[:-1]])
    dst_sorted = ja + jnp.sum(jnp.where(ja[:, None] >= start[None, :], d_off[None, :], 0), axis=1)
    _, pos = lax.sort((order, dst_sorted), num_keys=1)
    jp = jnp.arange(p, dtype=i32)
    in_or_after = jp[:, None] >= pstart[None, :]
    src = jp - jnp.sum(jnp.where(in_or_after, d_off[None, :], 0), axis=1)
    valid = src < jnp.sum(jnp.where(in_or_after, cnt[None, :], 0), axis=1)
    src = jnp.where(valid, src, jp % a)
    slot_tok = _rows(order, src) // TOP_K
    slot_w = jnp.where(valid, _rows(wsort, src), 0.0)
    jb = jnp.arange(nblk, dtype=i32) * bm
    blk_e = jnp.minimum(jnp.sum((jb[:, None] >= pend[None, :]).astype(i32), axis=1), N_EXPERTS - 1)
    nused = (pend[-1:] // bm).astype(i32)
    return slot_tok, slot_w, pos.reshape(n, TOP_K), blk_e, nused


def _combine_kernel(x_ref, y0_ref, y1_ref, ga_ref, gf_ref, o_ref, *, tiles_per_batch, nbatch, final):
    r = jnp.minimum(pl.program_id(0) // tiles_per_batch, nbatch)
    xn = x_ref[...] + ga_ref[pl.ds(r, 1), :] * (y0_ref[...].astype(F32) + y1_ref[...].astype(F32))
    if final:
        xn = xn * lax.rsqrt(jnp.mean(xn * xn, axis=-1, keepdims=True) + EPS) * gf_ref[...]
    o_ref[...] = xn


def _combine(x, y0, y1, mod, l, g_final, n_rows, nbatch, lat_rows, final):
    d = x.shape[1]
    tm = _pick_tile(OUT_TM, lat_rows // nbatch, n_rows)
    kern = functools.partial(_combine_kernel, tiles_per_batch=lat_rows // nbatch // tm, nbatch=nbatch,
                             final=final)
    row = pl.BlockSpec((tm, d), lambda i: (i, 0))
    return pl.pallas_call(
        kern,
        grid=(n_rows // tm,),
        in_specs=[row, row, row, pl.BlockSpec((None, SUBLANES, d), lambda i: (l, 0, 5)),
                  pl.BlockSpec((1, d), lambda i: (0, 0))],
        out_specs=row,
        out_shape=jax.ShapeDtypeStruct((n_rows, d), F32),
        compiler_params=_cparams(("arbitrary",), 40),
    )(x, y0, y1, mod, g_final.reshape(1, d))


def kernel(x, c, ctx, c_ctx, w_ada, b_ada, g_mix, g_ffn, w_in, conv_w, conv_b, conv_ln_g, conv_ln_b,
           na_rpb, hgrn_lb, hgrn_norm_g, w_out, w_router_group, b_router_group, w_router_expert,
           b_router_expert, w_exp_gate, w_exp_up, w_exp_down, g_final):
    nb, seq, d = x.shape
    ctx_len = ctx.shape[1]
    depth = w_ada.shape[0]
    lat_rows = nb * seq
    n_all = lat_rows + nb * ctx_len
    conv_ch = conv_w.shape[2]
    off_na = 2 * conv_ch
    off_hg = off_na + 3 * NA_HEADS * NA_HD
    rows = seq // GRID_W
    assert nb < SUBLANES and rows % NA_ROWS == 0 and rows >= NA_KEY_ROWS

    lbs = jnp.cumsum(jax.nn.softmax(hgrn_lb.astype(F32), axis=0), axis=0)
    lbs = lbs - lbs[:1]

    cond = jnp.concatenate([c, c_ctx[None, :], jnp.zeros((SUBLANES - nb - 1, d), F32)], axis=0)
    mod = _ada_mod(cond, w_ada, b_ada)

    xs = jnp.concatenate([x.reshape(lat_rows, d), ctx.reshape(nb * ctx_len, d)], axis=0)
    w_in_b = w_in.astype(BF16)
    w_out_b = w_out.astype(BF16)
    na_bias = _na_bias(na_rpb.reshape((depth * NA_HEADS,) + na_rpb.shape[2:]), rows)
    for l in range(depth):
        with_ctx = l < depth - 1
        n_act = n_all if with_ctx else lat_rows
        u = _norm_in(xs, g_mix[l], mod, w_in_b, l, nb, lat_rows)

        conv = _conv_module(u, conv_w[l], conv_b[l], conv_ln_g[l], conv_ln_b[l], n_act, lat_rows, seq)
        na = _na_latent(u, na_bias, l * NA_HEADS, nb, seq, lat_rows, off_na)
        if with_ctx:
            na = jnp.concatenate([na, _ctx_attn(u, nb, ctx_len, lat_rows, off_na)], axis=0)
        hg = _hgrn(u, lbs[l], hgrn_norm_g[l], nb, seq, ctx_len, lat_rows, off_hg)

        w_router = jnp.concatenate(
            [w_router_group[l], w_router_expert[l],
             jnp.zeros((d, ROUTER_PAD - N_GROUPS - N_EXPERTS), F32)], axis=1)
        b_router = jnp.concatenate(
            [b_router_group[l], b_router_expert[l],
             jnp.zeros((ROUTER_PAD - N_GROUPS - N_EXPERTS,), F32)]).reshape(1, ROUTER_PAD)
        x_mid, h, route = _out_proj(xs, conv, na, hg, w_out_b, l, mod, g_ffn[l],
                                    w_router, b_router, n_act, nb, lat_rows)

        slot_tok, slot_w, pos, blk_e, nused = _route_meta(route, n_act)
        ys = _moe_experts(_rows(h, slot_tok), slot_w, blk_e, nused, w_exp_gate, w_exp_up, w_exp_down, l)
        y0 = _rows(ys, pos[:, 0])
        y1 = _rows(ys, pos[:, 1])
        xs = _combine(x_mid, y0, y1, mod, l, g_final, n_act, nb, lat_rows, final=not with_ctx)
    return xs.reshape(nb, seq, d)
```

```python
import functools

import numpy as np
import jax
import jax.numpy as jnp
from jax import lax
from jax.experimental import pallas as pl
from jax.experimental.pallas import tpu as pltpu

F32 = jnp.float32
BF16 = jnp.bfloat16

EPS = 1e-6
NEG_INF = -1e30
LOG2_E = 1.4426950408889634

GRID_W = 64
CONV_K = 31
NA_HEADS = 16
NA_HD = 64
NA_KH = 8
NA_KW = 16
HG_HEADS = 4
HG_DK = 128
N_GROUPS = 4
EXP_PER_GROUP = 8
N_EXPERTS = N_GROUPS * EXP_PER_GROUP
TOP_K = 2

LANES = 128
SUBLANES = 8

ROW_TILE = 256
ADA_TN = 1024
IN_TM = 1024
IN_TN = 1664
OUT_TM = 512
CONV_ROWS = 64
NA_STEP_HEADS = 4
NA_AHEAD = 4
NA_ROWS = 8
NA_KEY_ROWS = 16
HG_BLOCK = 256
HG_AHEAD = 2
MOE_BM = 512
ROUTER_PAD = LANES
HALO = 16


def _pick_tile(pref, *extents):
    t = pref
    while t > ROW_TILE and any(e % t for e in extents):
        t //= 2
    assert all(e % t == 0 for e in extents)
    return t


def _cparams(sem, vmem_mb):
    return pltpu.CompilerParams(dimension_semantics=sem, vmem_limit_bytes=vmem_mb * 1024 * 1024)


def _dot(a, b):
    return jnp.dot(a, b, preferred_element_type=F32)


def _dot_nt(a, b):
    return lax.dot_general(a, b, (((1,), (1,)), ((), ())), preferred_element_type=F32)


def _dot_tn(a, b):
    return lax.dot_general(a, b, (((0,), (0,)), ((), ())), preferred_element_type=F32)


def _sigmoid(x):
    return 1.0 / (1.0 + jnp.exp(-x))


def _silu(x):
    return x * _sigmoid(x)


def _split_bf16(x):
    hi = x.astype(BF16)
    return hi, (x - hi.astype(F32)).astype(BF16)


def _ada_kernel(c_ref, w_ref, b_ref, o_ref):
    s_hi, s_lo = _split_bf16(_silu(c_ref[...]))
    w_hi, w_lo = _split_bf16(w_ref[0])
    o_ref[0] = _dot(s_hi, w_hi) + (_dot(s_lo, w_hi) + _dot(s_hi, w_lo)) + b_ref[0]


def _ada_mod(cond, w_ada, b_ada):
    depth, d, n = w_ada.shape
    tn = ADA_TN
    return pl.pallas_call(
        _ada_kernel,
        grid=(depth, n // tn),
        in_specs=[
            pl.BlockSpec((SUBLANES, d), lambda l, j: (0, 0)),
            pl.BlockSpec((1, d, tn), lambda l, j: (l, 0, j)),
            pl.BlockSpec((1, 1, tn), lambda l, j: (l, 0, j)),
        ],
        out_specs=pl.BlockSpec((1, SUBLANES, tn), lambda l, j: (l, 0, j)),
        out_shape=jax.ShapeDtypeStruct((depth, SUBLANES, n), F32),
        compiler_params=_cparams(("arbitrary", "arbitrary"), 40),
    )(cond, w_ada, b_ada.reshape(depth, 1, n))


def _rms_mod(x, g, scale, shift):
    y = x * lax.rsqrt(jnp.mean(x * x, axis=-1, keepdims=True) + EPS)
    return (y * g) * (1.0 + scale) + shift


def _norm_in_kernel(x_ref, g_ref, sh_ref, sc_ref, w_ref, o_ref, h_ref, *, tiles_per_batch, nbatch):
    i = pl.program_id(0)

    @pl.when(pl.program_id(1) == 0)
    def _():
        r = jnp.minimum(i // tiles_per_batch, nbatch)
        h = _rms_mod(x_ref[...], g_ref[...], sc_ref[pl.ds(r, 1), :], sh_ref[pl.ds(r, 1), :])
        h_ref[...] = h.astype(BF16)

    o_ref[...] = _dot(h_ref[...], w_ref[...])


def _norm_in(x, g, mod, w_bf16, l, nbatch, lat_rows):
    n, d = x.shape
    nout = w_bf16.shape[2]
    tm, tn = _pick_tile(IN_TM, lat_rows // nbatch, n), IN_TN
    kern = functools.partial(_norm_in_kernel, tiles_per_batch=lat_rows // nbatch // tm, nbatch=nbatch)
    return pl.pallas_call(
        kern,
        grid=(n // tm, nout // tn),
        in_specs=[
            pl.BlockSpec((tm, d), lambda i, j: (i, 0), pipeline_mode=pl.Buffered(1)),
            pl.BlockSpec((1, d), lambda i, j: (0, 0)),
            pl.BlockSpec((None, SUBLANES, d), lambda i, j: (l, 0, 0)),
            pl.BlockSpec((None, SUBLANES, d), lambda i, j: (l, 0, 1)),
            pl.BlockSpec((None, d, tn), lambda i, j: (l, 0, j)),
        ],
        out_specs=pl.BlockSpec((tm, tn), lambda i, j: (i, j)),
        out_shape=jax.ShapeDtypeStruct((n, nout), F32),
        scratch_shapes=[pltpu.VMEM((tm, d), BF16)],
        compiler_params=_cparams(("arbitrary", "arbitrary"), 52),
    )(x, g.reshape(1, d), mod, mod, w_bf16)


def _conv_kernel(ap_ref, gp_ref, a_ref, gt_ref, an_ref, gn_ref, w_ref, b_ref, lg_ref, lb_ref,
                 o_ref, buf_ref, acc_ref, *, lat_tiles, tiles_per_seq):
    i = pl.program_id(0)
    tc, ch = a_ref.shape
    is_lat = i < lat_tiles
    pos = i % tiles_per_seq
    first = jnp.logical_or(jnp.logical_not(is_lat), pos == 0)
    last = jnp.logical_or(jnp.logical_not(is_lat), pos == tiles_per_seq - 1)

    buf_ref[0:HALO] = jnp.where(first, 0.0, ap_ref[...] * _sigmoid(gp_ref[...]))
    buf_ref[HALO:HALO + tc] = a_ref[...] * _sigmoid(gt_ref[...])
    buf_ref[HALO + tc:2 * HALO + tc] = jnp.where(last, 0.0, an_ref[...] * _sigmoid(gn_ref[...]))

    rows = CONV_ROWS
    base = HALO - CONV_K // 2
    for c in range(ch // LANES):
        cs = slice(c * LANES, (c + 1) * LANES)
        for r in range(tc // rows):
            acc = None
            for res in range(SUBLANES):
                y = None
                for k in range(CONV_K):
                    if (base + k) % SUBLANES != res:
                        continue
                    lo = r * rows + (base + k) // SUBLANES * SUBLANES
                    term = w_ref[k:k + 1, cs] * buf_ref[lo:lo + rows + SUBLANES, cs]
                    y = term if y is None else y + term
                if y is not None:
                    y = y[res:res + rows]
                    acc = y if acc is None else acc + y
            acc_ref[r * rows:(r + 1) * rows, cs] = acc

    h = acc_ref[...] + b_ref[...]
    mu = jnp.mean(h, axis=-1, keepdims=True)
    var = jnp.mean(jnp.square(h - mu), axis=-1, keepdims=True)
    y = (h - mu) * lax.rsqrt(var + EPS) * lg_ref[...] + lb_ref[...]
    o_ref[...] = _silu(y).astype(o_ref.dtype)


def _conv_module(u, w_dw, b_dw, ln_g, ln_b, n_rows, lat_rows, seq):
    ch = w_dw.shape[1]
    tc = ROW_TILE
    per = tc // HALO
    nh = u.shape[0] // HALO
    kern = functools.partial(_conv_kernel, lat_tiles=lat_rows // tc, tiles_per_seq=seq // tc)
    prev_map = lambda c: (lambda i: (jnp.maximum(i * per - 1, 0), c))
    next_map = lambda c: (lambda i: (jnp.minimum((i + 1) * per, nh - 1), c))
    vec = lambda a: a.reshape(1, ch)
    return pl.pallas_call(
        kern,
        grid=(n_rows // tc,),
        in_specs=[
            pl.BlockSpec((HALO, ch), prev_map(0)),
            pl.BlockSpec((HALO, ch), prev_map(1)),
            pl.BlockSpec((tc, ch), lambda i: (i, 0)),
            pl.BlockSpec((tc, ch), lambda i: (i, 1)),
            pl.BlockSpec((HALO, ch), next_map(0)),
            pl.BlockSpec((HALO, ch), next_map(1)),
            pl.BlockSpec((CONV_K, ch), lambda i: (0, 0)),
            pl.BlockSpec((1, ch), lambda i: (0, 0)),
            pl.BlockSpec((1, ch), lambda i: (0, 0)),
            pl.BlockSpec((1, ch), lambda i: (0, 0)),
        ],
        out_specs=pl.BlockSpec((tc, ch), lambda i: (i, 0)),
        out_shape=jax.ShapeDtypeStruct((n_rows, ch), BF16),
        scratch_shapes=[pltpu.VMEM((tc + 2 * HALO, ch), F32), pltpu.VMEM((tc, ch), F32)],
        compiler_params=_cparams(("arbitrary",), 16),
    )(u, u, u, u, u, u, w_dw, vec(b_dw), vec(ln_g), vec(ln_b))


def _na_bias_tables(rows):
    groups = rows // NA_ROWS
    reps = [0, min(1, groups - 1), groups - 1]
    out = []
    for g in reps:
        start = int(np.clip(NA_ROWS * g - NA_KH // 2, 0, rows - NA_KEY_ROWS))
        per_row = []
        for i in range(NA_ROWS):
            r = NA_ROWS * g + i
            sr = int(np.clip(r - NA_KH // 2, 0, rows - NA_KH))
            per_row.append((sr - start, sr - r + NA_KH - 1))
        out.append(per_row)
    return out


def _na_bias(rpb, rows):
    nh = rpb.shape[0]
    ndr, ndc = 2 * NA_KH - 1, 2 * NA_KW - 1
    period = 2 * GRID_W - 1
    pad = GRID_W - NA_KW
    vp = jnp.pad(rpb.astype(F32), ((0, 0), (0, 0), (pad, period - ndc - pad)))
    hank = jnp.tile(vp, (1, 1, GRID_W + 1))[:, :, :GRID_W * (period + 1)]
    hank = hank.reshape(nh, ndr, GRID_W, period + 1)[..., :GRID_W]
    toe = hank[:, :, ::-1, :]
    c = np.arange(GRID_W)[:, None]
    j = np.arange(GRID_W)[None, :]
    ws = np.clip(c - NA_KW // 2, 0, GRID_W - NA_KW)
    col_ok = (j >= ws) & (j < ws + NA_KW)
    toe = jnp.where(col_ok[None, None], toe, NEG_INF)
    flat = jnp.transpose(toe, (0, 2, 1, 3)).reshape(nh, GRID_W, ndr * GRID_W)
    nk = NA_KEY_ROWS * GRID_W
    lpad = NA_ROWS * GRID_W
    total = 2 * nk

    def padded(shift):
        return jnp.pad(flat, ((0, 0), (0, 0), (lpad - shift, total - flat.shape[2] - lpad + shift)))

    tab = jnp.stack([padded(0), padded(GRID_W)])
    tab = tab.reshape(2, nh, GRID_W, total // LANES, LANES).transpose(0, 1, 3, 2, 4)
    tables = _na_bias_tables(rows)
    mask = np.full((len(tables), NA_ROWS, nk), NEG_INF, np.float32)
    dvals = []
    for cls, per_row in enumerate(tables):
        dvals.append(per_row[0][1] - per_row[0][0])
        for i, (off, lo) in enumerate(per_row):
            assert lo - off == dvals[-1] - i and -NA_ROWS <= lo - off < NA_ROWS
            mask[cls, i, off * GRID_W:(off + NA_KH) * GRID_W] = 0.0
    assert len({d % 2 for d in dvals}) == 1
    assert len(tables) < 3 or rows < 3 * NA_ROWS or all(off == i for i, (off, _) in enumerate(tables[1]))
    return tab * LOG2_E, jnp.asarray(mask * LOG2_E), tuple(dvals)


def _softmax_pv(s_parts, v_parts):
    m = functools.reduce(jnp.maximum, [jnp.max(s, axis=-1, keepdims=True) for s in s_parts])
    acc, l = None, None
    for s, v in zip(s_parts, v_parts):
        p = jnp.exp2(s - m)
        ps = jnp.sum(p, axis=-1, keepdims=True)
        pv = _dot(p.astype(BF16), v)
        l = ps if l is None else l + ps
        acc = pv if acc is None else acc + pv
    return acc / l


def _na_head_edge(s_raw, s_ctx, vs, vc, tab_ref, mask_ref, a, dcls, parity, pad_blocks):
    kw = s_raw[0].shape[1]
    bpk = kw // LANES
    s_parts = []
    for m, s in enumerate(s_raw):
        row_blocks = []
        for i in range(NA_ROWS):
            copy = (parity - i) % 2
            first = (dcls - i - copy + 2 * pad_blocks) // 2 + m * bpk
            bias = jnp.concatenate([tab_ref[copy, a, first + t] for t in range(bpk)], axis=1)
            bias = bias + mask_ref[i:i + 1, m * kw:(m + 1) * kw]
            row_blocks.append(s[i * GRID_W:(i + 1) * GRID_W, :] + bias)
        s_parts.append(jnp.concatenate(row_blocks, axis=0))
    return _softmax_pv(s_parts + [s_ctx], vs + [vc])


def _na_interior_rows(nchunks, kw):
    out = []
    for m in range(nchunks):
        k_lo, k_hi = m * kw // GRID_W, (m + 1) * kw // GRID_W
        i_lo, i_hi = max(0, k_lo - NA_KH + 1), min(NA_ROWS - 1, k_hi - 1)
        out.append((i_lo * GRID_W, (i_hi + 1) * GRID_W))
    return out


def _na_head_interior(s_raw, s_ctx, vs, vc, tab_ref, mask_ref, a, d0, pad_blocks):
    kw = s_raw[0].shape[1]
    bpk = kw // LANES
    nblk = len(s_raw) * bpk
    row_ranges = _na_interior_rows(len(s_raw), kw)
    p_rows, pc_rows, l_rows = [], [], []
    for i in range(NA_ROWS):
        rs = slice(i * GRID_W, (i + 1) * GRID_W)
        lo_lane, hi_lane = i * GRID_W, (i + NA_KH) * GRID_W
        b_lo, b_hi = lo_lane // LANES, -(-hi_lane // LANES)
        copy = (d0 - i) % 2
        first = (d0 - i - copy + 2 * pad_blocks) // 2
        blocks = []
        for b in range(b_lo, b_hi):
            ls = slice((b % bpk) * LANES, (b % bpk + 1) * LANES)
            r0 = row_ranges[b // bpk][0]
            sb = s_raw[b // bpk][rs.start - r0:rs.stop - r0, ls] + tab_ref[copy, a, first + b]
            if b * LANES < lo_lane or (b + 1) * LANES > hi_lane:
                sb = sb + mask_ref[i:i + 1, b * LANES:(b + 1) * LANES]
            blocks.append(sb)
        sw = jnp.concatenate(blocks, axis=1)
        sc = s_ctx[rs]
        m = jnp.maximum(jnp.max(sw, axis=-1, keepdims=True), jnp.max(sc, axis=-1, keepdims=True))
        pw = jnp.exp2(sw - m)
        pc = jnp.exp2(sc - m)
        l_rows.append(jnp.sum(pw, axis=-1, keepdims=True) + jnp.sum(pc, axis=-1, keepdims=True))
        pieces = [jnp.zeros((GRID_W, b_lo * LANES), BF16), pw.astype(BF16),
                  jnp.zeros((GRID_W, (nblk - b_hi) * LANES), BF16)]
        p_rows.append(jnp.concatenate([x for x in pieces if x.shape[1]], axis=1))
        pc_rows.append(pc.astype(BF16))
    p = jnp.concatenate(p_rows, axis=0)
    acc = _dot(jnp.concatenate(pc_rows, axis=0), vc)
    nq = p.shape[0]
    for m, v in enumerate(vs):
        r0, r1 = row_ranges[m]
        part = _dot(p[r0:r1, m * kw:(m + 1) * kw], v)
        pieces = [jnp.zeros((r0, part.shape[1]), F32), part, jnp.zeros((nq - r1, part.shape[1]), F32)]
        acc = acc + jnp.concatenate([x for x in pieces if x.shape[0]], axis=0)
    return acc / jnp.concatenate(l_rows, axis=0)


def _na_kernel(q_ref, k0, k1, k2, k3, v0, v1, v2, v3, kc_ref, vc_ref, tab_ref, mask_ref, o_ref, *,
               groups, dvals):
    g = pl.program_id(1)
    is_edge = jnp.logical_or(g == 0, g == groups - 1)
    pad_blocks = NA_ROWS * GRID_W // LANES

    def step(interior):
        q2 = q_ref[...] * (NA_HD ** -0.5 * LOG2_E)
        lane = lax.broadcasted_iota(jnp.int32, q2.shape, 1)
        ks = [k[...].astype(BF16) for k in (k0, k1, k2, k3)]
        vs = [v[...].astype(BF16) for v in (v0, v1, v2, v3)]
        kc = kc_ref[...].astype(BF16)
        vc = vc_ref[...].astype(BF16)
        nheads = q2.shape[1] // NA_HD

        def scores(a):
            sel = jnp.logical_and(lane >= a * NA_HD, lane < (a + 1) * NA_HD)
            qa = jnp.where(sel, q2, 0.0).astype(BF16)
            if interior:
                ranges = _na_interior_rows(len(ks), ks[0].shape[0])
                return [_dot_nt(qa[r0:r1], k) for (r0, r1), k in zip(ranges, ks)], _dot_nt(qa, kc)
            return [_dot_nt(qa, k) for k in ks], _dot_nt(qa, kc)

        pending = [scores(a) for a in range(min(NA_AHEAD, nheads))]
        out = None
        for a in range(nheads):
            if a + NA_AHEAD < nheads:
                pending.append(scores(a + NA_AHEAD))
            s_raw, s_ctx = pending[a]
            if interior:
                o = _na_head_interior(s_raw, s_ctx, vs, vc, tab_ref, mask_ref, a, dvals[1], pad_blocks)
            else:
                dcls = jnp.where(g == 0, dvals[0], dvals[2])
                o = _na_head_edge(s_raw, s_ctx, vs, vc, tab_ref, mask_ref, a, dcls, dvals[0], pad_blocks)
            out = o if out is None else jnp.where(lane >= a * NA_HD, o, out)
        o_ref[...] = out.astype(o_ref.dtype)

    @pl.when(is_edge)
    def _():
        step(False)

    @pl.when(jnp.logical_not(is_edge))
    def _():
        step(True)


def _na_latent(u, bias, head_base, nbatch, seq, lat_rows, off_na):
    tab, mask, dvals = bias
    rows = seq // GRID_W
    groups = rows // NA_ROWS
    nq = NA_ROWS * GRID_W
    kblk = ROW_TILE
    nkb = NA_KEY_ROWS * GRID_W // kblk
    assert nkb == 4
    hp = NA_STEP_HEADS
    wid = hp * NA_HD
    qcol = off_na // wid
    hsteps = NA_HEADS // hp
    kcol, vcol = qcol + hsteps, qcol + 2 * hsteps
    assert off_na % wid == 0 and NA_HEADS % hp == 0 and head_base % hp == 0
    kb_per_batch = seq // kblk
    kb_per_grow = GRID_W * NA_ROWS // kblk
    lat_kb = lat_rows // kblk

    def kmap(col, m):
        def f(h, g, b):
            st = jnp.clip(g * kb_per_grow - (NA_KH // 2) * GRID_W // kblk, 0, kb_per_batch - nkb)
            return (b * kb_per_batch + st + m, col + h)
        return f

    def mask_map(h, g, b):
        return (jnp.where(g == 0, 0, jnp.where(g == groups - 1, 2, 1)), 0, 0)

    in_specs = [pl.BlockSpec((nq, wid), lambda h, g, b: (b * groups + g, qcol + h))]
    in_specs += [pl.BlockSpec((kblk, wid), kmap(kcol, m)) for m in range(nkb)]
    in_specs += [pl.BlockSpec((kblk, wid), kmap(vcol, m)) for m in range(nkb)]
    in_specs += [pl.BlockSpec((ROW_TILE, wid), lambda h, g, b: (lat_kb + b, kcol + h)),
                 pl.BlockSpec((ROW_TILE, wid), lambda h, g, b: (lat_kb + b, vcol + h)),
                 pl.BlockSpec((2, hp) + tab.shape[2:], lambda h, g, b: (0, head_base // hp + h, 0, 0, 0)),
                 pl.BlockSpec((None,) + mask.shape[1:], mask_map)]
    return pl.pallas_call(
        functools.partial(_na_kernel, groups=groups, dvals=dvals),
        grid=(hsteps, groups, nbatch),
        in_specs=in_specs,
        out_specs=pl.BlockSpec((nq, wid), lambda h, g, b: (b * groups + g, h)),
        out_shape=jax.ShapeDtypeStruct((lat_rows, NA_HEADS * NA_HD), BF16),
        compiler_params=_cparams(("arbitrary", "arbitrary", "arbitrary"), 48),
    )(*([u] * 11), tab, mask)


def _ctx_attn_kernel(q_ref, k_ref, v_ref, o_ref):
    q2 = q_ref[...] * (NA_HD ** -0.5 * LOG2_E)
    lane = lax.broadcasted_iota(jnp.int32, q2.shape, 1)
    k = k_ref[...].astype(BF16)
    v = v_ref[...].astype(BF16)
    outs = []
    for a in range(2):
        sel = (lane < NA_HD) if a == 0 else (lane >= NA_HD)
        qa = jnp.where(sel, q2, 0.0).astype(BF16)
        outs.append(_softmax_pv([_dot_nt(qa, k)], [v]))
    o_ref[...] = jnp.where(lane < NA_HD, outs[0], outs[1]).astype(o_ref.dtype)


def _ctx_attn(u, nbatch, ctx_len, lat_rows, off_na):
    assert ctx_len == ROW_TILE
    qcol = off_na // LANES
    heads2 = NA_HEADS * NA_HD // LANES
    base = lat_rows // ROW_TILE
    spec = lambda col: pl.BlockSpec((ROW_TILE, LANES), lambda b, h: (base + b, col + h))
    return pl.pallas_call(
        _ctx_attn_kernel,
        grid=(nbatch, heads2),
        in_specs=[spec(qcol), spec(qcol + heads2), spec(qcol + 2 * heads2)],
        out_specs=pl.BlockSpec((ROW_TILE, LANES), lambda b, h: (b, h)),
        out_shape=jax.ShapeDtypeStruct((nbatch * ctx_len, NA_HEADS * NA_HD), BF16),
        compiler_params=_cparams(("arbitrary", "arbitrary"), 16),
    )(u, u, u)


def _hg_level_map(rev):
    size = HG_BLOCK // 2
    t = np.arange(size)[:, None]
    s = np.arange(size)[None, :]
    x = t ^ s
    lvl = np.where(x > 0, np.frexp(np.maximum(x, 1))[1] - 1, -1)
    causal = (s < t) if not rev else (s > t)
    out = np.where(causal, lvl, -1)
    out = np.where(t == s, int(np.log2(size)), out)
    return out.astype(np.int32)


def _hg_tri(rev):
    t = np.arange(HG_BLOCK)[:, None]
    s = np.arange(HG_BLOCK)[None, :]
    return ((s <= t) if not rev else (s >= t)).astype(np.float32)


def _hg_anchor(b3, m, rev):
    nv = b3.shape[0]
    if m >= SUBLANES:
        w = m // SUBLANES
        b4 = b3.reshape(nv // (2 * w), 2 * w, SUBLANES, LANES)
        a = b4[:, w:w + 1, 0:1, :] if rev else b4[:, w - 1:w, SUBLANES - 1:SUBLANES, :]
        return jnp.broadcast_to(a, b4.shape).reshape(b3.shape)
    sub = lax.broadcasted_iota(jnp.int32, b3.shape, 1)
    out = None
    for g in range(SUBLANES // (2 * m)):
        idx = g * 2 * m + (m if rev else m - 1)
        a = jnp.broadcast_to(b3[:, idx:idx + 1, :], b3.shape)
        out = a if out is None else jnp.where(sub >= g * 2 * m, a, out)
    return out


def _neg_abs(x):
    bits = lax.bitcast_convert_type(x, jnp.uint32) | jnp.uint32(0x80000000)
    return lax.bitcast_convert_type(bits, F32)


def _hg_pick(q3, k3, m, rev):
    nv = q3.shape[0]
    if m >= SUBLANES:
        w = m // SUBLANES
        shape4 = (nv // (2 * w), 2 * w, SUBLANES, LANES)
        q4, k4 = q3.reshape(shape4), k3.reshape(shape4)
        lower, upper = (q4, k4) if rev else (k4, q4)
        return jnp.concatenate([lower[:, :w], upper[:, w:]], axis=1).reshape(q3.shape)
    upper_rows = (lax.broadcasted_iota(jnp.int32, q3.shape, 1) & m) != 0
    return jnp.where(upper_rows, k3 if rev else q3, q3 if rev else k3)


def _hg_gates(q, z, alog, clog, oml, tri):
    q = _silu(q)
    t = jnp.exp(-jnp.abs(z))
    lsig = jnp.minimum(z, 0.0) - jnp.log(1.0 + t)
    cc = clog + lsig
    logf = jnp.maximum(alog, cc) + jnp.log(1.0 + jnp.exp(-jnp.abs(alog - cc)))
    kk = oml * jnp.where(z >= 0.0, t, 1.0) / (1.0 + t)

    hi = logf.astype(BF16)
    r1 = logf - hi.astype(F32)
    mid = r1.astype(BF16)
    lo = (r1 - mid.astype(F32)).astype(BF16)
    b = (_dot(tri, hi) + _dot(tri, mid) + _dot(tri, lo)) * LOG2_E
    return q, kk, b


def _hg_mix(q, kk, b, v, lv, st, rev):
    n = q.shape[0]
    half = n // 2
    nlev = int(np.log2(n))
    shape3 = (n // SUBLANES, SUBLANES, LANES)
    b3, q3, k3 = b.reshape(shape3), q.reshape(shape3), kk.reshape(shape3)
    halves = (slice(0, half), slice(half, n))
    qb, kb = q.astype(BF16), kk.astype(BF16)
    acc = [jnp.where(lv == nlev - 1, _dot_nt(qb[hs], kb[hs]), 0.0) for hs in halves]
    for lev in range(nlev - 1):
        m = 1 << lev
        e = jnp.exp2(_neg_abs(b3 - _hg_anchor(b3, m, rev)))
        w = (_hg_pick(q3, k3, m, rev) * e).reshape(n, LANES).astype(BF16)
        acc = [jnp.where(lv == lev, _dot_nt(w[hs], w[hs]), a) for hs, a in zip(halves, acc)]
    first, second = (halves[1], halves[0]) if rev else halves
    e = jnp.exp2(_neg_abs(b - (b[half:half + 1, :] if rev else b[half - 1:half, :])))
    top = _dot_nt((q[second] * e[second]).astype(BF16), (kk[first] * e[first]).astype(BF16))

    vb = v.astype(BF16)
    a0, a1 = acc[0].astype(BF16), acc[1].astype(BF16)
    tb = top.astype(BF16)
    if rev:
        o_lo = _dot(jnp.concatenate([a0, tb], axis=1), vb)
        o_hi = _dot(a1, vb[halves[1]])
    else:
        o_lo = _dot(a0, vb[halves[0]])
        o_hi = _dot(jnp.concatenate([tb, a1], axis=1), vb)
    b_last = b[0:1, :] if rev else b[n - 1:n, :]
    qh = (q * jnp.exp2(b)).astype(BF16)
    o = jnp.concatenate([o_lo, o_hi], axis=0) + _dot_nt(qh, st.astype(BF16))
    kh = (kk * jnp.exp2(b_last - b)).astype(BF16)
    st_new = st * jnp.exp2(b_last) + _dot_tn(vb, kh)
    return o, st_new


def _hg_gates_ahead(q_ref, z_ref, al_ref, cl_ref, om_ref, tri):
    def gates(h):
        hs = slice(h * HG_DK, (h + 1) * HG_DK)
        return _hg_gates(q_ref[:, hs], z_ref[:, hs], al_ref[:, hs], cl_ref[:, hs], om_ref[:, hs], tri)

    pending = [gates(h) for h in range(min(HG_AHEAD, HG_HEADS))]
    for h in range(HG_HEADS):
        if h + HG_AHEAD < HG_HEADS:
            pending.append(gates(h + HG_AHEAD))
        yield pending[h]


def _hg_fwd_kernel(q_ref, v_ref, z_ref, al_ref, cl_ref, om_ref, tri_ref, lv_ref, o_ref, st_ref):
    @pl.when(pl.program_id(1) == 0)
    def _():
        st_ref[...] = jnp.zeros_like(st_ref)

    lv = lv_ref[...]
    gates = _hg_gates_ahead(q_ref, z_ref, al_ref, cl_ref, om_ref, tri_ref[...])
    for h in range(HG_HEADS):
        hs = slice(h * HG_DK, (h + 1) * HG_DK)
        o, st = _hg_mix(*next(gates), v_ref[:, hs], lv, st_ref[h], False)
        o_ref[:, hs] = o
        st_ref[h] = st


def _hg_bwd_kernel(q_ref, v_ref, z_ref, g_ref, of_ref, al_ref, cl_ref, om_ref, ng_ref, tri_ref, lv_ref,
                   o_ref, st_ref):
    @pl.when(pl.program_id(1) == 0)
    def _():
        st_ref[...] = jnp.zeros_like(st_ref)

    lv = lv_ref[...]
    gates = _hg_gates_ahead(q_ref, z_ref, al_ref, cl_ref, om_ref, tri_ref[...])
    for h in range(HG_HEADS):
        hs = slice(h * HG_DK, (h + 1) * HG_DK)
        o, st = _hg_mix(*next(gates), v_ref[:, hs], lv, st_ref[h], True)
        st_ref[h] = st
        t = of_ref[:, hs] + o
        y = t * lax.rsqrt(jnp.mean(t * t, axis=-1, keepdims=True) + EPS)
        o_ref[:, hs] = (y * ng_ref[:, hs] * _silu(g_ref[:, hs])).astype(o_ref.dtype)


def _hgrn(u, lb, norm_g, nbatch, seq, ctx_len, lat_rows, off_hg):
    assert ctx_len == HG_BLOCK
    n = u.shape[0]
    hd = HG_HEADS * HG_DK
    col = off_hg // hd
    per = seq // HG_BLOCK
    lat_blocks = lat_rows // HG_BLOCK
    lbf = lb.astype(F32)
    alog, clog, oml = jnp.log(lbf), jnp.log1p(-lbf), 1.0 - lbf

    def fmap(c):
        return lambda b, j: (jnp.where(j == 0, lat_blocks + b, b * per + j - 1), c)

    def bmap(c):
        return lambda b, j: (jnp.where(j == 0, lat_blocks + b, b * per + per - j), c)

    const = lambda shape: pl.BlockSpec(shape, lambda b, j: (0, 0))
    grid = (nbatch, per + 1)
    vec = lambda a: a.reshape(1, hd)
    o_f = pl.pallas_call(
        _hg_fwd_kernel,
        grid=grid,
        in_specs=[pl.BlockSpec((HG_BLOCK, hd), fmap(col)), pl.BlockSpec((HG_BLOCK, hd), fmap(col + 1)),
                  pl.BlockSpec((HG_BLOCK, hd), fmap(col + 2)),
                  const((1, hd)), const((1, hd)), const((1, hd)),
                  const((HG_BLOCK, HG_BLOCK)), const((HG_BLOCK // 2, HG_BLOCK // 2))],
        out_specs=pl.BlockSpec((HG_BLOCK, hd), fmap(0)),
        out_shape=jax.ShapeDtypeStruct((n, hd), F32),
        scratch_shapes=[pltpu.VMEM((HG_HEADS, HG_DK, HG_DK), F32)],
        compiler_params=_cparams(("arbitrary", "arbitrary"), 32),
    )(u, u, u, vec(alog[0]), vec(clog[0]), vec(oml[0]),
      jnp.asarray(_hg_tri(False), BF16), jnp.asarray(_hg_level_map(False)))
    return pl.pallas_call(
        _hg_bwd_kernel,
        grid=grid,
        in_specs=[pl.BlockSpec((HG_BLOCK, hd), bmap(col)), pl.BlockSpec((HG_BLOCK, hd), bmap(col + 1)),
                  pl.BlockSpec((HG_BLOCK, hd), bmap(col + 3)), pl.BlockSpec((HG_BLOCK, hd), bmap(col + 4)),
                  pl.BlockSpec((HG_BLOCK, hd), bmap(0)),
                  const((1, hd)), const((1, hd)), const((1, hd)), const((1, hd)),
                  const((HG_BLOCK, HG_BLOCK)), const((HG_BLOCK // 2, HG_BLOCK // 2))],
        out_specs=pl.BlockSpec((HG_BLOCK, hd), bmap(0)),
        out_shape=jax.ShapeDtypeStruct((n, hd), BF16),
        scratch_shapes=[pltpu.VMEM((HG_HEADS, HG_DK, HG_DK), F32)],
        compiler_params=_cparams(("arbitrary", "arbitrary"), 32),
    )(u, u, u, u, o_f, vec(alog[1]), vec(clog[1]), vec(oml[1]), vec(norm_g.astype(F32)),
      jnp.asarray(_hg_tri(True), BF16), jnp.asarray(_hg_level_map(True)))


def _out_kernel(x_ref, cv_ref, na_ref, hg_ref, w_ref, ga_ref, g2_ref, sh2_ref, s2_ref, wrh_ref, wrl_ref, br_ref,
                xo_ref, h_ref, rt_ref, *, tiles_per_batch, nbatch):
    r = jnp.minimum(pl.program_id(0) // tiles_per_batch, nbatch)
    c0 = cv_ref.shape[1]
    c1 = c0 + na_ref.shape[1]
    mix = (_dot(cv_ref[...], w_ref[0:c0, :]) + _dot(na_ref[...], w_ref[c0:c1, :])
           + _dot(hg_ref[...], w_ref[c1:, :]))
    xn = x_ref[...] + ga_ref[pl.ds(r, 1), :] * mix
    xo_ref[...] = xn
    h = _rms_mod(xn, g2_ref[...], s2_ref[pl.ds(r, 1), :], sh2_ref[pl.ds(r, 1), :])
    h_ref[...] = h.astype(h_ref.dtype)
    h_hi, h_lo = _split_bf16(h)
    logits = (_dot(h_hi, wrh_ref[...]) + (_dot(h_lo, wrh_ref[...]) + _dot(h_hi, wrl_ref[...]))
              + br_ref[...])
    rt_ref[...] = _route_rows(logits)


def _route_rows(lg):
    lane = lax.broadcasted_iota(jnp.int32, lg.shape, 1)
    big = jnp.int32(2 ** 30)
    low = jnp.float32(-3e38)

    def first_max(vals, mask):
        m = jnp.max(vals, axis=-1, keepdims=True)
        idx = jnp.min(jnp.where(jnp.logical_and(vals == m, mask), lane, big), axis=-1, keepdims=True)
        return m, idx

    gmask = lane < N_GROUPS
    gl = jnp.where(gmask, lg, low)
    gm, grp = first_max(gl, gmask)
    p_grp = 1.0 / jnp.sum(jnp.where(gmask, jnp.exp(gl - gm), 0.0), axis=-1, keepdims=True)
    lo = N_GROUPS + grp * EXP_PER_GROUP
    emask = jnp.logical_and(lane >= lo, lane < lo + EXP_PER_GROUP)
    el = jnp.where(emask, lg, low)
    m1, i1 = first_max(el, emask)
    emask2 = jnp.logical_and(emask, lane != i1)
    el2 = jnp.where(emask2, lg, low)
    m2, i2 = first_max(el2, emask2)
    t = jnp.exp(m2 - m1)
    w1 = p_grp / (1.0 + t)
    w2 = p_grp * t / (1.0 + t)
    e1 = (i1 - N_GROUPS).astype(F32)
    e2 = (i2 - N_GROUPS).astype(F32)
    return jnp.where(lane == 0, e1, jnp.where(lane == 1, e2, jnp.where(lane == 2, w1,
                     jnp.where(lane == 3, w2, 0.0))))


def _out_proj(x, conv, na, hg, w_bf16, l, mod, g_ffn, w_router, b_router, n_rows, nbatch, lat_rows):
    d = x.shape[1]
    w_router_hi, w_router_lo = _split_bf16(w_router)
    tm = _pick_tile(OUT_TM, lat_rows // nbatch, n_rows)
    kern = functools.partial(_out_kernel, tiles_per_batch=lat_rows // nbatch // tm, nbatch=nbatch)
    row = lambda w: pl.BlockSpec((tm, w), lambda i: (i, 0))
    const = lambda shape: pl.BlockSpec(shape, lambda i: (0, 0))
    modc = lambda c: pl.BlockSpec((None, SUBLANES, d), lambda i: (l, 0, c))
    return pl.pallas_call(
        kern,
        grid=(n_rows // tm,),
        in_specs=[row(d), row(conv.shape[1]), row(na.shape[1]), row(hg.shape[1]),
                  pl.BlockSpec((None, d, d), lambda i: (l, 0, 0)),
                  modc(2),
                  const((1, d)),
                  modc(3),
                  modc(4),
                  const((d, ROUTER_PAD)), const((d, ROUTER_PAD)), const((1, ROUTER_PAD))],
        out_specs=[row(d), row(d), row(ROUTER_PAD)],
        out_shape=[jax.ShapeDtypeStruct((n_rows, d), F32), jax.ShapeDtypeStruct((n_rows, d), BF16),
                   jax.ShapeDtypeStruct((n_rows, ROUTER_PAD), F32)],
        compiler_params=_cparams(("arbitrary",), 56),
    )(x, conv, na, hg, w_bf16, mod, g_ffn.reshape(1, d), mod, mod, w_router_hi, w_router_lo, b_router)


def _moe_kernel(be_ref, nu_ref, xs_ref, w1_ref, w3_ref, w2_ref, sw_ref, o_ref, w1b, w3b, w2b):
    i = pl.program_id(0)
    e = be_ref[i]
    prev = be_ref[jnp.maximum(i - 1, 0)]

    @pl.when(jnp.logical_or(i == 0, e != prev))
    def _():
        w1b[...] = w1_ref[...].astype(BF16)
        w3b[...] = w3_ref[...].astype(BF16)
        w2b[...] = w2_ref[...].astype(BF16)

    @pl.when(i < nu_ref[0])
    def _():
        x = xs_ref[...]
        a = (_silu(_dot(x, w1b[...])) * _dot(x, w3b[...])).astype(BF16)
        o_ref[...] = (_dot(a, w2b[...]) * sw_ref[...]).astype(o_ref.dtype)

    @pl.when(i >= nu_ref[0])
    def _():
        o_ref[...] = jnp.zeros_like(o_ref)


def _moe_experts(xs, slot_w, blk_e, nused, w1, w3, w2, l):
    p, d = xs.shape
    de = w1.shape[3]
    bm = MOE_BM
    grid_spec = pltpu.PrefetchScalarGridSpec(
        num_scalar_prefetch=2,
        grid=(p // bm,),
        in_specs=[pl.BlockSpec((bm, d), lambda i, be, nu: (i, 0)),
                  pl.BlockSpec((None, None, d, de), lambda i, be, nu: (l, be[i], 0, 0)),
                  pl.BlockSpec((None, None, d, de), lambda i, be, nu: (l, be[i], 0, 0)),
                  pl.BlockSpec((None, None, de, d), lambda i, be, nu: (l, be[i], 0, 0)),
                  pl.BlockSpec((bm, 1), lambda i, be, nu: (i, 0))],
        out_specs=pl.BlockSpec((bm, d), lambda i, be, nu: (i, 0)),
        scratch_shapes=[pltpu.VMEM((d, de), BF16), pltpu.VMEM((d, de), BF16), pltpu.VMEM((de, d), BF16)],
    )
    return pl.pallas_call(
        _moe_kernel,
        grid_spec=grid_spec,
        out_shape=jax.ShapeDtypeStruct((p, d), BF16),
        compiler_params=_cparams(("arbitrary",), 48),
    )(blk_e, nused, xs, w1, w3, w2, slot_w.reshape(p, 1))


def _rows(a, idx):
    return a.at[idx].get(mode="promise_in_bounds")


def _route_meta(route, n):
    i32 = jnp.int32
    eid = route[:, 0:TOP_K].astype(i32).reshape(-1)
    wt = route[:, TOP_K:2 * TOP_K].reshape(-1)
    a = n * TOP_K
    bm = MOE_BM
    nblk = -(-a // bm) + N_EXPERTS
    p = nblk * bm
    experts = jnp.arange(N_EXPERTS, dtype=i32)[None, :]
    ja = jnp.arange(a, dtype=i32)
    se, order, wsort = lax.sort((eid, ja, wt), num_keys=1, is_stable=True)
    cnt = jnp.sum((eid[:, None] == experts).astype(i32), axis=0)
    pcnt = (cnt + bm - 1) // bm * bm
    pend = jnp.cumsum(pcnt)
    pstart = pend - pcnt
    end = jnp.cumsum(cnt)
    start = end - cnt
    off = pstart - start
    d_off = off - jnp.concatenate([jnp.zeros((1,), i32), off[:-1]])
    dst_sorted = ja + jnp.sum(jnp.where(ja[:, None] >= start[None, :], d_off[None, :], 0), axis=1)
    _, pos = lax.sort((order, dst_sorted), num_keys=1)
    jp = jnp.arange(p, dtype=i32)
    in_or_after = jp[:, None] >= pstart[None, :]
    src = jp - jnp.sum(jnp.where(in_or_after, d_off[None, :], 0), axis=1)
    valid = src < jnp.sum(jnp.where(in_or_after, cnt[None, :], 0), axis=1)
    src = jnp.where(valid, src, jp % a)
    slot_tok = _rows(order, src) // TOP_K
    slot_w = jnp.where(valid, _rows(wsort, src), 0.0)
    jb = jnp.arange(nblk, dtype=i32) * bm
    blk_e = jnp.minimum(jnp.sum((jb[:, None] >= pend[None, :]).astype(i32), axis=1), N_EXPERTS - 1)
    nused = (pend[-1:] // bm).astype(i32)
    return slot_tok, slot_w, pos.reshape(n, TOP_K), blk_e, nused


def _combine_kernel(x_ref, y0_ref, y1_ref, ga_ref, gf_ref, o_ref, *, tiles_per_batch, nbatch, final):
    r = jnp.minimum(pl.program_id(0) // tiles_per_batch, nbatch)
    xn = x_ref[...] + ga_ref[pl.ds(r, 1), :] * (y0_ref[...].astype(F32) + y1_ref[...].astype(F32))
    if final:
        xn = xn * lax.rsqrt(jnp.mean(xn * xn, axis=-1, keepdims=True) + EPS) * gf_ref[...]
    o_ref[...] = xn


def _combine(x, y0, y1, mod, l, g_final, n_rows, nbatch, lat_rows, final):
    d = x.shape[1]
    tm = _pick_tile(OUT_TM, lat_rows // nbatch, n_rows)
    kern = functools.partial(_combine_kernel, tiles_per_batch=lat_rows // nbatch // tm, nbatch=nbatch,
                             final=final)
    row = pl.BlockSpec((tm, d), lambda i: (i, 0))
    return pl.pallas_call(
        kern,
        grid=(n_rows // tm,),
        in_specs=[row, row, row, pl.BlockSpec((None, SUBLANES, d), lambda i: (l, 0, 5)),
                  pl.BlockSpec((1, d), lambda i: (0, 0))],
        out_specs=row,
        out_shape=jax.ShapeDtypeStruct((n_rows, d), F32),
        compiler_params=_cparams(("arbitrary",), 40),
    )(x, y0, y1, mod, g_final.reshape(1, d))


def kernel(x, c, ctx, c_ctx, w_ada, b_ada, g_mix, g_ffn, w_in, conv_w, conv_b, conv_ln_g, conv_ln_b,
           na_rpb, hgrn_lb, hgrn_norm_g, w_out, w_router_group, b_router_group, w_router_expert,
           b_router_expert, w_exp_gate, w_exp_up, w_exp_down, g_final):
    nb, seq, d = x.shape
    ctx_len = ctx.shape[1]
    depth = w_ada.shape[0]
    lat_rows = nb * seq
    n_all = lat_rows + nb * ctx_len
    conv_ch = conv_w.shape[2]
    off_na = 2 * conv_ch
    off_hg = off_na + 3 * NA_HEADS * NA_HD
    rows = seq // GRID_W
    assert nb < SUBLANES and rows % NA_ROWS == 0 and rows >= NA_KEY_ROWS

    lbs = jnp.cumsum(jax.nn.softmax(hgrn_lb.astype(F32), axis=0), axis=0)
    lbs = lbs - lbs[:1]

    cond = jnp.concatenate([c, c_ctx[None, :], jnp.zeros((SUBLANES - nb - 1, d), F32)], axis=0)
    mod = _ada_mod(cond, w_ada, b_ada)

    xs = jnp.concatenate([x.reshape(lat_rows, d), ctx.reshape(nb * ctx_len, d)], axis=0)
    w_in_b = w_in.astype(BF16)
    w_out_b = w_out.astype(BF16)
    na_bias = _na_bias(na_rpb.reshape((depth * NA_HEADS,) + na_rpb.shape[2:]), rows)
    for l in range(depth):
        with_ctx = l < depth - 1
        n_act = n_all if with_ctx else lat_rows
        u = _norm_in(xs, g_mix[l], mod, w_in_b, l, nb, lat_rows)

        conv = _conv_module(u, conv_w[l], conv_b[l], conv_ln_g[l], conv_ln_b[l], n_act, lat_rows, seq)
        na = _na_latent(u, na_bias, l * NA_HEADS, nb, seq, lat_rows, off_na)
        if with_ctx:
            na = jnp.concatenate([na, _ctx_attn(u, nb, ctx_len, lat_rows, off_na)], axis=0)
        hg = _hgrn(u, lbs[l], hgrn_norm_g[l], nb, seq, ctx_len, lat_rows, off_hg)

        w_router = jnp.concatenate(
            [w_router_group[l], w_router_expert[l],
             jnp.zeros((d, ROUTER_PAD - N_GROUPS - N_EXPERTS), F32)], axis=1)
        b_router = jnp.concatenate(
            [b_router_group[l], b_router_expert[l],
             jnp.zeros((ROUTER_PAD - N_GROUPS - N_EXPERTS,), F32)]).reshape(1, ROUTER_PAD)
        x_mid, h, route = _out_proj(xs, conv, na, hg, w_out_b, l, mod, g_ffn[l],
                                    w_router, b_router, n_act, nb, lat_rows)

        slot_tok, slot_w, pos, blk_e, nused = _route_meta(route, n_act)
        ys = _moe_experts(_rows(h, slot_tok), slot_w, blk_e, nused, w_exp_gate, w_exp_up, w_exp_down, l)
        y0 = _rows(ys, pos[:, 0])
        y1 = _rows(ys, pos[:, 1])
        xs = _combine(x_mid, y0, y1, mod, l, g_final, n_act, nb, lat_rows, final=not with_ctx)
    return xs.reshape(nb, seq, d)
```
